```python
import math
import jax, jax.numpy as jnp
from jax import lax
import numpy as np

D_MODEL = 1024
BATCH = 8
SEQ = 4096
DEPTH = 1

SSM_WIDTH = 512
SSM_GROUP = 16
SSM_GROUPS = SSM_WIDTH // SSM_GROUP
SSM_STATE = 64
DT_MIN = 1e-3
DT_MAX = 1e-1
N_HEADS = 8
HEAD_DIM = 64
ATTN_WIDTH = N_HEADS * HEAD_DIM
IDX_HEADS = 8
IDX_DIM = 64
TOPK_MAX = 256
Q_BLOCK = 128
N_EXPERTS = 64
TOP_K = 8
N_GROUPS = 8
TOPK_GROUPS = 4
EXPERT_DIM = 256
SHARED_DIM = 256
ROUTED_SCALE = 2.5
DISPATCH_BLOCK = 128
EPS = 1e-6

IN_SPLIT_SIZES = (SSM_WIDTH, ATTN_WIDTH, HEAD_DIM, HEAD_DIM, IDX_HEADS * IDX_DIM, IDX_DIM, IDX_HEADS, D_MODEL, D_MODEL)
IN_WIDTH = SSM_WIDTH + ATTN_WIDTH + 2 * HEAD_DIM + IDX_HEADS * IDX_DIM + IDX_DIM + IDX_HEADS + 2 * D_MODEL

kernel_name = "hybrid_s5_dsa_moe_adaln_block"


def rms_norm(x, g):
    xf = x.astype(jnp.float32)
    y = xf * lax.rsqrt(jnp.mean(xf * xf, axis=-1, keepdims=True) + EPS)
    return (y * g.astype(jnp.float32)).astype(x.dtype)


def modulate(x, g, shift, scale):
    return rms_norm(x, g) * (1.0 + scale) + shift


def alibi_slopes(n):
    return 2.0 ** (-8.0 * (jnp.arange(n, dtype=jnp.float32) + 1.0) / n)


def _complex_linear_combine(e1, e2):
    a1r, a1i, b1r, b1i = e1
    a2r, a2i, b2r, b2i = e2
    return (a1r * a2r - a1i * a2i,
            a1r * a2i + a1i * a2r,
            a2r * b1r - a2i * b1i + b2r,
            a2r * b1i + a2i * b1r + b2i)


def s5_mixer(u, lam_re, lam_im, log_dt, b_re, b_im, c_re, c_im, d_skip, w_glu, b_glu):
    bsz, seq, _ = u.shape
    f32 = jnp.float32
    uf = u.astype(f32)
    ug = uf.reshape(bsz, seq, SSM_GROUPS, SSM_GROUP)
    dt = jnp.exp(log_dt.astype(f32))[:, None]
    lr = lam_re.astype(f32)
    li = lam_im.astype(f32)
    mag = jnp.exp(lr * dt)
    a_re = mag * jnp.cos(li * dt)
    a_im = mag * jnp.sin(li * dt)
    den = lr * lr + li * li
    n_re = a_re - 1.0
    f_re = (n_re * lr + a_im * li) / den
    f_im = (a_im * lr - n_re * li) / den
    br = b_re.astype(f32)
    bi = b_im.astype(f32)
    bb_re = f_re[..., None] * br - f_im[..., None] * bi
    bb_im = f_re[..., None] * bi + f_im[..., None] * br
    bu_re = jnp.einsum('blgh,gph->lbgp', ug, bb_re)
    bu_im = jnp.einsum('blgh,gph->lbgp', ug, bb_im)
    shape_a = (seq, 1, SSM_GROUPS, SSM_STATE)
    a_re_t = jnp.broadcast_to(a_re, shape_a)
    a_im_t = jnp.broadcast_to(a_im, shape_a)
    _, _, h_re, h_im = lax.associative_scan(_complex_linear_combine, (a_re_t, a_im_t, bu_re, bu_im), axis=0)
    y = (jnp.einsum('ghp,lbgp->blgh', c_re.astype(f32), h_re)
         - jnp.einsum('ghp,lbgp->blgh', c_im.astype(f32), h_im))
    y = y.reshape(bsz, seq, SSM_WIDTH) + d_skip.astype(f32) * uf
    y = jax.nn.gelu(y)
    y = y * jax.nn.sigmoid(y @ w_glu.astype(f32) + b_glu.astype(f32))
    return y.astype(u.dtype)


def dsa_attention(q, k, v, q_idx, k_idx, w_idx):
    bsz, seq = q.shape[0], q.shape[1]
    f32 = jnp.float32
    topk = min(TOPK_MAX, seq // 4)
    n_blocks = seq // Q_BLOCK
    slopes = alibi_slopes(N_HEADS)
    key_pos = jnp.arange(seq)
    kf = k_idx.astype(f32)

    def to_blocks(a):
        return a.reshape((bsz, n_blocks, Q_BLOCK) + a.shape[2:]).swapaxes(0, 1)

    def block(args):
        qb, qib, wb, blk = args
        q_pos = blk * Q_BLOCK + jnp.arange(Q_BLOCK)
        s_idx = jnp.einsum('bqhd,bsd->bqhs', qib.astype(f32), kf) * (IDX_DIM ** -0.5)
        score = jnp.einsum('bqh,bqhs->bqs', wb.astype(f32) * (IDX_HEADS ** -0.5), jax.nn.relu(s_idx))
        causal = key_pos[None, :] <= q_pos[:, None]
        score = jnp.where(causal[None], score, -jnp.inf)
        _, sel = lax.top_k(score, topk)
        k_sel = jax.vmap(lambda kk, ii: kk[ii])(k, sel)
        v_sel = jax.vmap(lambda vv, ii: vv[ii])(v, sel)
        logits = jnp.einsum('bqhd,bqkd->bhqk', qb, k_sel).astype(f32) * (HEAD_DIM ** -0.5)
        dist = (q_pos[None, :, None] - sel).astype(f32)
        logits = logits - slopes[None, :, None, None] * dist[:, None]
        valid = sel <= q_pos[None, :, None]
        logits = jnp.where(valid[:, None], logits, -jnp.inf)
        p = jax.nn.softmax(logits, axis=-1).astype(v.dtype)
        return jnp.einsum('bhqk,bqkd->bqhd', p, v_sel)

    out = lax.map(block, (to_blocks(q), to_blocks(q_idx), to_blocks(w_idx), jnp.arange(n_blocks)))
    return out.swapaxes(0, 1).reshape(bsz, seq, ATTN_WIDTH)


def moe_ffn(u, w_router, router_bias, w_gate, w_up, w_down, ws_gate, ws_up, ws_down):
    bsz, seq, d = u.shape
    f32 = jnp.float32
    xt = u.reshape(-1, d)
    n_tok = xt.shape[0]
    scores = jax.nn.sigmoid(xt.astype(f32) @ w_router.astype(f32))
    biased = scores + router_bias.astype(f32)
    per_group = N_EXPERTS // N_GROUPS
    group_score = lax.top_k(biased.reshape(n_tok, N_GROUPS, per_group), 2)[0].sum(-1)
    _, gsel = lax.top_k(group_score, TOPK_GROUPS)
    gmask = jnp.any(gsel[:, :, None] == jnp.arange(N_GROUPS)[None, None, :], axis=1)
    emask = jnp.repeat(gmask, per_group, axis=1)
    _, eidx = lax.top_k(jnp.where(emask, biased, -jnp.inf), TOP_K)
    gate = jnp.take_along_axis(scores, eidx, axis=1)
    gate = gate / jnp.sum(gate, axis=-1, keepdims=True) * ROUTED_SCALE

    n_assign = n_tok * TOP_K
    e_flat = eidx.reshape(-1)
    t_flat = jnp.repeat(jnp.arange(n_tok, dtype=jnp.int32), TOP_K)
    g_flat = gate.reshape(-1)
    order = jnp.argsort(e_flat)
    e_s = e_flat[order]
    t_s = t_flat[order]
    g_s = g_flat[order]
    counts = jnp.bincount(e_flat, length=N_EXPERTS)
    padded = (counts + DISPATCH_BLOCK - 1) // DISPATCH_BLOCK * DISPATCH_BLOCK
    start = jnp.cumsum(counts) - counts
    pend = jnp.cumsum(padded)
    pstart = pend - padded
    dest = pstart[e_s] + jnp.arange(n_assign) - start[e_s]
    n_blocks = (n_assign + N_EXPERTS * (DISPATCH_BLOCK - 1) + DISPATCH_BLOCK - 1) // DISPATCH_BLOCK
    total = n_blocks * DISPATCH_BLOCK
    tok_pad = jnp.zeros((total,), jnp.int32).at[dest].set(t_s)
    gate_pad = jnp.zeros((total,), f32).at[dest].set(g_s)
    blk_expert = jnp.minimum(jnp.searchsorted(pend, jnp.arange(n_blocks) * DISPATCH_BLOCK, side='right'), N_EXPERTS - 1)

    def step(acc, blk):
        tok, g, e = blk
        xb = xt[tok]
        hb = jax.nn.silu(xb @ w_gate[e]) * (xb @ w_up[e])
        yb = (hb @ w_down[e]).astype(f32) * g[:, None]
        return acc.at[tok].add(yb), None

    routed, _ = lax.scan(step, jnp.zeros((n_tok, d), f32),
                         (tok_pad.reshape(n_blocks, DISPATCH_BLOCK), gate_pad.reshape(n_blocks, DISPATCH_BLOCK), blk_expert))
    shared = (jax.nn.silu(xt @ ws_gate) * (xt @ ws_up)) @ ws_down
    return (routed + shared.astype(f32)).astype(u.dtype).reshape(bsz, seq, d)


def setup_inputs(seed: int = 0) -> dict:
    key = jax.random.key(seed)
    ks = iter(jax.random.split(key, 40))
    f32 = jnp.float32

    def nrm(shape, s):
        return jax.random.normal(next(ks), shape, f32) * s

    L = DEPTH
    d = D_MODEL
    lam_im_base = math.pi * jnp.arange(SSM_STATE, dtype=f32)
    return {
        "x": nrm((BATCH, SEQ, d), 1.0),
        "c": nrm((BATCH, d), 1.0),
        "w_ada": nrm((L, d, 6 * d), 0.5 * d ** -0.5),
        "b_ada": nrm((L, 6 * d), 0.02),
        "norm1_g": 1.0 + nrm((L, d), 0.02),
        "w_in": nrm((L, d, IN_WIDTH), d ** -0.5),
        "ssm_lambda_re": -0.5 + nrm((L, SSM_GROUPS, SSM_STATE), 0.01),
        "ssm_lambda_im": lam_im_base + nrm((L, SSM_GROUPS, SSM_STATE), 0.01),
        "ssm_log_dt": jax.random.uniform(next(ks), (L, SSM_GROUPS), f32, math.log(DT_MIN), math.log(DT_MAX)),
        "ssm_b_re": nrm((L, SSM_GROUPS, SSM_STATE, SSM_GROUP), (2.0 * SSM_GROUP) ** -0.5),
        "ssm_b_im": nrm((L, SSM_GROUPS, SSM_STATE, SSM_GROUP), (2.0 * SSM_GROUP) ** -0.5),
        "ssm_c_re": nrm((L, SSM_GROUPS, SSM_GROUP, SSM_STATE), (2.0 * SSM_STATE) ** -0.5),
        "ssm_c_im": nrm((L, SSM_GROUPS, SSM_GROUP, SSM_STATE), (2.0 * SSM_STATE) ** -0.5),
        "ssm_d": nrm((L, SSM_WIDTH), 1.0),
        "ssm_w_glu": nrm((L, SSM_WIDTH, SSM_WIDTH), SSM_WIDTH ** -0.5),
        "ssm_b_glu": nrm((L, SSM_WIDTH), 0.02),
        "w_proj_ssm": nrm((L, SSM_WIDTH, d), SSM_WIDTH ** -0.5),
        "w_proj_attn": nrm((L, ATTN_WIDTH, d), ATTN_WIDTH ** -0.5),
        "w_out": nrm((L, d, d), d ** -0.5),
        "norm2_g": 1.0 + nrm((L, d), 0.02),
        "w_router": nrm((L, d, N_EXPERTS), d ** -0.5),
        "router_bias": nrm((L, N_EXPERTS), 0.01),
        "w_exp_gate": nrm((L, N_EXPERTS, d, EXPERT_DIM), d ** -0.5),
        "w_exp_up": nrm((L, N_EXPERTS, d, EXPERT_DIM), d ** -0.5),
        "w_exp_down": nrm((L, N_EXPERTS, EXPERT_DIM, d), EXPERT_DIM ** -0.5),
        "w_sh_gate": nrm((L, d, SHARED_DIM), d ** -0.5),
        "w_sh_up": nrm((L, d, SHARED_DIM), d ** -0.5),
        "w_sh_down": nrm((L, SHARED_DIM, d), SHARED_DIM ** -0.5),
        "norm_f_g": 1.0 + nrm((d,), 0.02),
    }


def reference(x, c, w_ada, b_ada, norm1_g, w_in, ssm_lambda_re, ssm_lambda_im, ssm_log_dt,
              ssm_b_re, ssm_b_im, ssm_c_re, ssm_c_im, ssm_d, ssm_w_glu, ssm_b_glu,
              w_proj_ssm, w_proj_attn, w_out, norm2_g, w_router, router_bias,
              w_exp_gate, w_exp_up, w_exp_down, w_sh_gate, w_sh_up, w_sh_down, norm_f_g):
    bsz, seq, _ = x.shape
    offsets = [int(o) for o in np.cumsum(IN_SPLIT_SIZES)[:-1]]
    h = x
    for layer in range(DEPTH):
        ada = jax.nn.silu(c) @ w_ada[layer] + b_ada[layer]
        shift1, scale1, gate1, shift2, scale2, gate2 = jnp.split(ada[:, None, :], 6, axis=-1)

        u = modulate(h, norm1_g[layer], shift1, scale1)
        proj = u @ w_in[layer]
        u_ssm, q, k, v, q_idx, k_idx, w_idx, g_ssm, g_attn = jnp.split(proj, offsets, axis=-1)
        y_ssm = s5_mixer(u_ssm, ssm_lambda_re[layer], ssm_lambda_im[layer], ssm_log_dt[layer],
                         ssm_b_re[layer], ssm_b_im[layer], ssm_c_re[layer], ssm_c_im[layer],
                         ssm_d[layer], ssm_w_glu[layer], ssm_b_glu[layer])
        y_attn = dsa_attention(q.reshape(bsz, seq, N_HEADS, HEAD_DIM), k, v,
                               q_idx.reshape(bsz, seq, IDX_HEADS, IDX_DIM), k_idx, w_idx)
        mixed = (jax.nn.sigmoid(g_ssm) * (y_ssm @ w_proj_ssm[layer])
                 + jax.nn.sigmoid(g_attn) * (y_attn @ w_proj_attn[layer]))
        h = h + gate1 * (mixed @ w_out[layer])

        u2 = modulate(h, norm2_g[layer], shift2, scale2)
        h = h + gate2 * moe_ffn(u2, w_router[layer], router_bias[layer], w_exp_gate[layer], w_exp_up[layer],
                                w_exp_down[layer], w_sh_gate[layer], w_sh_up[layer], w_sh_down[layer])
    return rms_norm(h, norm_f_g)
```

```python
import functools
import math

import jax
import jax.numpy as jnp
from jax import lax
from jax.experimental import pallas as pl
from jax.experimental.pallas import tpu as pltpu

F32 = jnp.float32
BF16 = jnp.bfloat16
I32 = jnp.int32

SSM_GROUP = 16
SSM_STATE = 64
N_HEADS = 8
HEAD_DIM = 64
IDX_HEADS = 8
IDX_DIM = 64
TOPK_MAX = 256
N_EXPERTS = 64
TOP_K = 8
N_GROUPS = 8
TOPK_GROUPS = 4
ROUTED_SCALE = 2.5
DT_EPS = 1e-6

V7X_VMEM_LIMIT_BYTES = 56 * 1024 * 1024
LANES = 128
SUBLANES = 8

INPROJ_ROWS = 256
S5_STEPS = 64
S5_LANE_CHUNK = 128
DSA_Q_ROWS = 256
DSA_K_COLS = 512
DSA_SEL_ROWS = 64
MIX_ROWS = 256
ROUTE_ROWS = 512
DEST_COLS = 4096
DISPATCH_ROWS = 512
DISPATCH_BATCH = 32
EXPERT_ROWS = 256
COMBINE_ROWS = 256

NEG_BIG = -1e30
INT_MIN = -(2 ** 31)


def _cparams(sem):
    return pltpu.CompilerParams(dimension_semantics=sem, vmem_limit_bytes=V7X_VMEM_LIMIT_BYTES)


def _bf(x):
    return x.astype(BF16)


def _dot(a, b):
    return jnp.dot(a, b, preferred_element_type=F32)


def _dot_nt(a, b):
    return lax.dot_general(a, b, (((1,), (1,)), ((), ())), preferred_element_type=F32)


def _split(x):
    hi = _bf(x)
    lo = _bf(x - hi.astype(F32))
    return hi, lo


def _dot3(a, b):
    ah, al = _split(a)
    bh, bl = _split(b)
    return _dot(ah, bh) + (_dot(ah, bl) + _dot(al, bh))


def _rms(x, g):
    return x * lax.rsqrt(jnp.mean(x * x, axis=-1, keepdims=True) + DT_EPS) * g


def _ada_kernel(c_ref, w_ref, b_ref, o_ref):
    c = c_ref[...]
    o_ref[...] = _dot3(c * jax.nn.sigmoid(c), w_ref[...]) + b_ref[...]


def _ada(c, w, b):
    bsz, d = c.shape
    n = w.shape[1]
    tn = 1024
    return pl.pallas_call(
        _ada_kernel,
        out_shape=jax.ShapeDtypeStruct((bsz, n), F32),
        grid=(n // tn,),
        in_specs=[pl.BlockSpec((bsz, d), lambda j: (0, 0)),
                  pl.BlockSpec((d, tn), lambda j: (0, j)),
                  pl.BlockSpec((1, tn), lambda j: (0, j))],
        out_specs=pl.BlockSpec((bsz, tn), lambda j: (0, j)),
        compiler_params=_cparams(("arbitrary",)),
        name="ada",
    )(c, w, b.reshape(1, n))


def _inproj_kernel(x_ref, ada_ref, g1_ref, w_ref, wkt_ref,
                   us_ref, q_ref, qi_ref, kkt_ref, v_ref, wi_ref, g_ref, *, d, ssm_w, attn_w, idx_w):
    x = x_ref[0]
    shift = ada_ref[0, :, 0:d]
    scale = ada_ref[0, :, d:2 * d]
    u = _bf(_rms(x, g1_ref[...]) * (1.0 + scale) + shift)
    r = _dot(u, w_ref[...])
    o = 0
    us_ref[0] = r[:, o:o + ssm_w]
    o += ssm_w
    q_ref[0] = _bf(r[:, o:o + attn_w])
    o += attn_w
    qi_ref[0] = _bf(r[:, o:o + idx_w])
    o += idx_w
    v_ref[0] = _bf(r[:, o:o + HEAD_DIM])
    o += LANES
    wi_ref[0] = r[:, o:o + IDX_HEADS]
    o += LANES
    g_ref[0] = r[:, o:o + 2 * d]
    kkt_ref[0] = _bf(_dot_nt(wkt_ref[...], u))


def _inproj(x, ada3, g1, w_in):
    bsz, seq, d = x.shape
    ssm_w = 512
    attn_w = N_HEADS * HEAD_DIM
    idx_w = IDX_HEADS * IDX_DIM
    sizes = (ssm_w, attn_w, HEAD_DIM, HEAD_DIM, idx_w, IDX_DIM, IDX_HEADS, d, d)
    offs = [0]
    for s in sizes:
        offs.append(offs[-1] + s)
    w_ssm, w_q, w_k, w_v, w_qi, w_ki, w_wi, w_gs, w_ga = [w_in[:, offs[i]:offs[i + 1]] for i in range(9)]
    zpad = lambda n: jnp.zeros((d, n), F32)
    wbig = jnp.concatenate([
        w_ssm, w_q * (HEAD_DIM ** -0.5), w_qi * (IDX_DIM ** -0.5),
        w_v, zpad(LANES - HEAD_DIM), w_wi, zpad(LANES - IDX_HEADS), w_gs, w_ga], axis=1).astype(BF16)
    wkt = jnp.concatenate([w_k, w_ki], axis=1).T.astype(BF16)
    nw = wbig.shape[1]
    tl = INPROJ_ROWS
    kern = functools.partial(_inproj_kernel, d=d, ssm_w=ssm_w, attn_w=attn_w, idx_w=idx_w)
    row = lambda w: pl.BlockSpec((1, tl, w), lambda b, l: (b, l, 0))
    return pl.pallas_call(
        kern,
        out_shape=(jax.ShapeDtypeStruct((bsz, seq, ssm_w), F32),
                   jax.ShapeDtypeStruct((bsz, seq, attn_w), BF16),
                   jax.ShapeDtypeStruct((bsz, seq, idx_w), BF16),
                   jax.ShapeDtypeStruct((bsz, 2 * HEAD_DIM, seq), BF16),
                   jax.ShapeDtypeStruct((bsz, seq, HEAD_DIM), BF16),
                   jax.ShapeDtypeStruct((bsz, seq, IDX_HEADS), F32),
                   jax.ShapeDtypeStruct((bsz, seq, 2 * d), F32)),
        grid=(bsz, seq // tl),
        in_specs=[row(d),
                  pl.BlockSpec((1, 1, ada3.shape[2]), lambda b, l: (b, 0, 0)),
                  pl.BlockSpec((1, d), lambda b, l: (0, 0)),
                  pl.BlockSpec((d, nw), lambda b, l: (0, 0)),
                  pl.BlockSpec((2 * HEAD_DIM, d), lambda b, l: (0, 0))],
        out_specs=(row(ssm_w), row(attn_w), row(idx_w),
                   pl.BlockSpec((1, 2 * HEAD_DIM, tl), lambda b, l: (b, 0, l)),
                   row(HEAD_DIM), row(IDX_HEADS), row(2 * d)),
        compiler_params=_cparams(("parallel", "parallel")),
        name="inproj",
    )(x, ada3, g1.reshape(1, d), wbig, wkt)


def _s5disc_kernel(lr_ref, li_ref, ldt_ref, br_ref, bi_ref, are_ref, aim_ref, bbr_ref, bbi_ref):
    lr = lr_ref[...]
    li = li_ref[...]
    dt = jnp.exp(ldt_ref[...])
    mag = jnp.exp(lr * dt)
    a_re = mag * jnp.cos(li * dt)
    a_im = mag * jnp.sin(li * dt)
    den = lr * lr + li * li
    n_re = a_re - 1.0
    f_re = (n_re * lr + a_im * li) / den
    f_im = (a_im * lr - n_re * li) / den
    br = br_ref[...]
    bi = bi_ref[...]
    are_ref[...] = a_re
    aim_ref[...] = a_im
    bbr_ref[...] = f_re * br - f_im * bi
    bbi_ref[...] = f_re * bi + f_im * br


def _s5disc(lam_re, lam_im, log_dt, b_re, b_im):
    g, p = lam_re.shape
    h = b_re.shape[2]
    rep = lambda a: jnp.repeat(a, h, axis=1)
    ldt = jnp.broadcast_to(log_dt[:, None], (g, p * h))
    sds = jax.ShapeDtypeStruct((g, p * h), F32)
    a_re, a_im, bb_re, bb_im = pl.pallas_call(
        _s5disc_kernel, out_shape=(sds, sds, sds, sds), name="s5disc",
    )(rep(lam_re), rep(lam_im), ldt, b_re.reshape(g, p * h), b_im.reshape(g, p * h))
    return a_re[:, ::h], a_im[:, ::h], bb_re.reshape(g, p, h), bb_im.reshape(g, p, h)


def _s5_kernel(u_ref, wbh_ref, wbl_ref, ar_ref, ai_ref, cch_ref, ccl_ref, dsk_ref, wg_ref, bg_ref,
               o_ref, buf, hst, *, tl, width):
    nch = width // S5_LANE_CHUNK
    sw = S5_LANE_CHUNK // SSM_GROUP * SSM_STATE
    rows = tl * SUBLANES

    @pl.when(pl.program_id(0) == 0)
    def _():
        hst[...] = jnp.zeros_like(hst)

    u = u_ref[...].reshape(rows, width)
    uh, ul = _split(u)
    for j in range(nch):
        cs = slice(j * S5_LANE_CHUNK, (j + 1) * S5_LANE_CHUNK)
        buf[:, j * 2 * sw:(j + 1) * 2 * sw] = (
            _dot(uh[:, cs], wbh_ref[j]) + (_dot(uh[:, cs], wbl_ref[j]) + _dot(ul[:, cs], wbh_ref[j])))

    for j in range(nch):
        re_cols = slice(j * 2 * sw, j * 2 * sw + sw)
        im_cols = slice(j * 2 * sw + sw, (j + 1) * 2 * sw)
        a_re = jnp.broadcast_to(ar_ref[:, j * sw:(j + 1) * sw], (SUBLANES, sw))
        a_im = jnp.broadcast_to(ai_ref[:, j * sw:(j + 1) * sw], (SUBLANES, sw))

        def step(t, carry, re_cols=re_cols, im_cols=im_cols, a_re=a_re, a_im=a_im):
            h_re, h_im = carry
            r0 = pl.multiple_of(t * SUBLANES, SUBLANES)
            n_re = (a_re * h_re - a_im * h_im) + buf[pl.ds(r0, SUBLANES), re_cols]
            n_im = (a_re * h_im + a_im * h_re) + buf[pl.ds(r0, SUBLANES), im_cols]
            buf[pl.ds(r0, SUBLANES), re_cols] = n_re
            buf[pl.ds(r0, SUBLANES), im_cols] = n_im
            return n_re, n_im

        h_re, h_im = lax.fori_loop(0, tl, step, (hst[:, re_cols], hst[:, im_cols]), unroll=8)
        hst[:, re_cols] = h_re
        hst[:, im_cols] = h_im

    ys = []
    for j in range(nch):
        hh, hl = _split(buf[:, j * 2 * sw:(j + 1) * 2 * sw])
        ys.append(_dot(hh, cch_ref[j]) + (_dot(hh, ccl_ref[j]) + _dot(hl, cch_ref[j])))
    y = jnp.concatenate(ys, axis=1) + dsk_ref[...] * u
    y = jax.nn.gelu(y)
    y = y * jax.nn.sigmoid(_dot(_bf(y), wg_ref[...]) + bg_ref[...])
    o_ref[...] = _bf(y).reshape(tl, SUBLANES, width)


def _s5(u_t, a_re, a_im, bb_re, bb_im, c_re, c_im, d_skip, w_glu, b_glu):
    seq, bsz, width = u_t.shape
    assert bsz == SUBLANES
    nch = width // S5_LANE_CHUNK
    gpc = S5_LANE_CHUNK // SSM_GROUP
    sw = gpc * SSM_STATE
    eye = jnp.eye(gpc, dtype=F32)

    def bmat(bb):
        t = bb.reshape(nch, gpc, SSM_STATE, SSM_GROUP).transpose(0, 1, 3, 2)
        return jnp.einsum('jghp,gk->jghkp', t, eye).reshape(nch, S5_LANE_CHUNK, sw)

    def cmat(cc):
        t = cc.reshape(nch, gpc, SSM_GROUP, SSM_STATE).transpose(0, 1, 3, 2)
        return jnp.einsum('jgph,gk->jgpkh', t, eye).reshape(nch, sw, S5_LANE_CHUNK)

    wb = jnp.concatenate([bmat(bb_re), bmat(bb_im)], axis=2)
    cc = jnp.concatenate([cmat(c_re), -cmat(c_im)], axis=1)
    wbh = wb.astype(BF16)
    wbl = (wb - wbh.astype(F32)).astype(BF16)
    cch = cc.astype(BF16)
    ccl = (cc - cch.astype(F32)).astype(BF16)
    tl = S5_STEPS
    full = lambda a: pl.BlockSpec(a.shape, lambda i: (0,) * a.ndim)
    args = (u_t, wbh, wbl, a_re.reshape(1, -1), a_im.reshape(1, -1), cch, ccl,
            d_skip.reshape(1, width), w_glu.astype(BF16), b_glu.reshape(1, width))
    return pl.pallas_call(
        functools.partial(_s5_kernel, tl=tl, width=width),
        out_shape=jax.ShapeDtypeStruct((seq, bsz, width), BF16),
        grid=(seq // tl,),
        in_specs=[pl.BlockSpec((tl, bsz, width), lambda i: (i, 0, 0))] + [full(a) for a in args[1:]],
        out_specs=pl.BlockSpec((tl, bsz, width), lambda i: (i, 0, 0)),
        scratch_shapes=[pltpu.VMEM((tl * SUBLANES, nch * 2 * sw), F32),
                        pltpu.VMEM((SUBLANES, nch * 2 * sw), F32)],
        compiler_params=_cparams(("arbitrary",)),
        name="s5",
    )(*args)


def _dsa_kernel(q_ref, qi_ref, wi_ref, kkt_ref, v_ref, o_ref, key_s, mb_s, m_s, l_s, acc_s, *, tq, tk, topk):
    i = pl.program_id(1)
    q0 = i * tq
    nkt = (q0 + tq + tk - 1) // tk
    row_g = q0 + lax.broadcasted_iota(I32, (tq, tk), 0)
    col_l = lax.broadcasted_iota(I32, (tq, tk), 1)

    wb = wi_ref[0] * (IDX_HEADS ** -0.5)

    def score_tile(j, _):
        c0 = pl.multiple_of(j * tk, tk)
        kit = kkt_ref[0, HEAD_DIM:2 * HEAD_DIM, pl.ds(c0, tk)]
        acc = jnp.zeros((tq, tk), F32)
        for h in range(IDX_HEADS):
            s = _dot(qi_ref[0, :, h * IDX_DIM:(h + 1) * IDX_DIM], kit)
            acc = acc + wb[:, h:h + 1] * jnp.maximum(s, 0.0)
        bits = lax.bitcast_convert_type(acc, I32)
        key = jnp.where(bits < 0, bits ^ jnp.int32(0x7FFFFFFF), bits)
        key = jnp.where(acc == 0.0, 0, key)
        key = jnp.where(col_l + c0 <= row_g, key, INT_MIN)
        key_s[:, pl.ds(c0, tk)] = key
        return 0

    lax.fori_loop(0, nkt, score_tile, 0)

    rb = DSA_SEL_ROWS
    for r in range(tq // rb):
        rows = slice(r * rb, (r + 1) * rb)

        def bit_step(b, ans_u, rows=rows):
            cand_u = ans_u | lax.shift_left(jnp.int32(1), 31 - b)
            cand_s = cand_u ^ INT_MIN

            def count_tile(j, cnt):
                c0 = pl.multiple_of(j * tk, tk)
                ind = jnp.where(key_s[rows, pl.ds(c0, tk)] >= cand_s, 1, 0)
                for c in range(tk // LANES):
                    cnt = cnt + ind[:, c * LANES:(c + 1) * LANES]
                return cnt

            cnt = lax.fori_loop(0, nkt, count_tile, jnp.zeros((rb, LANES), I32))
            total = jnp.sum(cnt.astype(F32), axis=1, keepdims=True)
            return jnp.where(total >= float(topk), cand_u, ans_u)

        ans_u = lax.fori_loop(0, 32, bit_step, jnp.zeros((rb, 1), I32))
        thr = jnp.maximum(ans_u ^ INT_MIN, INT_MIN + 1)

        def bias_tile(j, _, rows=rows, thr=thr):
            c0 = pl.multiple_of(j * tk, tk)
            mb_s[rows, pl.ds(c0, tk)] = jnp.where(key_s[rows, pl.ds(c0, tk)] >= thr, 0.0, NEG_BIG)
            return 0

        lax.fori_loop(0, nkt, bias_tile, 0)

    col_f = lax.broadcasted_iota(I32, (1, tk), 1).astype(F32)
    for h in range(N_HEADS):
        slope = 2.0 ** (-8.0 * (h + 1) / N_HEADS)
        m_s[...] = jnp.full(m_s.shape, NEG_BIG, F32)
        l_s[...] = jnp.zeros(l_s.shape, F32)
        acc_s[...] = jnp.zeros(acc_s.shape, F32)
        qh = q_ref[0, :, h * HEAD_DIM:(h + 1) * HEAD_DIM]

        def attn_tile(j, _, qh=qh, slope=slope):
            c0 = pl.multiple_of(j * tk, tk)
            kt = kkt_ref[0, 0:HEAD_DIM, pl.ds(c0, tk)]
            cb = (col_f - (q0 - c0).astype(F32)) * slope
            s = _dot(qh, kt) + mb_s[:, pl.ds(c0, tk)] + cb
            m_prev = m_s[...]
            m_new = jnp.maximum(m_prev, jnp.max(s, axis=1, keepdims=True))
            alpha = jnp.exp(m_prev - m_new)
            p = jnp.exp(s - m_new)
            l_s[...] = alpha * l_s[...] + jnp.sum(p, axis=1, keepdims=True)
            acc_s[...] = alpha * acc_s[...] + _dot(_bf(p), v_ref[0, pl.ds(c0, tk), :])
            m_s[...] = m_new
            return 0

        lax.fori_loop(0, nkt, attn_tile, 0)
        o_ref[0, :, h * HEAD_DIM:(h + 1) * HEAD_DIM] = _bf(acc_s[...] / l_s[...])


def _dsa(q, qi, wi, kkt, v):
    bsz, seq, aw = q.shape
    tq = min(DSA_Q_ROWS, seq)
    tk = min(DSA_K_COLS, seq)
    topk = min(TOPK_MAX, seq // 4)
    kern = functools.partial(_dsa_kernel, tq=tq, tk=tk, topk=topk)
    return pl.pallas_call(
        kern,
        out_shape=jax.ShapeDtypeStruct((bsz, seq, aw), BF16),
        grid=(bsz, seq // tq),
        in_specs=[pl.BlockSpec((1, tq, aw), lambda b, i: (b, i, 0)),
                  pl.BlockSpec((1, tq, qi.shape[2]), lambda b, i: (b, i, 0)),
                  pl.BlockSpec((1, tq, IDX_HEADS), lambda b, i: (b, i, 0)),
                  pl.BlockSpec((1, 2 * HEAD_DIM, seq), lambda b, i: (b, 0, 0)),
                  pl.BlockSpec((1, seq, HEAD_DIM), lambda b, i: (b, 0, 0))],
        out_specs=pl.BlockSpec((1, tq, aw), lambda b, i: (b, i, 0)),
        scratch_shapes=[pltpu.VMEM((tq, seq), I32),
                        pltpu.VMEM((tq, seq), F32),
                        pltpu.VMEM((tq, 1), F32),
                        pltpu.VMEM((tq, 1), F32),
                        pltpu.VMEM((tq, HEAD_DIM), F32)],
        compiler_params=_cparams(("parallel", "parallel")),
        name="dsa",
    )(q, qi, wi, kkt, v)


def _mix_kernel(x_ref, ys_ref, ya_ref, g_ref, ada_ref, wps_ref, wpa_ref, wo_ref, g2_ref, h_ref, u2_ref, *, d):
    gate1 = ada_ref[0, :, 2 * d:3 * d]
    shift2 = ada_ref[0, :, 3 * d:4 * d]
    scale2 = ada_ref[0, :, 4 * d:5 * d]
    g = g_ref[0]
    mixed = (jax.nn.sigmoid(g[:, 0:d]) * _dot(ys_ref[0], wps_ref[...])
             + jax.nn.sigmoid(g[:, d:2 * d]) * _dot(ya_ref[0], wpa_ref[...]))
    h = x_ref[0] + gate1 * _dot(_bf(mixed), wo_ref[...])
    h_ref[0] = h
    u2_ref[0] = _rms(h, g2_ref[...]) * (1.0 + scale2) + shift2


def _mix(x, ys, ya, g, ada3, wps, wpa, wo, g2):
    bsz, seq, d = x.shape
    tm = MIX_ROWS
    row = lambda w: pl.BlockSpec((1, tm, w), lambda b, l: (b, l, 0))
    full = lambda a: pl.BlockSpec(a.shape, lambda b, l: (0,) * a.ndim)
    wps, wpa, wo = wps.astype(BF16), wpa.astype(BF16), wo.astype(BF16)
    g2 = g2.reshape(1, d)
    return pl.pallas_call(
        functools.partial(_mix_kernel, d=d),
        out_shape=(jax.ShapeDtypeStruct((bsz, seq, d), F32), jax.ShapeDtypeStruct((bsz, seq, d), F32)),
        grid=(bsz, seq // tm),
        in_specs=[row(d), row(ys.shape[2]), row(ya.shape[2]), row(2 * d),
                  pl.BlockSpec((1, 1, ada3.shape[2]), lambda b, l: (b, 0, 0)),
                  full(wps), full(wpa), full(wo), full(g2)],
        out_specs=(row(d), row(d)),
        compiler_params=_cparams(("parallel", "parallel")),
        name="mix",
    )(x, ys, ya, g, ada3, wps, wpa, wo, g2)


def _first_max(cur, idx, axis, big):
    m = jnp.max(cur, axis=axis, keepdims=True)
    first = jnp.min(jnp.where(cur == m, idx, big), axis=axis, keepdims=True)
    return m, idx == first


def _route_kernel(u_ref, wrh_ref, wrl_ref, rb_ref, tri_ref, e_ref, gt_ref, rk_ref, cnt_ref, run_s, *, t):
    @pl.when(pl.program_id(0) == 0)
    def _():
        run_s[...] = jnp.zeros_like(run_s)

    uh, ul = _split(u_ref[...])
    logits = _dot_nt(wrh_ref[...], uh) + (_dot_nt(wrl_ref[...], uh) + _dot_nt(wrh_ref[...], ul))
    scores = jax.nn.sigmoid(logits)
    biased = scores + rb_ref[...]
    per_group = N_EXPERTS // N_GROUPS
    b3 = biased.reshape(N_GROUPS, per_group, t)
    i3 = lax.broadcasted_iota(I32, b3.shape, 1)
    m1, hit1 = _first_max(b3, i3, 1, per_group)
    m2 = jnp.max(jnp.where(hit1, -jnp.inf, b3), axis=1, keepdims=True)
    gs = (m1 + m2).reshape(N_GROUPS, t)
    gi = lax.broadcasted_iota(I32, gs.shape, 0)
    gsel = jnp.zeros(gs.shape, F32)
    for _ in range(TOPK_GROUPS):
        _, hit = _first_max(gs, gi, 0, N_GROUPS)
        gsel = jnp.where(hit, 1.0, gsel)
        gs = jnp.where(hit, -jnp.inf, gs)
    cur = jnp.where(gsel.reshape(N_GROUPS, 1, t) > 0.0, b3, -jnp.inf).reshape(N_EXPERTS, t)
    ei = lax.broadcasted_iota(I32, cur.shape, 0)
    hits = []
    gates = []
    for _ in range(TOP_K):
        _, hit = _first_max(cur, ei, 0, N_EXPERTS)
        hits.append(hit)
        gates.append(jnp.sum(jnp.where(hit, scores, 0.0), axis=0, keepdims=True))
        cur = jnp.where(hit, -jnp.inf, cur)
    gate = jnp.concatenate(gates, axis=0)
    gt_ref[...] = gate / jnp.sum(gate, axis=0, keepdims=True) * ROUTED_SCALE
    e_ref[...] = jnp.concatenate(
        [jnp.sum(jnp.where(hit, ei, 0), axis=0, keepdims=True) for hit in hits], axis=0)
    onehot = jnp.zeros(cur.shape, F32)
    for hit in hits:
        onehot = jnp.where(hit, 1.0, onehot)
    before = _dot(_bf(onehot), tri_ref[...]) + run_s[:, 0:1]
    rk_ref[...] = jnp.concatenate(
        [jnp.sum(jnp.where(hit, before, 0.0), axis=0, keepdims=True) for hit in hits], axis=0).astype(I32)
    run_s[...] = run_s[...] + jnp.sum(onehot, axis=1, keepdims=True)
    cnt_ref[...] = run_s[...]


def _route(u2, w_router, router_bias):
    n, d = u2.shape
    t = min(ROUTE_ROWS, n)
    wt = w_router.T
    wrh = wt.astype(BF16)
    wrl = (wt - wrh.astype(F32)).astype(BF16)
    tri = (jnp.arange(t)[:, None] < jnp.arange(t)[None, :]).astype(BF16)
    full = lambda a: pl.BlockSpec(a.shape, lambda i: (0,) * a.ndim)
    col = pl.BlockSpec((TOP_K, t), lambda i: (0, i))
    rb = router_bias.reshape(N_EXPERTS, 1)
    return pl.pallas_call(
        functools.partial(_route_kernel, t=t),
        out_shape=(jax.ShapeDtypeStruct((TOP_K, n), I32), jax.ShapeDtypeStruct((TOP_K, n), F32),
                   jax.ShapeDtypeStruct((TOP_K, n), I32), jax.ShapeDtypeStruct((N_EXPERTS, LANES), F32)),
        grid=(n // t,),
        in_specs=[pl.BlockSpec((t, d), lambda i: (i, 0)), full(wrh), full(wrl), full(rb), full(tri)],
        out_specs=(col, col, col, pl.BlockSpec((N_EXPERTS, LANES), lambda i: (0, 0))),
        scratch_shapes=[pltpu.VMEM((N_EXPERTS, LANES), F32)],
        compiler_params=_cparams(("arbitrary",)),
        name="route",
    )(u2, wrh, wrl, rb, tri)


def _dest_kernel(ps_ref, e_ref, rk_ref, o_ref):
    e = e_ref[...]
    acc = rk_ref[...]
    for x in range(N_EXPERTS):
        acc = acc + jnp.where(e == x, ps_ref[x], 0)
    o_ref[...] = acc


def _dest(pstart, eidx_t, rank_t):
    k, n = eidx_t.shape
    tn = min(DEST_COLS, n)
    blk = pl.BlockSpec((k, tn), lambda i, ps: (0, i))
    return pl.pallas_call(
        _dest_kernel,
        out_shape=jax.ShapeDtypeStruct((k, n), I32),
        grid_spec=pltpu.PrefetchScalarGridSpec(
            num_scalar_prefetch=1, grid=(n // tn,), in_specs=[blk, blk], out_specs=blk),
        compiler_params=_cparams(("parallel",)),
        name="dest",
    )(pstart, eidx_t, rank_t)


def _dispatch_kernel(dest_ref, u_hbm, xs_in, xs_hbm, sems, *, td):
    del xs_in
    base = pl.program_id(0) * td
    nb = td // DISPATCH_BATCH

    def row_copy(t, k, slot):
        return pltpu.make_async_copy(u_hbm.at[pl.ds(base + t, 1)],
                                     xs_hbm.at[pl.ds(dest_ref[t * TOP_K + k], 1)], sems.at[slot])

    def issue(bi, slot):
        def body(tt, _):
            for k in range(TOP_K):
                row_copy(bi * DISPATCH_BATCH + tt, k, slot).start()
            return 0
        lax.fori_loop(0, DISPATCH_BATCH, body, 0)

    def drain(bi, slot):
        def body(tt, _):
            for k in range(TOP_K):
                row_copy(bi * DISPATCH_BATCH + tt, k, slot).wait()
            return 0
        lax.fori_loop(0, DISPATCH_BATCH, body, 0)

    issue(0, 0)
    for bi in range(1, nb):
        issue(bi, bi % 2)
        drain(bi - 1, (bi - 1) % 2)
    drain(nb - 1, (nb - 1) % 2)


def _dispatch(dest_flat, u2, n_rows):
    n, d = u2.shape
    td = min(DISPATCH_ROWS, n)
    xs0 = jnp.zeros((n_rows, d), F32)
    return pl.pallas_call(
        functools.partial(_dispatch_kernel, td=td),
        out_shape=jax.ShapeDtypeStruct((n_rows, d), F32),
        grid=(n // td,),
        in_specs=[pl.BlockSpec((td * TOP_K,), lambda i: (i,), memory_space=pltpu.SMEM),
                  pl.BlockSpec(memory_space=pl.ANY),
                  pl.BlockSpec(memory_space=pl.ANY)],
        out_specs=pl.BlockSpec(memory_space=pl.ANY),
        scratch_shapes=[pltpu.SemaphoreType.DMA((2,))],
        input_output_aliases={2: 0},
        compiler_params=_cparams(("arbitrary",)),
        name="dispatch",
    )(dest_flat, u2, xs0)


def _experts_kernel(be_ref, xs_ref, wg_ref, wu_ref, wd_ref, ys_ref):
    del be_ref
    x = _bf(xs_ref[...])
    hmid = jax.nn.silu(_dot(x, wg_ref[0])) * _dot(x, wu_ref[0])
    ys_ref[...] = _dot(_bf(hmid), wd_ref[0])


def _experts(blk_expert, xs, wg, wu, wd):
    rows, d = xs.shape
    de = wg.shape[2]
    nblk = rows // EXPERT_ROWS
    return pl.pallas_call(
        _experts_kernel,
        out_shape=jax.ShapeDtypeStruct((rows, d), F32),
        grid_spec=pltpu.PrefetchScalarGridSpec(
            num_scalar_prefetch=1, grid=(nblk,),
            in_specs=[pl.BlockSpec((EXPERT_ROWS, d), lambda i, be: (i, 0)),
                      pl.BlockSpec((1, d, de), lambda i, be: (be[i], 0, 0)),
                      pl.BlockSpec((1, d, de), lambda i, be: (be[i], 0, 0)),
                      pl.BlockSpec((1, de, d), lambda i, be: (be[i], 0, 0))],
            out_specs=pl.BlockSpec((EXPERT_ROWS, d), lambda i, be: (i, 0))),
        compiler_params=_cparams(("parallel",)),
        name="experts",
    )(blk_expert, xs, wg.astype(BF16), wu.astype(BF16), wd.astype(BF16))


def _combine_kernel(dest_ref, ys_hbm, gate_ref, h_ref, u2_ref, ada_ref, wsg_ref, wsu_ref, wsd_ref, gf_ref,
                    o_ref, buf, sem, *, tc, d):
    def row_copy(t, k):
        return pltpu.make_async_copy(ys_hbm.at[pl.ds(dest_ref[t * TOP_K + k], 1)],
                                     buf.at[k, pl.ds(t, 1)], sem.at[0])

    def issue(t, _):
        for k in range(TOP_K):
            row_copy(t, k).start()
        return 0

    def drain(t, _):
        for k in range(TOP_K):
            row_copy(t, k).wait()
        return 0

    lax.fori_loop(0, tc, issue, 0)
    x = _bf(u2_ref[...])
    shared = _dot(_bf(jax.nn.silu(_dot(x, wsg_ref[...])) * _dot(x, wsu_ref[...])), wsd_ref[...])
    lax.fori_loop(0, tc, drain, 0)
    gate = gate_ref[...]
    routed = gate[:, 0:1] * buf[0]
    for k in range(1, TOP_K):
        routed = routed + gate[:, k:k + 1] * buf[k]
    gate2 = ada_ref[0, :, 5 * d:6 * d]
    h = h_ref[...] + gate2 * (routed + shared)
    o_ref[...] = _rms(h, gf_ref[...])


def _combine(dest_flat, ys, gate, h1, u2, ada3, wsg, wsu, wsd, gf, seq):
    n, d = h1.shape
    tc = min(COMBINE_ROWS, seq)
    per_b = seq // tc
    row = pl.BlockSpec((tc, d), lambda i: (i, 0))
    full = lambda a: pl.BlockSpec(a.shape, lambda i: (0,) * a.ndim)
    wsg, wsu, wsd = wsg.astype(BF16), wsu.astype(BF16), wsd.astype(BF16)
    gf = gf.reshape(1, d)
    return pl.pallas_call(
        functools.partial(_combine_kernel, tc=tc, d=d),
        out_shape=jax.ShapeDtypeStruct((n, d), F32),
        grid=(n // tc,),
        in_specs=[pl.BlockSpec((tc * TOP_K,), lambda i: (i,), memory_space=pltpu.SMEM),
                  pl.BlockSpec(memory_space=pl.ANY),
                  pl.BlockSpec((tc, TOP_K), lambda i: (i, 0)),
                  row, row,
                  pl.BlockSpec((1, 1, ada3.shape[2]), lambda i: (i // per_b, 0, 0)),
                  full(wsg), full(wsu), full(wsd), full(gf)],
        out_specs=row,
        scratch_shapes=[pltpu.VMEM((TOP_K, tc, d), F32), pltpu.SemaphoreType.DMA((1,))],
        compiler_params=_cparams(("arbitrary",)),
        name="combine",
    )(dest_flat, ys, gate, h1, u2, ada3, wsg, wsu, wsd, gf)


def _moe(h1, u2, ada3, w_router, router_bias, wg, wu, wd, wsg, wsu, wsd, gf):
    bsz, seq, d = h1.shape
    n = bsz * seq
    h1f = h1.reshape(n, d)
    u2f = u2.reshape(n, d)
    eidx_t, gate_t, rank_t, counts = _route(u2f, w_router, router_bias)
    cnt = counts[:, 0].astype(I32)
    padded = (cnt + EXPERT_ROWS - 1) // EXPERT_ROWS * EXPERT_ROWS
    pend = jnp.cumsum(padded)
    pstart = (pend - padded).astype(I32)
    nblk = (n * TOP_K + N_EXPERTS * (EXPERT_ROWS - 1) + EXPERT_ROWS - 1) // EXPERT_ROWS
    blk_expert = jnp.minimum(
        jnp.searchsorted(pend, jnp.arange(nblk, dtype=I32) * EXPERT_ROWS, side='right'), N_EXPERTS - 1).astype(I32)
    dest_flat = _dest(pstart, eidx_t, rank_t).T.reshape(-1)
    xs = _dispatch(dest_flat, u2f, nblk * EXPERT_ROWS)
    ys = _experts(blk_expert, xs, wg, wu, wd)
    out = _combine(dest_flat, ys, gate_t.T, h1f, u2f, ada3, wsg, wsu, wsd, gf, seq)
    return out.reshape(bsz, seq, d)


def kernel(x, c, w_ada, b_ada, norm1_g, w_in, ssm_lambda_re, ssm_lambda_im, ssm_log_dt, ssm_b_re, ssm_b_im,
           ssm_c_re, ssm_c_im, ssm_d, ssm_w_glu, ssm_b_glu, w_proj_ssm, w_proj_attn, w_out, norm2_g, w_router,
           router_bias, w_exp_gate, w_exp_up, w_exp_down, w_sh_gate, w_sh_up, w_sh_down, norm_f_g):
    depth = w_ada.shape[0]
    assert depth == 1, "the final norm is fused into the last (only) layer's combine kernel"
    bsz, seq, d = x.shape
    layer = 0
    ada3 = _ada(c, w_ada[layer], b_ada[layer]).reshape(bsz, 1, 6 * d)
    us, q, qi, kkt, v, wi, g = _inproj(x, ada3, norm1_g[layer], w_in[layer])
    a_re, a_im, bb_re, bb_im = _s5disc(ssm_lambda_re[layer], ssm_lambda_im[layer], ssm_log_dt[layer],
                                       ssm_b_re[layer], ssm_b_im[layer])
    ys_t = _s5(us.transpose(1, 0, 2), a_re, a_im, bb_re, bb_im, ssm_c_re[layer], ssm_c_im[layer],
               ssm_d[layer], ssm_w_glu[layer], ssm_b_glu[layer])
    ya = _dsa(q, qi, wi, kkt, v)
    h1, u2 = _mix(x, ys_t.transpose(1, 0, 2), ya, g, ada3, w_proj_ssm[layer], w_proj_attn[layer],
                  w_out[layer], norm2_g[layer])
    return _moe(h1, u2, ada3, w_router[layer], router_bias[layer], w_exp_gate[layer], w_exp_up[layer],
                w_exp_down[layer], w_sh_gate[layer], w_sh_up[layer], w_sh_down[layer], norm_f_g)
```

```python
import functools
import math

import jax
import jax.numpy as jnp
from jax import lax
from jax.experimental import pallas as pl
from jax.experimental.pallas import tpu as pltpu

F32 = jnp.float32
BF16 = jnp.bfloat16
I32 = jnp.int32

SSM_GROUP = 16
SSM_STATE = 64
N_HEADS = 8
HEAD_DIM = 64
IDX_HEADS = 8
IDX_DIM = 64
TOPK_MAX = 256
N_EXPERTS = 64
TOP_K = 8
N_GROUPS = 8
TOPK_GROUPS = 4
ROUTED_SCALE = 2.5
DT_EPS = 1e-6

V7X_VMEM_LIMIT_BYTES = 56 * 1024 * 1024
LANES = 128
SUBLANES = 8

INPROJ_ROWS = 256
S5_STEPS = 64
S5_LANE_CHUNK = 128
DSA_Q_ROWS = 256
DSA_K_COLS = 512
DSA_SEL_ROWS = 64
MIX_ROWS = 256
ROUTE_ROWS = 512
DEST_COLS = 4096
DISPATCH_ROWS = 512
DISPATCH_BATCH = 32
EXPERT_ROWS = 256
COMBINE_ROWS = 256

NEG_BIG = -1e30
INT_MIN = -(2 ** 31)


def _cparams(sem):
    return pltpu.CompilerParams(dimension_semantics=sem, vmem_limit_bytes=V7X_VMEM_LIMIT_BYTES)


def _bf(x):
    return x.astype(BF16)


def _dot(a, b):
    return jnp.dot(a, b, preferred_element_type=F32)


def _dot_nt(a, b):
    return lax.dot_general(a, b, (((1,), (1,)), ((), ())), preferred_element_type=F32)


def _split(x):
    hi = _bf(x)
    lo = _bf(x - hi.astype(F32))
    return hi, lo


def _dot3(a, b):
    ah, al = _split(a)
    bh, bl = _split(b)
    return _dot(ah, bh) + (_dot(ah, bl) + _dot(al, bh))


def _rms(x, g):
    return x * lax.rsqrt(jnp.mean(x * x, axis=-1, keepdims=True) + DT_EPS) * g


def _ada_kernel(c_ref, w_ref, b_ref, o_ref):
    c = c_ref[...]
    o_ref[...] = _dot3(c * jax.nn.sigmoid(c), w_ref[...]) + b_ref[...]


def _ada(c, w, b):
    bsz, d = c.shape
    n = w.shape[1]
    tn = 1024
    return pl.pallas_call(
        _ada_kernel,
        out_shape=jax.ShapeDtypeStruct((bsz, n), F32),
        grid=(n // tn,),
        in_specs=[pl.BlockSpec((bsz, d), lambda j: (0, 0)),
                  pl.BlockSpec((d, tn), lambda j: (0, j)),
                  pl.BlockSpec((1, tn), lambda j: (0, j))],
        out_specs=pl.BlockSpec((bsz, tn), lambda j: (0, j)),
        compiler_params=_cparams(("arbitrary",)),
        name="ada",
    )(c, w, b.reshape(1, n))


def _inproj_kernel(x_ref, ada_ref, g1_ref, w_ref, wkt_ref,
                   us_ref, q_ref, qi_ref, kkt_ref, v_ref, wi_ref, g_ref, *, d, ssm_w, attn_w, idx_w):
    x = x_ref[0]
    shift = ada_ref[0, :, 0:d]
    scale = ada_ref[0, :, d:2 * d]
    u = _bf(_rms(x, g1_ref[...]) * (1.0 + scale) + shift)
    r = _dot(u, w_ref[...])
    o = 0
    us_ref[0] = r[:, o:o + ssm_w]
    o += ssm_w
    q_ref[0] = _bf(r[:, o:o + attn_w])
    o += attn_w
    qi_ref[0] = _bf(r[:, o:o + idx_w])
    o += idx_w
    v_ref[0] = _bf(r[:, o:o + HEAD_DIM])
    o += LANES
    wi_ref[0] = r[:, o:o + IDX_HEADS]
    o += LANES
    g_ref[0] = r[:, o:o + 2 * d]
    kkt_ref[0] = _bf(_dot_nt(wkt_ref[...], u))


def _inproj(x, ada3, g1, w_in):
    bsz, seq, d = x.shape
    ssm_w = 512
    attn_w = N_HEADS * HEAD_DIM
    idx_w = IDX_HEADS * IDX_DIM
    sizes = (ssm_w, attn_w, HEAD_DIM, HEAD_DIM, idx_w, IDX_DIM, IDX_HEADS, d, d)
    offs = [0]
    for s in sizes:
        offs.append(offs[-1] + s)
    w_ssm, w_q, w_k, w_v, w_qi, w_ki, w_wi, w_gs, w_ga = [w_in[:, offs[i]:offs[i + 1]] for i in range(9)]
    zpad = lambda n: jnp.zeros((d, n), F32)
    wbig = jnp.concatenate([
        w_ssm, w_q * (HEAD_DIM ** -0.5), w_qi * (IDX_DIM ** -0.5),
        w_v, zpad(LANES - HEAD_DIM), w_wi, zpad(LANES - IDX_HEADS), w_gs, w_ga], axis=1).astype(BF16)
    wkt = jnp.concatenate([w_k, w_ki], axis=1).T.astype(BF16)
    nw = wbig.shape[1]
    tl = INPROJ_ROWS
    kern = functools.partial(_inproj_kernel, d=d, ssm_w=ssm_w, attn_w=attn_w, idx_w=idx_w)
    row = lambda w: pl.BlockSpec((1, tl, w), lambda b, l: (b, l, 0))
    return pl.pallas_call(
        kern,
        out_shape=(jax.ShapeDtypeStruct((bsz, seq, ssm_w), F32),
                   jax.ShapeDtypeStruct((bsz, seq, attn_w), BF16),
                   jax.ShapeDtypeStruct((bsz, seq, idx_w), BF16),
                   jax.ShapeDtypeStruct((bsz, 2 * HEAD_DIM, seq), BF16),
                   jax.ShapeDtypeStruct((bsz, seq, HEAD_DIM), BF16),
                   jax.ShapeDtypeStruct((bsz, seq, IDX_HEADS), F32),
                   jax.ShapeDtypeStruct((bsz, seq, 2 * d), F32)),
        grid=(bsz, seq // tl),
        in_specs=[row(d),
                  pl.BlockSpec((1, 1, ada3.shape[2]), lambda b, l: (b, 0, 0)),
                  pl.BlockSpec((1, d), lambda b, l: (0, 0)),
                  pl.BlockSpec((d, nw), lambda b, l: (0, 0)),
                  pl.BlockSpec((2 * HEAD_DIM, d), lambda b, l: (0, 0))],
        out_specs=(row(ssm_w), row(attn_w), row(idx_w),
                   pl.BlockSpec((1, 2 * HEAD_DIM, tl), lambda b, l: (b, 0, l)),
                   row(HEAD_DIM), row(IDX_HEADS), row(2 * d)),
        compiler_params=_cparams(("arbitrary", "arbitrary")),
        name="inproj",
    )(x, ada3, g1.reshape(1, d), wbig, wkt)


def _s5disc_kernel(lr_ref, li_ref, ldt_ref, br_ref, bi_ref, are_ref, aim_ref, bbr_ref, bbi_ref):
    lr = lr_ref[...]
    li = li_ref[...]
    dt = jnp.exp(ldt_ref[...])
    mag = jnp.exp(lr * dt)
    a_re = mag * jnp.cos(li * dt)
    a_im = mag * jnp.sin(li * dt)
    den = lr * lr + li * li
    n_re = a_re - 1.0
    f_re = (n_re * lr + a_im * li) / den
    f_im = (a_im * lr - n_re * li) / den
    br = br_ref[...]
    bi = bi_ref[...]
    are_ref[...] = a_re
    aim_ref[...] = a_im
    bbr_ref[...] = f_re * br - f_im * bi
    bbi_ref[...] = f_re * bi + f_im * br


def _s5disc(lam_re, lam_im, log_dt, b_re, b_im):
    g, p = lam_re.shape
    h = b_re.shape[2]
    rep = lambda a: jnp.repeat(a, h, axis=1)
    ldt = jnp.broadcast_to(log_dt[:, None], (g, p * h))
    sds = jax.ShapeDtypeStruct((g, p * h), F32)
    a_re, a_im, bb_re, bb_im = pl.pallas_call(
        _s5disc_kernel, out_shape=(sds, sds, sds, sds), name="s5disc",
    )(rep(lam_re), rep(lam_im), ldt, b_re.reshape(g, p * h), b_im.reshape(g, p * h))
    return a_re[:, ::h], a_im[:, ::h], bb_re.reshape(g, p, h), bb_im.reshape(g, p, h)


def _s5_kernel(u_ref, wbh_ref, wbl_ref, ar_ref, ai_ref, cch_ref, ccl_ref, dsk_ref, wg_ref, bg_ref,
               o_ref, buf, hst, *, tl, width):
    nch = width // S5_LANE_CHUNK
    sw = S5_LANE_CHUNK // SSM_GROUP * SSM_STATE
    rows = tl * SUBLANES

    @pl.when(pl.program_id(0) == 0)
    def _():
        hst[...] = jnp.zeros_like(hst)

    u = u_ref[...].reshape(rows, width)
    uh, ul = _split(u)
    for j in range(nch):
        cs = slice(j * S5_LANE_CHUNK, (j + 1) * S5_LANE_CHUNK)
        buf[:, j * 2 * sw:(j + 1) * 2 * sw] = (
            _dot(uh[:, cs], wbh_ref[j]) + (_dot(uh[:, cs], wbl_ref[j]) + _dot(ul[:, cs], wbh_ref[j])))

    for j in range(nch):
        re_cols = slice(j * 2 * sw, j * 2 * sw + sw)
        im_cols = slice(j * 2 * sw + sw, (j + 1) * 2 * sw)
        a_re = jnp.broadcast_to(ar_ref[:, j * sw:(j + 1) * sw], (SUBLANES, sw))
        a_im = jnp.broadcast_to(ai_ref[:, j * sw:(j + 1) * sw], (SUBLANES, sw))

        def step(t, carry, re_cols=re_cols, im_cols=im_cols, a_re=a_re, a_im=a_im):
            h_re, h_im = carry
            r0 = pl.multiple_of(t * SUBLANES, SUBLANES)
            n_re = (a_re * h_re - a_im * h_im) + buf[pl.ds(r0, SUBLANES), re_cols]
            n_im = (a_re * h_im + a_im * h_re) + buf[pl.ds(r0, SUBLANES), im_cols]
            buf[pl.ds(r0, SUBLANES), re_cols] = n_re
            buf[pl.ds(r0, SUBLANES), im_cols] = n_im
            return n_re, n_im

        h_re, h_im = lax.fori_loop(0, tl, step, (hst[:, re_cols], hst[:, im_cols]), unroll=8)
        hst[:, re_cols] = h_re
        hst[:, im_cols] = h_im

    ys = []
    for j in range(nch):
        hh, hl = _split(buf[:, j * 2 * sw:(j + 1) * 2 * sw])
        ys.append(_dot(hh, cch_ref[j]) + (_dot(hh, ccl_ref[j]) + _dot(hl, cch_ref[j])))
    y = jnp.concatenate(ys, axis=1) + dsk_ref[...] * u
    y = jax.nn.gelu(y)
    y = y * jax.nn.sigmoid(_dot(_bf(y), wg_ref[...]) + bg_ref[...])
    o_ref[...] = _bf(y).reshape(tl, SUBLANES, width)


def _s5(u_t, a_re, a_im, bb_re, bb_im, c_re, c_im, d_skip, w_glu, b_glu):
    seq, bsz, width = u_t.shape
    assert bsz == SUBLANES
    nch = width // S5_LANE_CHUNK
    gpc = S5_LANE_CHUNK // SSM_GROUP
    sw = gpc * SSM_STATE
    eye = jnp.eye(gpc, dtype=F32)

    def bmat(bb):
        t = bb.reshape(nch, gpc, SSM_STATE, SSM_GROUP).transpose(0, 1, 3, 2)
        return jnp.einsum('jghp,gk->jghkp', t, eye).reshape(nch, S5_LANE_CHUNK, sw)

    def cmat(cc):
        t = cc.reshape(nch, gpc, SSM_GROUP, SSM_STATE).transpose(0, 1, 3, 2)
        return jnp.einsum('jgph,gk->jgpkh', t, eye).reshape(nch, sw, S5_LANE_CHUNK)

    wb = jnp.concatenate([bmat(bb_re), bmat(bb_im)], axis=2)
    cc = jnp.concatenate([cmat(c_re), -cmat(c_im)], axis=1)
    wbh = wb.astype(BF16)
    wbl = (wb - wbh.astype(F32)).astype(BF16)
    cch = cc.astype(BF16)
    ccl = (cc - cch.astype(F32)).astype(BF16)
    tl = S5_STEPS
    full = lambda a: pl.BlockSpec(a.shape, lambda i: (0,) * a.ndim)
    args = (u_t, wbh, wbl, a_re.reshape(1, -1), a_im.reshape(1, -1), cch, ccl,
            d_skip.reshape(1, width), w_glu.astype(BF16), b_glu.reshape(1, width))
    return pl.pallas_call(
        functools.partial(_s5_kernel, tl=tl, width=width),
        out_shape=jax.ShapeDtypeStruct((seq, bsz, width), BF16),
        grid=(seq // tl,),
        in_specs=[pl.BlockSpec((tl, bsz, width), lambda i: (i, 0, 0))] + [full(a) for a in args[1:]],
        out_specs=pl.BlockSpec((tl, bsz, width), lambda i: (i, 0, 0)),
        scratch_shapes=[pltpu.VMEM((tl * SUBLANES, nch * 2 * sw), F32),
                        pltpu.VMEM((SUBLANES, nch * 2 * sw), F32)],
        compiler_params=_cparams(("arbitrary",)),
        name="s5",
    )(*args)


def _dsa_kernel(q_ref, qi_ref, wi_ref, kkt_ref, v_ref, o_ref, key_s, mb_s, m_s, l_s, acc_s, *, tq, tk, topk):
    i = pl.program_id(1)
    q0 = i * tq
    nkt = (q0 + tq + tk - 1) // tk
    row_g = q0 + lax.broadcasted_iota(I32, (tq, tk), 0)
    col_l = lax.broadcasted_iota(I32, (tq, tk), 1)

    wb = wi_ref[0] * (IDX_HEADS ** -0.5)

    def score_tile(j, _):
        c0 = pl.multiple_of(j * tk, tk)
        kit = kkt_ref[0, HEAD_DIM:2 * HEAD_DIM, pl.ds(c0, tk)]
        acc = jnp.zeros((tq, tk), F32)
        for h in range(IDX_HEADS):
            s = _dot(qi_ref[0, :, h * IDX_DIM:(h + 1) * IDX_DIM], kit)
            acc = acc + wb[:, h:h + 1] * jnp.maximum(s, 0.0)
        bits = lax.bitcast_convert_type(acc, I32)
        key = jnp.where(bits < 0, bits ^ jnp.int32(0x7FFFFFFF), bits)
        key = jnp.where(acc == 0.0, 0, key)
        key = jnp.where(col_l + c0 <= row_g, key, INT_MIN)
        key_s[:, pl.ds(c0, tk)] = key
        return 0

    lax.fori_loop(0, nkt, score_tile, 0)

    rb = DSA_SEL_ROWS
    for r in range(tq // rb):
        rows = slice(r * rb, (r + 1) * rb)

        def bit_step(b, ans_u, rows=rows):
            cand_u = ans_u | lax.shift_left(jnp.int32(1), 31 - b)
            cand_s = cand_u ^ INT_MIN

            def count_tile(j, cnt):
                c0 = pl.multiple_of(j * tk, tk)
                ind = jnp.where(key_s[rows, pl.ds(c0, tk)] >= cand_s, 1, 0)
                for c in range(tk // LANES):
                    cnt = cnt + ind[:, c * LANES:(c + 1) * LANES]
                return cnt

            cnt = lax.fori_loop(0, nkt, count_tile, jnp.zeros((rb, LANES), I32))
            total = jnp.sum(cnt.astype(F32), axis=1, keepdims=True)
            return jnp.where(total >= float(topk), cand_u, ans_u)

        ans_u = lax.fori_loop(0, 32, bit_step, jnp.zeros((rb, 1), I32))
        thr = jnp.maximum(ans_u ^ INT_MIN, INT_MIN + 1)

        def bias_tile(j, _, rows=rows, thr=thr):
            c0 = pl.multiple_of(j * tk, tk)
            mb_s[rows, pl.ds(c0, tk)] = jnp.where(key_s[rows, pl.ds(c0, tk)] >= thr, 0.0, NEG_BIG)
            return 0

        lax.fori_loop(0, nkt, bias_tile, 0)

    col_f = lax.broadcasted_iota(I32, (1, tk), 1).astype(F32)
    for h in range(N_HEADS):
        slope = 2.0 ** (-8.0 * (h + 1) / N_HEADS)
        m_s[...] = jnp.full(m_s.shape, NEG_BIG, F32)
        l_s[...] = jnp.zeros(l_s.shape, F32)
        acc_s[...] = jnp.zeros(acc_s.shape, F32)
        qh = q_ref[0, :, h * HEAD_DIM:(h + 1) * HEAD_DIM]

        def attn_tile(j, _, qh=qh, slope=slope):
            c0 = pl.multiple_of(j * tk, tk)
            kt = kkt_ref[0, 0:HEAD_DIM, pl.ds(c0, tk)]
            cb = (col_f - (q0 - c0).astype(F32)) * slope
            s = _dot(qh, kt) + mb_s[:, pl.ds(c0, tk)] + cb
            m_prev = m_s[...]
            m_new = jnp.maximum(m_prev, jnp.max(s, axis=1, keepdims=True))
            alpha = jnp.exp(m_prev - m_new)
            p = jnp.exp(s - m_new)
            l_s[...] = alpha * l_s[...] + jnp.sum(p, axis=1, keepdims=True)
            acc_s[...] = alpha * acc_s[...] + _dot(_bf(p), v_ref[0, pl.ds(c0, tk), :])
            m_s[...] = m_new
            return 0

        lax.fori_loop(0, nkt, attn_tile, 0)
        o_ref[0, :, h * HEAD_DIM:(h + 1) * HEAD_DIM] = _bf(acc_s[...] / l_s[...])


def _dsa(q, qi, wi, kkt, v):
    bsz, seq, aw = q.shape
    tq = min(DSA_Q_ROWS, seq)
    tk = min(DSA_K_COLS, seq)
    topk = min(TOPK_MAX, seq // 4)
    kern = functools.partial(_dsa_kernel, tq=tq, tk=tk, topk=topk)
    return pl.pallas_call(
        kern,
        out_shape=jax.ShapeDtypeStruct((bsz, seq, aw), BF16),
        grid=(bsz, seq // tq),
        in_specs=[pl.BlockSpec((1, tq, aw), lambda b, i: (b, i, 0)),
                  pl.BlockSpec((1, tq, qi.shape[2]), lambda b, i: (b, i, 0)),
                  pl.BlockSpec((1, tq, IDX_HEADS), lambda b, i: (b, i, 0)),
                  pl.BlockSpec((1, 2 * HEAD_DIM, seq), lambda b, i: (b, 0, 0)),
                  pl.BlockSpec((1, seq, HEAD_DIM), lambda b, i: (b, 0, 0))],
        out_specs=pl.BlockSpec((1, tq, aw), lambda b, i: (b, i, 0)),
        scratch_shapes=[pltpu.VMEM((tq, seq), I32),
                        pltpu.VMEM((tq, seq), F32),
                        pltpu.VMEM((tq, 1), F32),
                        pltpu.VMEM((tq, 1), F32),
                        pltpu.VMEM((tq, HEAD_DIM), F32)],
        compiler_params=_cparams(("arbitrary", "arbitrary")),
        name="dsa",
    )(q, qi, wi, kkt, v)


def _mix_kernel(x_ref, ys_ref, ya_ref, g_ref, ada_ref, wps_ref, wpa_ref, wo_ref, g2_ref, h_ref, u2_ref, *, d):
    gate1 = ada_ref[0, :, 2 * d:3 * d]
    shift2 = ada_ref[0, :, 3 * d:4 * d]
    scale2 = ada_ref[0, :, 4 * d:5 * d]
    g = g_ref[0]
    mixed = (jax.nn.sigmoid(g[:, 0:d]) * _dot(ys_ref[0], wps_ref[...])
             + jax.nn.sigmoid(g[:, d:2 * d]) * _dot(ya_ref[0], wpa_ref[...]))
    h = x_ref[0] + gate1 * _dot(_bf(mixed), wo_ref[...])
    h_ref[0] = h
    u2_ref[0] = _rms(h, g2_ref[...]) * (1.0 + scale2) + shift2


def _mix(x, ys, ya, g, ada3, wps, wpa, wo, g2):
    bsz, seq, d = x.shape
    tm = MIX_ROWS
    row = lambda w: pl.BlockSpec((1, tm, w), lambda b, l: (b, l, 0))
    full = lambda a: pl.BlockSpec(a.shape, lambda b, l: (0,) * a.ndim)
    wps, wpa, wo = wps.astype(BF16), wpa.astype(BF16), wo.astype(BF16)
    g2 = g2.reshape(1, d)
    return pl.pallas_call(
        functools.partial(_mix_kernel, d=d),
        out_shape=(jax.ShapeDtypeStruct((bsz, seq, d), F32), jax.ShapeDtypeStruct((bsz, seq, d), F32)),
        grid=(bsz, seq // tm),
        in_specs=[row(d), row(ys.shape[2]), row(ya.shape[2]), row(2 * d),
                  pl.BlockSpec((1, 1, ada3.shape[2]), lambda b, l: (b, 0, 0)),
                  full(wps), full(wpa), full(wo), full(g2)],
        out_specs=(row(d), row(d)),
        compiler_params=_cparams(("arbitrary", "arbitrary")),
        name="mix",
    )(x, ys, ya, g, ada3, wps, wpa, wo, g2)


def _first_max(cur, idx, axis, big):
    m = jnp.max(cur, axis=axis, keepdims=True)
    first = jnp.min(jnp.where(cur == m, idx, big), axis=axis, keepdims=True)
    return m, idx == first


def _route_kernel(u_ref, wrh_ref, wrl_ref, rb_ref, tri_ref, e_ref, gt_ref, rk_ref, cnt_ref, run_s, *, t):
    @pl.when(pl.program_id(0) == 0)
    def _():
        run_s[...] = jnp.zeros_like(run_s)

    uh, ul = _split(u_ref[...])
    logits = _dot_nt(wrh_ref[...], uh) + (_dot_nt(wrl_ref[...], uh) + _dot_nt(wrh_ref[...], ul))
    scores = jax.nn.sigmoid(logits)
    biased = scores + rb_ref[...]
    per_group = N_EXPERTS // N_GROUPS
    b3 = biased.reshape(N_GROUPS, per_group, t)
    i3 = lax.broadcasted_iota(I32, b3.shape, 1)
    m1, hit1 = _first_max(b3, i3, 1, per_group)
    m2 = jnp.max(jnp.where(hit1, -jnp.inf, b3), axis=1, keepdims=True)
    gs = (m1 + m2).reshape(N_GROUPS, t)
    gi = lax.broadcasted_iota(I32, gs.shape, 0)
    gsel = jnp.zeros(gs.shape, F32)
    for _ in range(TOPK_GROUPS):
        _, hit = _first_max(gs, gi, 0, N_GROUPS)
        gsel = jnp.where(hit, 1.0, gsel)
        gs = jnp.where(hit, -jnp.inf, gs)
    cur = jnp.where(gsel.reshape(N_GROUPS, 1, t) > 0.0, b3, -jnp.inf).reshape(N_EXPERTS, t)
    ei = lax.broadcasted_iota(I32, cur.shape, 0)
    hits = []
    gates = []
    for _ in range(TOP_K):
        _, hit = _first_max(cur, ei, 0, N_EXPERTS)
        hits.append(hit)
        gates.append(jnp.sum(jnp.where(hit, scores, 0.0), axis=0, keepdims=True))
        cur = jnp.where(hit, -jnp.inf, cur)
    gate = jnp.concatenate(gates, axis=0)
    gt_ref[...] = gate / jnp.sum(gate, axis=0, keepdims=True) * ROUTED_SCALE
    e_ref[...] = jnp.concatenate(
        [jnp.sum(jnp.where(hit, ei, 0), axis=0, keepdims=True) for hit in hits], axis=0)
    onehot = jnp.zeros(cur.shape, F32)
    for hit in hits:
        onehot = jnp.where(hit, 1.0, onehot)
    before = _dot(_bf(onehot), tri_ref[...]) + run_s[:, 0:1]
    rk_ref[...] = jnp.concatenate(
        [jnp.sum(jnp.where(hit, before, 0.0), axis=0, keepdims=True) for hit in hits], axis=0).astype(I32)
    run_s[...] = run_s[...] + jnp.sum(onehot, axis=1, keepdims=True)
    cnt_ref[...] = run_s[...]


def _route(u2, w_router, router_bias):
    n, d = u2.shape
    t = min(ROUTE_ROWS, n)
    wt = w_router.T
    wrh = wt.astype(BF16)
    wrl = (wt - wrh.astype(F32)).astype(BF16)
    tri = (jnp.arange(t)[:, None] < jnp.arange(t)[None, :]).astype(BF16)
    full = lambda a: pl.BlockSpec(a.shape, lambda i: (0,) * a.ndim)
    col = pl.BlockSpec((TOP_K, t), lambda i: (0, i))
    rb = router_bias.reshape(N_EXPERTS, 1)
    return pl.pallas_call(
        functools.partial(_route_kernel, t=t),
        out_shape=(jax.ShapeDtypeStruct((TOP_K, n), I32), jax.ShapeDtypeStruct((TOP_K, n), F32),
                   jax.ShapeDtypeStruct((TOP_K, n), I32), jax.ShapeDtypeStruct((N_EXPERTS, LANES), F32)),
        grid=(n // t,),
        in_specs=[pl.BlockSpec((t, d), lambda i: (i, 0)), full(wrh), full(wrl), full(rb), full(tri)],
        out_specs=(col, col, col, pl.BlockSpec((N_EXPERTS, LANES), lambda i: (0, 0))),
        scratch_shapes=[pltpu.VMEM((N_EXPERTS, LANES), F32)],
        compiler_params=_cparams(("arbitrary",)),
        name="route",
    )(u2, wrh, wrl, rb, tri)


def _dest_kernel(ps_ref, e_ref, rk_ref, o_ref):
    e = e_ref[...]
    acc = rk_ref[...]
    for x in range(N_EXPERTS):
        acc = acc + jnp.where(e == x, ps_ref[x], 0)
    o_ref[...] = acc


def _dest(pstart, eidx_t, rank_t):
    k, n = eidx_t.shape
    tn = min(DEST_COLS, n)
    blk = pl.BlockSpec((k, tn), lambda i, ps: (0, i))
    return pl.pallas_call(
        _dest_kernel,
        out_shape=jax.ShapeDtypeStruct((k, n), I32),
        grid_spec=pltpu.PrefetchScalarGridSpec(
            num_scalar_prefetch=1, grid=(n // tn,), in_specs=[blk, blk], out_specs=blk),
        compiler_params=_cparams(("arbitrary",)),
        name="dest",
    )(pstart, eidx_t, rank_t)


def _dispatch_kernel(dest_ref, u_ref, xs_in, xs_hbm, sems, *, td):
    del xs_in
    nb = td // DISPATCH_BATCH

    def row_copy(t, k, slot):
        return pltpu.make_async_copy(u_ref.at[pl.ds(t, 1)],
                                     xs_hbm.at[pl.ds(dest_ref[t * TOP_K + k], 1)], sems.at[slot])

    def issue(bi, slot):
        def body(tt, _):
            for k in range(TOP_K):
                row_copy(bi * DISPATCH_BATCH + tt, k, slot).start()
            return 0
        lax.fori_loop(0, DISPATCH_BATCH, body, 0)

    def drain(bi, slot):
        def body(tt, _):
            for k in range(TOP_K):
                row_copy(bi * DISPATCH_BATCH + tt, k, slot).wait()
            return 0
        lax.fori_loop(0, DISPATCH_BATCH, body, 0)

    issue(0, 0)
    for bi in range(1, nb):
        issue(bi, bi % 2)
        drain(bi - 1, (bi - 1) % 2)
    drain(nb - 1, (nb - 1) % 2)


def _dispatch(dest_flat, u2, n_rows):
    n, d = u2.shape
    td = min(DISPATCH_ROWS, n)
    xs0 = jnp.zeros((n_rows, d), F32)
    return pl.pallas_call(
        functools.partial(_dispatch_kernel, td=td),
        out_shape=jax.ShapeDtypeStruct((n_rows, d), F32),
        grid=(n // td,),
        in_specs=[pl.BlockSpec((td * TOP_K,), lambda i: (i,), memory_space=pltpu.SMEM),
                  pl.BlockSpec((td, d), lambda i: (i, 0)),
                  pl.BlockSpec(memory_space=pl.ANY)],
        out_specs=pl.BlockSpec(memory_space=pl.ANY),
        scratch_shapes=[pltpu.SemaphoreType.DMA((2,))],
        input_output_aliases={2: 0},
        compiler_params=_cparams(("arbitrary",)),
        name="dispatch",
    )(dest_flat, u2, xs0)


def _experts_kernel(be_ref, xs_ref, wg_ref, wu_ref, wd_ref, ys_ref):
    del be_ref
    x = _bf(xs_ref[...])
    hmid = jax.nn.silu(_dot(x, wg_ref[0])) * _dot(x, wu_ref[0])
    ys_ref[...] = _dot(_bf(hmid), wd_ref[0])


def _experts(blk_expert, xs, wg, wu, wd):
    rows, d = xs.shape
    de = wg.shape[2]
    nblk = rows // EXPERT_ROWS
    return pl.pallas_call(
        _experts_kernel,
        out_shape=jax.ShapeDtypeStruct((rows, d), F32),
        grid_spec=pltpu.PrefetchScalarGridSpec(
            num_scalar_prefetch=1, grid=(nblk,),
            in_specs=[pl.BlockSpec((EXPERT_ROWS, d), lambda i, be: (i, 0)),
                      pl.BlockSpec((1, d, de), lambda i, be: (be[i], 0, 0)),
                      pl.BlockSpec((1, d, de), lambda i, be: (be[i], 0, 0)),
                      pl.BlockSpec((1, de, d), lambda i, be: (be[i], 0, 0))],
            out_specs=pl.BlockSpec((EXPERT_ROWS, d), lambda i, be: (i, 0))),
        compiler_params=_cparams(("arbitrary",)),
        name="experts",
    )(blk_expert, xs, wg.astype(BF16), wu.astype(BF16), wd.astype(BF16))


def _combine_kernel(dest_ref, ys_hbm, gate_ref, h_ref, u2_ref, ada_ref, wsg_ref, wsu_ref, wsd_ref, gf_ref,
                    o_ref, buf, sem, *, tc, d):
    def row_copy(t, k):
        return pltpu.make_async_copy(ys_hbm.at[pl.ds(dest_ref[t * TOP_K + k], 1)],
                                     buf.at[k, pl.ds(t, 1)], sem.at[0])

    def issue(t, _):
        for k in range(TOP_K):
            row_copy(t, k).start()
        return 0

    def drain(t, _):
        for k in range(TOP_K):
            row_copy(t, k).wait()
        return 0

    lax.fori_loop(0, tc, issue, 0)
    x = _bf(u2_ref[...])
    shared = _dot(_bf(jax.nn.silu(_dot(x, wsg_ref[...])) * _dot(x, wsu_ref[...])), wsd_ref[...])
    lax.fori_loop(0, tc, drain, 0)
    gate = gate_ref[...]
    routed = gate[:, 0:1] * buf[0]
    for k in range(1, TOP_K):
        routed = routed + gate[:, k:k + 1] * buf[k]
    gate2 = ada_ref[0, :, 5 * d:6 * d]
    h = h_ref[...] + gate2 * (routed + shared)
    o_ref[...] = _rms(h, gf_ref[...])


def _combine(dest_flat, ys, gate, h1, u2, ada3, wsg, wsu, wsd, gf, seq):
    n, d = h1.shape
    tc = min(COMBINE_ROWS, seq)
    per_b = seq // tc
    row = pl.BlockSpec((tc, d), lambda i: (i, 0))
    full = lambda a: pl.BlockSpec(a.shape, lambda i: (0,) * a.ndim)
    wsg, wsu, wsd = wsg.astype(BF16), wsu.astype(BF16), wsd.astype(BF16)
    gf = gf.reshape(1, d)
    return pl.pallas_call(
        functools.partial(_combine_kernel, tc=tc, d=d),
        out_shape=jax.ShapeDtypeStruct((n, d), F32),
        grid=(n // tc,),
        in_specs=[pl.BlockSpec((tc * TOP_K,), lambda i: (i,), memory_space=pltpu.SMEM),
                  pl.BlockSpec(memory_space=pl.ANY),
                  pl.BlockSpec((tc, TOP_K), lambda i: (i, 0)),
                  row, row,
                  pl.BlockSpec((1, 1, ada3.shape[2]), lambda i: (i // per_b, 0, 0)),
                  full(wsg), full(wsu), full(wsd), full(gf)],
        out_specs=row,
        scratch_shapes=[pltpu.VMEM((TOP_K, tc, d), F32), pltpu.SemaphoreType.DMA((1,))],
        compiler_params=_cparams(("arbitrary",)),
        name="combine",
    )(dest_flat, ys, gate, h1, u2, ada3, wsg, wsu, wsd, gf)


def _moe(h1, u2, ada3, w_router, router_bias, wg, wu, wd, wsg, wsu, wsd, gf):
    bsz, seq, d = h1.shape
    n = bsz * seq
    h1f = h1.reshape(n, d)
    u2f = u2.reshape(n, d)
    eidx_t, gate_t, rank_t, counts = _route(u2f, w_router, router_bias)
    cnt = counts[:, 0].astype(I32)
    padded = (cnt + EXPERT_ROWS - 1) // EXPERT_ROWS * EXPERT_ROWS
    pend = jnp.cumsum(padded)
    pstart = (pend - padded).astype(I32)
    nblk = (n * TOP_K + N_EXPERTS * (EXPERT_ROWS - 1) + EXPERT_ROWS - 1) // EXPERT_ROWS
    blk_row0 = jnp.arange(nblk, dtype=I32) * EXPERT_ROWS
    blk_expert = jnp.minimum(jnp.sum(pend[None, :] <= blk_row0[:, None], axis=1), N_EXPERTS - 1).astype(I32)
    dest_flat = _dest(pstart, eidx_t, rank_t).T.reshape(-1)
    xs = _dispatch(dest_flat, u2f, nblk * EXPERT_ROWS)
    ys = _experts(blk_expert, xs, wg, wu, wd)
    out = _combine(dest_flat, ys, gate_t.T, h1f, u2f, ada3, wsg, wsu, wsd, gf, seq)
    return out.reshape(bsz, seq, d)


def kernel(x, c, w_ada, b_ada, norm1_g, w_in, ssm_lambda_re, ssm_lambda_im, ssm_log_dt, ssm_b_re, ssm_b_im,
           ssm_c_re, ssm_c_im, ssm_d, ssm_w_glu, ssm_b_glu, w_proj_ssm, w_proj_attn, w_out, norm2_g, w_router,
           router_bias, w_exp_gate, w_exp_up, w_exp_down, w_sh_gate, w_sh_up, w_sh_down, norm_f_g):
    depth = w_ada.shape[0]
    assert depth == 1, "the final norm is fused into the last (only) layer's combine kernel"
    bsz, seq, d = x.shape
    layer = 0
    ada3 = _ada(c, w_ada[layer], b_ada[layer]).reshape(bsz, 1, 6 * d)
    us, q, qi, kkt, v, wi, g = _inproj(x, ada3, norm1_g[layer], w_in[layer])
    a_re, a_im, bb_re, bb_im = _s5disc(ssm_lambda_re[layer], ssm_lambda_im[layer], ssm_log_dt[layer],
                                       ssm_b_re[layer], ssm_b_im[layer])
    ys_t = _s5(us.transpose(1, 0, 2), a_re, a_im, bb_re, bb_im, ssm_c_re[layer], ssm_c_im[layer],
               ssm_d[layer], ssm_w_glu[layer], ssm_b_glu[layer])
    ya = _dsa(q, qi, wi, kkt, v)
    h1, u2 = _mix(x, ys_t.transpose(1, 0, 2), ya, g, ada3, w_proj_ssm[layer], w_proj_attn[layer],
                  w_out[layer], norm2_g[layer])
    return _moe(h1, u2, ada3, w_router[layer], router_bias[layer], w_exp_gate[layer], w_exp_up[layer],
                w_exp_down[layer], w_sh_gate[layer], w_sh_up[layer], w_sh_down[layer], norm_f_g)
```

```python
import functools
import math

import jax
import jax.numpy as jnp
import numpy as np
from jax import lax
from jax.experimental import pallas as pl
from jax.experimental.pallas import tpu as pltpu

F32 = jnp.float32
BF16 = jnp.bfloat16
I32 = jnp.int32

SSM_GROUP = 16
SSM_STATE = 64
N_HEADS = 8
HEAD_DIM = 64
IDX_HEADS = 8
IDX_DIM = 64
TOPK_MAX = 256
N_EXPERTS = 64
TOP_K = 8
N_GROUPS = 8
TOPK_GROUPS = 4
ROUTED_SCALE = 2.5
EPS = 1e-6

V7X_VMEM_LIMIT_BYTES = 56 * 1024 * 1024
LANES = 128
SUBLANES = 8

INPROJ_ROWS = 256
S5_STEPS = 64
S5_LANE_CHUNK = 128
DSA_Q_COLS = 256
DSA_K_ROWS = 512
DSA_COUNT_ROWS = 64
POS_SPLIT = 64
MIX_ROWS = 256
ROUTE_ROWS = 512
DEST_COLS = 4096
DISPATCH_ROWS = 512
DISPATCH_BATCH = 32
EXPERT_ROWS = 256
COMBINE_ROWS = 256

NEG_BIG = -1e30
INT_MIN = -(2 ** 31)


def _cparams(sem):
    return pltpu.CompilerParams(dimension_semantics=sem, vmem_limit_bytes=V7X_VMEM_LIMIT_BYTES)


def _bf(x):
    return x.astype(BF16)


def _dot(a, b):
    return jnp.dot(a, b, preferred_element_type=F32)


def _dot_nt(a, b):
    return lax.dot_general(a, b, (((1,), (1,)), ((), ())), preferred_element_type=F32)


def _split(x):
    hi = _bf(x)
    lo = _bf(x - hi.astype(F32))
    return hi, lo


def _dot3(a, b):
    ah, al = _split(a)
    bh, bl = _split(b)
    return _dot(ah, bh) + (_dot(ah, bl) + _dot(al, bh))


def _rms(x, g):
    return x * lax.rsqrt(jnp.mean(x * x, axis=-1, keepdims=True) + EPS) * g


def _ada_kernel(c_ref, w_ref, b_ref, o_ref):
    c = c_ref[...]
    o_ref[...] = _dot3(c * jax.nn.sigmoid(c), w_ref[...]) + b_ref[...]


def _ada(c, w, b):
    bsz, d = c.shape
    n = w.shape[1]
    tn = 1024
    return pl.pallas_call(
        _ada_kernel,
        out_shape=jax.ShapeDtypeStruct((bsz, n), F32),
        grid=(n // tn,),
        in_specs=[pl.BlockSpec((bsz, d), lambda j: (0, 0)),
                  pl.BlockSpec((d, tn), lambda j: (0, j)),
                  pl.BlockSpec((1, tn), lambda j: (0, j))],
        out_specs=pl.BlockSpec((bsz, tn), lambda j: (0, j)),
        compiler_params=_cparams(("arbitrary",)),
        name="ada",
    )(c, w, b.reshape(1, n))


def _inproj_kernel(x_ref, ada_ref, g1_ref, w_ref, wt_ref,
                   us_ref, q_ref, qi_ref, k_ref, ki_ref, vt_ref, wit_ref, g_ref, *, d, ssm_w, attn_w, idx_w):
    x = x_ref[0]
    shift = ada_ref[0, :, 0:d]
    scale = ada_ref[0, :, d:2 * d]
    u = _bf(_rms(x, g1_ref[...]) * (1.0 + scale) + shift)
    r = _dot(u, w_ref[...])
    o = 0
    us_ref[0] = r[:, o:o + ssm_w]
    o += ssm_w
    q_ref[0] = _bf(r[:, o:o + attn_w])
    o += attn_w
    qi_ref[0] = _bf(r[:, o:o + idx_w])
    o += idx_w
    k_ref[0] = _bf(r[:, o:o + HEAD_DIM])
    o += LANES
    ki_ref[0] = _bf(r[:, o:o + IDX_DIM])
    o += LANES
    g_ref[0] = r[:, o:o + 2 * d]
    rt = _dot_nt(wt_ref[...], u)
    vt_ref[0] = _bf(rt[0:HEAD_DIM])
    wit_ref[0] = rt[HEAD_DIM:HEAD_DIM + IDX_HEADS]


def _inproj(x, ada3, g1, w_in):
    bsz, seq, d = x.shape
    ssm_w = 512
    attn_w = N_HEADS * HEAD_DIM
    idx_w = IDX_HEADS * IDX_DIM
    sizes = (ssm_w, attn_w, HEAD_DIM, HEAD_DIM, idx_w, IDX_DIM, IDX_HEADS, d, d)
    offs = [0]
    for s in sizes:
        offs.append(offs[-1] + s)
    w_ssm, w_q, w_k, w_v, w_qi, w_ki, w_wi, w_gs, w_ga = [w_in[:, offs[i]:offs[i + 1]] for i in range(9)]
    zpad = lambda n: jnp.zeros((d, n), F32)
    wbig = jnp.concatenate([
        w_ssm, w_q * (HEAD_DIM ** -0.5), w_qi * (IDX_DIM ** -0.5),
        w_k, zpad(LANES - HEAD_DIM), w_ki, zpad(LANES - IDX_DIM), w_gs, w_ga], axis=1).astype(BF16)
    wt = jnp.concatenate([w_v, w_wi, zpad(LANES - HEAD_DIM - IDX_HEADS)], axis=1).T.astype(BF16)
    nw = wbig.shape[1]
    tl = INPROJ_ROWS
    kern = functools.partial(_inproj_kernel, d=d, ssm_w=ssm_w, attn_w=attn_w, idx_w=idx_w)
    row = lambda w: pl.BlockSpec((1, tl, w), lambda b, l: (b, l, 0))
    colt = lambda h: pl.BlockSpec((1, h, tl), lambda b, l: (b, 0, l))
    return pl.pallas_call(
        kern,
        out_shape=(jax.ShapeDtypeStruct((bsz, seq, ssm_w), F32),
                   jax.ShapeDtypeStruct((bsz, seq, attn_w), BF16),
                   jax.ShapeDtypeStruct((bsz, seq, idx_w), BF16),
                   jax.ShapeDtypeStruct((bsz, seq, HEAD_DIM), BF16),
                   jax.ShapeDtypeStruct((bsz, seq, IDX_DIM), BF16),
                   jax.ShapeDtypeStruct((bsz, HEAD_DIM, seq), BF16),
                   jax.ShapeDtypeStruct((bsz, IDX_HEADS, seq), F32),
                   jax.ShapeDtypeStruct((bsz, seq, 2 * d), F32)),
        grid=(bsz, seq // tl),
        in_specs=[row(d),
                  pl.BlockSpec((1, 1, ada3.shape[2]), lambda b, l: (b, 0, 0)),
                  pl.BlockSpec((1, d), lambda b, l: (0, 0)),
                  pl.BlockSpec((d, nw), lambda b, l: (0, 0)),
                  pl.BlockSpec((LANES, d), lambda b, l: (0, 0))],
        out_specs=(row(ssm_w), row(attn_w), row(idx_w), row(HEAD_DIM), row(IDX_DIM),
                   colt(HEAD_DIM), colt(IDX_HEADS), row(2 * d)),
        compiler_params=_cparams(("arbitrary", "arbitrary")),
        name="inproj",
    )(x, ada3, g1.reshape(1, d), wbig, wt)


def _s5disc_kernel(lr_ref, li_ref, ldt_ref, br_ref, bi_ref, are_ref, aim_ref, bbr_ref, bbi_ref):
    lr = lr_ref[...]
    li = li_ref[...]
    dt = jnp.exp(ldt_ref[...])
    mag = jnp.exp(lr * dt)
    a_re = mag * jnp.cos(li * dt)
    a_im = mag * jnp.sin(li * dt)
    den = lr * lr + li * li
    n_re = a_re - 1.0
    f_re = (n_re * lr + a_im * li) / den
    f_im = (a_im * lr - n_re * li) / den
    br = br_ref[...]
    bi = bi_ref[...]
    are_ref[...] = a_re
    aim_ref[...] = a_im
    bbr_ref[...] = f_re * br - f_im * bi
    bbi_ref[...] = f_re * bi + f_im * br


def _s5disc(lam_re, lam_im, log_dt, b_re, b_im):
    g, p = lam_re.shape
    h = b_re.shape[2]
    rep = lambda a: jnp.repeat(a, h, axis=1)
    ldt = jnp.broadcast_to(log_dt[:, None], (g, p * h))
    sds = jax.ShapeDtypeStruct((g, p * h), F32)
    a_re, a_im, bb_re, bb_im = pl.pallas_call(
        _s5disc_kernel, out_shape=(sds, sds, sds, sds), name="s5disc",
    )(rep(lam_re), rep(lam_im), ldt, b_re.reshape(g, p * h), b_im.reshape(g, p * h))
    return a_re[:, ::h], a_im[:, ::h], bb_re.reshape(g, p, h), bb_im.reshape(g, p, h)


def _s5_kernel(u_ref, wbh_ref, wbl_ref, ar_ref, ai_ref, cch_ref, ccl_ref, dsk_ref, wg_ref, bg_ref,
               o_ref, buf, hst, *, tl, width):
    nch = width // S5_LANE_CHUNK
    sw = S5_LANE_CHUNK // SSM_GROUP * SSM_STATE
    rows = tl * SUBLANES

    @pl.when(pl.program_id(0) == 0)
    def _():
        hst[...] = jnp.zeros_like(hst)

    u = u_ref[...].reshape(rows, width)
    uh, ul = _split(u)
    for j in range(nch):
        cs = slice(j * S5_LANE_CHUNK, (j + 1) * S5_LANE_CHUNK)
        buf[:, j * 2 * sw:(j + 1) * 2 * sw] = (
            _dot(uh[:, cs], wbh_ref[j]) + (_dot(uh[:, cs], wbl_ref[j]) + _dot(ul[:, cs], wbh_ref[j])))

    for j in range(nch):
        re_cols = slice(j * 2 * sw, j * 2 * sw + sw)
        im_cols = slice(j * 2 * sw + sw, (j + 1) * 2 * sw)
        a_re = jnp.broadcast_to(ar_ref[:, j * sw:(j + 1) * sw], (SUBLANES, sw))
        a_im = jnp.broadcast_to(ai_ref[:, j * sw:(j + 1) * sw], (SUBLANES, sw))

        def step(t, carry, re_cols=re_cols, im_cols=im_cols, a_re=a_re, a_im=a_im):
            h_re, h_im = carry
            r0 = pl.multiple_of(t * SUBLANES, SUBLANES)
            n_re = (a_re * h_re - a_im * h_im) + buf[pl.ds(r0, SUBLANES), re_cols]
            n_im = (a_re * h_im + a_im * h_re) + buf[pl.ds(r0, SUBLANES), im_cols]
            buf[pl.ds(r0, SUBLANES), re_cols] = n_re
            buf[pl.ds(r0, SUBLANES), im_cols] = n_im
            return n_re, n_im

        h_re, h_im = lax.fori_loop(0, tl, step, (hst[:, re_cols], hst[:, im_cols]), unroll=8)
        hst[:, re_cols] = h_re
        hst[:, im_cols] = h_im

    ys = []
    for j in range(nch):
        hh, hl = _split(buf[:, j * 2 * sw:(j + 1) * 2 * sw])
        ys.append(_dot(hh, cch_ref[j]) + (_dot(hh, ccl_ref[j]) + _dot(hl, cch_ref[j])))
    y = jnp.concatenate(ys, axis=1) + dsk_ref[...] * u
    y = jax.nn.gelu(y)
    y = y * jax.nn.sigmoid(_dot(_bf(y), wg_ref[...]) + bg_ref[...])
    o_ref[...] = _bf(y).reshape(tl, SUBLANES, width)


def _s5(u_t, a_re, a_im, bb_re, bb_im, c_re, c_im, d_skip, w_glu, b_glu):
    seq, bsz, width = u_t.shape
    assert bsz == SUBLANES
    nch = width // S5_LANE_CHUNK
    gpc = S5_LANE_CHUNK // SSM_GROUP
    sw = gpc * SSM_STATE
    eye = jnp.eye(gpc, dtype=F32)

    def bmat(bb):
        t = bb.reshape(nch, gpc, SSM_STATE, SSM_GROUP).transpose(0, 1, 3, 2)
        return jnp.einsum('jghp,gk->jghkp', t, eye).reshape(nch, S5_LANE_CHUNK, sw)

    def cmat(cc):
        t = cc.reshape(nch, gpc, SSM_GROUP, SSM_STATE).transpose(0, 1, 3, 2)
        return jnp.einsum('jgph,gk->jgpkh', t, eye).reshape(nch, sw, S5_LANE_CHUNK)

    wb = jnp.concatenate([bmat(bb_re), bmat(bb_im)], axis=2)
    cc = jnp.concatenate([cmat(c_re), -cmat(c_im)], axis=1)
    wbh = wb.astype(BF16)
    wbl = (wb - wbh.astype(F32)).astype(BF16)
    cch = cc.astype(BF16)
    ccl = (cc - cch.astype(F32)).astype(BF16)
    tl = S5_STEPS
    full = lambda a: pl.BlockSpec(a.shape, lambda i: (0,) * a.ndim)
    args = (u_t, wbh, wbl, a_re.reshape(1, -1), a_im.reshape(1, -1), cch, ccl,
            d_skip.reshape(1, width), w_glu.astype(BF16), b_glu.reshape(1, width))
    return pl.pallas_call(
        functools.partial(_s5_kernel, tl=tl, width=width),
        out_shape=jax.ShapeDtypeStruct((seq, bsz, width), BF16),
        grid=(seq // tl,),
        in_specs=[pl.BlockSpec((tl, bsz, width), lambda i: (i, 0, 0))] + [full(a) for a in args[1:]],
        out_specs=pl.BlockSpec((tl, bsz, width), lambda i: (i, 0, 0)),
        scratch_shapes=[pltpu.VMEM((tl * SUBLANES, nch * 2 * sw), F32),
                        pltpu.VMEM((SUBLANES, nch * 2 * sw), F32)],
        compiler_params=_cparams(("arbitrary",)),
        name="s5",
    )(*args)


def _dsa_kernel(qt_ref, qit_ref, wit_ref, ka_ref, ki_ref, vt_ref, o_ref, key_s, mb_s, acc_s, *, tq, tk, topk, seq):
    i = pl.program_id(1)
    q0 = i * tq
    nkt = (q0 + tq + tk - 1) // tk
    ch = DSA_COUNT_ROWS
    krow = lax.broadcasted_iota(I32, (tk, tq), 0)
    qcol = q0 + lax.broadcasted_iota(I32, (tk, tq), 1)
    crow = lax.broadcasted_iota(I32, (ch, tq), 0)

    wb = wit_ref[0] * (IDX_HEADS ** -0.5)

    def score_tile(j, _):
        r0 = pl.multiple_of(j * tk, tk)
        kit = ki_ref[0, pl.ds(r0, tk), :]
        acc = jnp.zeros((tk, tq), F32)
        for h in range(IDX_HEADS):
            s = _dot(kit, qit_ref[0, h * IDX_DIM:(h + 1) * IDX_DIM, :])
            acc = acc + wb[h:h + 1, :] * jnp.maximum(s, 0.0)
        bits = lax.bitcast_convert_type(acc, I32)
        key = jnp.where(bits < 0, bits ^ jnp.int32(0x7FFFFFFF), bits)
        key = jnp.where(acc == 0.0, 0, key)
        key_s[pl.ds(r0, tk), :] = jnp.where(krow + r0 <= qcol, key, INT_MIN)
        return 0

    lax.fori_loop(0, nkt, score_tile, 0)

    def count(pred):
        def tile(j, cnt):
            for c in range(tk // ch):
                rr = pl.multiple_of(j * tk + c * ch, ch)
                cnt = cnt + jnp.where(pred(key_s[pl.ds(rr, ch), :], rr), 1, 0)
            return cnt
        cnt = lax.fori_loop(0, nkt, tile, jnp.zeros((ch, tq), I32))
        return jnp.sum(cnt.astype(F32), axis=0, keepdims=True)

    def bit_step(b, ans_u):
        cand_u = ans_u | lax.shift_left(jnp.int32(1), 31 - b)
        cand_s = cand_u ^ INT_MIN
        return jnp.where(count(lambda kb, rr: kb >= cand_s) >= float(topk), cand_u, ans_u)

    ans_u = lax.fori_loop(0, 32, bit_step, jnp.zeros((1, tq), I32))
    thr = jnp.maximum(ans_u ^ INT_MIN, INT_MIN + 1)
    cnt_ge = count(lambda kb, rr: kb >= thr)
    has_ties = jnp.max(cnt_ge) > float(topk)

    def tie_cut():
        need = float(topk) - count(lambda kb, rr: kb > thr)
        nbits = max(1, (seq - 1).bit_length())

        def idx_step(b, x):
            cand = x | lax.shift_left(jnp.int32(1), nbits - 1 - b)
            below = count(lambda kb, rr: jnp.where(kb == thr, crow + rr, seq) < cand)
            return jnp.where(below < need, cand, x)

        x = lax.fori_loop(0, nbits, idx_step, jnp.zeros((1, tq), I32))
        return jnp.where(cnt_ge > float(topk), x, seq)

    cut = lax.cond(has_ties, tie_cut, lambda: jnp.full((1, tq), seq, I32))

    def bias_tile(j, _):
        for c in range(tk // ch):
            rr = pl.multiple_of(j * tk + c * ch, ch)
            kb = key_s[pl.ds(rr, ch), :]
            tie_bias = jnp.where(crow + rr <= cut, 0.0, NEG_BIG)
            mb_s[pl.ds(rr, ch), :] = jnp.where(kb > thr, 0.0, jnp.where(kb == thr, tie_bias, NEG_BIG))
        return 0

    lax.fori_loop(0, nkt, bias_tile, 0)

    def logits(j, h):
        r0 = pl.multiple_of(j * tk, tk)
        s = _dot(ka_ref[0, pl.ds(r0, tk), :], qt_ref[0, h * LANES:(h + 1) * LANES, :]) + mb_s[pl.ds(r0, tk), :]
        return s.reshape(tk // SUBLANES, SUBLANES, tq)

    def max_tile(j, ms):
        return tuple(jnp.maximum(ms[h], jnp.max(logits(j, h), axis=0)) for h in range(N_HEADS))

    ms = lax.fori_loop(0, nkt, max_tile, (jnp.full((SUBLANES, tq), NEG_BIG, F32),) * N_HEADS)
    m = [jnp.max(mh, axis=0, keepdims=True) for mh in ms]
    acc_s[...] = jnp.zeros(acc_s.shape, F32)

    def sum_tile(j, ls):
        r0 = pl.multiple_of(j * tk, tk)
        out = []
        for h in range(N_HEADS):
            p = jnp.exp(logits(j, h) - m[h])
            out.append(ls[h] + jnp.sum(p, axis=0))
            rows = slice(h * HEAD_DIM, (h + 1) * HEAD_DIM)
            acc_s[rows, :] = acc_s[rows, :] + _dot(vt_ref[0, :, pl.ds(r0, tk)], _bf(p.reshape(tk, tq)))
        return tuple(out)

    ls = lax.fori_loop(0, nkt, sum_tile, (jnp.zeros((SUBLANES, tq), F32),) * N_HEADS)
    for h in range(N_HEADS):
        rows = slice(h * HEAD_DIM, (h + 1) * HEAD_DIM)
        o_ref[0, rows, :] = _bf(acc_s[rows, :] / jnp.sum(ls[h], axis=0, keepdims=True))


def _dsa(q, qi, wit, k, ki, vt):
    bsz, seq, aw = q.shape
    tq = min(DSA_Q_COLS, seq)
    tk = min(DSA_K_ROWS, seq)
    topk = min(TOPK_MAX, seq // 4)
    assert (seq - 1) // POS_SPLIT < 256 and POS_SPLIT <= 256, "key positions must split into two bf16-exact parts"
    slopes = [2.0 ** (-8.0 * (h + 1) / N_HEADS) for h in range(N_HEADS)]
    assert all(float(np.float32(sl).astype(BF16)) == sl for sl in slopes), "ALiBi slopes must be bf16-exact"
    pos = jnp.arange(seq, dtype=I32)
    posc = jnp.stack([(pos // POS_SPLIT) * POS_SPLIT, pos % POS_SPLIT], axis=1).astype(BF16)
    ka = jnp.concatenate([k, jnp.broadcast_to(posc[None], (bsz, seq, 2)),
                          jnp.zeros((bsz, seq, LANES - HEAD_DIM - 2), BF16)], axis=2)
    qh = q.reshape(bsz, seq, N_HEADS, HEAD_DIM).transpose(0, 2, 3, 1)
    srow = jnp.asarray(slopes, BF16)[None, :, None, None]
    extra = jnp.concatenate([jnp.broadcast_to(srow, (bsz, N_HEADS, 2, seq)),
                             jnp.zeros((bsz, N_HEADS, LANES - HEAD_DIM - 2, seq), BF16)], axis=2)
    qt = jnp.concatenate([qh, extra], axis=2).reshape(bsz, N_HEADS * LANES, seq)
    qit = qi.transpose(0, 2, 1)
    kern = functools.partial(_dsa_kernel, tq=tq, tk=tk, topk=topk, seq=seq)
    cols = lambda r: pl.BlockSpec((1, r, tq), lambda b, i: (b, 0, i))
    return pl.pallas_call(
        kern,
        out_shape=jax.ShapeDtypeStruct((bsz, aw, seq), BF16),
        grid=(bsz, seq // tq),
        in_specs=[cols(N_HEADS * LANES), cols(qit.shape[1]), cols(IDX_HEADS),
                  pl.BlockSpec((1, seq, LANES), lambda b, i: (b, 0, 0)),
                  pl.BlockSpec((1, seq, IDX_DIM), lambda b, i: (b, 0, 0)),
                  pl.BlockSpec((1, HEAD_DIM, seq), lambda b, i: (b, 0, 0))],
        out_specs=cols(aw),
        scratch_shapes=[pltpu.VMEM((seq, tq), I32), pltpu.VMEM((seq, tq), F32), pltpu.VMEM((aw, tq), F32)],
        compiler_params=_cparams(("arbitrary", "arbitrary")),
        name="dsa",
    )(qt, qit, wit, ka, ki, vt)


def _mix_kernel(x_ref, ys_ref, ya_ref, g_ref, ada_ref, wps_ref, wpa_ref, wo_ref, g2_ref, h_ref, u2_ref, *, d):
    gate1 = ada_ref[0, :, 2 * d:3 * d]
    shift2 = ada_ref[0, :, 3 * d:4 * d]
    scale2 = ada_ref[0, :, 4 * d:5 * d]
    g = g_ref[0]
    mixed = (jax.nn.sigmoid(g[:, 0:d]) * _dot(ys_ref[0], wps_ref[...])
             + jax.nn.sigmoid(g[:, d:2 * d]) * _dot(ya_ref[0], wpa_ref[...]))
    h = x_ref[0] + gate1 * _dot(_bf(mixed), wo_ref[...])
    h_ref[0] = h
    u2_ref[0] = _rms(h, g2_ref[...]) * (1.0 + scale2) + shift2


def _mix(x, ys, ya, g, ada3, wps, wpa, wo, g2):
    bsz, seq, d = x.shape
    tm = MIX_ROWS
    row = lambda w: pl.BlockSpec((1, tm, w), lambda b, l: (b, l, 0))
    full = lambda a: pl.BlockSpec(a.shape, lambda b, l: (0,) * a.ndim)
    wps, wpa, wo = wps.astype(BF16), wpa.astype(BF16), wo.astype(BF16)
    g2 = g2.reshape(1, d)
    return pl.pallas_call(
        functools.partial(_mix_kernel, d=d),
        out_shape=(jax.ShapeDtypeStruct((bsz, seq, d), F32), jax.ShapeDtypeStruct((bsz, seq, d), F32)),
        grid=(bsz, seq // tm),
        in_specs=[row(d), row(ys.shape[2]), row(ya.shape[2]), row(2 * d),
                  pl.BlockSpec((1, 1, ada3.shape[2]), lambda b, l: (b, 0, 0)),
                  full(wps), full(wpa), full(wo), full(g2)],
        out_specs=(row(d), row(d)),
        compiler_params=_cparams(("arbitrary", "arbitrary")),
        name="mix",
    )(x, ys, ya, g, ada3, wps, wpa, wo, g2)


def _first_max(cur, idx, axis, big):
    m = jnp.max(cur, axis=axis, keepdims=True)
    first = jnp.min(jnp.where(cur == m, idx, big), axis=axis, keepdims=True)
    return m, idx == first


def _route_kernel(u_ref, wrh_ref, wrl_ref, rb_ref, tri_ref, e_ref, gt_ref, rk_ref, cnt_ref, run_s, *, t):
    @pl.when(pl.program_id(0) == 0)
    def _():
        run_s[...] = jnp.zeros_like(run_s)

    uh, ul = _split(u_ref[...])
    logits = _dot_nt(wrh_ref[...], uh) + (_dot_nt(wrl_ref[...], uh) + _dot_nt(wrh_ref[...], ul))
    scores = jax.nn.sigmoid(logits)
    biased = scores + rb_ref[...]
    per_group = N_EXPERTS // N_GROUPS
    b3 = biased.reshape(N_GROUPS, per_group, t)
    i3 = lax.broadcasted_iota(I32, b3.shape, 1)
    m1, hit1 = _first_max(b3, i3, 1, per_group)
    m2 = jnp.max(jnp.where(hit1, -jnp.inf, b3), axis=1, keepdims=True)
    gs = (m1 + m2).reshape(N_GROUPS, t)
    gi = lax.broadcasted_iota(I32, gs.shape, 0)
    gsel = jnp.zeros(gs.shape, F32)
    for _ in range(TOPK_GROUPS):
        _, hit = _first_max(gs, gi, 0, N_GROUPS)
        gsel = jnp.where(hit, 1.0, gsel)
        gs = jnp.where(hit, -jnp.inf, gs)
    cur = jnp.where(gsel.reshape(N_GROUPS, 1, t) > 0.0, b3, -jnp.inf).reshape(N_EXPERTS, t)
    ei = lax.broadcasted_iota(I32, cur.shape, 0)
    hits = []
    gates = []
    for _ in range(TOP_K):
        _, hit = _first_max(cur, ei, 0, N_EXPERTS)
        hits.append(hit)
        gates.append(jnp.sum(jnp.where(hit, scores, 0.0), axis=0, keepdims=True))
        cur = jnp.where(hit, -jnp.inf, cur)
    gate = jnp.concatenate(gates, axis=0)
    gt_ref[...] = gate / jnp.sum(gate, axis=0, keepdims=True) * ROUTED_SCALE
    e_ref[...] = jnp.concatenate(
        [jnp.sum(jnp.where(hit, ei, 0), axis=0, keepdims=True) for hit in hits], axis=0)
    onehot = jnp.zeros(cur.shape, F32)
    for hit in hits:
        onehot = jnp.where(hit, 1.0, onehot)
    before = _dot(_bf(onehot), tri_ref[...]) + run_s[:, 0:1]
    rk_ref[...] = jnp.concatenate(
        [jnp.sum(jnp.where(hit, before, 0.0), axis=0, keepdims=True) for hit in hits], axis=0).astype(I32)
    run_s[...] = run_s[...] + jnp.sum(onehot, axis=1, keepdims=True)
    cnt_ref[...] = run_s[...]


def _route(u2, w_router, router_bias):
    n, d = u2.shape
    t = min(ROUTE_ROWS, n)
    wt = w_router.T
    wrh = wt.astype(BF16)
    wrl = (wt - wrh.astype(F32)).astype(BF16)
    tri = (jnp.arange(t)[:, None] < jnp.arange(t)[None, :]).astype(BF16)
    full = lambda a: pl.BlockSpec(a.shape, lambda i: (0,) * a.ndim)
    col = pl.BlockSpec((TOP_K, t), lambda i: (0, i))
    rb = router_bias.reshape(N_EXPERTS, 1)
    return pl.pallas_call(
        functools.partial(_route_kernel, t=t),
        out_shape=(jax.ShapeDtypeStruct((TOP_K, n), I32), jax.ShapeDtypeStruct((TOP_K, n), F32),
                   jax.ShapeDtypeStruct((TOP_K, n), I32), jax.ShapeDtypeStruct((N_EXPERTS, LANES), F32)),
        grid=(n // t,),
        in_specs=[pl.BlockSpec((t, d), lambda i: (i, 0)), full(wrh), full(wrl), full(rb), full(tri)],
        out_specs=(col, col, col, pl.BlockSpec((N_EXPERTS, LANES), lambda i: (0, 0))),
        scratch_shapes=[pltpu.VMEM((N_EXPERTS, LANES), F32)],
        compiler_params=_cparams(("arbitrary",)),
        name="route",
    )(u2, wrh, wrl, rb, tri)


def _dest_kernel(ps_ref, e_ref, rk_ref, o_ref):
    e = e_ref[...]
    acc = rk_ref[...]
    for x in range(N_EXPERTS):
        acc = acc + jnp.where(e == x, ps_ref[x], 0)
    o_ref[...] = acc


def _dest(pstart, eidx_t, rank_t):
    k, n = eidx_t.shape
    tn = min(DEST_COLS, n)
    blk = pl.BlockSpec((k, tn), lambda i, ps: (0, i))
    return pl.pallas_call(
        _dest_kernel,
        out_shape=jax.ShapeDtypeStruct((k, n), I32),
        grid_spec=pltpu.PrefetchScalarGridSpec(
            num_scalar_prefetch=1, grid=(n // tn,), in_specs=[blk, blk], out_specs=blk),
        compiler_params=_cparams(("arbitrary",)),
        name="dest",
    )(pstart, eidx_t, rank_t)


def _dispatch_kernel(dest_ref, u_ref, xs_in, xs_hbm, sems, *, td):
    del xs_in
    nb = td // DISPATCH_BATCH

    def row_copy(t, k, slot):
        return pltpu.make_async_copy(u_ref.at[pl.ds(t, 1)],
                                     xs_hbm.at[pl.ds(dest_ref[t * TOP_K + k], 1)], sems.at[slot])

    def issue(bi, slot):
        def body(tt, _):
            for k in range(TOP_K):
                row_copy(bi * DISPATCH_BATCH + tt, k, slot).start()
            return 0
        lax.fori_loop(0, DISPATCH_BATCH, body, 0)

    def drain(bi, slot):
        def body(tt, _):
            for k in range(TOP_K):
                row_copy(bi * DISPATCH_BATCH + tt, k, slot).wait()
            return 0
        lax.fori_loop(0, DISPATCH_BATCH, body, 0)

    issue(0, 0)
    for bi in range(1, nb):
        issue(bi, bi % 2)
        drain(bi - 1, (bi - 1) % 2)
    drain(nb - 1, (nb - 1) % 2)


def _dispatch(dest_flat, u2, n_rows):
    n, d = u2.shape
    td = min(DISPATCH_ROWS, n)
    xs0 = jnp.zeros((n_rows, d), F32)
    return pl.pallas_call(
        functools.partial(_dispatch_kernel, td=td),
        out_shape=jax.ShapeDtypeStruct((n_rows, d), F32),
        grid=(n // td,),
        in_specs=[pl.BlockSpec((td * TOP_K,), lambda i: (i,), memory_space=pltpu.SMEM),
                  pl.BlockSpec((td, d), lambda i: (i, 0)),
                  pl.BlockSpec(memory_space=pl.ANY)],
        out_specs=pl.BlockSpec(memory_space=pl.ANY),
        scratch_shapes=[pltpu.SemaphoreType.DMA((2,))],
        input_output_aliases={2: 0},
        compiler_params=_cparams(("arbitrary",)),
        name="dispatch",
    )(dest_flat, u2, xs0)


def _experts_kernel(be_ref, xs_ref, wg_ref, wu_ref, wd_ref, ys_ref):
    del be_ref
    x = _bf(xs_ref[...])
    hmid = jax.nn.silu(_dot(x, wg_ref[0])) * _dot(x, wu_ref[0])
    ys_ref[...] = _dot(_bf(hmid), wd_ref[0])


def _experts(blk_expert, xs, wg, wu, wd):
    rows, d = xs.shape
    de = wg.shape[2]
    nblk = rows // EXPERT_ROWS
    return pl.pallas_call(
        _experts_kernel,
        out_shape=jax.ShapeDtypeStruct((rows, d), F32),
        grid_spec=pltpu.PrefetchScalarGridSpec(
            num_scalar_prefetch=1, grid=(nblk,),
            in_specs=[pl.BlockSpec((EXPERT_ROWS, d), lambda i, be: (i, 0)),
                      pl.BlockSpec((1, d, de), lambda i, be: (be[i], 0, 0)),
                      pl.BlockSpec((1, d, de), lambda i, be: (be[i], 0, 0)),
                      pl.BlockSpec((1, de, d), lambda i, be: (be[i], 0, 0))],
            out_specs=pl.BlockSpec((EXPERT_ROWS, d), lambda i, be: (i, 0))),
        compiler_params=_cparams(("arbitrary",)),
        name="experts",
    )(blk_expert, xs, wg.astype(BF16), wu.astype(BF16), wd.astype(BF16))


def _combine_kernel(dest_ref, ys_hbm, gate_ref, h_ref, u2_ref, ada_ref, wsg_ref, wsu_ref, wsd_ref, gf_ref,
                    o_ref, buf, sem, *, tc, d):
    def row_copy(t, k):
        return pltpu.make_async_copy(ys_hbm.at[pl.ds(dest_ref[t * TOP_K + k], 1)],
                                     buf.at[k, pl.ds(t, 1)], sem.at[0])

    def issue(t, _):
        for k in range(TOP_K):
            row_copy(t, k).start()
        return 0

    def drain(t, _):
        for k in range(TOP_K):
            row_copy(t, k).wait()
        return 0

    lax.fori_loop(0, tc, issue, 0)
    x = _bf(u2_ref[...])
    shared = _dot(_bf(jax.nn.silu(_dot(x, wsg_ref[...])) * _dot(x, wsu_ref[...])), wsd_ref[...])
    lax.fori_loop(0, tc, drain, 0)
    gate = gate_ref[...]
    routed = gate[:, 0:1] * buf[0]
    for k in range(1, TOP_K):
        routed = routed + gate[:, k:k + 1] * buf[k]
    gate2 = ada_ref[0, :, 5 * d:6 * d]
    h = h_ref[...] + gate2 * (routed + shared)
    o_ref[...] = _rms(h, gf_ref[...])


def _combine(dest_flat, ys, gate, h1, u2, ada3, wsg, wsu, wsd, gf, seq):
    n, d = h1.shape
    tc = min(COMBINE_ROWS, seq)
    per_b = seq // tc
    row = pl.BlockSpec((tc, d), lambda i: (i, 0))
    full = lambda a: pl.BlockSpec(a.shape, lambda i: (0,) * a.ndim)
    wsg, wsu, wsd = wsg.astype(BF16), wsu.astype(BF16), wsd.astype(BF16)
    gf = gf.reshape(1, d)
    return pl.pallas_call(
        functools.partial(_combine_kernel, tc=tc, d=d),
        out_shape=jax.ShapeDtypeStruct((n, d), F32),
        grid=(n // tc,),
        in_specs=[pl.BlockSpec((tc * TOP_K,), lambda i: (i,), memory_space=pltpu.SMEM),
                  pl.BlockSpec(memory_space=pl.ANY),
                  pl.BlockSpec((tc, TOP_K), lambda i: (i, 0)),
                  row, row,
                  pl.BlockSpec((1, 1, ada3.shape[2]), lambda i: (i // per_b, 0, 0)),
                  full(wsg), full(wsu), full(wsd), full(gf)],
        out_specs=row,
        scratch_shapes=[pltpu.VMEM((TOP_K, tc, d), F32), pltpu.SemaphoreType.DMA((1,))],
        compiler_params=_cparams(("arbitrary",)),
        name="combine",
    )(dest_flat, ys, gate, h1, u2, ada3, wsg, wsu, wsd, gf)


def _moe(h1, u2, ada3, w_router, router_bias, wg, wu, wd, wsg, wsu, wsd, gf):
    bsz, seq, d = h1.shape
    n = bsz * seq
    h1f = h1.reshape(n, d)
    u2f = u2.reshape(n, d)
    eidx_t, gate_t, rank_t, counts = _route(u2f, w_router, router_bias)
    cnt = counts[:, 0].astype(I32)
    padded = (cnt + EXPERT_ROWS - 1) // EXPERT_ROWS * EXPERT_ROWS
    pend = jnp.cumsum(padded)
    pstart = (pend - padded).astype(I32)
    nblk = (n * TOP_K + N_EXPERTS * (EXPERT_ROWS - 1) + EXPERT_ROWS - 1) // EXPERT_ROWS
    blk_row0 = jnp.arange(nblk, dtype=I32) * EXPERT_ROWS
    blk_expert = jnp.minimum(jnp.sum(pend[None, :] <= blk_row0[:, None], axis=1), N_EXPERTS - 1).astype(I32)
    dest_flat = _dest(pstart, eidx_t, rank_t).T.reshape(-1)
    xs = _dispatch(dest_flat, u2f, nblk * EXPERT_ROWS)
    ys = _experts(blk_expert, xs, wg, wu, wd)
    out = _combine(dest_flat, ys, gate_t.T, h1f, u2f, ada3, wsg, wsu, wsd, gf, seq)
    return out.reshape(bsz, seq, d)


def kernel(x, c, w_ada, b_ada, norm1_g, w_in, ssm_lambda_re, ssm_lambda_im, ssm_log_dt, ssm_b_re, ssm_b_im,
           ssm_c_re, ssm_c_im, ssm_d, ssm_w_glu, ssm_b_glu, w_proj_ssm, w_proj_attn, w_out, norm2_g, w_router,
           router_bias, w_exp_gate, w_exp_up, w_exp_down, w_sh_gate, w_sh_up, w_sh_down, norm_f_g):
    depth = w_ada.shape[0]
    assert depth == 1, "the final norm is fused into the last (only) layer's combine kernel"
    bsz, seq, d = x.shape
    layer = 0
    ada3 = _ada(c, w_ada[layer], b_ada[layer]).reshape(bsz, 1, 6 * d)
    us, q, qi, k, ki, vt, wit, g = _inproj(x, ada3, norm1_g[layer], w_in[layer])
    a_re, a_im, bb_re, bb_im = _s5disc(ssm_lambda_re[layer], ssm_lambda_im[layer], ssm_log_dt[layer],
                                       ssm_b_re[layer], ssm_b_im[layer])
    ys_t = _s5(us.transpose(1, 0, 2), a_re, a_im, bb_re, bb_im, ssm_c_re[layer], ssm_c_im[layer],
               ssm_d[layer], ssm_w_glu[layer], ssm_b_glu[layer])
    ya = _dsa(q, qi, wit, k, ki, vt).transpose(0, 2, 1)
    h1, u2 = _mix(x, ys_t.transpose(1, 0, 2), ya, g, ada3, w_proj_ssm[layer], w_proj_attn[layer],
                  w_out[layer], norm2_g[layer])
    return _moe(h1, u2, ada3, w_router[layer], router_bias[layer], w_exp_gate[layer], w_exp_up[layer],
                w_exp_down[layer], w_sh_gate[layer], w_sh_up[layer], w_sh_down[layer], norm_f_g)
```

```python
import functools
import math

import jax
import jax.numpy as jnp
import numpy as np
from jax import lax
from jax.experimental import pallas as pl
from jax.experimental.pallas import tpu as pltpu

F32 = jnp.float32
BF16 = jnp.bfloat16
I32 = jnp.int32

SSM_GROUP = 16
SSM_STATE = 64
N_HEADS = 8
HEAD_DIM = 64
IDX_HEADS = 8
IDX_DIM = 64
TOPK_MAX = 256
N_EXPERTS = 64
TOP_K = 8
N_GROUPS = 8
TOPK_GROUPS = 4
ROUTED_SCALE = 2.5
EPS = 1e-6

V7X_VMEM_LIMIT_BYTES = 56 * 1024 * 1024
LANES = 128
SUBLANES = 8

INPROJ_ROWS = 256
S5_STEPS = 64
S5_LANE_CHUNK = 128
DSA_Q_COLS = 256
DSA_K_ROWS = 512
DSA_COUNT_ROWS = 64
BITSLICE_ROWS = 256
POS_SPLIT = 64
MIX_ROWS = 256
ROUTE_ROWS = 512
DEST_COLS = 4096
DISPATCH_ROWS = 512
DISPATCH_BATCH = 32
EXPERT_ROWS = 256
COMBINE_ROWS = 256

NEG_BIG = -1e30
INT_MIN = -(2 ** 31)


def _cparams(sem):
    return pltpu.CompilerParams(dimension_semantics=sem, vmem_limit_bytes=V7X_VMEM_LIMIT_BYTES)


def _bf(x):
    return x.astype(BF16)


def _dot(a, b):
    return jnp.dot(a, b, preferred_element_type=F32)


def _dot_nt(a, b):
    return lax.dot_general(a, b, (((1,), (1,)), ((), ())), preferred_element_type=F32)


def _split(x):
    hi = _bf(x)
    lo = _bf(x - hi.astype(F32))
    return hi, lo


def _dot3(a, b):
    ah, al = _split(a)
    bh, bl = _split(b)
    return _dot(ah, bh) + (_dot(ah, bl) + _dot(al, bh))


def _rms(x, g):
    return x * lax.rsqrt(jnp.mean(x * x, axis=-1, keepdims=True) + EPS) * g


def _ada_kernel(c_ref, w_ref, b_ref, o_ref):
    c = c_ref[...]
    o_ref[...] = _dot3(c * jax.nn.sigmoid(c), w_ref[...]) + b_ref[...]


def _ada(c, w, b):
    bsz, d = c.shape
    n = w.shape[1]
    tn = 1024
    return pl.pallas_call(
        _ada_kernel,
        out_shape=jax.ShapeDtypeStruct((bsz, n), F32),
        grid=(n // tn,),
        in_specs=[pl.BlockSpec((bsz, d), lambda j: (0, 0)),
                  pl.BlockSpec((d, tn), lambda j: (0, j)),
                  pl.BlockSpec((1, tn), lambda j: (0, j))],
        out_specs=pl.BlockSpec((bsz, tn), lambda j: (0, j)),
        compiler_params=_cparams(("arbitrary",)),
        name="ada",
    )(c, w, b.reshape(1, n))


def _inproj_kernel(x_ref, ada_ref, g1_ref, w_ref, wt_ref,
                   us_ref, q_ref, qi_ref, k_ref, ki_ref, vt_ref, wit_ref, g_ref, *, d, ssm_w, attn_w, idx_w):
    x = x_ref[0]
    shift = ada_ref[0, :, 0:d]
    scale = ada_ref[0, :, d:2 * d]
    u = _bf(_rms(x, g1_ref[...]) * (1.0 + scale) + shift)
    r = _dot(u, w_ref[...])
    o = 0
    us_ref[0] = r[:, o:o + ssm_w]
    o += ssm_w
    q_ref[0] = _bf(r[:, o:o + attn_w])
    o += attn_w
    qi_ref[0] = _bf(r[:, o:o + idx_w])
    o += idx_w
    k_ref[0] = _bf(r[:, o:o + HEAD_DIM])
    o += LANES
    ki_ref[0] = _bf(r[:, o:o + IDX_DIM])
    o += LANES
    g_ref[0] = r[:, o:o + 2 * d]
    rt = _dot_nt(wt_ref[...], u)
    vt_ref[0] = _bf(rt[0:HEAD_DIM])
    wit_ref[0] = rt[HEAD_DIM:HEAD_DIM + IDX_HEADS]


def _inproj(x, ada3, g1, w_in):
    bsz, seq, d = x.shape
    ssm_w = 512
    attn_w = N_HEADS * HEAD_DIM
    idx_w = IDX_HEADS * IDX_DIM
    sizes = (ssm_w, attn_w, HEAD_DIM, HEAD_DIM, idx_w, IDX_DIM, IDX_HEADS, d, d)
    offs = [0]
    for s in sizes:
        offs.append(offs[-1] + s)
    w_ssm, w_q, w_k, w_v, w_qi, w_ki, w_wi, w_gs, w_ga = [w_in[:, offs[i]:offs[i + 1]] for i in range(9)]
    zpad = lambda n: jnp.zeros((d, n), F32)
    wbig = jnp.concatenate([
        w_ssm, w_q * (HEAD_DIM ** -0.5), w_qi * (IDX_DIM ** -0.5),
        w_k, zpad(LANES - HEAD_DIM), w_ki, zpad(LANES - IDX_DIM), w_gs, w_ga], axis=1).astype(BF16)
    wt = jnp.concatenate([w_v, w_wi, zpad(LANES - HEAD_DIM - IDX_HEADS)], axis=1).T.astype(BF16)
    nw = wbig.shape[1]
    tl = INPROJ_ROWS
    kern = functools.partial(_inproj_kernel, d=d, ssm_w=ssm_w, attn_w=attn_w, idx_w=idx_w)
    row = lambda w: pl.BlockSpec((1, tl, w), lambda b, l: (b, l, 0))
    colt = lambda h: pl.BlockSpec((1, h, tl), lambda b, l: (b, 0, l))
    return pl.pallas_call(
        kern,
        out_shape=(jax.ShapeDtypeStruct((bsz, seq, ssm_w), F32),
                   jax.ShapeDtypeStruct((bsz, seq, attn_w), BF16),
                   jax.ShapeDtypeStruct((bsz, seq, idx_w), BF16),
                   jax.ShapeDtypeStruct((bsz, seq, HEAD_DIM), BF16),
                   jax.ShapeDtypeStruct((bsz, seq, IDX_DIM), BF16),
                   jax.ShapeDtypeStruct((bsz, HEAD_DIM, seq), BF16),
                   jax.ShapeDtypeStruct((bsz, IDX_HEADS, seq), F32),
                   jax.ShapeDtypeStruct((bsz, seq, 2 * d), F32)),
        grid=(bsz, seq // tl),
        in_specs=[row(d),
                  pl.BlockSpec((1, 1, ada3.shape[2]), lambda b, l: (b, 0, 0)),
                  pl.BlockSpec((1, d), lambda b, l: (0, 0)),
                  pl.BlockSpec((d, nw), lambda b, l: (0, 0)),
                  pl.BlockSpec((LANES, d), lambda b, l: (0, 0))],
        out_specs=(row(ssm_w), row(attn_w), row(idx_w), row(HEAD_DIM), row(IDX_DIM),
                   colt(HEAD_DIM), colt(IDX_HEADS), row(2 * d)),
        compiler_params=_cparams(("arbitrary", "arbitrary")),
        name="inproj",
    )(x, ada3, g1.reshape(1, d), wbig, wt)


def _s5disc_kernel(lr_ref, li_ref, ldt_ref, br_ref, bi_ref, are_ref, aim_ref, bbr_ref, bbi_ref):
    lr = lr_ref[...]
    li = li_ref[...]
    dt = jnp.exp(ldt_ref[...])
    mag = jnp.exp(lr * dt)
    a_re = mag * jnp.cos(li * dt)
    a_im = mag * jnp.sin(li * dt)
    den = lr * lr + li * li
    n_re = a_re - 1.0
    f_re = (n_re * lr + a_im * li) / den
    f_im = (a_im * lr - n_re * li) / den
    br = br_ref[...]
    bi = bi_ref[...]
    are_ref[...] = a_re
    aim_ref[...] = a_im
    bbr_ref[...] = f_re * br - f_im * bi
    bbi_ref[...] = f_re * bi + f_im * br


def _s5disc(lam_re, lam_im, log_dt, b_re, b_im):
    g, p = lam_re.shape
    h = b_re.shape[2]
    rep = lambda a: jnp.repeat(a, h, axis=1)
    ldt = jnp.broadcast_to(log_dt[:, None], (g, p * h))
    sds = jax.ShapeDtypeStruct((g, p * h), F32)
    a_re, a_im, bb_re, bb_im = pl.pallas_call(
        _s5disc_kernel, out_shape=(sds, sds, sds, sds), name="s5disc",
    )(rep(lam_re), rep(lam_im), ldt, b_re.reshape(g, p * h), b_im.reshape(g, p * h))
    return a_re[:, ::h], a_im[:, ::h], bb_re.reshape(g, p, h), bb_im.reshape(g, p, h)


def _s5_kernel(u_ref, wbh_ref, wbl_ref, ar_ref, ai_ref, cch_ref, ccl_ref, dsk_ref, wg_ref, bg_ref,
               o_ref, buf, hst, *, tl, width):
    nch = width // S5_LANE_CHUNK
    sw = S5_LANE_CHUNK // SSM_GROUP * SSM_STATE
    rows = tl * SUBLANES

    @pl.when(pl.program_id(0) == 0)
    def _():
        hst[...] = jnp.zeros_like(hst)

    u = u_ref[...].reshape(rows, width)
    uh, ul = _split(u)
    for j in range(nch):
        cs = slice(j * S5_LANE_CHUNK, (j + 1) * S5_LANE_CHUNK)
        buf[:, j * 2 * sw:(j + 1) * 2 * sw] = (
            _dot(uh[:, cs], wbh_ref[j]) + (_dot(uh[:, cs], wbl_ref[j]) + _dot(ul[:, cs], wbh_ref[j])))

    for j in range(nch):
        re_cols = slice(j * 2 * sw, j * 2 * sw + sw)
        im_cols = slice(j * 2 * sw + sw, (j + 1) * 2 * sw)
        a_re = jnp.broadcast_to(ar_ref[:, j * sw:(j + 1) * sw], (SUBLANES, sw))
        a_im = jnp.broadcast_to(ai_ref[:, j * sw:(j + 1) * sw], (SUBLANES, sw))

        def step(t, carry, re_cols=re_cols, im_cols=im_cols, a_re=a_re, a_im=a_im):
            h_re, h_im = carry
            r0 = pl.multiple_of(t * SUBLANES, SUBLANES)
            n_re = (a_re * h_re - a_im * h_im) + buf[pl.ds(r0, SUBLANES), re_cols]
            n_im = (a_re * h_im + a_im * h_re) + buf[pl.ds(r0, SUBLANES), im_cols]
            buf[pl.ds(r0, SUBLANES), re_cols] = n_re
            buf[pl.ds(r0, SUBLANES), im_cols] = n_im
            return n_re, n_im

        h_re, h_im = lax.fori_loop(0, tl, step, (hst[:, re_cols], hst[:, im_cols]), unroll=8)
        hst[:, re_cols] = h_re
        hst[:, im_cols] = h_im

    ys = []
    for j in range(nch):
        hh, hl = _split(buf[:, j * 2 * sw:(j + 1) * 2 * sw])
        ys.append(_dot(hh, cch_ref[j]) + (_dot(hh, ccl_ref[j]) + _dot(hl, cch_ref[j])))
    y = jnp.concatenate(ys, axis=1) + dsk_ref[...] * u
    y = jax.nn.gelu(y)
    y = y * jax.nn.sigmoid(_dot(_bf(y), wg_ref[...]) + bg_ref[...])
    o_ref[...] = _bf(y).reshape(tl, SUBLANES, width)


def _s5(u_t, a_re, a_im, bb_re, bb_im, c_re, c_im, d_skip, w_glu, b_glu):
    seq, bsz, width = u_t.shape
    assert bsz == SUBLANES
    nch = width // S5_LANE_CHUNK
    gpc = S5_LANE_CHUNK // SSM_GROUP
    sw = gpc * SSM_STATE
    eye = jnp.eye(gpc, dtype=F32)

    def bmat(bb):
        t = bb.reshape(nch, gpc, SSM_STATE, SSM_GROUP).transpose(0, 1, 3, 2)
        return jnp.einsum('jghp,gk->jghkp', t, eye).reshape(nch, S5_LANE_CHUNK, sw)

    def cmat(cc):
        t = cc.reshape(nch, gpc, SSM_GROUP, SSM_STATE).transpose(0, 1, 3, 2)
        return jnp.einsum('jgph,gk->jgpkh', t, eye).reshape(nch, sw, S5_LANE_CHUNK)

    wb = jnp.concatenate([bmat(bb_re), bmat(bb_im)], axis=2)
    cc = jnp.concatenate([cmat(c_re), -cmat(c_im)], axis=1)
    wbh = wb.astype(BF16)
    wbl = (wb - wbh.astype(F32)).astype(BF16)
    cch = cc.astype(BF16)
    ccl = (cc - cch.astype(F32)).astype(BF16)
    tl = S5_STEPS
    full = lambda a: pl.BlockSpec(a.shape, lambda i: (0,) * a.ndim)
    args = (u_t, wbh, wbl, a_re.reshape(1, -1), a_im.reshape(1, -1), cch, ccl,
            d_skip.reshape(1, width), w_glu.astype(BF16), b_glu.reshape(1, width))
    return pl.pallas_call(
        functools.partial(_s5_kernel, tl=tl, width=width),
        out_shape=jax.ShapeDtypeStruct((seq, bsz, width), BF16),
        grid=(seq // tl,),
        in_specs=[pl.BlockSpec((tl, bsz, width), lambda i: (i, 0, 0))] + [full(a) for a in args[1:]],
        out_specs=pl.BlockSpec((tl, bsz, width), lambda i: (i, 0, 0)),
        scratch_shapes=[pltpu.VMEM((tl * SUBLANES, nch * 2 * sw), F32),
                        pltpu.VMEM((SUBLANES, nch * 2 * sw), F32)],
        compiler_params=_cparams(("arbitrary",)),
        name="s5",
    )(*args)


def _bit_transpose32(words):
    x = list(words)
    j, m = 16, 0x0000FFFF
    while j:
        k = 0
        while k < 32:
            t = (x[k] ^ lax.shift_right_logical(x[k + j], jnp.int32(j))) & jnp.int32(m - (1 << 32) if m >= 1 << 31 else m)
            x[k] = x[k] ^ t
            x[k + j] = x[k + j] ^ lax.shift_left(t, jnp.int32(j))
            k = (k + j + 1) & ~j
        j >>= 1
        m = (m ^ (m << j)) & 0xFFFFFFFF
    return x


def _dsa_kernel(qt_ref, qit_ref, wit_ref, ka_ref, ki_ref, vt_ref, o_ref, key_s, mb_s, acc_s, pl_s, p_s, *, tq, tk, topk, seq):
    i = pl.program_id(1)
    q0 = i * tq
    nkt = (q0 + tq + tk - 1) // tk
    ch = DSA_COUNT_ROWS
    krow = lax.broadcasted_iota(I32, (tk, tq), 0)
    qcol = q0 + lax.broadcasted_iota(I32, (tk, tq), 1)
    crow = lax.broadcasted_iota(I32, (ch, tq), 0)

    wb = wit_ref[0] * (IDX_HEADS ** -0.5)

    def score_tile(j, _):
        r0 = pl.multiple_of(j * tk, tk)
        kit = ki_ref[0, pl.ds(r0, tk), :]
        acc = jnp.zeros((tk, tq), F32)
        for h in range(IDX_HEADS):
            s = _dot(kit, qit_ref[0, h * IDX_DIM:(h + 1) * IDX_DIM, :])
            acc = acc + wb[h:h + 1, :] * jnp.maximum(s, 0.0)
        bits = lax.bitcast_convert_type(acc, I32)
        key = jnp.where(bits < 0, bits ^ jnp.int32(0x7FFFFFFF), bits)
        key = jnp.where(acc == 0.0, 0, key)
        key = jnp.where(krow + r0 <= qcol, key, INT_MIN)
        key_s[pl.ds(r0, tk), :] = key
        ukey = key ^ INT_MIN
        for c in range(tk // BITSLICE_ROWS):
            words = [ukey[c * BITSLICE_ROWS + v * SUBLANES:c * BITSLICE_ROWS + (v + 1) * SUBLANES, :]
                     for v in range(32)]
            planes = _bit_transpose32(words)
            g0 = pl.multiple_of((j * (tk // BITSLICE_ROWS) + c) * SUBLANES, SUBLANES)
            for it in range(32):
                pl_s[it, pl.ds(g0, SUBLANES), :] = planes[it]
        return 0

    @pl.when((pl.program_id(0) == 0) & (i == 0))
    def _():
        pl_s[...] = jnp.zeros(pl_s.shape, I32)

    lax.fori_loop(0, nkt, score_tile, 0)

    def count(pred):
        def tile(j, cnt):
            for c in range(tk // ch):
                rr = pl.multiple_of(j * tk + c * ch, ch)
                cnt = cnt + jnp.where(pred(key_s[pl.ds(rr, ch), :], rr), 1, 0)
            return cnt
        cnt = lax.fori_loop(0, nkt, tile, jnp.zeros((ch, tq), I32))
        return jnp.sum(cnt.astype(F32), axis=0, keepdims=True)

    ngrp = seq // 32

    def lane_count(words):
        pc = lax.population_count(words).reshape(ngrp // SUBLANES, SUBLANES, tq)
        return jnp.sum(jnp.sum(pc, axis=0).astype(F32), axis=0, keepdims=True)

    def bit_step(it, carry):
        alive, above, ans_u = carry
        ones = alive & pl_s[it]
        cnt1 = lane_count(ones)
        take = above + cnt1 >= float(topk)
        alive = jnp.where(take, ones, alive ^ ones)
        above = jnp.where(take, above, above + cnt1)
        ans_u = jnp.where(take, ans_u | lax.shift_left(jnp.int32(1), 31 - it), ans_u)
        return alive, above, ans_u

    grow = lax.broadcasted_iota(I32, (ngrp, tq), 0)
    alive0 = jnp.where(grow < nkt * (tk // 32), -1, 0)
    alive, above, ans_u = lax.fori_loop(
        0, 32, bit_step, (alive0, jnp.zeros((1, tq), F32), jnp.zeros((1, tq), I32)))
    thr = jnp.maximum(ans_u ^ INT_MIN, INT_MIN + 1)
    cnt_ge = above + lane_count(alive)
    tied = jnp.where(ans_u != 0, cnt_ge, 0.0) > float(topk)
    has_ties = jnp.max(jnp.where(tied, 1.0, 0.0)) > 0.0

    def tie_cut():
        need = float(topk) - count(lambda kb, rr: kb > thr)
        nbits = max(1, (seq - 1).bit_length())

        def idx_step(b, x):
            cand = x | lax.shift_left(jnp.int32(1), nbits - 1 - b)
            below = count(lambda kb, rr: jnp.where(kb == thr, crow + rr, seq) < cand)
            return jnp.where(below < need, cand, x)

        x = lax.fori_loop(0, nbits, idx_step, jnp.zeros((1, tq), I32))
        return jnp.where(tied, x, seq)

    cut = lax.cond(has_ties, tie_cut, lambda: jnp.full((1, tq), seq, I32))

    def bias_tile(j, _):
        for c in range(tk // ch):
            rr = pl.multiple_of(j * tk + c * ch, ch)
            kb = key_s[pl.ds(rr, ch), :]
            tie_bias = jnp.where(crow + rr <= cut, 0.0, NEG_BIG)
            mb_s[pl.ds(rr, ch), :] = jnp.where(kb > thr, 0.0, jnp.where(kb == thr, tie_bias, NEG_BIG))
        return 0

    lax.fori_loop(0, nkt, bias_tile, 0)

    def logits(j, h):
        r0 = pl.multiple_of(j * tk, tk)
        s = _dot(ka_ref[0, pl.ds(r0, tk), :], qt_ref[0, h * LANES:(h + 1) * LANES, :]) + mb_s[pl.ds(r0, tk), :]
        return s.reshape(tk // SUBLANES, SUBLANES, tq)

    def max_tile(j, ms):
        return tuple(jnp.maximum(ms[h], jnp.max(logits(j, h), axis=0)) for h in range(N_HEADS))

    ms = lax.fori_loop(0, nkt, max_tile, (jnp.full((SUBLANES, tq), NEG_BIG, F32),) * N_HEADS)
    m = [jnp.max(mh, axis=0, keepdims=True) for mh in ms]
    acc_s[...] = jnp.zeros(acc_s.shape, F32)

    def sum_tile(j, ls):
        r0 = pl.multiple_of(j * tk, tk)
        out = []
        for h in range(N_HEADS):
            p = jnp.exp(logits(j, h) - m[h])
            out.append(ls[h] + jnp.sum(p, axis=0))
            p_s[h] = _bf(p.reshape(tk, tq))
        for h in range(N_HEADS):
            rows = slice(h * HEAD_DIM, (h + 1) * HEAD_DIM)
            acc_s[rows, :] = acc_s[rows, :] + _dot(vt_ref[0, :, pl.ds(r0, tk)], p_s[h])
        return tuple(out)

    ls = lax.fori_loop(0, nkt, sum_tile, (jnp.zeros((SUBLANES, tq), F32),) * N_HEADS)
    for h in range(N_HEADS):
        rows = slice(h * HEAD_DIM, (h + 1) * HEAD_DIM)
        o_ref[0, rows, :] = _bf(acc_s[rows, :] / jnp.sum(ls[h], axis=0, keepdims=True))


def _dsa(q, qi, wit, k, ki, vt):
    bsz, seq, aw = q.shape
    tq = min(DSA_Q_COLS, seq)
    tk = min(DSA_K_ROWS, seq)
    topk = min(TOPK_MAX, seq // 4)
    assert (seq - 1) // POS_SPLIT < 256 and POS_SPLIT <= 256, "key positions must split into two bf16-exact parts"
    slopes = [2.0 ** (-8.0 * (h + 1) / N_HEADS) for h in range(N_HEADS)]
    assert all(float(np.float32(sl).astype(BF16)) == sl for sl in slopes), "ALiBi slopes must be bf16-exact"
    pos = jnp.arange(seq, dtype=I32)
    posc = jnp.stack([(pos // POS_SPLIT) * POS_SPLIT, pos % POS_SPLIT], axis=1).astype(BF16)
    ka = jnp.concatenate([k, jnp.broadcast_to(posc[None], (bsz, seq, 2)),
                          jnp.zeros((bsz, seq, LANES - HEAD_DIM - 2), BF16)], axis=2)
    qh = q.reshape(bsz, seq, N_HEADS, HEAD_DIM).transpose(0, 2, 3, 1)
    srow = jnp.asarray(slopes, BF16)[None, :, None, None]
    extra = jnp.concatenate([jnp.broadcast_to(srow, (bsz, N_HEADS, 2, seq)),
                             jnp.zeros((bsz, N_HEADS, LANES - HEAD_DIM - 2, seq), BF16)], axis=2)
    qt = jnp.concatenate([qh, extra], axis=2).reshape(bsz, N_HEADS * LANES, seq)
    qit = qi.transpose(0, 2, 1)
    kern = functools.partial(_dsa_kernel, tq=tq, tk=tk, topk=topk, seq=seq)
    cols = lambda r: pl.BlockSpec((1, r, tq), lambda b, i: (b, 0, i))
    return pl.pallas_call(
        kern,
        out_shape=jax.ShapeDtypeStruct((bsz, aw, seq), BF16),
        grid=(bsz, seq // tq),
        in_specs=[cols(N_HEADS * LANES), cols(qit.shape[1]), cols(IDX_HEADS),
                  pl.BlockSpec((1, seq, LANES), lambda b, i: (b, 0, 0)),
                  pl.BlockSpec((1, seq, IDX_DIM), lambda b, i: (b, 0, 0)),
                  pl.BlockSpec((1, HEAD_DIM, seq), lambda b, i: (b, 0, 0))],
        out_specs=cols(aw),
        scratch_shapes=[pltpu.VMEM((seq, tq), I32), pltpu.VMEM((seq, tq), F32), pltpu.VMEM((aw, tq), F32),
                        pltpu.VMEM((32, seq // 32, tq), I32), pltpu.VMEM((N_HEADS, tk, tq), BF16)],
        compiler_params=_cparams(("arbitrary", "arbitrary")),
        name="dsa",
    )(qt, qit, wit, ka, ki, vt)


def _mix_kernel(x_ref, ys_ref, ya_ref, g_ref, ada_ref, wps_ref, wpa_ref, wo_ref, g2_ref, h_ref, u2_ref, *, d):
    gate1 = ada_ref[0, :, 2 * d:3 * d]
    shift2 = ada_ref[0, :, 3 * d:4 * d]
    scale2 = ada_ref[0, :, 4 * d:5 * d]
    g = g_ref[0]
    mixed = (jax.nn.sigmoid(g[:, 0:d]) * _dot(ys_ref[0], wps_ref[...])
             + jax.nn.sigmoid(g[:, d:2 * d]) * _dot(ya_ref[0], wpa_ref[...]))
    h = x_ref[0] + gate1 * _dot(_bf(mixed), wo_ref[...])
    h_ref[0] = h
    u2_ref[0] = _rms(h, g2_ref[...]) * (1.0 + scale2) + shift2


def _mix(x, ys, ya, g, ada3, wps, wpa, wo, g2):
    bsz, seq, d = x.shape
    tm = MIX_ROWS
    row = lambda w: pl.BlockSpec((1, tm, w), lambda b, l: (b, l, 0))
    full = lambda a: pl.BlockSpec(a.shape, lambda b, l: (0,) * a.ndim)
    wps, wpa, wo = wps.astype(BF16), wpa.astype(BF16), wo.astype(BF16)
    g2 = g2.reshape(1, d)
    return pl.pallas_call(
        functools.partial(_mix_kernel, d=d),
        out_shape=(jax.ShapeDtypeStruct((bsz, seq, d), F32), jax.ShapeDtypeStruct((bsz, seq, d), F32)),
        grid=(bsz, seq // tm),
        in_specs=[row(d), row(ys.shape[2]), row(ya.shape[2]), row(2 * d),
                  pl.BlockSpec((1, 1, ada3.shape[2]), lambda b, l: (b, 0, 0)),
                  full(wps), full(wpa), full(wo), full(g2)],
        out_specs=(row(d), row(d)),
        compiler_params=_cparams(("arbitrary", "arbitrary")),
        name="mix",
    )(x, ys, ya, g, ada3, wps, wpa, wo, g2)


def _first_max(cur, idx, axis, big):
    m = jnp.max(cur, axis=axis, keepdims=True)
    first = jnp.min(jnp.where(cur == m, idx, big), axis=axis, keepdims=True)
    return m, idx == first


def _route_kernel(u_ref, wrh_ref, wrl_ref, rb_ref, tri_ref, e_ref, gt_ref, rk_ref, cnt_ref, run_s, *, t):
    @pl.when(pl.program_id(0) == 0)
    def _():
        run_s[...] = jnp.zeros_like(run_s)

    uh, ul = _split(u_ref[...])
    logits = _dot_nt(wrh_ref[...], uh) + (_dot_nt(wrl_ref[...], uh) + _dot_nt(wrh_ref[...], ul))
    scores = jax.nn.sigmoid(logits)
    biased = scores + rb_ref[...]
    per_group = N_EXPERTS // N_GROUPS
    b3 = biased.reshape(N_GROUPS, per_group, t)
    i3 = lax.broadcasted_iota(I32, b3.shape, 1)
    m1, hit1 = _first_max(b3, i3, 1, per_group)
    m2 = jnp.max(jnp.where(hit1, -jnp.inf, b3), axis=1, keepdims=True)
    gs = (m1 + m2).reshape(N_GROUPS, t)
    gi = lax.broadcasted_iota(I32, gs.shape, 0)
    gsel = jnp.zeros(gs.shape, F32)
    for _ in range(TOPK_GROUPS):
        _, hit = _first_max(gs, gi, 0, N_GROUPS)
        gsel = jnp.where(hit, 1.0, gsel)
        gs = jnp.where(hit, -jnp.inf, gs)
    cur = jnp.where(gsel.reshape(N_GROUPS, 1, t) > 0.0, b3, -jnp.inf).reshape(N_EXPERTS, t)
    ei = lax.broadcasted_iota(I32, cur.shape, 0)
    hits = []
    gates = []
    for _ in range(TOP_K):
        _, hit = _first_max(cur, ei, 0, N_EXPERTS)
        hits.append(hit)
        gates.append(jnp.sum(jnp.where(hit, scores, 0.0), axis=0, keepdims=True))
        cur = jnp.where(hit, -jnp.inf, cur)
    gate = jnp.concatenate(gates, axis=0)
    gt_ref[...] = gate / jnp.sum(gate, axis=0, keepdims=True) * ROUTED_SCALE
    e_ref[...] = jnp.concatenate(
        [jnp.sum(jnp.where(hit, ei, 0), axis=0, keepdims=True) for hit in hits], axis=0)
    onehot = jnp.zeros(cur.shape, F32)
    for hit in hits:
        onehot = jnp.where(hit, 1.0, onehot)
    before = _dot(_bf(onehot), tri_ref[...]) + run_s[:, 0:1]
    rk_ref[...] = jnp.concatenate(
        [jnp.sum(jnp.where(hit, before, 0.0), axis=0, keepdims=True) for hit in hits], axis=0).astype(I32)
    run_s[...] = run_s[...] + jnp.sum(onehot, axis=1, keepdims=True)
    cnt_ref[...] = run_s[...]


def _route(u2, w_router, router_bias):
    n, d = u2.shape
    t = min(ROUTE_ROWS, n)
    wt = w_router.T
    wrh = wt.astype(BF16)
    wrl = (wt - wrh.astype(F32)).astype(BF16)
    tri = (jnp.arange(t)[:, None] < jnp.arange(t)[None, :]).astype(BF16)
    full = lambda a: pl.BlockSpec(a.shape, lambda i: (0,) * a.ndim)
    col = pl.BlockSpec((TOP_K, t), lambda i: (0, i))
    rb = router_bias.reshape(N_EXPERTS, 1)
    return pl.pallas_call(
        functools.partial(_route_kernel, t=t),
        out_shape=(jax.ShapeDtypeStruct((TOP_K, n), I32), jax.ShapeDtypeStruct((TOP_K, n), F32),
                   jax.ShapeDtypeStruct((TOP_K, n), I32), jax.ShapeDtypeStruct((N_EXPERTS, LANES), F32)),
        grid=(n // t,),
        in_specs=[pl.BlockSpec((t, d), lambda i: (i, 0)), full(wrh), full(wrl), full(rb), full(tri)],
        out_specs=(col, col, col, pl.BlockSpec((N_EXPERTS, LANES), lambda i: (0, 0))),
        scratch_shapes=[pltpu.VMEM((N_EXPERTS, LANES), F32)],
        compiler_params=_cparams(("arbitrary",)),
        name="route",
    )(u2, wrh, wrl, rb, tri)


def _dest_kernel(ps_ref, e_ref, rk_ref, o_ref):
    e = e_ref[...]
    acc = rk_ref[...]
    for x in range(N_EXPERTS):
        acc = acc + jnp.where(e == x, ps_ref[x], 0)
    o_ref[...] = acc


def _dest(pstart, eidx_t, rank_t):
    k, n = eidx_t.shape
    tn = min(DEST_COLS, n)
    blk = pl.BlockSpec((k, tn), lambda i, ps: (0, i))
    return pl.pallas_call(
        _dest_kernel,
        out_shape=jax.ShapeDtypeStruct((k, n), I32),
        grid_spec=pltpu.PrefetchScalarGridSpec(
            num_scalar_prefetch=1, grid=(n // tn,), in_specs=[blk, blk], out_specs=blk),
        compiler_params=_cparams(("arbitrary",)),
        name="dest",
    )(pstart, eidx_t, rank_t)


def _dispatch_kernel(dest_ref, u_ref, xs_in, xs_hbm, sems, *, td):
    del xs_in
    nb = td // DISPATCH_BATCH

    def row_copy(t, k, slot):
        return pltpu.make_async_copy(u_ref.at[pl.ds(t, 1)],
                                     xs_hbm.at[pl.ds(dest_ref[t * TOP_K + k], 1)], sems.at[slot])

    def issue(bi, slot):
        def body(tt, _):
            for k in range(TOP_K):
                row_copy(bi * DISPATCH_BATCH + tt, k, slot).start()
            return 0
        lax.fori_loop(0, DISPATCH_BATCH, body, 0)

    def drain(bi, slot):
        def body(tt, _):
            for k in range(TOP_K):
                row_copy(bi * DISPATCH_BATCH + tt, k, slot).wait()
            return 0
        lax.fori_loop(0, DISPATCH_BATCH, body, 0)

    issue(0, 0)
    for bi in range(1, nb):
        issue(bi, bi % 2)
        drain(bi - 1, (bi - 1) % 2)
    drain(nb - 1, (nb - 1) % 2)


def _dispatch(dest_flat, u2, n_rows):
    n, d = u2.shape
    td = min(DISPATCH_ROWS, n)
    xs0 = jnp.zeros((n_rows, d), F32)
    return pl.pallas_call(
        functools.partial(_dispatch_kernel, td=td),
        out_shape=jax.ShapeDtypeStruct((n_rows, d), F32),
        grid=(n // td,),
        in_specs=[pl.BlockSpec((td * TOP_K,), lambda i: (i,), memory_space=pltpu.SMEM),
                  pl.BlockSpec((td, d), lambda i: (i, 0)),
                  pl.BlockSpec(memory_space=pl.ANY)],
        out_specs=pl.BlockSpec(memory_space=pl.ANY),
        scratch_shapes=[pltpu.SemaphoreType.DMA((2,))],
        input_output_aliases={2: 0},
        compiler_params=_cparams(("arbitrary",)),
        name="dispatch",
    )(dest_flat, u2, xs0)


def _experts_kernel(be_ref, xs_ref, wg_ref, wu_ref, wd_ref, ys_ref):
    del be_ref
    x = _bf(xs_ref[...])
    hmid = jax.nn.silu(_dot(x, wg_ref[0])) * _dot(x, wu_ref[0])
    ys_ref[...] = _dot(_bf(hmid), wd_ref[0])


def _experts(blk_expert, xs, wg, wu, wd):
    rows, d = xs.shape
    de = wg.shape[2]
    nblk = rows // EXPERT_ROWS
    return pl.pallas_call(
        _experts_kernel,
        out_shape=jax.ShapeDtypeStruct((rows, d), F32),
        grid_spec=pltpu.PrefetchScalarGridSpec(
            num_scalar_prefetch=1, grid=(nblk,),
            in_specs=[pl.BlockSpec((EXPERT_ROWS, d), lambda i, be: (i, 0)),
                      pl.BlockSpec((1, d, de), lambda i, be: (be[i], 0, 0)),
                      pl.BlockSpec((1, d, de), lambda i, be: (be[i], 0, 0)),
                      pl.BlockSpec((1, de, d), lambda i, be: (be[i], 0, 0))],
            out_specs=pl.BlockSpec((EXPERT_ROWS, d), lambda i, be: (i, 0))),
        compiler_params=_cparams(("arbitrary",)),
        name="experts",
    )(blk_expert, xs, wg.astype(BF16), wu.astype(BF16), wd.astype(BF16))


def _combine_kernel(dest_ref, ys_hbm, gate_ref, h_ref, u2_ref, ada_ref, wsg_ref, wsu_ref, wsd_ref, gf_ref,
                    o_ref, buf, sem, *, tc, d):
    def row_copy(t, k):
        return pltpu.make_async_copy(ys_hbm.at[pl.ds(dest_ref[t * TOP_K + k], 1)],
                                     buf.at[k, pl.ds(t, 1)], sem.at[0])

    def issue(t, _):
        for k in range(TOP_K):
            row_copy(t, k).start()
        return 0

    def drain(t, _):
        for k in range(TOP_K):
            row_copy(t, k).wait()
        return 0

    lax.fori_loop(0, tc, issue, 0)
    x = _bf(u2_ref[...])
    shared = _dot(_bf(jax.nn.silu(_dot(x, wsg_ref[...])) * _dot(x, wsu_ref[...])), wsd_ref[...])
    lax.fori_loop(0, tc, drain, 0)
    gate = gate_ref[...]
    routed = gate[:, 0:1] * buf[0]
    for k in range(1, TOP_K):
        routed = routed + gate[:, k:k + 1] * buf[k]
    gate2 = ada_ref[0, :, 5 * d:6 * d]
    h = h_ref[...] + gate2 * (routed + shared)
    o_ref[...] = _rms(h, gf_ref[...])


def _combine(dest_flat, ys, gate, h1, u2, ada3, wsg, wsu, wsd, gf, seq):
    n, d = h1.shape
    tc = min(COMBINE_ROWS, seq)
    per_b = seq // tc
    row = pl.BlockSpec((tc, d), lambda i: (i, 0))
    full = lambda a: pl.BlockSpec(a.shape, lambda i: (0,) * a.ndim)
    wsg, wsu, wsd = wsg.astype(BF16), wsu.astype(BF16), wsd.astype(BF16)
    gf = gf.reshape(1, d)
    return pl.pallas_call(
        functools.partial(_combine_kernel, tc=tc, d=d),
        out_shape=jax.ShapeDtypeStruct((n, d), F32),
        grid=(n // tc,),
        in_specs=[pl.BlockSpec((tc * TOP_K,), lambda i: (i,), memory_space=pltpu.SMEM),
                  pl.BlockSpec(memory_space=pl.ANY),
                  pl.BlockSpec((tc, TOP_K), lambda i: (i, 0)),
                  row, row,
                  pl.BlockSpec((1, 1, ada3.shape[2]), lambda i: (i // per_b, 0, 0)),
                  full(wsg), full(wsu), full(wsd), full(gf)],
        out_specs=row,
        scratch_shapes=[pltpu.VMEM((TOP_K, tc, d), F32), pltpu.SemaphoreType.DMA((1,))],
        compiler_params=_cparams(("arbitrary",)),
        name="combine",
    )(dest_flat, ys, gate, h1, u2, ada3, wsg, wsu, wsd, gf)


def _moe(h1, u2, ada3, w_router, router_bias, wg, wu, wd, wsg, wsu, wsd, gf):
    bsz, seq, d = h1.shape
    n = bsz * seq
    h1f = h1.reshape(n, d)
    u2f = u2.reshape(n, d)
    eidx_t, gate_t, rank_t, counts = _route(u2f, w_router, router_bias)
    cnt = counts[:, 0].astype(I32)
    padded = (cnt + EXPERT_ROWS - 1) // EXPERT_ROWS * EXPERT_ROWS
    pend = jnp.cumsum(padded)
    pstart = (pend - padded).astype(I32)
    nblk = (n * TOP_K + N_EXPERTS * (EXPERT_ROWS - 1) + EXPERT_ROWS - 1) // EXPERT_ROWS
    blk_row0 = jnp.arange(nblk, dtype=I32) * EXPERT_ROWS
    blk_expert = jnp.minimum(jnp.sum(pend[None, :] <= blk_row0[:, None], axis=1), N_EXPERTS - 1).astype(I32)
    dest_flat = _dest(pstart, eidx_t, rank_t).T.reshape(-1)
    xs = _dispatch(dest_flat, u2f, nblk * EXPERT_ROWS)
    ys = _experts(blk_expert, xs, wg, wu, wd)
    out = _combine(dest_flat, ys, gate_t.T, h1f, u2f, ada3, wsg, wsu, wsd, gf, seq)
    return out.reshape(bsz, seq, d)


def kernel(x, c, w_ada, b_ada, norm1_g, w_in, ssm_lambda_re, ssm_lambda_im, ssm_log_dt, ssm_b_re, ssm_b_im,
           ssm_c_re, ssm_c_im, ssm_d, ssm_w_glu, ssm_b_glu, w_proj_ssm, w_proj_attn, w_out, norm2_g, w_router,
           router_bias, w_exp_gate, w_exp_up, w_exp_down, w_sh_gate, w_sh_up, w_sh_down, norm_f_g):
    depth = w_ada.shape[0]
    assert depth == 1, "the final norm is fused into the last (only) layer's combine kernel"
    bsz, seq, d = x.shape
    layer = 0
    ada3 = _ada(c, w_ada[layer], b_ada[layer]).reshape(bsz, 1, 6 * d)
    us, q, qi, k, ki, vt, wit, g = _inproj(x, ada3, norm1_g[layer], w_in[layer])
    a_re, a_im, bb_re, bb_im = _s5disc(ssm_lambda_re[layer], ssm_lambda_im[layer], ssm_log_dt[layer],
                                       ssm_b_re[layer], ssm_b_im[layer])
    ys_t = _s5(us.transpose(1, 0, 2), a_re, a_im, bb_re, bb_im, ssm_c_re[layer], ssm_c_im[layer],
               ssm_d[layer], ssm_w_glu[layer], ssm_b_glu[layer])
    ya = _dsa(q, qi, wit, k, ki, vt).transpose(0, 2, 1)
    h1, u2 = _mix(x, ys_t.transpose(1, 0, 2), ya, g, ada3, w_proj_ssm[layer], w_proj_attn[layer],
                  w_out[layer], norm2_g[layer])
    return _moe(h1, u2, ada3, w_router[layer], router_bias[layer], w_exp_gate[layer], w_exp_up[layer],
                w_exp_down[layer], w_sh_gate[layer], w_sh_up[layer], w_sh_down[layer], norm_f_g)
```

```python
import functools
import math

import jax
import jax.numpy as jnp
import numpy as np
from jax import lax
from jax.experimental import pallas as pl
from jax.experimental.pallas import tpu as pltpu

F32 = jnp.float32
BF16 = jnp.bfloat16
I32 = jnp.int32

SSM_GROUP = 16
SSM_STATE = 64
N_HEADS = 8
HEAD_DIM = 64
IDX_HEADS = 8
IDX_DIM = 64
TOPK_MAX = 256
N_EXPERTS = 64
TOP_K = 8
N_GROUPS = 8
TOPK_GROUPS = 4
ROUTED_SCALE = 2.5
EPS = 1e-6

V7X_VMEM_LIMIT_BYTES = 56 * 1024 * 1024
LANES = 128
SUBLANES = 8

INPROJ_ROWS = 256
S5_STEPS = 64
S5_LANE_CHUNK = 128
DSA_Q_COLS = 256
DSA_K_ROWS = 512
DSA_COUNT_ROWS = 64
BITSLICE_ROWS = 256
POS_SPLIT = 64
MIX_ROWS = 256
MOE_TILE = 256
SLOT_CHUNK = 512
EXPERT_ROWS = 256

NEG_BIG = -1e30
INT_MIN = -(2 ** 31)


def _cparams(sem):
    return pltpu.CompilerParams(dimension_semantics=sem, vmem_limit_bytes=V7X_VMEM_LIMIT_BYTES)


def _bf(x):
    return x.astype(BF16)


def _dot(a, b):
    return jnp.dot(a, b, preferred_element_type=F32)


def _dot_nt(a, b):
    return lax.dot_general(a, b, (((1,), (1,)), ((), ())), preferred_element_type=F32)


def _split(x):
    hi = _bf(x)
    lo = _bf(x - hi.astype(F32))
    return hi, lo


def _dot3(a, b):
    ah, al = _split(a)
    bh, bl = _split(b)
    return _dot(ah, bh) + (_dot(ah, bl) + _dot(al, bh))


def _rms(x, g):
    return x * lax.rsqrt(jnp.mean(x * x, axis=-1, keepdims=True) + EPS) * g


def _ada_kernel(c_ref, w_ref, b_ref, o_ref):
    c = c_ref[...]
    o_ref[...] = _dot3(c * jax.nn.sigmoid(c), w_ref[...]) + b_ref[...]


def _ada(c, w, b):
    bsz, d = c.shape
    n = w.shape[1]
    tn = 1024
    return pl.pallas_call(
        _ada_kernel,
        out_shape=jax.ShapeDtypeStruct((bsz, n), F32),
        grid=(n // tn,),
        in_specs=[pl.BlockSpec((bsz, d), lambda j: (0, 0)),
                  pl.BlockSpec((d, tn), lambda j: (0, j)),
                  pl.BlockSpec((1, tn), lambda j: (0, j))],
        out_specs=pl.BlockSpec((bsz, tn), lambda j: (0, j)),
        compiler_params=_cparams(("arbitrary",)),
        name="ada",
    )(c, w, b.reshape(1, n))


def _inproj_kernel(x_ref, ada_ref, g1_ref, w_ref, wt_ref,
                   us_ref, q_ref, qi_ref, k_ref, ki_ref, vt_ref, wit_ref, g_ref, *, d, ssm_w, attn_w, idx_w):
    x = x_ref[0]
    shift = ada_ref[0, :, 0:d]
    scale = ada_ref[0, :, d:2 * d]
    u = _bf(_rms(x, g1_ref[...]) * (1.0 + scale) + shift)
    r = _dot(u, w_ref[...])
    o = 0
    us_ref[0] = r[:, o:o + ssm_w]
    o += ssm_w
    q_ref[0] = _bf(r[:, o:o + attn_w])
    o += attn_w
    qi_ref[0] = _bf(r[:, o:o + idx_w])
    o += idx_w
    k_ref[0] = _bf(r[:, o:o + HEAD_DIM])
    o += LANES
    ki_ref[0] = _bf(r[:, o:o + IDX_DIM])
    o += LANES
    g_ref[0] = r[:, o:o + 2 * d]
    rt = _dot_nt(wt_ref[...], u)
    vt_ref[0] = _bf(rt[0:HEAD_DIM])
    wit_ref[0] = rt[HEAD_DIM:HEAD_DIM + IDX_HEADS]


def _inproj(x, ada3, g1, w_in):
    bsz, seq, d = x.shape
    ssm_w = 512
    attn_w = N_HEADS * HEAD_DIM
    idx_w = IDX_HEADS * IDX_DIM
    sizes = (ssm_w, attn_w, HEAD_DIM, HEAD_DIM, idx_w, IDX_DIM, IDX_HEADS, d, d)
    offs = [0]
    for s in sizes:
        offs.append(offs[-1] + s)
    w_ssm, w_q, w_k, w_v, w_qi, w_ki, w_wi, w_gs, w_ga = [w_in[:, offs[i]:offs[i + 1]] for i in range(9)]
    zpad = lambda n: jnp.zeros((d, n), F32)
    wbig = jnp.concatenate([
        w_ssm, w_q * (HEAD_DIM ** -0.5), w_qi * (IDX_DIM ** -0.5),
        w_k, zpad(LANES - HEAD_DIM), w_ki, zpad(LANES - IDX_DIM), w_gs, w_ga], axis=1).astype(BF16)
    wt = jnp.concatenate([w_v, w_wi, zpad(LANES - HEAD_DIM - IDX_HEADS)], axis=1).T.astype(BF16)
    nw = wbig.shape[1]
    tl = INPROJ_ROWS
    kern = functools.partial(_inproj_kernel, d=d, ssm_w=ssm_w, attn_w=attn_w, idx_w=idx_w)
    row = lambda w: pl.BlockSpec((1, tl, w), lambda b, l: (b, l, 0))
    colt = lambda h: pl.BlockSpec((1, h, tl), lambda b, l: (b, 0, l))
    return pl.pallas_call(
        kern,
        out_shape=(jax.ShapeDtypeStruct((bsz, seq, ssm_w), F32),
                   jax.ShapeDtypeStruct((bsz, seq, attn_w), BF16),
                   jax.ShapeDtypeStruct((bsz, seq, idx_w), BF16),
                   jax.ShapeDtypeStruct((bsz, seq, HEAD_DIM), BF16),
                   jax.ShapeDtypeStruct((bsz, seq, IDX_DIM), BF16),
                   jax.ShapeDtypeStruct((bsz, HEAD_DIM, seq), BF16),
                   jax.ShapeDtypeStruct((bsz, IDX_HEADS, seq), F32),
                   jax.ShapeDtypeStruct((bsz, seq, 2 * d), F32)),
        grid=(bsz, seq // tl),
        in_specs=[row(d),
                  pl.BlockSpec((1, 1, ada3.shape[2]), lambda b, l: (b, 0, 0)),
                  pl.BlockSpec((1, d), lambda b, l: (0, 0)),
                  pl.BlockSpec((d, nw), lambda b, l: (0, 0)),
                  pl.BlockSpec((LANES, d), lambda b, l: (0, 0))],
        out_specs=(row(ssm_w), row(attn_w), row(idx_w), row(HEAD_DIM), row(IDX_DIM),
                   colt(HEAD_DIM), colt(IDX_HEADS), row(2 * d)),
        compiler_params=_cparams(("arbitrary", "arbitrary")),
        name="inproj",
    )(x, ada3, g1.reshape(1, d), wbig, wt)


def _s5disc_kernel(lr_ref, li_ref, ldt_ref, br_ref, bi_ref, are_ref, aim_ref, bbr_ref, bbi_ref):
    lr = lr_ref[...]
    li = li_ref[...]
    dt = jnp.exp(ldt_ref[...])
    mag = jnp.exp(lr * dt)
    a_re = mag * jnp.cos(li * dt)
    a_im = mag * jnp.sin(li * dt)
    den = lr * lr + li * li
    n_re = a_re - 1.0
    f_re = (n_re * lr + a_im * li) / den
    f_im = (a_im * lr - n_re * li) / den
    br = br_ref[...]
    bi = bi_ref[...]
    are_ref[...] = a_re
    aim_ref[...] = a_im
    bbr_ref[...] = f_re * br - f_im * bi
    bbi_ref[...] = f_re * bi + f_im * br


def _s5disc(lam_re, lam_im, log_dt, b_re, b_im):
    g, p = lam_re.shape
    h = b_re.shape[2]
    rep = lambda a: jnp.repeat(a, h, axis=1)
    ldt = jnp.broadcast_to(log_dt[:, None], (g, p * h))
    sds = jax.ShapeDtypeStruct((g, p * h), F32)
    a_re, a_im, bb_re, bb_im = pl.pallas_call(
        _s5disc_kernel, out_shape=(sds, sds, sds, sds), name="s5disc",
    )(rep(lam_re), rep(lam_im), ldt, b_re.reshape(g, p * h), b_im.reshape(g, p * h))
    return a_re[:, ::h], a_im[:, ::h], bb_re.reshape(g, p, h), bb_im.reshape(g, p, h)


def _s5_kernel(u_ref, wbh_ref, wbl_ref, ar_ref, ai_ref, cch_ref, ccl_ref, dsk_ref, wg_ref, bg_ref,
               o_ref, buf, hst, *, tl, width):
    nch = width // S5_LANE_CHUNK
    sw = S5_LANE_CHUNK // SSM_GROUP * SSM_STATE
    rows = tl * SUBLANES

    @pl.when(pl.program_id(0) == 0)
    def _():
        hst[...] = jnp.zeros_like(hst)

    u = u_ref[...].reshape(rows, width)
    uh, ul = _split(u)
    for j in range(nch):
        cs = slice(j * S5_LANE_CHUNK, (j + 1) * S5_LANE_CHUNK)
        buf[:, j * 2 * sw:(j + 1) * 2 * sw] = (
            _dot(uh[:, cs], wbh_ref[j]) + (_dot(uh[:, cs], wbl_ref[j]) + _dot(ul[:, cs], wbh_ref[j])))

    for j in range(nch):
        re_cols = slice(j * 2 * sw, j * 2 * sw + sw)
        im_cols = slice(j * 2 * sw + sw, (j + 1) * 2 * sw)
        a_re = jnp.broadcast_to(ar_ref[:, j * sw:(j + 1) * sw], (SUBLANES, sw))
        a_im = jnp.broadcast_to(ai_ref[:, j * sw:(j + 1) * sw], (SUBLANES, sw))

        def step(t, carry, re_cols=re_cols, im_cols=im_cols, a_re=a_re, a_im=a_im):
            h_re, h_im = carry
            r0 = pl.multiple_of(t * SUBLANES, SUBLANES)
            n_re = (a_re * h_re - a_im * h_im) + buf[pl.ds(r0, SUBLANES), re_cols]
            n_im = (a_re * h_im + a_im * h_re) + buf[pl.ds(r0, SUBLANES), im_cols]
            buf[pl.ds(r0, SUBLANES), re_cols] = n_re
            buf[pl.ds(r0, SUBLANES), im_cols] = n_im
            return n_re, n_im

        h_re, h_im = lax.fori_loop(0, tl, step, (hst[:, re_cols], hst[:, im_cols]), unroll=8)
        hst[:, re_cols] = h_re
        hst[:, im_cols] = h_im

    ys = []
    for j in range(nch):
        hh, hl = _split(buf[:, j * 2 * sw:(j + 1) * 2 * sw])
        ys.append(_dot(hh, cch_ref[j]) + (_dot(hh, ccl_ref[j]) + _dot(hl, cch_ref[j])))
    y = jnp.concatenate(ys, axis=1) + dsk_ref[...] * u
    y = jax.nn.gelu(y)
    y = y * jax.nn.sigmoid(_dot(_bf(y), wg_ref[...]) + bg_ref[...])
    o_ref[...] = _bf(y).reshape(tl, SUBLANES, width)


def _s5(u_t, a_re, a_im, bb_re, bb_im, c_re, c_im, d_skip, w_glu, b_glu):
    seq, bsz, width = u_t.shape
    assert bsz == SUBLANES
    nch = width // S5_LANE_CHUNK
    gpc = S5_LANE_CHUNK // SSM_GROUP
    sw = gpc * SSM_STATE
    eye = jnp.eye(gpc, dtype=F32)

    def bmat(bb):
        t = bb.reshape(nch, gpc, SSM_STATE, SSM_GROUP).transpose(0, 1, 3, 2)
        return jnp.einsum('jghp,gk->jghkp', t, eye).reshape(nch, S5_LANE_CHUNK, sw)

    def cmat(cc):
        t = cc.reshape(nch, gpc, SSM_GROUP, SSM_STATE).transpose(0, 1, 3, 2)
        return jnp.einsum('jgph,gk->jgpkh', t, eye).reshape(nch, sw, S5_LANE_CHUNK)

    wb = jnp.concatenate([bmat(bb_re), bmat(bb_im)], axis=2)
    cc = jnp.concatenate([cmat(c_re), -cmat(c_im)], axis=1)
    wbh = wb.astype(BF16)
    wbl = (wb - wbh.astype(F32)).astype(BF16)
    cch = cc.astype(BF16)
    ccl = (cc - cch.astype(F32)).astype(BF16)
    tl = S5_STEPS
    full = lambda a: pl.BlockSpec(a.shape, lambda i: (0,) * a.ndim)
    args = (u_t, wbh, wbl, a_re.reshape(1, -1), a_im.reshape(1, -1), cch, ccl,
            d_skip.reshape(1, width), w_glu.astype(BF16), b_glu.reshape(1, width))
    return pl.pallas_call(
        functools.partial(_s5_kernel, tl=tl, width=width),
        out_shape=jax.ShapeDtypeStruct((seq, bsz, width), BF16),
        grid=(seq // tl,),
        in_specs=[pl.BlockSpec((tl, bsz, width), lambda i: (i, 0, 0))] + [full(a) for a in args[1:]],
        out_specs=pl.BlockSpec((tl, bsz, width), lambda i: (i, 0, 0)),
        scratch_shapes=[pltpu.VMEM((tl * SUBLANES, nch * 2 * sw), F32),
                        pltpu.VMEM((SUBLANES, nch * 2 * sw), F32)],
        compiler_params=_cparams(("arbitrary",)),
        name="s5",
    )(*args)


def _bit_transpose32(words):
    x = list(words)
    j, m = 16, 0x0000FFFF
    while j:
        k = 0
        while k < 32:
            t = (x[k] ^ lax.shift_right_logical(x[k + j], jnp.int32(j))) & jnp.int32(m - (1 << 32) if m >= 1 << 31 else m)
            x[k] = x[k] ^ t
            x[k + j] = x[k + j] ^ lax.shift_left(t, jnp.int32(j))
            k = (k + j + 1) & ~j
        j >>= 1
        m = (m ^ (m << j)) & 0xFFFFFFFF
    return x


def _dsa_kernel(qt_ref, qit_ref, wit_ref, ka_ref, ki_ref, vt_ref, o_ref, key_s, mb_s, acc_s, pl_s, p_s, *, tq, tk, topk, seq):
    i = pl.program_id(1)
    q0 = i * tq
    nkt = (q0 + tq + tk - 1) // tk
    ch = DSA_COUNT_ROWS
    krow = lax.broadcasted_iota(I32, (tk, tq), 0)
    qcol = q0 + lax.broadcasted_iota(I32, (tk, tq), 1)
    crow = lax.broadcasted_iota(I32, (ch, tq), 0)

    wb = wit_ref[0] * (IDX_HEADS ** -0.5)

    def score_tile(j, _):
        r0 = pl.multiple_of(j * tk, tk)
        kit = ki_ref[0, pl.ds(r0, tk), :]
        acc = jnp.zeros((tk, tq), F32)
        for h in range(IDX_HEADS):
            s = _dot(kit, qit_ref[0, h * IDX_DIM:(h + 1) * IDX_DIM, :])
            acc = acc + wb[h:h + 1, :] * jnp.maximum(s, 0.0)
        bits = lax.bitcast_convert_type(acc, I32)
        key = jnp.where(bits < 0, bits ^ jnp.int32(0x7FFFFFFF), bits)
        key = jnp.where(acc == 0.0, 0, key)
        key = jnp.where(krow + r0 <= qcol, key, INT_MIN)
        key_s[pl.ds(r0, tk), :] = key
        ukey = key ^ INT_MIN
        for c in range(tk // BITSLICE_ROWS):
            words = [ukey[c * BITSLICE_ROWS + v * SUBLANES:c * BITSLICE_ROWS + (v + 1) * SUBLANES, :]
                     for v in range(32)]
            planes = _bit_transpose32(words)
            g0 = pl.multiple_of((j * (tk // BITSLICE_ROWS) + c) * SUBLANES, SUBLANES)
            for it in range(32):
                pl_s[it, pl.ds(g0, SUBLANES), :] = planes[it]
        return 0

    @pl.when((pl.program_id(0) == 0) & (i == 0))
    def _():
        pl_s[...] = jnp.zeros(pl_s.shape, I32)

    lax.fori_loop(0, nkt, score_tile, 0)

    def count(pred):
        def tile(j, cnt):
            for c in range(tk // ch):
                rr = pl.multiple_of(j * tk + c * ch, ch)
                cnt = cnt + jnp.where(pred(key_s[pl.ds(rr, ch), :], rr), 1, 0)
            return cnt
        cnt = lax.fori_loop(0, nkt, tile, jnp.zeros((ch, tq), I32))
        return jnp.sum(cnt.astype(F32), axis=0, keepdims=True)

    ngrp = seq // 32

    def lane_count(words):
        pc = lax.population_count(words).reshape(ngrp // SUBLANES, SUBLANES, tq)
        return jnp.sum(jnp.sum(pc, axis=0).astype(F32), axis=0, keepdims=True)

    def bit_step(it, carry):
        alive, above, ans_u = carry
        ones = alive & pl_s[it]
        cnt1 = lane_count(ones)
        take = above + cnt1 >= float(topk)
        alive = jnp.where(take, ones, alive ^ ones)
        above = jnp.where(take, above, above + cnt1)
        ans_u = jnp.where(take, ans_u | lax.shift_left(jnp.int32(1), 31 - it), ans_u)
        return alive, above, ans_u

    grow = lax.broadcasted_iota(I32, (ngrp, tq), 0)
    alive0 = jnp.where(grow < nkt * (tk // 32), -1, 0)
    alive, above, ans_u = lax.fori_loop(
        0, 32, bit_step, (alive0, jnp.zeros((1, tq), F32), jnp.zeros((1, tq), I32)))
    thr = jnp.maximum(ans_u ^ INT_MIN, INT_MIN + 1)
    cnt_ge = above + lane_count(alive)
    tied = jnp.where(ans_u != 0, cnt_ge, 0.0) > float(topk)
    has_ties = jnp.max(jnp.where(tied, 1.0, 0.0)) > 0.0

    def tie_cut():
        need = float(topk) - count(lambda kb, rr: kb > thr)
        nbits = max(1, (seq - 1).bit_length())

        def idx_step(b, x):
            cand = x | lax.shift_left(jnp.int32(1), nbits - 1 - b)
            below = count(lambda kb, rr: jnp.where(kb == thr, crow + rr, seq) < cand)
            return jnp.where(below < need, cand, x)

        x = lax.fori_loop(0, nbits, idx_step, jnp.zeros((1, tq), I32))
        return jnp.where(tied, x, seq)

    cut = lax.cond(has_ties, tie_cut, lambda: jnp.full((1, tq), seq, I32))

    def bias_tile(j, _):
        for c in range(tk // ch):
            rr = pl.multiple_of(j * tk + c * ch, ch)
            kb = key_s[pl.ds(rr, ch), :]
            tie_bias = jnp.where(crow + rr <= cut, 0.0, NEG_BIG)
            mb_s[pl.ds(rr, ch), :] = jnp.where(kb > thr, 0.0, jnp.where(kb == thr, tie_bias, NEG_BIG))
        return 0

    lax.fori_loop(0, nkt, bias_tile, 0)

    def logits(j, h):
        r0 = pl.multiple_of(j * tk, tk)
        s = _dot(ka_ref[0, pl.ds(r0, tk), :], qt_ref[0, h * LANES:(h + 1) * LANES, :]) + mb_s[pl.ds(r0, tk), :]
        return s.reshape(tk // SUBLANES, SUBLANES, tq)

    def max_tile(j, ms):
        return tuple(jnp.maximum(ms[h], jnp.max(logits(j, h), axis=0)) for h in range(N_HEADS))

    ms = lax.fori_loop(0, nkt, max_tile, (jnp.full((SUBLANES, tq), NEG_BIG, F32),) * N_HEADS)
    m = [jnp.max(mh, axis=0, keepdims=True) for mh in ms]
    acc_s[...] = jnp.zeros(acc_s.shape, F32)

    def sum_tile(j, ls):
        r0 = pl.multiple_of(j * tk, tk)
        out = []
        for h in range(N_HEADS):
            p = jnp.exp(logits(j, h) - m[h])
            out.append(ls[h] + jnp.sum(p, axis=0))
            p_s[h] = _bf(p.reshape(tk, tq))
        for h in range(N_HEADS):
            rows = slice(h * HEAD_DIM, (h + 1) * HEAD_DIM)
            acc_s[rows, :] = acc_s[rows, :] + _dot(vt_ref[0, :, pl.ds(r0, tk)], p_s[h])
        return tuple(out)

    ls = lax.fori_loop(0, nkt, sum_tile, (jnp.zeros((SUBLANES, tq), F32),) * N_HEADS)
    for h in range(N_HEADS):
        rows = slice(h * HEAD_DIM, (h + 1) * HEAD_DIM)
        o_ref[0, rows, :] = _bf(acc_s[rows, :] / jnp.sum(ls[h], axis=0, keepdims=True))


def _dsa(q, qi, wit, k, ki, vt):
    bsz, seq, aw = q.shape
    tq = min(DSA_Q_COLS, seq)
    tk = min(DSA_K_ROWS, seq)
    topk = min(TOPK_MAX, seq // 4)
    assert (seq - 1) // POS_SPLIT < 256 and POS_SPLIT <= 256, "key positions must split into two bf16-exact parts"
    slopes = [2.0 ** (-8.0 * (h + 1) / N_HEADS) for h in range(N_HEADS)]
    assert all(float(np.float32(sl).astype(BF16)) == sl for sl in slopes), "ALiBi slopes must be bf16-exact"
    pos = jnp.arange(seq, dtype=I32)
    posc = jnp.stack([(pos // POS_SPLIT) * POS_SPLIT, pos % POS_SPLIT], axis=1).astype(BF16)
    ka = jnp.concatenate([k, jnp.broadcast_to(posc[None], (bsz, seq, 2)),
                          jnp.zeros((bsz, seq, LANES - HEAD_DIM - 2), BF16)], axis=2)
    qh = q.reshape(bsz, seq, N_HEADS, HEAD_DIM).transpose(0, 2, 3, 1)
    srow = jnp.asarray(slopes, BF16)[None, :, None, None]
    extra = jnp.concatenate([jnp.broadcast_to(srow, (bsz, N_HEADS, 2, seq)),
                             jnp.zeros((bsz, N_HEADS, LANES - HEAD_DIM - 2, seq), BF16)], axis=2)
    qt = jnp.concatenate([qh, extra], axis=2).reshape(bsz, N_HEADS * LANES, seq)
    qit = qi.transpose(0, 2, 1)
    kern = functools.partial(_dsa_kernel, tq=tq, tk=tk, topk=topk, seq=seq)
    cols = lambda r: pl.BlockSpec((1, r, tq), lambda b, i: (b, 0, i))
    return pl.pallas_call(
        kern,
        out_shape=jax.ShapeDtypeStruct((bsz, aw, seq), BF16),
        grid=(bsz, seq // tq),
        in_specs=[cols(N_HEADS * LANES), cols(qit.shape[1]), cols(IDX_HEADS),
                  pl.BlockSpec((1, seq, LANES), lambda b, i: (b, 0, 0)),
                  pl.BlockSpec((1, seq, IDX_DIM), lambda b, i: (b, 0, 0)),
                  pl.BlockSpec((1, HEAD_DIM, seq), lambda b, i: (b, 0, 0))],
        out_specs=cols(aw),
        scratch_shapes=[pltpu.VMEM((seq, tq), I32), pltpu.VMEM((seq, tq), F32), pltpu.VMEM((aw, tq), F32),
                        pltpu.VMEM((32, seq // 32, tq), I32), pltpu.VMEM((N_HEADS, tk, tq), BF16)],
        compiler_params=_cparams(("arbitrary", "arbitrary")),
        name="dsa",
    )(qt, qit, wit, ka, ki, vt)


def _mix_kernel(x_ref, ys_ref, ya_ref, g_ref, ada_ref, wps_ref, wpa_ref, wo_ref, g2_ref, h_ref, u2_ref, *, d):
    gate1 = ada_ref[0, :, 2 * d:3 * d]
    shift2 = ada_ref[0, :, 3 * d:4 * d]
    scale2 = ada_ref[0, :, 4 * d:5 * d]
    g = g_ref[0]
    mixed = (jax.nn.sigmoid(g[:, 0:d]) * _dot(ys_ref[0], wps_ref[...])
             + jax.nn.sigmoid(g[:, d:2 * d]) * _dot(ya_ref[0], wpa_ref[...]))
    h = x_ref[0] + gate1 * _dot(_bf(mixed), wo_ref[...])
    h_ref[0] = h
    u2_ref[0] = _rms(h, g2_ref[...]) * (1.0 + scale2) + shift2


def _mix(x, ys, ya, g, ada3, wps, wpa, wo, g2):
    bsz, seq, d = x.shape
    tm = MIX_ROWS
    row = lambda w: pl.BlockSpec((1, tm, w), lambda b, l: (b, l, 0))
    full = lambda a: pl.BlockSpec(a.shape, lambda b, l: (0,) * a.ndim)
    wps, wpa, wo = wps.astype(BF16), wpa.astype(BF16), wo.astype(BF16)
    g2 = g2.reshape(1, d)
    return pl.pallas_call(
        functools.partial(_mix_kernel, d=d),
        out_shape=(jax.ShapeDtypeStruct((bsz, seq, d), F32), jax.ShapeDtypeStruct((bsz, seq, d), F32)),
        grid=(bsz, seq // tm),
        in_specs=[row(d), row(ys.shape[2]), row(ya.shape[2]), row(2 * d),
                  pl.BlockSpec((1, 1, ada3.shape[2]), lambda b, l: (b, 0, 0)),
                  full(wps), full(wpa), full(wo), full(g2)],
        out_specs=(row(d), row(d)),
        compiler_params=_cparams(("arbitrary", "arbitrary")),
        name="mix",
    )(x, ys, ya, g, ada3, wps, wpa, wo, g2)


def _first_max(cur, idx, axis, big):
    m = jnp.max(cur, axis=axis, keepdims=True)
    first = jnp.min(jnp.where(cur == m, idx, big), axis=axis, keepdims=True)
    return m, idx == first


def _route_kernel(u_ref, wrh_ref, wrl_ref, rb_ref, tri_ref, ltri_ref,
                  gt_ref, loc_ref, c8_ref, loff_ref, run0_ref, tot_ref, run_s, *, t):
    @pl.when(pl.program_id(0) == 0)
    def _():
        run_s[...] = jnp.zeros_like(run_s)

    uh, ul = _split(u_ref[...])
    logits = _dot_nt(wrh_ref[...], uh) + (_dot_nt(wrl_ref[...], uh) + _dot_nt(wrh_ref[...], ul))
    scores = jax.nn.sigmoid(logits)
    biased = scores + rb_ref[...]
    per_group = N_EXPERTS // N_GROUPS
    b3 = biased.reshape(N_GROUPS, per_group, t)
    i3 = lax.broadcasted_iota(I32, b3.shape, 1)
    m1, hit1 = _first_max(b3, i3, 1, per_group)
    m2 = jnp.max(jnp.where(hit1, -jnp.inf, b3), axis=1, keepdims=True)
    gs = (m1 + m2).reshape(N_GROUPS, t)
    gi = lax.broadcasted_iota(I32, gs.shape, 0)
    gsel = jnp.zeros(gs.shape, F32)
    for _ in range(TOPK_GROUPS):
        _, hit = _first_max(gs, gi, 0, N_GROUPS)
        gsel = jnp.where(hit, 1.0, gsel)
        gs = jnp.where(hit, -jnp.inf, gs)
    cur = jnp.where(gsel.reshape(N_GROUPS, 1, t) > 0.0, b3, -jnp.inf).reshape(N_EXPERTS, t)
    ei = lax.broadcasted_iota(I32, cur.shape, 0)
    hits = []
    gates = []
    for _ in range(TOP_K):
        _, hit = _first_max(cur, ei, 0, N_EXPERTS)
        hits.append(hit)
        gates.append(jnp.sum(jnp.where(hit, scores, 0.0), axis=0, keepdims=True))
        cur = jnp.where(hit, -jnp.inf, cur)
    gate = jnp.concatenate(gates, axis=0)
    gt_ref[...] = gate / jnp.sum(gate, axis=0, keepdims=True) * ROUTED_SCALE
    onehot = jnp.zeros(cur.shape, F32)
    for hit in hits:
        onehot = jnp.where(hit, 1.0, onehot)
    cnt = jnp.sum(onehot, axis=1, keepdims=True)
    c8 = jnp.floor((cnt + (SUBLANES - 1)) * (1.0 / SUBLANES)) * SUBLANES
    c8l = jnp.broadcast_to(c8, (N_EXPERTS, LANES))
    loff = _dot(ltri_ref[...], _bf(c8l))
    slot = _dot(_bf(onehot), tri_ref[...]) + loff[:, 0:1]
    loc_ref[...] = jnp.concatenate(
        [jnp.sum(jnp.where(hit, slot, 0.0), axis=0, keepdims=True) for hit in hits], axis=0).astype(I32)
    c8_ref[0] = c8l
    loff_ref[0] = loff
    run0_ref[0] = run_s[...]
    run_s[...] = run_s[...] + c8
    tot_ref[...] = run_s[...]


def _route(u2, w_router, router_bias):
    n, d = u2.shape
    t = min(MOE_TILE, n)
    nt = n // t
    wt = w_router.T
    wrh = wt.astype(BF16)
    wrl = (wt - wrh.astype(F32)).astype(BF16)
    tri = (jnp.arange(t)[:, None] < jnp.arange(t)[None, :]).astype(BF16)
    ex = jnp.arange(N_EXPERTS)
    ltri = (ex[None, :] < ex[:, None]).astype(BF16)
    full = lambda a: pl.BlockSpec(a.shape, lambda i: (0,) * a.ndim)
    col = pl.BlockSpec((TOP_K, t), lambda i: (0, i))
    tab = pl.BlockSpec((1, N_EXPERTS, LANES), lambda i: (i, 0, 0))
    tab_sds = jax.ShapeDtypeStruct((nt, N_EXPERTS, LANES), F32)
    rb = router_bias.reshape(N_EXPERTS, 1)
    return pl.pallas_call(
        functools.partial(_route_kernel, t=t),
        out_shape=(jax.ShapeDtypeStruct((TOP_K, n), F32), jax.ShapeDtypeStruct((TOP_K, n), I32),
                   tab_sds, tab_sds, tab_sds, jax.ShapeDtypeStruct((N_EXPERTS, LANES), F32)),
        grid=(nt,),
        in_specs=[pl.BlockSpec((t, d), lambda i: (i, 0)), full(wrh), full(wrl), full(rb), full(tri), full(ltri)],
        out_specs=(col, col, tab, tab, tab, pl.BlockSpec((N_EXPERTS, LANES), lambda i: (0, 0))),
        scratch_shapes=[pltpu.VMEM((N_EXPERTS, LANES), F32)],
        compiler_params=_cparams(("arbitrary",)),
        name="route",
    )(u2, wrh, wrl, rb, tri, ltri)


RUN_BITS = tuple(1 << b for b in reversed(range((MOE_TILE // SUBLANES).bit_length())))


def _for_each_run_piece(n8_ref, src_ref, dst_ref, tile, bits, fn):
    def per_expert(e, _):
        idx = tile * N_EXPERTS + e
        n8 = n8_ref[idx]
        src = src_ref[idx]
        dst = dst_ref[idx]
        for p in bits:
            off = (n8 & ~(2 * p - 1)) * SUBLANES

            @pl.when((n8 & p) != 0)
            def _(p=p, off=off):
                fn(pl.multiple_of(src + off, SUBLANES), pl.multiple_of(dst + off, SUBLANES), p * SUBLANES)
        return 0

    lax.fori_loop(0, N_EXPERTS, per_expert, 0)


def _dispatch_kernel(n8_ref, src_ref, dst_ref, zn8_ref, zdst_ref, u_ref, loc_ref, gate_ref, xs_hbm, gs_hbm,
                     lbuf, gbuf, zx, zg, sems, *, tt, nslot):
    i = pl.program_id(0)
    last = pl.num_programs(0) - 1
    par = i % 2

    def copies(slot, fn):
        def piece(s0, d0, rows):
            fn(pltpu.make_async_copy(lbuf.at[slot, pl.ds(s0, rows)], xs_hbm.at[pl.ds(d0, rows)], sems.at[slot]))
            fn(pltpu.make_async_copy(gbuf.at[slot, pl.ds(s0, rows)], gs_hbm.at[pl.ds(d0, rows)], sems.at[slot]))
        return piece

    @pl.when(i == 0)
    def _():
        zx[...] = jnp.zeros(zx.shape, F32)
        zg[...] = jnp.zeros(zg.shape, F32)

        def zero_piece(s0, d0, rows):
            cx = pltpu.make_async_copy(zx.at[pl.ds(0, rows)], xs_hbm.at[pl.ds(d0, rows)], sems.at[2])
            cg = pltpu.make_async_copy(zg.at[pl.ds(0, rows)], gs_hbm.at[pl.ds(d0, rows)], sems.at[2])
            cx.start()
            cg.start()
            cx.wait()
            cg.wait()

        zbits = tuple(b for b in RUN_BITS if b * SUBLANES < EXPERT_ROWS)
        _for_each_run_piece(zn8_ref, zdst_ref, zdst_ref, 0, zbits, zero_piece)

    ub = _bf(u_ref[...])
    ones = jnp.ones((tt, LANES), BF16)
    loc = loc_ref[...]
    gate = gate_ref[...]
    for c in range(nslot // SLOT_CHUNK):
        rows = c * SLOT_CHUNK + lax.broadcasted_iota(I32, (SLOT_CHUNK, tt), 0)
        perm = jnp.zeros((SLOT_CHUNK, tt), F32)
        pgate = jnp.zeros((SLOT_CHUNK, tt), F32)
        for k in range(TOP_K):
            eq = rows == loc[k:k + 1, :]
            perm = jnp.where(eq, 1.0, perm)
            pgate = jnp.where(eq, gate[k:k + 1, :], pgate)
        cs = slice(c * SLOT_CHUNK, (c + 1) * SLOT_CHUNK)
        lbuf[par, cs, :] = _dot(_bf(perm), ub)
        gh, gl = _split(pgate)
        gbuf[par, cs, :] = _dot(gh, ones) + _dot(gl, ones)

    _for_each_run_piece(n8_ref, src_ref, dst_ref, i, RUN_BITS, copies(par, lambda cp: cp.start()))

    @pl.when(i > 0)
    def _():
        _for_each_run_piece(n8_ref, src_ref, dst_ref, i - 1, RUN_BITS, copies(1 - par, lambda cp: cp.wait()))

    @pl.when(i == last)
    def _():
        _for_each_run_piece(n8_ref, src_ref, dst_ref, i, RUN_BITS, copies(par, lambda cp: cp.wait()))


def _dispatch(tabs, ztabs, u2, loc_t, gate_t, n_rows):
    n, d = u2.shape
    tt = min(MOE_TILE, n)
    nslot = TOP_K * tt + N_EXPERTS * SUBLANES
    assert nslot % SLOT_CHUNK == 0 and tt // SUBLANES == RUN_BITS[0]
    col = pl.BlockSpec((TOP_K, tt), lambda i, *_: (0, i))
    hbm = pl.BlockSpec(memory_space=pl.ANY)
    return pl.pallas_call(
        functools.partial(_dispatch_kernel, tt=tt, nslot=nslot),
        out_shape=(jax.ShapeDtypeStruct((n_rows, d), F32), jax.ShapeDtypeStruct((n_rows, LANES), F32)),
        grid_spec=pltpu.PrefetchScalarGridSpec(
            num_scalar_prefetch=5, grid=(n // tt,),
            in_specs=[pl.BlockSpec((tt, d), lambda i, *_: (i, 0)), col, col],
            out_specs=(hbm, hbm),
            scratch_shapes=[pltpu.VMEM((2, nslot, d), F32), pltpu.VMEM((2, nslot, LANES), F32),
                            pltpu.VMEM((EXPERT_ROWS // 2, d), F32), pltpu.VMEM((EXPERT_ROWS // 2, LANES), F32),
                            pltpu.SemaphoreType.DMA((3,))]),
        compiler_params=_cparams(("arbitrary",)),
        name="dispatch",
    )(*tabs, *ztabs, u2, loc_t, gate_t)


def _experts_kernel(be_ref, nu_ref, xs_ref, gs_ref, wg_ref, wu_ref, wd_ref, ys_ref, *, d):
    del be_ref

    @pl.when(pl.program_id(0) < nu_ref[0])
    def _():
        x = _bf(xs_ref[...])
        hmid = jax.nn.silu(_dot(x, wg_ref[0])) * _dot(x, wu_ref[0])
        ys_ref[...] = _dot(_bf(hmid), wd_ref[0]) * jnp.tile(gs_ref[...], (1, d // LANES))


def _experts(blk_expert, n_used, xs, gs, wg, wu, wd):
    rows, d = xs.shape
    de = wg.shape[2]
    nblk = rows // EXPERT_ROWS
    blk = lambda i, be, nu: jnp.minimum(i, nu[0] - 1)
    return pl.pallas_call(
        functools.partial(_experts_kernel, d=d),
        out_shape=jax.ShapeDtypeStruct((rows, d), F32),
        grid_spec=pltpu.PrefetchScalarGridSpec(
            num_scalar_prefetch=2, grid=(nblk,),
            in_specs=[pl.BlockSpec((EXPERT_ROWS, d), lambda i, be, nu: (blk(i, be, nu), 0)),
                      pl.BlockSpec((EXPERT_ROWS, LANES), lambda i, be, nu: (blk(i, be, nu), 0)),
                      pl.BlockSpec((1, d, de), lambda i, be, nu: (be[blk(i, be, nu)], 0, 0)),
                      pl.BlockSpec((1, d, de), lambda i, be, nu: (be[blk(i, be, nu)], 0, 0)),
                      pl.BlockSpec((1, de, d), lambda i, be, nu: (be[blk(i, be, nu)], 0, 0))],
            out_specs=pl.BlockSpec((EXPERT_ROWS, d), lambda i, be, nu: (blk(i, be, nu), 0))),
        compiler_params=_cparams(("arbitrary",)),
        name="experts",
    )(blk_expert, n_used, xs, gs, wg.astype(BF16), wu.astype(BF16), wd.astype(BF16))


def _combine_kernel(n8_ref, src_ref, dst_ref, ys_hbm, loc_ref, h_ref, u2_ref, ada_ref, wsg_ref, wsu_ref, wsd_ref,
                    gf_ref, o_ref, ybuf, sems, *, tt, nslot, d):
    i = pl.program_id(0)
    last = pl.num_programs(0) - 1
    par = i % 2

    def copies(slot, fn):
        def piece(s0, d0, rows):
            fn(pltpu.make_async_copy(ys_hbm.at[pl.ds(d0, rows)], ybuf.at[slot, pl.ds(s0, rows)], sems.at[slot]))
        return piece

    @pl.when(i == 0)
    def _():
        ybuf[...] = jnp.zeros(ybuf.shape, F32)
        _for_each_run_piece(n8_ref, src_ref, dst_ref, 0, RUN_BITS, copies(0, lambda cp: cp.start()))

    @pl.when(i < last)
    def _():
        _for_each_run_piece(n8_ref, src_ref, dst_ref, i + 1, RUN_BITS, copies(1 - par, lambda cp: cp.start()))

    x = _bf(u2_ref[...])
    shared = _dot(_bf(jax.nn.silu(_dot(x, wsg_ref[...])) * _dot(x, wsu_ref[...])), wsd_ref[...])
    _for_each_run_piece(n8_ref, src_ref, dst_ref, i, RUN_BITS, copies(par, lambda cp: cp.wait()))

    loc = loc_ref[...]
    routed = jnp.zeros((tt, d), F32)
    for c in range(nslot // SLOT_CHUNK):
        cols = c * SLOT_CHUNK + lax.broadcasted_iota(I32, (tt, SLOT_CHUNK), 1)
        pick = jnp.zeros((tt, SLOT_CHUNK), F32)
        for k in range(TOP_K):
            pick = jnp.where(cols == loc[:, k:k + 1], 1.0, pick)
        pick = _bf(pick)
        yh, yl = _split(ybuf[par, c * SLOT_CHUNK:(c + 1) * SLOT_CHUNK, :])
        routed = routed + (_dot(pick, yh) + _dot(pick, yl))
    gate2 = ada_ref[0, :, 5 * d:6 * d]
    h = h_ref[...] + gate2 * (routed + shared)
    o_ref[...] = _rms(h, gf_ref[...])


def _combine(tabs, ys, loc, h1, u2, ada3, wsg, wsu, wsd, gf, seq):
    n, d = h1.shape
    tt = min(MOE_TILE, n)
    nslot = TOP_K * tt + N_EXPERTS * SUBLANES
    per_b = seq // tt
    row = pl.BlockSpec((tt, d), lambda i, *_: (i, 0))
    full = lambda a: pl.BlockSpec(a.shape, lambda i, *_: (0,) * a.ndim)
    wsg, wsu, wsd = wsg.astype(BF16), wsu.astype(BF16), wsd.astype(BF16)
    gf = gf.reshape(1, d)
    return pl.pallas_call(
        functools.partial(_combine_kernel, tt=tt, nslot=nslot, d=d),
        out_shape=jax.ShapeDtypeStruct((n, d), F32),
        grid_spec=pltpu.PrefetchScalarGridSpec(
            num_scalar_prefetch=3, grid=(n // tt,),
            in_specs=[pl.BlockSpec(memory_space=pl.ANY),
                      pl.BlockSpec((tt, TOP_K), lambda i, *_: (i, 0)),
                      row, row,
                      pl.BlockSpec((1, 1, ada3.shape[2]), lambda i, *_: (i // per_b, 0, 0)),
                      full(wsg), full(wsu), full(wsd), full(gf)],
            out_specs=row,
            scratch_shapes=[pltpu.VMEM((2, nslot, d), F32), pltpu.SemaphoreType.DMA((2,))]),
        compiler_params=_cparams(("arbitrary",)),
        name="combine",
    )(*tabs, ys, loc, h1, u2, ada3, wsg, wsu, wsd, gf)


def _moe(h1, u2, ada3, w_router, router_bias, wg, wu, wd, wsg, wsu, wsd, gf):
    bsz, seq, d = h1.shape
    n = bsz * seq
    assert seq % min(MOE_TILE, n) == 0
    h1f = h1.reshape(n, d)
    u2f = u2.reshape(n, d)
    gate_t, loc_t, c8, loff, run0, tot = _route(u2f, w_router, router_bias)
    nt = c8.shape[0]
    as_tab = lambda a: a[:, :, 0].astype(I32)
    tot8 = tot[:, 0].astype(I32)
    padded = (tot8 + EXPERT_ROWS - 1) // EXPERT_ROWS * EXPERT_ROWS
    pend = jnp.cumsum(padded)
    pstart = (pend - padded).astype(I32)
    nblk = (n * TOP_K + nt * N_EXPERTS * (SUBLANES - 1) + N_EXPERTS * (EXPERT_ROWS - 1) + EXPERT_ROWS - 1) // EXPERT_ROWS
    blk_row0 = jnp.arange(nblk, dtype=I32) * EXPERT_ROWS
    blk_expert = jnp.minimum(jnp.sum(pend[None, :] <= blk_row0[:, None], axis=1), N_EXPERTS - 1).astype(I32)
    n_used = (pend[-1:] // EXPERT_ROWS).astype(I32)
    tabs = ((as_tab(c8) // SUBLANES).reshape(-1), as_tab(loff).reshape(-1),
            (pstart[None, :] + as_tab(run0)).reshape(-1))
    ztabs = ((padded - tot8) // SUBLANES, pstart + tot8)
    xs, gs = _dispatch(tabs, ztabs, u2f, loc_t, gate_t, nblk * EXPERT_ROWS)
    ys = _experts(blk_expert, n_used, xs, gs, wg, wu, wd)
    out = _combine(tabs, ys, loc_t.T, h1f, u2f, ada3, wsg, wsu, wsd, gf, seq)
    return out.reshape(bsz, seq, d)


def kernel(x, c, w_ada, b_ada, norm1_g, w_in, ssm_lambda_re, ssm_lambda_im, ssm_log_dt, ssm_b_re, ssm_b_im,
           ssm_c_re, ssm_c_im, ssm_d, ssm_w_glu, ssm_b_glu, w_proj_ssm, w_proj_attn, w_out, norm2_g, w_router,
           router_bias, w_exp_gate, w_exp_up, w_exp_down, w_sh_gate, w_sh_up, w_sh_down, norm_f_g):
    depth = w_ada.shape[0]
    assert depth == 1, "the final norm is fused into the last (only) layer's combine kernel"
    bsz, seq, d = x.shape
    layer = 0
    ada3 = _ada(c, w_ada[layer], b_ada[layer]).reshape(bsz, 1, 6 * d)
    us, q, qi, k, ki, vt, wit, g = _inproj(x, ada3, norm1_g[layer], w_in[layer])
    a_re, a_im, bb_re, bb_im = _s5disc(ssm_lambda_re[layer], ssm_lambda_im[layer], ssm_log_dt[layer],
                                       ssm_b_re[layer], ssm_b_im[layer])
    ys_t = _s5(us.transpose(1, 0, 2), a_re, a_im, bb_re, bb_im, ssm_c_re[layer], ssm_c_im[layer],
               ssm_d[layer], ssm_w_glu[layer], ssm_b_glu[layer])
    ya = _dsa(q, qi, wit, k, ki, vt).transpose(0, 2, 1)
    h1, u2 = _mix(x, ys_t.transpose(1, 0, 2), ya, g, ada3, w_proj_ssm[layer], w_proj_attn[layer],
                  w_out[layer], norm2_g[layer])
    return _moe(h1, u2, ada3, w_router[layer], router_bias[layer], w_exp_gate[layer], w_exp_up[layer],
                w_exp_down[layer], w_sh_gate[layer], w_sh_up[layer], w_sh_down[layer], norm_f_g)
```

```python
import functools
import math

import jax
import jax.numpy as jnp
import numpy as np
from jax import lax
from jax.experimental import pallas as pl
from jax.experimental.pallas import tpu as pltpu

F32 = jnp.float32
BF16 = jnp.bfloat16
I32 = jnp.int32

SSM_GROUP = 16
SSM_STATE = 64
N_HEADS = 8
HEAD_DIM = 64
IDX_HEADS = 8
IDX_DIM = 64
TOPK_MAX = 256
N_EXPERTS = 64
TOP_K = 8
N_GROUPS = 8
TOPK_GROUPS = 4
ROUTED_SCALE = 2.5
EPS = 1e-6

V7X_VMEM_LIMIT_BYTES = 56 * 1024 * 1024
LANES = 128
SUBLANES = 8

INPROJ_ROWS = 256
S5_STEPS = 64
S5_LANE_CHUNK = 128
DSA_Q_COLS = 256
DSA_K_ROWS = 512
DSA_COUNT_ROWS = 64
BITSLICE_ROWS = 256
POS_SPLIT = 64
MIX_ROWS = 256
MOE_TILE = 256
SLOT_CHUNK = 512
EXPERT_ROWS = 256

NEG_BIG = -1e30
INT_MIN = -(2 ** 31)


def _cparams(sem):
    return pltpu.CompilerParams(dimension_semantics=sem, vmem_limit_bytes=V7X_VMEM_LIMIT_BYTES)


def _bf(x):
    return x.astype(BF16)


def _dot(a, b):
    return jnp.dot(a, b, preferred_element_type=F32)


def _dot_nt(a, b):
    return lax.dot_general(a, b, (((1,), (1,)), ((), ())), preferred_element_type=F32)


def _split(x):
    hi = _bf(x)
    lo = _bf(x - hi.astype(F32))
    return hi, lo


def _dot3(a, b):
    ah, al = _split(a)
    bh, bl = _split(b)
    return _dot(ah, bh) + (_dot(ah, bl) + _dot(al, bh))


def _rms(x, g):
    return x * lax.rsqrt(jnp.mean(x * x, axis=-1, keepdims=True) + EPS) * g


def _ada_kernel(c_ref, w_ref, b_ref, o_ref):
    c = c_ref[...]
    o_ref[...] = _dot3(c * jax.nn.sigmoid(c), w_ref[...]) + b_ref[...]


def _ada(c, w, b):
    bsz, d = c.shape
    n = w.shape[1]
    tn = 1024
    return pl.pallas_call(
        _ada_kernel,
        out_shape=jax.ShapeDtypeStruct((bsz, n), F32),
        grid=(n // tn,),
        in_specs=[pl.BlockSpec((bsz, d), lambda j: (0, 0)),
                  pl.BlockSpec((d, tn), lambda j: (0, j)),
                  pl.BlockSpec((1, tn), lambda j: (0, j))],
        out_specs=pl.BlockSpec((bsz, tn), lambda j: (0, j)),
        compiler_params=_cparams(("arbitrary",)),
        name="ada",
    )(c, w, b.reshape(1, n))


def _inproj_kernel(x_ref, ada_ref, g1_ref, w_ref, wt_ref,
                   us_ref, q_ref, qi_ref, k_ref, ki_ref, vt_ref, wit_ref, g_ref, *, d, ssm_w, attn_w, idx_w):
    x = x_ref[0]
    shift = ada_ref[0, :, 0:d]
    scale = ada_ref[0, :, d:2 * d]
    u = _bf(_rms(x, g1_ref[...]) * (1.0 + scale) + shift)
    r = _dot(u, w_ref[...])
    o = 0
    us_ref[0] = r[:, o:o + ssm_w]
    o += ssm_w
    q_ref[0] = _bf(r[:, o:o + attn_w])
    o += attn_w
    qi_ref[0] = _bf(r[:, o:o + idx_w])
    o += idx_w
    k_ref[0] = _bf(r[:, o:o + HEAD_DIM])
    o += LANES
    ki_ref[0] = _bf(r[:, o:o + IDX_DIM])
    o += LANES
    g_ref[0] = r[:, o:o + 2 * d]
    rt = _dot_nt(wt_ref[...], u)
    vt_ref[0] = _bf(rt[0:HEAD_DIM])
    wit_ref[0] = rt[HEAD_DIM:HEAD_DIM + IDX_HEADS]


def _inproj(x, ada3, g1, w_in):
    bsz, seq, d = x.shape
    ssm_w = 512
    attn_w = N_HEADS * HEAD_DIM
    idx_w = IDX_HEADS * IDX_DIM
    sizes = (ssm_w, attn_w, HEAD_DIM, HEAD_DIM, idx_w, IDX_DIM, IDX_HEADS, d, d)
    offs = [0]
    for s in sizes:
        offs.append(offs[-1] + s)
    w_ssm, w_q, w_k, w_v, w_qi, w_ki, w_wi, w_gs, w_ga = [w_in[:, offs[i]:offs[i + 1]] for i in range(9)]
    zpad = lambda n: jnp.zeros((d, n), F32)
    wbig = jnp.concatenate([
        w_ssm, w_q * (HEAD_DIM ** -0.5), w_qi * (IDX_DIM ** -0.5),
        w_k, zpad(LANES - HEAD_DIM), w_ki, zpad(LANES - IDX_DIM), w_gs, w_ga], axis=1).astype(BF16)
    wt = jnp.concatenate([w_v, w_wi, zpad(LANES - HEAD_DIM - IDX_HEADS)], axis=1).T.astype(BF16)
    nw = wbig.shape[1]
    tl = INPROJ_ROWS
    kern = functools.partial(_inproj_kernel, d=d, ssm_w=ssm_w, attn_w=attn_w, idx_w=idx_w)
    row = lambda w: pl.BlockSpec((1, tl, w), lambda b, l: (b, l, 0))
    colt = lambda h: pl.BlockSpec((1, h, tl), lambda b, l: (b, 0, l))
    return pl.pallas_call(
        kern,
        out_shape=(jax.ShapeDtypeStruct((bsz, seq, ssm_w), F32),
                   jax.ShapeDtypeStruct((bsz, seq, attn_w), BF16),
                   jax.ShapeDtypeStruct((bsz, seq, idx_w), BF16),
                   jax.ShapeDtypeStruct((bsz, seq, HEAD_DIM), BF16),
                   jax.ShapeDtypeStruct((bsz, seq, IDX_DIM), BF16),
                   jax.ShapeDtypeStruct((bsz, HEAD_DIM, seq), BF16),
                   jax.ShapeDtypeStruct((bsz, IDX_HEADS, seq), F32),
                   jax.ShapeDtypeStruct((bsz, seq, 2 * d), F32)),
        grid=(bsz, seq // tl),
        in_specs=[row(d),
                  pl.BlockSpec((1, 1, ada3.shape[2]), lambda b, l: (b, 0, 0)),
                  pl.BlockSpec((1, d), lambda b, l: (0, 0)),
                  pl.BlockSpec((d, nw), lambda b, l: (0, 0)),
                  pl.BlockSpec((LANES, d), lambda b, l: (0, 0))],
        out_specs=(row(ssm_w), row(attn_w), row(idx_w), row(HEAD_DIM), row(IDX_DIM),
                   colt(HEAD_DIM), colt(IDX_HEADS), row(2 * d)),
        compiler_params=_cparams(("arbitrary", "arbitrary")),
        name="inproj",
    )(x, ada3, g1.reshape(1, d), wbig, wt)


def _s5disc_kernel(lr_ref, li_ref, ldt_ref, br_ref, bi_ref, are_ref, aim_ref, bbr_ref, bbi_ref):
    lr = lr_ref[...]
    li = li_ref[...]
    dt = jnp.exp(ldt_ref[...])
    mag = jnp.exp(lr * dt)
    a_re = mag * jnp.cos(li * dt)
    a_im = mag * jnp.sin(li * dt)
    den = lr * lr + li * li
    n_re = a_re - 1.0
    f_re = (n_re * lr + a_im * li) / den
    f_im = (a_im * lr - n_re * li) / den
    br = br_ref[...]
    bi = bi_ref[...]
    are_ref[...] = a_re
    aim_ref[...] = a_im
    bbr_ref[...] = f_re * br - f_im * bi
    bbi_ref[...] = f_re * bi + f_im * br


def _s5disc(lam_re, lam_im, log_dt, b_re, b_im):
    g, p = lam_re.shape
    h = b_re.shape[2]
    rep = lambda a: jnp.repeat(a, h, axis=1)
    ldt = jnp.broadcast_to(log_dt[:, None], (g, p * h))
    sds = jax.ShapeDtypeStruct((g, p * h), F32)
    a_re, a_im, bb_re, bb_im = pl.pallas_call(
        _s5disc_kernel, out_shape=(sds, sds, sds, sds), name="s5disc",
    )(rep(lam_re), rep(lam_im), ldt, b_re.reshape(g, p * h), b_im.reshape(g, p * h))
    return a_re[:, ::h], a_im[:, ::h], bb_re.reshape(g, p, h), bb_im.reshape(g, p, h)


def _s5_kernel(u_ref, wbh_ref, wbl_ref, ar_ref, ai_ref, cch_ref, ccl_ref, dsk_ref, wg_ref, bg_ref,
               o_ref, buf, hst, *, tl, width):
    nch = width // S5_LANE_CHUNK
    sw = S5_LANE_CHUNK // SSM_GROUP * SSM_STATE
    rows = tl * SUBLANES

    @pl.when(pl.program_id(0) == 0)
    def _():
        hst[...] = jnp.zeros_like(hst)

    u = u_ref[...].reshape(rows, width)
    uh, ul = _split(u)
    for j in range(nch):
        cs = slice(j * S5_LANE_CHUNK, (j + 1) * S5_LANE_CHUNK)
        buf[:, j * 2 * sw:(j + 1) * 2 * sw] = (
            _dot(uh[:, cs], wbh_ref[j]) + (_dot(uh[:, cs], wbl_ref[j]) + _dot(ul[:, cs], wbh_ref[j])))

    for j in range(nch):
        re_cols = slice(j * 2 * sw, j * 2 * sw + sw)
        im_cols = slice(j * 2 * sw + sw, (j + 1) * 2 * sw)
        a_re = jnp.broadcast_to(ar_ref[:, j * sw:(j + 1) * sw], (SUBLANES, sw))
        a_im = jnp.broadcast_to(ai_ref[:, j * sw:(j + 1) * sw], (SUBLANES, sw))

        def step(t, carry, re_cols=re_cols, im_cols=im_cols, a_re=a_re, a_im=a_im):
            h_re, h_im = carry
            r0 = pl.multiple_of(t * SUBLANES, SUBLANES)
            n_re = (a_re * h_re - a_im * h_im) + buf[pl.ds(r0, SUBLANES), re_cols]
            n_im = (a_re * h_im + a_im * h_re) + buf[pl.ds(r0, SUBLANES), im_cols]
            buf[pl.ds(r0, SUBLANES), re_cols] = n_re
            buf[pl.ds(r0, SUBLANES), im_cols] = n_im
            return n_re, n_im

        h_re, h_im = lax.fori_loop(0, tl, step, (hst[:, re_cols], hst[:, im_cols]), unroll=8)
        hst[:, re_cols] = h_re
        hst[:, im_cols] = h_im

    ys = []
    for j in range(nch):
        hh, hl = _split(buf[:, j * 2 * sw:(j + 1) * 2 * sw])
        ys.append(_dot(hh, cch_ref[j]) + (_dot(hh, ccl_ref[j]) + _dot(hl, cch_ref[j])))
    y = jnp.concatenate(ys, axis=1) + dsk_ref[...] * u
    y = jax.nn.gelu(y)
    y = y * jax.nn.sigmoid(_dot(_bf(y), wg_ref[...]) + bg_ref[...])
    o_ref[...] = _bf(y).reshape(tl, SUBLANES, width)


def _s5(u_t, a_re, a_im, bb_re, bb_im, c_re, c_im, d_skip, w_glu, b_glu):
    seq, bsz, width = u_t.shape
    assert bsz == SUBLANES
    nch = width // S5_LANE_CHUNK
    gpc = S5_LANE_CHUNK // SSM_GROUP
    sw = gpc * SSM_STATE
    eye = jnp.eye(gpc, dtype=F32)

    def bmat(bb):
        t = bb.reshape(nch, gpc, SSM_STATE, SSM_GROUP).transpose(0, 1, 3, 2)
        return jnp.einsum('jghp,gk->jghkp', t, eye).reshape(nch, S5_LANE_CHUNK, sw)

    def cmat(cc):
        t = cc.reshape(nch, gpc, SSM_GROUP, SSM_STATE).transpose(0, 1, 3, 2)
        return jnp.einsum('jgph,gk->jgpkh', t, eye).reshape(nch, sw, S5_LANE_CHUNK)

    wb = jnp.concatenate([bmat(bb_re), bmat(bb_im)], axis=2)
    cc = jnp.concatenate([cmat(c_re), -cmat(c_im)], axis=1)
    wbh = wb.astype(BF16)
    wbl = (wb - wbh.astype(F32)).astype(BF16)
    cch = cc.astype(BF16)
    ccl = (cc - cch.astype(F32)).astype(BF16)
    tl = S5_STEPS
    full = lambda a: pl.BlockSpec(a.shape, lambda i: (0,) * a.ndim)
    args = (u_t, wbh, wbl, a_re.reshape(1, -1), a_im.reshape(1, -1), cch, ccl,
            d_skip.reshape(1, width), w_glu.astype(BF16), b_glu.reshape(1, width))
    return pl.pallas_call(
        functools.partial(_s5_kernel, tl=tl, width=width),
        out_shape=jax.ShapeDtypeStruct((seq, bsz, width), BF16),
        grid=(seq // tl,),
        in_specs=[pl.BlockSpec((tl, bsz, width), lambda i: (i, 0, 0))] + [full(a) for a in args[1:]],
        out_specs=pl.BlockSpec((tl, bsz, width), lambda i: (i, 0, 0)),
        scratch_shapes=[pltpu.VMEM((tl * SUBLANES, nch * 2 * sw), F32),
                        pltpu.VMEM((SUBLANES, nch * 2 * sw), F32)],
        compiler_params=_cparams(("arbitrary",)),
        name="s5",
    )(*args)


def _bit_transpose32(words):
    x = list(words)
    j, m = 16, 0x0000FFFF
    while j:
        k = 0
        while k < 32:
            t = (x[k] ^ lax.shift_right_logical(x[k + j], jnp.int32(j))) & jnp.int32(m - (1 << 32) if m >= 1 << 31 else m)
            x[k] = x[k] ^ t
            x[k + j] = x[k + j] ^ lax.shift_left(t, jnp.int32(j))
            k = (k + j + 1) & ~j
        j >>= 1
        m = (m ^ (m << j)) & 0xFFFFFFFF
    return x


def _dsa_kernel(qt_ref, qit_ref, wit_ref, ka_ref, ki_ref, vt_ref, o_ref, key_s, mb_s, acc_s, pl_s, p_s, *, tq, tk, topk, seq):
    i = pl.program_id(1)
    q0 = i * tq
    nkt = (q0 + tq + tk - 1) // tk
    ch = DSA_COUNT_ROWS
    krow = lax.broadcasted_iota(I32, (tk, tq), 0)
    qcol = q0 + lax.broadcasted_iota(I32, (tk, tq), 1)
    crow = lax.broadcasted_iota(I32, (ch, tq), 0)

    wb = wit_ref[0] * (IDX_HEADS ** -0.5)

    def score_tile(j, _):
        r0 = pl.multiple_of(j * tk, tk)
        kit = ki_ref[0, pl.ds(r0, tk), :]
        acc = jnp.zeros((tk, tq), F32)
        for h in range(IDX_HEADS):
            s = _dot(kit, qit_ref[0, h * IDX_DIM:(h + 1) * IDX_DIM, :])
            acc = acc + wb[h:h + 1, :] * jnp.maximum(s, 0.0)
        bits = lax.bitcast_convert_type(acc, I32)
        key = jnp.where(bits < 0, bits ^ jnp.int32(0x7FFFFFFF), bits)
        key = jnp.where(acc == 0.0, 0, key)
        key = jnp.where(krow + r0 <= qcol, key, INT_MIN)
        key_s[pl.ds(r0, tk), :] = key
        ukey = key ^ INT_MIN
        for c in range(tk // BITSLICE_ROWS):
            words = [ukey[c * BITSLICE_ROWS + v * SUBLANES:c * BITSLICE_ROWS + (v + 1) * SUBLANES, :]
                     for v in range(32)]
            planes = _bit_transpose32(words)
            g0 = pl.multiple_of((j * (tk // BITSLICE_ROWS) + c) * SUBLANES, SUBLANES)
            for it in range(32):
                pl_s[it, pl.ds(g0, SUBLANES), :] = planes[it]
        return 0

    @pl.when((pl.program_id(0) == 0) & (i == 0))
    def _():
        pl_s[...] = jnp.zeros(pl_s.shape, I32)

    lax.fori_loop(0, nkt, score_tile, 0)

    def count(pred):
        def tile(j, cnt):
            for c in range(tk // ch):
                rr = pl.multiple_of(j * tk + c * ch, ch)
                cnt = cnt + jnp.where(pred(key_s[pl.ds(rr, ch), :], rr), 1, 0)
            return cnt
        cnt = lax.fori_loop(0, nkt, tile, jnp.zeros((ch, tq), I32))
        return jnp.sum(cnt.astype(F32), axis=0, keepdims=True)

    ngrp = seq // 32

    def lane_count(words):
        pc = lax.population_count(words).reshape(ngrp // SUBLANES, SUBLANES, tq)
        return jnp.sum(jnp.sum(pc, axis=0).astype(F32), axis=0, keepdims=True)

    def bit_step(it, carry):
        alive, above, ans_u = carry
        ones = alive & pl_s[it]
        cnt1 = lane_count(ones)
        take = above + cnt1 >= float(topk)
        alive = jnp.where(take, ones, alive ^ ones)
        above = jnp.where(take, above, above + cnt1)
        ans_u = jnp.where(take, ans_u | lax.shift_left(jnp.int32(1), 31 - it), ans_u)
        return alive, above, ans_u

    grow = lax.broadcasted_iota(I32, (ngrp, tq), 0)
    alive0 = jnp.where(grow < nkt * (tk // 32), -1, 0)
    alive, above, ans_u = lax.fori_loop(
        0, 32, bit_step, (alive0, jnp.zeros((1, tq), F32), jnp.zeros((1, tq), I32)))
    thr = jnp.maximum(ans_u ^ INT_MIN, INT_MIN + 1)
    cnt_ge = above + lane_count(alive)
    tied = jnp.where(ans_u != 0, cnt_ge, 0.0) > float(topk)
    has_ties = jnp.max(jnp.where(tied, 1.0, 0.0)) > 0.0

    def tie_cut():
        need = float(topk) - count(lambda kb, rr: kb > thr)
        nbits = max(1, (seq - 1).bit_length())

        def idx_step(b, x):
            cand = x | lax.shift_left(jnp.int32(1), nbits - 1 - b)
            below = count(lambda kb, rr: jnp.where(kb == thr, crow + rr, seq) < cand)
            return jnp.where(below < need, cand, x)

        x = lax.fori_loop(0, nbits, idx_step, jnp.zeros((1, tq), I32))
        return jnp.where(tied, x, seq)

    cut = lax.cond(has_ties, tie_cut, lambda: jnp.full((1, tq), seq, I32))

    def bias_tile(j, _):
        for c in range(tk // ch):
            rr = pl.multiple_of(j * tk + c * ch, ch)
            kb = key_s[pl.ds(rr, ch), :]
            tie_bias = jnp.where(crow + rr <= cut, 0.0, NEG_BIG)
            mb_s[pl.ds(rr, ch), :] = jnp.where(kb > thr, 0.0, jnp.where(kb == thr, tie_bias, NEG_BIG))
        return 0

    lax.fori_loop(0, nkt, bias_tile, 0)

    def logits(j, h):
        r0 = pl.multiple_of(j * tk, tk)
        s = _dot(ka_ref[0, pl.ds(r0, tk), :], qt_ref[0, h * LANES:(h + 1) * LANES, :]) + mb_s[pl.ds(r0, tk), :]
        return s.reshape(tk // SUBLANES, SUBLANES, tq)

    def max_tile(j, ms):
        return tuple(jnp.maximum(ms[h], jnp.max(logits(j, h), axis=0)) for h in range(N_HEADS))

    ms = lax.fori_loop(0, nkt, max_tile, (jnp.full((SUBLANES, tq), NEG_BIG, F32),) * N_HEADS)
    m = [jnp.max(mh, axis=0, keepdims=True) for mh in ms]
    acc_s[...] = jnp.zeros(acc_s.shape, F32)

    def sum_tile(j, ls):
        r0 = pl.multiple_of(j * tk, tk)
        out = []
        for h in range(N_HEADS):
            p = jnp.exp(logits(j, h) - m[h])
            out.append(ls[h] + jnp.sum(p, axis=0))
            p_s[h] = _bf(p.reshape(tk, tq))
        for h in range(N_HEADS):
            rows = slice(h * HEAD_DIM, (h + 1) * HEAD_DIM)
            acc_s[rows, :] = acc_s[rows, :] + _dot(vt_ref[0, :, pl.ds(r0, tk)], p_s[h])
        return tuple(out)

    ls = lax.fori_loop(0, nkt, sum_tile, (jnp.zeros((SUBLANES, tq), F32),) * N_HEADS)
    for h in range(N_HEADS):
        rows = slice(h * HEAD_DIM, (h + 1) * HEAD_DIM)
        o_ref[0, rows, :] = _bf(acc_s[rows, :] / jnp.sum(ls[h], axis=0, keepdims=True))


def _dsa(q, qi, wit, k, ki, vt):
    bsz, seq, aw = q.shape
    tq = min(DSA_Q_COLS, seq)
    tk = min(DSA_K_ROWS, seq)
    topk = min(TOPK_MAX, seq // 4)
    assert (seq - 1) // POS_SPLIT < 256 and POS_SPLIT <= 256, "key positions must split into two bf16-exact parts"
    slopes = [2.0 ** (-8.0 * (h + 1) / N_HEADS) for h in range(N_HEADS)]
    assert all(float(np.float32(sl).astype(BF16)) == sl for sl in slopes), "ALiBi slopes must be bf16-exact"
    pos = jnp.arange(seq, dtype=I32)
    posc = jnp.stack([(pos // POS_SPLIT) * POS_SPLIT, pos % POS_SPLIT], axis=1).astype(BF16)
    ka = jnp.concatenate([k, jnp.broadcast_to(posc[None], (bsz, seq, 2)),
                          jnp.zeros((bsz, seq, LANES - HEAD_DIM - 2), BF16)], axis=2)
    qh = q.reshape(bsz, seq, N_HEADS, HEAD_DIM).transpose(0, 2, 3, 1)
    srow = jnp.asarray(slopes, BF16)[None, :, None, None]
    extra = jnp.concatenate([jnp.broadcast_to(srow, (bsz, N_HEADS, 2, seq)),
                             jnp.zeros((bsz, N_HEADS, LANES - HEAD_DIM - 2, seq), BF16)], axis=2)
    qt = jnp.concatenate([qh, extra], axis=2).reshape(bsz, N_HEADS * LANES, seq)
    qit = qi.transpose(0, 2, 1)
    kern = functools.partial(_dsa_kernel, tq=tq, tk=tk, topk=topk, seq=seq)
    cols = lambda r: pl.BlockSpec((1, r, tq), lambda b, i: (b, 0, i))
    return pl.pallas_call(
        kern,
        out_shape=jax.ShapeDtypeStruct((bsz, aw, seq), BF16),
        grid=(bsz, seq // tq),
        in_specs=[cols(N_HEADS * LANES), cols(qit.shape[1]), cols(IDX_HEADS),
                  pl.BlockSpec((1, seq, LANES), lambda b, i: (b, 0, 0)),
                  pl.BlockSpec((1, seq, IDX_DIM), lambda b, i: (b, 0, 0)),
                  pl.BlockSpec((1, HEAD_DIM, seq), lambda b, i: (b, 0, 0))],
        out_specs=cols(aw),
        scratch_shapes=[pltpu.VMEM((seq, tq), I32), pltpu.VMEM((seq, tq), F32), pltpu.VMEM((aw, tq), F32),
                        pltpu.VMEM((32, seq // 32, tq), I32), pltpu.VMEM((N_HEADS, tk, tq), BF16)],
        compiler_params=_cparams(("arbitrary", "arbitrary")),
        name="dsa",
    )(qt, qit, wit, ka, ki, vt)


def _mix_kernel(x_ref, ys_ref, ya_ref, g_ref, ada_ref, wps_ref, wpa_ref, wo_ref, g2_ref, h_ref, u2_ref, *, d):
    gate1 = ada_ref[0, :, 2 * d:3 * d]
    shift2 = ada_ref[0, :, 3 * d:4 * d]
    scale2 = ada_ref[0, :, 4 * d:5 * d]
    g = g_ref[0]
    mixed = (jax.nn.sigmoid(g[:, 0:d]) * _dot(ys_ref[0], wps_ref[...])
             + jax.nn.sigmoid(g[:, d:2 * d]) * _dot(ya_ref[0], wpa_ref[...]))
    h = x_ref[0] + gate1 * _dot(_bf(mixed), wo_ref[...])
    h_ref[0] = h
    u2_ref[0] = _rms(h, g2_ref[...]) * (1.0 + scale2) + shift2


def _mix(x, ys, ya, g, ada3, wps, wpa, wo, g2):
    bsz, seq, d = x.shape
    tm = MIX_ROWS
    row = lambda w: pl.BlockSpec((1, tm, w), lambda b, l: (b, l, 0))
    full = lambda a: pl.BlockSpec(a.shape, lambda b, l: (0,) * a.ndim)
    wps, wpa, wo = wps.astype(BF16), wpa.astype(BF16), wo.astype(BF16)
    g2 = g2.reshape(1, d)
    return pl.pallas_call(
        functools.partial(_mix_kernel, d=d),
        out_shape=(jax.ShapeDtypeStruct((bsz, seq, d), F32), jax.ShapeDtypeStruct((bsz, seq, d), F32)),
        grid=(bsz, seq // tm),
        in_specs=[row(d), row(ys.shape[2]), row(ya.shape[2]), row(2 * d),
                  pl.BlockSpec((1, 1, ada3.shape[2]), lambda b, l: (b, 0, 0)),
                  full(wps), full(wpa), full(wo), full(g2)],
        out_specs=(row(d), row(d)),
        compiler_params=_cparams(("arbitrary", "arbitrary")),
        name="mix",
    )(x, ys, ya, g, ada3, wps, wpa, wo, g2)


def _first_max(cur, idx, axis, big):
    m = jnp.max(cur, axis=axis, keepdims=True)
    first = jnp.min(jnp.where(cur == m, idx, big), axis=axis, keepdims=True)
    return m, idx == first


def _route_kernel(u_ref, wrh_ref, wrl_ref, rb_ref, tri_ref, ltri_ref,
                  gt_ref, loc_ref, c8_ref, loff_ref, run0_ref, tot_ref, run_s, *, t):
    @pl.when(pl.program_id(0) == 0)
    def _():
        run_s[...] = jnp.zeros_like(run_s)

    uh, ul = _split(u_ref[...])
    logits = _dot_nt(wrh_ref[...], uh) + (_dot_nt(wrl_ref[...], uh) + _dot_nt(wrh_ref[...], ul))
    scores = jax.nn.sigmoid(logits)
    biased = scores + rb_ref[...]
    per_group = N_EXPERTS // N_GROUPS
    b3 = biased.reshape(N_GROUPS, per_group, t)
    i3 = lax.broadcasted_iota(I32, b3.shape, 1)
    m1, hit1 = _first_max(b3, i3, 1, per_group)
    m2 = jnp.max(jnp.where(hit1, -jnp.inf, b3), axis=1, keepdims=True)
    gs = (m1 + m2).reshape(N_GROUPS, t)
    gi = lax.broadcasted_iota(I32, gs.shape, 0)
    gsel = jnp.zeros(gs.shape, F32)
    for _ in range(TOPK_GROUPS):
        _, hit = _first_max(gs, gi, 0, N_GROUPS)
        gsel = jnp.where(hit, 1.0, gsel)
        gs = jnp.where(hit, -jnp.inf, gs)
    cur = jnp.where(gsel.reshape(N_GROUPS, 1, t) > 0.0, b3, -jnp.inf).reshape(N_EXPERTS, t)
    ei = lax.broadcasted_iota(I32, cur.shape, 0)
    hits = []
    gates = []
    for _ in range(TOP_K):
        _, hit = _first_max(cur, ei, 0, N_EXPERTS)
        hits.append(hit)
        gates.append(jnp.sum(jnp.where(hit, scores, 0.0), axis=0, keepdims=True))
        cur = jnp.where(hit, -jnp.inf, cur)
    gate = jnp.concatenate(gates, axis=0)
    gt_ref[...] = gate / jnp.sum(gate, axis=0, keepdims=True) * ROUTED_SCALE
    onehot = jnp.zeros(cur.shape, F32)
    for hit in hits:
        onehot = jnp.where(hit, 1.0, onehot)
    cnt = jnp.sum(onehot, axis=1, keepdims=True)
    c8 = jnp.floor((cnt + (SUBLANES - 1)) * (1.0 / SUBLANES)) * SUBLANES
    c8l = jnp.broadcast_to(c8, (N_EXPERTS, LANES))
    loff = _dot(ltri_ref[...], _bf(c8l))
    slot = _dot(_bf(onehot), tri_ref[...]) + loff[:, 0:1]
    loc_ref[...] = jnp.concatenate(
        [jnp.sum(jnp.where(hit, slot, 0.0), axis=0, keepdims=True) for hit in hits], axis=0).astype(I32)
    c8_ref[0] = c8l
    loff_ref[0] = loff
    run0_ref[0] = run_s[...]
    run_s[...] = run_s[...] + c8
    tot_ref[...] = run_s[...]


def _route(u2, w_router, router_bias):
    n, d = u2.shape
    t = min(MOE_TILE, n)
    nt = n // t
    wt = w_router.T
    wrh = wt.astype(BF16)
    wrl = (wt - wrh.astype(F32)).astype(BF16)
    tri = (jnp.arange(t)[:, None] < jnp.arange(t)[None, :]).astype(BF16)
    ex = jnp.arange(N_EXPERTS)
    ltri = (ex[None, :] < ex[:, None]).astype(BF16)
    full = lambda a: pl.BlockSpec(a.shape, lambda i: (0,) * a.ndim)
    col = pl.BlockSpec((TOP_K, t), lambda i: (0, i))
    tab = pl.BlockSpec((1, N_EXPERTS, LANES), lambda i: (i, 0, 0))
    tab_sds = jax.ShapeDtypeStruct((nt, N_EXPERTS, LANES), F32)
    rb = router_bias.reshape(N_EXPERTS, 1)
    return pl.pallas_call(
        functools.partial(_route_kernel, t=t),
        out_shape=(jax.ShapeDtypeStruct((TOP_K, n), F32), jax.ShapeDtypeStruct((TOP_K, n), I32),
                   tab_sds, tab_sds, tab_sds, jax.ShapeDtypeStruct((N_EXPERTS, LANES), F32)),
        grid=(nt,),
        in_specs=[pl.BlockSpec((t, d), lambda i: (i, 0)), full(wrh), full(wrl), full(rb), full(tri), full(ltri)],
        out_specs=(col, col, tab, tab, tab, pl.BlockSpec((N_EXPERTS, LANES), lambda i: (0, 0))),
        scratch_shapes=[pltpu.VMEM((N_EXPERTS, LANES), F32)],
        compiler_params=_cparams(("arbitrary",)),
        name="route",
    )(u2, wrh, wrl, rb, tri, ltri)


RUN_BITS = tuple(1 << b for b in reversed(range((MOE_TILE // SUBLANES).bit_length())))


def _for_each_run_piece(n8_ref, src_ref, dst_ref, tile, bits, fn):
    def per_expert(e, _):
        idx = tile * N_EXPERTS + e
        n8 = n8_ref[idx]
        src = src_ref[idx]
        dst = dst_ref[idx]
        for p in bits:
            off = (n8 & ~(2 * p - 1)) * SUBLANES

            @pl.when((n8 & p) != 0)
            def _(p=p, off=off):
                fn(pl.multiple_of(src + off, SUBLANES), pl.multiple_of(dst + off, SUBLANES), p * SUBLANES)
        return 0

    lax.fori_loop(0, N_EXPERTS, per_expert, 0)


def _wait_rows(n8, make_copy, max_rows):
    for p in tuple(1 << b for b in reversed(range((max_rows // SUBLANES).bit_length()))):
        @pl.when((n8 & p) != 0)
        def _(p=p):
            make_copy(p * SUBLANES).wait()


def _dispatch_kernel(n8_ref, src_ref, dst_ref, tot_ref, zn8_ref, zdst_ref, u_ref, loc_ref, gate_ref, xs_hbm,
                     lbuf, zx, sems, *, tt, nslot, dh):
    i = pl.program_id(0)
    last = pl.num_programs(0) - 1
    par = i % 2

    def run_copy(slot):
        def piece(s0, d0, rows):
            pltpu.make_async_copy(lbuf.at[slot, pl.ds(s0, rows)], xs_hbm.at[pl.ds(d0, rows)], sems.at[slot]).start()
        return piece

    def wait_tile(tile, slot):
        _wait_rows(tot_ref[tile], lambda rows: pltpu.make_async_copy(
            lbuf.at[slot, pl.ds(0, rows)], xs_hbm.at[pl.ds(0, rows)], sems.at[slot]), nslot)

    @pl.when(i == 0)
    def _():
        zx[...] = jnp.zeros(zx.shape, I32)

        def zero_piece(s0, d0, rows):
            cx = pltpu.make_async_copy(zx.at[pl.ds(0, rows)], xs_hbm.at[pl.ds(d0, rows)], sems.at[2])
            cx.start()
            cx.wait()

        zbits = tuple(b for b in RUN_BITS if b * SUBLANES < EXPERT_ROWS)
        _for_each_run_piece(zn8_ref, zdst_ref, zdst_ref, 0, zbits, zero_piece)

    ub = _bf(u_ref[...])
    ones = jnp.ones((tt, LANES), BF16)
    loc = loc_ref[...]
    gate = gate_ref[...]
    for c in range(nslot // SLOT_CHUNK):
        rows = c * SLOT_CHUNK + lax.broadcasted_iota(I32, (SLOT_CHUNK, tt), 0)
        perm = jnp.zeros((SLOT_CHUNK, tt), F32)
        pgate = jnp.zeros((SLOT_CHUNK, tt), F32)
        for k in range(TOP_K):
            eq = rows == loc[k:k + 1, :]
            perm = jnp.where(eq, 1.0, perm)
            pgate = jnp.where(eq, gate[k:k + 1, :], pgate)
        cs = slice(c * SLOT_CHUNK, (c + 1) * SLOT_CHUNK)
        xp = lax.bitcast_convert_type(_dot(_bf(perm), ub), I32)
        lbuf[par, cs, 0:dh] = xp[:, 0:dh] | lax.shift_right_logical(xp[:, dh:2 * dh], 16)
        gh, gl = _split(pgate)
        lbuf[par, cs, dh:dh + LANES] = lax.bitcast_convert_type(_dot(gh, ones) + _dot(gl, ones), I32)

    _for_each_run_piece(n8_ref, src_ref, dst_ref, i, RUN_BITS, run_copy(par))

    @pl.when(i > 0)
    def _():
        wait_tile(i - 1, 1 - par)

    @pl.when(i == last)
    def _():
        wait_tile(i, par)


def _dispatch(tabs, ztabs, u2, loc_t, gate_t, n_rows):
    n, d = u2.shape
    tt = min(MOE_TILE, n)
    nslot = TOP_K * tt + N_EXPERTS * SUBLANES
    dh = d // 2
    assert nslot % SLOT_CHUNK == 0 and tt // SUBLANES == RUN_BITS[0]
    col = pl.BlockSpec((TOP_K, tt), lambda i, *_: (0, i))
    return pl.pallas_call(
        functools.partial(_dispatch_kernel, tt=tt, nslot=nslot, dh=dh),
        out_shape=jax.ShapeDtypeStruct((n_rows, dh + LANES), I32),
        grid_spec=pltpu.PrefetchScalarGridSpec(
            num_scalar_prefetch=6, grid=(n // tt,),
            in_specs=[pl.BlockSpec((tt, d), lambda i, *_: (i, 0)), col, col],
            out_specs=pl.BlockSpec(memory_space=pl.ANY),
            scratch_shapes=[pltpu.VMEM((2, nslot, dh + LANES), I32),
                            pltpu.VMEM((EXPERT_ROWS // 2, dh + LANES), I32),
                            pltpu.SemaphoreType.DMA((3,))]),
        compiler_params=_cparams(("arbitrary",)),
        name="dispatch",
    )(*tabs, *ztabs, u2, loc_t, gate_t)


def _experts_kernel(be_ref, nu_ref, xs_ref, wg_ref, wu_ref, wd_ref, ys_ref, *, d):
    del be_ref
    dh = d // 2

    @pl.when(pl.program_id(0) < nu_ref[0])
    def _():
        w = xs_ref[:, 0:dh]
        xa = _bf(lax.bitcast_convert_type(w & jnp.int32(-65536), F32))
        xb = _bf(lax.bitcast_convert_type(lax.shift_left(w, 16), F32))
        gate = lax.bitcast_convert_type(xs_ref[:, dh:dh + LANES], F32)
        hg = _dot(xa, wg_ref[0, 0:dh, :]) + _dot(xb, wg_ref[0, dh:d, :])
        hu = _dot(xa, wu_ref[0, 0:dh, :]) + _dot(xb, wu_ref[0, dh:d, :])
        ys_ref[...] = _dot(_bf(jax.nn.silu(hg) * hu), wd_ref[0]) * jnp.tile(gate, (1, d // LANES))


def _experts(blk_expert, n_used, xs, wg, wu, wd):
    rows, xw = xs.shape
    d = wg.shape[1]
    de = wg.shape[2]
    nblk = rows // EXPERT_ROWS
    blk = lambda i, be, nu: jnp.minimum(i, nu[0] - 1)
    return pl.pallas_call(
        functools.partial(_experts_kernel, d=d),
        out_shape=jax.ShapeDtypeStruct((rows, d), F32),
        grid_spec=pltpu.PrefetchScalarGridSpec(
            num_scalar_prefetch=2, grid=(nblk,),
            in_specs=[pl.BlockSpec((EXPERT_ROWS, xw), lambda i, be, nu: (blk(i, be, nu), 0)),
                      pl.BlockSpec((1, d, de), lambda i, be, nu: (be[blk(i, be, nu)], 0, 0)),
                      pl.BlockSpec((1, d, de), lambda i, be, nu: (be[blk(i, be, nu)], 0, 0)),
                      pl.BlockSpec((1, de, d), lambda i, be, nu: (be[blk(i, be, nu)], 0, 0))],
            out_specs=pl.BlockSpec((EXPERT_ROWS, d), lambda i, be, nu: (blk(i, be, nu), 0))),
        compiler_params=_cparams(("arbitrary",)),
        name="experts",
    )(blk_expert, n_used, xs, wg.astype(BF16), wu.astype(BF16), wd.astype(BF16))


def _combine_kernel(n8_ref, src_ref, dst_ref, tot_ref, ys_hbm, loc_ref, h_ref, u2_ref, ada_ref, wsg_ref, wsu_ref, wsd_ref,
                    gf_ref, o_ref, ybuf, sems, *, tt, nslot, d):
    i = pl.program_id(0)
    last = pl.num_programs(0) - 1
    par = i % 2

    def run_copy(slot):
        def piece(s0, d0, rows):
            pltpu.make_async_copy(ys_hbm.at[pl.ds(d0, rows)], ybuf.at[slot, pl.ds(s0, rows)], sems.at[slot]).start()
        return piece

    @pl.when(i == 0)
    def _():
        ybuf[...] = jnp.zeros(ybuf.shape, F32)
        _for_each_run_piece(n8_ref, src_ref, dst_ref, 0, RUN_BITS, run_copy(0))

    @pl.when(i < last)
    def _():
        _for_each_run_piece(n8_ref, src_ref, dst_ref, i + 1, RUN_BITS, run_copy(1 - par))

    x = _bf(u2_ref[...])
    shared = _dot(_bf(jax.nn.silu(_dot(x, wsg_ref[...])) * _dot(x, wsu_ref[...])), wsd_ref[...])
    _wait_rows(tot_ref[i], lambda rows: pltpu.make_async_copy(
        ys_hbm.at[pl.ds(0, rows)], ybuf.at[par, pl.ds(0, rows)], sems.at[par]), nslot)

    loc = loc_ref[...]
    routed = jnp.zeros((tt, d), F32)
    for c in range(nslot // SLOT_CHUNK):
        cols = c * SLOT_CHUNK + lax.broadcasted_iota(I32, (tt, SLOT_CHUNK), 1)
        pick = jnp.zeros((tt, SLOT_CHUNK), F32)
        for k in range(TOP_K):
            pick = jnp.where(cols == loc[:, k:k + 1], 1.0, pick)
        pick = _bf(pick)
        yh, yl = _split(ybuf[par, c * SLOT_CHUNK:(c + 1) * SLOT_CHUNK, :])
        routed = routed + (_dot(pick, yh) + _dot(pick, yl))
    gate2 = ada_ref[0, :, 5 * d:6 * d]
    h = h_ref[...] + gate2 * (routed + shared)
    o_ref[...] = _rms(h, gf_ref[...])


def _combine(tabs, ys, loc, h1, u2, ada3, wsg, wsu, wsd, gf, seq):
    n, d = h1.shape
    tt = min(MOE_TILE, n)
    nslot = TOP_K * tt + N_EXPERTS * SUBLANES
    per_b = seq // tt
    row = pl.BlockSpec((tt, d), lambda i, *_: (i, 0))
    full = lambda a: pl.BlockSpec(a.shape, lambda i, *_: (0,) * a.ndim)
    wsg, wsu, wsd = wsg.astype(BF16), wsu.astype(BF16), wsd.astype(BF16)
    gf = gf.reshape(1, d)
    return pl.pallas_call(
        functools.partial(_combine_kernel, tt=tt, nslot=nslot, d=d),
        out_shape=jax.ShapeDtypeStruct((n, d), F32),
        grid_spec=pltpu.PrefetchScalarGridSpec(
            num_scalar_prefetch=4, grid=(n // tt,),
            in_specs=[pl.BlockSpec(memory_space=pl.ANY),
                      pl.BlockSpec((tt, TOP_K), lambda i, *_: (i, 0)),
                      row, row,
                      pl.BlockSpec((1, 1, ada3.shape[2]), lambda i, *_: (i // per_b, 0, 0)),
                      full(wsg), full(wsu), full(wsd), full(gf)],
            out_specs=row,
            scratch_shapes=[pltpu.VMEM((2, nslot, d), F32), pltpu.SemaphoreType.DMA((2,))]),
        compiler_params=_cparams(("arbitrary",)),
        name="combine",
    )(*tabs, ys, loc, h1, u2, ada3, wsg, wsu, wsd, gf)


def _moe(h1, u2, ada3, w_router, router_bias, wg, wu, wd, wsg, wsu, wsd, gf):
    bsz, seq, d = h1.shape
    n = bsz * seq
    assert seq % min(MOE_TILE, n) == 0
    h1f = h1.reshape(n, d)
    u2f = u2.reshape(n, d)
    gate_t, loc_t, c8, loff, run0, tot = _route(u2f, w_router, router_bias)
    nt = c8.shape[0]
    as_tab = lambda a: a[:, :, 0].astype(I32)
    tot8 = tot[:, 0].astype(I32)
    padded = (tot8 + EXPERT_ROWS - 1) // EXPERT_ROWS * EXPERT_ROWS
    pend = jnp.cumsum(padded)
    pstart = (pend - padded).astype(I32)
    nblk = (n * TOP_K + nt * N_EXPERTS * (SUBLANES - 1) + N_EXPERTS * (EXPERT_ROWS - 1) + EXPERT_ROWS - 1) // EXPERT_ROWS
    blk_row0 = jnp.arange(nblk, dtype=I32) * EXPERT_ROWS
    blk_expert = jnp.minimum(jnp.sum(pend[None, :] <= blk_row0[:, None], axis=1), N_EXPERTS - 1).astype(I32)
    n_used = (pend[-1:] // EXPERT_ROWS).astype(I32)
    n8 = as_tab(c8) // SUBLANES
    tabs = (n8.reshape(-1), as_tab(loff).reshape(-1), (pstart[None, :] + as_tab(run0)).reshape(-1),
            jnp.sum(n8, axis=1))
    ztabs = ((padded - tot8) // SUBLANES, pstart + tot8)
    xs = _dispatch(tabs, ztabs, u2f, loc_t, gate_t, nblk * EXPERT_ROWS)
    ys = _experts(blk_expert, n_used, xs, wg, wu, wd)
    out = _combine(tabs, ys, loc_t.T, h1f, u2f, ada3, wsg, wsu, wsd, gf, seq)
    return out.reshape(bsz, seq, d)


def kernel(x, c, w_ada, b_ada, norm1_g, w_in, ssm_lambda_re, ssm_lambda_im, ssm_log_dt, ssm_b_re, ssm_b_im,
           ssm_c_re, ssm_c_im, ssm_d, ssm_w_glu, ssm_b_glu, w_proj_ssm, w_proj_attn, w_out, norm2_g, w_router,
           router_bias, w_exp_gate, w_exp_up, w_exp_down, w_sh_gate, w_sh_up, w_sh_down, norm_f_g):
    depth = w_ada.shape[0]
    assert depth == 1, "the final norm is fused into the last (only) layer's combine kernel"
    bsz, seq, d = x.shape
    layer = 0
    ada3 = _ada(c, w_ada[layer], b_ada[layer]).reshape(bsz, 1, 6 * d)
    us, q, qi, k, ki, vt, wit, g = _inproj(x, ada3, norm1_g[layer], w_in[layer])
    a_re, a_im, bb_re, bb_im = _s5disc(ssm_lambda_re[layer], ssm_lambda_im[layer], ssm_log_dt[layer],
                                       ssm_b_re[layer], ssm_b_im[layer])
    ys_t = _s5(us.transpose(1, 0, 2), a_re, a_im, bb_re, bb_im, ssm_c_re[layer], ssm_c_im[layer],
               ssm_d[layer], ssm_w_glu[layer], ssm_b_glu[layer])
    ya = _dsa(q, qi, wit, k, ki, vt).transpose(0, 2, 1)
    h1, u2 = _mix(x, ys_t.transpose(1, 0, 2), ya, g, ada3, w_proj_ssm[layer], w_proj_attn[layer],
                  w_out[layer], norm2_g[layer])
    return _moe(h1, u2, ada3, w_router[layer], router_bias[layer], w_exp_gate[layer], w_exp_up[layer],
                w_exp_down[layer], w_sh_gate[layer], w_sh_up[layer], w_sh_down[layer], norm_f_g)
```

```python
import functools
import math

import jax
import jax.numpy as jnp
import numpy as np
from jax import lax
from jax.experimental import pallas as pl
from jax.experimental.pallas import tpu as pltpu

F32 = jnp.float32
BF16 = jnp.bfloat16
I32 = jnp.int32

SSM_GROUP = 16
SSM_STATE = 64
N_HEADS = 8
HEAD_DIM = 64
IDX_HEADS = 8
IDX_DIM = 64
TOPK_MAX = 256
N_EXPERTS = 64
TOP_K = 8
N_GROUPS = 8
TOPK_GROUPS = 4
ROUTED_SCALE = 2.5
EPS = 1e-6

V7X_VMEM_LIMIT_BYTES = 56 * 1024 * 1024
LANES = 128
SUBLANES = 8

INPROJ_ROWS = 256
S5_STEPS = 64
S5_LANE_CHUNK = 128
DSA_Q_COLS = 256
DSA_K_ROWS = 512
DSA_COUNT_ROWS = 64
BITSLICE_ROWS = 256
POS_SPLIT = 64
MIX_ROWS = 256
MOE_TILE = 256
SLOT_CHUNK = 512
EXPERT_ROWS = 512

NEG_BIG = -1e30
INT_MIN = -(2 ** 31)


def _cparams(sem):
    return pltpu.CompilerParams(dimension_semantics=sem, vmem_limit_bytes=V7X_VMEM_LIMIT_BYTES)


def _bf(x):
    return x.astype(BF16)


def _dot(a, b):
    return jnp.dot(a, b, preferred_element_type=F32)


def _dot_nt(a, b):
    return lax.dot_general(a, b, (((1,), (1,)), ((), ())), preferred_element_type=F32)


def _split(x):
    hi = _bf(x)
    lo = _bf(x - hi.astype(F32))
    return hi, lo


def _dot3(a, b):
    ah, al = _split(a)
    bh, bl = _split(b)
    return _dot(ah, bh) + (_dot(ah, bl) + _dot(al, bh))


def _rms(x, g):
    return x * lax.rsqrt(jnp.mean(x * x, axis=-1, keepdims=True) + EPS) * g


def _ada_kernel(c_ref, w_ref, b_ref, o_ref):
    c = c_ref[...]
    o_ref[...] = _dot3(c * jax.nn.sigmoid(c), w_ref[...]) + b_ref[...]


def _ada(c, w, b):
    bsz, d = c.shape
    n = w.shape[1]
    tn = 1024
    return pl.pallas_call(
        _ada_kernel,
        out_shape=jax.ShapeDtypeStruct((bsz, n), F32),
        grid=(n // tn,),
        in_specs=[pl.BlockSpec((bsz, d), lambda j: (0, 0)),
                  pl.BlockSpec((d, tn), lambda j: (0, j)),
                  pl.BlockSpec((1, tn), lambda j: (0, j))],
        out_specs=pl.BlockSpec((bsz, tn), lambda j: (0, j)),
        compiler_params=_cparams(("arbitrary",)),
        name="ada",
    )(c, w, b.reshape(1, n))


def _inproj_kernel(x_ref, ada_ref, g1_ref, w_ref, wt_ref,
                   us_ref, q_ref, qi_ref, k_ref, ki_ref, vt_ref, wit_ref, g_ref, *, d, ssm_w, attn_w, idx_w):
    x = x_ref[0]
    shift = ada_ref[0, :, 0:d]
    scale = ada_ref[0, :, d:2 * d]
    u = _bf(_rms(x, g1_ref[...]) * (1.0 + scale) + shift)
    r = _dot(u, w_ref[...])
    o = 0
    us_ref[0] = r[:, o:o + ssm_w]
    o += ssm_w
    q_ref[0] = _bf(r[:, o:o + attn_w])
    o += attn_w
    qi_ref[0] = _bf(r[:, o:o + idx_w])
    o += idx_w
    k_ref[0] = _bf(r[:, o:o + HEAD_DIM])
    o += LANES
    ki_ref[0] = _bf(r[:, o:o + IDX_DIM])
    o += LANES
    g_ref[0] = r[:, o:o + 2 * d]
    rt = _dot_nt(wt_ref[...], u)
    vt_ref[0] = _bf(rt[0:HEAD_DIM])
    wit_ref[0] = rt[HEAD_DIM:HEAD_DIM + IDX_HEADS]


def _inproj(x, ada3, g1, w_in):
    bsz, seq, d = x.shape
    ssm_w = 512
    attn_w = N_HEADS * HEAD_DIM
    idx_w = IDX_HEADS * IDX_DIM
    sizes = (ssm_w, attn_w, HEAD_DIM, HEAD_DIM, idx_w, IDX_DIM, IDX_HEADS, d, d)
    offs = [0]
    for s in sizes:
        offs.append(offs[-1] + s)
    w_ssm, w_q, w_k, w_v, w_qi, w_ki, w_wi, w_gs, w_ga = [w_in[:, offs[i]:offs[i + 1]] for i in range(9)]
    zpad = lambda n: jnp.zeros((d, n), F32)
    wbig = jnp.concatenate([
        w_ssm, w_q * (HEAD_DIM ** -0.5), w_qi * (IDX_DIM ** -0.5),
        w_k, zpad(LANES - HEAD_DIM), w_ki, zpad(LANES - IDX_DIM), w_gs, w_ga], axis=1).astype(BF16)
    wt = jnp.concatenate([w_v, w_wi, zpad(LANES - HEAD_DIM - IDX_HEADS)], axis=1).T.astype(BF16)
    nw = wbig.shape[1]
    tl = INPROJ_ROWS
    kern = functools.partial(_inproj_kernel, d=d, ssm_w=ssm_w, attn_w=attn_w, idx_w=idx_w)
    row = lambda w: pl.BlockSpec((1, tl, w), lambda b, l: (b, l, 0))
    colt = lambda h: pl.BlockSpec((1, h, tl), lambda b, l: (b, 0, l))
    return pl.pallas_call(
        kern,
        out_shape=(jax.ShapeDtypeStruct((bsz, seq, ssm_w), F32),
                   jax.ShapeDtypeStruct((bsz, seq, attn_w), BF16),
                   jax.ShapeDtypeStruct((bsz, seq, idx_w), BF16),
                   jax.ShapeDtypeStruct((bsz, seq, HEAD_DIM), BF16),
                   jax.ShapeDtypeStruct((bsz, seq, IDX_DIM), BF16),
                   jax.ShapeDtypeStruct((bsz, HEAD_DIM, seq), BF16),
                   jax.ShapeDtypeStruct((bsz, IDX_HEADS, seq), F32),
                   jax.ShapeDtypeStruct((bsz, seq, 2 * d), F32)),
        grid=(bsz, seq // tl),
        in_specs=[row(d),
                  pl.BlockSpec((1, 1, ada3.shape[2]), lambda b, l: (b, 0, 0)),
                  pl.BlockSpec((1, d), lambda b, l: (0, 0)),
                  pl.BlockSpec((d, nw), lambda b, l: (0, 0)),
                  pl.BlockSpec((LANES, d), lambda b, l: (0, 0))],
        out_specs=(row(ssm_w), row(attn_w), row(idx_w), row(HEAD_DIM), row(IDX_DIM),
                   colt(HEAD_DIM), colt(IDX_HEADS), row(2 * d)),
        compiler_params=_cparams(("arbitrary", "arbitrary")),
        name="inproj",
    )(x, ada3, g1.reshape(1, d), wbig, wt)


def _s5disc_kernel(lr_ref, li_ref, ldt_ref, br_ref, bi_ref, are_ref, aim_ref, bbr_ref, bbi_ref):
    lr = lr_ref[...]
    li = li_ref[...]
    dt = jnp.exp(ldt_ref[...])
    mag = jnp.exp(lr * dt)
    a_re = mag * jnp.cos(li * dt)
    a_im = mag * jnp.sin(li * dt)
    den = lr * lr + li * li
    n_re = a_re - 1.0
    f_re = (n_re * lr + a_im * li) / den
    f_im = (a_im * lr - n_re * li) / den
    br = br_ref[...]
    bi = bi_ref[...]
    are_ref[...] = a_re
    aim_ref[...] = a_im
    bbr_ref[...] = f_re * br - f_im * bi
    bbi_ref[...] = f_re * bi + f_im * br


def _s5disc(lam_re, lam_im, log_dt, b_re, b_im):
    g, p = lam_re.shape
    h = b_re.shape[2]
    rep = lambda a: jnp.repeat(a, h, axis=1)
    ldt = jnp.broadcast_to(log_dt[:, None], (g, p * h))
    sds = jax.ShapeDtypeStruct((g, p * h), F32)
    a_re, a_im, bb_re, bb_im = pl.pallas_call(
        _s5disc_kernel, out_shape=(sds, sds, sds, sds), name="s5disc",
    )(rep(lam_re), rep(lam_im), ldt, b_re.reshape(g, p * h), b_im.reshape(g, p * h))
    return a_re[:, ::h], a_im[:, ::h], bb_re.reshape(g, p, h), bb_im.reshape(g, p, h)


def _s5_kernel(u_ref, wb_ref, ar_ref, ai_ref, cc_ref, dsk_ref, wg_ref, bg_ref, o_ref, buf, hst, *, tl, width):
    nch = width // S5_LANE_CHUNK
    sw = S5_LANE_CHUNK // SSM_GROUP * SSM_STATE
    rows = tl * SUBLANES

    @pl.when(pl.program_id(0) == 0)
    def _():
        hst[...] = jnp.zeros_like(hst)

    u = u_ref[...].reshape(rows, width)
    ub = _bf(u)
    for j in range(nch):
        buf[:, j * 2 * sw:(j + 1) * 2 * sw] = _dot(ub[:, j * S5_LANE_CHUNK:(j + 1) * S5_LANE_CHUNK], wb_ref[j])

    for j in range(nch):
        re_cols = slice(j * 2 * sw, j * 2 * sw + sw)
        im_cols = slice(j * 2 * sw + sw, (j + 1) * 2 * sw)
        a_re = jnp.broadcast_to(ar_ref[:, j * sw:(j + 1) * sw], (SUBLANES, sw))
        a_im = jnp.broadcast_to(ai_ref[:, j * sw:(j + 1) * sw], (SUBLANES, sw))

        def step(t, carry, re_cols=re_cols, im_cols=im_cols, a_re=a_re, a_im=a_im):
            h_re, h_im = carry
            r0 = pl.multiple_of(t * SUBLANES, SUBLANES)
            n_re = (a_re * h_re - a_im * h_im) + buf[pl.ds(r0, SUBLANES), re_cols]
            n_im = (a_re * h_im + a_im * h_re) + buf[pl.ds(r0, SUBLANES), im_cols]
            buf[pl.ds(r0, SUBLANES), re_cols] = n_re
            buf[pl.ds(r0, SUBLANES), im_cols] = n_im
            return n_re, n_im

        h_re, h_im = lax.fori_loop(0, tl, step, (hst[:, re_cols], hst[:, im_cols]), unroll=8)
        hst[:, re_cols] = h_re
        hst[:, im_cols] = h_im

    ys = [_dot(_bf(buf[:, j * 2 * sw:(j + 1) * 2 * sw]), cc_ref[j]) for j in range(nch)]
    y = jnp.concatenate(ys, axis=1) + dsk_ref[...] * u
    y = jax.nn.gelu(y)
    y = y * jax.nn.sigmoid(_dot(_bf(y), wg_ref[...]) + bg_ref[...])
    o_ref[...] = _bf(y).reshape(tl, SUBLANES, width)


def _s5(u_t, a_re, a_im, bb_re, bb_im, c_re, c_im, d_skip, w_glu, b_glu):
    seq, bsz, width = u_t.shape
    assert bsz == SUBLANES
    nch = width // S5_LANE_CHUNK
    gpc = S5_LANE_CHUNK // SSM_GROUP
    sw = gpc * SSM_STATE
    eye = jnp.eye(gpc, dtype=F32)

    def bmat(bb):
        t = bb.reshape(nch, gpc, SSM_STATE, SSM_GROUP).transpose(0, 1, 3, 2)
        return jnp.einsum('jghp,gk->jghkp', t, eye).reshape(nch, S5_LANE_CHUNK, sw)

    def cmat(cc):
        t = cc.reshape(nch, gpc, SSM_GROUP, SSM_STATE).transpose(0, 1, 3, 2)
        return jnp.einsum('jgph,gk->jgpkh', t, eye).reshape(nch, sw, S5_LANE_CHUNK)

    wb = jnp.concatenate([bmat(bb_re), bmat(bb_im)], axis=2)
    cc = jnp.concatenate([cmat(c_re), -cmat(c_im)], axis=1)
    tl = S5_STEPS
    full = lambda a: pl.BlockSpec(a.shape, lambda i: (0,) * a.ndim)
    args = (u_t, wb.astype(BF16), a_re.reshape(1, -1), a_im.reshape(1, -1), cc.astype(BF16),
            d_skip.reshape(1, width), w_glu.astype(BF16), b_glu.reshape(1, width))
    return pl.pallas_call(
        functools.partial(_s5_kernel, tl=tl, width=width),
        out_shape=jax.ShapeDtypeStruct((seq, bsz, width), BF16),
        grid=(seq // tl,),
        in_specs=[pl.BlockSpec((tl, bsz, width), lambda i: (i, 0, 0))] + [full(a) for a in args[1:]],
        out_specs=pl.BlockSpec((tl, bsz, width), lambda i: (i, 0, 0)),
        scratch_shapes=[pltpu.VMEM((tl * SUBLANES, nch * 2 * sw), F32),
                        pltpu.VMEM((SUBLANES, nch * 2 * sw), F32)],
        compiler_params=_cparams(("arbitrary",)),
        name="s5",
    )(*args)


def _bit_transpose32(words):
    x = list(words)
    j, m = 16, 0x0000FFFF
    while j:
        k = 0
        while k < 32:
            t = (x[k] ^ lax.shift_right_logical(x[k + j], jnp.int32(j))) & jnp.int32(m - (1 << 32) if m >= 1 << 31 else m)
            x[k] = x[k] ^ t
            x[k + j] = x[k + j] ^ lax.shift_left(t, jnp.int32(j))
            k = (k + j + 1) & ~j
        j >>= 1
        m = (m ^ (m << j)) & 0xFFFFFFFF
    return x


def _dsa_kernel(qt_ref, qit_ref, wit_ref, ka_ref, ki_ref, vt_ref, o_ref, key_s, mb_s, acc_s, pl_s, p_s, *, tq, tk, topk, seq):
    i = pl.program_id(1)
    q0 = i * tq
    nkt = (q0 + tq + tk - 1) // tk
    ch = DSA_COUNT_ROWS
    krow = lax.broadcasted_iota(I32, (tk, tq), 0)
    qcol = q0 + lax.broadcasted_iota(I32, (tk, tq), 1)
    crow = lax.broadcasted_iota(I32, (ch, tq), 0)

    wb = wit_ref[0] * (IDX_HEADS ** -0.5)

    def score_tile(j, _):
        r0 = pl.multiple_of(j * tk, tk)
        kit = ki_ref[0, pl.ds(r0, tk), :]
        acc = jnp.zeros((tk, tq), F32)
        for h in range(IDX_HEADS):
            s = _dot(kit, qit_ref[0, h * IDX_DIM:(h + 1) * IDX_DIM, :])
            acc = acc + wb[h:h + 1, :] * jnp.maximum(s, 0.0)
        bits = lax.bitcast_convert_type(acc, I32)
        key = jnp.where(bits < 0, bits ^ jnp.int32(0x7FFFFFFF), bits)
        key = jnp.where(acc == 0.0, 0, key)
        key = jnp.where(krow + r0 <= qcol, key, INT_MIN)
        key_s[pl.ds(r0, tk), :] = key
        ukey = key ^ INT_MIN
        for c in range(tk // BITSLICE_ROWS):
            words = [ukey[c * BITSLICE_ROWS + v * SUBLANES:c * BITSLICE_ROWS + (v + 1) * SUBLANES, :]
                     for v in range(32)]
            planes = _bit_transpose32(words)
            g0 = pl.multiple_of((j * (tk // BITSLICE_ROWS) + c) * SUBLANES, SUBLANES)
            for it in range(32):
                pl_s[it, pl.ds(g0, SUBLANES), :] = planes[it]
        return 0

    @pl.when((pl.program_id(0) == 0) & (i == 0))
    def _():
        pl_s[...] = jnp.zeros(pl_s.shape, I32)

    lax.fori_loop(0, nkt, score_tile, 0)

    def count(pred):
        def tile(j, cnt):
            for c in range(tk // ch):
                rr = pl.multiple_of(j * tk + c * ch, ch)
                cnt = cnt + jnp.where(pred(key_s[pl.ds(rr, ch), :], rr), 1, 0)
            return cnt
        cnt = lax.fori_loop(0, nkt, tile, jnp.zeros((ch, tq), I32))
        return jnp.sum(cnt.astype(F32), axis=0, keepdims=True)

    ngrp = seq // 32

    def lane_count(words):
        pc = lax.population_count(words).reshape(ngrp // SUBLANES, SUBLANES, tq)
        return jnp.sum(jnp.sum(pc, axis=0).astype(F32), axis=0, keepdims=True)

    def bit_step(it, carry):
        alive, above, ans_u = carry
        ones = alive & pl_s[it]
        cnt1 = lane_count(ones)
        take = above + cnt1 >= float(topk)
        alive = jnp.where(take, ones, alive ^ ones)
        above = jnp.where(take, above, above + cnt1)
        ans_u = jnp.where(take, ans_u | lax.shift_left(jnp.int32(1), 31 - it), ans_u)
        return alive, above, ans_u

    grow = lax.broadcasted_iota(I32, (ngrp, tq), 0)
    alive0 = jnp.where(grow < nkt * (tk // 32), -1, 0)
    alive, above, ans_u = lax.fori_loop(
        0, 32, bit_step, (alive0, jnp.zeros((1, tq), F32), jnp.zeros((1, tq), I32)))
    thr = jnp.maximum(ans_u ^ INT_MIN, INT_MIN + 1)
    cnt_ge = above + lane_count(alive)
    tied = jnp.where(ans_u != 0, cnt_ge, 0.0) > float(topk)
    has_ties = jnp.max(jnp.where(tied, 1.0, 0.0)) > 0.0

    def tie_cut():
        need = float(topk) - count(lambda kb, rr: kb > thr)
        nbits = max(1, (seq - 1).bit_length())

        def idx_step(b, x):
            cand = x | lax.shift_left(jnp.int32(1), nbits - 1 - b)
            below = count(lambda kb, rr: jnp.where(kb == thr, crow + rr, seq) < cand)
            return jnp.where(below < need, cand, x)

        x = lax.fori_loop(0, nbits, idx_step, jnp.zeros((1, tq), I32))
        return jnp.where(tied, x, seq)

    cut = lax.cond(has_ties, tie_cut, lambda: jnp.full((1, tq), seq, I32))

    def bias_tile(j, _):
        for c in range(tk // ch):
            rr = pl.multiple_of(j * tk + c * ch, ch)
            kb = key_s[pl.ds(rr, ch), :]
            tie_bias = jnp.where(crow + rr <= cut, 0.0, NEG_BIG)
            mb_s[pl.ds(rr, ch), :] = jnp.where(kb > thr, 0.0, jnp.where(kb == thr, tie_bias, NEG_BIG))
        return 0

    lax.fori_loop(0, nkt, bias_tile, 0)

    def logits(j, h):
        r0 = pl.multiple_of(j * tk, tk)
        s = _dot(ka_ref[0, pl.ds(r0, tk), :], qt_ref[0, h * LANES:(h + 1) * LANES, :]) + mb_s[pl.ds(r0, tk), :]
        return s.reshape(tk // SUBLANES, SUBLANES, tq)

    def max_tile(j, ms):
        return tuple(jnp.maximum(ms[h], jnp.max(logits(j, h), axis=0)) for h in range(N_HEADS))

    ms = lax.fori_loop(0, nkt, max_tile, (jnp.full((SUBLANES, tq), NEG_BIG, F32),) * N_HEADS)
    m = [jnp.max(mh, axis=0, keepdims=True) for mh in ms]
    acc_s[...] = jnp.zeros(acc_s.shape, F32)

    def sum_tile(j, ls):
        r0 = pl.multiple_of(j * tk, tk)
        out = []
        for h in range(N_HEADS):
            p = jnp.exp(logits(j, h) - m[h])
            out.append(ls[h] + jnp.sum(p, axis=0))
            p_s[h] = _bf(p.reshape(tk, tq))
        for h in range(N_HEADS):
            rows = slice(h * HEAD_DIM, (h + 1) * HEAD_DIM)
            acc_s[rows, :] = acc_s[rows, :] + _dot(vt_ref[0, :, pl.ds(r0, tk)], p_s[h])
        return tuple(out)

    ls = lax.fori_loop(0, nkt, sum_tile, (jnp.zeros((SUBLANES, tq), F32),) * N_HEADS)
    for h in range(N_HEADS):
        rows = slice(h * HEAD_DIM, (h + 1) * HEAD_DIM)
        o_ref[0, rows, :] = _bf(acc_s[rows, :] / jnp.sum(ls[h], axis=0, keepdims=True))


def _dsa(q, qi, wit, k, ki, vt):
    bsz, seq, aw = q.shape
    tq = min(DSA_Q_COLS, seq)
    tk = min(DSA_K_ROWS, seq)
    topk = min(TOPK_MAX, seq // 4)
    assert (seq - 1) // POS_SPLIT < 256 and POS_SPLIT <= 256, "key positions must split into two bf16-exact parts"
    slopes = [2.0 ** (-8.0 * (h + 1) / N_HEADS) for h in range(N_HEADS)]
    assert all(float(np.float32(sl).astype(BF16)) == sl for sl in slopes), "ALiBi slopes must be bf16-exact"
    pos = jnp.arange(seq, dtype=I32)
    posc = jnp.stack([(pos // POS_SPLIT) * POS_SPLIT, pos % POS_SPLIT], axis=1).astype(BF16)
    ka = jnp.concatenate([k, jnp.broadcast_to(posc[None], (bsz, seq, 2)),
                          jnp.zeros((bsz, seq, LANES - HEAD_DIM - 2), BF16)], axis=2)
    qh = q.reshape(bsz, seq, N_HEADS, HEAD_DIM).transpose(0, 2, 3, 1)
    srow = jnp.asarray(slopes, BF16)[None, :, None, None]
    extra = jnp.concatenate([jnp.broadcast_to(srow, (bsz, N_HEADS, 2, seq)),
                             jnp.zeros((bsz, N_HEADS, LANES - HEAD_DIM - 2, seq), BF16)], axis=2)
    qt = jnp.concatenate([qh, extra], axis=2).reshape(bsz, N_HEADS * LANES, seq)
    qit = qi.transpose(0, 2, 1)
    kern = functools.partial(_dsa_kernel, tq=tq, tk=tk, topk=topk, seq=seq)
    cols = lambda r: pl.BlockSpec((1, r, tq), lambda b, i: (b, 0, i))
    return pl.pallas_call(
        kern,
        out_shape=jax.ShapeDtypeStruct((bsz, aw, seq), BF16),
        grid=(bsz, seq // tq),
        in_specs=[cols(N_HEADS * LANES), cols(qit.shape[1]), cols(IDX_HEADS),
                  pl.BlockSpec((1, seq, LANES), lambda b, i: (b, 0, 0)),
                  pl.BlockSpec((1, seq, IDX_DIM), lambda b, i: (b, 0, 0)),
                  pl.BlockSpec((1, HEAD_DIM, seq), lambda b, i: (b, 0, 0))],
        out_specs=cols(aw),
        scratch_shapes=[pltpu.VMEM((seq, tq), I32), pltpu.VMEM((seq, tq), F32), pltpu.VMEM((aw, tq), F32),
                        pltpu.VMEM((32, seq // 32, tq), I32), pltpu.VMEM((N_HEADS, tk, tq), BF16)],
        compiler_params=_cparams(("arbitrary", "arbitrary")),
        name="dsa",
    )(qt, qit, wit, ka, ki, vt)


def _mix_kernel(x_ref, ys_ref, ya_ref, g_ref, ada_ref, wps_ref, wpa_ref, wo_ref, g2_ref, h_ref, u2_ref, *, d):
    gate1 = ada_ref[0, :, 2 * d:3 * d]
    shift2 = ada_ref[0, :, 3 * d:4 * d]
    scale2 = ada_ref[0, :, 4 * d:5 * d]
    g = g_ref[0]
    mixed = (jax.nn.sigmoid(g[:, 0:d]) * _dot(ys_ref[0], wps_ref[...])
             + jax.nn.sigmoid(g[:, d:2 * d]) * _dot(ya_ref[0], wpa_ref[...]))
    h = x_ref[0] + gate1 * _dot(_bf(mixed), wo_ref[...])
    h_ref[0] = h
    u2_ref[0] = _rms(h, g2_ref[...]) * (1.0 + scale2) + shift2


def _mix(x, ys, ya, g, ada3, wps, wpa, wo, g2):
    bsz, seq, d = x.shape
    tm = MIX_ROWS
    row = lambda w: pl.BlockSpec((1, tm, w), lambda b, l: (b, l, 0))
    full = lambda a: pl.BlockSpec(a.shape, lambda b, l: (0,) * a.ndim)
    wps, wpa, wo = wps.astype(BF16), wpa.astype(BF16), wo.astype(BF16)
    g2 = g2.reshape(1, d)
    return pl.pallas_call(
        functools.partial(_mix_kernel, d=d),
        out_shape=(jax.ShapeDtypeStruct((bsz, seq, d), F32), jax.ShapeDtypeStruct((bsz, seq, d), F32)),
        grid=(bsz, seq // tm),
        in_specs=[row(d), row(ys.shape[2]), row(ya.shape[2]), row(2 * d),
                  pl.BlockSpec((1, 1, ada3.shape[2]), lambda b, l: (b, 0, 0)),
                  full(wps), full(wpa), full(wo), full(g2)],
        out_specs=(row(d), row(d)),
        compiler_params=_cparams(("arbitrary", "arbitrary")),
        name="mix",
    )(x, ys, ya, g, ada3, wps, wpa, wo, g2)


def _first_max(cur, idx, axis, big):
    m = jnp.max(cur, axis=axis, keepdims=True)
    first = jnp.min(jnp.where(cur == m, idx, big), axis=axis, keepdims=True)
    return m, idx == first


def _route_kernel(u_ref, wrh_ref, wrl_ref, rb_ref, tri_ref, ltri_ref,
                  gt_ref, loc_ref, c8_ref, loff_ref, run0_ref, tot_ref, run_s, *, t):
    @pl.when(pl.program_id(0) == 0)
    def _():
        run_s[...] = jnp.zeros_like(run_s)

    uh, ul = _split(u_ref[...])
    logits = _dot_nt(wrh_ref[...], uh) + (_dot_nt(wrl_ref[...], uh) + _dot_nt(wrh_ref[...], ul))
    scores = jax.nn.sigmoid(logits)
    biased = scores + rb_ref[...]
    per_group = N_EXPERTS // N_GROUPS
    b3 = biased.reshape(N_GROUPS, per_group, t)
    i3 = lax.broadcasted_iota(I32, b3.shape, 1)
    m1, hit1 = _first_max(b3, i3, 1, per_group)
    m2 = jnp.max(jnp.where(hit1, -jnp.inf, b3), axis=1, keepdims=True)
    gs = (m1 + m2).reshape(N_GROUPS, t)
    gi = lax.broadcasted_iota(I32, gs.shape, 0)
    gsel = jnp.zeros(gs.shape, F32)
    for _ in range(TOPK_GROUPS):
        _, hit = _first_max(gs, gi, 0, N_GROUPS)
        gsel = jnp.where(hit, 1.0, gsel)
        gs = jnp.where(hit, -jnp.inf, gs)
    cur = jnp.where(gsel.reshape(N_GROUPS, 1, t) > 0.0, b3, -jnp.inf).reshape(N_EXPERTS, t)
    ei = lax.broadcasted_iota(I32, cur.shape, 0)
    hits = []
    gates = []
    for _ in range(TOP_K):
        _, hit = _first_max(cur, ei, 0, N_EXPERTS)
        hits.append(hit)
        gates.append(jnp.sum(jnp.where(hit, scores, 0.0), axis=0, keepdims=True))
        cur = jnp.where(hit, -jnp.inf, cur)
    gate = jnp.concatenate(gates, axis=0)
    gt_ref[...] = gate / jnp.sum(gate, axis=0, keepdims=True) * ROUTED_SCALE
    onehot = jnp.zeros(cur.shape, F32)
    for hit in hits:
        onehot = jnp.where(hit, 1.0, onehot)
    cnt = jnp.sum(onehot, axis=1, keepdims=True)
    c8 = jnp.floor((cnt + (SUBLANES - 1)) * (1.0 / SUBLANES)) * SUBLANES
    c8l = jnp.broadcast_to(c8, (N_EXPERTS, LANES))
    loff = _dot(ltri_ref[...], _bf(c8l))
    slot = _dot(_bf(onehot), tri_ref[...]) + loff[:, 0:1]
    loc_ref[...] = jnp.concatenate(
        [jnp.sum(jnp.where(hit, slot, 0.0), axis=0, keepdims=True) for hit in hits], axis=0).astype(I32)
    c8_ref[0] = c8l
    loff_ref[0] = loff
    run0_ref[0] = run_s[...]
    run_s[...] = run_s[...] + c8
    tot_ref[...] = run_s[...]


def _route(u2, w_router, router_bias):
    n, d = u2.shape
    t = min(MOE_TILE, n)
    nt = n // t
    wt = w_router.T
    wrh = wt.astype(BF16)
    wrl = (wt - wrh.astype(F32)).astype(BF16)
    tri = (jnp.arange(t)[:, None] < jnp.arange(t)[None, :]).astype(BF16)
    ex = jnp.arange(N_EXPERTS)
    ltri = (ex[None, :] < ex[:, None]).astype(BF16)
    full = lambda a: pl.BlockSpec(a.shape, lambda i: (0,) * a.ndim)
    col = pl.BlockSpec((TOP_K, t), lambda i: (0, i))
    tab = pl.BlockSpec((1, N_EXPERTS, LANES), lambda i: (i, 0, 0))
    tab_sds = jax.ShapeDtypeStruct((nt, N_EXPERTS, LANES), F32)
    rb = router_bias.reshape(N_EXPERTS, 1)
    return pl.pallas_call(
        functools.partial(_route_kernel, t=t),
        out_shape=(jax.ShapeDtypeStruct((TOP_K, n), F32), jax.ShapeDtypeStruct((TOP_K, n), I32),
                   tab_sds, tab_sds, tab_sds, jax.ShapeDtypeStruct((N_EXPERTS, LANES), F32)),
        grid=(nt,),
        in_specs=[pl.BlockSpec((t, d), lambda i: (i, 0)), full(wrh), full(wrl), full(rb), full(tri), full(ltri)],
        out_specs=(col, col, tab, tab, tab, pl.BlockSpec((N_EXPERTS, LANES), lambda i: (0, 0))),
        scratch_shapes=[pltpu.VMEM((N_EXPERTS, LANES), F32)],
        compiler_params=_cparams(("arbitrary",)),
        name="route",
    )(u2, wrh, wrl, rb, tri, ltri)


RUN_BITS = tuple(1 << b for b in reversed(range((MOE_TILE // SUBLANES).bit_length())))


def _for_each_run_piece(n8_ref, src_ref, dst_ref, tile, bits, fn):
    def per_expert(e, _):
        idx = tile * N_EXPERTS + e
        n8 = n8_ref[idx]
        src = src_ref[idx]
        dst = dst_ref[idx]
        for b, p in enumerate(bits):
            off = (n8 & ~(2 * p - 1)) * SUBLANES

            @pl.when((n8 & p) != 0)
            def _(b=b, p=p, off=off):
                fn(pl.multiple_of(src + off, SUBLANES), pl.multiple_of(dst + off, SUBLANES), p * SUBLANES, b % 2)
        return 0

    lax.fori_loop(0, N_EXPERTS, per_expert, 0)


def _wait_rows(n8, make_copy, max_rows):
    for p in tuple(1 << b for b in reversed(range((max_rows // SUBLANES).bit_length()))):
        @pl.when((n8 & p) != 0)
        def _(p=p):
            make_copy(p * SUBLANES).wait()


def _dispatch_kernel(n8_ref, src_ref, dst_ref, tot_ref, zn8_ref, zdst_ref, u_ref, loc_ref, gate_ref, xs_hbm,
                     lbuf, zx, sems, *, tt, nslot, dh):
    i = pl.program_id(0)
    last = pl.num_programs(0) - 1
    par = i % 2

    def run_copy(slot):
        def piece(s0, d0, rows, prio):
            pltpu.make_async_copy(
                lbuf.at[slot, pl.ds(s0, rows)], xs_hbm.at[pl.ds(d0, rows)], sems.at[slot]).start(priority=prio)
        return piece

    def wait_tile(tile, slot):
        _wait_rows(tot_ref[tile], lambda rows: pltpu.make_async_copy(
            lbuf.at[slot, pl.ds(0, rows)], xs_hbm.at[pl.ds(0, rows)], sems.at[slot]), nslot)

    @pl.when(i == 0)
    def _():
        zx[...] = jnp.zeros(zx.shape, I32)

        def zero_piece(s0, d0, rows, prio):
            cx = pltpu.make_async_copy(zx.at[pl.ds(0, rows)], xs_hbm.at[pl.ds(d0, rows)], sems.at[2])
            cx.start()
            cx.wait()

        zbits = tuple(b for b in RUN_BITS if b * SUBLANES < EXPERT_ROWS)
        _for_each_run_piece(zn8_ref, zdst_ref, zdst_ref, 0, zbits, zero_piece)

    ub = _bf(u_ref[...])
    ones = jnp.ones((tt, LANES), BF16)
    loc = loc_ref[...]
    gate = gate_ref[...]
    for c in range(nslot // SLOT_CHUNK):
        rows = c * SLOT_CHUNK + lax.broadcasted_iota(I32, (SLOT_CHUNK, tt), 0)
        perm = jnp.zeros((SLOT_CHUNK, tt), F32)
        pgate = jnp.zeros((SLOT_CHUNK, tt), F32)
        for k in range(TOP_K):
            eq = rows == loc[k:k + 1, :]
            perm = jnp.where(eq, 1.0, perm)
            pgate = jnp.where(eq, gate[k:k + 1, :], pgate)
        cs = slice(c * SLOT_CHUNK, (c + 1) * SLOT_CHUNK)
        xp = lax.bitcast_convert_type(_dot(_bf(perm), ub), I32)
        lbuf[par, cs, 0:dh] = xp[:, 0:dh] | lax.shift_right_logical(xp[:, dh:2 * dh], 16)
        gh, gl = _split(pgate)
        lbuf[par, cs, dh:dh + LANES] = lax.bitcast_convert_type(_dot(gh, ones) + _dot(gl, ones), I32)

    _for_each_run_piece(n8_ref, src_ref, dst_ref, i, RUN_BITS, run_copy(par))

    @pl.when(i > 0)
    def _():
        wait_tile(i - 1, 1 - par)

    @pl.when(i == last)
    def _():
        wait_tile(i, par)


def _dispatch(tabs, ztabs, u2, loc_t, gate_t, n_rows):
    n, d = u2.shape
    tt = min(MOE_TILE, n)
    nslot = TOP_K * tt + N_EXPERTS * SUBLANES
    dh = d // 2
    assert nslot % SLOT_CHUNK == 0 and tt // SUBLANES == RUN_BITS[0]
    col = pl.BlockSpec((TOP_K, tt), lambda i, *_: (0, i))
    return pl.pallas_call(
        functools.partial(_dispatch_kernel, tt=tt, nslot=nslot, dh=dh),
        out_shape=jax.ShapeDtypeStruct((n_rows, dh + LANES), I32),
        grid_spec=pltpu.PrefetchScalarGridSpec(
            num_scalar_prefetch=6, grid=(n // tt,),
            in_specs=[pl.BlockSpec((tt, d), lambda i, *_: (i, 0)), col, col],
            out_specs=pl.BlockSpec(memory_space=pl.ANY),
            scratch_shapes=[pltpu.VMEM((2, nslot, dh + LANES), I32),
                            pltpu.VMEM((EXPERT_ROWS // 2, dh + LANES), I32),
                            pltpu.SemaphoreType.DMA((3,))]),
        compiler_params=_cparams(("arbitrary",)),
        name="dispatch",
    )(*tabs, *ztabs, u2, loc_t, gate_t)


def _experts_kernel(be_ref, nu_ref, xs_ref, wg_ref, wu_ref, wd_ref, ys_ref, *, d):
    del be_ref
    dh = d // 2

    @pl.when(pl.program_id(0) < nu_ref[0])
    def _():
        w = xs_ref[:, 0:dh]
        xa = _bf(lax.bitcast_convert_type(w & jnp.int32(-65536), F32))
        xb = _bf(lax.bitcast_convert_type(lax.shift_left(w, 16), F32))
        gate = lax.bitcast_convert_type(xs_ref[:, dh:dh + LANES], F32)
        hg = _dot(xa, wg_ref[0, 0:dh, :]) + _dot(xb, wg_ref[0, dh:d, :])
        hu = _dot(xa, wu_ref[0, 0:dh, :]) + _dot(xb, wu_ref[0, dh:d, :])
        ys_ref[...] = _dot(_bf(jax.nn.silu(hg) * hu), wd_ref[0]) * jnp.tile(gate, (1, d // LANES))


def _experts(blk_expert, n_used, xs, wg, wu, wd):
    rows, xw = xs.shape
    d = wg.shape[1]
    de = wg.shape[2]
    nblk = rows // EXPERT_ROWS
    blk = lambda i, be, nu: jnp.minimum(i, nu[0] - 1)
    return pl.pallas_call(
        functools.partial(_experts_kernel, d=d),
        out_shape=jax.ShapeDtypeStruct((rows, d), F32),
        grid_spec=pltpu.PrefetchScalarGridSpec(
            num_scalar_prefetch=2, grid=(nblk,),
            in_specs=[pl.BlockSpec((EXPERT_ROWS, xw), lambda i, be, nu: (blk(i, be, nu), 0)),
                      pl.BlockSpec((1, d, de), lambda i, be, nu: (be[blk(i, be, nu)], 0, 0)),
                      pl.BlockSpec((1, d, de), lambda i, be, nu: (be[blk(i, be, nu)], 0, 0)),
                      pl.BlockSpec((1, de, d), lambda i, be, nu: (be[blk(i, be, nu)], 0, 0))],
            out_specs=pl.BlockSpec((EXPERT_ROWS, d), lambda i, be, nu: (blk(i, be, nu), 0))),
        compiler_params=_cparams(("arbitrary",)),
        name="experts",
    )(blk_expert, n_used, xs, wg.astype(BF16), wu.astype(BF16), wd.astype(BF16))


def _combine_kernel(n8_ref, src_ref, dst_ref, tot_ref, ys_hbm, loc_ref, h_ref, u2_ref, ada_ref, wsg_ref, wsu_ref, wsd_ref,
                    gf_ref, o_ref, ybuf, sems, *, tt, nslot, d):
    i = pl.program_id(0)
    last = pl.num_programs(0) - 1
    par = i % 2

    def run_copy(slot):
        def piece(s0, d0, rows, prio):
            pltpu.make_async_copy(
                ys_hbm.at[pl.ds(d0, rows)], ybuf.at[slot, pl.ds(s0, rows)], sems.at[slot]).start(priority=prio)
        return piece

    @pl.when(i == 0)
    def _():
        ybuf[...] = jnp.zeros(ybuf.shape, F32)
        _for_each_run_piece(n8_ref, src_ref, dst_ref, 0, RUN_BITS, run_copy(0))

    @pl.when(i < last)
    def _():
        _for_each_run_piece(n8_ref, src_ref, dst_ref, i + 1, RUN_BITS, run_copy(1 - par))

    x = _bf(u2_ref[...])
    shared = _dot(_bf(jax.nn.silu(_dot(x, wsg_ref[...])) * _dot(x, wsu_ref[...])), wsd_ref[...])
    _wait_rows(tot_ref[i], lambda rows: pltpu.make_async_copy(
        ys_hbm.at[pl.ds(0, rows)], ybuf.at[par, pl.ds(0, rows)], sems.at[par]), nslot)

    loc = loc_ref[...]
    routed = jnp.zeros((tt, d), F32)
    for c in range(nslot // SLOT_CHUNK):
        cols = c * SLOT_CHUNK + lax.broadcasted_iota(I32, (tt, SLOT_CHUNK), 1)
        pick = jnp.zeros((tt, SLOT_CHUNK), F32)
        for k in range(TOP_K):
            pick = jnp.where(cols == loc[:, k:k + 1], 1.0, pick)
        pick = _bf(pick)
        yh, yl = _split(ybuf[par, c * SLOT_CHUNK:(c + 1) * SLOT_CHUNK, :])
        routed = routed + (_dot(pick, yh) + _dot(pick, yl))
    gate2 = ada_ref[0, :, 5 * d:6 * d]
    h = h_ref[...] + gate2 * (routed + shared)
    o_ref[...] = _rms(h, gf_ref[...])


def _combine(tabs, ys, loc, h1, u2, ada3, wsg, wsu, wsd, gf, seq):
    n, d = h1.shape
    tt = min(MOE_TILE, n)
    nslot = TOP_K * tt + N_EXPERTS * SUBLANES
    per_b = seq // tt
    row = pl.BlockSpec((tt, d), lambda i, *_: (i, 0))
    full = lambda a: pl.BlockSpec(a.shape, lambda i, *_: (0,) * a.ndim)
    wsg, wsu, wsd = wsg.astype(BF16), wsu.astype(BF16), wsd.astype(BF16)
    gf = gf.reshape(1, d)
    return pl.pallas_call(
        functools.partial(_combine_kernel, tt=tt, nslot=nslot, d=d),
        out_shape=jax.ShapeDtypeStruct((n, d), F32),
        grid_spec=pltpu.PrefetchScalarGridSpec(
            num_scalar_prefetch=4, grid=(n // tt,),
            in_specs=[pl.BlockSpec(memory_space=pl.ANY),
                      pl.BlockSpec((tt, TOP_K), lambda i, *_: (i, 0)),
                      row, row,
                      pl.BlockSpec((1, 1, ada3.shape[2]), lambda i, *_: (i // per_b, 0, 0)),
                      full(wsg), full(wsu), full(wsd), full(gf)],
            out_specs=row,
            scratch_shapes=[pltpu.VMEM((2, nslot, d), F32), pltpu.SemaphoreType.DMA((2,))]),
        compiler_params=_cparams(("arbitrary",)),
        name="combine",
    )(*tabs, ys, loc, h1, u2, ada3, wsg, wsu, wsd, gf)


def _moe(h1, u2, ada3, w_router, router_bias, wg, wu, wd, wsg, wsu, wsd, gf):
    bsz, seq, d = h1.shape
    n = bsz * seq
    assert seq % min(MOE_TILE, n) == 0
    h1f = h1.reshape(n, d)
    u2f = u2.reshape(n, d)
    gate_t, loc_t, c8, loff, run0, tot = _route(u2f, w_router, router_bias)
    nt = c8.shape[0]
    as_tab = lambda a: a[:, :, 0].astype(I32)
    tot8 = tot[:, 0].astype(I32)
    padded = (tot8 + EXPERT_ROWS - 1) // EXPERT_ROWS * EXPERT_ROWS
    pend = jnp.cumsum(padded)
    pstart = (pend - padded).astype(I32)
    nblk = (n * TOP_K + nt * N_EXPERTS * (SUBLANES - 1) + N_EXPERTS * (EXPERT_ROWS - 1) + EXPERT_ROWS - 1) // EXPERT_ROWS
    blk_row0 = jnp.arange(nblk, dtype=I32) * EXPERT_ROWS
    blk_expert = jnp.minimum(jnp.sum(pend[None, :] <= blk_row0[:, None], axis=1), N_EXPERTS - 1).astype(I32)
    n_used = (pend[-1:] // EXPERT_ROWS).astype(I32)
    n8 = as_tab(c8) // SUBLANES
    tabs = (n8.reshape(-1), as_tab(loff).reshape(-1), (pstart[None, :] + as_tab(run0)).reshape(-1),
            jnp.sum(n8, axis=1))
    ztabs = ((padded - tot8) // SUBLANES, pstart + tot8)
    xs = _dispatch(tabs, ztabs, u2f, loc_t, gate_t, nblk * EXPERT_ROWS)
    ys = _experts(blk_expert, n_used, xs, wg, wu, wd)
    out = _combine(tabs, ys, loc_t.T, h1f, u2f, ada3, wsg, wsu, wsd, gf, seq)
    return out.reshape(bsz, seq, d)


def kernel(x, c, w_ada, b_ada, norm1_g, w_in, ssm_lambda_re, ssm_lambda_im, ssm_log_dt, ssm_b_re, ssm_b_im,
           ssm_c_re, ssm_c_im, ssm_d, ssm_w_glu, ssm_b_glu, w_proj_ssm, w_proj_attn, w_out, norm2_g, w_router,
           router_bias, w_exp_gate, w_exp_up, w_exp_down, w_sh_gate, w_sh_up, w_sh_down, norm_f_g):
    depth = w_ada.shape[0]
    assert depth == 1, "the final norm is fused into the last (only) layer's combine kernel"
    bsz, seq, d = x.shape
    layer = 0
    ada3 = _ada(c, w_ada[layer], b_ada[layer]).reshape(bsz, 1, 6 * d)
    us, q, qi, k, ki, vt, wit, g = _inproj(x, ada3, norm1_g[layer], w_in[layer])
    a_re, a_im, bb_re, bb_im = _s5disc(ssm_lambda_re[layer], ssm_lambda_im[layer], ssm_log_dt[layer],
                                       ssm_b_re[layer], ssm_b_im[layer])
    ys_t = _s5(us.transpose(1, 0, 2), a_re, a_im, bb_re, bb_im, ssm_c_re[layer], ssm_c_im[layer],
               ssm_d[layer], ssm_w_glu[layer], ssm_b_glu[layer])
    ya = _dsa(q, qi, wit, k, ki, vt).transpose(0, 2, 1)
    h1, u2 = _mix(x, ys_t.transpose(1, 0, 2), ya, g, ada3, w_proj_ssm[layer], w_proj_attn[layer],
                  w_out[layer], norm2_g[layer])
    return _moe(h1, u2, ada3, w_router[layer], router_bias[layer], w_exp_gate[layer], w_exp_up[layer],
                w_exp_down[layer], w_sh_gate[layer], w_sh_up[layer], w_sh_down[layer], norm_f_g)
```

```python
import functools
import math

import jax
import jax.numpy as jnp
import numpy as np
from jax import lax
from jax.experimental import pallas as pl
from jax.experimental.pallas import tpu as pltpu

F32 = jnp.float32
BF16 = jnp.bfloat16
I32 = jnp.int32

SSM_GROUP = 16
SSM_STATE = 64
N_HEADS = 8
HEAD_DIM = 64
IDX_HEADS = 8
IDX_DIM = 64
TOPK_MAX = 256
N_EXPERTS = 64
TOP_K = 8
N_GROUPS = 8
TOPK_GROUPS = 4
ROUTED_SCALE = 2.5
EPS = 1e-6

V7X_VMEM_LIMIT_BYTES = 56 * 1024 * 1024
LANES = 128
SUBLANES = 8

INPROJ_ROWS = 256
S5_STEPS = 64
S5_LANE_CHUNK = 128
DSA_Q_COLS = 256
DSA_K_ROWS = 512
DSA_COUNT_ROWS = 64
BITSLICE_ROWS = 256
POS_SPLIT = 64
MIX_ROWS = 256
MOE_TILE = 256
SLOT_CHUNK = 512
EXPERT_ROWS = 512

NEG_BIG = -1e30
INT_MIN = -(2 ** 31)


def _cparams(sem):
    return pltpu.CompilerParams(dimension_semantics=sem, vmem_limit_bytes=V7X_VMEM_LIMIT_BYTES)


def _bf(x):
    return x.astype(BF16)


def _dot(a, b):
    return jnp.dot(a, b, preferred_element_type=F32)


def _dot_nt(a, b):
    return lax.dot_general(a, b, (((1,), (1,)), ((), ())), preferred_element_type=F32)


def _split(x):
    hi = _bf(x)
    lo = _bf(x - hi.astype(F32))
    return hi, lo


def _dot3(a, b):
    ah, al = _split(a)
    bh, bl = _split(b)
    return _dot(ah, bh) + (_dot(ah, bl) + _dot(al, bh))


def _rms(x, g):
    return x * lax.rsqrt(jnp.mean(x * x, axis=-1, keepdims=True) + EPS) * g


def _ada_kernel(c_ref, w_ref, b_ref, o_ref):
    c = c_ref[...]
    o_ref[...] = _dot3(c * jax.nn.sigmoid(c), w_ref[...]) + b_ref[...]


def _ada(c, w, b):
    bsz, d = c.shape
    n = w.shape[1]
    tn = 1024
    return pl.pallas_call(
        _ada_kernel,
        out_shape=jax.ShapeDtypeStruct((bsz, n), F32),
        grid=(n // tn,),
        in_specs=[pl.BlockSpec((bsz, d), lambda j: (0, 0)),
                  pl.BlockSpec((d, tn), lambda j: (0, j)),
                  pl.BlockSpec((1, tn), lambda j: (0, j))],
        out_specs=pl.BlockSpec((bsz, tn), lambda j: (0, j)),
        compiler_params=_cparams(("arbitrary",)),
        name="ada",
    )(c, w, b.reshape(1, n))


def _inproj_kernel(x_ref, ada_ref, g1_ref, w_ref, wt_ref,
                   us_ref, q_ref, qi_ref, k_ref, ki_ref, vt_ref, wit_ref, *, d, ssm_w, attn_w, idx_w):
    x = x_ref[0]
    shift = ada_ref[0, :, 0:d]
    scale = ada_ref[0, :, d:2 * d]
    u = _bf(_rms(x, g1_ref[...]) * (1.0 + scale) + shift)
    r = _dot(u, w_ref[...])
    o = 0
    us_ref[0] = r[:, o:o + ssm_w]
    o += ssm_w
    q_ref[0] = _bf(r[:, o:o + attn_w])
    o += attn_w
    qi_ref[0] = _bf(r[:, o:o + idx_w])
    o += idx_w
    k_ref[0] = _bf(r[:, o:o + HEAD_DIM])
    o += LANES
    ki_ref[0] = _bf(r[:, o:o + IDX_DIM])
    rt = _dot_nt(wt_ref[...], u)
    vt_ref[0] = _bf(rt[0:HEAD_DIM])
    wit_ref[0] = rt[HEAD_DIM:HEAD_DIM + IDX_HEADS]


def _split_w_in(w_in, d):
    ssm_w = 512
    sizes = (ssm_w, N_HEADS * HEAD_DIM, HEAD_DIM, HEAD_DIM, IDX_HEADS * IDX_DIM, IDX_DIM, IDX_HEADS, d, d)
    offs = [0]
    for s in sizes:
        offs.append(offs[-1] + s)
    return [w_in[:, offs[i]:offs[i + 1]] for i in range(9)]


def _inproj(x, ada3, g1, w_in):
    bsz, seq, d = x.shape
    ssm_w = 512
    attn_w = N_HEADS * HEAD_DIM
    idx_w = IDX_HEADS * IDX_DIM
    w_ssm, w_q, w_k, w_v, w_qi, w_ki, w_wi, _, _ = _split_w_in(w_in, d)
    zpad = lambda n: jnp.zeros((d, n), F32)
    wbig = jnp.concatenate([
        w_ssm, w_q * (HEAD_DIM ** -0.5), w_qi * (IDX_DIM ** -0.5),
        w_k, zpad(LANES - HEAD_DIM), w_ki, zpad(LANES - IDX_DIM)], axis=1).astype(BF16)
    wt = jnp.concatenate([w_v, w_wi, zpad(LANES - HEAD_DIM - IDX_HEADS)], axis=1).T.astype(BF16)
    nw = wbig.shape[1]
    tl = INPROJ_ROWS
    kern = functools.partial(_inproj_kernel, d=d, ssm_w=ssm_w, attn_w=attn_w, idx_w=idx_w)
    row = lambda w: pl.BlockSpec((1, tl, w), lambda b, l: (b, l, 0))
    colt = lambda h: pl.BlockSpec((1, h, tl), lambda b, l: (b, 0, l))
    return pl.pallas_call(
        kern,
        out_shape=(jax.ShapeDtypeStruct((bsz, seq, ssm_w), F32),
                   jax.ShapeDtypeStruct((bsz, seq, attn_w), BF16),
                   jax.ShapeDtypeStruct((bsz, seq, idx_w), BF16),
                   jax.ShapeDtypeStruct((bsz, seq, HEAD_DIM), BF16),
                   jax.ShapeDtypeStruct((bsz, seq, IDX_DIM), BF16),
                   jax.ShapeDtypeStruct((bsz, HEAD_DIM, seq), BF16),
                   jax.ShapeDtypeStruct((bsz, IDX_HEADS, seq), F32)),
        grid=(bsz, seq // tl),
        in_specs=[row(d),
                  pl.BlockSpec((1, 1, ada3.shape[2]), lambda b, l: (b, 0, 0)),
                  pl.BlockSpec((1, d), lambda b, l: (0, 0)),
                  pl.BlockSpec((d, nw), lambda b, l: (0, 0)),
                  pl.BlockSpec((LANES, d), lambda b, l: (0, 0))],
        out_specs=(row(ssm_w), row(attn_w), row(idx_w), row(HEAD_DIM), row(IDX_DIM),
                   colt(HEAD_DIM), colt(IDX_HEADS)),
        compiler_params=_cparams(("arbitrary", "arbitrary")),
        name="inproj",
    )(x, ada3, g1.reshape(1, d), wbig, wt)


def _s5disc_kernel(lr_ref, li_ref, ldt_ref, br_ref, bi_ref, are_ref, aim_ref, bbr_ref, bbi_ref):
    lr = lr_ref[...]
    li = li_ref[...]
    dt = jnp.exp(ldt_ref[...])
    mag = jnp.exp(lr * dt)
    a_re = mag * jnp.cos(li * dt)
    a_im = mag * jnp.sin(li * dt)
    den = lr * lr + li * li
    n_re = a_re - 1.0
    f_re = (n_re * lr + a_im * li) / den
    f_im = (a_im * lr - n_re * li) / den
    br = br_ref[...]
    bi = bi_ref[...]
    are_ref[...] = a_re
    aim_ref[...] = a_im
    bbr_ref[...] = f_re * br - f_im * bi
    bbi_ref[...] = f_re * bi + f_im * br


def _s5disc(lam_re, lam_im, log_dt, b_re, b_im):
    g, p = lam_re.shape
    h = b_re.shape[2]
    rep = lambda a: jnp.repeat(a, h, axis=1)
    ldt = jnp.broadcast_to(log_dt[:, None], (g, p * h))
    sds = jax.ShapeDtypeStruct((g, p * h), F32)
    a_re, a_im, bb_re, bb_im = pl.pallas_call(
        _s5disc_kernel, out_shape=(sds, sds, sds, sds), name="s5disc",
    )(rep(lam_re), rep(lam_im), ldt, b_re.reshape(g, p * h), b_im.reshape(g, p * h))
    return a_re[:, ::h], a_im[:, ::h], bb_re.reshape(g, p, h), bb_im.reshape(g, p, h)


def _s5_kernel(u_ref, wb_ref, ar_ref, ai_ref, cc_ref, dsk_ref, wg_ref, bg_ref, o_ref, buf, hst, *, tl, width):
    nch = width // S5_LANE_CHUNK
    sw = S5_LANE_CHUNK // SSM_GROUP * SSM_STATE
    rows = tl * SUBLANES

    @pl.when(pl.program_id(0) == 0)
    def _():
        hst[...] = jnp.zeros_like(hst)

    u = u_ref[...].reshape(rows, width)
    ub = _bf(u)
    for j in range(nch):
        buf[:, j * 2 * sw:(j + 1) * 2 * sw] = _dot(ub[:, j * S5_LANE_CHUNK:(j + 1) * S5_LANE_CHUNK], wb_ref[j])

    for j in range(nch):
        re_cols = slice(j * 2 * sw, j * 2 * sw + sw)
        im_cols = slice(j * 2 * sw + sw, (j + 1) * 2 * sw)
        a_re = jnp.broadcast_to(ar_ref[:, j * sw:(j + 1) * sw], (SUBLANES, sw))
        a_im = jnp.broadcast_to(ai_ref[:, j * sw:(j + 1) * sw], (SUBLANES, sw))

        def step(t, carry, re_cols=re_cols, im_cols=im_cols, a_re=a_re, a_im=a_im):
            h_re, h_im = carry
            r0 = pl.multiple_of(t * SUBLANES, SUBLANES)
            n_re = (a_re * h_re - a_im * h_im) + buf[pl.ds(r0, SUBLANES), re_cols]
            n_im = (a_re * h_im + a_im * h_re) + buf[pl.ds(r0, SUBLANES), im_cols]
            buf[pl.ds(r0, SUBLANES), re_cols] = n_re
            buf[pl.ds(r0, SUBLANES), im_cols] = n_im
            return n_re, n_im

        h_re, h_im = lax.fori_loop(0, tl, step, (hst[:, re_cols], hst[:, im_cols]), unroll=8)
        hst[:, re_cols] = h_re
        hst[:, im_cols] = h_im

    ys = [_dot(_bf(buf[:, j * 2 * sw:(j + 1) * 2 * sw]), cc_ref[j]) for j in range(nch)]
    y = jnp.concatenate(ys, axis=1) + dsk_ref[...] * u
    y = jax.nn.gelu(y)
    y = y * jax.nn.sigmoid(_dot(_bf(y), wg_ref[...]) + bg_ref[...])
    o_ref[...] = _bf(y).reshape(tl, SUBLANES, width)


def _s5(u_t, a_re, a_im, bb_re, bb_im, c_re, c_im, d_skip, w_glu, b_glu):
    seq, bsz, width = u_t.shape
    assert bsz == SUBLANES
    nch = width // S5_LANE_CHUNK
    gpc = S5_LANE_CHUNK // SSM_GROUP
    sw = gpc * SSM_STATE
    eye = jnp.eye(gpc, dtype=F32)

    def bmat(bb):
        t = bb.reshape(nch, gpc, SSM_STATE, SSM_GROUP).transpose(0, 1, 3, 2)
        return jnp.einsum('jghp,gk->jghkp', t, eye).reshape(nch, S5_LANE_CHUNK, sw)

    def cmat(cc):
        t = cc.reshape(nch, gpc, SSM_GROUP, SSM_STATE).transpose(0, 1, 3, 2)
        return jnp.einsum('jgph,gk->jgpkh', t, eye).reshape(nch, sw, S5_LANE_CHUNK)

    wb = jnp.concatenate([bmat(bb_re), bmat(bb_im)], axis=2)
    cc = jnp.concatenate([cmat(c_re), -cmat(c_im)], axis=1)
    tl = S5_STEPS
    full = lambda a: pl.BlockSpec(a.shape, lambda i: (0,) * a.ndim)
    args = (u_t, wb.astype(BF16), a_re.reshape(1, -1), a_im.reshape(1, -1), cc.astype(BF16),
            d_skip.reshape(1, width), w_glu.astype(BF16), b_glu.reshape(1, width))
    return pl.pallas_call(
        functools.partial(_s5_kernel, tl=tl, width=width),
        out_shape=jax.ShapeDtypeStruct((seq, bsz, width), BF16),
        grid=(seq // tl,),
        in_specs=[pl.BlockSpec((tl, bsz, width), lambda i: (i, 0, 0))] + [full(a) for a in args[1:]],
        out_specs=pl.BlockSpec((tl, bsz, width), lambda i: (i, 0, 0)),
        scratch_shapes=[pltpu.VMEM((tl * SUBLANES, nch * 2 * sw), F32),
                        pltpu.VMEM((SUBLANES, nch * 2 * sw), F32)],
        compiler_params=_cparams(("arbitrary",)),
        name="s5",
    )(*args)


def _bit_transpose32(words):
    x = list(words)
    j, m = 16, 0x0000FFFF
    while j:
        k = 0
        while k < 32:
            t = (x[k] ^ lax.shift_right_logical(x[k + j], jnp.int32(j))) & jnp.int32(m - (1 << 32) if m >= 1 << 31 else m)
            x[k] = x[k] ^ t
            x[k + j] = x[k + j] ^ lax.shift_left(t, jnp.int32(j))
            k = (k + j + 1) & ~j
        j >>= 1
        m = (m ^ (m << j)) & 0xFFFFFFFF
    return x


def _dsa_kernel(qt_ref, qit_ref, wit_ref, ka_ref, ki_ref, vt_ref, o_ref, key_s, mb_s, acc_s, pl_s, p_s, *, tq, tk, topk, seq):
    i = pl.program_id(1)
    q0 = i * tq
    nkt = (q0 + tq + tk - 1) // tk
    ch = DSA_COUNT_ROWS
    krow = lax.broadcasted_iota(I32, (tk, tq), 0)
    qcol = q0 + lax.broadcasted_iota(I32, (tk, tq), 1)
    crow = lax.broadcasted_iota(I32, (ch, tq), 0)

    wb = wit_ref[0] * (IDX_HEADS ** -0.5)

    def score_tile(j, _):
        r0 = pl.multiple_of(j * tk, tk)
        kit = ki_ref[0, pl.ds(r0, tk), :]
        acc = jnp.zeros((tk, tq), F32)
        for h in range(IDX_HEADS):
            s = _dot(kit, qit_ref[0, h * IDX_DIM:(h + 1) * IDX_DIM, :])
            acc = acc + wb[h:h + 1, :] * jnp.maximum(s, 0.0)
        bits = lax.bitcast_convert_type(acc, I32)
        key = jnp.where(bits < 0, bits ^ jnp.int32(0x7FFFFFFF), bits)
        key = jnp.where(acc == 0.0, 0, key)
        key = jnp.where(krow + r0 <= qcol, key, INT_MIN)
        key_s[pl.ds(r0, tk), :] = key
        ukey = key ^ INT_MIN
        for c in range(tk // BITSLICE_ROWS):
            words = [ukey[c * BITSLICE_ROWS + v * SUBLANES:c * BITSLICE_ROWS + (v + 1) * SUBLANES, :]
                     for v in range(32)]
            planes = _bit_transpose32(words)
            g0 = pl.multiple_of((j * (tk // BITSLICE_ROWS) + c) * SUBLANES, SUBLANES)
            for it in range(32):
                pl_s[it, pl.ds(g0, SUBLANES), :] = planes[it]
        return 0

    @pl.when((pl.program_id(0) == 0) & (i == 0))
    def _():
        pl_s[...] = jnp.zeros(pl_s.shape, I32)

    lax.fori_loop(0, nkt, score_tile, 0)

    def count(pred):
        def tile(j, cnt):
            for c in range(tk // ch):
                rr = pl.multiple_of(j * tk + c * ch, ch)
                cnt = cnt + jnp.where(pred(key_s[pl.ds(rr, ch), :], rr), 1, 0)
            return cnt
        cnt = lax.fori_loop(0, nkt, tile, jnp.zeros((ch, tq), I32))
        return jnp.sum(cnt.astype(F32), axis=0, keepdims=True)

    ngrp = seq // 32

    def lane_count(words):
        pc = lax.population_count(words).reshape(ngrp // SUBLANES, SUBLANES, tq)
        return jnp.sum(jnp.sum(pc, axis=0).astype(F32), axis=0, keepdims=True)

    def bit_step(it, carry):
        alive, above, ans_u = carry
        ones = alive & pl_s[it]
        cnt1 = lane_count(ones)
        take = above + cnt1 >= float(topk)
        alive = jnp.where(take, ones, alive ^ ones)
        above = jnp.where(take, above, above + cnt1)
        ans_u = jnp.where(take, ans_u | lax.shift_left(jnp.int32(1), 31 - it), ans_u)
        return alive, above, ans_u

    grow = lax.broadcasted_iota(I32, (ngrp, tq), 0)
    alive0 = jnp.where(grow < nkt * (tk // 32), -1, 0)
    alive, above, ans_u = lax.fori_loop(
        0, 32, bit_step, (alive0, jnp.zeros((1, tq), F32), jnp.zeros((1, tq), I32)))
    thr = jnp.maximum(ans_u ^ INT_MIN, INT_MIN + 1)
    cnt_ge = above + lane_count(alive)
    tied = jnp.where(ans_u != 0, cnt_ge, 0.0) > float(topk)
    has_ties = jnp.max(jnp.where(tied, 1.0, 0.0)) > 0.0

    def tie_cut():
        need = float(topk) - count(lambda kb, rr: kb > thr)
        nbits = max(1, (seq - 1).bit_length())

        def idx_step(b, x):
            cand = x | lax.shift_left(jnp.int32(1), nbits - 1 - b)
            below = count(lambda kb, rr: jnp.where(kb == thr, crow + rr, seq) < cand)
            return jnp.where(below < need, cand, x)

        x = lax.fori_loop(0, nbits, idx_step, jnp.zeros((1, tq), I32))
        return jnp.where(tied, x, seq)

    cut = lax.cond(has_ties, tie_cut, lambda: jnp.full((1, tq), seq, I32))

    def bias_tile(j, _):
        for c in range(tk // ch):
            rr = pl.multiple_of(j * tk + c * ch, ch)
            kb = key_s[pl.ds(rr, ch), :]
            tie_bias = jnp.where(crow + rr <= cut, 0.0, NEG_BIG)
            mb_s[pl.ds(rr, ch), :] = jnp.where(kb > thr, 0.0, jnp.where(kb == thr, tie_bias, NEG_BIG))
        return 0

    lax.fori_loop(0, nkt, bias_tile, 0)

    def logits(j, h):
        r0 = pl.multiple_of(j * tk, tk)
        s = _dot(ka_ref[0, pl.ds(r0, tk), :], qt_ref[0, h * LANES:(h + 1) * LANES, :]) + mb_s[pl.ds(r0, tk), :]
        return s.reshape(tk // SUBLANES, SUBLANES, tq)

    def max_tile(j, ms):
        return tuple(jnp.maximum(ms[h], jnp.max(logits(j, h), axis=0)) for h in range(N_HEADS))

    ms = lax.fori_loop(0, nkt, max_tile, (jnp.full((SUBLANES, tq), NEG_BIG, F32),) * N_HEADS)
    m = [jnp.max(mh, axis=0, keepdims=True) for mh in ms]
    acc_s[...] = jnp.zeros(acc_s.shape, F32)

    def sum_tile(j, ls):
        r0 = pl.multiple_of(j * tk, tk)
        out = []
        for h in range(N_HEADS):
            p = jnp.exp(logits(j, h) - m[h])
            out.append(ls[h] + jnp.sum(p, axis=0))
            p_s[h] = _bf(p.reshape(tk, tq))
        for h in range(N_HEADS):
            rows = slice(h * HEAD_DIM, (h + 1) * HEAD_DIM)
            acc_s[rows, :] = acc_s[rows, :] + _dot(vt_ref[0, :, pl.ds(r0, tk)], p_s[h])
        return tuple(out)

    ls = lax.fori_loop(0, nkt, sum_tile, (jnp.zeros((SUBLANES, tq), F32),) * N_HEADS)
    for h in range(N_HEADS):
        rows = slice(h * HEAD_DIM, (h + 1) * HEAD_DIM)
        o_ref[0, rows, :] = _bf(acc_s[rows, :] / jnp.sum(ls[h], axis=0, keepdims=True))


def _dsa(q, qi, wit, k, ki, vt):
    bsz, seq, aw = q.shape
    tq = min(DSA_Q_COLS, seq)
    tk = min(DSA_K_ROWS, seq)
    topk = min(TOPK_MAX, seq // 4)
    assert (seq - 1) // POS_SPLIT < 256 and POS_SPLIT <= 256, "key positions must split into two bf16-exact parts"
    slopes = [2.0 ** (-8.0 * (h + 1) / N_HEADS) for h in range(N_HEADS)]
    assert all(float(np.float32(sl).astype(BF16)) == sl for sl in slopes), "ALiBi slopes must be bf16-exact"
    pos = jnp.arange(seq, dtype=I32)
    posc = jnp.stack([(pos // POS_SPLIT) * POS_SPLIT, pos % POS_SPLIT], axis=1).astype(BF16)
    ka = jnp.concatenate([k, jnp.broadcast_to(posc[None], (bsz, seq, 2)),
                          jnp.zeros((bsz, seq, LANES - HEAD_DIM - 2), BF16)], axis=2)
    qh = q.reshape(bsz, seq, N_HEADS, HEAD_DIM).transpose(0, 2, 3, 1)
    srow = jnp.asarray(slopes, BF16)[None, :, None, None]
    extra = jnp.concatenate([jnp.broadcast_to(srow, (bsz, N_HEADS, 2, seq)),
                             jnp.zeros((bsz, N_HEADS, LANES - HEAD_DIM - 2, seq), BF16)], axis=2)
    qt = jnp.concatenate([qh, extra], axis=2).reshape(bsz, N_HEADS * LANES, seq)
    qit = qi.transpose(0, 2, 1)
    kern = functools.partial(_dsa_kernel, tq=tq, tk=tk, topk=topk, seq=seq)
    cols = lambda r: pl.BlockSpec((1, r, tq), lambda b, i: (b, 0, i))
    return pl.pallas_call(
        kern,
        out_shape=jax.ShapeDtypeStruct((bsz, aw, seq), BF16),
        grid=(bsz, seq // tq),
        in_specs=[cols(N_HEADS * LANES), cols(qit.shape[1]), cols(IDX_HEADS),
                  pl.BlockSpec((1, seq, LANES), lambda b, i: (b, 0, 0)),
                  pl.BlockSpec((1, seq, IDX_DIM), lambda b, i: (b, 0, 0)),
                  pl.BlockSpec((1, HEAD_DIM, seq), lambda b, i: (b, 0, 0))],
        out_specs=cols(aw),
        scratch_shapes=[pltpu.VMEM((seq, tq), I32), pltpu.VMEM((seq, tq), F32), pltpu.VMEM((aw, tq), F32),
                        pltpu.VMEM((32, seq // 32, tq), I32), pltpu.VMEM((N_HEADS, tk, tq), BF16)],
        compiler_params=_cparams(("arbitrary", "arbitrary")),
        name="dsa",
    )(qt, qit, wit, ka, ki, vt)


def _mix_kernel(x_ref, ys_ref, ya_ref, ada_ref, g1_ref, wgt_ref, wps_ref, wpa_ref, wo_ref, g2_ref,
                h_ref, u2_ref, *, d):
    gate1 = ada_ref[0, :, 2 * d:3 * d]
    shift2 = ada_ref[0, :, 3 * d:4 * d]
    scale2 = ada_ref[0, :, 4 * d:5 * d]
    x = x_ref[0]
    u = _bf(_rms(x, g1_ref[...]) * (1.0 + ada_ref[0, :, d:2 * d]) + ada_ref[0, :, 0:d])
    g = _dot(u, wgt_ref[...])
    mixed = (jax.nn.sigmoid(g[:, 0:d]) * _dot(ys_ref[0], wps_ref[...])
             + jax.nn.sigmoid(g[:, d:2 * d]) * _dot(ya_ref[0], wpa_ref[...]))
    h = x + gate1 * _dot(_bf(mixed), wo_ref[...])
    h_ref[0] = h
    u2_ref[0] = _rms(h, g2_ref[...]) * (1.0 + scale2) + shift2


def _mix(x, ys, ya, ada3, g1, w_in, wps, wpa, wo, g2):
    bsz, seq, d = x.shape
    tm = MIX_ROWS
    row = lambda w: pl.BlockSpec((1, tm, w), lambda b, l: (b, l, 0))
    full = lambda a: pl.BlockSpec(a.shape, lambda b, l: (0,) * a.ndim)
    wps, wpa, wo = wps.astype(BF16), wpa.astype(BF16), wo.astype(BF16)
    wgt = jnp.concatenate(_split_w_in(w_in, d)[7:9], axis=1).astype(BF16)
    g1 = g1.reshape(1, d)
    g2 = g2.reshape(1, d)
    return pl.pallas_call(
        functools.partial(_mix_kernel, d=d),
        out_shape=(jax.ShapeDtypeStruct((bsz, seq, d), F32), jax.ShapeDtypeStruct((bsz, seq, d), F32)),
        grid=(bsz, seq // tm),
        in_specs=[row(d), row(ys.shape[2]), row(ya.shape[2]),
                  pl.BlockSpec((1, 1, ada3.shape[2]), lambda b, l: (b, 0, 0)),
                  full(g1), full(wgt), full(wps), full(wpa), full(wo), full(g2)],
        out_specs=(row(d), row(d)),
        compiler_params=_cparams(("arbitrary", "arbitrary")),
        name="mix",
    )(x, ys, ya, ada3, g1, wgt, wps, wpa, wo, g2)


def _first_max(cur, idx, axis, big):
    m = jnp.max(cur, axis=axis, keepdims=True)
    first = jnp.min(jnp.where(cur == m, idx, big), axis=axis, keepdims=True)
    return m, idx == first


def _route_kernel(u_ref, wrh_ref, wrl_ref, rb_ref, tri_ref, ltri_ref,
                  gt_ref, loc_ref, c8_ref, loff_ref, run0_ref, tot_ref, run_s, *, t):
    @pl.when(pl.program_id(0) == 0)
    def _():
        run_s[...] = jnp.zeros_like(run_s)

    uh, ul = _split(u_ref[...])
    logits = _dot_nt(wrh_ref[...], uh) + (_dot_nt(wrl_ref[...], uh) + _dot_nt(wrh_ref[...], ul))
    scores = jax.nn.sigmoid(logits)
    biased = scores + rb_ref[...]
    per_group = N_EXPERTS // N_GROUPS
    b3 = biased.reshape(N_GROUPS, per_group, t)
    i3 = lax.broadcasted_iota(I32, b3.shape, 1)
    m1, hit1 = _first_max(b3, i3, 1, per_group)
    m2 = jnp.max(jnp.where(hit1, -jnp.inf, b3), axis=1, keepdims=True)
    gs = (m1 + m2).reshape(N_GROUPS, t)
    gi = lax.broadcasted_iota(I32, gs.shape, 0)
    gsel = jnp.zeros(gs.shape, F32)
    for _ in range(TOPK_GROUPS):
        _, hit = _first_max(gs, gi, 0, N_GROUPS)
        gsel = jnp.where(hit, 1.0, gsel)
        gs = jnp.where(hit, -jnp.inf, gs)
    cur = jnp.where(gsel.reshape(N_GROUPS, 1, t) > 0.0, b3, -jnp.inf).reshape(N_EXPERTS, t)
    ei = lax.broadcasted_iota(I32, cur.shape, 0)
    hits = []
    gates = []
    for _ in range(TOP_K):
        _, hit = _first_max(cur, ei, 0, N_EXPERTS)
        hits.append(hit)
        gates.append(jnp.sum(jnp.where(hit, scores, 0.0), axis=0, keepdims=True))
        cur = jnp.where(hit, -jnp.inf, cur)
    gate = jnp.concatenate(gates, axis=0)
    gt_ref[...] = gate / jnp.sum(gate, axis=0, keepdims=True) * ROUTED_SCALE
    onehot = jnp.zeros(cur.shape, F32)
    for hit in hits:
        onehot = jnp.where(hit, 1.0, onehot)
    cnt = jnp.sum(onehot, axis=1, keepdims=True)
    c8 = jnp.floor((cnt + (SUBLANES - 1)) * (1.0 / SUBLANES)) * SUBLANES
    c8l = jnp.broadcast_to(c8, (N_EXPERTS, LANES))
    loff = _dot(ltri_ref[...], _bf(c8l))
    slot = _dot(_bf(onehot), tri_ref[...]) + loff[:, 0:1]
    loc_ref[...] = jnp.concatenate(
        [jnp.sum(jnp.where(hit, slot, 0.0), axis=0, keepdims=True) for hit in hits], axis=0).astype(I32)
    c8_ref[0] = c8l
    loff_ref[0] = loff
    run0_ref[0] = run_s[...]
    run_s[...] = run_s[...] + c8
    tot_ref[...] = run_s[...]


def _route(u2, w_router, router_bias):
    n, d = u2.shape
    t = min(MOE_TILE, n)
    nt = n // t
    wt = w_router.T
    wrh = wt.astype(BF16)
    wrl = (wt - wrh.astype(F32)).astype(BF16)
    tri = (jnp.arange(t)[:, None] < jnp.arange(t)[None, :]).astype(BF16)
    ex = jnp.arange(N_EXPERTS)
    ltri = (ex[None, :] < ex[:, None]).astype(BF16)
    full = lambda a: pl.BlockSpec(a.shape, lambda i: (0,) * a.ndim)
    col = pl.BlockSpec((TOP_K, t), lambda i: (0, i))
    tab = pl.BlockSpec((1, N_EXPERTS, LANES), lambda i: (i, 0, 0))
    tab_sds = jax.ShapeDtypeStruct((nt, N_EXPERTS, LANES), F32)
    rb = router_bias.reshape(N_EXPERTS, 1)
    return pl.pallas_call(
        functools.partial(_route_kernel, t=t),
        out_shape=(jax.ShapeDtypeStruct((TOP_K, n), F32), jax.ShapeDtypeStruct((TOP_K, n), I32),
                   tab_sds, tab_sds, tab_sds, jax.ShapeDtypeStruct((N_EXPERTS, LANES), F32)),
        grid=(nt,),
        in_specs=[pl.BlockSpec((t, d), lambda i: (i, 0)), full(wrh), full(wrl), full(rb), full(tri), full(ltri)],
        out_specs=(col, col, tab, tab, tab, pl.BlockSpec((N_EXPERTS, LANES), lambda i: (0, 0))),
        scratch_shapes=[pltpu.VMEM((N_EXPERTS, LANES), F32)],
        compiler_params=_cparams(("arbitrary",)),
        name="route",
    )(u2, wrh, wrl, rb, tri, ltri)


RUN_BITS = tuple(1 << b for b in reversed(range((MOE_TILE // SUBLANES).bit_length())))


def _for_each_run_piece(n8_ref, src_ref, dst_ref, tile, bits, fn):
    def per_expert(e, _):
        idx = tile * N_EXPERTS + e
        n8 = n8_ref[idx]
        src = src_ref[idx]
        dst = dst_ref[idx]
        for b, p in enumerate(bits):
            off = (n8 & ~(2 * p - 1)) * SUBLANES

            @pl.when((n8 & p) != 0)
            def _(b=b, p=p, off=off):
                fn(pl.multiple_of(src + off, SUBLANES), pl.multiple_of(dst + off, SUBLANES), p * SUBLANES, b % 2)
        return 0

    lax.fori_loop(0, N_EXPERTS, per_expert, 0)


def _wait_rows(n8, make_copy, max_rows):
    for p in tuple(1 << b for b in reversed(range((max_rows // SUBLANES).bit_length()))):
        @pl.when((n8 & p) != 0)
        def _(p=p):
            make_copy(p * SUBLANES).wait()


def _dispatch_kernel(n8_ref, src_ref, dst_ref, tot_ref, zn8_ref, zdst_ref, u_ref, loc_ref, gate_ref, xs_hbm,
                     lbuf, zx, sems, *, tt, nslot, dh):
    i = pl.program_id(0)
    last = pl.num_programs(0) - 1
    par = i % 2

    def run_copy(slot):
        def piece(s0, d0, rows, prio):
            pltpu.make_async_copy(
                lbuf.at[slot, pl.ds(s0, rows)], xs_hbm.at[pl.ds(d0, rows)], sems.at[slot]).start(priority=prio)
        return piece

    def wait_tile(tile, slot):
        _wait_rows(tot_ref[tile], lambda rows: pltpu.make_async_copy(
            lbuf.at[slot, pl.ds(0, rows)], xs_hbm.at[pl.ds(0, rows)], sems.at[slot]), nslot)

    @pl.when(i == 0)
    def _():
        zx[...] = jnp.zeros(zx.shape, I32)

        def zero_piece(s0, d0, rows, prio):
            cx = pltpu.make_async_copy(zx.at[pl.ds(0, rows)], xs_hbm.at[pl.ds(d0, rows)], sems.at[2])
            cx.start()
            cx.wait()

        zbits = tuple(b for b in RUN_BITS if b * SUBLANES < EXPERT_ROWS)
        _for_each_run_piece(zn8_ref, zdst_ref, zdst_ref, 0, zbits, zero_piece)

    ub = _bf(u_ref[...])
    ones = jnp.ones((tt, LANES), BF16)
    loc = loc_ref[...]
    gate = gate_ref[...]
    for c in range(nslot // SLOT_CHUNK):
        rows = c * SLOT_CHUNK + lax.broadcasted_iota(I32, (SLOT_CHUNK, tt), 0)
        perm = jnp.zeros((SLOT_CHUNK, tt), F32)
        pgate = jnp.zeros((SLOT_CHUNK, tt), F32)
        for k in range(TOP_K):
            eq = rows == loc[k:k + 1, :]
            perm = jnp.where(eq, 1.0, perm)
            pgate = jnp.where(eq, gate[k:k + 1, :], pgate)
        cs = slice(c * SLOT_CHUNK, (c + 1) * SLOT_CHUNK)
        xp = lax.bitcast_convert_type(_dot(_bf(perm), ub), I32)
        lbuf[par, cs, 0:dh] = xp[:, 0:dh] | lax.shift_right_logical(xp[:, dh:2 * dh], 16)
        gh, gl = _split(pgate)
        lbuf[par, cs, dh:dh + LANES] = lax.bitcast_convert_type(_dot(gh, ones) + _dot(gl, ones), I32)

    _for_each_run_piece(n8_ref, src_ref, dst_ref, i, RUN_BITS, run_copy(par))

    @pl.when(i > 0)
    def _():
        wait_tile(i - 1, 1 - par)

    @pl.when(i == last)
    def _():
        wait_tile(i, par)


def _dispatch(tabs, ztabs, u2, loc_t, gate_t, n_rows):
    n, d = u2.shape
    tt = min(MOE_TILE, n)
    nslot = TOP_K * tt + N_EXPERTS * SUBLANES
    dh = d // 2
    assert nslot % SLOT_CHUNK == 0 and tt // SUBLANES == RUN_BITS[0]
    col = pl.BlockSpec((TOP_K, tt), lambda i, *_: (0, i))
    return pl.pallas_call(
        functools.partial(_dispatch_kernel, tt=tt, nslot=nslot, dh=dh),
        out_shape=jax.ShapeDtypeStruct((n_rows, dh + LANES), I32),
        grid_spec=pltpu.PrefetchScalarGridSpec(
            num_scalar_prefetch=6, grid=(n // tt,),
            in_specs=[pl.BlockSpec((tt, d), lambda i, *_: (i, 0)), col, col],
            out_specs=pl.BlockSpec(memory_space=pl.ANY),
            scratch_shapes=[pltpu.VMEM((2, nslot, dh + LANES), I32),
                            pltpu.VMEM((EXPERT_ROWS // 2, dh + LANES), I32),
                            pltpu.SemaphoreType.DMA((3,))]),
        compiler_params=_cparams(("arbitrary",)),
        name="dispatch",
    )(*tabs, *ztabs, u2, loc_t, gate_t)


def _experts_kernel(be_ref, nu_ref, xs_ref, wg_ref, wu_ref, wd_ref, ys_ref, *, d):
    del be_ref
    dh = d // 2

    @pl.when(pl.program_id(0) < nu_ref[0])
    def _():
        w = xs_ref[:, 0:dh]
        xa = _bf(lax.bitcast_convert_type(w & jnp.int32(-65536), F32))
        xb = _bf(lax.bitcast_convert_type(lax.shift_left(w, 16), F32))
        gate = lax.bitcast_convert_type(xs_ref[:, dh:dh + LANES], F32)
        hg = _dot(xa, wg_ref[0, 0:dh, :]) + _dot(xb, wg_ref[0, dh:d, :])
        hu = _dot(xa, wu_ref[0, 0:dh, :]) + _dot(xb, wu_ref[0, dh:d, :])
        ys_ref[...] = _dot(_bf(jax.nn.silu(hg) * hu), wd_ref[0]) * jnp.tile(gate, (1, d // LANES))


def _experts(blk_expert, n_used, xs, wg, wu, wd):
    rows, xw = xs.shape
    d = wg.shape[1]
    de = wg.shape[2]
    nblk = rows // EXPERT_ROWS
    blk = lambda i, be, nu: jnp.minimum(i, nu[0] - 1)
    return pl.pallas_call(
        functools.partial(_experts_kernel, d=d),
        out_shape=jax.ShapeDtypeStruct((rows, d), F32),
        grid_spec=pltpu.PrefetchScalarGridSpec(
            num_scalar_prefetch=2, grid=(nblk,),
            in_specs=[pl.BlockSpec((EXPERT_ROWS, xw), lambda i, be, nu: (blk(i, be, nu), 0)),
                      pl.BlockSpec((1, d, de), lambda i, be, nu: (be[blk(i, be, nu)], 0, 0)),
                      pl.BlockSpec((1, d, de), lambda i, be, nu: (be[blk(i, be, nu)], 0, 0)),
                      pl.BlockSpec((1, de, d), lambda i, be, nu: (be[blk(i, be, nu)], 0, 0))],
            out_specs=pl.BlockSpec((EXPERT_ROWS, d), lambda i, be, nu: (blk(i, be, nu), 0))),
        compiler_params=_cparams(("arbitrary",)),
        name="experts",
    )(blk_expert, n_used, xs, wg.astype(BF16), wu.astype(BF16), wd.astype(BF16))


def _combine_kernel(n8_ref, src_ref, dst_ref, tot_ref, ys_hbm, loc_ref, h_ref, u2_ref, ada_ref, wsg_ref, wsu_ref, wsd_ref,
                    gf_ref, o_ref, ybuf, sems, *, tt, nslot, d):
    i = pl.program_id(0)
    last = pl.num_programs(0) - 1
    par = i % 2

    def run_copy(slot):
        def piece(s0, d0, rows, prio):
            pltpu.make_async_copy(
                ys_hbm.at[pl.ds(d0, rows)], ybuf.at[slot, pl.ds(s0, rows)], sems.at[slot]).start(priority=prio)
        return piece

    @pl.when(i == 0)
    def _():
        ybuf[...] = jnp.zeros(ybuf.shape, F32)
        _for_each_run_piece(n8_ref, src_ref, dst_ref, 0, RUN_BITS, run_copy(0))

    @pl.when(i < last)
    def _():
        _for_each_run_piece(n8_ref, src_ref, dst_ref, i + 1, RUN_BITS, run_copy(1 - par))

    x = _bf(u2_ref[...])
    shared = _dot(_bf(jax.nn.silu(_dot(x, wsg_ref[...])) * _dot(x, wsu_ref[...])), wsd_ref[...])
    _wait_rows(tot_ref[i], lambda rows: pltpu.make_async_copy(
        ys_hbm.at[pl.ds(0, rows)], ybuf.at[par, pl.ds(0, rows)], sems.at[par]), nslot)

    loc = loc_ref[...]
    routed = jnp.zeros((tt, d), F32)
    for c in range(nslot // SLOT_CHUNK):
        cols = c * SLOT_CHUNK + lax.broadcasted_iota(I32, (tt, SLOT_CHUNK), 1)
        pick = jnp.zeros((tt, SLOT_CHUNK), F32)
        for k in range(TOP_K):
            pick = jnp.where(cols == loc[:, k:k + 1], 1.0, pick)
        pick = _bf(pick)
        yh, yl = _split(ybuf[par, c * SLOT_CHUNK:(c + 1) * SLOT_CHUNK, :])
        routed = routed + (_dot(pick, yh) + _dot(pick, yl))
    gate2 = ada_ref[0, :, 5 * d:6 * d]
    h = h_ref[...] + gate2 * (routed + shared)
    o_ref[...] = _rms(h, gf_ref[...])


def _combine(tabs, ys, loc, h1, u2, ada3, wsg, wsu, wsd, gf, seq):
    n, d = h1.shape
    tt = min(MOE_TILE, n)
    nslot = TOP_K * tt + N_EXPERTS * SUBLANES
    per_b = seq // tt
    row = pl.BlockSpec((tt, d), lambda i, *_: (i, 0))
    full = lambda a: pl.BlockSpec(a.shape, lambda i, *_: (0,) * a.ndim)
    wsg, wsu, wsd = wsg.astype(BF16), wsu.astype(BF16), wsd.astype(BF16)
    gf = gf.reshape(1, d)
    return pl.pallas_call(
        functools.partial(_combine_kernel, tt=tt, nslot=nslot, d=d),
        out_shape=jax.ShapeDtypeStruct((n, d), F32),
        grid_spec=pltpu.PrefetchScalarGridSpec(
            num_scalar_prefetch=4, grid=(n // tt,),
            in_specs=[pl.BlockSpec(memory_space=pl.ANY),
                      pl.BlockSpec((tt, TOP_K), lambda i, *_: (i, 0)),
                      row, row,
                      pl.BlockSpec((1, 1, ada3.shape[2]), lambda i, *_: (i // per_b, 0, 0)),
                      full(wsg), full(wsu), full(wsd), full(gf)],
            out_specs=row,
            scratch_shapes=[pltpu.VMEM((2, nslot, d), F32), pltpu.SemaphoreType.DMA((2,))]),
        compiler_params=_cparams(("arbitrary",)),
        name="combine",
    )(*tabs, ys, loc, h1, u2, ada3, wsg, wsu, wsd, gf)


def _moe(h1, u2, ada3, w_router, router_bias, wg, wu, wd, wsg, wsu, wsd, gf):
    bsz, seq, d = h1.shape
    n = bsz * seq
    assert seq % min(MOE_TILE, n) == 0
    h1f = h1.reshape(n, d)
    u2f = u2.reshape(n, d)
    gate_t, loc_t, c8, loff, run0, tot = _route(u2f, w_router, router_bias)
    nt = c8.shape[0]
    as_tab = lambda a: a[:, :, 0].astype(I32)
    tot8 = tot[:, 0].astype(I32)
    padded = (tot8 + EXPERT_ROWS - 1) // EXPERT_ROWS * EXPERT_ROWS
    pend = jnp.cumsum(padded)
    pstart = (pend - padded).astype(I32)
    nblk = (n * TOP_K + nt * N_EXPERTS * (SUBLANES - 1) + N_EXPERTS * (EXPERT_ROWS - 1) + EXPERT_ROWS - 1) // EXPERT_ROWS
    blk_row0 = jnp.arange(nblk, dtype=I32) * EXPERT_ROWS
    blk_expert = jnp.minimum(jnp.sum(pend[None, :] <= blk_row0[:, None], axis=1), N_EXPERTS - 1).astype(I32)
    n_used = (pend[-1:] // EXPERT_ROWS).astype(I32)
    n8 = as_tab(c8) // SUBLANES
    tabs = (n8.reshape(-1), as_tab(loff).reshape(-1), (pstart[None, :] + as_tab(run0)).reshape(-1),
            jnp.sum(n8, axis=1))
    ztabs = ((padded - tot8) // SUBLANES, pstart + tot8)
    xs = _dispatch(tabs, ztabs, u2f, loc_t, gate_t, nblk * EXPERT_ROWS)
    ys = _experts(blk_expert, n_used, xs, wg, wu, wd)
    out = _combine(tabs, ys, loc_t.T, h1f, u2f, ada3, wsg, wsu, wsd, gf, seq)
    return out.reshape(bsz, seq, d)


def kernel(x, c, w_ada, b_ada, norm1_g, w_in, ssm_lambda_re, ssm_lambda_im, ssm_log_dt, ssm_b_re, ssm_b_im,
           ssm_c_re, ssm_c_im, ssm_d, ssm_w_glu, ssm_b_glu, w_proj_ssm, w_proj_attn, w_out, norm2_g, w_router,
           router_bias, w_exp_gate, w_exp_up, w_exp_down, w_sh_gate, w_sh_up, w_sh_down, norm_f_g):
    depth = w_ada.shape[0]
    assert depth == 1, "the final norm is fused into the last (only) layer's combine kernel"
    bsz, seq, d = x.shape
    layer = 0
    ada3 = _ada(c, w_ada[layer], b_ada[layer]).reshape(bsz, 1, 6 * d)
    us, q, qi, k, ki, vt, wit = _inproj(x, ada3, norm1_g[layer], w_in[layer])
    a_re, a_im, bb_re, bb_im = _s5disc(ssm_lambda_re[layer], ssm_lambda_im[layer], ssm_log_dt[layer],
                                       ssm_b_re[layer], ssm_b_im[layer])
    ys_t = _s5(us.transpose(1, 0, 2), a_re, a_im, bb_re, bb_im, ssm_c_re[layer], ssm_c_im[layer],
               ssm_d[layer], ssm_w_glu[layer], ssm_b_glu[layer])
    ya = _dsa(q, qi, wit, k, ki, vt).transpose(0, 2, 1)
    h1, u2 = _mix(x, ys_t.transpose(1, 0, 2), ya, ada3, norm1_g[layer], w_in[layer], w_proj_ssm[layer],
                  w_proj_attn[layer], w_out[layer], norm2_g[layer])
    return _moe(h1, u2, ada3, w_router[layer], router_bias[layer], w_exp_gate[layer], w_exp_up[layer],
                w_exp_down[layer], w_sh_gate[layer], w_sh_up[layer], w_sh_down[layer], norm_f_g)
```

```python
import functools
import math

import jax
import jax.numpy as jnp
import numpy as np
from jax import lax
from jax.experimental import pallas as pl
from jax.experimental.pallas import tpu as pltpu

F32 = jnp.float32
BF16 = jnp.bfloat16
I32 = jnp.int32

SSM_GROUP = 16
SSM_STATE = 64
N_HEADS = 8
HEAD_DIM = 64
IDX_HEADS = 8
IDX_DIM = 64
TOPK_MAX = 256
N_EXPERTS = 64
TOP_K = 8
N_GROUPS = 8
TOPK_GROUPS = 4
ROUTED_SCALE = 2.5
EPS = 1e-6

V7X_VMEM_LIMIT_BYTES = 56 * 1024 * 1024
LANES = 128
SUBLANES = 8

INPROJ_ROWS = 256
S5_STEPS = 64
S5_LANE_CHUNK = 128
DSA_Q_COLS = 256
DSA_K_ROWS = 512
DSA_COUNT_ROWS = 64
BITSLICE_ROWS = 256
POS_SPLIT = 64
MIX_ROWS = 256
MOE_TILE = 256
SLOT_CHUNK = 512
EXPERT_ROWS = 512

NEG_BIG = -1e30
INT_MIN = -(2 ** 31)


def _cparams(sem):
    return pltpu.CompilerParams(dimension_semantics=sem, vmem_limit_bytes=V7X_VMEM_LIMIT_BYTES)


def _bf(x):
    return x.astype(BF16)


def _dot(a, b):
    return jnp.dot(a, b, preferred_element_type=F32)


def _dot_nt(a, b):
    return lax.dot_general(a, b, (((1,), (1,)), ((), ())), preferred_element_type=F32)


def _split(x):
    hi = _bf(x)
    lo = _bf(x - hi.astype(F32))
    return hi, lo


def _dot3(a, b):
    ah, al = _split(a)
    bh, bl = _split(b)
    return _dot(ah, bh) + (_dot(ah, bl) + _dot(al, bh))


def _rms(x, g):
    return x * lax.rsqrt(jnp.mean(x * x, axis=-1, keepdims=True) + EPS) * g


def _ada_kernel(c_ref, w_ref, b_ref, o_ref):
    c = c_ref[...]
    o_ref[...] = _dot3(c * jax.nn.sigmoid(c), w_ref[...]) + b_ref[...]


def _ada(c, w, b):
    bsz, d = c.shape
    n = w.shape[1]
    tn = 1024
    return pl.pallas_call(
        _ada_kernel,
        out_shape=jax.ShapeDtypeStruct((bsz, n), F32),
        grid=(n // tn,),
        in_specs=[pl.BlockSpec((bsz, d), lambda j: (0, 0)),
                  pl.BlockSpec((d, tn), lambda j: (0, j)),
                  pl.BlockSpec((1, tn), lambda j: (0, j))],
        out_specs=pl.BlockSpec((bsz, tn), lambda j: (0, j)),
        compiler_params=_cparams(("arbitrary",)),
        name="ada",
    )(c, w, b.reshape(1, n))


def _inproj_kernel(x_ref, ada_ref, g1_ref, w_ref, wt_ref,
                   us_ref, q_ref, qi_ref, k_ref, ki_ref, vt_ref, wit_ref, *, d, ssm_w, attn_w, idx_w):
    x = x_ref[0]
    shift = ada_ref[0, :, 0:d]
    scale = ada_ref[0, :, d:2 * d]
    u = _bf(_rms(x, g1_ref[...]) * (1.0 + scale) + shift)
    r = _dot(u, w_ref[...])
    o = 0
    us_ref[0] = r[:, o:o + ssm_w]
    o += ssm_w
    q_ref[0] = _bf(r[:, o:o + attn_w])
    o += attn_w
    qi_ref[0] = _bf(r[:, o:o + idx_w])
    o += idx_w
    k_ref[0] = _bf(r[:, o:o + HEAD_DIM])
    o += LANES
    ki_ref[0] = _bf(r[:, o:o + IDX_DIM])
    rt = _dot_nt(wt_ref[...], u)
    vt_ref[0] = _bf(rt[0:HEAD_DIM])
    wit_ref[0] = rt[HEAD_DIM:HEAD_DIM + IDX_HEADS]


def _split_w_in(w_in, d):
    ssm_w = 512
    sizes = (ssm_w, N_HEADS * HEAD_DIM, HEAD_DIM, HEAD_DIM, IDX_HEADS * IDX_DIM, IDX_DIM, IDX_HEADS, d, d)
    offs = [0]
    for s in sizes:
        offs.append(offs[-1] + s)
    return [w_in[:, offs[i]:offs[i + 1]] for i in range(9)]


def _inproj(x, ada3, g1, w_in):
    bsz, seq, d = x.shape
    ssm_w = 512
    attn_w = N_HEADS * HEAD_DIM
    idx_w = IDX_HEADS * IDX_DIM
    w_ssm, w_q, w_k, w_v, w_qi, w_ki, w_wi, _, _ = _split_w_in(w_in, d)
    zpad = lambda n: jnp.zeros((d, n), F32)
    wbig = jnp.concatenate([
        w_ssm, w_q * (HEAD_DIM ** -0.5), w_qi * (IDX_DIM ** -0.5),
        w_k, zpad(LANES - HEAD_DIM), w_ki, zpad(LANES - IDX_DIM)], axis=1).astype(BF16)
    wt = jnp.concatenate([w_v, w_wi, zpad(LANES - HEAD_DIM - IDX_HEADS)], axis=1).T.astype(BF16)
    nw = wbig.shape[1]
    tl = INPROJ_ROWS
    kern = functools.partial(_inproj_kernel, d=d, ssm_w=ssm_w, attn_w=attn_w, idx_w=idx_w)
    row = lambda w: pl.BlockSpec((1, tl, w), lambda b, l: (b, l, 0))
    colt = lambda h: pl.BlockSpec((1, h, tl), lambda b, l: (b, 0, l))
    return pl.pallas_call(
        kern,
        out_shape=(jax.ShapeDtypeStruct((bsz, seq, ssm_w), F32),
                   jax.ShapeDtypeStruct((bsz, seq, attn_w), BF16),
                   jax.ShapeDtypeStruct((bsz, seq, idx_w), BF16),
                   jax.ShapeDtypeStruct((bsz, seq, HEAD_DIM), BF16),
                   jax.ShapeDtypeStruct((bsz, seq, IDX_DIM), BF16),
                   jax.ShapeDtypeStruct((bsz, HEAD_DIM, seq), BF16),
                   jax.ShapeDtypeStruct((bsz, IDX_HEADS, seq), F32)),
        grid=(bsz, seq // tl),
        in_specs=[row(d),
                  pl.BlockSpec((1, 1, ada3.shape[2]), lambda b, l: (b, 0, 0)),
                  pl.BlockSpec((1, d), lambda b, l: (0, 0)),
                  pl.BlockSpec((d, nw), lambda b, l: (0, 0)),
                  pl.BlockSpec((LANES, d), lambda b, l: (0, 0))],
        out_specs=(row(ssm_w), row(attn_w), row(idx_w), row(HEAD_DIM), row(IDX_DIM),
                   colt(HEAD_DIM), colt(IDX_HEADS)),
        compiler_params=_cparams(("arbitrary", "arbitrary")),
        name="inproj",
    )(x, ada3, g1.reshape(1, d), wbig, wt)


def _s5disc_kernel(lr_ref, li_ref, ldt_ref, br_ref, bi_ref, are_ref, aim_ref, bbr_ref, bbi_ref):
    lr = lr_ref[...]
    li = li_ref[...]
    dt = jnp.exp(ldt_ref[...])
    mag = jnp.exp(lr * dt)
    a_re = mag * jnp.cos(li * dt)
    a_im = mag * jnp.sin(li * dt)
    den = lr * lr + li * li
    n_re = a_re - 1.0
    f_re = (n_re * lr + a_im * li) / den
    f_im = (a_im * lr - n_re * li) / den
    br = br_ref[...]
    bi = bi_ref[...]
    are_ref[...] = a_re
    aim_ref[...] = a_im
    bbr_ref[...] = f_re * br - f_im * bi
    bbi_ref[...] = f_re * bi + f_im * br


def _s5disc(lam_re, lam_im, log_dt, b_re, b_im):
    g, p = lam_re.shape
    h = b_re.shape[2]
    rep = lambda a: jnp.repeat(a, h, axis=1)
    ldt = jnp.broadcast_to(log_dt[:, None], (g, p * h))
    sds = jax.ShapeDtypeStruct((g, p * h), F32)
    a_re, a_im, bb_re, bb_im = pl.pallas_call(
        _s5disc_kernel, out_shape=(sds, sds, sds, sds), name="s5disc",
    )(rep(lam_re), rep(lam_im), ldt, b_re.reshape(g, p * h), b_im.reshape(g, p * h))
    return a_re[:, ::h], a_im[:, ::h], bb_re.reshape(g, p, h), bb_im.reshape(g, p, h)


def _s5_kernel(u_ref, wb_ref, ar_ref, ai_ref, cc_ref, dsk_ref, wg_ref, bg_ref, o_ref, buf, hst, *, tl, width):
    nch = width // S5_LANE_CHUNK
    sw = S5_LANE_CHUNK // SSM_GROUP * SSM_STATE
    rows = tl * SUBLANES

    @pl.when(pl.program_id(0) == 0)
    def _():
        hst[...] = jnp.zeros_like(hst)

    u = u_ref[...].reshape(rows, width)
    ub = _bf(u)
    for j in range(nch):
        buf[:, j * 2 * sw:(j + 1) * 2 * sw] = _dot(ub[:, j * S5_LANE_CHUNK:(j + 1) * S5_LANE_CHUNK], wb_ref[j])

    for j in range(nch):
        re_cols = slice(j * 2 * sw, j * 2 * sw + sw)
        im_cols = slice(j * 2 * sw + sw, (j + 1) * 2 * sw)
        a_re = jnp.broadcast_to(ar_ref[:, j * sw:(j + 1) * sw], (SUBLANES, sw))
        a_im = jnp.broadcast_to(ai_ref[:, j * sw:(j + 1) * sw], (SUBLANES, sw))

        def step(t, carry, re_cols=re_cols, im_cols=im_cols, a_re=a_re, a_im=a_im):
            h_re, h_im = carry
            r0 = pl.multiple_of(t * SUBLANES, SUBLANES)
            n_re = (a_re * h_re - a_im * h_im) + buf[pl.ds(r0, SUBLANES), re_cols]
            n_im = (a_re * h_im + a_im * h_re) + buf[pl.ds(r0, SUBLANES), im_cols]
            buf[pl.ds(r0, SUBLANES), re_cols] = n_re
            buf[pl.ds(r0, SUBLANES), im_cols] = n_im
            return n_re, n_im

        h_re, h_im = lax.fori_loop(0, tl, step, (hst[:, re_cols], hst[:, im_cols]), unroll=8)
        hst[:, re_cols] = h_re
        hst[:, im_cols] = h_im

    ys = [_dot(_bf(buf[:, j * 2 * sw:(j + 1) * 2 * sw]), cc_ref[j]) for j in range(nch)]
    y = jnp.concatenate(ys, axis=1) + dsk_ref[...] * u
    y = jax.nn.gelu(y)
    y = y * jax.nn.sigmoid(_dot(_bf(y), wg_ref[...]) + bg_ref[...])
    o_ref[...] = _bf(y).reshape(tl, SUBLANES, width)


def _s5(u_t, a_re, a_im, bb_re, bb_im, c_re, c_im, d_skip, w_glu, b_glu):
    seq, bsz, width = u_t.shape
    assert bsz == SUBLANES
    nch = width // S5_LANE_CHUNK
    gpc = S5_LANE_CHUNK // SSM_GROUP
    sw = gpc * SSM_STATE
    eye = jnp.eye(gpc, dtype=F32)

    def bmat(bb):
        t = bb.reshape(nch, gpc, SSM_STATE, SSM_GROUP).transpose(0, 1, 3, 2)
        return jnp.einsum('jghp,gk->jghkp', t, eye).reshape(nch, S5_LANE_CHUNK, sw)

    def cmat(cc):
        t = cc.reshape(nch, gpc, SSM_GROUP, SSM_STATE).transpose(0, 1, 3, 2)
        return jnp.einsum('jgph,gk->jgpkh', t, eye).reshape(nch, sw, S5_LANE_CHUNK)

    wb = jnp.concatenate([bmat(bb_re), bmat(bb_im)], axis=2)
    cc = jnp.concatenate([cmat(c_re), -cmat(c_im)], axis=1)
    tl = S5_STEPS
    full = lambda a: pl.BlockSpec(a.shape, lambda i: (0,) * a.ndim)
    args = (u_t, wb.astype(BF16), a_re.reshape(1, -1), a_im.reshape(1, -1), cc.astype(BF16),
            d_skip.reshape(1, width), w_glu.astype(BF16), b_glu.reshape(1, width))
    return pl.pallas_call(
        functools.partial(_s5_kernel, tl=tl, width=width),
        out_shape=jax.ShapeDtypeStruct((seq, bsz, width), BF16),
        grid=(seq // tl,),
        in_specs=[pl.BlockSpec((tl, bsz, width), lambda i: (i, 0, 0))] + [full(a) for a in args[1:]],
        out_specs=pl.BlockSpec((tl, bsz, width), lambda i: (i, 0, 0)),
        scratch_shapes=[pltpu.VMEM((tl * SUBLANES, nch * 2 * sw), F32),
                        pltpu.VMEM((SUBLANES, nch * 2 * sw), F32)],
        compiler_params=_cparams(("arbitrary",)),
        name="s5",
    )(*args)


def _bit_transpose32(words):
    x = list(words)
    j, m = 16, 0x0000FFFF
    while j:
        k = 0
        while k < 32:
            t = (x[k] ^ lax.shift_right_logical(x[k + j], jnp.int32(j))) & jnp.int32(m - (1 << 32) if m >= 1 << 31 else m)
            x[k] = x[k] ^ t
            x[k + j] = x[k + j] ^ lax.shift_left(t, jnp.int32(j))
            k = (k + j + 1) & ~j
        j >>= 1
        m = (m ^ (m << j)) & 0xFFFFFFFF
    return x


def _dsa_kernel(qt_ref, qit_ref, wit_ref, ka_ref, ki_ref, vt_ref, o_ref, key_s, mb_s, acc_s, pl_s, p_s, *, tq, tk, topk, seq):
    i = pl.program_id(1)
    q0 = i * tq
    nkt = (q0 + tq + tk - 1) // tk
    ch = DSA_COUNT_ROWS
    krow = lax.broadcasted_iota(I32, (tk, tq), 0)
    qcol = q0 + lax.broadcasted_iota(I32, (tk, tq), 1)
    crow = lax.broadcasted_iota(I32, (ch, tq), 0)

    wb = wit_ref[0] * (IDX_HEADS ** -0.5)

    def score_tile(j, _):
        r0 = pl.multiple_of(j * tk, tk)
        kit = ki_ref[0, pl.ds(r0, tk), :]
        acc = jnp.zeros((tk, tq), F32)
        for h in range(IDX_HEADS):
            s = _dot(kit, qit_ref[0, h * IDX_DIM:(h + 1) * IDX_DIM, :])
            acc = acc + wb[h:h + 1, :] * jnp.maximum(s, 0.0)
        bits = lax.bitcast_convert_type(acc, I32)
        key = jnp.where(bits < 0, bits ^ jnp.int32(0x7FFFFFFF), bits)
        key = jnp.where(acc == 0.0, 0, key)
        key = jnp.where(krow + r0 <= qcol, key, INT_MIN)
        key_s[pl.ds(r0, tk), :] = key
        ukey = key ^ INT_MIN
        for c in range(tk // BITSLICE_ROWS):
            words = [ukey[c * BITSLICE_ROWS + v * SUBLANES:c * BITSLICE_ROWS + (v + 1) * SUBLANES, :]
                     for v in range(32)]
            planes = _bit_transpose32(words)
            g0 = pl.multiple_of((j * (tk // BITSLICE_ROWS) + c) * SUBLANES, SUBLANES)
            for it in range(32):
                pl_s[it, pl.ds(g0, SUBLANES), :] = planes[it]
        return 0

    @pl.when((pl.program_id(0) == 0) & (i == 0))
    def _():
        pl_s[...] = jnp.zeros(pl_s.shape, I32)

    lax.fori_loop(0, nkt, score_tile, 0)

    def count(pred):
        def tile(j, cnt):
            for c in range(tk // ch):
                rr = pl.multiple_of(j * tk + c * ch, ch)
                cnt = cnt + jnp.where(pred(key_s[pl.ds(rr, ch), :], rr), 1, 0)
            return cnt
        cnt = lax.fori_loop(0, nkt, tile, jnp.zeros((ch, tq), I32))
        return jnp.sum(cnt.astype(F32), axis=0, keepdims=True)

    ngrp = seq // 32

    def lane_count(words):
        pc = lax.population_count(words).reshape(ngrp // SUBLANES, SUBLANES, tq)
        return jnp.sum(jnp.sum(pc, axis=0).astype(F32), axis=0, keepdims=True)

    def bit_step(it, carry):
        alive, above, ans_u = carry
        ones = alive & pl_s[it]
        cnt1 = lane_count(ones)
        take = above + cnt1 >= float(topk)
        alive = jnp.where(take, ones, alive ^ ones)
        above = jnp.where(take, above, above + cnt1)
        ans_u = jnp.where(take, ans_u | lax.shift_left(jnp.int32(1), 31 - it), ans_u)
        return alive, above, ans_u

    grow = lax.broadcasted_iota(I32, (ngrp, tq), 0)
    alive0 = jnp.where(grow < nkt * (tk // 32), -1, 0)
    alive, above, ans_u = lax.fori_loop(
        0, 32, bit_step, (alive0, jnp.zeros((1, tq), F32), jnp.zeros((1, tq), I32)))
    thr = jnp.maximum(ans_u ^ INT_MIN, INT_MIN + 1)
    cnt_ge = above + lane_count(alive)
    tied = jnp.where(ans_u != 0, cnt_ge, 0.0) > float(topk)
    has_ties = jnp.max(jnp.where(tied, 1.0, 0.0)) > 0.0

    def tie_cut():
        need = float(topk) - count(lambda kb, rr: kb > thr)
        nbits = max(1, (seq - 1).bit_length())

        def idx_step(b, x):
            cand = x | lax.shift_left(jnp.int32(1), nbits - 1 - b)
            below = count(lambda kb, rr: jnp.where(kb == thr, crow + rr, seq) < cand)
            return jnp.where(below < need, cand, x)

        x = lax.fori_loop(0, nbits, idx_step, jnp.zeros((1, tq), I32))
        return jnp.where(tied, x, seq)

    cut = lax.cond(has_ties, tie_cut, lambda: jnp.full((1, tq), seq, I32))

    def bias_tile(j, _):
        for c in range(tk // ch):
            rr = pl.multiple_of(j * tk + c * ch, ch)
            kb = key_s[pl.ds(rr, ch), :]
            tie_bias = jnp.where(crow + rr <= cut, 0.0, NEG_BIG)
            mb_s[pl.ds(rr, ch), :] = jnp.where(kb > thr, 0.0, jnp.where(kb == thr, tie_bias, NEG_BIG))
        return 0

    lax.fori_loop(0, nkt, bias_tile, 0)

    def logits(j, h):
        r0 = pl.multiple_of(j * tk, tk)
        s = _dot(ka_ref[0, pl.ds(r0, tk), :], qt_ref[0, h * LANES:(h + 1) * LANES, :]) + mb_s[pl.ds(r0, tk), :]
        return s.reshape(tk // SUBLANES, SUBLANES, tq)

    def max_tile(j, ms):
        return tuple(jnp.maximum(ms[h], jnp.max(logits(j, h), axis=0)) for h in range(N_HEADS))

    ms = lax.fori_loop(0, nkt, max_tile, (jnp.full((SUBLANES, tq), NEG_BIG, F32),) * N_HEADS)
    m = [jnp.max(mh, axis=0, keepdims=True) for mh in ms]
    acc_s[...] = jnp.zeros(acc_s.shape, F32)

    def sum_tile(j, ls):
        r0 = pl.multiple_of(j * tk, tk)
        out = []
        for h in range(N_HEADS):
            p = jnp.exp(logits(j, h) - m[h])
            out.append(ls[h] + jnp.sum(p, axis=0))
            p_s[h] = _bf(p.reshape(tk, tq))
        for h in range(N_HEADS):
            rows = slice(h * HEAD_DIM, (h + 1) * HEAD_DIM)
            acc_s[rows, :] = acc_s[rows, :] + _dot(vt_ref[0, :, pl.ds(r0, tk)], p_s[h])
        return tuple(out)

    ls = lax.fori_loop(0, nkt, sum_tile, (jnp.zeros((SUBLANES, tq), F32),) * N_HEADS)
    for h in range(N_HEADS):
        rows = slice(h * HEAD_DIM, (h + 1) * HEAD_DIM)
        o_ref[0, rows, :] = _bf(acc_s[rows, :] / jnp.sum(ls[h], axis=0, keepdims=True))


def _dsa(q, qi, wit, k, ki, vt):
    bsz, seq, aw = q.shape
    tq = min(DSA_Q_COLS, seq)
    tk = min(DSA_K_ROWS, seq)
    topk = min(TOPK_MAX, seq // 4)
    assert (seq - 1) // POS_SPLIT < 256 and POS_SPLIT <= 256, "key positions must split into two bf16-exact parts"
    slopes = [2.0 ** (-8.0 * (h + 1) / N_HEADS) for h in range(N_HEADS)]
    assert all(float(np.float32(sl).astype(BF16)) == sl for sl in slopes), "ALiBi slopes must be bf16-exact"
    pos = jnp.arange(seq, dtype=I32)
    posc = jnp.stack([(pos // POS_SPLIT) * POS_SPLIT, pos % POS_SPLIT], axis=1).astype(BF16)
    ka = jnp.concatenate([k, jnp.broadcast_to(posc[None], (bsz, seq, 2)),
                          jnp.zeros((bsz, seq, LANES - HEAD_DIM - 2), BF16)], axis=2)
    qh = q.reshape(bsz, seq, N_HEADS, HEAD_DIM).transpose(0, 2, 3, 1)
    srow = jnp.asarray(slopes, BF16)[None, :, None, None]
    extra = jnp.concatenate([jnp.broadcast_to(srow, (bsz, N_HEADS, 2, seq)),
                             jnp.zeros((bsz, N_HEADS, LANES - HEAD_DIM - 2, seq), BF16)], axis=2)
    qt = jnp.concatenate([qh, extra], axis=2).reshape(bsz, N_HEADS * LANES, seq)
    qit = qi.transpose(0, 2, 1)
    kern = functools.partial(_dsa_kernel, tq=tq, tk=tk, topk=topk, seq=seq)
    cols = lambda r: pl.BlockSpec((1, r, tq), lambda b, i: (b, 0, i))
    return pl.pallas_call(
        kern,
        out_shape=jax.ShapeDtypeStruct((bsz, aw, seq), BF16),
        grid=(bsz, seq // tq),
        in_specs=[cols(N_HEADS * LANES), cols(qit.shape[1]), cols(IDX_HEADS),
                  pl.BlockSpec((1, seq, LANES), lambda b, i: (b, 0, 0)),
                  pl.BlockSpec((1, seq, IDX_DIM), lambda b, i: (b, 0, 0)),
                  pl.BlockSpec((1, HEAD_DIM, seq), lambda b, i: (b, 0, 0))],
        out_specs=cols(aw),
        scratch_shapes=[pltpu.VMEM((seq, tq), I32), pltpu.VMEM((seq, tq), F32), pltpu.VMEM((aw, tq), F32),
                        pltpu.VMEM((32, seq // 32, tq), I32), pltpu.VMEM((N_HEADS, tk, tq), BF16)],
        compiler_params=_cparams(("arbitrary", "arbitrary")),
        name="dsa",
    )(qt, qit, wit, ka, ki, vt)


def _mix_kernel(x_ref, ys_ref, ya_ref, ada_ref, g1_ref, wgt_ref, wps_ref, wpa_ref, wo_ref, g2_ref,
                h_ref, u2_ref, *, d):
    gate1 = ada_ref[0, :, 2 * d:3 * d]
    shift2 = ada_ref[0, :, 3 * d:4 * d]
    scale2 = ada_ref[0, :, 4 * d:5 * d]
    x = x_ref[0]
    u = _bf(_rms(x, g1_ref[...]) * (1.0 + ada_ref[0, :, d:2 * d]) + ada_ref[0, :, 0:d])
    g = _dot(u, wgt_ref[...])
    mixed = (jax.nn.sigmoid(g[:, 0:d]) * _dot(ys_ref[0], wps_ref[...])
             + jax.nn.sigmoid(g[:, d:2 * d]) * _dot(ya_ref[0], wpa_ref[...]))
    h = x + gate1 * _dot(_bf(mixed), wo_ref[...])
    h_ref[0] = h
    u2_ref[0] = _rms(h, g2_ref[...]) * (1.0 + scale2) + shift2


def _mix(x, ys, ya, ada3, g1, w_in, wps, wpa, wo, g2):
    bsz, seq, d = x.shape
    tm = MIX_ROWS
    row = lambda w: pl.BlockSpec((1, tm, w), lambda b, l: (b, l, 0))
    full = lambda a: pl.BlockSpec(a.shape, lambda b, l: (0,) * a.ndim)
    wps, wpa, wo = wps.astype(BF16), wpa.astype(BF16), wo.astype(BF16)
    wgt = jnp.concatenate(_split_w_in(w_in, d)[7:9], axis=1).astype(BF16)
    g1 = g1.reshape(1, d)
    g2 = g2.reshape(1, d)
    return pl.pallas_call(
        functools.partial(_mix_kernel, d=d),
        out_shape=(jax.ShapeDtypeStruct((bsz, seq, d), F32), jax.ShapeDtypeStruct((bsz, seq, d), F32)),
        grid=(bsz, seq // tm),
        in_specs=[row(d), row(ys.shape[2]), row(ya.shape[2]),
                  pl.BlockSpec((1, 1, ada3.shape[2]), lambda b, l: (b, 0, 0)),
                  full(g1), full(wgt), full(wps), full(wpa), full(wo), full(g2)],
        out_specs=(row(d), row(d)),
        compiler_params=_cparams(("arbitrary", "arbitrary")),
        name="mix",
    )(x, ys, ya, ada3, g1, wgt, wps, wpa, wo, g2)


def _first_max(cur, idx, axis, big):
    m = jnp.max(cur, axis=axis, keepdims=True)
    first = jnp.min(jnp.where(cur == m, idx, big), axis=axis, keepdims=True)
    return m, idx == first


def _route_kernel(u_ref, wrh_ref, wrl_ref, rb_ref, tri_ref, ltri_ref,
                  gt_ref, loc_ref, c8_ref, loff_ref, run0_ref, tot_ref, run_s, *, t):
    @pl.when(pl.program_id(0) == 0)
    def _():
        run_s[...] = jnp.zeros_like(run_s)

    uh, ul = _split(u_ref[...])
    logits = _dot_nt(wrh_ref[...], uh) + (_dot_nt(wrl_ref[...], uh) + _dot_nt(wrh_ref[...], ul))
    scores = jax.nn.sigmoid(logits)
    biased = scores + rb_ref[...]
    per_group = N_EXPERTS // N_GROUPS
    b3 = biased.reshape(N_GROUPS, per_group, t)
    i3 = lax.broadcasted_iota(I32, b3.shape, 1)
    m1, hit1 = _first_max(b3, i3, 1, per_group)
    m2 = jnp.max(jnp.where(hit1, -jnp.inf, b3), axis=1, keepdims=True)
    gs = (m1 + m2).reshape(N_GROUPS, t)
    gi = lax.broadcasted_iota(I32, gs.shape, 0)
    gsel = jnp.zeros(gs.shape, F32)
    for _ in range(TOPK_GROUPS):
        _, hit = _first_max(gs, gi, 0, N_GROUPS)
        gsel = jnp.where(hit, 1.0, gsel)
        gs = jnp.where(hit, -jnp.inf, gs)
    cur = jnp.where(gsel.reshape(N_GROUPS, 1, t) > 0.0, b3, -jnp.inf).reshape(N_EXPERTS, t)
    ei = lax.broadcasted_iota(I32, cur.shape, 0)
    hits = []
    gates = []
    for _ in range(TOP_K):
        _, hit = _first_max(cur, ei, 0, N_EXPERTS)
        hits.append(hit)
        gates.append(jnp.sum(jnp.where(hit, scores, 0.0), axis=0, keepdims=True))
        cur = jnp.where(hit, -jnp.inf, cur)
    gate = jnp.concatenate(gates, axis=0)
    gt_ref[...] = gate / jnp.sum(gate, axis=0, keepdims=True) * ROUTED_SCALE
    onehot = jnp.zeros(cur.shape, F32)
    for hit in hits:
        onehot = jnp.where(hit, 1.0, onehot)
    cnt = jnp.sum(onehot, axis=1, keepdims=True)
    c8 = jnp.floor((cnt + (SUBLANES - 1)) * (1.0 / SUBLANES)) * SUBLANES
    c8l = jnp.broadcast_to(c8, (N_EXPERTS, LANES))
    loff = _dot(ltri_ref[...], _bf(c8l))
    slot = _dot(_bf(onehot), tri_ref[...]) + loff[:, 0:1]
    loc_ref[...] = jnp.concatenate(
        [jnp.sum(jnp.where(hit, slot, 0.0), axis=0, keepdims=True) for hit in hits], axis=0).astype(I32)
    c8_ref[0] = c8l
    loff_ref[0] = loff
    run0_ref[0] = run_s[...]
    run_s[...] = run_s[...] + c8
    tot_ref[...] = run_s[...]


def _route(u2, w_router, router_bias):
    n, d = u2.shape
    t = min(MOE_TILE, n)
    nt = n // t
    wt = w_router.T
    wrh = wt.astype(BF16)
    wrl = (wt - wrh.astype(F32)).astype(BF16)
    tri = (jnp.arange(t)[:, None] < jnp.arange(t)[None, :]).astype(BF16)
    ex = jnp.arange(N_EXPERTS)
    ltri = (ex[None, :] < ex[:, None]).astype(BF16)
    full = lambda a: pl.BlockSpec(a.shape, lambda i: (0,) * a.ndim)
    col = pl.BlockSpec((TOP_K, t), lambda i: (0, i))
    tab = pl.BlockSpec((1, N_EXPERTS, LANES), lambda i: (i, 0, 0))
    tab_sds = jax.ShapeDtypeStruct((nt, N_EXPERTS, LANES), F32)
    rb = router_bias.reshape(N_EXPERTS, 1)
    return pl.pallas_call(
        functools.partial(_route_kernel, t=t),
        out_shape=(jax.ShapeDtypeStruct((TOP_K, n), F32), jax.ShapeDtypeStruct((TOP_K, n), I32),
                   tab_sds, tab_sds, tab_sds, jax.ShapeDtypeStruct((N_EXPERTS, LANES), F32)),
        grid=(nt,),
        in_specs=[pl.BlockSpec((t, d), lambda i: (i, 0)), full(wrh), full(wrl), full(rb), full(tri), full(ltri)],
        out_specs=(col, col, tab, tab, tab, pl.BlockSpec((N_EXPERTS, LANES), lambda i: (0, 0))),
        scratch_shapes=[pltpu.VMEM((N_EXPERTS, LANES), F32)],
        compiler_params=_cparams(("arbitrary",)),
        name="route",
    )(u2, wrh, wrl, rb, tri, ltri)


RUN_BITS = tuple(1 << b for b in reversed(range((MOE_TILE // SUBLANES).bit_length())))


def _for_each_run_piece(n8_ref, src_ref, dst_ref, tile, bits, fn):
    def per_expert(e, _):
        idx = tile * N_EXPERTS + e
        n8 = n8_ref[idx]
        src = src_ref[idx]
        dst = dst_ref[idx]
        for b, p in enumerate(bits):
            off = (n8 & ~(2 * p - 1)) * SUBLANES

            @pl.when((n8 & p) != 0)
            def _(b=b, p=p, off=off):
                fn(pl.multiple_of(src + off, SUBLANES), pl.multiple_of(dst + off, SUBLANES), p * SUBLANES, b % 2)
        return 0

    lax.fori_loop(0, N_EXPERTS, per_expert, 0)


def _wait_rows(n8, make_copy, max_rows):
    for p in tuple(1 << b for b in reversed(range((max_rows // SUBLANES).bit_length()))):
        @pl.when((n8 & p) != 0)
        def _(p=p):
            make_copy(p * SUBLANES).wait()


def _dispatch_kernel(n8_ref, src_ref, dst_ref, tot_ref, zn8_ref, zdst_ref, u_ref, loc_ref, gate_ref, xs_hbm,
                     lbuf, zx, sems, *, tt, nslot, dh):
    i = pl.program_id(0)
    last = pl.num_programs(0) - 1
    par = i % 2

    def run_copy(slot):
        def piece(s0, d0, rows, prio):
            pltpu.make_async_copy(
                lbuf.at[slot, pl.ds(s0, rows)], xs_hbm.at[pl.ds(d0, rows)], sems.at[slot]).start(priority=prio)
        return piece

    def wait_tile(tile, slot):
        _wait_rows(tot_ref[tile], lambda rows: pltpu.make_async_copy(
            lbuf.at[slot, pl.ds(0, rows)], xs_hbm.at[pl.ds(0, rows)], sems.at[slot]), nslot)

    @pl.when(i == 0)
    def _():
        zx[...] = jnp.zeros(zx.shape, I32)

        def zero_piece(s0, d0, rows, prio):
            cx = pltpu.make_async_copy(zx.at[pl.ds(0, rows)], xs_hbm.at[pl.ds(d0, rows)], sems.at[2])
            cx.start()
            cx.wait()

        zbits = tuple(b for b in RUN_BITS if b * SUBLANES < EXPERT_ROWS)
        _for_each_run_piece(zn8_ref, zdst_ref, zdst_ref, 0, zbits, zero_piece)

    ub = _bf(u_ref[...])
    ones = jnp.ones((tt, LANES), BF16)
    loc = loc_ref[...]
    gate = gate_ref[...]
    for c in range(nslot // SLOT_CHUNK):
        rows = c * SLOT_CHUNK + lax.broadcasted_iota(I32, (SLOT_CHUNK, tt), 0)
        perm = jnp.zeros((SLOT_CHUNK, tt), F32)
        pgate = jnp.zeros((SLOT_CHUNK, tt), F32)
        for k in range(TOP_K):
            eq = rows == loc[k:k + 1, :]
            perm = jnp.where(eq, 1.0, perm)
            pgate = jnp.where(eq, gate[k:k + 1, :], pgate)
        cs = slice(c * SLOT_CHUNK, (c + 1) * SLOT_CHUNK)
        xp = lax.bitcast_convert_type(_dot(_bf(perm), ub), I32)
        lbuf[par, cs, 0:dh] = xp[:, 0:dh] | lax.shift_right_logical(xp[:, dh:2 * dh], 16)
        gh, gl = _split(pgate)
        lbuf[par, cs, dh:dh + LANES] = lax.bitcast_convert_type(_dot(gh, ones) + _dot(gl, ones), I32)

    _for_each_run_piece(n8_ref, src_ref, dst_ref, i, RUN_BITS, run_copy(par))

    @pl.when(i > 0)
    def _():
        wait_tile(i - 1, 1 - par)

    @pl.when(i == last)
    def _():
        wait_tile(i, par)


def _dispatch(tabs, ztabs, u2, loc_t, gate_t, n_rows):
    n, d = u2.shape
    tt = min(MOE_TILE, n)
    nslot = TOP_K * tt + N_EXPERTS * SUBLANES
    dh = d // 2
    assert nslot % SLOT_CHUNK == 0 and tt // SUBLANES == RUN_BITS[0]
    col = pl.BlockSpec((TOP_K, tt), lambda i, *_: (0, i))
    return pl.pallas_call(
        functools.partial(_dispatch_kernel, tt=tt, nslot=nslot, dh=dh),
        out_shape=jax.ShapeDtypeStruct((n_rows, dh + LANES), I32),
        grid_spec=pltpu.PrefetchScalarGridSpec(
            num_scalar_prefetch=6, grid=(n // tt,),
            in_specs=[pl.BlockSpec((tt, d), lambda i, *_: (i, 0)), col, col],
            out_specs=pl.BlockSpec(memory_space=pl.ANY),
            scratch_shapes=[pltpu.VMEM((2, nslot, dh + LANES), I32),
                            pltpu.VMEM((EXPERT_ROWS // 2, dh + LANES), I32),
                            pltpu.SemaphoreType.DMA((3,))]),
        compiler_params=_cparams(("arbitrary",)),
        name="dispatch",
    )(*tabs, *ztabs, u2, loc_t, gate_t)


def _experts_kernel(be_ref, nu_ref, xs_ref, wg_ref, wu_ref, wd_ref, ys_ref, *, d):
    del be_ref
    dh = d // 2

    @pl.when(pl.program_id(0) < nu_ref[0])
    def _():
        w = xs_ref[:, 0:dh]
        xa = _bf(lax.bitcast_convert_type(w & jnp.int32(-65536), F32))
        xb = _bf(lax.bitcast_convert_type(lax.shift_left(w, 16), F32))
        gate = lax.bitcast_convert_type(xs_ref[:, dh:dh + LANES], F32)
        hg = _dot(xa, wg_ref[0, 0:dh, :]) + _dot(xb, wg_ref[0, dh:d, :])
        hu = _dot(xa, wu_ref[0, 0:dh, :]) + _dot(xb, wu_ref[0, dh:d, :])
        y = _dot(_bf(jax.nn.silu(hg) * hu), wd_ref[0]) * jnp.tile(gate, (1, d // LANES))
        yb = lax.bitcast_convert_type(_bf(y).astype(F32), I32)
        ys_ref[...] = yb[:, 0:dh] | lax.shift_right_logical(yb[:, dh:d], 16)


def _experts(blk_expert, n_used, xs, wg, wu, wd):
    rows, xw = xs.shape
    d = wg.shape[1]
    de = wg.shape[2]
    nblk = rows // EXPERT_ROWS
    blk = lambda i, be, nu: jnp.minimum(i, nu[0] - 1)
    return pl.pallas_call(
        functools.partial(_experts_kernel, d=d),
        out_shape=jax.ShapeDtypeStruct((rows, d // 2), I32),
        grid_spec=pltpu.PrefetchScalarGridSpec(
            num_scalar_prefetch=2, grid=(nblk,),
            in_specs=[pl.BlockSpec((EXPERT_ROWS, xw), lambda i, be, nu: (blk(i, be, nu), 0)),
                      pl.BlockSpec((1, d, de), lambda i, be, nu: (be[blk(i, be, nu)], 0, 0)),
                      pl.BlockSpec((1, d, de), lambda i, be, nu: (be[blk(i, be, nu)], 0, 0)),
                      pl.BlockSpec((1, de, d), lambda i, be, nu: (be[blk(i, be, nu)], 0, 0))],
            out_specs=pl.BlockSpec((EXPERT_ROWS, d // 2), lambda i, be, nu: (blk(i, be, nu), 0))),
        compiler_params=_cparams(("arbitrary",)),
        name="experts",
    )(blk_expert, n_used, xs, wg.astype(BF16), wu.astype(BF16), wd.astype(BF16))


def _combine_kernel(n8_ref, src_ref, dst_ref, tot_ref, ys_hbm, loc_ref, h_ref, u2_ref, ada_ref, wsg_ref, wsu_ref, wsd_ref,
                    gf_ref, o_ref, ybuf, sems, *, tt, nslot, d):
    i = pl.program_id(0)
    last = pl.num_programs(0) - 1
    par = i % 2

    def run_copy(slot):
        def piece(s0, d0, rows, prio):
            pltpu.make_async_copy(
                ys_hbm.at[pl.ds(d0, rows)], ybuf.at[slot, pl.ds(s0, rows)], sems.at[slot]).start(priority=prio)
        return piece

    @pl.when(i == 0)
    def _():
        ybuf[...] = jnp.zeros(ybuf.shape, I32)
        _for_each_run_piece(n8_ref, src_ref, dst_ref, 0, RUN_BITS, run_copy(0))

    @pl.when(i < last)
    def _():
        _for_each_run_piece(n8_ref, src_ref, dst_ref, i + 1, RUN_BITS, run_copy(1 - par))

    x = _bf(u2_ref[...])
    shared = _dot(_bf(jax.nn.silu(_dot(x, wsg_ref[...])) * _dot(x, wsu_ref[...])), wsd_ref[...])
    _wait_rows(tot_ref[i], lambda rows: pltpu.make_async_copy(
        ys_hbm.at[pl.ds(0, rows)], ybuf.at[par, pl.ds(0, rows)], sems.at[par]), nslot)

    loc = loc_ref[...]
    dh = d // 2
    routed_a = jnp.zeros((tt, dh), F32)
    routed_b = jnp.zeros((tt, dh), F32)
    for c in range(nslot // SLOT_CHUNK):
        cols = c * SLOT_CHUNK + lax.broadcasted_iota(I32, (tt, SLOT_CHUNK), 1)
        pick = jnp.zeros((tt, SLOT_CHUNK), F32)
        for k in range(TOP_K):
            pick = jnp.where(cols == loc[:, k:k + 1], 1.0, pick)
        pick = _bf(pick)
        w = ybuf[par, c * SLOT_CHUNK:(c + 1) * SLOT_CHUNK, :]
        routed_a = routed_a + _dot(pick, _bf(lax.bitcast_convert_type(w & jnp.int32(-65536), F32)))
        routed_b = routed_b + _dot(pick, _bf(lax.bitcast_convert_type(lax.shift_left(w, 16), F32)))
    routed = jnp.concatenate([routed_a, routed_b], axis=1)
    gate2 = ada_ref[0, :, 5 * d:6 * d]
    h = h_ref[...] + gate2 * (routed + shared)
    o_ref[...] = _rms(h, gf_ref[...])


def _combine(tabs, ys, loc, h1, u2, ada3, wsg, wsu, wsd, gf, seq):
    n, d = h1.shape
    tt = min(MOE_TILE, n)
    nslot = TOP_K * tt + N_EXPERTS * SUBLANES
    per_b = seq // tt
    row = pl.BlockSpec((tt, d), lambda i, *_: (i, 0))
    full = lambda a: pl.BlockSpec(a.shape, lambda i, *_: (0,) * a.ndim)
    wsg, wsu, wsd = wsg.astype(BF16), wsu.astype(BF16), wsd.astype(BF16)
    gf = gf.reshape(1, d)
    return pl.pallas_call(
        functools.partial(_combine_kernel, tt=tt, nslot=nslot, d=d),
        out_shape=jax.ShapeDtypeStruct((n, d), F32),
        grid_spec=pltpu.PrefetchScalarGridSpec(
            num_scalar_prefetch=4, grid=(n // tt,),
            in_specs=[pl.BlockSpec(memory_space=pl.ANY),
                      pl.BlockSpec((tt, TOP_K), lambda i, *_: (i, 0)),
                      row, row,
                      pl.BlockSpec((1, 1, ada3.shape[2]), lambda i, *_: (i // per_b, 0, 0)),
                      full(wsg), full(wsu), full(wsd), full(gf)],
            out_specs=row,
            scratch_shapes=[pltpu.VMEM((2, nslot, d // 2), I32), pltpu.SemaphoreType.DMA((2,))]),
        compiler_params=_cparams(("arbitrary",)),
        name="combine",
    )(*tabs, ys, loc, h1, u2, ada3, wsg, wsu, wsd, gf)


def _moe(h1, u2, ada3, w_router, router_bias, wg, wu, wd, wsg, wsu, wsd, gf):
    bsz, seq, d = h1.shape
    n = bsz * seq
    assert seq % min(MOE_TILE, n) == 0
    h1f = h1.reshape(n, d)
    u2f = u2.reshape(n, d)
    gate_t, loc_t, c8, loff, run0, tot = _route(u2f, w_router, router_bias)
    nt = c8.shape[0]
    as_tab = lambda a: a[:, :, 0].astype(I32)
    tot8 = tot[:, 0].astype(I32)
    padded = (tot8 + EXPERT_ROWS - 1) // EXPERT_ROWS * EXPERT_ROWS
    pend = jnp.cumsum(padded)
    pstart = (pend - padded).astype(I32)
    nblk = (n * TOP_K + nt * N_EXPERTS * (SUBLANES - 1) + N_EXPERTS * (EXPERT_ROWS - 1) + EXPERT_ROWS - 1) // EXPERT_ROWS
    blk_row0 = jnp.arange(nblk, dtype=I32) * EXPERT_ROWS
    blk_expert = jnp.minimum(jnp.sum(pend[None, :] <= blk_row0[:, None], axis=1), N_EXPERTS - 1).astype(I32)
    n_used = (pend[-1:] // EXPERT_ROWS).astype(I32)
    n8 = as_tab(c8) // SUBLANES
    tabs = (n8.reshape(-1), as_tab(loff).reshape(-1), (pstart[None, :] + as_tab(run0)).reshape(-1),
            jnp.sum(n8, axis=1))
    ztabs = ((padded - tot8) // SUBLANES, pstart + tot8)
    xs = _dispatch(tabs, ztabs, u2f, loc_t, gate_t, nblk * EXPERT_ROWS)
    ys = _experts(blk_expert, n_used, xs, wg, wu, wd)
    out = _combine(tabs, ys, loc_t.T, h1f, u2f, ada3, wsg, wsu, wsd, gf, seq)
    return out.reshape(bsz, seq, d)


def kernel(x, c, w_ada, b_ada, norm1_g, w_in, ssm_lambda_re, ssm_lambda_im, ssm_log_dt, ssm_b_re, ssm_b_im,
           ssm_c_re, ssm_c_im, ssm_d, ssm_w_glu, ssm_b_glu, w_proj_ssm, w_proj_attn, w_out, norm2_g, w_router,
           router_bias, w_exp_gate, w_exp_up, w_exp_down, w_sh_gate, w_sh_up, w_sh_down, norm_f_g):
    depth = w_ada.shape[0]
    assert depth == 1, "the final norm is fused into the last (only) layer's combine kernel"
    bsz, seq, d = x.shape
    layer = 0
    ada3 = _ada(c, w_ada[layer], b_ada[layer]).reshape(bsz, 1, 6 * d)
    us, q, qi, k, ki, vt, wit = _inproj(x, ada3, norm1_g[layer], w_in[layer])
    a_re, a_im, bb_re, bb_im = _s5disc(ssm_lambda_re[layer], ssm_lambda_im[layer], ssm_log_dt[layer],
                                       ssm_b_re[layer], ssm_b_im[layer])
    ys_t = _s5(us.transpose(1, 0, 2), a_re, a_im, bb_re, bb_im, ssm_c_re[layer], ssm_c_im[layer],
               ssm_d[layer], ssm_w_glu[layer], ssm_b_glu[layer])
    ya = _dsa(q, qi, wit, k, ki, vt).transpose(0, 2, 1)
    h1, u2 = _mix(x, ys_t.transpose(1, 0, 2), ya, ada3, norm1_g[layer], w_in[layer], w_proj_ssm[layer],
                  w_proj_attn[layer], w_out[layer], norm2_g[layer])
    return _moe(h1, u2, ada3, w_router[layer], router_bias[layer], w_exp_gate[layer], w_exp_up[layer],
                w_exp_down[layer], w_sh_gate[layer], w_sh_up[layer], w_sh_down[layer], norm_f_g)
```

```python
import functools
import math

import jax
import jax.numpy as jnp
import numpy as np
from jax import lax
from jax.experimental import pallas as pl
from jax.experimental.pallas import tpu as pltpu

F32 = jnp.float32
BF16 = jnp.bfloat16
I32 = jnp.int32

SSM_GROUP = 16
SSM_STATE = 64
N_HEADS = 8
HEAD_DIM = 64
IDX_HEADS = 8
IDX_DIM = 64
TOPK_MAX = 256
N_EXPERTS = 64
TOP_K = 8
N_GROUPS = 8
TOPK_GROUPS = 4
ROUTED_SCALE = 2.5
EPS = 1e-6

V7X_VMEM_LIMIT_BYTES = 56 * 1024 * 1024
LANES = 128
SUBLANES = 8

INPROJ_ROWS = 256
S5_STEPS = 64
S5_LANE_CHUNK = 128
DSA_Q_COLS = 256
DSA_K_ROWS = 512
DSA_COUNT_ROWS = 64
BITSLICE_ROWS = 256
POS_SPLIT = 64
MIX_ROWS = 256
MOE_TILE = 256
SLOT_CHUNK = 256
EXPERT_ROWS = 512

NEG_BIG = -1e30
INT_MIN = -(2 ** 31)


def _cparams(sem):
    return pltpu.CompilerParams(dimension_semantics=sem, vmem_limit_bytes=V7X_VMEM_LIMIT_BYTES)


def _bf(x):
    return x.astype(BF16)


def _dot(a, b):
    return jnp.dot(a, b, preferred_element_type=F32)


def _dot_nt(a, b):
    return lax.dot_general(a, b, (((1,), (1,)), ((), ())), preferred_element_type=F32)


def _split(x):
    hi = _bf(x)
    lo = _bf(x - hi.astype(F32))
    return hi, lo


def _dot3(a, b):
    ah, al = _split(a)
    bh, bl = _split(b)
    return _dot(ah, bh) + (_dot(ah, bl) + _dot(al, bh))


def _rms(x, g):
    return x * lax.rsqrt(jnp.mean(x * x, axis=-1, keepdims=True) + EPS) * g


def _ada_kernel(c_ref, w_ref, b_ref, o_ref):
    c = c_ref[...]
    o_ref[...] = _dot3(c * jax.nn.sigmoid(c), w_ref[...]) + b_ref[...]


def _ada(c, w, b):
    bsz, d = c.shape
    n = w.shape[1]
    tn = 1024
    return pl.pallas_call(
        _ada_kernel,
        out_shape=jax.ShapeDtypeStruct((bsz, n), F32),
        grid=(n // tn,),
        in_specs=[pl.BlockSpec((bsz, d), lambda j: (0, 0)),
                  pl.BlockSpec((d, tn), lambda j: (0, j)),
                  pl.BlockSpec((1, tn), lambda j: (0, j))],
        out_specs=pl.BlockSpec((bsz, tn), lambda j: (0, j)),
        compiler_params=_cparams(("arbitrary",)),
        name="ada",
    )(c, w, b.reshape(1, n))


def _inproj_kernel(x_ref, ada_ref, g1_ref, w_ref, wt_ref,
                   us_ref, q_ref, qi_ref, k_ref, ki_ref, vt_ref, wit_ref, *, d, ssm_w, attn_w, idx_w):
    x = x_ref[0]
    shift = ada_ref[0, :, 0:d]
    scale = ada_ref[0, :, d:2 * d]
    u = _bf(_rms(x, g1_ref[...]) * (1.0 + scale) + shift)
    r = _dot(u, w_ref[...])
    o = 0
    us_ref[0] = r[:, o:o + ssm_w]
    o += ssm_w
    q_ref[0] = _bf(r[:, o:o + attn_w])
    o += attn_w
    qi_ref[0] = _bf(r[:, o:o + idx_w])
    o += idx_w
    k_ref[0] = _bf(r[:, o:o + HEAD_DIM])
    o += LANES
    ki_ref[0] = _bf(r[:, o:o + IDX_DIM])
    rt = _dot_nt(wt_ref[...], u)
    vt_ref[0] = _bf(rt[0:HEAD_DIM])
    wit_ref[0] = rt[HEAD_DIM:HEAD_DIM + IDX_HEADS]


def _split_w_in(w_in, d):
    ssm_w = 512
    sizes = (ssm_w, N_HEADS * HEAD_DIM, HEAD_DIM, HEAD_DIM, IDX_HEADS * IDX_DIM, IDX_DIM, IDX_HEADS, d, d)
    offs = [0]
    for s in sizes:
        offs.append(offs[-1] + s)
    return [w_in[:, offs[i]:offs[i + 1]] for i in range(9)]


def _inproj(x, ada3, g1, w_in):
    bsz, seq, d = x.shape
    ssm_w = 512
    attn_w = N_HEADS * HEAD_DIM
    idx_w = IDX_HEADS * IDX_DIM
    w_ssm, w_q, w_k, w_v, w_qi, w_ki, w_wi, _, _ = _split_w_in(w_in, d)
    zpad = lambda n: jnp.zeros((d, n), F32)
    wbig = jnp.concatenate([
        w_ssm, w_q * (HEAD_DIM ** -0.5), w_qi * (IDX_DIM ** -0.5),
        w_k, zpad(LANES - HEAD_DIM), w_ki, zpad(LANES - IDX_DIM)], axis=1).astype(BF16)
    wt = jnp.concatenate([w_v, w_wi, zpad(LANES - HEAD_DIM - IDX_HEADS)], axis=1).T.astype(BF16)
    nw = wbig.shape[1]
    tl = INPROJ_ROWS
    kern = functools.partial(_inproj_kernel, d=d, ssm_w=ssm_w, attn_w=attn_w, idx_w=idx_w)
    row = lambda w: pl.BlockSpec((1, tl, w), lambda b, l: (b, l, 0))
    colt = lambda h: pl.BlockSpec((1, h, tl), lambda b, l: (b, 0, l))
    return pl.pallas_call(
        kern,
        out_shape=(jax.ShapeDtypeStruct((bsz, seq, ssm_w), F32),
                   jax.ShapeDtypeStruct((bsz, seq, attn_w), BF16),
                   jax.ShapeDtypeStruct((bsz, seq, idx_w), BF16),
                   jax.ShapeDtypeStruct((bsz, seq, HEAD_DIM), BF16),
                   jax.ShapeDtypeStruct((bsz, seq, IDX_DIM), BF16),
                   jax.ShapeDtypeStruct((bsz, HEAD_DIM, seq), BF16),
                   jax.ShapeDtypeStruct((bsz, IDX_HEADS, seq), F32)),
        grid=(bsz, seq // tl),
        in_specs=[row(d),
                  pl.BlockSpec((1, 1, ada3.shape[2]), lambda b, l: (b, 0, 0)),
                  pl.BlockSpec((1, d), lambda b, l: (0, 0)),
                  pl.BlockSpec((d, nw), lambda b, l: (0, 0)),
                  pl.BlockSpec((LANES, d), lambda b, l: (0, 0))],
        out_specs=(row(ssm_w), row(attn_w), row(idx_w), row(HEAD_DIM), row(IDX_DIM),
                   colt(HEAD_DIM), colt(IDX_HEADS)),
        compiler_params=_cparams(("arbitrary", "arbitrary")),
        name="inproj",
    )(x, ada3, g1.reshape(1, d), wbig, wt)


def _s5disc_kernel(lr_ref, li_ref, ldt_ref, br_ref, bi_ref, are_ref, aim_ref, bbr_ref, bbi_ref):
    lr = lr_ref[...]
    li = li_ref[...]
    dt = jnp.exp(ldt_ref[...])
    mag = jnp.exp(lr * dt)
    a_re = mag * jnp.cos(li * dt)
    a_im = mag * jnp.sin(li * dt)
    den = lr * lr + li * li
    n_re = a_re - 1.0
    f_re = (n_re * lr + a_im * li) / den
    f_im = (a_im * lr - n_re * li) / den
    br = br_ref[...]
    bi = bi_ref[...]
    are_ref[...] = a_re
    aim_ref[...] = a_im
    bbr_ref[...] = f_re * br - f_im * bi
    bbi_ref[...] = f_re * bi + f_im * br


def _s5disc(lam_re, lam_im, log_dt, b_re, b_im):
    g, p = lam_re.shape
    h = b_re.shape[2]
    rep = lambda a: jnp.repeat(a, h, axis=1)
    ldt = jnp.broadcast_to(log_dt[:, None], (g, p * h))
    sds = jax.ShapeDtypeStruct((g, p * h), F32)
    a_re, a_im, bb_re, bb_im = pl.pallas_call(
        _s5disc_kernel, out_shape=(sds, sds, sds, sds), name="s5disc",
    )(rep(lam_re), rep(lam_im), ldt, b_re.reshape(g, p * h), b_im.reshape(g, p * h))
    return a_re[:, ::h], a_im[:, ::h], bb_re.reshape(g, p, h), bb_im.reshape(g, p, h)


def _s5_kernel(u_ref, wb_ref, ar_ref, ai_ref, cc_ref, dsk_ref, wg_ref, bg_ref, o_ref, buf, hst, *, tl, width):
    nch = width // S5_LANE_CHUNK
    sw = S5_LANE_CHUNK // SSM_GROUP * SSM_STATE
    rows = tl * SUBLANES

    @pl.when(pl.program_id(0) == 0)
    def _():
        hst[...] = jnp.zeros_like(hst)

    u = u_ref[...].reshape(rows, width)
    ub = _bf(u)
    for j in range(nch):
        buf[:, j * 2 * sw:(j + 1) * 2 * sw] = _dot(ub[:, j * S5_LANE_CHUNK:(j + 1) * S5_LANE_CHUNK], wb_ref[j])

    for j in range(nch):
        re_cols = slice(j * 2 * sw, j * 2 * sw + sw)
        im_cols = slice(j * 2 * sw + sw, (j + 1) * 2 * sw)
        a_re = jnp.broadcast_to(ar_ref[:, j * sw:(j + 1) * sw], (SUBLANES, sw))
        a_im = jnp.broadcast_to(ai_ref[:, j * sw:(j + 1) * sw], (SUBLANES, sw))

        def step(t, carry, re_cols=re_cols, im_cols=im_cols, a_re=a_re, a_im=a_im):
            h_re, h_im = carry
            r0 = pl.multiple_of(t * SUBLANES, SUBLANES)
            n_re = (a_re * h_re - a_im * h_im) + buf[pl.ds(r0, SUBLANES), re_cols]
            n_im = (a_re * h_im + a_im * h_re) + buf[pl.ds(r0, SUBLANES), im_cols]
            buf[pl.ds(r0, SUBLANES), re_cols] = n_re
            buf[pl.ds(r0, SUBLANES), im_cols] = n_im
            return n_re, n_im

        h_re, h_im = lax.fori_loop(0, tl, step, (hst[:, re_cols], hst[:, im_cols]), unroll=8)
        hst[:, re_cols] = h_re
        hst[:, im_cols] = h_im

    ys = [_dot(_bf(buf[:, j * 2 * sw:(j + 1) * 2 * sw]), cc_ref[j]) for j in range(nch)]
    y = jnp.concatenate(ys, axis=1) + dsk_ref[...] * u
    y = jax.nn.gelu(y)
    y = y * jax.nn.sigmoid(_dot(_bf(y), wg_ref[...]) + bg_ref[...])
    o_ref[...] = _bf(y).reshape(tl, SUBLANES, width)


def _s5(u_t, a_re, a_im, bb_re, bb_im, c_re, c_im, d_skip, w_glu, b_glu):
    seq, bsz, width = u_t.shape
    assert bsz == SUBLANES
    nch = width // S5_LANE_CHUNK
    gpc = S5_LANE_CHUNK // SSM_GROUP
    sw = gpc * SSM_STATE
    eye = jnp.eye(gpc, dtype=F32)

    def bmat(bb):
        t = bb.reshape(nch, gpc, SSM_STATE, SSM_GROUP).transpose(0, 1, 3, 2)
        return jnp.einsum('jghp,gk->jghkp', t, eye).reshape(nch, S5_LANE_CHUNK, sw)

    def cmat(cc):
        t = cc.reshape(nch, gpc, SSM_GROUP, SSM_STATE).transpose(0, 1, 3, 2)
        return jnp.einsum('jgph,gk->jgpkh', t, eye).reshape(nch, sw, S5_LANE_CHUNK)

    wb = jnp.concatenate([bmat(bb_re), bmat(bb_im)], axis=2)
    cc = jnp.concatenate([cmat(c_re), -cmat(c_im)], axis=1)
    tl = S5_STEPS
    full = lambda a: pl.BlockSpec(a.shape, lambda i: (0,) * a.ndim)
    args = (u_t, wb.astype(BF16), a_re.reshape(1, -1), a_im.reshape(1, -1), cc.astype(BF16),
            d_skip.reshape(1, width), w_glu.astype(BF16), b_glu.reshape(1, width))
    return pl.pallas_call(
        functools.partial(_s5_kernel, tl=tl, width=width),
        out_shape=jax.ShapeDtypeStruct((seq, bsz, width), BF16),
        grid=(seq // tl,),
        in_specs=[pl.BlockSpec((tl, bsz, width), lambda i: (i, 0, 0))] + [full(a) for a in args[1:]],
        out_specs=pl.BlockSpec((tl, bsz, width), lambda i: (i, 0, 0)),
        scratch_shapes=[pltpu.VMEM((tl * SUBLANES, nch * 2 * sw), F32),
                        pltpu.VMEM((SUBLANES, nch * 2 * sw), F32)],
        compiler_params=_cparams(("arbitrary",)),
        name="s5",
    )(*args)


def _bit_transpose32(words):
    x = list(words)
    j, m = 16, 0x0000FFFF
    while j:
        k = 0
        while k < 32:
            t = (x[k] ^ lax.shift_right_logical(x[k + j], jnp.int32(j))) & jnp.int32(m - (1 << 32) if m >= 1 << 31 else m)
            x[k] = x[k] ^ t
            x[k + j] = x[k + j] ^ lax.shift_left(t, jnp.int32(j))
            k = (k + j + 1) & ~j
        j >>= 1
        m = (m ^ (m << j)) & 0xFFFFFFFF
    return x


def _dsa_kernel(qt_ref, qit_ref, wit_ref, ka_ref, ki_ref, vt_ref, o_ref, key_s, mb_s, acc_s, pl_s, p_s, *, tq, tk, topk, seq):
    i = pl.program_id(1)
    q0 = i * tq
    nkt = (q0 + tq + tk - 1) // tk
    ch = DSA_COUNT_ROWS
    krow = lax.broadcasted_iota(I32, (tk, tq), 0)
    qcol = q0 + lax.broadcasted_iota(I32, (tk, tq), 1)
    crow = lax.broadcasted_iota(I32, (ch, tq), 0)

    wb = wit_ref[0] * (IDX_HEADS ** -0.5)

    def score_tile(j, _):
        r0 = pl.multiple_of(j * tk, tk)
        kit = ki_ref[0, pl.ds(r0, tk), :]
        acc = jnp.zeros((tk, tq), F32)
        for h in range(IDX_HEADS):
            s = _dot(kit, qit_ref[0, h * IDX_DIM:(h + 1) * IDX_DIM, :])
            acc = acc + wb[h:h + 1, :] * jnp.maximum(s, 0.0)
        bits = lax.bitcast_convert_type(acc, I32)
        key = jnp.where(bits < 0, bits ^ jnp.int32(0x7FFFFFFF), bits)
        key = jnp.where(acc == 0.0, 0, key)
        key = jnp.where(krow + r0 <= qcol, key, INT_MIN)
        key_s[pl.ds(r0, tk), :] = key
        ukey = key ^ INT_MIN
        for c in range(tk // BITSLICE_ROWS):
            words = [ukey[c * BITSLICE_ROWS + v * SUBLANES:c * BITSLICE_ROWS + (v + 1) * SUBLANES, :]
                     for v in range(32)]
            planes = _bit_transpose32(words)
            g0 = pl.multiple_of((j * (tk // BITSLICE_ROWS) + c) * SUBLANES, SUBLANES)
            for it in range(32):
                pl_s[it, pl.ds(g0, SUBLANES), :] = planes[it]
        return 0

    @pl.when((pl.program_id(0) == 0) & (i == 0))
    def _():
        pl_s[...] = jnp.zeros(pl_s.shape, I32)

    lax.fori_loop(0, nkt, score_tile, 0)

    def count(pred):
        def tile(j, cnt):
            for c in range(tk // ch):
                rr = pl.multiple_of(j * tk + c * ch, ch)
                cnt = cnt + jnp.where(pred(key_s[pl.ds(rr, ch), :], rr), 1, 0)
            return cnt
        cnt = lax.fori_loop(0, nkt, tile, jnp.zeros((ch, tq), I32))
        return jnp.sum(cnt.astype(F32), axis=0, keepdims=True)

    ngrp = seq // 32

    def lane_count(words):
        pc = lax.population_count(words).reshape(ngrp // SUBLANES, SUBLANES, tq)
        return jnp.sum(jnp.sum(pc, axis=0).astype(F32), axis=0, keepdims=True)

    def bit_step(it, carry):
        alive, above, ans_u = carry
        ones = alive & pl_s[it]
        cnt1 = lane_count(ones)
        take = above + cnt1 >= float(topk)
        alive = jnp.where(take, ones, alive ^ ones)
        above = jnp.where(take, above, above + cnt1)
        ans_u = jnp.where(take, ans_u | lax.shift_left(jnp.int32(1), 31 - it), ans_u)
        return alive, above, ans_u

    grow = lax.broadcasted_iota(I32, (ngrp, tq), 0)
    alive0 = jnp.where(grow < nkt * (tk // 32), -1, 0)
    alive, above, ans_u = lax.fori_loop(
        0, 32, bit_step, (alive0, jnp.zeros((1, tq), F32), jnp.zeros((1, tq), I32)))
    thr = jnp.maximum(ans_u ^ INT_MIN, INT_MIN + 1)
    cnt_ge = above + lane_count(alive)
    tied = jnp.where(ans_u != 0, cnt_ge, 0.0) > float(topk)
    has_ties = jnp.max(jnp.where(tied, 1.0, 0.0)) > 0.0

    def tie_cut():
        need = float(topk) - count(lambda kb, rr: kb > thr)
        nbits = max(1, (seq - 1).bit_length())

        def idx_step(b, x):
            cand = x | lax.shift_left(jnp.int32(1), nbits - 1 - b)
            below = count(lambda kb, rr: jnp.where(kb == thr, crow + rr, seq) < cand)
            return jnp.where(below < need, cand, x)

        x = lax.fori_loop(0, nbits, idx_step, jnp.zeros((1, tq), I32))
        return jnp.where(tied, x, seq)

    cut = lax.cond(has_ties, tie_cut, lambda: jnp.full((1, tq), seq, I32))

    def bias_tile(j, _):
        for c in range(tk // ch):
            rr = pl.multiple_of(j * tk + c * ch, ch)
            kb = key_s[pl.ds(rr, ch), :]
            tie_bias = jnp.where(crow + rr <= cut, 0.0, NEG_BIG)
            mb_s[pl.ds(rr, ch), :] = jnp.where(kb > thr, 0.0, jnp.where(kb == thr, tie_bias, NEG_BIG))
        return 0

    lax.fori_loop(0, nkt, bias_tile, 0)

    def logits(j, h):
        r0 = pl.multiple_of(j * tk, tk)
        s = _dot(ka_ref[0, pl.ds(r0, tk), :], qt_ref[0, h * LANES:(h + 1) * LANES, :]) + mb_s[pl.ds(r0, tk), :]
        return s.reshape(tk // SUBLANES, SUBLANES, tq)

    acc_s[...] = jnp.zeros(acc_s.shape, F32)

    def attn_tile(j, carry):
        ms, ls = carry
        r0 = pl.multiple_of(j * tk, tk)
        new_m, new_l, alphas = [], [], []
        for h in range(N_HEADS):
            s = logits(j, h)
            m_new = jnp.maximum(ms[h], jnp.max(jnp.max(s, axis=0), axis=0, keepdims=True))
            alpha = jnp.exp(ms[h] - m_new)
            p = jnp.exp(s - m_new)
            new_m.append(m_new)
            new_l.append(alpha * ls[h] + jnp.sum(p, axis=0))
            alphas.append(alpha)
            p_s[h] = _bf(p.reshape(tk, tq))
        for h in range(N_HEADS):
            rows = slice(h * HEAD_DIM, (h + 1) * HEAD_DIM)
            acc_s[rows, :] = alphas[h] * acc_s[rows, :] + _dot(vt_ref[0, :, pl.ds(r0, tk)], p_s[h])
        return tuple(new_m), tuple(new_l)

    init = ((jnp.full((1, tq), NEG_BIG, F32),) * N_HEADS, (jnp.zeros((SUBLANES, tq), F32),) * N_HEADS)
    _, ls = lax.fori_loop(0, nkt, attn_tile, init)
    for h in range(N_HEADS):
        rows = slice(h * HEAD_DIM, (h + 1) * HEAD_DIM)
        o_ref[0, rows, :] = _bf(acc_s[rows, :] / jnp.sum(ls[h], axis=0, keepdims=True))


def _dsa(q, qi, wit, k, ki, vt):
    bsz, seq, aw = q.shape
    tq = min(DSA_Q_COLS, seq)
    tk = min(DSA_K_ROWS, seq)
    topk = min(TOPK_MAX, seq // 4)
    assert (seq - 1) // POS_SPLIT < 256 and POS_SPLIT <= 256, "key positions must split into two bf16-exact parts"
    slopes = [2.0 ** (-8.0 * (h + 1) / N_HEADS) for h in range(N_HEADS)]
    assert all(float(np.float32(sl).astype(BF16)) == sl for sl in slopes), "ALiBi slopes must be bf16-exact"
    pos = jnp.arange(seq, dtype=I32)
    posc = jnp.stack([(pos // POS_SPLIT) * POS_SPLIT, pos % POS_SPLIT], axis=1).astype(BF16)
    ka = jnp.concatenate([k, jnp.broadcast_to(posc[None], (bsz, seq, 2)),
                          jnp.zeros((bsz, seq, LANES - HEAD_DIM - 2), BF16)], axis=2)
    qh = q.reshape(bsz, seq, N_HEADS, HEAD_DIM).transpose(0, 2, 3, 1)
    srow = jnp.asarray(slopes, BF16)[None, :, None, None]
    extra = jnp.concatenate([jnp.broadcast_to(srow, (bsz, N_HEADS, 2, seq)),
                             jnp.zeros((bsz, N_HEADS, LANES - HEAD_DIM - 2, seq), BF16)], axis=2)
    qt = jnp.concatenate([qh, extra], axis=2).reshape(bsz, N_HEADS * LANES, seq)
    qit = qi.transpose(0, 2, 1)
    kern = functools.partial(_dsa_kernel, tq=tq, tk=tk, topk=topk, seq=seq)
    cols = lambda r: pl.BlockSpec((1, r, tq), lambda b, i: (b, 0, i))
    return pl.pallas_call(
        kern,
        out_shape=jax.ShapeDtypeStruct((bsz, aw, seq), BF16),
        grid=(bsz, seq // tq),
        in_specs=[cols(N_HEADS * LANES), cols(qit.shape[1]), cols(IDX_HEADS),
                  pl.BlockSpec((1, seq, LANES), lambda b, i: (b, 0, 0)),
                  pl.BlockSpec((1, seq, IDX_DIM), lambda b, i: (b, 0, 0)),
                  pl.BlockSpec((1, HEAD_DIM, seq), lambda b, i: (b, 0, 0))],
        out_specs=cols(aw),
        scratch_shapes=[pltpu.VMEM((seq, tq), I32), pltpu.VMEM((seq, tq), F32), pltpu.VMEM((aw, tq), F32),
                        pltpu.VMEM((32, seq // 32, tq), I32), pltpu.VMEM((N_HEADS, tk, tq), BF16)],
        compiler_params=_cparams(("arbitrary", "arbitrary")),
        name="dsa",
    )(qt, qit, wit, ka, ki, vt)


def _mix_kernel(x_ref, ys_ref, ya_ref, ada_ref, g1_ref, wgt_ref, wps_ref, wpa_ref, wo_ref, g2_ref,
                h_ref, u2_ref, *, d):
    gate1 = ada_ref[0, :, 2 * d:3 * d]
    shift2 = ada_ref[0, :, 3 * d:4 * d]
    scale2 = ada_ref[0, :, 4 * d:5 * d]
    x = x_ref[0]
    u = _bf(_rms(x, g1_ref[...]) * (1.0 + ada_ref[0, :, d:2 * d]) + ada_ref[0, :, 0:d])
    g = _dot(u, wgt_ref[...])
    mixed = (jax.nn.sigmoid(g[:, 0:d]) * _dot(ys_ref[0], wps_ref[...])
             + jax.nn.sigmoid(g[:, d:2 * d]) * _dot(ya_ref[0], wpa_ref[...]))
    h = x + gate1 * _dot(_bf(mixed), wo_ref[...])
    h_ref[0] = h
    u2_ref[0] = _rms(h, g2_ref[...]) * (1.0 + scale2) + shift2


def _mix(x, ys, ya, ada3, g1, w_in, wps, wpa, wo, g2):
    bsz, seq, d = x.shape
    tm = MIX_ROWS
    row = lambda w: pl.BlockSpec((1, tm, w), lambda b, l: (b, l, 0))
    full = lambda a: pl.BlockSpec(a.shape, lambda b, l: (0,) * a.ndim)
    wps, wpa, wo = wps.astype(BF16), wpa.astype(BF16), wo.astype(BF16)
    wgt = jnp.concatenate(_split_w_in(w_in, d)[7:9], axis=1).astype(BF16)
    g1 = g1.reshape(1, d)
    g2 = g2.reshape(1, d)
    return pl.pallas_call(
        functools.partial(_mix_kernel, d=d),
        out_shape=(jax.ShapeDtypeStruct((bsz, seq, d), F32), jax.ShapeDtypeStruct((bsz, seq, d), F32)),
        grid=(bsz, seq // tm),
        in_specs=[row(d), row(ys.shape[2]), row(ya.shape[2]),
                  pl.BlockSpec((1, 1, ada3.shape[2]), lambda b, l: (b, 0, 0)),
                  full(g1), full(wgt), full(wps), full(wpa), full(wo), full(g2)],
        out_specs=(row(d), row(d)),
        compiler_params=_cparams(("arbitrary", "arbitrary")),
        name="mix",
    )(x, ys, ya, ada3, g1, wgt, wps, wpa, wo, g2)


def _first_max(cur, idx, axis, big):
    m = jnp.max(cur, axis=axis, keepdims=True)
    first = jnp.min(jnp.where(cur == m, idx, big), axis=axis, keepdims=True)
    return m, idx == first


def _route_kernel(u_ref, wrh_ref, wrl_ref, rb_ref, tri_ref, ltri_ref,
                  gt_ref, loc_ref, c8_ref, loff_ref, run0_ref, tot_ref, run_s, *, t):
    @pl.when(pl.program_id(0) == 0)
    def _():
        run_s[...] = jnp.zeros_like(run_s)

    uh, ul = _split(u_ref[...])
    logits = _dot_nt(wrh_ref[...], uh) + (_dot_nt(wrl_ref[...], uh) + _dot_nt(wrh_ref[...], ul))
    scores = jax.nn.sigmoid(logits)
    biased = scores + rb_ref[...]
    per_group = N_EXPERTS // N_GROUPS
    b3 = biased.reshape(N_GROUPS, per_group, t)
    i3 = lax.broadcasted_iota(I32, b3.shape, 1)
    m1, hit1 = _first_max(b3, i3, 1, per_group)
    m2 = jnp.max(jnp.where(hit1, -jnp.inf, b3), axis=1, keepdims=True)
    gs = (m1 + m2).reshape(N_GROUPS, t)
    gi = lax.broadcasted_iota(I32, gs.shape, 0)
    gsel = jnp.zeros(gs.shape, F32)
    for _ in range(TOPK_GROUPS):
        _, hit = _first_max(gs, gi, 0, N_GROUPS)
        gsel = jnp.where(hit, 1.0, gsel)
        gs = jnp.where(hit, -jnp.inf, gs)
    cur = jnp.where(gsel.reshape(N_GROUPS, 1, t) > 0.0, b3, -jnp.inf).reshape(N_EXPERTS, t)
    ei = lax.broadcasted_iota(I32, cur.shape, 0)
    hits = []
    gates = []
    for _ in range(TOP_K):
        _, hit = _first_max(cur, ei, 0, N_EXPERTS)
        hits.append(hit)
        gates.append(jnp.sum(jnp.where(hit, scores, 0.0), axis=0, keepdims=True))
        cur = jnp.where(hit, -jnp.inf, cur)
    gate = jnp.concatenate(gates, axis=0)
    gt_ref[...] = gate / jnp.sum(gate, axis=0, keepdims=True) * ROUTED_SCALE
    onehot = jnp.zeros(cur.shape, F32)
    for hit in hits:
        onehot = jnp.where(hit, 1.0, onehot)
    cnt = jnp.sum(onehot, axis=1, keepdims=True)
    c8 = jnp.floor((cnt + (SUBLANES - 1)) * (1.0 / SUBLANES)) * SUBLANES
    c8l = jnp.broadcast_to(c8, (N_EXPERTS, LANES))
    loff = _dot(ltri_ref[...], _bf(c8l))
    slot = _dot(_bf(onehot), tri_ref[...]) + loff[:, 0:1]
    loc_ref[...] = jnp.concatenate(
        [jnp.sum(jnp.where(hit, slot, 0.0), axis=0, keepdims=True) for hit in hits], axis=0).astype(I32)
    c8_ref[0] = c8l
    loff_ref[0] = loff
    run0_ref[0] = run_s[...]
    run_s[...] = run_s[...] + c8
    tot_ref[...] = run_s[...]


def _route(u2, w_router, router_bias):
    n, d = u2.shape
    t = min(MOE_TILE, n)
    nt = n // t
    wt = w_router.T
    wrh = wt.astype(BF16)
    wrl = (wt - wrh.astype(F32)).astype(BF16)
    tri = (jnp.arange(t)[:, None] < jnp.arange(t)[None, :]).astype(BF16)
    ex = jnp.arange(N_EXPERTS)
    ltri = (ex[None, :] < ex[:, None]).astype(BF16)
    full = lambda a: pl.BlockSpec(a.shape, lambda i: (0,) * a.ndim)
    col = pl.BlockSpec((TOP_K, t), lambda i: (0, i))
    tab = pl.BlockSpec((1, N_EXPERTS, LANES), lambda i: (i, 0, 0))
    tab_sds = jax.ShapeDtypeStruct((nt, N_EXPERTS, LANES), F32)
    rb = router_bias.reshape(N_EXPERTS, 1)
    return pl.pallas_call(
        functools.partial(_route_kernel, t=t),
        out_shape=(jax.ShapeDtypeStruct((TOP_K, n), F32), jax.ShapeDtypeStruct((TOP_K, n), I32),
                   tab_sds, tab_sds, tab_sds, jax.ShapeDtypeStruct((N_EXPERTS, LANES), F32)),
        grid=(nt,),
        in_specs=[pl.BlockSpec((t, d), lambda i: (i, 0)), full(wrh), full(wrl), full(rb), full(tri), full(ltri)],
        out_specs=(col, col, tab, tab, tab, pl.BlockSpec((N_EXPERTS, LANES), lambda i: (0, 0))),
        scratch_shapes=[pltpu.VMEM((N_EXPERTS, LANES), F32)],
        compiler_params=_cparams(("arbitrary",)),
        name="route",
    )(u2, wrh, wrl, rb, tri, ltri)


RUN_BITS = tuple(1 << b for b in reversed(range((MOE_TILE // SUBLANES).bit_length())))


def _for_each_run_piece(n8_ref, src_ref, dst_ref, tile, bits, fn):
    def per_expert(e, _):
        idx = tile * N_EXPERTS + e
        n8 = n8_ref[idx]
        src = src_ref[idx]
        dst = dst_ref[idx]
        for b, p in enumerate(bits):
            off = (n8 & ~(2 * p - 1)) * SUBLANES

            @pl.when((n8 & p) != 0)
            def _(b=b, p=p, off=off):
                fn(pl.multiple_of(src + off, SUBLANES), pl.multiple_of(dst + off, SUBLANES), p * SUBLANES, b % 2)
        return 0

    lax.fori_loop(0, N_EXPERTS, per_expert, 0)


def _wait_rows(n8, make_copy, max_rows):
    for p in tuple(1 << b for b in reversed(range((max_rows // SUBLANES).bit_length()))):
        @pl.when((n8 & p) != 0)
        def _(p=p):
            make_copy(p * SUBLANES).wait()


def _dispatch_kernel(n8_ref, src_ref, dst_ref, tot_ref, zn8_ref, zdst_ref, u_ref, loc_ref, gate_ref, xs_hbm,
                     lbuf, zx, sems, *, tt, nslot, dh):
    i = pl.program_id(0)
    last = pl.num_programs(0) - 1
    par = i % 2

    def run_copy(slot):
        def piece(s0, d0, rows, prio):
            pltpu.make_async_copy(
                lbuf.at[slot, pl.ds(s0, rows)], xs_hbm.at[pl.ds(d0, rows)], sems.at[slot]).start(priority=prio)
        return piece

    def wait_tile(tile, slot):
        _wait_rows(tot_ref[tile], lambda rows: pltpu.make_async_copy(
            lbuf.at[slot, pl.ds(0, rows)], xs_hbm.at[pl.ds(0, rows)], sems.at[slot]), nslot)

    @pl.when(i == 0)
    def _():
        zx[...] = jnp.zeros(zx.shape, I32)

        def zero_piece(s0, d0, rows, prio):
            cx = pltpu.make_async_copy(zx.at[pl.ds(0, rows)], xs_hbm.at[pl.ds(d0, rows)], sems.at[2])
            cx.start()
            cx.wait()

        zbits = tuple(b for b in RUN_BITS if b * SUBLANES < EXPERT_ROWS)
        _for_each_run_piece(zn8_ref, zdst_ref, zdst_ref, 0, zbits, zero_piece)

    ub = _bf(u_ref[...])
    ones = jnp.ones((tt, LANES), BF16)
    loc = loc_ref[...]
    gate = gate_ref[...]
    rows_b = lax.broadcasted_iota(I32, (SLOT_CHUNK, tt), 0).astype(F32).astype(BF16)
    loc_hi = lax.shift_right_logical(loc, SLOT_CHUNK.bit_length() - 1)
    loc_lo = (loc & (SLOT_CHUNK - 1)).astype(F32)
    gate_h = _bf(gate)
    gate_l = _bf(gate - gate_h.astype(F32))
    one_b = jnp.ones((SLOT_CHUNK, tt), BF16)
    for c in range(nslot // SLOT_CHUNK):
        perm = jnp.zeros((SLOT_CHUNK, tt), BF16)
        pgh = jnp.zeros((SLOT_CHUNK, tt), BF16)
        pgl = jnp.zeros((SLOT_CHUNK, tt), BF16)
        for k in range(TOP_K):
            lk = _bf(jnp.where(loc_hi[k:k + 1, :] == c, loc_lo[k:k + 1, :], -1.0))
            eq = rows_b == lk
            perm = jnp.where(eq, one_b, perm)
            pgh = jnp.where(eq, jnp.broadcast_to(gate_h[k:k + 1, :], (SLOT_CHUNK, tt)), pgh)
            pgl = jnp.where(eq, jnp.broadcast_to(gate_l[k:k + 1, :], (SLOT_CHUNK, tt)), pgl)
        cs = slice(c * SLOT_CHUNK, (c + 1) * SLOT_CHUNK)
        xp = lax.bitcast_convert_type(_dot(perm, ub), I32)
        lbuf[par, cs, 0:dh] = xp[:, 0:dh] | lax.shift_right_logical(xp[:, dh:2 * dh], 16)
        lbuf[par, cs, dh:dh + LANES] = lax.bitcast_convert_type(_dot(pgh, ones) + _dot(pgl, ones), I32)

    _for_each_run_piece(n8_ref, src_ref, dst_ref, i, RUN_BITS, run_copy(par))

    @pl.when(i > 0)
    def _():
        wait_tile(i - 1, 1 - par)

    @pl.when(i == last)
    def _():
        wait_tile(i, par)


def _dispatch(tabs, ztabs, u2, loc_t, gate_t, n_rows):
    n, d = u2.shape
    tt = min(MOE_TILE, n)
    nslot = TOP_K * tt + N_EXPERTS * SUBLANES
    dh = d // 2
    assert nslot % SLOT_CHUNK == 0 and tt // SUBLANES == RUN_BITS[0]
    col = pl.BlockSpec((TOP_K, tt), lambda i, *_: (0, i))
    return pl.pallas_call(
        functools.partial(_dispatch_kernel, tt=tt, nslot=nslot, dh=dh),
        out_shape=jax.ShapeDtypeStruct((n_rows, dh + LANES), I32),
        grid_spec=pltpu.PrefetchScalarGridSpec(
            num_scalar_prefetch=6, grid=(n // tt,),
            in_specs=[pl.BlockSpec((tt, d), lambda i, *_: (i, 0)), col, col],
            out_specs=pl.BlockSpec(memory_space=pl.ANY),
            scratch_shapes=[pltpu.VMEM((2, nslot, dh + LANES), I32),
                            pltpu.VMEM((EXPERT_ROWS // 2, dh + LANES), I32),
                            pltpu.SemaphoreType.DMA((3,))]),
        compiler_params=_cparams(("arbitrary",)),
        name="dispatch",
    )(*tabs, *ztabs, u2, loc_t, gate_t)


def _experts_kernel(be_ref, nu_ref, xs_ref, wg_ref, wu_ref, wd_ref, ys_ref, *, d):
    del be_ref
    dh = d // 2

    @pl.when(pl.program_id(0) < nu_ref[0])
    def _():
        w = xs_ref[:, 0:dh]
        xa = _bf(lax.bitcast_convert_type(w & jnp.int32(-65536), F32))
        xb = _bf(lax.bitcast_convert_type(lax.shift_left(w, 16), F32))
        gate = lax.bitcast_convert_type(xs_ref[:, dh:dh + LANES], F32)
        hg = _dot(xa, wg_ref[0, 0:dh, :]) + _dot(xb, wg_ref[0, dh:d, :])
        hu = _dot(xa, wu_ref[0, 0:dh, :]) + _dot(xb, wu_ref[0, dh:d, :])
        y = _dot(_bf(jax.nn.silu(hg) * hu), wd_ref[0]) * jnp.tile(gate, (1, d // LANES))
        yb = lax.bitcast_convert_type(_bf(y).astype(F32), I32)
        ys_ref[...] = yb[:, 0:dh] | lax.shift_right_logical(yb[:, dh:d], 16)


def _experts(blk_expert, n_used, xs, wg, wu, wd):
    rows, xw = xs.shape
    d = wg.shape[1]
    de = wg.shape[2]
    nblk = rows // EXPERT_ROWS
    blk = lambda i, be, nu: jnp.minimum(i, nu[0] - 1)
    return pl.pallas_call(
        functools.partial(_experts_kernel, d=d),
        out_shape=jax.ShapeDtypeStruct((rows, d // 2), I32),
        grid_spec=pltpu.PrefetchScalarGridSpec(
            num_scalar_prefetch=2, grid=(nblk,),
            in_specs=[pl.BlockSpec((EXPERT_ROWS, xw), lambda i, be, nu: (blk(i, be, nu), 0)),
                      pl.BlockSpec((1, d, de), lambda i, be, nu: (be[blk(i, be, nu)], 0, 0)),
                      pl.BlockSpec((1, d, de), lambda i, be, nu: (be[blk(i, be, nu)], 0, 0)),
                      pl.BlockSpec((1, de, d), lambda i, be, nu: (be[blk(i, be, nu)], 0, 0))],
            out_specs=pl.BlockSpec((EXPERT_ROWS, d // 2), lambda i, be, nu: (blk(i, be, nu), 0))),
        compiler_params=_cparams(("arbitrary",)),
        name="experts",
    )(blk_expert, n_used, xs, wg.astype(BF16), wu.astype(BF16), wd.astype(BF16))


def _combine_kernel(n8_ref, src_ref, dst_ref, tot_ref, ys_hbm, loc_ref, h_ref, u2_ref, ada_ref, wsg_ref, wsu_ref, wsd_ref,
                    gf_ref, o_ref, ybuf, sems, *, tt, nslot, d):
    i = pl.program_id(0)
    last = pl.num_programs(0) - 1
    par = i % 2

    def run_copy(slot):
        def piece(s0, d0, rows, prio):
            pltpu.make_async_copy(
                ys_hbm.at[pl.ds(d0, rows)], ybuf.at[slot, pl.ds(s0, rows)], sems.at[slot]).start(priority=prio)
        return piece

    @pl.when(i == 0)
    def _():
        ybuf[...] = jnp.zeros(ybuf.shape, I32)
        _for_each_run_piece(n8_ref, src_ref, dst_ref, 0, RUN_BITS, run_copy(0))

    @pl.when(i < last)
    def _():
        _for_each_run_piece(n8_ref, src_ref, dst_ref, i + 1, RUN_BITS, run_copy(1 - par))

    x = _bf(u2_ref[...])
    shared = _dot(_bf(jax.nn.silu(_dot(x, wsg_ref[...])) * _dot(x, wsu_ref[...])), wsd_ref[...])
    _wait_rows(tot_ref[i], lambda rows: pltpu.make_async_copy(
        ys_hbm.at[pl.ds(0, rows)], ybuf.at[par, pl.ds(0, rows)], sems.at[par]), nslot)

    loc = loc_ref[...]
    dh = d // 2
    routed_a = jnp.zeros((tt, dh), F32)
    routed_b = jnp.zeros((tt, dh), F32)
    cols_b = lax.broadcasted_iota(I32, (tt, SLOT_CHUNK), 1).astype(F32).astype(BF16)
    loc_hi = lax.shift_right_logical(loc, SLOT_CHUNK.bit_length() - 1)
    loc_lo = (loc & (SLOT_CHUNK - 1)).astype(F32)
    one_b = jnp.ones((tt, SLOT_CHUNK), BF16)
    for c in range(nslot // SLOT_CHUNK):
        pick = jnp.zeros((tt, SLOT_CHUNK), BF16)
        for k in range(TOP_K):
            lk = _bf(jnp.where(loc_hi[:, k:k + 1] == c, loc_lo[:, k:k + 1], -1.0))
            pick = jnp.where(cols_b == lk, one_b, pick)
        w = ybuf[par, c * SLOT_CHUNK:(c + 1) * SLOT_CHUNK, :]
        routed_a = routed_a + _dot(pick, _bf(lax.bitcast_convert_type(w & jnp.int32(-65536), F32)))
        routed_b = routed_b + _dot(pick, _bf(lax.bitcast_convert_type(lax.shift_left(w, 16), F32)))
    routed = jnp.concatenate([routed_a, routed_b], axis=1)
    gate2 = ada_ref[0, :, 5 * d:6 * d]
    h = h_ref[...] + gate2 * (routed + shared)
    o_ref[...] = _rms(h, gf_ref[...])


def _combine(tabs, ys, loc, h1, u2, ada3, wsg, wsu, wsd, gf, seq):
    n, d = h1.shape
    tt = min(MOE_TILE, n)
    nslot = TOP_K * tt + N_EXPERTS * SUBLANES
    per_b = seq // tt
    row = pl.BlockSpec((tt, d), lambda i, *_: (i, 0))
    full = lambda a: pl.BlockSpec(a.shape, lambda i, *_: (0,) * a.ndim)
    wsg, wsu, wsd = wsg.astype(BF16), wsu.astype(BF16), wsd.astype(BF16)
    gf = gf.reshape(1, d)
    return pl.pallas_call(
        functools.partial(_combine_kernel, tt=tt, nslot=nslot, d=d),
        out_shape=jax.ShapeDtypeStruct((n, d), F32),
        grid_spec=pltpu.PrefetchScalarGridSpec(
            num_scalar_prefetch=4, grid=(n // tt,),
            in_specs=[pl.BlockSpec(memory_space=pl.ANY),
                      pl.BlockSpec((tt, TOP_K), lambda i, *_: (i, 0)),
                      row, row,
                      pl.BlockSpec((1, 1, ada3.shape[2]), lambda i, *_: (i // per_b, 0, 0)),
                      full(wsg), full(wsu), full(wsd), full(gf)],
            out_specs=row,
            scratch_shapes=[pltpu.VMEM((2, nslot, d // 2), I32), pltpu.SemaphoreType.DMA((2,))]),
        compiler_params=_cparams(("arbitrary",)),
        name="combine",
    )(*tabs, ys, loc, h1, u2, ada3, wsg, wsu, wsd, gf)


def _moe(h1, u2, ada3, w_router, router_bias, wg, wu, wd, wsg, wsu, wsd, gf):
    bsz, seq, d = h1.shape
    n = bsz * seq
    assert seq % min(MOE_TILE, n) == 0
    h1f = h1.reshape(n, d)
    u2f = u2.reshape(n, d)
    gate_t, loc_t, c8, loff, run0, tot = _route(u2f, w_router, router_bias)
    nt = c8.shape[0]
    as_tab = lambda a: a[:, :, 0].astype(I32)
    tot8 = tot[:, 0].astype(I32)
    padded = (tot8 + EXPERT_ROWS - 1) // EXPERT_ROWS * EXPERT_ROWS
    pend = jnp.cumsum(padded)
    pstart = (pend - padded).astype(I32)
    nblk = (n * TOP_K + nt * N_EXPERTS * (SUBLANES - 1) + N_EXPERTS * (EXPERT_ROWS - 1) + EXPERT_ROWS - 1) // EXPERT_ROWS
    blk_row0 = jnp.arange(nblk, dtype=I32) * EXPERT_ROWS
    blk_expert = jnp.minimum(jnp.sum(pend[None, :] <= blk_row0[:, None], axis=1), N_EXPERTS - 1).astype(I32)
    n_used = (pend[-1:] // EXPERT_ROWS).astype(I32)
    n8 = as_tab(c8) // SUBLANES
    tabs = (n8.reshape(-1), as_tab(loff).reshape(-1), (pstart[None, :] + as_tab(run0)).reshape(-1),
            jnp.sum(n8, axis=1))
    ztabs = ((padded - tot8) // SUBLANES, pstart + tot8)
    xs = _dispatch(tabs, ztabs, u2f, loc_t, gate_t, nblk * EXPERT_ROWS)
    ys = _experts(blk_expert, n_used, xs, wg, wu, wd)
    out = _combine(tabs, ys, loc_t.T, h1f, u2f, ada3, wsg, wsu, wsd, gf, seq)
    return out.reshape(bsz, seq, d)


def kernel(x, c, w_ada, b_ada, norm1_g, w_in, ssm_lambda_re, ssm_lambda_im, ssm_log_dt, ssm_b_re, ssm_b_im,
           ssm_c_re, ssm_c_im, ssm_d, ssm_w_glu, ssm_b_glu, w_proj_ssm, w_proj_attn, w_out, norm2_g, w_router,
           router_bias, w_exp_gate, w_exp_up, w_exp_down, w_sh_gate, w_sh_up, w_sh_down, norm_f_g):
    depth = w_ada.shape[0]
    assert depth == 1, "the final norm is fused into the last (only) layer's combine kernel"
    bsz, seq, d = x.shape
    layer = 0
    ada3 = _ada(c, w_ada[layer], b_ada[layer]).reshape(bsz, 1, 6 * d)
    us, q, qi, k, ki, vt, wit = _inproj(x, ada3, norm1_g[layer], w_in[layer])
    a_re, a_im, bb_re, bb_im = _s5disc(ssm_lambda_re[layer], ssm_lambda_im[layer], ssm_log_dt[layer],
                                       ssm_b_re[layer], ssm_b_im[layer])
    ys_t = _s5(us.transpose(1, 0, 2), a_re, a_im, bb_re, bb_im, ssm_c_re[layer], ssm_c_im[layer],
               ssm_d[layer], ssm_w_glu[layer], ssm_b_glu[layer])
    ya = _dsa(q, qi, wit, k, ki, vt).transpose(0, 2, 1)
    h1, u2 = _mix(x, ys_t.transpose(1, 0, 2), ya, ada3, norm1_g[layer], w_in[layer], w_proj_ssm[layer],
                  w_proj_attn[layer], w_out[layer], norm2_g[layer])
    return _moe(h1, u2, ada3, w_router[layer], router_bias[layer], w_exp_gate[layer], w_exp_up[layer],
                w_exp_down[layer], w_sh_gate[layer], w_sh_up[layer], w_sh_down[layer], norm_f_g)
```

```python
import functools
import math

import jax
import jax.numpy as jnp
import numpy as np
from jax import lax
from jax.experimental import pallas as pl
from jax.experimental.pallas import tpu as pltpu

F32 = jnp.float32
BF16 = jnp.bfloat16
I32 = jnp.int32

SSM_GROUP = 16
SSM_STATE = 64
N_HEADS = 8
HEAD_DIM = 64
IDX_HEADS = 8
IDX_DIM = 64
TOPK_MAX = 256
N_EXPERTS = 64
TOP_K = 8
N_GROUPS = 8
TOPK_GROUPS = 4
ROUTED_SCALE = 2.5
EPS = 1e-6

V7X_VMEM_LIMIT_BYTES = 56 * 1024 * 1024
LANES = 128
SUBLANES = 8

INPROJ_ROWS = 256
S5_STEPS = 64
S5_LANE_CHUNK = 128
DSA_Q_COLS = 256
DSA_K_ROWS = 512
DSA_COUNT_ROWS = 64
BITSLICE_ROWS = 256
POS_SPLIT = 64
MIX_ROWS = 512
MOE_TILE = 256
SLOT_CHUNK = 256
EXPERT_ROWS = 512

NEG_BIG = -1e30
INT_MIN = -(2 ** 31)


def _cparams(sem):
    return pltpu.CompilerParams(dimension_semantics=sem, vmem_limit_bytes=V7X_VMEM_LIMIT_BYTES)


def _bf(x):
    return x.astype(BF16)


def _dot(a, b):
    return jnp.dot(a, b, preferred_element_type=F32)


def _dot_nt(a, b):
    return lax.dot_general(a, b, (((1,), (1,)), ((), ())), preferred_element_type=F32)


def _split(x):
    hi = _bf(x)
    lo = _bf(x - hi.astype(F32))
    return hi, lo


def _dot3(a, b):
    ah, al = _split(a)
    bh, bl = _split(b)
    return _dot(ah, bh) + (_dot(ah, bl) + _dot(al, bh))


def _rms(x, g):
    return x * lax.rsqrt(jnp.mean(x * x, axis=-1, keepdims=True) + EPS) * g


def _ada_kernel(c_ref, w_ref, b_ref, o_ref):
    c = c_ref[...]
    o_ref[...] = _dot3(c * jax.nn.sigmoid(c), w_ref[...]) + b_ref[...]


def _ada(c, w, b):
    bsz, d = c.shape
    n = w.shape[1]
    tn = 1024
    return pl.pallas_call(
        _ada_kernel,
        out_shape=jax.ShapeDtypeStruct((bsz, n), F32),
        grid=(n // tn,),
        in_specs=[pl.BlockSpec((bsz, d), lambda j: (0, 0)),
                  pl.BlockSpec((d, tn), lambda j: (0, j)),
                  pl.BlockSpec((1, tn), lambda j: (0, j))],
        out_specs=pl.BlockSpec((bsz, tn), lambda j: (0, j)),
        compiler_params=_cparams(("arbitrary",)),
        name="ada",
    )(c, w, b.reshape(1, n))


ALIBI_SLOPES = tuple(2.0 ** (-8.0 * (h + 1) / N_HEADS) for h in range(N_HEADS))
QAUG_ROWS = 16


def _inproj_kernel(x_ref, ada_ref, g1_ref, w_ref, wt_ref,
                   us_ref, k_ref, ki_ref, qt_ref, qit_ref, vt_ref, wit_ref, *, d, ssm_w, attn_w, idx_w, tl):
    x = x_ref[0]
    shift = ada_ref[0, :, 0:d]
    scale = ada_ref[0, :, d:2 * d]
    u = _bf(_rms(x, g1_ref[...]) * (1.0 + scale) + shift)
    r = _dot(u, w_ref[...])
    us_ref[0] = r[:, 0:ssm_w]
    k_ref[0] = _bf(r[:, ssm_w:ssm_w + HEAD_DIM])
    ki_ref[0] = _bf(r[:, ssm_w + LANES:ssm_w + LANES + IDX_DIM])
    rt = _dot_nt(wt_ref[...], u)
    arow = lax.broadcasted_iota(I32, (QAUG_ROWS, tl), 0)
    for h in range(N_HEADS):
        base = h * LANES
        qt_ref[0, base:base + HEAD_DIM, :] = _bf(rt[h * HEAD_DIM:(h + 1) * HEAD_DIM])
        qt_ref[0, base + HEAD_DIM:base + HEAD_DIM + QAUG_ROWS, :] = _bf(jnp.where(arow < 2, ALIBI_SLOPES[h], 0.0))
        qt_ref[0, base + HEAD_DIM + QAUG_ROWS:base + LANES, :] = jnp.zeros((LANES - HEAD_DIM - QAUG_ROWS, tl), BF16)
    qit_ref[0] = _bf(rt[attn_w:attn_w + idx_w])
    vt_ref[0] = _bf(rt[attn_w + idx_w:attn_w + idx_w + HEAD_DIM])
    wit_ref[0] = rt[attn_w + idx_w + HEAD_DIM:attn_w + idx_w + HEAD_DIM + IDX_HEADS]


def _split_w_in(w_in, d):
    ssm_w = 512
    sizes = (ssm_w, N_HEADS * HEAD_DIM, HEAD_DIM, HEAD_DIM, IDX_HEADS * IDX_DIM, IDX_DIM, IDX_HEADS, d, d)
    offs = [0]
    for s in sizes:
        offs.append(offs[-1] + s)
    return [w_in[:, offs[i]:offs[i + 1]] for i in range(9)]


def _inproj(x, ada3, g1, w_in):
    bsz, seq, d = x.shape
    ssm_w = 512
    attn_w = N_HEADS * HEAD_DIM
    idx_w = IDX_HEADS * IDX_DIM
    w_ssm, w_q, w_k, w_v, w_qi, w_ki, w_wi, _, _ = _split_w_in(w_in, d)
    zpad = lambda n: jnp.zeros((d, n), F32)
    wbig = jnp.concatenate([w_ssm, w_k, zpad(LANES - HEAD_DIM), w_ki, zpad(LANES - IDX_DIM)], axis=1).astype(BF16)
    wt = jnp.concatenate([w_q * (HEAD_DIM ** -0.5), w_qi * (IDX_DIM ** -0.5), w_v, w_wi,
                          zpad(LANES - HEAD_DIM - IDX_HEADS)], axis=1).T.astype(BF16)
    tl = INPROJ_ROWS
    kern = functools.partial(_inproj_kernel, d=d, ssm_w=ssm_w, attn_w=attn_w, idx_w=idx_w, tl=tl)
    row = lambda w: pl.BlockSpec((1, tl, w), lambda b, l: (b, l, 0))
    colt = lambda h: pl.BlockSpec((1, h, tl), lambda b, l: (b, 0, l))
    full = lambda a: pl.BlockSpec(a.shape, lambda b, l: (0,) * a.ndim)
    return pl.pallas_call(
        kern,
        out_shape=(jax.ShapeDtypeStruct((bsz, seq, ssm_w), F32),
                   jax.ShapeDtypeStruct((bsz, seq, HEAD_DIM), BF16),
                   jax.ShapeDtypeStruct((bsz, seq, IDX_DIM), BF16),
                   jax.ShapeDtypeStruct((bsz, N_HEADS * LANES, seq), BF16),
                   jax.ShapeDtypeStruct((bsz, idx_w, seq), BF16),
                   jax.ShapeDtypeStruct((bsz, HEAD_DIM, seq), BF16),
                   jax.ShapeDtypeStruct((bsz, IDX_HEADS, seq), F32)),
        grid=(bsz, seq // tl),
        in_specs=[row(d),
                  pl.BlockSpec((1, 1, ada3.shape[2]), lambda b, l: (b, 0, 0)),
                  pl.BlockSpec((1, d), lambda b, l: (0, 0)),
                  full(wbig), full(wt)],
        out_specs=(row(ssm_w), row(HEAD_DIM), row(IDX_DIM),
                   colt(N_HEADS * LANES), colt(idx_w), colt(HEAD_DIM), colt(IDX_HEADS)),
        compiler_params=_cparams(("arbitrary", "arbitrary")),
        name="inproj",
    )(x, ada3, g1.reshape(1, d), wbig, wt)


def _s5disc_kernel(lr_ref, li_ref, ldt_ref, br_ref, bi_ref, are_ref, aim_ref, bbr_ref, bbi_ref):
    lr = lr_ref[...]
    li = li_ref[...]
    dt = jnp.exp(ldt_ref[...])
    mag = jnp.exp(lr * dt)
    a_re = mag * jnp.cos(li * dt)
    a_im = mag * jnp.sin(li * dt)
    den = lr * lr + li * li
    n_re = a_re - 1.0
    f_re = (n_re * lr + a_im * li) / den
    f_im = (a_im * lr - n_re * li) / den
    br = br_ref[...]
    bi = bi_ref[...]
    are_ref[...] = a_re
    aim_ref[...] = a_im
    bbr_ref[...] = f_re * br - f_im * bi
    bbi_ref[...] = f_re * bi + f_im * br


def _s5disc(lam_re, lam_im, log_dt, b_re, b_im):
    g, p = lam_re.shape
    h = b_re.shape[2]
    rep = lambda a: jnp.repeat(a, h, axis=1)
    ldt = jnp.broadcast_to(log_dt[:, None], (g, p * h))
    sds = jax.ShapeDtypeStruct((g, p * h), F32)
    a_re, a_im, bb_re, bb_im = pl.pallas_call(
        _s5disc_kernel, out_shape=(sds, sds, sds, sds), name="s5disc",
    )(rep(lam_re), rep(lam_im), ldt, b_re.reshape(g, p * h), b_im.reshape(g, p * h))
    return a_re[:, ::h], a_im[:, ::h], bb_re.reshape(g, p, h), bb_im.reshape(g, p, h)


def _s5_kernel(u_ref, wb_ref, ar_ref, ai_ref, cc_ref, dsk_ref, wg_ref, bg_ref, o_ref, buf, hst, *, tl, width):
    nch = width // S5_LANE_CHUNK
    sw = S5_LANE_CHUNK // SSM_GROUP * SSM_STATE
    rows = tl * SUBLANES

    @pl.when(pl.program_id(0) == 0)
    def _():
        hst[...] = jnp.zeros_like(hst)

    u = u_ref[...].reshape(rows, width)
    ub = _bf(u)
    for j in range(nch):
        buf[:, j * 2 * sw:(j + 1) * 2 * sw] = _dot(ub[:, j * S5_LANE_CHUNK:(j + 1) * S5_LANE_CHUNK], wb_ref[j])

    for j in range(nch):
        re_cols = slice(j * 2 * sw, j * 2 * sw + sw)
        im_cols = slice(j * 2 * sw + sw, (j + 1) * 2 * sw)
        a_re = jnp.broadcast_to(ar_ref[:, j * sw:(j + 1) * sw], (SUBLANES, sw))
        a_im = jnp.broadcast_to(ai_ref[:, j * sw:(j + 1) * sw], (SUBLANES, sw))

        def step(t, carry, re_cols=re_cols, im_cols=im_cols, a_re=a_re, a_im=a_im):
            h_re, h_im = carry
            r0 = pl.multiple_of(t * SUBLANES, SUBLANES)
            n_re = (a_re * h_re - a_im * h_im) + buf[pl.ds(r0, SUBLANES), re_cols]
            n_im = (a_re * h_im + a_im * h_re) + buf[pl.ds(r0, SUBLANES), im_cols]
            buf[pl.ds(r0, SUBLANES), re_cols] = n_re
            buf[pl.ds(r0, SUBLANES), im_cols] = n_im
            return n_re, n_im

        h_re, h_im = lax.fori_loop(0, tl, step, (hst[:, re_cols], hst[:, im_cols]), unroll=8)
        hst[:, re_cols] = h_re
        hst[:, im_cols] = h_im

    ys = [_dot(_bf(buf[:, j * 2 * sw:(j + 1) * 2 * sw]), cc_ref[j]) for j in range(nch)]
    y = jnp.concatenate(ys, axis=1) + dsk_ref[...] * u
    y = jax.nn.gelu(y)
    y = y * jax.nn.sigmoid(_dot(_bf(y), wg_ref[...]) + bg_ref[...])
    o_ref[...] = _bf(y).reshape(tl, SUBLANES, width)


def _s5(u_t, a_re, a_im, bb_re, bb_im, c_re, c_im, d_skip, w_glu, b_glu):
    seq, bsz, width = u_t.shape
    assert bsz == SUBLANES
    nch = width // S5_LANE_CHUNK
    gpc = S5_LANE_CHUNK // SSM_GROUP
    sw = gpc * SSM_STATE
    eye = jnp.eye(gpc, dtype=F32)

    def bmat(bb):
        t = bb.reshape(nch, gpc, SSM_STATE, SSM_GROUP).transpose(0, 1, 3, 2)
        return jnp.einsum('jghp,gk->jghkp', t, eye).reshape(nch, S5_LANE_CHUNK, sw)

    def cmat(cc):
        t = cc.reshape(nch, gpc, SSM_GROUP, SSM_STATE).transpose(0, 1, 3, 2)
        return jnp.einsum('jgph,gk->jgpkh', t, eye).reshape(nch, sw, S5_LANE_CHUNK)

    wb = jnp.concatenate([bmat(bb_re), bmat(bb_im)], axis=2)
    cc = jnp.concatenate([cmat(c_re), -cmat(c_im)], axis=1)
    tl = S5_STEPS
    full = lambda a: pl.BlockSpec(a.shape, lambda i: (0,) * a.ndim)
    args = (u_t, wb.astype(BF16), a_re.reshape(1, -1), a_im.reshape(1, -1), cc.astype(BF16),
            d_skip.reshape(1, width), w_glu.astype(BF16), b_glu.reshape(1, width))
    return pl.pallas_call(
        functools.partial(_s5_kernel, tl=tl, width=width),
        out_shape=jax.ShapeDtypeStruct((seq, bsz, width), BF16),
        grid=(seq // tl,),
        in_specs=[pl.BlockSpec((tl, bsz, width), lambda i: (i, 0, 0))] + [full(a) for a in args[1:]],
        out_specs=pl.BlockSpec((tl, bsz, width), lambda i: (i, 0, 0)),
        scratch_shapes=[pltpu.VMEM((tl * SUBLANES, nch * 2 * sw), F32),
                        pltpu.VMEM((SUBLANES, nch * 2 * sw), F32)],
        compiler_params=_cparams(("arbitrary",)),
        name="s5",
    )(*args)


def _bit_transpose32(words):
    x = list(words)
    j, m = 16, 0x0000FFFF
    while j:
        k = 0
        while k < 32:
            t = (x[k] ^ lax.shift_right_logical(x[k + j], jnp.int32(j))) & jnp.int32(m - (1 << 32) if m >= 1 << 31 else m)
            x[k] = x[k] ^ t
            x[k + j] = x[k + j] ^ lax.shift_left(t, jnp.int32(j))
            k = (k + j + 1) & ~j
        j >>= 1
        m = (m ^ (m << j)) & 0xFFFFFFFF
    return x


def _dsa_kernel(qt_ref, qit_ref, wit_ref, ka_ref, ki_ref, vt_ref, o_ref, key_s, mb_s, acc_s, pl_s, p_s, *, tq, tk, topk, seq):
    i = pl.program_id(1)
    q0 = i * tq
    nkt = (q0 + tq + tk - 1) // tk
    ch = DSA_COUNT_ROWS
    krow = lax.broadcasted_iota(I32, (tk, tq), 0)
    qcol = q0 + lax.broadcasted_iota(I32, (tk, tq), 1)
    crow = lax.broadcasted_iota(I32, (ch, tq), 0)

    wb = wit_ref[0] * (IDX_HEADS ** -0.5)

    def score_tile(j, _):
        r0 = pl.multiple_of(j * tk, tk)
        kit = ki_ref[0, pl.ds(r0, tk), :]
        acc = jnp.zeros((tk, tq), F32)
        for h in range(IDX_HEADS):
            s = _dot(kit, qit_ref[0, h * IDX_DIM:(h + 1) * IDX_DIM, :])
            acc = acc + wb[h:h + 1, :] * jnp.maximum(s, 0.0)
        bits = lax.bitcast_convert_type(acc, I32)
        key = jnp.where(bits < 0, bits ^ jnp.int32(0x7FFFFFFF), bits)
        key = jnp.where(acc == 0.0, 0, key)
        key = jnp.where(krow + r0 <= qcol, key, INT_MIN)
        key_s[pl.ds(r0, tk), :] = key
        ukey = key ^ INT_MIN
        for c in range(tk // BITSLICE_ROWS):
            words = [ukey[c * BITSLICE_ROWS + v * SUBLANES:c * BITSLICE_ROWS + (v + 1) * SUBLANES, :]
                     for v in range(32)]
            planes = _bit_transpose32(words)
            g0 = pl.multiple_of((j * (tk // BITSLICE_ROWS) + c) * SUBLANES, SUBLANES)
            for it in range(32):
                pl_s[it, pl.ds(g0, SUBLANES), :] = planes[it]
        return 0

    @pl.when((pl.program_id(0) == 0) & (i == 0))
    def _():
        pl_s[...] = jnp.zeros(pl_s.shape, I32)

    lax.fori_loop(0, nkt, score_tile, 0)

    def count(pred):
        def tile(j, cnt):
            for c in range(tk // ch):
                rr = pl.multiple_of(j * tk + c * ch, ch)
                cnt = cnt + jnp.where(pred(key_s[pl.ds(rr, ch), :], rr), 1, 0)
            return cnt
        cnt = lax.fori_loop(0, nkt, tile, jnp.zeros((ch, tq), I32))
        return jnp.sum(cnt.astype(F32), axis=0, keepdims=True)

    ngrp = seq // 32

    def lane_count(words):
        pc = lax.population_count(words).reshape(ngrp // SUBLANES, SUBLANES, tq)
        return jnp.sum(jnp.sum(pc, axis=0).astype(F32), axis=0, keepdims=True)

    def bit_step(it, carry):
        alive, above, ans_u = carry
        ones = alive & pl_s[it]
        cnt1 = lane_count(ones)
        take = above + cnt1 >= float(topk)
        alive = jnp.where(take, ones, alive ^ ones)
        above = jnp.where(take, above, above + cnt1)
        ans_u = jnp.where(take, ans_u | lax.shift_left(jnp.int32(1), 31 - it), ans_u)
        return alive, above, ans_u

    grow = lax.broadcasted_iota(I32, (ngrp, tq), 0)
    alive0 = jnp.where(grow < nkt * (tk // 32), -1, 0)
    alive, above, ans_u = lax.fori_loop(
        0, 32, bit_step, (alive0, jnp.zeros((1, tq), F32), jnp.zeros((1, tq), I32)))
    thr = jnp.maximum(ans_u ^ INT_MIN, INT_MIN + 1)
    cnt_ge = above + lane_count(alive)
    tied = jnp.where(ans_u != 0, cnt_ge, 0.0) > float(topk)
    has_ties = jnp.max(jnp.where(tied, 1.0, 0.0)) > 0.0

    def tie_cut():
        need = float(topk) - count(lambda kb, rr: kb > thr)
        nbits = max(1, (seq - 1).bit_length())

        def idx_step(b, x):
            cand = x | lax.shift_left(jnp.int32(1), nbits - 1 - b)
            below = count(lambda kb, rr: jnp.where(kb == thr, crow + rr, seq) < cand)
            return jnp.where(below < need, cand, x)

        x = lax.fori_loop(0, nbits, idx_step, jnp.zeros((1, tq), I32))
        return jnp.where(tied, x, seq)

    cut = lax.cond(has_ties, tie_cut, lambda: jnp.full((1, tq), seq, I32))

    def bias_tile(j, _):
        for c in range(tk // ch):
            rr = pl.multiple_of(j * tk + c * ch, ch)
            kb = key_s[pl.ds(rr, ch), :]
            tie_bias = jnp.where(crow + rr <= cut, 0.0, NEG_BIG)
            mb_s[pl.ds(rr, ch), :] = jnp.where(kb > thr, 0.0, jnp.where(kb == thr, tie_bias, NEG_BIG))
        return 0

    lax.fori_loop(0, nkt, bias_tile, 0)

    def logits(j, h):
        r0 = pl.multiple_of(j * tk, tk)
        s = _dot(ka_ref[0, pl.ds(r0, tk), :], qt_ref[0, h * LANES:(h + 1) * LANES, :]) + mb_s[pl.ds(r0, tk), :]
        return s.reshape(tk // SUBLANES, SUBLANES, tq)

    acc_s[...] = jnp.zeros(acc_s.shape, F32)

    def attn_tile(j, carry):
        ms, ls = carry
        r0 = pl.multiple_of(j * tk, tk)
        new_m, new_l, alphas = [], [], []
        for h in range(N_HEADS):
            s = logits(j, h)
            m_new = jnp.maximum(ms[h], jnp.max(jnp.max(s, axis=0), axis=0, keepdims=True))
            alpha = jnp.exp(ms[h] - m_new)
            p = jnp.exp(s - m_new)
            new_m.append(m_new)
            new_l.append(alpha * ls[h] + jnp.sum(p, axis=0))
            alphas.append(alpha)
            p_s[h] = _bf(p.reshape(tk, tq))
        for h in range(N_HEADS):
            rows = slice(h * HEAD_DIM, (h + 1) * HEAD_DIM)
            acc_s[rows, :] = alphas[h] * acc_s[rows, :] + _dot(vt_ref[0, :, pl.ds(r0, tk)], p_s[h])
        return tuple(new_m), tuple(new_l)

    init = ((jnp.full((1, tq), NEG_BIG, F32),) * N_HEADS, (jnp.zeros((SUBLANES, tq), F32),) * N_HEADS)
    _, ls = lax.fori_loop(0, nkt, attn_tile, init)
    for h in range(N_HEADS):
        rows = slice(h * HEAD_DIM, (h + 1) * HEAD_DIM)
        o_ref[0, rows, :] = _bf(acc_s[rows, :] / jnp.sum(ls[h], axis=0, keepdims=True))


def _dsa(qt, qit, wit, k, ki, vt):
    bsz, seq = k.shape[0], k.shape[1]
    aw = N_HEADS * HEAD_DIM
    tq = min(DSA_Q_COLS, seq)
    tk = min(DSA_K_ROWS, seq)
    topk = min(TOPK_MAX, seq // 4)
    assert (seq - 1) // POS_SPLIT < 256 and POS_SPLIT <= 256, "key positions must split into two bf16-exact parts"
    assert all(float(np.float32(sl).astype(BF16)) == sl for sl in ALIBI_SLOPES), "ALiBi slopes must be bf16-exact"
    pos = jnp.arange(seq, dtype=I32)
    posc = jnp.stack([(pos // POS_SPLIT) * POS_SPLIT, pos % POS_SPLIT], axis=1).astype(BF16)
    ka = jnp.concatenate([k, jnp.broadcast_to(posc[None], (bsz, seq, 2)),
                          jnp.zeros((bsz, seq, LANES - HEAD_DIM - 2), BF16)], axis=2)
    kern = functools.partial(_dsa_kernel, tq=tq, tk=tk, topk=topk, seq=seq)
    cols = lambda r: pl.BlockSpec((1, r, tq), lambda b, i: (b, 0, i))
    return pl.pallas_call(
        kern,
        out_shape=jax.ShapeDtypeStruct((bsz, aw, seq), BF16),
        grid=(bsz, seq // tq),
        in_specs=[cols(N_HEADS * LANES), cols(qit.shape[1]), cols(IDX_HEADS),
                  pl.BlockSpec((1, seq, LANES), lambda b, i: (b, 0, 0)),
                  pl.BlockSpec((1, seq, IDX_DIM), lambda b, i: (b, 0, 0)),
                  pl.BlockSpec((1, HEAD_DIM, seq), lambda b, i: (b, 0, 0))],
        out_specs=cols(aw),
        scratch_shapes=[pltpu.VMEM((seq, tq), I32), pltpu.VMEM((seq, tq), F32), pltpu.VMEM((aw, tq), F32),
                        pltpu.VMEM((32, seq // 32, tq), I32), pltpu.VMEM((N_HEADS, tk, tq), BF16)],
        compiler_params=_cparams(("arbitrary", "arbitrary")),
        name="dsa",
    )(qt, qit, wit, ka, ki, vt)


def _mix_kernel(x_ref, ys_ref, ya_ref, ada_ref, g1_ref, wgt_ref, wps_ref, wpa_ref, wo_ref, g2_ref,
                h_ref, u2_ref, *, d):
    gate1 = ada_ref[0, :, 2 * d:3 * d]
    shift2 = ada_ref[0, :, 3 * d:4 * d]
    scale2 = ada_ref[0, :, 4 * d:5 * d]
    x = x_ref[0]
    u = _bf(_rms(x, g1_ref[...]) * (1.0 + ada_ref[0, :, d:2 * d]) + ada_ref[0, :, 0:d])
    g = _dot(u, wgt_ref[...])
    mixed = (jax.nn.sigmoid(g[:, 0:d]) * _dot(ys_ref[0], wps_ref[...])
             + jax.nn.sigmoid(g[:, d:2 * d]) * _dot(ya_ref[0], wpa_ref[...]))
    h = x + gate1 * _dot(_bf(mixed), wo_ref[...])
    h_ref[0] = h
    u2_ref[0] = _rms(h, g2_ref[...]) * (1.0 + scale2) + shift2


def _mix(x, ys, ya, ada3, g1, w_in, wps, wpa, wo, g2):
    bsz, seq, d = x.shape
    tm = MIX_ROWS
    row = lambda w: pl.BlockSpec((1, tm, w), lambda b, l: (b, l, 0))
    full = lambda a: pl.BlockSpec(a.shape, lambda b, l: (0,) * a.ndim)
    wps, wpa, wo = wps.astype(BF16), wpa.astype(BF16), wo.astype(BF16)
    wgt = jnp.concatenate(_split_w_in(w_in, d)[7:9], axis=1).astype(BF16)
    g1 = g1.reshape(1, d)
    g2 = g2.reshape(1, d)
    return pl.pallas_call(
        functools.partial(_mix_kernel, d=d),
        out_shape=(jax.ShapeDtypeStruct((bsz, seq, d), F32), jax.ShapeDtypeStruct((bsz, seq, d), F32)),
        grid=(bsz, seq // tm),
        in_specs=[row(d), row(ys.shape[2]), row(ya.shape[2]),
                  pl.BlockSpec((1, 1, ada3.shape[2]), lambda b, l: (b, 0, 0)),
                  full(g1), full(wgt), full(wps), full(wpa), full(wo), full(g2)],
        out_specs=(row(d), row(d)),
        compiler_params=_cparams(("arbitrary", "arbitrary")),
        name="mix",
    )(x, ys, ya, ada3, g1, wgt, wps, wpa, wo, g2)


def _first_max(cur, idx, axis, big):
    m = jnp.max(cur, axis=axis, keepdims=True)
    first = jnp.min(jnp.where(cur == m, idx, big), axis=axis, keepdims=True)
    return m, idx == first


def _route_kernel(u_ref, wrh_ref, wrl_ref, rb_ref, tri_ref, ltri_ref,
                  gt_ref, loc_ref, c8_ref, loff_ref, run0_ref, tot_ref, run_s, *, t):
    @pl.when(pl.program_id(0) == 0)
    def _():
        run_s[...] = jnp.zeros_like(run_s)

    uh, ul = _split(u_ref[...])
    logits = _dot_nt(wrh_ref[...], uh) + (_dot_nt(wrl_ref[...], uh) + _dot_nt(wrh_ref[...], ul))
    scores = jax.nn.sigmoid(logits)
    biased = scores + rb_ref[...]
    per_group = N_EXPERTS // N_GROUPS
    b3 = biased.reshape(N_GROUPS, per_group, t)
    i3 = lax.broadcasted_iota(I32, b3.shape, 1)
    m1, hit1 = _first_max(b3, i3, 1, per_group)
    m2 = jnp.max(jnp.where(hit1, -jnp.inf, b3), axis=1, keepdims=True)
    gs = (m1 + m2).reshape(N_GROUPS, t)
    gi = lax.broadcasted_iota(I32, gs.shape, 0)
    gsel = jnp.zeros(gs.shape, F32)
    for _ in range(TOPK_GROUPS):
        _, hit = _first_max(gs, gi, 0, N_GROUPS)
        gsel = jnp.where(hit, 1.0, gsel)
        gs = jnp.where(hit, -jnp.inf, gs)
    cur = jnp.where(gsel.reshape(N_GROUPS, 1, t) > 0.0, b3, -jnp.inf).reshape(N_EXPERTS, t)
    ei = lax.broadcasted_iota(I32, cur.shape, 0)
    hits = []
    gates = []
    for _ in range(TOP_K):
        _, hit = _first_max(cur, ei, 0, N_EXPERTS)
        hits.append(hit)
        gates.append(jnp.sum(jnp.where(hit, scores, 0.0), axis=0, keepdims=True))
        cur = jnp.where(hit, -jnp.inf, cur)
    gate = jnp.concatenate(gates, axis=0)
    gt_ref[...] = gate / jnp.sum(gate, axis=0, keepdims=True) * ROUTED_SCALE
    onehot = jnp.zeros(cur.shape, F32)
    for hit in hits:
        onehot = jnp.where(hit, 1.0, onehot)
    cnt = jnp.sum(onehot, axis=1, keepdims=True)
    c8 = jnp.floor((cnt + (SUBLANES - 1)) * (1.0 / SUBLANES)) * SUBLANES
    c8l = jnp.broadcast_to(c8, (N_EXPERTS, LANES))
    loff = _dot(ltri_ref[...], _bf(c8l))
    slot = _dot(_bf(onehot), tri_ref[...]) + loff[:, 0:1]
    loc_ref[...] = jnp.concatenate(
        [jnp.sum(jnp.where(hit, slot, 0.0), axis=0, keepdims=True) for hit in hits], axis=0).astype(I32)
    c8_ref[0] = c8l
    loff_ref[0] = loff
    run0_ref[0] = run_s[...]
    run_s[...] = run_s[...] + c8
    tot_ref[...] = run_s[...]


def _route(u2, w_router, router_bias):
    n, d = u2.shape
    t = min(MOE_TILE, n)
    nt = n // t
    wt = w_router.T
    wrh = wt.astype(BF16)
    wrl = (wt - wrh.astype(F32)).astype(BF16)
    tri = (jnp.arange(t)[:, None] < jnp.arange(t)[None, :]).astype(BF16)
    ex = jnp.arange(N_EXPERTS)
    ltri = (ex[None, :] < ex[:, None]).astype(BF16)
    full = lambda a: pl.BlockSpec(a.shape, lambda i: (0,) * a.ndim)
    col = pl.BlockSpec((TOP_K, t), lambda i: (0, i))
    tab = pl.BlockSpec((1, N_EXPERTS, LANES), lambda i: (i, 0, 0))
    tab_sds = jax.ShapeDtypeStruct((nt, N_EXPERTS, LANES), F32)
    rb = router_bias.reshape(N_EXPERTS, 1)
    return pl.pallas_call(
        functools.partial(_route_kernel, t=t),
        out_shape=(jax.ShapeDtypeStruct((TOP_K, n), F32), jax.ShapeDtypeStruct((TOP_K, n), I32),
                   tab_sds, tab_sds, tab_sds, jax.ShapeDtypeStruct((N_EXPERTS, LANES), F32)),
        grid=(nt,),
        in_specs=[pl.BlockSpec((t, d), lambda i: (i, 0)), full(wrh), full(wrl), full(rb), full(tri), full(ltri)],
        out_specs=(col, col, tab, tab, tab, pl.BlockSpec((N_EXPERTS, LANES), lambda i: (0, 0))),
        scratch_shapes=[pltpu.VMEM((N_EXPERTS, LANES), F32)],
        compiler_params=_cparams(("arbitrary",)),
        name="route",
    )(u2, wrh, wrl, rb, tri, ltri)


RUN_BITS = tuple(1 << b for b in reversed(range((MOE_TILE // SUBLANES).bit_length())))
RUN_LONG = 8


def _for_each_run_piece(n8_ref, src_ref, dst_ref, tile, bits, fn):
    def per_expert(e, _):
        idx = tile * N_EXPERTS + e
        n8 = n8_ref[idx]
        src = src_ref[idx]
        dst = dst_ref[idx]
        def pieces(some_bits):
            for p in some_bits:
                off = (n8 & ~(2 * p - 1)) * SUBLANES

                @pl.when((n8 & p) != 0)
                def _(p=p, off=off):
                    fn(pl.multiple_of(src + off, SUBLANES), pl.multiple_of(dst + off, SUBLANES), p * SUBLANES,
                       bits.index(p) % 2)

        long_bits = tuple(p for p in bits if p >= RUN_LONG)
        if long_bits:
            pl.when(n8 >= RUN_LONG)(lambda: pieces(long_bits))
        pieces(tuple(p for p in bits if p < RUN_LONG))
        return 0

    lax.fori_loop(0, N_EXPERTS, per_expert, 0)


def _wait_rows(n8, make_copy, max_rows):
    for p in tuple(1 << b for b in reversed(range((max_rows // SUBLANES).bit_length()))):
        @pl.when((n8 & p) != 0)
        def _(p=p):
            make_copy(p * SUBLANES).wait()


def _dispatch_kernel(n8_ref, src_ref, dst_ref, tot_ref, zn8_ref, zdst_ref, u_ref, loc_ref, gate_ref, xs_hbm,
                     lbuf, zx, sems, *, tt, nslot, dh):
    i = pl.program_id(0)
    last = pl.num_programs(0) - 1
    par = i % 2

    def run_copy(slot):
        def piece(s0, d0, rows, prio):
            pltpu.make_async_copy(
                lbuf.at[slot, pl.ds(s0, rows)], xs_hbm.at[pl.ds(d0, rows)], sems.at[slot]).start(priority=prio)
        return piece

    def wait_tile(tile, slot):
        _wait_rows(tot_ref[tile], lambda rows: pltpu.make_async_copy(
            lbuf.at[slot, pl.ds(0, rows)], xs_hbm.at[pl.ds(0, rows)], sems.at[slot]), nslot)

    @pl.when(i == 0)
    def _():
        zx[...] = jnp.zeros(zx.shape, I32)

        def zero_piece(s0, d0, rows, prio):
            cx = pltpu.make_async_copy(zx.at[pl.ds(0, rows)], xs_hbm.at[pl.ds(d0, rows)], sems.at[2])
            cx.start()
            cx.wait()

        zbits = tuple(b for b in RUN_BITS if b * SUBLANES < EXPERT_ROWS)
        _for_each_run_piece(zn8_ref, zdst_ref, zdst_ref, 0, zbits, zero_piece)

    ub = _bf(u_ref[...])
    ones = jnp.ones((tt, LANES), BF16)
    loc = loc_ref[...]
    gate = gate_ref[...]
    rows_b = lax.broadcasted_iota(I32, (SLOT_CHUNK, tt), 0).astype(F32).astype(BF16)
    loc_hi = lax.shift_right_logical(loc, SLOT_CHUNK.bit_length() - 1)
    loc_lo = (loc & (SLOT_CHUNK - 1)).astype(F32)
    gate_h = _bf(gate)
    gate_l = _bf(gate - gate_h.astype(F32))
    one_b = jnp.ones((SLOT_CHUNK, tt), BF16)
    for c in range(nslot // SLOT_CHUNK):
        perm = jnp.zeros((SLOT_CHUNK, tt), BF16)
        pgh = jnp.zeros((SLOT_CHUNK, tt), BF16)
        pgl = jnp.zeros((SLOT_CHUNK, tt), BF16)
        for k in range(TOP_K):
            lk = _bf(jnp.where(loc_hi[k:k + 1, :] == c, loc_lo[k:k + 1, :], -1.0))
            eq = rows_b == lk
            perm = jnp.where(eq, one_b, perm)
            pgh = jnp.where(eq, jnp.broadcast_to(gate_h[k:k + 1, :], (SLOT_CHUNK, tt)), pgh)
            pgl = jnp.where(eq, jnp.broadcast_to(gate_l[k:k + 1, :], (SLOT_CHUNK, tt)), pgl)
        cs = slice(c * SLOT_CHUNK, (c + 1) * SLOT_CHUNK)
        xp = lax.bitcast_convert_type(_dot(perm, ub), I32)
        lbuf[par, cs, 0:dh] = xp[:, 0:dh] | lax.shift_right_logical(xp[:, dh:2 * dh], 16)
        lbuf[par, cs, dh:dh + LANES] = lax.bitcast_convert_type(_dot(pgh, ones) + _dot(pgl, ones), I32)

    _for_each_run_piece(n8_ref, src_ref, dst_ref, i, RUN_BITS, run_copy(par))

    @pl.when(i > 0)
    def _():
        wait_tile(i - 1, 1 - par)

    @pl.when(i == last)
    def _():
        wait_tile(i, par)


def _dispatch(tabs, ztabs, u2, loc_t, gate_t, n_rows):
    n, d = u2.shape
    tt = min(MOE_TILE, n)
    nslot = TOP_K * tt + N_EXPERTS * SUBLANES
    dh = d // 2
    assert nslot % SLOT_CHUNK == 0 and tt // SUBLANES == RUN_BITS[0]
    col = pl.BlockSpec((TOP_K, tt), lambda i, *_: (0, i))
    return pl.pallas_call(
        functools.partial(_dispatch_kernel, tt=tt, nslot=nslot, dh=dh),
        out_shape=jax.ShapeDtypeStruct((n_rows, dh + LANES), I32),
        grid_spec=pltpu.PrefetchScalarGridSpec(
            num_scalar_prefetch=6, grid=(n // tt,),
            in_specs=[pl.BlockSpec((tt, d), lambda i, *_: (i, 0)), col, col],
            out_specs=pl.BlockSpec(memory_space=pl.ANY),
            scratch_shapes=[pltpu.VMEM((2, nslot, dh + LANES), I32),
                            pltpu.VMEM((EXPERT_ROWS // 2, dh + LANES), I32),
                            pltpu.SemaphoreType.DMA((3,))]),
        compiler_params=_cparams(("arbitrary",)),
        name="dispatch",
    )(*tabs, *ztabs, u2, loc_t, gate_t)


def _experts_kernel(be_ref, nu_ref, xs_ref, wg_ref, wu_ref, wd_ref, ys_ref, *, d):
    del be_ref
    dh = d // 2

    @pl.when(pl.program_id(0) < nu_ref[0])
    def _():
        w = xs_ref[:, 0:dh]
        xa = _bf(lax.bitcast_convert_type(w & jnp.int32(-65536), F32))
        xb = _bf(lax.bitcast_convert_type(lax.shift_left(w, 16), F32))
        gate = lax.bitcast_convert_type(xs_ref[:, dh:dh + LANES], F32)
        hg = _dot(xa, wg_ref[0, 0:dh, :]) + _dot(xb, wg_ref[0, dh:d, :])
        hu = _dot(xa, wu_ref[0, 0:dh, :]) + _dot(xb, wu_ref[0, dh:d, :])
        y = _dot(_bf(jax.nn.silu(hg) * hu), wd_ref[0]) * jnp.tile(gate, (1, d // LANES))
        yb = lax.bitcast_convert_type(_bf(y).astype(F32), I32)
        ys_ref[...] = yb[:, 0:dh] | lax.shift_right_logical(yb[:, dh:d], 16)


def _experts(blk_expert, n_used, xs, wg, wu, wd):
    rows, xw = xs.shape
    d = wg.shape[1]
    de = wg.shape[2]
    nblk = rows // EXPERT_ROWS
    blk = lambda i, be, nu: jnp.minimum(i, nu[0] - 1)
    return pl.pallas_call(
        functools.partial(_experts_kernel, d=d),
        out_shape=jax.ShapeDtypeStruct((rows, d // 2), I32),
        grid_spec=pltpu.PrefetchScalarGridSpec(
            num_scalar_prefetch=2, grid=(nblk,),
            in_specs=[pl.BlockSpec((EXPERT_ROWS, xw), lambda i, be, nu: (blk(i, be, nu), 0)),
                      pl.BlockSpec((1, d, de), lambda i, be, nu: (be[blk(i, be, nu)], 0, 0)),
                      pl.BlockSpec((1, d, de), lambda i, be, nu: (be[blk(i, be, nu)], 0, 0)),
                      pl.BlockSpec((1, de, d), lambda i, be, nu: (be[blk(i, be, nu)], 0, 0))],
            out_specs=pl.BlockSpec((EXPERT_ROWS, d // 2), lambda i, be, nu: (blk(i, be, nu), 0))),
        compiler_params=_cparams(("arbitrary",)),
        name="experts",
    )(blk_expert, n_used, xs, wg.astype(BF16), wu.astype(BF16), wd.astype(BF16))


def _combine_kernel(n8_ref, src_ref, dst_ref, tot_ref, ys_hbm, loc_ref, h_ref, u2_ref, ada_ref, wsg_ref, wsu_ref, wsd_ref,
                    gf_ref, o_ref, ybuf, sems, *, tt, nslot, d):
    i = pl.program_id(0)
    last = pl.num_programs(0) - 1
    par = i % 2

    def run_copy(slot):
        def piece(s0, d0, rows, prio):
            pltpu.make_async_copy(
                ys_hbm.at[pl.ds(d0, rows)], ybuf.at[slot, pl.ds(s0, rows)], sems.at[slot]).start(priority=prio)
        return piece

    @pl.when(i == 0)
    def _():
        ybuf[...] = jnp.zeros(ybuf.shape, I32)
        _for_each_run_piece(n8_ref, src_ref, dst_ref, 0, RUN_BITS, run_copy(0))

    @pl.when(i < last)
    def _():
        _for_each_run_piece(n8_ref, src_ref, dst_ref, i + 1, RUN_BITS, run_copy(1 - par))

    x = _bf(u2_ref[...])
    shared = _dot(_bf(jax.nn.silu(_dot(x, wsg_ref[...])) * _dot(x, wsu_ref[...])), wsd_ref[...])
    _wait_rows(tot_ref[i], lambda rows: pltpu.make_async_copy(
        ys_hbm.at[pl.ds(0, rows)], ybuf.at[par, pl.ds(0, rows)], sems.at[par]), nslot)

    loc = loc_ref[...]
    dh = d // 2
    routed_a = jnp.zeros((tt, dh), F32)
    routed_b = jnp.zeros((tt, dh), F32)
    cols_b = lax.broadcasted_iota(I32, (tt, SLOT_CHUNK), 1).astype(F32).astype(BF16)
    loc_hi = lax.shift_right_logical(loc, SLOT_CHUNK.bit_length() - 1)
    loc_lo = (loc & (SLOT_CHUNK - 1)).astype(F32)
    one_b = jnp.ones((tt, SLOT_CHUNK), BF16)
    for c in range(nslot // SLOT_CHUNK):
        pick = jnp.zeros((tt, SLOT_CHUNK), BF16)
        for k in range(TOP_K):
            lk = _bf(jnp.where(loc_hi[:, k:k + 1] == c, loc_lo[:, k:k + 1], -1.0))
            pick = jnp.where(cols_b == lk, one_b, pick)
        w = ybuf[par, c * SLOT_CHUNK:(c + 1) * SLOT_CHUNK, :]
        routed_a = routed_a + _dot(pick, _bf(lax.bitcast_convert_type(w & jnp.int32(-65536), F32)))
        routed_b = routed_b + _dot(pick, _bf(lax.bitcast_convert_type(lax.shift_left(w, 16), F32)))
    routed = jnp.concatenate([routed_a, routed_b], axis=1)
    gate2 = ada_ref[0, :, 5 * d:6 * d]
    h = h_ref[...] + gate2 * (routed + shared)
    o_ref[...] = _rms(h, gf_ref[...])


def _combine(tabs, ys, loc, h1, u2, ada3, wsg, wsu, wsd, gf, seq):
    n, d = h1.shape
    tt = min(MOE_TILE, n)
    nslot = TOP_K * tt + N_EXPERTS * SUBLANES
    per_b = seq // tt
    row = pl.BlockSpec((tt, d), lambda i, *_: (i, 0))
    full = lambda a: pl.BlockSpec(a.shape, lambda i, *_: (0,) * a.ndim)
    wsg, wsu, wsd = wsg.astype(BF16), wsu.astype(BF16), wsd.astype(BF16)
    gf = gf.reshape(1, d)
    return pl.pallas_call(
        functools.partial(_combine_kernel, tt=tt, nslot=nslot, d=d),
        out_shape=jax.ShapeDtypeStruct((n, d), F32),
        grid_spec=pltpu.PrefetchScalarGridSpec(
            num_scalar_prefetch=4, grid=(n // tt,),
            in_specs=[pl.BlockSpec(memory_space=pl.ANY),
                      pl.BlockSpec((tt, TOP_K), lambda i, *_: (i, 0)),
                      row, row,
                      pl.BlockSpec((1, 1, ada3.shape[2]), lambda i, *_: (i // per_b, 0, 0)),
                      full(wsg), full(wsu), full(wsd), full(gf)],
            out_specs=row,
            scratch_shapes=[pltpu.VMEM((2, nslot, d // 2), I32), pltpu.SemaphoreType.DMA((2,))]),
        compiler_params=_cparams(("arbitrary",)),
        name="combine",
    )(*tabs, ys, loc, h1, u2, ada3, wsg, wsu, wsd, gf)


def _moe(h1, u2, ada3, w_router, router_bias, wg, wu, wd, wsg, wsu, wsd, gf):
    bsz, seq, d = h1.shape
    n = bsz * seq
    assert seq % min(MOE_TILE, n) == 0
    h1f = h1.reshape(n, d)
    u2f = u2.reshape(n, d)
    gate_t, loc_t, c8, loff, run0, tot = _route(u2f, w_router, router_bias)
    nt = c8.shape[0]
    as_tab = lambda a: a[:, :, 0].astype(I32)
    tot8 = tot[:, 0].astype(I32)
    padded = (tot8 + EXPERT_ROWS - 1) // EXPERT_ROWS * EXPERT_ROWS
    pend = jnp.cumsum(padded)
    pstart = (pend - padded).astype(I32)
    nblk = (n * TOP_K + nt * N_EXPERTS * (SUBLANES - 1) + N_EXPERTS * (EXPERT_ROWS - 1) + EXPERT_ROWS - 1) // EXPERT_ROWS
    blk_row0 = jnp.arange(nblk, dtype=I32) * EXPERT_ROWS
    blk_expert = jnp.minimum(jnp.sum(pend[None, :] <= blk_row0[:, None], axis=1), N_EXPERTS - 1).astype(I32)
    n_used = (pend[-1:] // EXPERT_ROWS).astype(I32)
    n8 = as_tab(c8) // SUBLANES
    tabs = (n8.reshape(-1), as_tab(loff).reshape(-1), (pstart[None, :] + as_tab(run0)).reshape(-1),
            jnp.sum(n8, axis=1))
    ztabs = ((padded - tot8) // SUBLANES, pstart + tot8)
    xs = _dispatch(tabs, ztabs, u2f, loc_t, gate_t, nblk * EXPERT_ROWS)
    ys = _experts(blk_expert, n_used, xs, wg, wu, wd)
    out = _combine(tabs, ys, loc_t.T, h1f, u2f, ada3, wsg, wsu, wsd, gf, seq)
    return out.reshape(bsz, seq, d)


def kernel(x, c, w_ada, b_ada, norm1_g, w_in, ssm_lambda_re, ssm_lambda_im, ssm_log_dt, ssm_b_re, ssm_b_im,
           ssm_c_re, ssm_c_im, ssm_d, ssm_w_glu, ssm_b_glu, w_proj_ssm, w_proj_attn, w_out, norm2_g, w_router,
           router_bias, w_exp_gate, w_exp_up, w_exp_down, w_sh_gate, w_sh_up, w_sh_down, norm_f_g):
    depth = w_ada.shape[0]
    assert depth == 1, "the final norm is fused into the last (only) layer's combine kernel"
    bsz, seq, d = x.shape
    layer = 0
    ada3 = _ada(c, w_ada[layer], b_ada[layer]).reshape(bsz, 1, 6 * d)
    us, k, ki, qt, qit, vt, wit = _inproj(x, ada3, norm1_g[layer], w_in[layer])
    a_re, a_im, bb_re, bb_im = _s5disc(ssm_lambda_re[layer], ssm_lambda_im[layer], ssm_log_dt[layer],
                                       ssm_b_re[layer], ssm_b_im[layer])
    ys_t = _s5(us.transpose(1, 0, 2), a_re, a_im, bb_re, bb_im, ssm_c_re[layer], ssm_c_im[layer],
               ssm_d[layer], ssm_w_glu[layer], ssm_b_glu[layer])
    ya = _dsa(qt, qit, wit, k, ki, vt).transpose(0, 2, 1)
    h1, u2 = _mix(x, ys_t.transpose(1, 0, 2), ya, ada3, norm1_g[layer], w_in[layer], w_proj_ssm[layer],
                  w_proj_attn[layer], w_out[layer], norm2_g[layer])
    return _moe(h1, u2, ada3, w_router[layer], router_bias[layer], w_exp_gate[layer], w_exp_up[layer],
                w_exp_down[layer], w_sh_gate[layer], w_sh_up[layer], w_sh_down[layer], norm_f_g)
```

```python
import functools
import math

import jax
import jax.numpy as jnp
import numpy as np
from jax import lax
from jax.experimental import pallas as pl
from jax.experimental.pallas import tpu as pltpu

F32 = jnp.float32
BF16 = jnp.bfloat16
I32 = jnp.int32

SSM_GROUP = 16
SSM_STATE = 64
N_HEADS = 8
HEAD_DIM = 64
IDX_HEADS = 8
IDX_DIM = 64
TOPK_MAX = 256
N_EXPERTS = 64
TOP_K = 8
N_GROUPS = 8
TOPK_GROUPS = 4
ROUTED_SCALE = 2.5
EPS = 1e-6

V7X_VMEM_LIMIT_BYTES = 56 * 1024 * 1024
LANES = 128
SUBLANES = 8

INPROJ_ROWS = 256
S5_STEPS = 64
S5_LANE_CHUNK = 128
DSA_Q_COLS = 256
DSA_K_ROWS = 512
DSA_COUNT_ROWS = 64
BITSLICE_ROWS = 256
POS_SPLIT = 64
MIX_ROWS = 512
MOE_TILE = 256
ROUTE_TILES = 2
SLOT_CHUNK = 256
EXPERT_ROWS = 512

NEG_BIG = -1e30
INT_MIN = -(2 ** 31)


def _cparams(sem):
    return pltpu.CompilerParams(dimension_semantics=sem, vmem_limit_bytes=V7X_VMEM_LIMIT_BYTES)


def _bf(x):
    return x.astype(BF16)


def _dot(a, b):
    return jnp.dot(a, b, preferred_element_type=F32)


def _dot_nt(a, b):
    return lax.dot_general(a, b, (((1,), (1,)), ((), ())), preferred_element_type=F32)


def _split(x):
    hi = _bf(x)
    lo = _bf(x - hi.astype(F32))
    return hi, lo


def _dot3(a, b):
    ah, al = _split(a)
    bh, bl = _split(b)
    return _dot(ah, bh) + (_dot(ah, bl) + _dot(al, bh))


def _rms(x, g):
    return x * lax.rsqrt(jnp.mean(x * x, axis=-1, keepdims=True) + EPS) * g


def _ada_kernel(c_ref, w_ref, b_ref, o_ref):
    c = c_ref[...]
    o_ref[...] = _dot3(c * jax.nn.sigmoid(c), w_ref[...]) + b_ref[...]


def _ada(c, w, b):
    bsz, d = c.shape
    n = w.shape[1]
    tn = 1024
    return pl.pallas_call(
        _ada_kernel,
        out_shape=jax.ShapeDtypeStruct((bsz, n), F32),
        grid=(n // tn,),
        in_specs=[pl.BlockSpec((bsz, d), lambda j: (0, 0)),
                  pl.BlockSpec((d, tn), lambda j: (0, j)),
                  pl.BlockSpec((1, tn), lambda j: (0, j))],
        out_specs=pl.BlockSpec((bsz, tn), lambda j: (0, j)),
        compiler_params=_cparams(("arbitrary",)),
        name="ada",
    )(c, w, b.reshape(1, n))


ALIBI_SLOPES = tuple(2.0 ** (-8.0 * (h + 1) / N_HEADS) for h in range(N_HEADS))
QAUG_ROWS = 16


def _inproj_kernel(x_ref, ada_ref, g1_ref, w_ref, wt_ref,
                   us_ref, k_ref, ki_ref, qt_ref, qit_ref, vt_ref, wit_ref, *, d, ssm_w, attn_w, idx_w, tl):
    x = x_ref[0]
    shift = ada_ref[0, :, 0:d]
    scale = ada_ref[0, :, d:2 * d]
    u = _bf(_rms(x, g1_ref[...]) * (1.0 + scale) + shift)
    r = _dot(u, w_ref[...])
    us_ref[0] = r[:, 0:ssm_w]
    k_ref[0] = _bf(r[:, ssm_w:ssm_w + HEAD_DIM])
    ki_ref[0] = _bf(r[:, ssm_w + LANES:ssm_w + LANES + IDX_DIM])
    rt = _dot_nt(wt_ref[...], u)
    arow = lax.broadcasted_iota(I32, (QAUG_ROWS, tl), 0)
    for h in range(N_HEADS):
        base = h * LANES
        qt_ref[0, base:base + HEAD_DIM, :] = _bf(rt[h * HEAD_DIM:(h + 1) * HEAD_DIM])
        qt_ref[0, base + HEAD_DIM:base + HEAD_DIM + QAUG_ROWS, :] = _bf(jnp.where(arow < 2, ALIBI_SLOPES[h], 0.0))
        qt_ref[0, base + HEAD_DIM + QAUG_ROWS:base + LANES, :] = jnp.zeros((LANES - HEAD_DIM - QAUG_ROWS, tl), BF16)
    qit_ref[0] = _bf(rt[attn_w:attn_w + idx_w])
    vt_ref[0] = _bf(rt[attn_w + idx_w:attn_w + idx_w + HEAD_DIM])
    wit_ref[0] = rt[attn_w + idx_w + HEAD_DIM:attn_w + idx_w + HEAD_DIM + IDX_HEADS]


def _split_w_in(w_in, d):
    ssm_w = 512
    sizes = (ssm_w, N_HEADS * HEAD_DIM, HEAD_DIM, HEAD_DIM, IDX_HEADS * IDX_DIM, IDX_DIM, IDX_HEADS, d, d)
    offs = [0]
    for s in sizes:
        offs.append(offs[-1] + s)
    return [w_in[:, offs[i]:offs[i + 1]] for i in range(9)]


def _inproj(x, ada3, g1, w_in):
    bsz, seq, d = x.shape
    ssm_w = 512
    attn_w = N_HEADS * HEAD_DIM
    idx_w = IDX_HEADS * IDX_DIM
    w_ssm, w_q, w_k, w_v, w_qi, w_ki, w_wi, _, _ = _split_w_in(w_in, d)
    zpad = lambda n: jnp.zeros((d, n), F32)
    wbig = jnp.concatenate([w_ssm, w_k, zpad(LANES - HEAD_DIM), w_ki, zpad(LANES - IDX_DIM)], axis=1).astype(BF16)
    wt = jnp.concatenate([w_q * (HEAD_DIM ** -0.5), w_qi * (IDX_DIM ** -0.5), w_v, w_wi,
                          zpad(LANES - HEAD_DIM - IDX_HEADS)], axis=1).T.astype(BF16)
    tl = INPROJ_ROWS
    kern = functools.partial(_inproj_kernel, d=d, ssm_w=ssm_w, attn_w=attn_w, idx_w=idx_w, tl=tl)
    row = lambda w: pl.BlockSpec((1, tl, w), lambda b, l: (b, l, 0))
    colt = lambda h: pl.BlockSpec((1, h, tl), lambda b, l: (b, 0, l))
    full = lambda a: pl.BlockSpec(a.shape, lambda b, l: (0,) * a.ndim)
    return pl.pallas_call(
        kern,
        out_shape=(jax.ShapeDtypeStruct((bsz, seq, ssm_w), F32),
                   jax.ShapeDtypeStruct((bsz, seq, HEAD_DIM), BF16),
                   jax.ShapeDtypeStruct((bsz, seq, IDX_DIM), BF16),
                   jax.ShapeDtypeStruct((bsz, N_HEADS * LANES, seq), BF16),
                   jax.ShapeDtypeStruct((bsz, idx_w, seq), BF16),
                   jax.ShapeDtypeStruct((bsz, HEAD_DIM, seq), BF16),
                   jax.ShapeDtypeStruct((bsz, IDX_HEADS, seq), F32)),
        grid=(bsz, seq // tl),
        in_specs=[row(d),
                  pl.BlockSpec((1, 1, ada3.shape[2]), lambda b, l: (b, 0, 0)),
                  pl.BlockSpec((1, d), lambda b, l: (0, 0)),
                  full(wbig), full(wt)],
        out_specs=(row(ssm_w), row(HEAD_DIM), row(IDX_DIM),
                   colt(N_HEADS * LANES), colt(idx_w), colt(HEAD_DIM), colt(IDX_HEADS)),
        compiler_params=_cparams(("arbitrary", "arbitrary")),
        name="inproj",
    )(x, ada3, g1.reshape(1, d), wbig, wt)


def _s5disc_kernel(lr_ref, li_ref, ldt_ref, br_ref, bi_ref, are_ref, aim_ref, bbr_ref, bbi_ref):
    lr = lr_ref[...]
    li = li_ref[...]
    dt = jnp.exp(ldt_ref[...])
    mag = jnp.exp(lr * dt)
    a_re = mag * jnp.cos(li * dt)
    a_im = mag * jnp.sin(li * dt)
    den = lr * lr + li * li
    n_re = a_re - 1.0
    f_re = (n_re * lr + a_im * li) / den
    f_im = (a_im * lr - n_re * li) / den
    br = br_ref[...]
    bi = bi_ref[...]
    are_ref[...] = a_re
    aim_ref[...] = a_im
    bbr_ref[...] = f_re * br - f_im * bi
    bbi_ref[...] = f_re * bi + f_im * br


def _s5disc(lam_re, lam_im, log_dt, b_re, b_im):
    g, p = lam_re.shape
    h = b_re.shape[2]
    rep = lambda a: jnp.repeat(a, h, axis=1)
    ldt = jnp.broadcast_to(log_dt[:, None], (g, p * h))
    sds = jax.ShapeDtypeStruct((g, p * h), F32)
    a_re, a_im, bb_re, bb_im = pl.pallas_call(
        _s5disc_kernel, out_shape=(sds, sds, sds, sds), name="s5disc",
    )(rep(lam_re), rep(lam_im), ldt, b_re.reshape(g, p * h), b_im.reshape(g, p * h))
    return a_re[:, ::h], a_im[:, ::h], bb_re.reshape(g, p, h), bb_im.reshape(g, p, h)


def _s5_kernel(u_ref, wb_ref, ar_ref, ai_ref, cc_ref, dsk_ref, wg_ref, bg_ref, o_ref, buf, hst, *, tl, width):
    nch = width // S5_LANE_CHUNK
    sw = S5_LANE_CHUNK // SSM_GROUP * SSM_STATE
    rows = tl * SUBLANES

    @pl.when(pl.program_id(0) == 0)
    def _():
        hst[...] = jnp.zeros_like(hst)

    u = u_ref[...].reshape(rows, width)
    ub = _bf(u)
    for j in range(nch):
        buf[:, j * 2 * sw:(j + 1) * 2 * sw] = _dot(ub[:, j * S5_LANE_CHUNK:(j + 1) * S5_LANE_CHUNK], wb_ref[j])

    for j in range(nch):
        re_cols = slice(j * 2 * sw, j * 2 * sw + sw)
        im_cols = slice(j * 2 * sw + sw, (j + 1) * 2 * sw)
        a_re = jnp.broadcast_to(ar_ref[:, j * sw:(j + 1) * sw], (SUBLANES, sw))
        a_im = jnp.broadcast_to(ai_ref[:, j * sw:(j + 1) * sw], (SUBLANES, sw))

        def step(t, carry, re_cols=re_cols, im_cols=im_cols, a_re=a_re, a_im=a_im):
            h_re, h_im = carry
            r0 = pl.multiple_of(t * SUBLANES, SUBLANES)
            n_re = (a_re * h_re - a_im * h_im) + buf[pl.ds(r0, SUBLANES), re_cols]
            n_im = (a_re * h_im + a_im * h_re) + buf[pl.ds(r0, SUBLANES), im_cols]
            buf[pl.ds(r0, SUBLANES), re_cols] = n_re
            buf[pl.ds(r0, SUBLANES), im_cols] = n_im
            return n_re, n_im

        h_re, h_im = lax.fori_loop(0, tl, step, (hst[:, re_cols], hst[:, im_cols]), unroll=8)
        hst[:, re_cols] = h_re
        hst[:, im_cols] = h_im

    ys = [_dot(_bf(buf[:, j * 2 * sw:(j + 1) * 2 * sw]), cc_ref[j]) for j in range(nch)]
    y = jnp.concatenate(ys, axis=1) + dsk_ref[...] * u
    y = jax.nn.gelu(y)
    y = y * jax.nn.sigmoid(_dot(_bf(y), wg_ref[...]) + bg_ref[...])
    o_ref[...] = _bf(y).reshape(tl, SUBLANES, width)


def _s5(u_t, a_re, a_im, bb_re, bb_im, c_re, c_im, d_skip, w_glu, b_glu):
    seq, bsz, width = u_t.shape
    assert bsz == SUBLANES
    nch = width // S5_LANE_CHUNK
    gpc = S5_LANE_CHUNK // SSM_GROUP
    sw = gpc * SSM_STATE
    eye = jnp.eye(gpc, dtype=F32)

    def bmat(bb):
        t = bb.reshape(nch, gpc, SSM_STATE, SSM_GROUP).transpose(0, 1, 3, 2)
        return jnp.einsum('jghp,gk->jghkp', t, eye).reshape(nch, S5_LANE_CHUNK, sw)

    def cmat(cc):
        t = cc.reshape(nch, gpc, SSM_GROUP, SSM_STATE).transpose(0, 1, 3, 2)
        return jnp.einsum('jgph,gk->jgpkh', t, eye).reshape(nch, sw, S5_LANE_CHUNK)

    wb = jnp.concatenate([bmat(bb_re), bmat(bb_im)], axis=2)
    cc = jnp.concatenate([cmat(c_re), -cmat(c_im)], axis=1)
    tl = S5_STEPS
    full = lambda a: pl.BlockSpec(a.shape, lambda i: (0,) * a.ndim)
    args = (u_t, wb.astype(BF16), a_re.reshape(1, -1), a_im.reshape(1, -1), cc.astype(BF16),
            d_skip.reshape(1, width), w_glu.astype(BF16), b_glu.reshape(1, width))
    return pl.pallas_call(
        functools.partial(_s5_kernel, tl=tl, width=width),
        out_shape=jax.ShapeDtypeStruct((seq, bsz, width), BF16),
        grid=(seq // tl,),
        in_specs=[pl.BlockSpec((tl, bsz, width), lambda i: (i, 0, 0))] + [full(a) for a in args[1:]],
        out_specs=pl.BlockSpec((tl, bsz, width), lambda i: (i, 0, 0)),
        scratch_shapes=[pltpu.VMEM((tl * SUBLANES, nch * 2 * sw), F32),
                        pltpu.VMEM((SUBLANES, nch * 2 * sw), F32)],
        compiler_params=_cparams(("arbitrary",)),
        name="s5",
    )(*args)


def _bit_transpose32(words):
    x = list(words)
    j, m = 16, 0x0000FFFF
    while j:
        k = 0
        while k < 32:
            t = (x[k] ^ lax.shift_right_logical(x[k + j], jnp.int32(j))) & jnp.int32(m - (1 << 32) if m >= 1 << 31 else m)
            x[k] = x[k] ^ t
            x[k + j] = x[k + j] ^ lax.shift_left(t, jnp.int32(j))
            k = (k + j + 1) & ~j
        j >>= 1
        m = (m ^ (m << j)) & 0xFFFFFFFF
    return x


def _dsa_kernel(qt_ref, qit_ref, wit_ref, ka_ref, ki_ref, vt_ref, o_ref, key_s, mb_s, acc_s, pl_s, p_s, *, tq, tk, topk, seq):
    i = pl.program_id(1)
    q0 = i * tq
    nkt = (q0 + tq + tk - 1) // tk
    ch = DSA_COUNT_ROWS
    krow = lax.broadcasted_iota(I32, (tk, tq), 0)
    qcol = q0 + lax.broadcasted_iota(I32, (tk, tq), 1)
    crow = lax.broadcasted_iota(I32, (ch, tq), 0)

    wb = wit_ref[0] * (IDX_HEADS ** -0.5)

    def score_tile(j, _):
        r0 = pl.multiple_of(j * tk, tk)
        kit = ki_ref[0, pl.ds(r0, tk), :]
        acc = jnp.zeros((tk, tq), F32)
        for h in range(IDX_HEADS):
            s = _dot(kit, qit_ref[0, h * IDX_DIM:(h + 1) * IDX_DIM, :])
            acc = acc + wb[h:h + 1, :] * jnp.maximum(s, 0.0)
        bits = lax.bitcast_convert_type(acc, I32)
        key = jnp.where(bits < 0, bits ^ jnp.int32(0x7FFFFFFF), bits)
        key = jnp.where(acc == 0.0, 0, key)
        key = jnp.where(krow + r0 <= qcol, key, INT_MIN)
        key_s[pl.ds(r0, tk), :] = key
        ukey = key ^ INT_MIN
        for c in range(tk // BITSLICE_ROWS):
            words = [ukey[c * BITSLICE_ROWS + v * SUBLANES:c * BITSLICE_ROWS + (v + 1) * SUBLANES, :]
                     for v in range(32)]
            planes = _bit_transpose32(words)
            g0 = pl.multiple_of((j * (tk // BITSLICE_ROWS) + c) * SUBLANES, SUBLANES)
            for it in range(32):
                pl_s[it, pl.ds(g0, SUBLANES), :] = planes[it]
        return 0

    @pl.when((pl.program_id(0) == 0) & (i == 0))
    def _():
        pl_s[...] = jnp.zeros(pl_s.shape, I32)

    lax.fori_loop(0, nkt, score_tile, 0)

    def count(pred):
        def tile(j, cnt):
            for c in range(tk // ch):
                rr = pl.multiple_of(j * tk + c * ch, ch)
                cnt = cnt + jnp.where(pred(key_s[pl.ds(rr, ch), :], rr), 1, 0)
            return cnt
        cnt = lax.fori_loop(0, nkt, tile, jnp.zeros((ch, tq), I32))
        return jnp.sum(cnt.astype(F32), axis=0, keepdims=True)

    ngrp = seq // 32

    def lane_count(words):
        pc = lax.population_count(words).reshape(ngrp // SUBLANES, SUBLANES, tq)
        return jnp.sum(jnp.sum(pc, axis=0).astype(F32), axis=0, keepdims=True)

    def bit_step(it, carry):
        alive, above, ans_u = carry
        ones = alive & pl_s[it]
        cnt1 = lane_count(ones)
        take = above + cnt1 >= float(topk)
        alive = jnp.where(take, ones, alive ^ ones)
        above = jnp.where(take, above, above + cnt1)
        ans_u = jnp.where(take, ans_u | lax.shift_left(jnp.int32(1), 31 - it), ans_u)
        return alive, above, ans_u

    grow = lax.broadcasted_iota(I32, (ngrp, tq), 0)
    alive0 = jnp.where(grow < nkt * (tk // 32), -1, 0)
    alive, above, ans_u = lax.fori_loop(
        0, 32, bit_step, (alive0, jnp.zeros((1, tq), F32), jnp.zeros((1, tq), I32)))
    thr = jnp.maximum(ans_u ^ INT_MIN, INT_MIN + 1)
    cnt_ge = above + lane_count(alive)
    tied = jnp.where(ans_u != 0, cnt_ge, 0.0) > float(topk)
    has_ties = jnp.max(jnp.where(tied, 1.0, 0.0)) > 0.0

    def tie_cut():
        need = float(topk) - count(lambda kb, rr: kb > thr)
        nbits = max(1, (seq - 1).bit_length())

        def idx_step(b, x):
            cand = x | lax.shift_left(jnp.int32(1), nbits - 1 - b)
            below = count(lambda kb, rr: jnp.where(kb == thr, crow + rr, seq) < cand)
            return jnp.where(below < need, cand, x)

        x = lax.fori_loop(0, nbits, idx_step, jnp.zeros((1, tq), I32))
        return jnp.where(tied, x, seq)

    cut = lax.cond(has_ties, tie_cut, lambda: jnp.full((1, tq), seq, I32))

    def bias_tile(j, _):
        for c in range(tk // ch):
            rr = pl.multiple_of(j * tk + c * ch, ch)
            kb = key_s[pl.ds(rr, ch), :]
            tie_bias = jnp.where(crow + rr <= cut, 0.0, NEG_BIG)
            mb_s[pl.ds(rr, ch), :] = jnp.where(kb > thr, 0.0, jnp.where(kb == thr, tie_bias, NEG_BIG))
        return 0

    lax.fori_loop(0, nkt, bias_tile, 0)

    def logits(j, h):
        r0 = pl.multiple_of(j * tk, tk)
        s = _dot(ka_ref[0, pl.ds(r0, tk), :], qt_ref[0, h * LANES:(h + 1) * LANES, :]) + mb_s[pl.ds(r0, tk), :]
        return s.reshape(tk // SUBLANES, SUBLANES, tq)

    acc_s[...] = jnp.zeros(acc_s.shape, F32)

    def attn_tile(j, carry):
        ms, ls = carry
        r0 = pl.multiple_of(j * tk, tk)
        new_m, new_l, alphas = [], [], []
        for h in range(N_HEADS):
            s = logits(j, h)
            m_new = jnp.maximum(ms[h], jnp.max(jnp.max(s, axis=0), axis=0, keepdims=True))
            alpha = jnp.exp(ms[h] - m_new)
            p = jnp.exp(s - m_new)
            new_m.append(m_new)
            new_l.append(alpha * ls[h] + jnp.sum(p, axis=0))
            alphas.append(alpha)
            p_s[h] = _bf(p.reshape(tk, tq))
        for h in range(N_HEADS):
            rows = slice(h * HEAD_DIM, (h + 1) * HEAD_DIM)
            acc_s[rows, :] = alphas[h] * acc_s[rows, :] + _dot(vt_ref[0, :, pl.ds(r0, tk)], p_s[h])
        return tuple(new_m), tuple(new_l)

    init = ((jnp.full((1, tq), NEG_BIG, F32),) * N_HEADS, (jnp.zeros((SUBLANES, tq), F32),) * N_HEADS)
    _, ls = lax.fori_loop(0, nkt, attn_tile, init)
    for h in range(N_HEADS):
        rows = slice(h * HEAD_DIM, (h + 1) * HEAD_DIM)
        acc_s[rows, :] = acc_s[rows, :] / jnp.sum(ls[h], axis=0, keepdims=True)
    o_ref[0] = _bf(acc_s[...].T)


def _dsa(qt, qit, wit, k, ki, vt):
    bsz, seq = k.shape[0], k.shape[1]
    aw = N_HEADS * HEAD_DIM
    tq = min(DSA_Q_COLS, seq)
    tk = min(DSA_K_ROWS, seq)
    topk = min(TOPK_MAX, seq // 4)
    assert (seq - 1) // POS_SPLIT < 256 and POS_SPLIT <= 256, "key positions must split into two bf16-exact parts"
    assert all(float(np.float32(sl).astype(BF16)) == sl for sl in ALIBI_SLOPES), "ALiBi slopes must be bf16-exact"
    pos = jnp.arange(seq, dtype=I32)
    posc = jnp.stack([(pos // POS_SPLIT) * POS_SPLIT, pos % POS_SPLIT], axis=1).astype(BF16)
    ka = jnp.concatenate([k, jnp.broadcast_to(posc[None], (bsz, seq, 2)),
                          jnp.zeros((bsz, seq, LANES - HEAD_DIM - 2), BF16)], axis=2)
    kern = functools.partial(_dsa_kernel, tq=tq, tk=tk, topk=topk, seq=seq)
    cols = lambda r: pl.BlockSpec((1, r, tq), lambda b, i: (b, 0, i))
    return pl.pallas_call(
        kern,
        out_shape=jax.ShapeDtypeStruct((bsz, seq, aw), BF16),
        grid=(bsz, seq // tq),
        in_specs=[cols(N_HEADS * LANES), cols(qit.shape[1]), cols(IDX_HEADS),
                  pl.BlockSpec((1, seq, LANES), lambda b, i: (b, 0, 0)),
                  pl.BlockSpec((1, seq, IDX_DIM), lambda b, i: (b, 0, 0)),
                  pl.BlockSpec((1, HEAD_DIM, seq), lambda b, i: (b, 0, 0))],
        out_specs=pl.BlockSpec((1, tq, aw), lambda b, i: (b, i, 0)),
        scratch_shapes=[pltpu.VMEM((seq, tq), I32), pltpu.VMEM((seq, tq), F32), pltpu.VMEM((aw, tq), F32),
                        pltpu.VMEM((32, seq // 32, tq), I32), pltpu.VMEM((N_HEADS, tk, tq), BF16)],
        compiler_params=_cparams(("arbitrary", "arbitrary")),
        name="dsa",
    )(qt, qit, wit, ka, ki, vt)


def _mix_kernel(x_ref, ys_ref, ya_ref, ada_ref, g1_ref, wgt_ref, wps_ref, wpa_ref, wo_ref, g2_ref,
                h_ref, u2_ref, *, d):
    gate1 = ada_ref[0, :, 2 * d:3 * d]
    shift2 = ada_ref[0, :, 3 * d:4 * d]
    scale2 = ada_ref[0, :, 4 * d:5 * d]
    x = x_ref[0]
    u = _bf(_rms(x, g1_ref[...]) * (1.0 + ada_ref[0, :, d:2 * d]) + ada_ref[0, :, 0:d])
    g = _dot(u, wgt_ref[...])
    mixed = (jax.nn.sigmoid(g[:, 0:d]) * _dot(ys_ref[0], wps_ref[...])
             + jax.nn.sigmoid(g[:, d:2 * d]) * _dot(ya_ref[0], wpa_ref[...]))
    h = x + gate1 * _dot(_bf(mixed), wo_ref[...])
    h_ref[0] = h
    u2_ref[0] = _rms(h, g2_ref[...]) * (1.0 + scale2) + shift2


def _mix(x, ys, ya, ada3, g1, w_in, wps, wpa, wo, g2):
    bsz, seq, d = x.shape
    tm = MIX_ROWS
    row = lambda w: pl.BlockSpec((1, tm, w), lambda b, l: (b, l, 0))
    full = lambda a: pl.BlockSpec(a.shape, lambda b, l: (0,) * a.ndim)
    wps, wpa, wo = wps.astype(BF16), wpa.astype(BF16), wo.astype(BF16)
    wgt = jnp.concatenate(_split_w_in(w_in, d)[7:9], axis=1).astype(BF16)
    g1 = g1.reshape(1, d)
    g2 = g2.reshape(1, d)
    return pl.pallas_call(
        functools.partial(_mix_kernel, d=d),
        out_shape=(jax.ShapeDtypeStruct((bsz, seq, d), F32), jax.ShapeDtypeStruct((bsz, seq, d), F32)),
        grid=(bsz, seq // tm),
        in_specs=[row(d), row(ys.shape[2]), row(ya.shape[2]),
                  pl.BlockSpec((1, 1, ada3.shape[2]), lambda b, l: (b, 0, 0)),
                  full(g1), full(wgt), full(wps), full(wpa), full(wo), full(g2)],
        out_specs=(row(d), row(d)),
        compiler_params=_cparams(("arbitrary", "arbitrary")),
        name="mix",
    )(x, ys, ya, ada3, g1, wgt, wps, wpa, wo, g2)


def _first_max(cur, idx, axis, big):
    m = jnp.max(cur, axis=axis, keepdims=True)
    first = jnp.min(jnp.where(cur == m, idx, big), axis=axis, keepdims=True)
    return m, idx == first


def _route_kernel(u_ref, wrh_ref, wrl_ref, rb_ref, tri_ref, ltri_ref,
                  gt_ref, loc_ref, c8_ref, loff_ref, run0_ref, tot_ref, run_s, *, t, tt):
    @pl.when(pl.program_id(0) == 0)
    def _():
        run_s[...] = jnp.zeros_like(run_s)

    uh, ul = _split(u_ref[...])
    logits = _dot_nt(wrh_ref[...], uh) + (_dot_nt(wrl_ref[...], uh) + _dot_nt(wrh_ref[...], ul))
    scores = jax.nn.sigmoid(logits)
    biased = scores + rb_ref[...]
    per_group = N_EXPERTS // N_GROUPS
    b3 = biased.reshape(N_GROUPS, per_group, t)
    i3 = lax.broadcasted_iota(I32, b3.shape, 1)
    m1, hit1 = _first_max(b3, i3, 1, per_group)
    m2 = jnp.max(jnp.where(hit1, -jnp.inf, b3), axis=1, keepdims=True)
    gs = (m1 + m2).reshape(N_GROUPS, t)
    gi = lax.broadcasted_iota(I32, gs.shape, 0)
    gsel = jnp.zeros(gs.shape, F32)
    for _ in range(TOPK_GROUPS):
        _, hit = _first_max(gs, gi, 0, N_GROUPS)
        gsel = jnp.where(hit, 1.0, gsel)
        gs = jnp.where(hit, -jnp.inf, gs)
    cur = jnp.where(gsel.reshape(N_GROUPS, 1, t) > 0.0, b3, -jnp.inf).reshape(N_EXPERTS, t)
    ei = lax.broadcasted_iota(I32, cur.shape, 0)
    hits = []
    gates = []
    for _ in range(TOP_K):
        _, hit = _first_max(cur, ei, 0, N_EXPERTS)
        hits.append(hit)
        gates.append(jnp.sum(jnp.where(hit, scores, 0.0), axis=0, keepdims=True))
        cur = jnp.where(hit, -jnp.inf, cur)
    gate = jnp.concatenate(gates, axis=0)
    gt_ref[...] = gate / jnp.sum(gate, axis=0, keepdims=True) * ROUTED_SCALE
    onehot = jnp.zeros(cur.shape, F32)
    for hit in hits:
        onehot = jnp.where(hit, 1.0, onehot)
    for sub in range(t // tt):
        cols = slice(sub * tt, (sub + 1) * tt)
        oh = onehot[:, cols]
        cnt = jnp.sum(oh, axis=1, keepdims=True)
        c8 = jnp.floor((cnt + (SUBLANES - 1)) * (1.0 / SUBLANES)) * SUBLANES
        c8l = jnp.broadcast_to(c8, (N_EXPERTS, LANES))
        loff = _dot(ltri_ref[...], _bf(c8l))
        slot = _dot(_bf(oh), tri_ref[...]) + loff[:, 0:1]
        loc_ref[:, cols] = jnp.concatenate(
            [jnp.sum(jnp.where(hit[:, cols], slot, 0.0), axis=0, keepdims=True) for hit in hits],
            axis=0).astype(I32)
        c8_ref[sub] = c8l
        loff_ref[sub] = loff
        run0_ref[sub] = run_s[...]
        run_s[...] = run_s[...] + c8
    tot_ref[...] = run_s[...]


def _route(u2, w_router, router_bias):
    n, d = u2.shape
    tt = min(MOE_TILE, n)
    t = min(ROUTE_TILES * tt, n)
    nt = n // tt
    wt = w_router.T
    wrh = wt.astype(BF16)
    wrl = (wt - wrh.astype(F32)).astype(BF16)
    tri = (jnp.arange(tt)[:, None] < jnp.arange(tt)[None, :]).astype(BF16)
    ex = jnp.arange(N_EXPERTS)
    ltri = (ex[None, :] < ex[:, None]).astype(BF16)
    full = lambda a: pl.BlockSpec(a.shape, lambda i: (0,) * a.ndim)
    col = pl.BlockSpec((TOP_K, t), lambda i: (0, i))
    tab = pl.BlockSpec((t // tt, N_EXPERTS, LANES), lambda i: (i, 0, 0))
    tab_sds = jax.ShapeDtypeStruct((nt, N_EXPERTS, LANES), F32)
    rb = router_bias.reshape(N_EXPERTS, 1)
    return pl.pallas_call(
        functools.partial(_route_kernel, t=t, tt=tt),
        out_shape=(jax.ShapeDtypeStruct((TOP_K, n), F32), jax.ShapeDtypeStruct((TOP_K, n), I32),
                   tab_sds, tab_sds, tab_sds, jax.ShapeDtypeStruct((N_EXPERTS, LANES), F32)),
        grid=(n // t,),
        in_specs=[pl.BlockSpec((t, d), lambda i: (i, 0)), full(wrh), full(wrl), full(rb), full(tri), full(ltri)],
        out_specs=(col, col, tab, tab, tab, pl.BlockSpec((N_EXPERTS, LANES), lambda i: (0, 0))),
        scratch_shapes=[pltpu.VMEM((N_EXPERTS, LANES), F32)],
        compiler_params=_cparams(("arbitrary",)),
        name="route",
    )(u2, wrh, wrl, rb, tri, ltri)


RUN_BITS = tuple(1 << b for b in reversed(range((MOE_TILE // SUBLANES).bit_length())))
RUN_LONG = 8


def _for_each_run_piece(n8_ref, src_ref, dst_ref, tile, bits, fn):
    def per_expert(e, _):
        idx = tile * N_EXPERTS + e
        n8 = n8_ref[idx]
        src = src_ref[idx]
        dst = dst_ref[idx]
        def pieces(some_bits):
            for p in some_bits:
                off = (n8 & ~(2 * p - 1)) * SUBLANES

                @pl.when((n8 & p) != 0)
                def _(p=p, off=off):
                    fn(pl.multiple_of(src + off, SUBLANES), pl.multiple_of(dst + off, SUBLANES), p * SUBLANES,
                       bits.index(p) % 2)

        long_bits = tuple(p for p in bits if p >= RUN_LONG)
        if long_bits:
            pl.when(n8 >= RUN_LONG)(lambda: pieces(long_bits))
        pieces(tuple(p for p in bits if p < RUN_LONG))
        return 0

    lax.fori_loop(0, N_EXPERTS, per_expert, 0)


def _wait_rows(n8, make_copy, max_rows):
    for p in tuple(1 << b for b in reversed(range((max_rows // SUBLANES).bit_length()))):
        @pl.when((n8 & p) != 0)
        def _(p=p):
            make_copy(p * SUBLANES).wait()


def _dispatch_kernel(n8_ref, src_ref, dst_ref, tot_ref, zn8_ref, zdst_ref, u_ref, loc_ref, gate_ref, xs_hbm,
                     lbuf, zx, sems, *, tt, nslot, dh):
    i = pl.program_id(0)
    last = pl.num_programs(0) - 1
    par = i % 2

    def run_copy(slot):
        def piece(s0, d0, rows, prio):
            pltpu.make_async_copy(
                lbuf.at[slot, pl.ds(s0, rows)], xs_hbm.at[pl.ds(d0, rows)], sems.at[slot]).start(priority=prio)
        return piece

    def wait_tile(tile, slot):
        _wait_rows(tot_ref[tile], lambda rows: pltpu.make_async_copy(
            lbuf.at[slot, pl.ds(0, rows)], xs_hbm.at[pl.ds(0, rows)], sems.at[slot]), nslot)

    @pl.when(i == 0)
    def _():
        zx[...] = jnp.zeros(zx.shape, I32)

        def zero_piece(s0, d0, rows, prio):
            cx = pltpu.make_async_copy(zx.at[pl.ds(0, rows)], xs_hbm.at[pl.ds(d0, rows)], sems.at[2])
            cx.start()
            cx.wait()

        zbits = tuple(b for b in RUN_BITS if b * SUBLANES < EXPERT_ROWS)
        _for_each_run_piece(zn8_ref, zdst_ref, zdst_ref, 0, zbits, zero_piece)

    ub = _bf(u_ref[...])
    ones = jnp.ones((tt, LANES), BF16)
    loc = loc_ref[...]
    gate = gate_ref[...]
    rows_b = lax.broadcasted_iota(I32, (SLOT_CHUNK, tt), 0).astype(F32).astype(BF16)
    loc_hi = lax.shift_right_logical(loc, SLOT_CHUNK.bit_length() - 1)
    loc_lo = (loc & (SLOT_CHUNK - 1)).astype(F32)
    gate_h = _bf(gate)
    gate_l = _bf(gate - gate_h.astype(F32))
    one_b = jnp.ones((SLOT_CHUNK, tt), BF16)
    for c in range(nslot // SLOT_CHUNK):
        perm = jnp.zeros((SLOT_CHUNK, tt), BF16)
        pgh = jnp.zeros((SLOT_CHUNK, tt), BF16)
        pgl = jnp.zeros((SLOT_CHUNK, tt), BF16)
        for k in range(TOP_K):
            lk = _bf(jnp.where(loc_hi[k:k + 1, :] == c, loc_lo[k:k + 1, :], -1.0))
            eq = rows_b == lk
            perm = jnp.where(eq, one_b, perm)
            pgh = jnp.where(eq, jnp.broadcast_to(gate_h[k:k + 1, :], (SLOT_CHUNK, tt)), pgh)
            pgl = jnp.where(eq, jnp.broadcast_to(gate_l[k:k + 1, :], (SLOT_CHUNK, tt)), pgl)
        cs = slice(c * SLOT_CHUNK, (c + 1) * SLOT_CHUNK)
        xp = lax.bitcast_convert_type(_dot(perm, ub), I32)
        lbuf[par, cs, 0:dh] = xp[:, 0:dh] | lax.shift_right_logical(xp[:, dh:2 * dh], 16)
        lbuf[par, cs, dh:dh + LANES] = lax.bitcast_convert_type(_dot(pgh, ones) + _dot(pgl, ones), I32)

    _for_each_run_piece(n8_ref, src_ref, dst_ref, i, RUN_BITS, run_copy(par))

    @pl.when(i > 0)
    def _():
        wait_tile(i - 1, 1 - par)

    @pl.when(i == last)
    def _():
        wait_tile(i, par)


def _dispatch(tabs, ztabs, u2, loc_t, gate_t, n_rows):
    n, d = u2.shape
    tt = min(MOE_TILE, n)
    nslot = TOP_K * tt + N_EXPERTS * SUBLANES
    dh = d // 2
    assert nslot % SLOT_CHUNK == 0 and tt // SUBLANES == RUN_BITS[0]
    col = pl.BlockSpec((TOP_K, tt), lambda i, *_: (0, i))
    return pl.pallas_call(
        functools.partial(_dispatch_kernel, tt=tt, nslot=nslot, dh=dh),
        out_shape=jax.ShapeDtypeStruct((n_rows, dh + LANES), I32),
        grid_spec=pltpu.PrefetchScalarGridSpec(
            num_scalar_prefetch=6, grid=(n // tt,),
            in_specs=[pl.BlockSpec((tt, d), lambda i, *_: (i, 0)), col, col],
            out_specs=pl.BlockSpec(memory_space=pl.ANY),
            scratch_shapes=[pltpu.VMEM((2, nslot, dh + LANES), I32),
                            pltpu.VMEM((EXPERT_ROWS // 2, dh + LANES), I32),
                            pltpu.SemaphoreType.DMA((3,))]),
        compiler_params=_cparams(("arbitrary",)),
        name="dispatch",
    )(*tabs, *ztabs, u2, loc_t, gate_t)


def _experts_kernel(be_ref, nu_ref, xs_ref, wg_ref, wu_ref, wd_ref, ys_ref, *, d):
    del be_ref
    dh = d // 2

    @pl.when(pl.program_id(0) < nu_ref[0])
    def _():
        w = xs_ref[:, 0:dh]
        xa = _bf(lax.bitcast_convert_type(w & jnp.int32(-65536), F32))
        xb = _bf(lax.bitcast_convert_type(lax.shift_left(w, 16), F32))
        gate = lax.bitcast_convert_type(xs_ref[:, dh:dh + LANES], F32)
        hg = _dot(xa, wg_ref[0, 0:dh, :]) + _dot(xb, wg_ref[0, dh:d, :])
        hu = _dot(xa, wu_ref[0, 0:dh, :]) + _dot(xb, wu_ref[0, dh:d, :])
        y = _dot(_bf(jax.nn.silu(hg) * hu), wd_ref[0]) * jnp.tile(gate, (1, d // LANES))
        yb = lax.bitcast_convert_type(_bf(y).astype(F32), I32)
        ys_ref[...] = yb[:, 0:dh] | lax.shift_right_logical(yb[:, dh:d], 16)


def _experts(blk_expert, n_used, xs, wg, wu, wd):
    rows, xw = xs.shape
    d = wg.shape[1]
    de = wg.shape[2]
    nblk = rows // EXPERT_ROWS
    blk = lambda i, be, nu: jnp.minimum(i, nu[0] - 1)
    return pl.pallas_call(
        functools.partial(_experts_kernel, d=d),
        out_shape=jax.ShapeDtypeStruct((rows, d // 2), I32),
        grid_spec=pltpu.PrefetchScalarGridSpec(
            num_scalar_prefetch=2, grid=(nblk,),
            in_specs=[pl.BlockSpec((EXPERT_ROWS, xw), lambda i, be, nu: (blk(i, be, nu), 0)),
                      pl.BlockSpec((1, d, de), lambda i, be, nu: (be[blk(i, be, nu)], 0, 0)),
                      pl.BlockSpec((1, d, de), lambda i, be, nu: (be[blk(i, be, nu)], 0, 0)),
                      pl.BlockSpec((1, de, d), lambda i, be, nu: (be[blk(i, be, nu)], 0, 0))],
            out_specs=pl.BlockSpec((EXPERT_ROWS, d // 2), lambda i, be, nu: (blk(i, be, nu), 0))),
        compiler_params=_cparams(("arbitrary",)),
        name="experts",
    )(blk_expert, n_used, xs, wg.astype(BF16), wu.astype(BF16), wd.astype(BF16))


def _combine_kernel(n8_ref, src_ref, dst_ref, tot_ref, ys_hbm, loc_ref, h_ref, u2_ref, ada_ref, wsg_ref, wsu_ref, wsd_ref,
                    gf_ref, o_ref, ybuf, sems, *, tt, nslot, d):
    i = pl.program_id(0)
    last = pl.num_programs(0) - 1
    par = i % 2

    def run_copy(slot):
        def piece(s0, d0, rows, prio):
            pltpu.make_async_copy(
                ys_hbm.at[pl.ds(d0, rows)], ybuf.at[slot, pl.ds(s0, rows)], sems.at[slot]).start(priority=prio)
        return piece

    @pl.when(i == 0)
    def _():
        ybuf[...] = jnp.zeros(ybuf.shape, I32)
        _for_each_run_piece(n8_ref, src_ref, dst_ref, 0, RUN_BITS, run_copy(0))

    @pl.when(i < last)
    def _():
        _for_each_run_piece(n8_ref, src_ref, dst_ref, i + 1, RUN_BITS, run_copy(1 - par))

    x = _bf(u2_ref[...])
    shared = _dot(_bf(jax.nn.silu(_dot(x, wsg_ref[...])) * _dot(x, wsu_ref[...])), wsd_ref[...])
    _wait_rows(tot_ref[i], lambda rows: pltpu.make_async_copy(
        ys_hbm.at[pl.ds(0, rows)], ybuf.at[par, pl.ds(0, rows)], sems.at[par]), nslot)

    loc = loc_ref[...]
    dh = d // 2
    routed_a = jnp.zeros((tt, dh), F32)
    routed_b = jnp.zeros((tt, dh), F32)
    cols_b = lax.broadcasted_iota(I32, (tt, SLOT_CHUNK), 1).astype(F32).astype(BF16)
    loc_hi = lax.shift_right_logical(loc, SLOT_CHUNK.bit_length() - 1)
    loc_lo = (loc & (SLOT_CHUNK - 1)).astype(F32)
    one_b = jnp.ones((tt, SLOT_CHUNK), BF16)
    for c in range(nslot // SLOT_CHUNK):
        pick = jnp.zeros((tt, SLOT_CHUNK), BF16)
        for k in range(TOP_K):
            lk = _bf(jnp.where(loc_hi[:, k:k + 1] == c, loc_lo[:, k:k + 1], -1.0))
            pick = jnp.where(cols_b == lk, one_b, pick)
        w = ybuf[par, c * SLOT_CHUNK:(c + 1) * SLOT_CHUNK, :]
        routed_a = routed_a + _dot(pick, _bf(lax.bitcast_convert_type(w & jnp.int32(-65536), F32)))
        routed_b = routed_b + _dot(pick, _bf(lax.bitcast_convert_type(lax.shift_left(w, 16), F32)))
    routed = jnp.concatenate([routed_a, routed_b], axis=1)
    gate2 = ada_ref[0, :, 5 * d:6 * d]
    h = h_ref[...] + gate2 * (routed + shared)
    o_ref[...] = _rms(h, gf_ref[...])


def _combine(tabs, ys, loc, h1, u2, ada3, wsg, wsu, wsd, gf, seq):
    n, d = h1.shape
    tt = min(MOE_TILE, n)
    nslot = TOP_K * tt + N_EXPERTS * SUBLANES
    per_b = seq // tt
    row = pl.BlockSpec((tt, d), lambda i, *_: (i, 0))
    full = lambda a: pl.BlockSpec(a.shape, lambda i, *_: (0,) * a.ndim)
    wsg, wsu, wsd = wsg.astype(BF16), wsu.astype(BF16), wsd.astype(BF16)
    gf = gf.reshape(1, d)
    return pl.pallas_call(
        functools.partial(_combine_kernel, tt=tt, nslot=nslot, d=d),
        out_shape=jax.ShapeDtypeStruct((n, d), F32),
        grid_spec=pltpu.PrefetchScalarGridSpec(
            num_scalar_prefetch=4, grid=(n // tt,),
            in_specs=[pl.BlockSpec(memory_space=pl.ANY),
                      pl.BlockSpec((tt, TOP_K), lambda i, *_: (i, 0)),
                      row, row,
                      pl.BlockSpec((1, 1, ada3.shape[2]), lambda i, *_: (i // per_b, 0, 0)),
                      full(wsg), full(wsu), full(wsd), full(gf)],
            out_specs=row,
            scratch_shapes=[pltpu.VMEM((2, nslot, d // 2), I32), pltpu.SemaphoreType.DMA((2,))]),
        compiler_params=_cparams(("arbitrary",)),
        name="combine",
    )(*tabs, ys, loc, h1, u2, ada3, wsg, wsu, wsd, gf)


def _moe(h1, u2, ada3, w_router, router_bias, wg, wu, wd, wsg, wsu, wsd, gf):
    bsz, seq, d = h1.shape
    n = bsz * seq
    assert seq % min(MOE_TILE, n) == 0
    h1f = h1.reshape(n, d)
    u2f = u2.reshape(n, d)
    gate_t, loc_t, c8, loff, run0, tot = _route(u2f, w_router, router_bias)
    nt = c8.shape[0]
    as_tab = lambda a: a[:, :, 0].astype(I32)
    tot8 = tot[:, 0].astype(I32)
    padded = (tot8 + EXPERT_ROWS - 1) // EXPERT_ROWS * EXPERT_ROWS
    pend = jnp.cumsum(padded)
    pstart = (pend - padded).astype(I32)
    nblk = (n * TOP_K + nt * N_EXPERTS * (SUBLANES - 1) + N_EXPERTS * (EXPERT_ROWS - 1) + EXPERT_ROWS - 1) // EXPERT_ROWS
    blk_row0 = jnp.arange(nblk, dtype=I32) * EXPERT_ROWS
    blk_expert = jnp.minimum(jnp.sum(pend[None, :] <= blk_row0[:, None], axis=1), N_EXPERTS - 1).astype(I32)
    n_used = (pend[-1:] // EXPERT_ROWS).astype(I32)
    n8 = as_tab(c8) // SUBLANES
    tabs = (n8.reshape(-1), as_tab(loff).reshape(-1), (pstart[None, :] + as_tab(run0)).reshape(-1),
            jnp.sum(n8, axis=1))
    ztabs = ((padded - tot8) // SUBLANES, pstart + tot8)
    xs = _dispatch(tabs, ztabs, u2f, loc_t, gate_t, nblk * EXPERT_ROWS)
    ys = _experts(blk_expert, n_used, xs, wg, wu, wd)
    out = _combine(tabs, ys, loc_t.T, h1f, u2f, ada3, wsg, wsu, wsd, gf, seq)
    return out.reshape(bsz, seq, d)


def kernel(x, c, w_ada, b_ada, norm1_g, w_in, ssm_lambda_re, ssm_lambda_im, ssm_log_dt, ssm_b_re, ssm_b_im,
           ssm_c_re, ssm_c_im, ssm_d, ssm_w_glu, ssm_b_glu, w_proj_ssm, w_proj_attn, w_out, norm2_g, w_router,
           router_bias, w_exp_gate, w_exp_up, w_exp_down, w_sh_gate, w_sh_up, w_sh_down, norm_f_g):
    depth = w_ada.shape[0]
    assert depth == 1, "the final norm is fused into the last (only) layer's combine kernel"
    bsz, seq, d = x.shape
    layer = 0
    ada3 = _ada(c, w_ada[layer], b_ada[layer]).reshape(bsz, 1, 6 * d)
    us, k, ki, qt, qit, vt, wit = _inproj(x, ada3, norm1_g[layer], w_in[layer])
    a_re, a_im, bb_re, bb_im = _s5disc(ssm_lambda_re[layer], ssm_lambda_im[layer], ssm_log_dt[layer],
                                       ssm_b_re[layer], ssm_b_im[layer])
    ys_t = _s5(us.transpose(1, 0, 2), a_re, a_im, bb_re, bb_im, ssm_c_re[layer], ssm_c_im[layer],
               ssm_d[layer], ssm_w_glu[layer], ssm_b_glu[layer])
    ya = _dsa(qt, qit, wit, k, ki, vt)
    h1, u2 = _mix(x, ys_t.transpose(1, 0, 2), ya, ada3, norm1_g[layer], w_in[layer], w_proj_ssm[layer],
                  w_proj_attn[layer], w_out[layer], norm2_g[layer])
    return _moe(h1, u2, ada3, w_router[layer], router_bias[layer], w_exp_gate[layer], w_exp_up[layer],
                w_exp_down[layer], w_sh_gate[layer], w_sh_up[layer], w_sh_down[layer], norm_f_g)
```

```python
import functools
import math

import jax
import jax.numpy as jnp
import numpy as np
from jax import lax
from jax.experimental import pallas as pl
from jax.experimental.pallas import tpu as pltpu

F32 = jnp.float32
BF16 = jnp.bfloat16
I32 = jnp.int32

SSM_GROUP = 16
SSM_STATE = 64
N_HEADS = 8
HEAD_DIM = 64
IDX_HEADS = 8
IDX_DIM = 64
TOPK_MAX = 256
N_EXPERTS = 64
TOP_K = 8
N_GROUPS = 8
TOPK_GROUPS = 4
ROUTED_SCALE = 2.5
EPS = 1e-6

V7X_VMEM_LIMIT_BYTES = 56 * 1024 * 1024
LANES = 128
SUBLANES = 8

INPROJ_ROWS = 256
S5_STEPS = 64
S5_LANE_CHUNK = 128
DSA_Q_COLS = 256
DSA_K_ROWS = 512
DSA_COUNT_ROWS = 64
BITSLICE_ROWS = 256
POS_SPLIT = 64
MIX_ROWS = 512
MOE_TILE = 256
ROUTE_TILES = 2
SLOT_CHUNK = 256
EXPERT_ROWS = 1024

NEG_BIG = -1e30
INT_MIN = -(2 ** 31)


def _cparams(sem):
    return pltpu.CompilerParams(dimension_semantics=sem, vmem_limit_bytes=V7X_VMEM_LIMIT_BYTES)


def _bf(x):
    return x.astype(BF16)


def _dot(a, b):
    return jnp.dot(a, b, preferred_element_type=F32)


def _dot_nt(a, b):
    return lax.dot_general(a, b, (((1,), (1,)), ((), ())), preferred_element_type=F32)


def _split(x):
    hi = _bf(x)
    lo = _bf(x - hi.astype(F32))
    return hi, lo


def _dot3(a, b):
    ah, al = _split(a)
    bh, bl = _split(b)
    return _dot(ah, bh) + (_dot(ah, bl) + _dot(al, bh))


def _rms(x, g):
    return x * lax.rsqrt(jnp.mean(x * x, axis=-1, keepdims=True) + EPS) * g


def _ada_kernel(c_ref, w_ref, b_ref, o_ref):
    c = c_ref[...]
    o_ref[...] = _dot3(c * jax.nn.sigmoid(c), w_ref[...]) + b_ref[...]


def _ada(c, w, b):
    bsz, d = c.shape
    n = w.shape[1]
    tn = 1024
    return pl.pallas_call(
        _ada_kernel,
        out_shape=jax.ShapeDtypeStruct((bsz, n), F32),
        grid=(n // tn,),
        in_specs=[pl.BlockSpec((bsz, d), lambda j: (0, 0)),
                  pl.BlockSpec((d, tn), lambda j: (0, j)),
                  pl.BlockSpec((1, tn), lambda j: (0, j))],
        out_specs=pl.BlockSpec((bsz, tn), lambda j: (0, j)),
        compiler_params=_cparams(("arbitrary",)),
        name="ada",
    )(c, w, b.reshape(1, n))


ALIBI_SLOPES = tuple(2.0 ** (-8.0 * (h + 1) / N_HEADS) for h in range(N_HEADS))
QAUG_ROWS = 16


def _inproj_kernel(x_ref, ada_ref, g1_ref, w_ref, wt_ref,
                   us_ref, k_ref, ki_ref, qt_ref, qit_ref, vt_ref, wit_ref, *, d, ssm_w, attn_w, idx_w, tl):
    x = x_ref[0]
    shift = ada_ref[0, :, 0:d]
    scale = ada_ref[0, :, d:2 * d]
    u = _bf(_rms(x, g1_ref[...]) * (1.0 + scale) + shift)
    r = _dot(u, w_ref[...])
    us_ref[0] = r[:, 0:ssm_w]
    k_ref[0] = _bf(r[:, ssm_w:ssm_w + HEAD_DIM])
    ki_ref[0] = _bf(r[:, ssm_w + LANES:ssm_w + LANES + IDX_DIM])
    rt = _dot_nt(wt_ref[...], u)
    arow = lax.broadcasted_iota(I32, (QAUG_ROWS, tl), 0)
    for h in range(N_HEADS):
        base = h * LANES
        qt_ref[0, base:base + HEAD_DIM, :] = _bf(rt[h * HEAD_DIM:(h + 1) * HEAD_DIM])
        qt_ref[0, base + HEAD_DIM:base + HEAD_DIM + QAUG_ROWS, :] = _bf(jnp.where(arow < 2, ALIBI_SLOPES[h], 0.0))
        qt_ref[0, base + HEAD_DIM + QAUG_ROWS:base + LANES, :] = jnp.zeros((LANES - HEAD_DIM - QAUG_ROWS, tl), BF16)
    qit_ref[0] = _bf(rt[attn_w:attn_w + idx_w])
    vt_ref[0] = _bf(rt[attn_w + idx_w:attn_w + idx_w + HEAD_DIM])
    wit_ref[0] = rt[attn_w + idx_w + HEAD_DIM:attn_w + idx_w + HEAD_DIM + IDX_HEADS]


def _split_w_in(w_in, d):
    ssm_w = 512
    sizes = (ssm_w, N_HEADS * HEAD_DIM, HEAD_DIM, HEAD_DIM, IDX_HEADS * IDX_DIM, IDX_DIM, IDX_HEADS, d, d)
    offs = [0]
    for s in sizes:
        offs.append(offs[-1] + s)
    return [w_in[:, offs[i]:offs[i + 1]] for i in range(9)]


def _inproj(x, ada3, g1, w_in):
    bsz, seq, d = x.shape
    ssm_w = 512
    attn_w = N_HEADS * HEAD_DIM
    idx_w = IDX_HEADS * IDX_DIM
    w_ssm, w_q, w_k, w_v, w_qi, w_ki, w_wi, _, _ = _split_w_in(w_in, d)
    zpad = lambda n: jnp.zeros((d, n), F32)
    wbig = jnp.concatenate([w_ssm, w_k, zpad(LANES - HEAD_DIM), w_ki, zpad(LANES - IDX_DIM)], axis=1).astype(BF16)
    wt = jnp.concatenate([w_q * (HEAD_DIM ** -0.5), w_qi * (IDX_DIM ** -0.5), w_v, w_wi,
                          zpad(LANES - HEAD_DIM - IDX_HEADS)], axis=1).T.astype(BF16)
    tl = INPROJ_ROWS
    kern = functools.partial(_inproj_kernel, d=d, ssm_w=ssm_w, attn_w=attn_w, idx_w=idx_w, tl=tl)
    row = lambda w: pl.BlockSpec((1, tl, w), lambda b, l: (b, l, 0))
    colt = lambda h: pl.BlockSpec((1, h, tl), lambda b, l: (b, 0, l))
    full = lambda a: pl.BlockSpec(a.shape, lambda b, l: (0,) * a.ndim)
    return pl.pallas_call(
        kern,
        out_shape=(jax.ShapeDtypeStruct((bsz, seq, ssm_w), F32),
                   jax.ShapeDtypeStruct((bsz, seq, HEAD_DIM), BF16),
                   jax.ShapeDtypeStruct((bsz, seq, IDX_DIM), BF16),
                   jax.ShapeDtypeStruct((bsz, N_HEADS * LANES, seq), BF16),
                   jax.ShapeDtypeStruct((bsz, idx_w, seq), BF16),
                   jax.ShapeDtypeStruct((bsz, HEAD_DIM, seq), BF16),
                   jax.ShapeDtypeStruct((bsz, IDX_HEADS, seq), F32)),
        grid=(bsz, seq // tl),
        in_specs=[row(d),
                  pl.BlockSpec((1, 1, ada3.shape[2]), lambda b, l: (b, 0, 0)),
                  pl.BlockSpec((1, d), lambda b, l: (0, 0)),
                  full(wbig), full(wt)],
        out_specs=(row(ssm_w), row(HEAD_DIM), row(IDX_DIM),
                   colt(N_HEADS * LANES), colt(idx_w), colt(HEAD_DIM), colt(IDX_HEADS)),
        compiler_params=_cparams(("arbitrary", "arbitrary")),
        name="inproj",
    )(x, ada3, g1.reshape(1, d), wbig, wt)


def _s5disc_kernel(lr_ref, li_ref, ldt_ref, br_ref, bi_ref, are_ref, aim_ref, bbr_ref, bbi_ref):
    lr = lr_ref[...]
    li = li_ref[...]
    dt = jnp.exp(ldt_ref[...])
    mag = jnp.exp(lr * dt)
    a_re = mag * jnp.cos(li * dt)
    a_im = mag * jnp.sin(li * dt)
    den = lr * lr + li * li
    n_re = a_re - 1.0
    f_re = (n_re * lr + a_im * li) / den
    f_im = (a_im * lr - n_re * li) / den
    br = br_ref[...]
    bi = bi_ref[...]
    are_ref[...] = a_re
    aim_ref[...] = a_im
    bbr_ref[...] = f_re * br - f_im * bi
    bbi_ref[...] = f_re * bi + f_im * br


def _s5disc(lam_re, lam_im, log_dt, b_re, b_im):
    g, p = lam_re.shape
    h = b_re.shape[2]
    rep = lambda a: jnp.repeat(a, h, axis=1)
    ldt = jnp.broadcast_to(log_dt[:, None], (g, p * h))
    sds = jax.ShapeDtypeStruct((g, p * h), F32)
    a_re, a_im, bb_re, bb_im = pl.pallas_call(
        _s5disc_kernel, out_shape=(sds, sds, sds, sds), name="s5disc",
    )(rep(lam_re), rep(lam_im), ldt, b_re.reshape(g, p * h), b_im.reshape(g, p * h))
    return a_re[:, ::h], a_im[:, ::h], bb_re.reshape(g, p, h), bb_im.reshape(g, p, h)


def _s5_kernel(u_ref, wb_ref, ar_ref, ai_ref, cc_ref, dsk_ref, wg_ref, bg_ref, o_ref, buf, hst, *, tl, width):
    nch = width // S5_LANE_CHUNK
    sw = S5_LANE_CHUNK // SSM_GROUP * SSM_STATE
    rows = tl * SUBLANES

    @pl.when(pl.program_id(0) == 0)
    def _():
        hst[...] = jnp.zeros_like(hst)

    u = u_ref[...].reshape(rows, width)
    ub = _bf(u)
    for j in range(nch):
        buf[:, j * 2 * sw:(j + 1) * 2 * sw] = _dot(ub[:, j * S5_LANE_CHUNK:(j + 1) * S5_LANE_CHUNK], wb_ref[j])

    for j in range(nch):
        re_cols = slice(j * 2 * sw, j * 2 * sw + sw)
        im_cols = slice(j * 2 * sw + sw, (j + 1) * 2 * sw)
        a_re = jnp.broadcast_to(ar_ref[:, j * sw:(j + 1) * sw], (SUBLANES, sw))
        a_im = jnp.broadcast_to(ai_ref[:, j * sw:(j + 1) * sw], (SUBLANES, sw))

        def step(t, carry, re_cols=re_cols, im_cols=im_cols, a_re=a_re, a_im=a_im):
            h_re, h_im = carry
            r0 = pl.multiple_of(t * SUBLANES, SUBLANES)
            n_re = (a_re * h_re - a_im * h_im) + buf[pl.ds(r0, SUBLANES), re_cols]
            n_im = (a_re * h_im + a_im * h_re) + buf[pl.ds(r0, SUBLANES), im_cols]
            buf[pl.ds(r0, SUBLANES), re_cols] = n_re
            buf[pl.ds(r0, SUBLANES), im_cols] = n_im
            return n_re, n_im

        h_re, h_im = lax.fori_loop(0, tl, step, (hst[:, re_cols], hst[:, im_cols]), unroll=8)
        hst[:, re_cols] = h_re
        hst[:, im_cols] = h_im

    ys = [_dot(_bf(buf[:, j * 2 * sw:(j + 1) * 2 * sw]), cc_ref[j]) for j in range(nch)]
    y = jnp.concatenate(ys, axis=1) + dsk_ref[...] * u
    y = jax.nn.gelu(y)
    y = y * jax.nn.sigmoid(_dot(_bf(y), wg_ref[...]) + bg_ref[...])
    o_ref[...] = _bf(y).reshape(tl, SUBLANES, width)


def _s5(u_t, a_re, a_im, bb_re, bb_im, c_re, c_im, d_skip, w_glu, b_glu):
    seq, bsz, width = u_t.shape
    assert bsz == SUBLANES
    nch = width // S5_LANE_CHUNK
    gpc = S5_LANE_CHUNK // SSM_GROUP
    sw = gpc * SSM_STATE
    eye = jnp.eye(gpc, dtype=F32)

    def bmat(bb):
        t = bb.reshape(nch, gpc, SSM_STATE, SSM_GROUP).transpose(0, 1, 3, 2)
        return jnp.einsum('jghp,gk->jghkp', t, eye).reshape(nch, S5_LANE_CHUNK, sw)

    def cmat(cc):
        t = cc.reshape(nch, gpc, SSM_GROUP, SSM_STATE).transpose(0, 1, 3, 2)
        return jnp.einsum('jgph,gk->jgpkh', t, eye).reshape(nch, sw, S5_LANE_CHUNK)

    wb = jnp.concatenate([bmat(bb_re), bmat(bb_im)], axis=2)
    cc = jnp.concatenate([cmat(c_re), -cmat(c_im)], axis=1)
    tl = S5_STEPS
    full = lambda a: pl.BlockSpec(a.shape, lambda i: (0,) * a.ndim)
    args = (u_t, wb.astype(BF16), a_re.reshape(1, -1), a_im.reshape(1, -1), cc.astype(BF16),
            d_skip.reshape(1, width), w_glu.astype(BF16), b_glu.reshape(1, width))
    return pl.pallas_call(
        functools.partial(_s5_kernel, tl=tl, width=width),
        out_shape=jax.ShapeDtypeStruct((seq, bsz, width), BF16),
        grid=(seq // tl,),
        in_specs=[pl.BlockSpec((tl, bsz, width), lambda i: (i, 0, 0))] + [full(a) for a in args[1:]],
        out_specs=pl.BlockSpec((tl, bsz, width), lambda i: (i, 0, 0)),
        scratch_shapes=[pltpu.VMEM((tl * SUBLANES, nch * 2 * sw), F32),
                        pltpu.VMEM((SUBLANES, nch * 2 * sw), F32)],
        compiler_params=_cparams(("arbitrary",)),
        name="s5",
    )(*args)


def _bit_transpose32(words):
    x = list(words)
    j, m = 16, 0x0000FFFF
    while j:
        k = 0
        while k < 32:
            t = (x[k] ^ lax.shift_right_logical(x[k + j], jnp.int32(j))) & jnp.int32(m - (1 << 32) if m >= 1 << 31 else m)
            x[k] = x[k] ^ t
            x[k + j] = x[k + j] ^ lax.shift_left(t, jnp.int32(j))
            k = (k + j + 1) & ~j
        j >>= 1
        m = (m ^ (m << j)) & 0xFFFFFFFF
    return x


def _dsa_kernel(qt_ref, qit_ref, wit_ref, ka_ref, ki_ref, vt_ref, o_ref, key_s, mb_s, acc_s, pl_s, p_s, *, tq, tk, topk, seq):
    i = pl.program_id(1)
    q0 = i * tq
    nkt = (q0 + tq + tk - 1) // tk
    ch = DSA_COUNT_ROWS
    krow = lax.broadcasted_iota(I32, (tk, tq), 0)
    qcol = q0 + lax.broadcasted_iota(I32, (tk, tq), 1)
    crow = lax.broadcasted_iota(I32, (ch, tq), 0)

    wb = wit_ref[0] * (IDX_HEADS ** -0.5)

    def score_tile(j, _):
        r0 = pl.multiple_of(j * tk, tk)
        kit = ki_ref[0, pl.ds(r0, tk), :]
        acc = jnp.zeros((tk, tq), F32)
        for h in range(IDX_HEADS):
            s = _dot(kit, qit_ref[0, h * IDX_DIM:(h + 1) * IDX_DIM, :])
            acc = acc + wb[h:h + 1, :] * jnp.maximum(s, 0.0)
        bits = lax.bitcast_convert_type(acc, I32)
        key = jnp.where(bits < 0, bits ^ jnp.int32(0x7FFFFFFF), bits)
        key = jnp.where(acc == 0.0, 0, key)
        key = jnp.where(krow + r0 <= qcol, key, INT_MIN)
        key_s[pl.ds(r0, tk), :] = key
        ukey = key ^ INT_MIN
        for c in range(tk // BITSLICE_ROWS):
            words = [ukey[c * BITSLICE_ROWS + v * SUBLANES:c * BITSLICE_ROWS + (v + 1) * SUBLANES, :]
                     for v in range(32)]
            planes = _bit_transpose32(words)
            g0 = pl.multiple_of((j * (tk // BITSLICE_ROWS) + c) * SUBLANES, SUBLANES)
            for it in range(32):
                pl_s[it, pl.ds(g0, SUBLANES), :] = planes[it]
        return 0

    @pl.when((pl.program_id(0) == 0) & (i == 0))
    def _():
        pl_s[...] = jnp.zeros(pl_s.shape, I32)

    lax.fori_loop(0, nkt, score_tile, 0)

    def count(pred):
        def tile(j, cnt):
            for c in range(tk // ch):
                rr = pl.multiple_of(j * tk + c * ch, ch)
                cnt = cnt + jnp.where(pred(key_s[pl.ds(rr, ch), :], rr), 1, 0)
            return cnt
        cnt = lax.fori_loop(0, nkt, tile, jnp.zeros((ch, tq), I32))
        return jnp.sum(cnt.astype(F32), axis=0, keepdims=True)

    ngrp = seq // 32

    def lane_count(words):
        pc = lax.population_count(words).reshape(ngrp // SUBLANES, SUBLANES, tq)
        return jnp.sum(jnp.sum(pc, axis=0).astype(F32), axis=0, keepdims=True)

    def bit_step(it, carry):
        alive, above, ans_u = carry
        ones = alive & pl_s[it]
        cnt1 = lane_count(ones)
        take = above + cnt1 >= float(topk)
        alive = jnp.where(take, ones, alive ^ ones)
        above = jnp.where(take, above, above + cnt1)
        ans_u = jnp.where(take, ans_u | lax.shift_left(jnp.int32(1), 31 - it), ans_u)
        return alive, above, ans_u

    grow = lax.broadcasted_iota(I32, (ngrp, tq), 0)
    alive0 = jnp.where(grow < nkt * (tk // 32), -1, 0)
    alive, above, ans_u = lax.fori_loop(
        0, 32, bit_step, (alive0, jnp.zeros((1, tq), F32), jnp.zeros((1, tq), I32)))
    thr = jnp.maximum(ans_u ^ INT_MIN, INT_MIN + 1)
    cnt_ge = above + lane_count(alive)
    tied = jnp.where(ans_u != 0, cnt_ge, 0.0) > float(topk)
    has_ties = jnp.max(jnp.where(tied, 1.0, 0.0)) > 0.0

    def tie_cut():
        need = float(topk) - count(lambda kb, rr: kb > thr)
        nbits = max(1, (seq - 1).bit_length())

        def idx_step(b, x):
            cand = x | lax.shift_left(jnp.int32(1), nbits - 1 - b)
            below = count(lambda kb, rr: jnp.where(kb == thr, crow + rr, seq) < cand)
            return jnp.where(below < need, cand, x)

        x = lax.fori_loop(0, nbits, idx_step, jnp.zeros((1, tq), I32))
        return jnp.where(tied, x, seq)

    cut = lax.cond(has_ties, tie_cut, lambda: jnp.full((1, tq), seq, I32))

    def bias_tile(j, _):
        for c in range(tk // ch):
            rr = pl.multiple_of(j * tk + c * ch, ch)
            kb = key_s[pl.ds(rr, ch), :]
            tie_bias = jnp.where(crow + rr <= cut, 0.0, NEG_BIG)
            mb_s[pl.ds(rr, ch), :] = jnp.where(kb > thr, 0.0, jnp.where(kb == thr, tie_bias, NEG_BIG))
        return 0

    lax.fori_loop(0, nkt, bias_tile, 0)

    def logits(j, h):
        r0 = pl.multiple_of(j * tk, tk)
        s = _dot(ka_ref[0, pl.ds(r0, tk), :], qt_ref[0, h * LANES:(h + 1) * LANES, :]) + mb_s[pl.ds(r0, tk), :]
        return s.reshape(tk // SUBLANES, SUBLANES, tq)

    acc_s[...] = jnp.zeros(acc_s.shape, F32)

    def attn_tile(j, carry):
        ms, ls = carry
        r0 = pl.multiple_of(j * tk, tk)
        new_m, new_l, alphas = [], [], []
        for h in range(N_HEADS):
            s = logits(j, h)
            m_new = jnp.maximum(ms[h], jnp.max(jnp.max(s, axis=0), axis=0, keepdims=True))
            alpha = jnp.exp(ms[h] - m_new)
            p = jnp.exp(s - m_new)
            new_m.append(m_new)
            new_l.append(alpha * ls[h] + jnp.sum(p, axis=0))
            alphas.append(alpha)
            p_s[h] = _bf(p.reshape(tk, tq))
        for h in range(N_HEADS):
            rows = slice(h * HEAD_DIM, (h + 1) * HEAD_DIM)
            acc_s[rows, :] = alphas[h] * acc_s[rows, :] + _dot(vt_ref[0, :, pl.ds(r0, tk)], p_s[h])
        return tuple(new_m), tuple(new_l)

    init = ((jnp.full((1, tq), NEG_BIG, F32),) * N_HEADS, (jnp.zeros((SUBLANES, tq), F32),) * N_HEADS)
    _, ls = lax.fori_loop(0, nkt, attn_tile, init)
    for h in range(N_HEADS):
        rows = slice(h * HEAD_DIM, (h + 1) * HEAD_DIM)
        acc_s[rows, :] = acc_s[rows, :] / jnp.sum(ls[h], axis=0, keepdims=True)
    o_ref[0] = _bf(acc_s[...].T)


def _dsa(qt, qit, wit, k, ki, vt):
    bsz, seq = k.shape[0], k.shape[1]
    aw = N_HEADS * HEAD_DIM
    tq = min(DSA_Q_COLS, seq)
    tk = min(DSA_K_ROWS, seq)
    topk = min(TOPK_MAX, seq // 4)
    assert (seq - 1) // POS_SPLIT < 256 and POS_SPLIT <= 256, "key positions must split into two bf16-exact parts"
    assert all(float(np.float32(sl).astype(BF16)) == sl for sl in ALIBI_SLOPES), "ALiBi slopes must be bf16-exact"
    pos = jnp.arange(seq, dtype=I32)
    posc = jnp.stack([(pos // POS_SPLIT) * POS_SPLIT, pos % POS_SPLIT], axis=1).astype(BF16)
    ka = jnp.concatenate([k, jnp.broadcast_to(posc[None], (bsz, seq, 2)),
                          jnp.zeros((bsz, seq, LANES - HEAD_DIM - 2), BF16)], axis=2)
    kern = functools.partial(_dsa_kernel, tq=tq, tk=tk, topk=topk, seq=seq)
    cols = lambda r: pl.BlockSpec((1, r, tq), lambda b, i: (b, 0, i))
    return pl.pallas_call(
        kern,
        out_shape=jax.ShapeDtypeStruct((bsz, seq, aw), BF16),
        grid=(bsz, seq // tq),
        in_specs=[cols(N_HEADS * LANES), cols(qit.shape[1]), cols(IDX_HEADS),
                  pl.BlockSpec((1, seq, LANES), lambda b, i: (b, 0, 0)),
                  pl.BlockSpec((1, seq, IDX_DIM), lambda b, i: (b, 0, 0)),
                  pl.BlockSpec((1, HEAD_DIM, seq), lambda b, i: (b, 0, 0))],
        out_specs=pl.BlockSpec((1, tq, aw), lambda b, i: (b, i, 0)),
        scratch_shapes=[pltpu.VMEM((seq, tq), I32), pltpu.VMEM((seq, tq), F32), pltpu.VMEM((aw, tq), F32),
                        pltpu.VMEM((32, seq // 32, tq), I32), pltpu.VMEM((N_HEADS, tk, tq), BF16)],
        compiler_params=_cparams(("arbitrary", "arbitrary")),
        name="dsa",
    )(qt, qit, wit, ka, ki, vt)


def _mix_kernel(x_ref, ys_ref, ya_ref, ada_ref, g1_ref, wgt_ref, wps_ref, wpa_ref, wo_ref, g2_ref,
                h_ref, u2_ref, *, d):
    gate1 = ada_ref[0, :, 2 * d:3 * d]
    shift2 = ada_ref[0, :, 3 * d:4 * d]
    scale2 = ada_ref[0, :, 4 * d:5 * d]
    x = x_ref[0]
    u = _bf(_rms(x, g1_ref[...]) * (1.0 + ada_ref[0, :, d:2 * d]) + ada_ref[0, :, 0:d])
    g = _dot(u, wgt_ref[...])
    mixed = (jax.nn.sigmoid(g[:, 0:d]) * _dot(ys_ref[0], wps_ref[...])
             + jax.nn.sigmoid(g[:, d:2 * d]) * _dot(ya_ref[0], wpa_ref[...]))
    h = x + gate1 * _dot(_bf(mixed), wo_ref[...])
    h_ref[0] = h
    u2_ref[0] = _rms(h, g2_ref[...]) * (1.0 + scale2) + shift2


def _mix(x, ys, ya, ada3, g1, w_in, wps, wpa, wo, g2):
    bsz, seq, d = x.shape
    tm = MIX_ROWS
    row = lambda w: pl.BlockSpec((1, tm, w), lambda b, l: (b, l, 0))
    full = lambda a: pl.BlockSpec(a.shape, lambda b, l: (0,) * a.ndim)
    wps, wpa, wo = wps.astype(BF16), wpa.astype(BF16), wo.astype(BF16)
    wgt = jnp.concatenate(_split_w_in(w_in, d)[7:9], axis=1).astype(BF16)
    g1 = g1.reshape(1, d)
    g2 = g2.reshape(1, d)
    return pl.pallas_call(
        functools.partial(_mix_kernel, d=d),
        out_shape=(jax.ShapeDtypeStruct((bsz, seq, d), F32), jax.ShapeDtypeStruct((bsz, seq, d), F32)),
        grid=(bsz, seq // tm),
        in_specs=[row(d), row(ys.shape[2]), row(ya.shape[2]),
                  pl.BlockSpec((1, 1, ada3.shape[2]), lambda b, l: (b, 0, 0)),
                  full(g1), full(wgt), full(wps), full(wpa), full(wo), full(g2)],
        out_specs=(row(d), row(d)),
        compiler_params=_cparams(("arbitrary", "arbitrary")),
        name="mix",
    )(x, ys, ya, ada3, g1, wgt, wps, wpa, wo, g2)


def _first_max(cur, idx, axis, big):
    m = jnp.max(cur, axis=axis, keepdims=True)
    first = jnp.min(jnp.where(cur == m, idx, big), axis=axis, keepdims=True)
    return m, idx == first


def _route_kernel(u_ref, wrh_ref, wrl_ref, rb_ref, tri_ref, ltri_ref,
                  gt_ref, loc_ref, c8_ref, loff_ref, run0_ref, tot_ref, run_s, *, t, tt):
    @pl.when(pl.program_id(0) == 0)
    def _():
        run_s[...] = jnp.zeros_like(run_s)

    uh, ul = _split(u_ref[...])
    logits = _dot_nt(wrh_ref[...], uh) + (_dot_nt(wrl_ref[...], uh) + _dot_nt(wrh_ref[...], ul))
    scores = jax.nn.sigmoid(logits)
    biased = scores + rb_ref[...]
    per_group = N_EXPERTS // N_GROUPS
    b3 = biased.reshape(N_GROUPS, per_group, t)
    i3 = lax.broadcasted_iota(I32, b3.shape, 1)
    m1, hit1 = _first_max(b3, i3, 1, per_group)
    m2 = jnp.max(jnp.where(hit1, -jnp.inf, b3), axis=1, keepdims=True)
    gs = (m1 + m2).reshape(N_GROUPS, t)
    gi = lax.broadcasted_iota(I32, gs.shape, 0)
    gsel = jnp.zeros(gs.shape, F32)
    for _ in range(TOPK_GROUPS):
        _, hit = _first_max(gs, gi, 0, N_GROUPS)
        gsel = jnp.where(hit, 1.0, gsel)
        gs = jnp.where(hit, -jnp.inf, gs)
    cur = jnp.where(gsel.reshape(N_GROUPS, 1, t) > 0.0, b3, -jnp.inf).reshape(N_EXPERTS, t)
    ei = lax.broadcasted_iota(I32, cur.shape, 0)
    hits = []
    gates = []
    for _ in range(TOP_K):
        _, hit = _first_max(cur, ei, 0, N_EXPERTS)
        hits.append(hit)
        gates.append(jnp.sum(jnp.where(hit, scores, 0.0), axis=0, keepdims=True))
        cur = jnp.where(hit, -jnp.inf, cur)
    gate = jnp.concatenate(gates, axis=0)
    gt_ref[...] = gate / jnp.sum(gate, axis=0, keepdims=True) * ROUTED_SCALE
    onehot = jnp.zeros(cur.shape, F32)
    for hit in hits:
        onehot = jnp.where(hit, 1.0, onehot)
    for sub in range(t // tt):
        cols = slice(sub * tt, (sub + 1) * tt)
        oh = onehot[:, cols]
        cnt = jnp.sum(oh, axis=1, keepdims=True)
        c8 = jnp.floor((cnt + (SUBLANES - 1)) * (1.0 / SUBLANES)) * SUBLANES
        c8l = jnp.broadcast_to(c8, (N_EXPERTS, LANES))
        loff = _dot(ltri_ref[...], _bf(c8l))
        slot = _dot(_bf(oh), tri_ref[...]) + loff[:, 0:1]
        loc_ref[:, cols] = jnp.concatenate(
            [jnp.sum(jnp.where(hit[:, cols], slot, 0.0), axis=0, keepdims=True) for hit in hits],
            axis=0).astype(I32)
        c8_ref[sub] = c8l
        loff_ref[sub] = loff
        run0_ref[sub] = run_s[...]
        run_s[...] = run_s[...] + c8
    tot_ref[...] = run_s[...]


def _route(u2, w_router, router_bias):
    n, d = u2.shape
    tt = min(MOE_TILE, n)
    t = min(ROUTE_TILES * tt, n)
    nt = n // tt
    wt = w_router.T
    wrh = wt.astype(BF16)
    wrl = (wt - wrh.astype(F32)).astype(BF16)
    tri = (jnp.arange(tt)[:, None] < jnp.arange(tt)[None, :]).astype(BF16)
    ex = jnp.arange(N_EXPERTS)
    ltri = (ex[None, :] < ex[:, None]).astype(BF16)
    full = lambda a: pl.BlockSpec(a.shape, lambda i: (0,) * a.ndim)
    col = pl.BlockSpec((TOP_K, t), lambda i: (0, i))
    tab = pl.BlockSpec((t // tt, N_EXPERTS, LANES), lambda i: (i, 0, 0))
    tab_sds = jax.ShapeDtypeStruct((nt, N_EXPERTS, LANES), F32)
    rb = router_bias.reshape(N_EXPERTS, 1)
    return pl.pallas_call(
        functools.partial(_route_kernel, t=t, tt=tt),
        out_shape=(jax.ShapeDtypeStruct((TOP_K, n), F32), jax.ShapeDtypeStruct((TOP_K, n), I32),
                   tab_sds, tab_sds, tab_sds, jax.ShapeDtypeStruct((N_EXPERTS, LANES), F32)),
        grid=(n // t,),
        in_specs=[pl.BlockSpec((t, d), lambda i: (i, 0)), full(wrh), full(wrl), full(rb), full(tri), full(ltri)],
        out_specs=(col, col, tab, tab, tab, pl.BlockSpec((N_EXPERTS, LANES), lambda i: (0, 0))),
        scratch_shapes=[pltpu.VMEM((N_EXPERTS, LANES), F32)],
        compiler_params=_cparams(("arbitrary",)),
        name="route",
    )(u2, wrh, wrl, rb, tri, ltri)


RUN_BITS = tuple(1 << b for b in reversed(range((MOE_TILE // SUBLANES).bit_length())))
RUN_LONG = 8


def _for_each_run_piece(n8_ref, src_ref, dst_ref, tile, bits, fn):
    def per_expert(e, _):
        idx = tile * N_EXPERTS + e
        n8 = n8_ref[idx]
        src = src_ref[idx]
        dst = dst_ref[idx]
        def pieces(some_bits):
            for p in some_bits:
                off = (n8 & ~(2 * p - 1)) * SUBLANES

                @pl.when((n8 & p) != 0)
                def _(p=p, off=off):
                    fn(pl.multiple_of(src + off, SUBLANES), pl.multiple_of(dst + off, SUBLANES), p * SUBLANES,
                       bits.index(p) % 2)

        long_bits = tuple(p for p in bits if p >= RUN_LONG)
        if long_bits:
            pl.when(n8 >= RUN_LONG)(lambda: pieces(long_bits))
        pieces(tuple(p for p in bits if p < RUN_LONG))
        return 0

    lax.fori_loop(0, N_EXPERTS, per_expert, 0)


def _wait_rows(n8, make_copy, max_rows):
    for p in tuple(1 << b for b in reversed(range((max_rows // SUBLANES).bit_length()))):
        @pl.when((n8 & p) != 0)
        def _(p=p):
            make_copy(p * SUBLANES).wait()


def _dispatch_kernel(n8_ref, src_ref, dst_ref, tot_ref, zn8_ref, zdst_ref, u_ref, loc_ref, gate_ref, xs_hbm,
                     lbuf, zx, sems, *, tt, nslot, dh):
    i = pl.program_id(0)
    last = pl.num_programs(0) - 1
    par = i % 2

    def run_copy(slot):
        def piece(s0, d0, rows, prio):
            pltpu.make_async_copy(
                lbuf.at[slot, pl.ds(s0, rows)], xs_hbm.at[pl.ds(d0, rows)], sems.at[slot]).start(priority=prio)
        return piece

    def wait_tile(tile, slot):
        _wait_rows(tot_ref[tile], lambda rows: pltpu.make_async_copy(
            lbuf.at[slot, pl.ds(0, rows)], xs_hbm.at[pl.ds(0, rows)], sems.at[slot]), nslot)

    @pl.when(i == 0)
    def _():
        zx[...] = jnp.zeros(zx.shape, I32)

        def zero_piece(s0, d0, rows, prio):
            cx = pltpu.make_async_copy(zx.at[pl.ds(0, rows)], xs_hbm.at[pl.ds(d0, rows)], sems.at[2])
            cx.start()
            cx.wait()

        zbits = tuple(1 << b for b in reversed(range((EXPERT_ROWS // SUBLANES - 1).bit_length())))
        _for_each_run_piece(zn8_ref, zdst_ref, zdst_ref, 0, zbits, zero_piece)

    ub = _bf(u_ref[...])
    ones = jnp.ones((tt, LANES), BF16)
    loc = loc_ref[...]
    gate = gate_ref[...]
    rows_b = lax.broadcasted_iota(I32, (SLOT_CHUNK, tt), 0).astype(F32).astype(BF16)
    loc_hi = lax.shift_right_logical(loc, SLOT_CHUNK.bit_length() - 1)
    loc_lo = (loc & (SLOT_CHUNK - 1)).astype(F32)
    gate_h = _bf(gate)
    gate_l = _bf(gate - gate_h.astype(F32))
    one_b = jnp.ones((SLOT_CHUNK, tt), BF16)
    for c in range(nslot // SLOT_CHUNK):
        perm = jnp.zeros((SLOT_CHUNK, tt), BF16)
        pgh = jnp.zeros((SLOT_CHUNK, tt), BF16)
        pgl = jnp.zeros((SLOT_CHUNK, tt), BF16)
        for k in range(TOP_K):
            lk = _bf(jnp.where(loc_hi[k:k + 1, :] == c, loc_lo[k:k + 1, :], -1.0))
            eq = rows_b == lk
            perm = jnp.where(eq, one_b, perm)
            pgh = jnp.where(eq, jnp.broadcast_to(gate_h[k:k + 1, :], (SLOT_CHUNK, tt)), pgh)
            pgl = jnp.where(eq, jnp.broadcast_to(gate_l[k:k + 1, :], (SLOT_CHUNK, tt)), pgl)
        cs = slice(c * SLOT_CHUNK, (c + 1) * SLOT_CHUNK)
        xp = lax.bitcast_convert_type(_dot(perm, ub), I32)
        lbuf[par, cs, 0:dh] = xp[:, 0:dh] | lax.shift_right_logical(xp[:, dh:2 * dh], 16)
        lbuf[par, cs, dh:dh + LANES] = lax.bitcast_convert_type(_dot(pgh, ones) + _dot(pgl, ones), I32)

    _for_each_run_piece(n8_ref, src_ref, dst_ref, i, RUN_BITS, run_copy(par))

    @pl.when(i > 0)
    def _():
        wait_tile(i - 1, 1 - par)

    @pl.when(i == last)
    def _():
        wait_tile(i, par)


def _dispatch(tabs, ztabs, u2, loc_t, gate_t, n_rows):
    n, d = u2.shape
    tt = min(MOE_TILE, n)
    nslot = TOP_K * tt + N_EXPERTS * SUBLANES
    dh = d // 2
    assert nslot % SLOT_CHUNK == 0 and tt // SUBLANES == RUN_BITS[0]
    col = pl.BlockSpec((TOP_K, tt), lambda i, *_: (0, i))
    return pl.pallas_call(
        functools.partial(_dispatch_kernel, tt=tt, nslot=nslot, dh=dh),
        out_shape=jax.ShapeDtypeStruct((n_rows, dh + LANES), I32),
        grid_spec=pltpu.PrefetchScalarGridSpec(
            num_scalar_prefetch=6, grid=(n // tt,),
            in_specs=[pl.BlockSpec((tt, d), lambda i, *_: (i, 0)), col, col],
            out_specs=pl.BlockSpec(memory_space=pl.ANY),
            scratch_shapes=[pltpu.VMEM((2, nslot, dh + LANES), I32),
                            pltpu.VMEM((EXPERT_ROWS // 2, dh + LANES), I32),
                            pltpu.SemaphoreType.DMA((3,))]),
        compiler_params=_cparams(("arbitrary",)),
        name="dispatch",
    )(*tabs, *ztabs, u2, loc_t, gate_t)


def _experts_kernel(be_ref, nu_ref, xs_ref, wg_ref, wu_ref, wd_ref, ys_ref, *, d):
    del be_ref
    dh = d // 2

    @pl.when(pl.program_id(0) < nu_ref[0])
    def _():
        w = xs_ref[:, 0:dh]
        xa = _bf(lax.bitcast_convert_type(w & jnp.int32(-65536), F32))
        xb = _bf(lax.bitcast_convert_type(lax.shift_left(w, 16), F32))
        gate = lax.bitcast_convert_type(xs_ref[:, dh:dh + LANES], F32)
        hg = _dot(xa, _bf(wg_ref[0, 0:dh, :])) + _dot(xb, _bf(wg_ref[0, dh:d, :]))
        hu = _dot(xa, _bf(wu_ref[0, 0:dh, :])) + _dot(xb, _bf(wu_ref[0, dh:d, :]))
        y = _dot(_bf(jax.nn.silu(hg) * hu), _bf(wd_ref[0])) * jnp.tile(gate, (1, d // LANES))
        yb = lax.bitcast_convert_type(_bf(y).astype(F32), I32)
        ys_ref[...] = yb[:, 0:dh] | lax.shift_right_logical(yb[:, dh:d], 16)


def _experts(blk_expert, n_used, xs, wg, wu, wd):
    rows, xw = xs.shape
    d = wg.shape[1]
    de = wg.shape[2]
    nblk = rows // EXPERT_ROWS
    blk = lambda i, be, nu: jnp.minimum(i, nu[0] - 1)
    return pl.pallas_call(
        functools.partial(_experts_kernel, d=d),
        out_shape=jax.ShapeDtypeStruct((rows, d // 2), I32),
        grid_spec=pltpu.PrefetchScalarGridSpec(
            num_scalar_prefetch=2, grid=(nblk,),
            in_specs=[pl.BlockSpec((EXPERT_ROWS, xw), lambda i, be, nu: (blk(i, be, nu), 0)),
                      pl.BlockSpec((1, d, de), lambda i, be, nu: (be[blk(i, be, nu)], 0, 0)),
                      pl.BlockSpec((1, d, de), lambda i, be, nu: (be[blk(i, be, nu)], 0, 0)),
                      pl.BlockSpec((1, de, d), lambda i, be, nu: (be[blk(i, be, nu)], 0, 0))],
            out_specs=pl.BlockSpec((EXPERT_ROWS, d // 2), lambda i, be, nu: (blk(i, be, nu), 0))),
        compiler_params=_cparams(("arbitrary",)),
        name="experts",
    )(blk_expert, n_used, xs, wg, wu, wd)


def _combine_kernel(n8_ref, src_ref, dst_ref, tot_ref, ys_hbm, loc_ref, h_ref, u2_ref, ada_ref, wsg_ref, wsu_ref, wsd_ref,
                    gf_ref, o_ref, ybuf, sems, *, tt, nslot, d):
    i = pl.program_id(0)
    last = pl.num_programs(0) - 1
    par = i % 2

    def run_copy(slot):
        def piece(s0, d0, rows, prio):
            pltpu.make_async_copy(
                ys_hbm.at[pl.ds(d0, rows)], ybuf.at[slot, pl.ds(s0, rows)], sems.at[slot]).start(priority=prio)
        return piece

    @pl.when(i == 0)
    def _():
        ybuf[...] = jnp.zeros(ybuf.shape, I32)
        _for_each_run_piece(n8_ref, src_ref, dst_ref, 0, RUN_BITS, run_copy(0))

    @pl.when(i < last)
    def _():
        _for_each_run_piece(n8_ref, src_ref, dst_ref, i + 1, RUN_BITS, run_copy(1 - par))

    x = _bf(u2_ref[...])
    shared = _dot(_bf(jax.nn.silu(_dot(x, wsg_ref[...])) * _dot(x, wsu_ref[...])), wsd_ref[...])
    _wait_rows(tot_ref[i], lambda rows: pltpu.make_async_copy(
        ys_hbm.at[pl.ds(0, rows)], ybuf.at[par, pl.ds(0, rows)], sems.at[par]), nslot)

    loc = loc_ref[...]
    dh = d // 2
    routed_a = jnp.zeros((tt, dh), F32)
    routed_b = jnp.zeros((tt, dh), F32)
    cols_b = lax.broadcasted_iota(I32, (tt, SLOT_CHUNK), 1).astype(F32).astype(BF16)
    loc_hi = lax.shift_right_logical(loc, SLOT_CHUNK.bit_length() - 1)
    loc_lo = (loc & (SLOT_CHUNK - 1)).astype(F32)
    one_b = jnp.ones((tt, SLOT_CHUNK), BF16)
    for c in range(nslot // SLOT_CHUNK):
        pick = jnp.zeros((tt, SLOT_CHUNK), BF16)
        for k in range(TOP_K):
            lk = _bf(jnp.where(loc_hi[:, k:k + 1] == c, loc_lo[:, k:k + 1], -1.0))
            pick = jnp.where(cols_b == lk, one_b, pick)
        w = ybuf[par, c * SLOT_CHUNK:(c + 1) * SLOT_CHUNK, :]
        routed_a = routed_a + _dot(pick, _bf(lax.bitcast_convert_type(w & jnp.int32(-65536), F32)))
        routed_b = routed_b + _dot(pick, _bf(lax.bitcast_convert_type(lax.shift_left(w, 16), F32)))
    routed = jnp.concatenate([routed_a, routed_b], axis=1)
    gate2 = ada_ref[0, :, 5 * d:6 * d]
    h = h_ref[...] + gate2 * (routed + shared)
    o_ref[...] = _rms(h, gf_ref[...])


def _combine(tabs, ys, loc, h1, u2, ada3, wsg, wsu, wsd, gf, seq):
    n, d = h1.shape
    tt = min(MOE_TILE, n)
    nslot = TOP_K * tt + N_EXPERTS * SUBLANES
    per_b = seq // tt
    row = pl.BlockSpec((tt, d), lambda i, *_: (i, 0))
    full = lambda a: pl.BlockSpec(a.shape, lambda i, *_: (0,) * a.ndim)
    wsg, wsu, wsd = wsg.astype(BF16), wsu.astype(BF16), wsd.astype(BF16)
    gf = gf.reshape(1, d)
    return pl.pallas_call(
        functools.partial(_combine_kernel, tt=tt, nslot=nslot, d=d),
        out_shape=jax.ShapeDtypeStruct((n, d), F32),
        grid_spec=pltpu.PrefetchScalarGridSpec(
            num_scalar_prefetch=4, grid=(n // tt,),
            in_specs=[pl.BlockSpec(memory_space=pl.ANY),
                      pl.BlockSpec((tt, TOP_K), lambda i, *_: (i, 0)),
                      row, row,
                      pl.BlockSpec((1, 1, ada3.shape[2]), lambda i, *_: (i // per_b, 0, 0)),
                      full(wsg), full(wsu), full(wsd), full(gf)],
            out_specs=row,
            scratch_shapes=[pltpu.VMEM((2, nslot, d // 2), I32), pltpu.SemaphoreType.DMA((2,))]),
        compiler_params=_cparams(("arbitrary",)),
        name="combine",
    )(*tabs, ys, loc, h1, u2, ada3, wsg, wsu, wsd, gf)


def _moe(h1, u2, ada3, w_router, router_bias, wg, wu, wd, wsg, wsu, wsd, gf):
    bsz, seq, d = h1.shape
    n = bsz * seq
    assert seq % min(MOE_TILE, n) == 0
    h1f = h1.reshape(n, d)
    u2f = u2.reshape(n, d)
    gate_t, loc_t, c8, loff, run0, tot = _route(u2f, w_router, router_bias)
    nt = c8.shape[0]
    as_tab = lambda a: a[:, :, 0].astype(I32)
    tot8 = tot[:, 0].astype(I32)
    padded = (tot8 + EXPERT_ROWS - 1) // EXPERT_ROWS * EXPERT_ROWS
    pend = jnp.cumsum(padded)
    pstart = (pend - padded).astype(I32)
    nblk = (n * TOP_K + nt * N_EXPERTS * (SUBLANES - 1) + N_EXPERTS * (EXPERT_ROWS - 1) + EXPERT_ROWS - 1) // EXPERT_ROWS
    blk_row0 = jnp.arange(nblk, dtype=I32) * EXPERT_ROWS
    blk_expert = jnp.minimum(jnp.sum(pend[None, :] <= blk_row0[:, None], axis=1), N_EXPERTS - 1).astype(I32)
    n_used = (pend[-1:] // EXPERT_ROWS).astype(I32)
    n8 = as_tab(c8) // SUBLANES
    tabs = (n8.reshape(-1), as_tab(loff).reshape(-1), (pstart[None, :] + as_tab(run0)).reshape(-1),
            jnp.sum(n8, axis=1))
    ztabs = ((padded - tot8) // SUBLANES, pstart + tot8)
    xs = _dispatch(tabs, ztabs, u2f, loc_t, gate_t, nblk * EXPERT_ROWS)
    ys = _experts(blk_expert, n_used, xs, wg, wu, wd)
    out = _combine(tabs, ys, loc_t.T, h1f, u2f, ada3, wsg, wsu, wsd, gf, seq)
    return out.reshape(bsz, seq, d)


def kernel(x, c, w_ada, b_ada, norm1_g, w_in, ssm_lambda_re, ssm_lambda_im, ssm_log_dt, ssm_b_re, ssm_b_im,
           ssm_c_re, ssm_c_im, ssm_d, ssm_w_glu, ssm_b_glu, w_proj_ssm, w_proj_attn, w_out, norm2_g, w_router,
           router_bias, w_exp_gate, w_exp_up, w_exp_down, w_sh_gate, w_sh_up, w_sh_down, norm_f_g):
    depth = w_ada.shape[0]
    assert depth == 1, "the final norm is fused into the last (only) layer's combine kernel"
    bsz, seq, d = x.shape
    layer = 0
    ada3 = _ada(c, w_ada[layer], b_ada[layer]).reshape(bsz, 1, 6 * d)
    us, k, ki, qt, qit, vt, wit = _inproj(x, ada3, norm1_g[layer], w_in[layer])
    a_re, a_im, bb_re, bb_im = _s5disc(ssm_lambda_re[layer], ssm_lambda_im[layer], ssm_log_dt[layer],
                                       ssm_b_re[layer], ssm_b_im[layer])
    ys_t = _s5(us.transpose(1, 0, 2), a_re, a_im, bb_re, bb_im, ssm_c_re[layer], ssm_c_im[layer],
               ssm_d[layer], ssm_w_glu[layer], ssm_b_glu[layer])
    ya = _dsa(qt, qit, wit, k, ki, vt)
    h1, u2 = _mix(x, ys_t.transpose(1, 0, 2), ya, ada3, norm1_g[layer], w_in[layer], w_proj_ssm[layer],
                  w_proj_attn[layer], w_out[layer], norm2_g[layer])
    return _moe(h1, u2, ada3, w_router[layer], router_bias[layer], w_exp_gate[layer], w_exp_up[layer],
                w_exp_down[layer], w_sh_gate[layer], w_sh_up[layer], w_sh_down[layer], norm_f_g)
```

```python
import functools
import math

import jax
import jax.numpy as jnp
import numpy as np
from jax import lax
from jax.experimental import pallas as pl
from jax.experimental.pallas import tpu as pltpu

F32 = jnp.float32
BF16 = jnp.bfloat16
I32 = jnp.int32

SSM_GROUP = 16
SSM_STATE = 64
N_HEADS = 8
HEAD_DIM = 64
IDX_HEADS = 8
IDX_DIM = 64
TOPK_MAX = 256
N_EXPERTS = 64
TOP_K = 8
N_GROUPS = 8
TOPK_GROUPS = 4
ROUTED_SCALE = 2.5
EPS = 1e-6

V7X_VMEM_LIMIT_BYTES = 56 * 1024 * 1024
LANES = 128
SUBLANES = 8

INPROJ_ROWS = 256
S5_STEPS = 64
S5_LANE_CHUNK = 128
DSA_Q_COLS = 256
DSA_K_ROWS = 512
DSA_COUNT_ROWS = 64
BITSLICE_ROWS = 256
POS_SPLIT = 64
MIX_ROWS = 512
MOE_TILE = 256
ROUTE_TILES = 2
SLOT_CHUNK = 256
EXPERT_ROWS = 1024

NEG_BIG = -1e30
INT_MIN = -(2 ** 31)


def _cparams(sem):
    return pltpu.CompilerParams(dimension_semantics=sem, vmem_limit_bytes=V7X_VMEM_LIMIT_BYTES)


def _bf(x):
    return x.astype(BF16)


def _dot(a, b):
    return jnp.dot(a, b, preferred_element_type=F32)


def _dot_nt(a, b):
    return lax.dot_general(a, b, (((1,), (1,)), ((), ())), preferred_element_type=F32)


def _split(x):
    hi = _bf(x)
    lo = _bf(x - hi.astype(F32))
    return hi, lo


def _dot3(a, b):
    ah, al = _split(a)
    bh, bl = _split(b)
    return _dot(ah, bh) + (_dot(ah, bl) + _dot(al, bh))


def _rms(x, g):
    return x * lax.rsqrt(jnp.mean(x * x, axis=-1, keepdims=True) + EPS) * g


def _ada_kernel(c_ref, w_ref, b_ref, o_ref):
    c = c_ref[...]
    o_ref[...] = _dot3(c * jax.nn.sigmoid(c), w_ref[...]) + b_ref[...]


def _ada(c, w, b):
    bsz, d = c.shape
    n = w.shape[1]
    tn = 1024
    return pl.pallas_call(
        _ada_kernel,
        out_shape=jax.ShapeDtypeStruct((bsz, n), F32),
        grid=(n // tn,),
        in_specs=[pl.BlockSpec((bsz, d), lambda j: (0, 0)),
                  pl.BlockSpec((d, tn), lambda j: (0, j)),
                  pl.BlockSpec((1, tn), lambda j: (0, j))],
        out_specs=pl.BlockSpec((bsz, tn), lambda j: (0, j)),
        compiler_params=_cparams(("arbitrary",)),
        name="ada",
    )(c, w, b.reshape(1, n))


ALIBI_SLOPES = tuple(2.0 ** (-8.0 * (h + 1) / N_HEADS) for h in range(N_HEADS))
QAUG_ROWS = 16


def _inproj_kernel(x_ref, ada_ref, g1_ref, w_ref, wt_ref,
                   us_ref, k_ref, ki_ref, qt_ref, qit_ref, vt_ref, wit_ref, *, d, ssm_w, attn_w, idx_w, tl):
    x = x_ref[0]
    shift = ada_ref[0, :, 0:d]
    scale = ada_ref[0, :, d:2 * d]
    u = _bf(_rms(x, g1_ref[...]) * (1.0 + scale) + shift)
    r = _dot(u, w_ref[...])
    us_ref[0] = r[:, 0:ssm_w]
    k_ref[0] = _bf(r[:, ssm_w:ssm_w + HEAD_DIM])
    ki_ref[0] = _bf(r[:, ssm_w + LANES:ssm_w + LANES + IDX_DIM])
    rt = _dot_nt(wt_ref[...], u)
    arow = lax.broadcasted_iota(I32, (QAUG_ROWS, tl), 0)
    for h in range(N_HEADS):
        base = h * LANES
        qt_ref[0, base:base + HEAD_DIM, :] = _bf(rt[h * HEAD_DIM:(h + 1) * HEAD_DIM])
        qt_ref[0, base + HEAD_DIM:base + HEAD_DIM + QAUG_ROWS, :] = _bf(jnp.where(arow < 2, ALIBI_SLOPES[h], 0.0))
        qt_ref[0, base + HEAD_DIM + QAUG_ROWS:base + LANES, :] = jnp.zeros((LANES - HEAD_DIM - QAUG_ROWS, tl), BF16)
    qit_ref[0] = _bf(rt[attn_w:attn_w + idx_w])
    vt_ref[0] = _bf(rt[attn_w + idx_w:attn_w + idx_w + HEAD_DIM])
    wit_ref[0] = rt[attn_w + idx_w + HEAD_DIM:attn_w + idx_w + HEAD_DIM + IDX_HEADS]


def _split_w_in(w_in, d):
    ssm_w = 512
    sizes = (ssm_w, N_HEADS * HEAD_DIM, HEAD_DIM, HEAD_DIM, IDX_HEADS * IDX_DIM, IDX_DIM, IDX_HEADS, d, d)
    offs = [0]
    for s in sizes:
        offs.append(offs[-1] + s)
    return [w_in[:, offs[i]:offs[i + 1]] for i in range(9)]


def _inproj(x, ada3, g1, w_in):
    bsz, seq, d = x.shape
    ssm_w = 512
    attn_w = N_HEADS * HEAD_DIM
    idx_w = IDX_HEADS * IDX_DIM
    w_ssm, w_q, w_k, w_v, w_qi, w_ki, w_wi, _, _ = _split_w_in(w_in, d)
    zpad = lambda n: jnp.zeros((d, n), F32)
    wbig = jnp.concatenate([w_ssm, w_k, zpad(LANES - HEAD_DIM), w_ki, zpad(LANES - IDX_DIM)], axis=1).astype(BF16)
    wt = jnp.concatenate([w_q * (HEAD_DIM ** -0.5), w_qi * (IDX_DIM ** -0.5), w_v, w_wi,
                          zpad(LANES - HEAD_DIM - IDX_HEADS)], axis=1).T.astype(BF16)
    tl = INPROJ_ROWS
    kern = functools.partial(_inproj_kernel, d=d, ssm_w=ssm_w, attn_w=attn_w, idx_w=idx_w, tl=tl)
    row = lambda w: pl.BlockSpec((1, tl, w), lambda b, l: (b, l, 0))
    colt = lambda h: pl.BlockSpec((1, h, tl), lambda b, l: (b, 0, l))
    full = lambda a: pl.BlockSpec(a.shape, lambda b, l: (0,) * a.ndim)
    return pl.pallas_call(
        kern,
        out_shape=(jax.ShapeDtypeStruct((bsz, seq, ssm_w), F32),
                   jax.ShapeDtypeStruct((bsz, seq, HEAD_DIM), BF16),
                   jax.ShapeDtypeStruct((bsz, seq, IDX_DIM), BF16),
                   jax.ShapeDtypeStruct((bsz, N_HEADS * LANES, seq), BF16),
                   jax.ShapeDtypeStruct((bsz, idx_w, seq), BF16),
                   jax.ShapeDtypeStruct((bsz, HEAD_DIM, seq), BF16),
                   jax.ShapeDtypeStruct((bsz, IDX_HEADS, seq), F32)),
        grid=(bsz, seq // tl),
        in_specs=[row(d),
                  pl.BlockSpec((1, 1, ada3.shape[2]), lambda b, l: (b, 0, 0)),
                  pl.BlockSpec((1, d), lambda b, l: (0, 0)),
                  full(wbig), full(wt)],
        out_specs=(row(ssm_w), row(HEAD_DIM), row(IDX_DIM),
                   colt(N_HEADS * LANES), colt(idx_w), colt(HEAD_DIM), colt(IDX_HEADS)),
        compiler_params=_cparams(("arbitrary", "arbitrary")),
        name="inproj",
    )(x, ada3, g1.reshape(1, d), wbig, wt)


def _s5disc_kernel(lr_ref, li_ref, ldt_ref, br_ref, bi_ref, are_ref, aim_ref, bbr_ref, bbi_ref):
    lr = lr_ref[...]
    li = li_ref[...]
    dt = jnp.exp(ldt_ref[...])
    mag = jnp.exp(lr * dt)
    a_re = mag * jnp.cos(li * dt)
    a_im = mag * jnp.sin(li * dt)
    den = lr * lr + li * li
    n_re = a_re - 1.0
    f_re = (n_re * lr + a_im * li) / den
    f_im = (a_im * lr - n_re * li) / den
    br = br_ref[...]
    bi = bi_ref[...]
    are_ref[...] = a_re
    aim_ref[...] = a_im
    bbr_ref[...] = f_re * br - f_im * bi
    bbi_ref[...] = f_re * bi + f_im * br


def _s5disc(lam_re, lam_im, log_dt, b_re, b_im):
    g, p = lam_re.shape
    h = b_re.shape[2]
    rep = lambda a: jnp.repeat(a, h, axis=1)
    ldt = jnp.broadcast_to(log_dt[:, None], (g, p * h))
    sds = jax.ShapeDtypeStruct((g, p * h), F32)
    a_re, a_im, bb_re, bb_im = pl.pallas_call(
        _s5disc_kernel, out_shape=(sds, sds, sds, sds), name="s5disc",
    )(rep(lam_re), rep(lam_im), ldt, b_re.reshape(g, p * h), b_im.reshape(g, p * h))
    return a_re[:, ::h], a_im[:, ::h], bb_re.reshape(g, p, h), bb_im.reshape(g, p, h)


def _s5_kernel(u_ref, wb_ref, ar_ref, ai_ref, cc_ref, dsk_ref, wg_ref, bg_ref, o_ref, buf, hst, *, tl, width):
    nch = width // S5_LANE_CHUNK
    sw = S5_LANE_CHUNK // SSM_GROUP * SSM_STATE
    rows = tl * SUBLANES

    @pl.when(pl.program_id(0) == 0)
    def _():
        hst[...] = jnp.zeros_like(hst)

    u = u_ref[...].reshape(rows, width)
    ub = _bf(u)
    for j in range(nch):
        buf[:, j * 2 * sw:(j + 1) * 2 * sw] = _dot(ub[:, j * S5_LANE_CHUNK:(j + 1) * S5_LANE_CHUNK], wb_ref[j])

    for j in range(nch):
        re_cols = slice(j * 2 * sw, j * 2 * sw + sw)
        im_cols = slice(j * 2 * sw + sw, (j + 1) * 2 * sw)
        a_re = jnp.broadcast_to(ar_ref[:, j * sw:(j + 1) * sw], (SUBLANES, sw))
        a_im = jnp.broadcast_to(ai_ref[:, j * sw:(j + 1) * sw], (SUBLANES, sw))

        def step(t, carry, re_cols=re_cols, im_cols=im_cols, a_re=a_re, a_im=a_im):
            h_re, h_im = carry
            r0 = pl.multiple_of(t * SUBLANES, SUBLANES)
            n_re = (a_re * h_re - a_im * h_im) + buf[pl.ds(r0, SUBLANES), re_cols]
            n_im = (a_re * h_im + a_im * h_re) + buf[pl.ds(r0, SUBLANES), im_cols]
            buf[pl.ds(r0, SUBLANES), re_cols] = n_re
            buf[pl.ds(r0, SUBLANES), im_cols] = n_im
            return n_re, n_im

        h_re, h_im = lax.fori_loop(0, tl, step, (hst[:, re_cols], hst[:, im_cols]), unroll=True)
        hst[:, re_cols] = h_re
        hst[:, im_cols] = h_im

    ys = [_dot(_bf(buf[:, j * 2 * sw:(j + 1) * 2 * sw]), cc_ref[j]) for j in range(nch)]
    y = jnp.concatenate(ys, axis=1) + dsk_ref[...] * u
    y = jax.nn.gelu(y)
    y = y * jax.nn.sigmoid(_dot(_bf(y), wg_ref[...]) + bg_ref[...])
    o_ref[...] = _bf(y).reshape(tl, SUBLANES, width)


def _s5(u_t, a_re, a_im, bb_re, bb_im, c_re, c_im, d_skip, w_glu, b_glu):
    seq, bsz, width = u_t.shape
    assert bsz == SUBLANES
    nch = width // S5_LANE_CHUNK
    gpc = S5_LANE_CHUNK // SSM_GROUP
    sw = gpc * SSM_STATE
    eye = jnp.eye(gpc, dtype=F32)

    def bmat(bb):
        t = bb.reshape(nch, gpc, SSM_STATE, SSM_GROUP).transpose(0, 1, 3, 2)
        return jnp.einsum('jghp,gk->jghkp', t, eye).reshape(nch, S5_LANE_CHUNK, sw)

    def cmat(cc):
        t = cc.reshape(nch, gpc, SSM_GROUP, SSM_STATE).transpose(0, 1, 3, 2)
        return jnp.einsum('jgph,gk->jgpkh', t, eye).reshape(nch, sw, S5_LANE_CHUNK)

    wb = jnp.concatenate([bmat(bb_re), bmat(bb_im)], axis=2)
    cc = jnp.concatenate([cmat(c_re), -cmat(c_im)], axis=1)
    tl = S5_STEPS
    full = lambda a: pl.BlockSpec(a.shape, lambda i: (0,) * a.ndim)
    args = (u_t, wb.astype(BF16), a_re.reshape(1, -1), a_im.reshape(1, -1), cc.astype(BF16),
            d_skip.reshape(1, width), w_glu.astype(BF16), b_glu.reshape(1, width))
    return pl.pallas_call(
        functools.partial(_s5_kernel, tl=tl, width=width),
        out_shape=jax.ShapeDtypeStruct((seq, bsz, width), BF16),
        grid=(seq // tl,),
        in_specs=[pl.BlockSpec((tl, bsz, width), lambda i: (i, 0, 0))] + [full(a) for a in args[1:]],
        out_specs=pl.BlockSpec((tl, bsz, width), lambda i: (i, 0, 0)),
        scratch_shapes=[pltpu.VMEM((tl * SUBLANES, nch * 2 * sw), F32),
                        pltpu.VMEM((SUBLANES, nch * 2 * sw), F32)],
        compiler_params=_cparams(("arbitrary",)),
        name="s5",
    )(*args)


def _bit_transpose32(words):
    x = list(words)
    j, m = 16, 0x0000FFFF
    while j:
        k = 0
        while k < 32:
            t = (x[k] ^ lax.shift_right_logical(x[k + j], jnp.int32(j))) & jnp.int32(m - (1 << 32) if m >= 1 << 31 else m)
            x[k] = x[k] ^ t
            x[k + j] = x[k + j] ^ lax.shift_left(t, jnp.int32(j))
            k = (k + j + 1) & ~j
        j >>= 1
        m = (m ^ (m << j)) & 0xFFFFFFFF
    return x


def _dsa_kernel(qt_ref, qit_ref, wit_ref, ka_ref, ki_ref, vt_ref, o_ref, key_s, mb_s, acc_s, pl_s, p_s, *, tq, tk, topk, seq):
    i = pl.program_id(1)
    q0 = i * tq
    nkt = (q0 + tq + tk - 1) // tk
    ch = DSA_COUNT_ROWS
    krow = lax.broadcasted_iota(I32, (tk, tq), 0)
    qcol = q0 + lax.broadcasted_iota(I32, (tk, tq), 1)
    crow = lax.broadcasted_iota(I32, (ch, tq), 0)

    wb = wit_ref[0] * (IDX_HEADS ** -0.5)

    def score_tile(j, _):
        r0 = pl.multiple_of(j * tk, tk)
        kit = ki_ref[0, pl.ds(r0, tk), :]
        acc = jnp.zeros((tk, tq), F32)
        for h in range(IDX_HEADS):
            s = _dot(kit, qit_ref[0, h * IDX_DIM:(h + 1) * IDX_DIM, :])
            acc = acc + wb[h:h + 1, :] * jnp.maximum(s, 0.0)
        bits = lax.bitcast_convert_type(acc, I32)
        key = jnp.where(bits < 0, bits ^ jnp.int32(0x7FFFFFFF), bits)
        key = jnp.where(acc == 0.0, 0, key)
        key = jnp.where(krow + r0 <= qcol, key, INT_MIN)
        key_s[pl.ds(r0, tk), :] = key
        ukey = key ^ INT_MIN
        for c in range(tk // BITSLICE_ROWS):
            words = [ukey[c * BITSLICE_ROWS + v * SUBLANES:c * BITSLICE_ROWS + (v + 1) * SUBLANES, :]
                     for v in range(32)]
            planes = _bit_transpose32(words)
            g0 = pl.multiple_of((j * (tk // BITSLICE_ROWS) + c) * SUBLANES, SUBLANES)
            for it in range(32):
                pl_s[it, pl.ds(g0, SUBLANES), :] = planes[it]
        return 0

    @pl.when((pl.program_id(0) == 0) & (i == 0))
    def _():
        pl_s[...] = jnp.zeros(pl_s.shape, I32)

    lax.fori_loop(0, nkt, score_tile, 0)

    def count(pred):
        def tile(j, cnt):
            for c in range(tk // ch):
                rr = pl.multiple_of(j * tk + c * ch, ch)
                cnt = cnt + jnp.where(pred(key_s[pl.ds(rr, ch), :], rr), 1, 0)
            return cnt
        cnt = lax.fori_loop(0, nkt, tile, jnp.zeros((ch, tq), I32))
        return jnp.sum(cnt.astype(F32), axis=0, keepdims=True)

    ngrp = seq // 32

    def lane_count(words):
        pc = lax.population_count(words).reshape(ngrp // SUBLANES, SUBLANES, tq)
        return jnp.sum(jnp.sum(pc, axis=0).astype(F32), axis=0, keepdims=True)

    def bit_step(it, carry):
        alive, above, ans_u = carry
        ones = alive & pl_s[it]
        cnt1 = lane_count(ones)
        take = above + cnt1 >= float(topk)
        alive = jnp.where(take, ones, alive ^ ones)
        above = jnp.where(take, above, above + cnt1)
        ans_u = jnp.where(take, ans_u | lax.shift_left(jnp.int32(1), 31 - it), ans_u)
        return alive, above, ans_u

    grow = lax.broadcasted_iota(I32, (ngrp, tq), 0)
    alive0 = jnp.where(grow < nkt * (tk // 32), -1, 0)
    alive, above, ans_u = lax.fori_loop(
        0, 32, bit_step, (alive0, jnp.zeros((1, tq), F32), jnp.zeros((1, tq), I32)))
    thr = jnp.maximum(ans_u ^ INT_MIN, INT_MIN + 1)
    cnt_ge = above + lane_count(alive)
    tied = jnp.where(ans_u != 0, cnt_ge, 0.0) > float(topk)
    has_ties = jnp.max(jnp.where(tied, 1.0, 0.0)) > 0.0

    def tie_cut():
        need = float(topk) - count(lambda kb, rr: kb > thr)
        nbits = max(1, (seq - 1).bit_length())

        def idx_step(b, x):
            cand = x | lax.shift_left(jnp.int32(1), nbits - 1 - b)
            below = count(lambda kb, rr: jnp.where(kb == thr, crow + rr, seq) < cand)
            return jnp.where(below < need, cand, x)

        x = lax.fori_loop(0, nbits, idx_step, jnp.zeros((1, tq), I32))
        return jnp.where(tied, x, seq)

    cut = lax.cond(has_ties, tie_cut, lambda: jnp.full((1, tq), seq, I32))

    def bias_tile(j, _):
        for c in range(tk // ch):
            rr = pl.multiple_of(j * tk + c * ch, ch)
            kb = key_s[pl.ds(rr, ch), :]
            tie_bias = jnp.where(crow + rr <= cut, 0.0, NEG_BIG)
            mb_s[pl.ds(rr, ch), :] = jnp.where(kb > thr, 0.0, jnp.where(kb == thr, tie_bias, NEG_BIG))
        return 0

    lax.fori_loop(0, nkt, bias_tile, 0)

    def logits(j, h):
        r0 = pl.multiple_of(j * tk, tk)
        s = _dot(ka_ref[0, pl.ds(r0, tk), :], qt_ref[0, h * LANES:(h + 1) * LANES, :]) + mb_s[pl.ds(r0, tk), :]
        return s.reshape(tk // SUBLANES, SUBLANES, tq)

    acc_s[...] = jnp.zeros(acc_s.shape, F32)

    def attn_tile(j, carry):
        ms, ls = carry
        r0 = pl.multiple_of(j * tk, tk)
        new_m, new_l, alphas = [], [], []
        for h in range(N_HEADS):
            s = logits(j, h)
            m_new = jnp.maximum(ms[h], jnp.max(jnp.max(s, axis=0), axis=0, keepdims=True))
            alpha = jnp.exp(ms[h] - m_new)
            p = jnp.exp(s - m_new)
            new_m.append(m_new)
            new_l.append(alpha * ls[h] + jnp.sum(p, axis=0))
            alphas.append(alpha)
            p_s[h] = _bf(p.reshape(tk, tq))
        for h in range(N_HEADS):
            rows = slice(h * HEAD_DIM, (h + 1) * HEAD_DIM)
            acc_s[rows, :] = alphas[h] * acc_s[rows, :] + _dot(vt_ref[0, :, pl.ds(r0, tk)], p_s[h])
        return tuple(new_m), tuple(new_l)

    init = ((jnp.full((1, tq), NEG_BIG, F32),) * N_HEADS, (jnp.zeros((SUBLANES, tq), F32),) * N_HEADS)
    _, ls = lax.fori_loop(0, nkt, attn_tile, init)
    for h in range(N_HEADS):
        rows = slice(h * HEAD_DIM, (h + 1) * HEAD_DIM)
        acc_s[rows, :] = acc_s[rows, :] / jnp.sum(ls[h], axis=0, keepdims=True)
    o_ref[0] = _bf(acc_s[...].T)


def _dsa(qt, qit, wit, k, ki, vt):
    bsz, seq = k.shape[0], k.shape[1]
    aw = N_HEADS * HEAD_DIM
    tq = min(DSA_Q_COLS, seq)
    tk = min(DSA_K_ROWS, seq)
    topk = min(TOPK_MAX, seq // 4)
    assert (seq - 1) // POS_SPLIT < 256 and POS_SPLIT <= 256, "key positions must split into two bf16-exact parts"
    assert all(float(np.float32(sl).astype(BF16)) == sl for sl in ALIBI_SLOPES), "ALiBi slopes must be bf16-exact"
    pos = jnp.arange(seq, dtype=I32)
    posc = jnp.stack([(pos // POS_SPLIT) * POS_SPLIT, pos % POS_SPLIT], axis=1).astype(BF16)
    ka = jnp.concatenate([k, jnp.broadcast_to(posc[None], (bsz, seq, 2)),
                          jnp.zeros((bsz, seq, LANES - HEAD_DIM - 2), BF16)], axis=2)
    kern = functools.partial(_dsa_kernel, tq=tq, tk=tk, topk=topk, seq=seq)
    cols = lambda r: pl.BlockSpec((1, r, tq), lambda b, i: (b, 0, i))
    return pl.pallas_call(
        kern,
        out_shape=jax.ShapeDtypeStruct((bsz, seq, aw), BF16),
        grid=(bsz, seq // tq),
        in_specs=[cols(N_HEADS * LANES), cols(qit.shape[1]), cols(IDX_HEADS),
                  pl.BlockSpec((1, seq, LANES), lambda b, i: (b, 0, 0)),
                  pl.BlockSpec((1, seq, IDX_DIM), lambda b, i: (b, 0, 0)),
                  pl.BlockSpec((1, HEAD_DIM, seq), lambda b, i: (b, 0, 0))],
        out_specs=pl.BlockSpec((1, tq, aw), lambda b, i: (b, i, 0)),
        scratch_shapes=[pltpu.VMEM((seq, tq), I32), pltpu.VMEM((seq, tq), F32), pltpu.VMEM((aw, tq), F32),
                        pltpu.VMEM((32, seq // 32, tq), I32), pltpu.VMEM((N_HEADS, tk, tq), BF16)],
        compiler_params=_cparams(("arbitrary", "arbitrary")),
        name="dsa",
    )(qt, qit, wit, ka, ki, vt)


def _mix_kernel(x_ref, ys_ref, ya_ref, ada_ref, g1_ref, wgt_ref, wps_ref, wpa_ref, wo_ref, g2_ref,
                h_ref, u2_ref, *, d):
    gate1 = ada_ref[0, :, 2 * d:3 * d]
    shift2 = ada_ref[0, :, 3 * d:4 * d]
    scale2 = ada_ref[0, :, 4 * d:5 * d]
    x = x_ref[0]
    u = _bf(_rms(x, g1_ref[...]) * (1.0 + ada_ref[0, :, d:2 * d]) + ada_ref[0, :, 0:d])
    g = _dot(u, wgt_ref[...])
    mixed = (jax.nn.sigmoid(g[:, 0:d]) * _dot(ys_ref[0], wps_ref[...])
             + jax.nn.sigmoid(g[:, d:2 * d]) * _dot(ya_ref[0], wpa_ref[...]))
    h = x + gate1 * _dot(_bf(mixed), wo_ref[...])
    h_ref[0] = h
    u2_ref[0] = _rms(h, g2_ref[...]) * (1.0 + scale2) + shift2


def _mix(x, ys, ya, ada3, g1, w_in, wps, wpa, wo, g2):
    bsz, seq, d = x.shape
    tm = MIX_ROWS
    row = lambda w: pl.BlockSpec((1, tm, w), lambda b, l: (b, l, 0))
    full = lambda a: pl.BlockSpec(a.shape, lambda b, l: (0,) * a.ndim)
    wps, wpa, wo = wps.astype(BF16), wpa.astype(BF16), wo.astype(BF16)
    wgt = jnp.concatenate(_split_w_in(w_in, d)[7:9], axis=1).astype(BF16)
    g1 = g1.reshape(1, d)
    g2 = g2.reshape(1, d)
    return pl.pallas_call(
        functools.partial(_mix_kernel, d=d),
        out_shape=(jax.ShapeDtypeStruct((bsz, seq, d), F32), jax.ShapeDtypeStruct((bsz, seq, d), F32)),
        grid=(bsz, seq // tm),
        in_specs=[row(d), row(ys.shape[2]), row(ya.shape[2]),
                  pl.BlockSpec((1, 1, ada3.shape[2]), lambda b, l: (b, 0, 0)),
                  full(g1), full(wgt), full(wps), full(wpa), full(wo), full(g2)],
        out_specs=(row(d), row(d)),
        compiler_params=_cparams(("arbitrary", "arbitrary")),
        name="mix",
    )(x, ys, ya, ada3, g1, wgt, wps, wpa, wo, g2)


def _first_max(cur, idx, axis, big):
    m = jnp.max(cur, axis=axis, keepdims=True)
    first = jnp.min(jnp.where(cur == m, idx, big), axis=axis, keepdims=True)
    return m, idx == first


def _route_kernel(u_ref, wrh_ref, wrl_ref, rb_ref, tri_ref, ltri_ref,
                  gt_ref, loc_ref, c8_ref, loff_ref, run0_ref, tot_ref, run_s, *, t, tt):
    @pl.when(pl.program_id(0) == 0)
    def _():
        run_s[...] = jnp.zeros_like(run_s)

    uh, ul = _split(u_ref[...])
    logits = _dot_nt(wrh_ref[...], uh) + (_dot_nt(wrl_ref[...], uh) + _dot_nt(wrh_ref[...], ul))
    scores = jax.nn.sigmoid(logits)
    biased = scores + rb_ref[...]
    per_group = N_EXPERTS // N_GROUPS
    b3 = biased.reshape(N_GROUPS, per_group, t)
    i3 = lax.broadcasted_iota(I32, b3.shape, 1)
    m1, hit1 = _first_max(b3, i3, 1, per_group)
    m2 = jnp.max(jnp.where(hit1, -jnp.inf, b3), axis=1, keepdims=True)
    gs = (m1 + m2).reshape(N_GROUPS, t)
    gi = lax.broadcasted_iota(I32, gs.shape, 0)
    gsel = jnp.zeros(gs.shape, F32)
    for _ in range(TOPK_GROUPS):
        _, hit = _first_max(gs, gi, 0, N_GROUPS)
        gsel = jnp.where(hit, 1.0, gsel)
        gs = jnp.where(hit, -jnp.inf, gs)
    cur = jnp.where(gsel.reshape(N_GROUPS, 1, t) > 0.0, b3, -jnp.inf).reshape(N_EXPERTS, t)
    ei = lax.broadcasted_iota(I32, cur.shape, 0)
    hits = []
    gates = []
    for _ in range(TOP_K):
        _, hit = _first_max(cur, ei, 0, N_EXPERTS)
        hits.append(hit)
        gates.append(jnp.sum(jnp.where(hit, scores, 0.0), axis=0, keepdims=True))
        cur = jnp.where(hit, -jnp.inf, cur)
    gate = jnp.concatenate(gates, axis=0)
    gt_ref[...] = gate / jnp.sum(gate, axis=0, keepdims=True) * ROUTED_SCALE
    onehot = jnp.zeros(cur.shape, F32)
    for hit in hits:
        onehot = jnp.where(hit, 1.0, onehot)
    for sub in range(t // tt):
        cols = slice(sub * tt, (sub + 1) * tt)
        oh = onehot[:, cols]
        cnt = jnp.sum(oh, axis=1, keepdims=True)
        c8 = jnp.floor((cnt + (SUBLANES - 1)) * (1.0 / SUBLANES)) * SUBLANES
        c8l = jnp.broadcast_to(c8, (N_EXPERTS, LANES))
        loff = _dot(ltri_ref[...], _bf(c8l))
        slot = _dot(_bf(oh), tri_ref[...]) + loff[:, 0:1]
        loc_ref[:, cols] = jnp.concatenate(
            [jnp.sum(jnp.where(hit[:, cols], slot, 0.0), axis=0, keepdims=True) for hit in hits],
            axis=0).astype(I32)
        c8_ref[sub] = c8l
        loff_ref[sub] = loff
        run0_ref[sub] = run_s[...]
        run_s[...] = run_s[...] + c8
    tot_ref[...] = run_s[...]


def _route(u2, w_router, router_bias):
    n, d = u2.shape
    tt = min(MOE_TILE, n)
    t = min(ROUTE_TILES * tt, n)
    nt = n // tt
    wt = w_router.T
    wrh = wt.astype(BF16)
    wrl = (wt - wrh.astype(F32)).astype(BF16)
    tri = (jnp.arange(tt)[:, None] < jnp.arange(tt)[None, :]).astype(BF16)
    ex = jnp.arange(N_EXPERTS)
    ltri = (ex[None, :] < ex[:, None]).astype(BF16)
    full = lambda a: pl.BlockSpec(a.shape, lambda i: (0,) * a.ndim)
    col = pl.BlockSpec((TOP_K, t), lambda i: (0, i))
    tab = pl.BlockSpec((t // tt, N_EXPERTS, LANES), lambda i: (i, 0, 0))
    tab_sds = jax.ShapeDtypeStruct((nt, N_EXPERTS, LANES), F32)
    rb = router_bias.reshape(N_EXPERTS, 1)
    return pl.pallas_call(
        functools.partial(_route_kernel, t=t, tt=tt),
        out_shape=(jax.ShapeDtypeStruct((TOP_K, n), F32), jax.ShapeDtypeStruct((TOP_K, n), I32),
                   tab_sds, tab_sds, tab_sds, jax.ShapeDtypeStruct((N_EXPERTS, LANES), F32)),
        grid=(n // t,),
        in_specs=[pl.BlockSpec((t, d), lambda i: (i, 0)), full(wrh), full(wrl), full(rb), full(tri), full(ltri)],
        out_specs=(col, col, tab, tab, tab, pl.BlockSpec((N_EXPERTS, LANES), lambda i: (0, 0))),
        scratch_shapes=[pltpu.VMEM((N_EXPERTS, LANES), F32)],
        compiler_params=_cparams(("arbitrary",)),
        name="route",
    )(u2, wrh, wrl, rb, tri, ltri)


RUN_BITS = tuple(1 << b for b in reversed(range((MOE_TILE // SUBLANES).bit_length())))
RUN_LONG = 8


def _for_each_run_piece(n8_ref, src_ref, dst_ref, tile, bits, fn):
    def per_expert(e, _):
        idx = tile * N_EXPERTS + e
        n8 = n8_ref[idx]
        src = src_ref[idx]
        dst = dst_ref[idx]
        def pieces(some_bits):
            for p in some_bits:
                off = (n8 & ~(2 * p - 1)) * SUBLANES

                @pl.when((n8 & p) != 0)
                def _(p=p, off=off):
                    fn(pl.multiple_of(src + off, SUBLANES), pl.multiple_of(dst + off, SUBLANES), p * SUBLANES,
                       bits.index(p) % 2)

        long_bits = tuple(p for p in bits if p >= RUN_LONG)
        if long_bits:
            pl.when(n8 >= RUN_LONG)(lambda: pieces(long_bits))
        pieces(tuple(p for p in bits if p < RUN_LONG))
        return 0

    lax.fori_loop(0, N_EXPERTS, per_expert, 0)


def _issue_runs_inline(n8_ref, src_ref, dst_ref, tile, enable, fn, experts=range(N_EXPERTS)):
    for e in experts:
        idx = tile * N_EXPERTS + e
        n8 = jnp.where(enable, n8_ref[idx], 0)
        src = src_ref[idx]
        dst = dst_ref[idx]
        for b, p in enumerate(RUN_BITS):
            off = (n8 & ~(2 * p - 1)) * SUBLANES

            @pl.when((n8 & p) != 0)
            def _(b=b, p=p, off=off, src=src, dst=dst):
                fn(pl.multiple_of(src + off, SUBLANES), pl.multiple_of(dst + off, SUBLANES), p * SUBLANES, b % 2)


def _wait_rows(n8, make_copy, max_rows):
    for p in tuple(1 << b for b in reversed(range((max_rows // SUBLANES).bit_length()))):
        @pl.when((n8 & p) != 0)
        def _(p=p):
            make_copy(p * SUBLANES).wait()


def _dispatch_kernel(n8_ref, src_ref, dst_ref, tot_ref, zn8_ref, zdst_ref, u_ref, loc_ref, gate_ref, xs_hbm,
                     lbuf0, lbuf1, zx, sems, *, tt, nslot, dh):
    i = pl.program_id(0)
    last = pl.num_programs(0) - 1
    bufs = (lbuf0, lbuf1)

    def copy_from(slot):
        def piece(s0, d0, rows, prio):
            pltpu.make_async_copy(
                bufs[slot].at[pl.ds(s0, rows)], xs_hbm.at[pl.ds(d0, rows)], sems.at[slot]).start(priority=prio)
        return piece

    def wait_tile(tile, slot):
        _wait_rows(tot_ref[tile], lambda rows: pltpu.make_async_copy(
            bufs[slot].at[pl.ds(0, rows)], xs_hbm.at[pl.ds(0, rows)], sems.at[slot]), nslot)

    @pl.when(i == 0)
    def _():
        zx[...] = jnp.zeros(zx.shape, I32)

        def zero_piece(s0, d0, rows, prio):
            cx = pltpu.make_async_copy(zx.at[pl.ds(0, rows)], xs_hbm.at[pl.ds(d0, rows)], sems.at[2])
            cx.start()
            cx.wait()

        zbits = tuple(1 << b for b in reversed(range((EXPERT_ROWS // SUBLANES - 1).bit_length())))
        _for_each_run_piece(zn8_ref, zdst_ref, zdst_ref, 0, zbits, zero_piece)

    def build(buf):
        ub = _bf(u_ref[...])
        ones = jnp.ones((tt, LANES), BF16)
        loc = loc_ref[...]
        gate = gate_ref[...]
        rows_b = lax.broadcasted_iota(I32, (SLOT_CHUNK, tt), 0).astype(F32).astype(BF16)
        loc_hi = lax.shift_right_logical(loc, SLOT_CHUNK.bit_length() - 1)
        loc_lo = (loc & (SLOT_CHUNK - 1)).astype(F32)
        gate_h = _bf(gate)
        gate_l = _bf(gate - gate_h.astype(F32))
        one_b = jnp.ones((SLOT_CHUNK, tt), BF16)
        for c in range(nslot // SLOT_CHUNK):
            perm = jnp.zeros((SLOT_CHUNK, tt), BF16)
            pgh = jnp.zeros((SLOT_CHUNK, tt), BF16)
            pgl = jnp.zeros((SLOT_CHUNK, tt), BF16)
            for k in range(TOP_K):
                lk = _bf(jnp.where(loc_hi[k:k + 1, :] == c, loc_lo[k:k + 1, :], -1.0))
                eq = rows_b == lk
                perm = jnp.where(eq, one_b, perm)
                pgh = jnp.where(eq, jnp.broadcast_to(gate_h[k:k + 1, :], (SLOT_CHUNK, tt)), pgh)
                pgl = jnp.where(eq, jnp.broadcast_to(gate_l[k:k + 1, :], (SLOT_CHUNK, tt)), pgl)
            cs = slice(c * SLOT_CHUNK, (c + 1) * SLOT_CHUNK)
            xp = lax.bitcast_convert_type(_dot(perm, ub), I32)
            buf[cs, 0:dh] = xp[:, 0:dh] | lax.shift_right_logical(xp[:, dh:2 * dh], 16)
            buf[cs, dh:dh + LANES] = lax.bitcast_convert_type(_dot(pgh, ones) + _dot(pgl, ones), I32)

    def step(cur, prv):
        @pl.when(i >= 2)
        def _():
            wait_tile(i - 2, cur)

        _issue_runs_inline(n8_ref, src_ref, dst_ref, jnp.maximum(i - 1, 0), i >= 1, copy_from(prv))
        build(bufs[cur])

        @pl.when(i == last)
        def _():
            _for_each_run_piece(n8_ref, src_ref, dst_ref, i, RUN_BITS, copy_from(cur))

            @pl.when(i >= 1)
            def _():
                wait_tile(i - 1, prv)
            wait_tile(i, cur)

    pl.when(i % 2 == 0)(lambda: step(0, 1))
    pl.when(i % 2 == 1)(lambda: step(1, 0))


def _dispatch(tabs, ztabs, u2, loc_t, gate_t, n_rows):
    n, d = u2.shape
    tt = min(MOE_TILE, n)
    nslot = TOP_K * tt + N_EXPERTS * SUBLANES
    dh = d // 2
    assert nslot % SLOT_CHUNK == 0 and tt // SUBLANES == RUN_BITS[0]
    col = pl.BlockSpec((TOP_K, tt), lambda i, *_: (0, i))
    return pl.pallas_call(
        functools.partial(_dispatch_kernel, tt=tt, nslot=nslot, dh=dh),
        out_shape=jax.ShapeDtypeStruct((n_rows, dh + LANES), I32),
        grid_spec=pltpu.PrefetchScalarGridSpec(
            num_scalar_prefetch=6, grid=(n // tt,),
            in_specs=[pl.BlockSpec((tt, d), lambda i, *_: (i, 0)), col, col],
            out_specs=pl.BlockSpec(memory_space=pl.ANY),
            scratch_shapes=[pltpu.VMEM((nslot, dh + LANES), I32), pltpu.VMEM((nslot, dh + LANES), I32),
                            pltpu.VMEM((EXPERT_ROWS // 2, dh + LANES), I32),
                            pltpu.SemaphoreType.DMA((3,))]),
        compiler_params=_cparams(("arbitrary",)),
        name="dispatch",
    )(*tabs, *ztabs, u2, loc_t, gate_t)


def _experts_kernel(be_ref, nu_ref, xs_ref, wg_ref, wu_ref, wd_ref, ys_ref, *, d):
    del be_ref
    dh = d // 2

    @pl.when(pl.program_id(0) < nu_ref[0])
    def _():
        w = xs_ref[:, 0:dh]
        xa = _bf(lax.bitcast_convert_type(w & jnp.int32(-65536), F32))
        xb = _bf(lax.bitcast_convert_type(lax.shift_left(w, 16), F32))
        gate = lax.bitcast_convert_type(xs_ref[:, dh:dh + LANES], F32)
        hg = _dot(xa, _bf(wg_ref[0, 0:dh, :])) + _dot(xb, _bf(wg_ref[0, dh:d, :]))
        hu = _dot(xa, _bf(wu_ref[0, 0:dh, :])) + _dot(xb, _bf(wu_ref[0, dh:d, :]))
        y = _dot(_bf(jax.nn.silu(hg) * hu), _bf(wd_ref[0])) * jnp.tile(gate, (1, d // LANES))
        yb = lax.bitcast_convert_type(_bf(y).astype(F32), I32)
        ys_ref[...] = yb[:, 0:dh] | lax.shift_right_logical(yb[:, dh:d], 16)


def _experts(blk_expert, n_used, xs, wg, wu, wd):
    rows, xw = xs.shape
    d = wg.shape[1]
    de = wg.shape[2]
    nblk = rows // EXPERT_ROWS
    blk = lambda i, be, nu: jnp.minimum(i, nu[0] - 1)
    return pl.pallas_call(
        functools.partial(_experts_kernel, d=d),
        out_shape=jax.ShapeDtypeStruct((rows, d // 2), I32),
        grid_spec=pltpu.PrefetchScalarGridSpec(
            num_scalar_prefetch=2, grid=(nblk,),
            in_specs=[pl.BlockSpec((EXPERT_ROWS, xw), lambda i, be, nu: (blk(i, be, nu), 0)),
                      pl.BlockSpec((1, d, de), lambda i, be, nu: (be[blk(i, be, nu)], 0, 0)),
                      pl.BlockSpec((1, d, de), lambda i, be, nu: (be[blk(i, be, nu)], 0, 0)),
                      pl.BlockSpec((1, de, d), lambda i, be, nu: (be[blk(i, be, nu)], 0, 0))],
            out_specs=pl.BlockSpec((EXPERT_ROWS, d // 2), lambda i, be, nu: (blk(i, be, nu), 0))),
        compiler_params=_cparams(("arbitrary",)),
        name="experts",
    )(blk_expert, n_used, xs, wg, wu, wd)


def _combine_kernel(n8_ref, src_ref, dst_ref, tot_ref, ys_hbm, loc_ref, h_ref, u2_ref, ada_ref, wsg_ref, wsu_ref, wsd_ref,
                    gf_ref, o_ref, ybuf0, ybuf1, pick_s, sems, *, tt, nslot, d):
    i = pl.program_id(0)
    last = pl.num_programs(0) - 1
    bufs = (ybuf0, ybuf1)
    dh = d // 2
    nchunk = nslot // SLOT_CHUNK

    def copy_into(slot):
        def piece(s0, d0, rows, prio):
            pltpu.make_async_copy(
                ys_hbm.at[pl.ds(d0, rows)], bufs[slot].at[pl.ds(s0, rows)], sems.at[slot]).start(priority=prio)
        return piece

    @pl.when(i == 0)
    def _():
        ybuf0[...] = jnp.zeros(ybuf0.shape, I32)
        ybuf1[...] = jnp.zeros(ybuf1.shape, I32)
        _for_each_run_piece(n8_ref, src_ref, dst_ref, 0, RUN_BITS, copy_into(0))

    def step(cur, nxt):
        nxt_tile = jnp.minimum(i + 1, last)
        _issue_runs_inline(n8_ref, src_ref, dst_ref, nxt_tile, i < last, copy_into(nxt), range(N_EXPERTS // 2))
        x = _bf(u2_ref[...])
        shared = _dot(_bf(jax.nn.silu(_dot(x, wsg_ref[...])) * _dot(x, wsu_ref[...])), wsd_ref[...])
        loc = loc_ref[...]
        cols_b = lax.broadcasted_iota(I32, (tt, SLOT_CHUNK), 1).astype(F32).astype(BF16)
        loc_hi = lax.shift_right_logical(loc, SLOT_CHUNK.bit_length() - 1)
        loc_lo = (loc & (SLOT_CHUNK - 1)).astype(F32)
        one_b = jnp.ones((tt, SLOT_CHUNK), BF16)
        for c in range(nchunk):
            pick = jnp.zeros((tt, SLOT_CHUNK), BF16)
            for k in range(TOP_K):
                lk = _bf(jnp.where(loc_hi[:, k:k + 1] == c, loc_lo[:, k:k + 1], -1.0))
                pick = jnp.where(cols_b == lk, one_b, pick)
            pick_s[c] = pick
        _wait_rows(tot_ref[i], lambda rows: pltpu.make_async_copy(
            ys_hbm.at[pl.ds(0, rows)], bufs[cur].at[pl.ds(0, rows)], sems.at[cur]), nslot)
        _issue_runs_inline(n8_ref, src_ref, dst_ref, nxt_tile, i < last, copy_into(nxt),
                           range(N_EXPERTS // 2, N_EXPERTS))
        routed_a = jnp.zeros((tt, dh), F32)
        routed_b = jnp.zeros((tt, dh), F32)
        for c in range(nchunk):
            w = bufs[cur][c * SLOT_CHUNK:(c + 1) * SLOT_CHUNK, :]
            routed_a = routed_a + _dot(pick_s[c], _bf(lax.bitcast_convert_type(w & jnp.int32(-65536), F32)))
            routed_b = routed_b + _dot(pick_s[c], _bf(lax.bitcast_convert_type(lax.shift_left(w, 16), F32)))
        routed = jnp.concatenate([routed_a, routed_b], axis=1)
        gate2 = ada_ref[0, :, 5 * d:6 * d]
        h = h_ref[...] + gate2 * (routed + shared)
        o_ref[...] = _rms(h, gf_ref[...])

    pl.when(i % 2 == 0)(lambda: step(0, 1))
    pl.when(i % 2 == 1)(lambda: step(1, 0))


def _combine(tabs, ys, loc, h1, u2, ada3, wsg, wsu, wsd, gf, seq):
    n, d = h1.shape
    tt = min(MOE_TILE, n)
    nslot = TOP_K * tt + N_EXPERTS * SUBLANES
    per_b = seq // tt
    row = pl.BlockSpec((tt, d), lambda i, *_: (i, 0))
    full = lambda a: pl.BlockSpec(a.shape, lambda i, *_: (0,) * a.ndim)
    wsg, wsu, wsd = wsg.astype(BF16), wsu.astype(BF16), wsd.astype(BF16)
    gf = gf.reshape(1, d)
    return pl.pallas_call(
        functools.partial(_combine_kernel, tt=tt, nslot=nslot, d=d),
        out_shape=jax.ShapeDtypeStruct((n, d), F32),
        grid_spec=pltpu.PrefetchScalarGridSpec(
            num_scalar_prefetch=4, grid=(n // tt,),
            in_specs=[pl.BlockSpec(memory_space=pl.ANY),
                      pl.BlockSpec((tt, TOP_K), lambda i, *_: (i, 0)),
                      row, row,
                      pl.BlockSpec((1, 1, ada3.shape[2]), lambda i, *_: (i // per_b, 0, 0)),
                      full(wsg), full(wsu), full(wsd), full(gf)],
            out_specs=row,
            scratch_shapes=[pltpu.VMEM((nslot, d // 2), I32), pltpu.VMEM((nslot, d // 2), I32),
                            pltpu.VMEM((nslot // SLOT_CHUNK, tt, SLOT_CHUNK), BF16),
                            pltpu.SemaphoreType.DMA((2,))]),
        compiler_params=_cparams(("arbitrary",)),
        name="combine",
    )(*tabs, ys, loc, h1, u2, ada3, wsg, wsu, wsd, gf)


def _moe(h1, u2, ada3, w_router, router_bias, wg, wu, wd, wsg, wsu, wsd, gf):
    bsz, seq, d = h1.shape
    n = bsz * seq
    assert seq % min(MOE_TILE, n) == 0
    h1f = h1.reshape(n, d)
    u2f = u2.reshape(n, d)
    gate_t, loc_t, c8, loff, run0, tot = _route(u2f, w_router, router_bias)
    nt = c8.shape[0]
    as_tab = lambda a: a[:, :, 0].astype(I32)
    tot8 = tot[:, 0].astype(I32)
    padded = (tot8 + EXPERT_ROWS - 1) // EXPERT_ROWS * EXPERT_ROWS
    pend = jnp.cumsum(padded)
    pstart = (pend - padded).astype(I32)
    nblk = (n * TOP_K + nt * N_EXPERTS * (SUBLANES - 1) + N_EXPERTS * (EXPERT_ROWS - 1) + EXPERT_ROWS - 1) // EXPERT_ROWS
    blk_row0 = jnp.arange(nblk, dtype=I32) * EXPERT_ROWS
    blk_expert = jnp.minimum(jnp.sum(pend[None, :] <= blk_row0[:, None], axis=1), N_EXPERTS - 1).astype(I32)
    n_used = (pend[-1:] // EXPERT_ROWS).astype(I32)
    n8 = as_tab(c8) // SUBLANES
    tabs = (n8.reshape(-1), as_tab(loff).reshape(-1), (pstart[None, :] + as_tab(run0)).reshape(-1),
            jnp.sum(n8, axis=1))
    ztabs = ((padded - tot8) // SUBLANES, pstart + tot8)
    xs = _dispatch(tabs, ztabs, u2f, loc_t, gate_t, nblk * EXPERT_ROWS)
    ys = _experts(blk_expert, n_used, xs, wg, wu, wd)
    out = _combine(tabs, ys, loc_t.T, h1f, u2f, ada3, wsg, wsu, wsd, gf, seq)
    return out.reshape(bsz, seq, d)


def kernel(x, c, w_ada, b_ada, norm1_g, w_in, ssm_lambda_re, ssm_lambda_im, ssm_log_dt, ssm_b_re, ssm_b_im,
           ssm_c_re, ssm_c_im, ssm_d, ssm_w_glu, ssm_b_glu, w_proj_ssm, w_proj_attn, w_out, norm2_g, w_router,
           router_bias, w_exp_gate, w_exp_up, w_exp_down, w_sh_gate, w_sh_up, w_sh_down, norm_f_g):
    depth = w_ada.shape[0]
    assert depth == 1, "the final norm is fused into the last (only) layer's combine kernel"
    bsz, seq, d = x.shape
    layer = 0
    ada3 = _ada(c, w_ada[layer], b_ada[layer]).reshape(bsz, 1, 6 * d)
    us, k, ki, qt, qit, vt, wit = _inproj(x, ada3, norm1_g[layer], w_in[layer])
    a_re, a_im, bb_re, bb_im = _s5disc(ssm_lambda_re[layer], ssm_lambda_im[layer], ssm_log_dt[layer],
                                       ssm_b_re[layer], ssm_b_im[layer])
    ys_t = _s5(us.transpose(1, 0, 2), a_re, a_im, bb_re, bb_im, ssm_c_re[layer], ssm_c_im[layer],
               ssm_d[layer], ssm_w_glu[layer], ssm_b_glu[layer])
    ya = _dsa(qt, qit, wit, k, ki, vt)
    h1, u2 = _mix(x, ys_t.transpose(1, 0, 2), ya, ada3, norm1_g[layer], w_in[layer], w_proj_ssm[layer],
                  w_proj_attn[layer], w_out[layer], norm2_g[layer])
    return _moe(h1, u2, ada3, w_router[layer], router_bias[layer], w_exp_gate[layer], w_exp_up[layer],
                w_exp_down[layer], w_sh_gate[layer], w_sh_up[layer], w_sh_down[layer], norm_f_g)
```

```python
import functools
import math

import jax
import jax.numpy as jnp
import numpy as np
from jax import lax
from jax.experimental import pallas as pl
from jax.experimental.pallas import tpu as pltpu

F32 = jnp.float32
BF16 = jnp.bfloat16
I32 = jnp.int32

SSM_GROUP = 16
SSM_STATE = 64
N_HEADS = 8
HEAD_DIM = 64
IDX_HEADS = 8
IDX_DIM = 64
TOPK_MAX = 256
N_EXPERTS = 64
TOP_K = 8
N_GROUPS = 8
TOPK_GROUPS = 4
ROUTED_SCALE = 2.5
EPS = 1e-6

V7X_VMEM_LIMIT_BYTES = 56 * 1024 * 1024
LANES = 128
SUBLANES = 8

INPROJ_ROWS = 256
S5_STEPS = 64
S5_LANE_CHUNK = 128
DSA_Q_COLS = 256
DSA_K_ROWS = 512
DSA_COUNT_ROWS = 64
BITSLICE_ROWS = 256
POS_SPLIT = 64
MIX_ROWS = 512
MOE_TILE = 256
ROUTE_TILES = 2
SLOT_CHUNK = 256
EXPERT_ROWS = 1024

NEG_BIG = -1e30
INT_MIN = -(2 ** 31)


def _cparams(sem):
    return pltpu.CompilerParams(dimension_semantics=sem, vmem_limit_bytes=V7X_VMEM_LIMIT_BYTES)


def _bf(x):
    return x.astype(BF16)


def _dot(a, b):
    return jnp.dot(a, b, preferred_element_type=F32)


def _dot_nt(a, b):
    return lax.dot_general(a, b, (((1,), (1,)), ((), ())), preferred_element_type=F32)


def _split(x):
    hi = _bf(x)
    lo = _bf(x - hi.astype(F32))
    return hi, lo


def _dot3(a, b):
    ah, al = _split(a)
    bh, bl = _split(b)
    return _dot(ah, bh) + (_dot(ah, bl) + _dot(al, bh))


def _rms(x, g):
    return x * lax.rsqrt(jnp.mean(x * x, axis=-1, keepdims=True) + EPS) * g


def _ada_kernel(c_ref, w_ref, b_ref, o_ref):
    c = c_ref[...]
    o_ref[...] = _dot3(c * jax.nn.sigmoid(c), w_ref[...]) + b_ref[...]


def _ada(c, w, b):
    bsz, d = c.shape
    n = w.shape[1]
    tn = 1024
    return pl.pallas_call(
        _ada_kernel,
        out_shape=jax.ShapeDtypeStruct((bsz, n), F32),
        grid=(n // tn,),
        in_specs=[pl.BlockSpec((bsz, d), lambda j: (0, 0)),
                  pl.BlockSpec((d, tn), lambda j: (0, j)),
                  pl.BlockSpec((1, tn), lambda j: (0, j))],
        out_specs=pl.BlockSpec((bsz, tn), lambda j: (0, j)),
        compiler_params=_cparams(("arbitrary",)),
        name="ada",
    )(c, w, b.reshape(1, n))


ALIBI_SLOPES = tuple(2.0 ** (-8.0 * (h + 1) / N_HEADS) for h in range(N_HEADS))
QAUG_ROWS = 16


def _inproj_kernel(x_ref, ada_ref, g1_ref, w_ref, wt_ref,
                   us_ref, k_ref, ki_ref, qt_ref, qit_ref, vt_ref, wit_ref, *, d, ssm_w, attn_w, idx_w, tl):
    x = x_ref[0]
    shift = ada_ref[0, :, 0:d]
    scale = ada_ref[0, :, d:2 * d]
    u = _bf(_rms(x, g1_ref[...]) * (1.0 + scale) + shift)
    r = _dot(u, w_ref[...])
    us_ref[0] = r[:, 0:ssm_w]
    k_ref[0] = _bf(r[:, ssm_w:ssm_w + HEAD_DIM])
    ki_ref[0] = _bf(r[:, ssm_w + LANES:ssm_w + LANES + IDX_DIM])
    rt = _dot_nt(wt_ref[...], u)
    arow = lax.broadcasted_iota(I32, (QAUG_ROWS, tl), 0)
    for h in range(N_HEADS):
        base = h * LANES
        qt_ref[0, base:base + HEAD_DIM, :] = _bf(rt[h * HEAD_DIM:(h + 1) * HEAD_DIM])
        qt_ref[0, base + HEAD_DIM:base + HEAD_DIM + QAUG_ROWS, :] = _bf(jnp.where(arow < 2, ALIBI_SLOPES[h], 0.0))
        qt_ref[0, base + HEAD_DIM + QAUG_ROWS:base + LANES, :] = jnp.zeros((LANES - HEAD_DIM - QAUG_ROWS, tl), BF16)
    qit_ref[0] = _bf(rt[attn_w:attn_w + idx_w])
    vt_ref[0] = _bf(rt[attn_w + idx_w:attn_w + idx_w + HEAD_DIM])
    wit_ref[0] = rt[attn_w + idx_w + HEAD_DIM:attn_w + idx_w + HEAD_DIM + IDX_HEADS]


def _split_w_in(w_in, d):
    ssm_w = 512
    sizes = (ssm_w, N_HEADS * HEAD_DIM, HEAD_DIM, HEAD_DIM, IDX_HEADS * IDX_DIM, IDX_DIM, IDX_HEADS, d, d)
    offs = [0]
    for s in sizes:
        offs.append(offs[-1] + s)
    return [w_in[:, offs[i]:offs[i + 1]] for i in range(9)]


def _inproj(x, ada3, g1, w_in):
    bsz, seq, d = x.shape
    ssm_w = 512
    attn_w = N_HEADS * HEAD_DIM
    idx_w = IDX_HEADS * IDX_DIM
    w_ssm, w_q, w_k, w_v, w_qi, w_ki, w_wi, _, _ = _split_w_in(w_in, d)
    zpad = lambda n: jnp.zeros((d, n), F32)
    wbig = jnp.concatenate([w_ssm, w_k, zpad(LANES - HEAD_DIM), w_ki, zpad(LANES - IDX_DIM)], axis=1).astype(BF16)
    wt = jnp.concatenate([w_q * (HEAD_DIM ** -0.5), w_qi * (IDX_DIM ** -0.5), w_v, w_wi,
                          zpad(LANES - HEAD_DIM - IDX_HEADS)], axis=1).T.astype(BF16)
    tl = INPROJ_ROWS
    kern = functools.partial(_inproj_kernel, d=d, ssm_w=ssm_w, attn_w=attn_w, idx_w=idx_w, tl=tl)
    row = lambda w: pl.BlockSpec((1, tl, w), lambda b, l: (b, l, 0))
    colt = lambda h: pl.BlockSpec((1, h, tl), lambda b, l: (b, 0, l))
    full = lambda a: pl.BlockSpec(a.shape, lambda b, l: (0,) * a.ndim)
    return pl.pallas_call(
        kern,
        out_shape=(jax.ShapeDtypeStruct((bsz, seq, ssm_w), F32),
                   jax.ShapeDtypeStruct((bsz, seq, HEAD_DIM), BF16),
                   jax.ShapeDtypeStruct((bsz, seq, IDX_DIM), BF16),
                   jax.ShapeDtypeStruct((bsz, N_HEADS * LANES, seq), BF16),
                   jax.ShapeDtypeStruct((bsz, idx_w, seq), BF16),
                   jax.ShapeDtypeStruct((bsz, HEAD_DIM, seq), BF16),
                   jax.ShapeDtypeStruct((bsz, IDX_HEADS, seq), F32)),
        grid=(bsz, seq // tl),
        in_specs=[row(d),
                  pl.BlockSpec((1, 1, ada3.shape[2]), lambda b, l: (b, 0, 0)),
                  pl.BlockSpec((1, d), lambda b, l: (0, 0)),
                  full(wbig), full(wt)],
        out_specs=(row(ssm_w), row(HEAD_DIM), row(IDX_DIM),
                   colt(N_HEADS * LANES), colt(idx_w), colt(HEAD_DIM), colt(IDX_HEADS)),
        compiler_params=_cparams(("arbitrary", "arbitrary")),
        name="inproj",
    )(x, ada3, g1.reshape(1, d), wbig, wt)


def _s5disc_kernel(lr_ref, li_ref, ldt_ref, br_ref, bi_ref, are_ref, aim_ref, bbr_ref, bbi_ref):
    lr = lr_ref[...]
    li = li_ref[...]
    dt = jnp.exp(ldt_ref[...])
    mag = jnp.exp(lr * dt)
    a_re = mag * jnp.cos(li * dt)
    a_im = mag * jnp.sin(li * dt)
    den = lr * lr + li * li
    n_re = a_re - 1.0
    f_re = (n_re * lr + a_im * li) / den
    f_im = (a_im * lr - n_re * li) / den
    br = br_ref[...]
    bi = bi_ref[...]
    are_ref[...] = a_re
    aim_ref[...] = a_im
    bbr_ref[...] = f_re * br - f_im * bi
    bbi_ref[...] = f_re * bi + f_im * br


def _s5disc(lam_re, lam_im, log_dt, b_re, b_im):
    g, p = lam_re.shape
    h = b_re.shape[2]
    rep = lambda a: jnp.repeat(a, h, axis=1)
    ldt = jnp.broadcast_to(log_dt[:, None], (g, p * h))
    sds = jax.ShapeDtypeStruct((g, p * h), F32)
    a_re, a_im, bb_re, bb_im = pl.pallas_call(
        _s5disc_kernel, out_shape=(sds, sds, sds, sds), name="s5disc",
    )(rep(lam_re), rep(lam_im), ldt, b_re.reshape(g, p * h), b_im.reshape(g, p * h))
    return a_re[:, ::h], a_im[:, ::h], bb_re.reshape(g, p, h), bb_im.reshape(g, p, h)


def _s5_kernel(u_ref, wb_ref, ar_ref, ai_ref, cc_ref, dsk_ref, wg_ref, bg_ref, o_ref, buf, hst, *, tl, width):
    nch = width // S5_LANE_CHUNK
    sw = S5_LANE_CHUNK // SSM_GROUP * SSM_STATE
    rows = tl * SUBLANES

    @pl.when(pl.program_id(0) == 0)
    def _():
        hst[...] = jnp.zeros_like(hst)

    u = u_ref[...].reshape(rows, width)
    ub = _bf(u)
    for j in range(nch):
        buf[:, j * 2 * sw:(j + 1) * 2 * sw] = _dot(ub[:, j * S5_LANE_CHUNK:(j + 1) * S5_LANE_CHUNK], wb_ref[j])

    for j in range(nch):
        re_cols = slice(j * 2 * sw, j * 2 * sw + sw)
        im_cols = slice(j * 2 * sw + sw, (j + 1) * 2 * sw)
        a_re = jnp.broadcast_to(ar_ref[:, j * sw:(j + 1) * sw], (SUBLANES, sw))
        a_im = jnp.broadcast_to(ai_ref[:, j * sw:(j + 1) * sw], (SUBLANES, sw))

        def step(t, carry, re_cols=re_cols, im_cols=im_cols, a_re=a_re, a_im=a_im):
            h_re, h_im = carry
            r0 = pl.multiple_of(t * SUBLANES, SUBLANES)
            n_re = (a_re * h_re - a_im * h_im) + buf[pl.ds(r0, SUBLANES), re_cols]
            n_im = (a_re * h_im + a_im * h_re) + buf[pl.ds(r0, SUBLANES), im_cols]
            buf[pl.ds(r0, SUBLANES), re_cols] = n_re
            buf[pl.ds(r0, SUBLANES), im_cols] = n_im
            return n_re, n_im

        h_re, h_im = lax.fori_loop(0, tl, step, (hst[:, re_cols], hst[:, im_cols]), unroll=True)
        hst[:, re_cols] = h_re
        hst[:, im_cols] = h_im

    ys = [_dot(_bf(buf[:, j * 2 * sw:(j + 1) * 2 * sw]), cc_ref[j]) for j in range(nch)]
    y = jnp.concatenate(ys, axis=1) + dsk_ref[...] * u
    y = jax.nn.gelu(y)
    y = y * jax.nn.sigmoid(_dot(_bf(y), wg_ref[...]) + bg_ref[...])
    o_ref[...] = _bf(y).reshape(tl, SUBLANES, width)


def _s5(u_t, a_re, a_im, bb_re, bb_im, c_re, c_im, d_skip, w_glu, b_glu):
    seq, bsz, width = u_t.shape
    assert bsz == SUBLANES
    nch = width // S5_LANE_CHUNK
    gpc = S5_LANE_CHUNK // SSM_GROUP
    sw = gpc * SSM_STATE
    eye = jnp.eye(gpc, dtype=F32)

    def bmat(bb):
        t = bb.reshape(nch, gpc, SSM_STATE, SSM_GROUP).transpose(0, 1, 3, 2)
        return jnp.einsum('jghp,gk->jghkp', t, eye).reshape(nch, S5_LANE_CHUNK, sw)

    def cmat(cc):
        t = cc.reshape(nch, gpc, SSM_GROUP, SSM_STATE).transpose(0, 1, 3, 2)
        return jnp.einsum('jgph,gk->jgpkh', t, eye).reshape(nch, sw, S5_LANE_CHUNK)

    wb = jnp.concatenate([bmat(bb_re), bmat(bb_im)], axis=2)
    cc = jnp.concatenate([cmat(c_re), -cmat(c_im)], axis=1)
    tl = S5_STEPS
    full = lambda a: pl.BlockSpec(a.shape, lambda i: (0,) * a.ndim)
    args = (u_t, wb.astype(BF16), a_re.reshape(1, -1), a_im.reshape(1, -1), cc.astype(BF16),
            d_skip.reshape(1, width), w_glu.astype(BF16), b_glu.reshape(1, width))
    return pl.pallas_call(
        functools.partial(_s5_kernel, tl=tl, width=width),
        out_shape=jax.ShapeDtypeStruct((seq, bsz, width), BF16),
        grid=(seq // tl,),
        in_specs=[pl.BlockSpec((tl, bsz, width), lambda i: (i, 0, 0))] + [full(a) for a in args[1:]],
        out_specs=pl.BlockSpec((tl, bsz, width), lambda i: (i, 0, 0)),
        scratch_shapes=[pltpu.VMEM((tl * SUBLANES, nch * 2 * sw), F32),
                        pltpu.VMEM((SUBLANES, nch * 2 * sw), F32)],
        compiler_params=_cparams(("arbitrary",)),
        name="s5",
    )(*args)


def _bit_transpose32(words):
    x = list(words)
    j, m = 16, 0x0000FFFF
    while j:
        k = 0
        while k < 32:
            t = (x[k] ^ lax.shift_right_logical(x[k + j], jnp.int32(j))) & jnp.int32(m - (1 << 32) if m >= 1 << 31 else m)
            x[k] = x[k] ^ t
            x[k + j] = x[k + j] ^ lax.shift_left(t, jnp.int32(j))
            k = (k + j + 1) & ~j
        j >>= 1
        m = (m ^ (m << j)) & 0xFFFFFFFF
    return x


def _dsa_kernel(qt_ref, qit_ref, wit_ref, ka_ref, ki_ref, vt_ref, o_ref, key_s, mb_s, acc_s, pl_s, p_s, *, tq, tk, topk, seq):
    i = pl.program_id(1)
    q0 = i * tq
    nkt = (q0 + tq + tk - 1) // tk
    ch = DSA_COUNT_ROWS
    krow = lax.broadcasted_iota(I32, (tk, tq), 0)
    qcol = q0 + lax.broadcasted_iota(I32, (tk, tq), 1)
    crow = lax.broadcasted_iota(I32, (ch, tq), 0)

    wb = wit_ref[0] * (IDX_HEADS ** -0.5)

    def score_tile(j, _):
        r0 = pl.multiple_of(j * tk, tk)
        kit = ki_ref[0, pl.ds(r0, tk), :]
        acc = jnp.zeros((tk, tq), F32)
        for h in range(IDX_HEADS):
            s = _dot(kit, qit_ref[0, h * IDX_DIM:(h + 1) * IDX_DIM, :])
            acc = acc + wb[h:h + 1, :] * jnp.maximum(s, 0.0)
        bits = lax.bitcast_convert_type(acc, I32)
        key = jnp.where(bits < 0, bits ^ jnp.int32(0x7FFFFFFF), bits)
        key = jnp.where(acc == 0.0, 0, key)
        key = jnp.where(krow + r0 <= qcol, key, INT_MIN)
        key_s[pl.ds(r0, tk), :] = key
        ukey = key ^ INT_MIN
        for c in range(tk // BITSLICE_ROWS):
            words = [ukey[c * BITSLICE_ROWS + v * SUBLANES:c * BITSLICE_ROWS + (v + 1) * SUBLANES, :]
                     for v in range(32)]
            planes = _bit_transpose32(words)
            g0 = pl.multiple_of((j * (tk // BITSLICE_ROWS) + c) * SUBLANES, SUBLANES)
            for it in range(32):
                pl_s[it, pl.ds(g0, SUBLANES), :] = planes[it]
        return 0

    @pl.when((pl.program_id(0) == 0) & (i == 0))
    def _():
        pl_s[...] = jnp.zeros(pl_s.shape, I32)

    lax.fori_loop(0, nkt, score_tile, 0)

    def count(pred):
        def tile(j, cnt):
            for c in range(tk // ch):
                rr = pl.multiple_of(j * tk + c * ch, ch)
                cnt = cnt + jnp.where(pred(key_s[pl.ds(rr, ch), :], rr), 1, 0)
            return cnt
        cnt = lax.fori_loop(0, nkt, tile, jnp.zeros((ch, tq), I32))
        return jnp.sum(cnt.astype(F32), axis=0, keepdims=True)

    ngrp = seq // 32

    def lane_count(words):
        pc = lax.population_count(words).reshape(ngrp // SUBLANES, SUBLANES, tq)
        return jnp.sum(jnp.sum(pc, axis=0).astype(F32), axis=0, keepdims=True)

    def bit_step(it, carry):
        alive, above, ans_u = carry
        ones = alive & pl_s[it]
        cnt1 = lane_count(ones)
        take = above + cnt1 >= float(topk)
        alive = jnp.where(take, ones, alive ^ ones)
        above = jnp.where(take, above, above + cnt1)
        ans_u = jnp.where(take, ans_u | lax.shift_left(jnp.int32(1), 31 - it), ans_u)
        return alive, above, ans_u

    grow = lax.broadcasted_iota(I32, (ngrp, tq), 0)
    alive0 = jnp.where(grow < nkt * (tk // 32), -1, 0)
    alive, above, ans_u = lax.fori_loop(
        0, 32, bit_step, (alive0, jnp.zeros((1, tq), F32), jnp.zeros((1, tq), I32)))
    thr = jnp.maximum(ans_u ^ INT_MIN, INT_MIN + 1)
    cnt_ge = above + lane_count(alive)
    tied = jnp.where(ans_u != 0, cnt_ge, 0.0) > float(topk)
    has_ties = jnp.max(jnp.where(tied, 1.0, 0.0)) > 0.0

    def tie_cut():
        need = float(topk) - count(lambda kb, rr: kb > thr)
        nbits = max(1, (seq - 1).bit_length())

        def idx_step(b, x):
            cand = x | lax.shift_left(jnp.int32(1), nbits - 1 - b)
            below = count(lambda kb, rr: jnp.where(kb == thr, crow + rr, seq) < cand)
            return jnp.where(below < need, cand, x)

        x = lax.fori_loop(0, nbits, idx_step, jnp.zeros((1, tq), I32))
        return jnp.where(tied, x, seq)

    cut = lax.cond(has_ties, tie_cut, lambda: jnp.full((1, tq), seq, I32))

    def bias_tile(j, _):
        for c in range(tk // ch):
            rr = pl.multiple_of(j * tk + c * ch, ch)
            kb = key_s[pl.ds(rr, ch), :]
            tie_bias = jnp.where(crow + rr <= cut, 0.0, NEG_BIG)
            mb_s[pl.ds(rr, ch), :] = jnp.where(kb > thr, 0.0, jnp.where(kb == thr, tie_bias, NEG_BIG))
        return 0

    lax.fori_loop(0, nkt, bias_tile, 0)

    def logits(j, h):
        r0 = pl.multiple_of(j * tk, tk)
        s = _dot(ka_ref[0, pl.ds(r0, tk), :], qt_ref[0, h * LANES:(h + 1) * LANES, :]) + mb_s[pl.ds(r0, tk), :]
        return s.reshape(tk // SUBLANES, SUBLANES, tq)

    acc_s[...] = jnp.zeros(acc_s.shape, F32)

    def attn_tile(j, carry):
        ms, ls = carry
        r0 = pl.multiple_of(j * tk, tk)
        new_m, new_l, alphas = [], [], []
        for h in range(N_HEADS):
            s = logits(j, h)
            m_new = jnp.maximum(ms[h], jnp.max(jnp.max(s, axis=0), axis=0, keepdims=True))
            alpha = jnp.exp(ms[h] - m_new)
            p = jnp.exp(s - m_new)
            new_m.append(m_new)
            new_l.append(alpha * ls[h] + jnp.sum(p, axis=0))
            alphas.append(alpha)
            p_s[h] = _bf(p.reshape(tk, tq))
        for h in range(N_HEADS):
            rows = slice(h * HEAD_DIM, (h + 1) * HEAD_DIM)
            acc_s[rows, :] = alphas[h] * acc_s[rows, :] + _dot(vt_ref[0, :, pl.ds(r0, tk)], p_s[h])
        return tuple(new_m), tuple(new_l)

    init = ((jnp.full((1, tq), NEG_BIG, F32),) * N_HEADS, (jnp.zeros((SUBLANES, tq), F32),) * N_HEADS)
    _, ls = lax.fori_loop(0, nkt, attn_tile, init)
    for h in range(N_HEADS):
        rows = slice(h * HEAD_DIM, (h + 1) * HEAD_DIM)
        acc_s[rows, :] = acc_s[rows, :] / jnp.sum(ls[h], axis=0, keepdims=True)
    o_ref[0] = _bf(acc_s[...].T)


def _dsa(qt, qit, wit, k, ki, vt):
    bsz, seq = k.shape[0], k.shape[1]
    aw = N_HEADS * HEAD_DIM
    tq = min(DSA_Q_COLS, seq)
    tk = min(DSA_K_ROWS, seq)
    topk = min(TOPK_MAX, seq // 4)
    assert (seq - 1) // POS_SPLIT < 256 and POS_SPLIT <= 256, "key positions must split into two bf16-exact parts"
    assert all(float(np.float32(sl).astype(BF16)) == sl for sl in ALIBI_SLOPES), "ALiBi slopes must be bf16-exact"
    pos = jnp.arange(seq, dtype=I32)
    posc = jnp.stack([(pos // POS_SPLIT) * POS_SPLIT, pos % POS_SPLIT], axis=1).astype(BF16)
    ka = jnp.concatenate([k, jnp.broadcast_to(posc[None], (bsz, seq, 2)),
                          jnp.zeros((bsz, seq, LANES - HEAD_DIM - 2), BF16)], axis=2)
    kern = functools.partial(_dsa_kernel, tq=tq, tk=tk, topk=topk, seq=seq)
    cols = lambda r: pl.BlockSpec((1, r, tq), lambda b, i: (b, 0, i))
    return pl.pallas_call(
        kern,
        out_shape=jax.ShapeDtypeStruct((bsz, seq, aw), BF16),
        grid=(bsz, seq // tq),
        in_specs=[cols(N_HEADS * LANES), cols(qit.shape[1]), cols(IDX_HEADS),
                  pl.BlockSpec((1, seq, LANES), lambda b, i: (b, 0, 0)),
                  pl.BlockSpec((1, seq, IDX_DIM), lambda b, i: (b, 0, 0)),
                  pl.BlockSpec((1, HEAD_DIM, seq), lambda b, i: (b, 0, 0))],
        out_specs=pl.BlockSpec((1, tq, aw), lambda b, i: (b, i, 0)),
        scratch_shapes=[pltpu.VMEM((seq, tq), I32), pltpu.VMEM((seq, tq), F32), pltpu.VMEM((aw, tq), F32),
                        pltpu.VMEM((32, seq // 32, tq), I32), pltpu.VMEM((N_HEADS, tk, tq), BF16)],
        compiler_params=_cparams(("arbitrary", "arbitrary")),
        name="dsa",
    )(qt, qit, wit, ka, ki, vt)


def _mix_kernel(x_ref, ys_ref, ya_ref, ada_ref, g1_ref, wgt_ref, wps_ref, wpa_ref, wo_ref, g2_ref,
                h_ref, u2_ref, *, d):
    gate1 = ada_ref[0, :, 2 * d:3 * d]
    shift2 = ada_ref[0, :, 3 * d:4 * d]
    scale2 = ada_ref[0, :, 4 * d:5 * d]
    x = x_ref[0]
    u = _bf(_rms(x, g1_ref[...]) * (1.0 + ada_ref[0, :, d:2 * d]) + ada_ref[0, :, 0:d])
    g = _dot(u, wgt_ref[...])
    mixed = (jax.nn.sigmoid(g[:, 0:d]) * _dot(ys_ref[0], wps_ref[...])
             + jax.nn.sigmoid(g[:, d:2 * d]) * _dot(ya_ref[0], wpa_ref[...]))
    h = x + gate1 * _dot(_bf(mixed), wo_ref[...])
    h_ref[0] = h
    u2_ref[0] = _rms(h, g2_ref[...]) * (1.0 + scale2) + shift2


def _mix(x, ys, ya, ada3, g1, w_in, wps, wpa, wo, g2):
    bsz, seq, d = x.shape
    tm = MIX_ROWS
    row = lambda w: pl.BlockSpec((1, tm, w), lambda b, l: (b, l, 0))
    full = lambda a: pl.BlockSpec(a.shape, lambda b, l: (0,) * a.ndim)
    wps, wpa, wo = wps.astype(BF16), wpa.astype(BF16), wo.astype(BF16)
    wgt = jnp.concatenate(_split_w_in(w_in, d)[7:9], axis=1).astype(BF16)
    g1 = g1.reshape(1, d)
    g2 = g2.reshape(1, d)
    return pl.pallas_call(
        functools.partial(_mix_kernel, d=d),
        out_shape=(jax.ShapeDtypeStruct((bsz, seq, d), F32), jax.ShapeDtypeStruct((bsz, seq, d), F32)),
        grid=(bsz, seq // tm),
        in_specs=[row(d), row(ys.shape[2]), row(ya.shape[2]),
                  pl.BlockSpec((1, 1, ada3.shape[2]), lambda b, l: (b, 0, 0)),
                  full(g1), full(wgt), full(wps), full(wpa), full(wo), full(g2)],
        out_specs=(row(d), row(d)),
        compiler_params=_cparams(("arbitrary", "arbitrary")),
        name="mix",
    )(x, ys, ya, ada3, g1, wgt, wps, wpa, wo, g2)


def _first_max(cur, idx, axis, big):
    m = jnp.max(cur, axis=axis, keepdims=True)
    first = jnp.min(jnp.where(cur == m, idx, big), axis=axis, keepdims=True)
    return m, idx == first


def _route_kernel(u_ref, wrh_ref, wrl_ref, rb_ref, tri_ref, ltri_ref,
                  gt_ref, loc_ref, c8_ref, loff_ref, run0_ref, tot_ref, run_s, *, t, tt):
    @pl.when(pl.program_id(0) == 0)
    def _():
        run_s[...] = jnp.zeros_like(run_s)

    uh, ul = _split(u_ref[...])
    logits = _dot_nt(wrh_ref[...], uh) + (_dot_nt(wrl_ref[...], uh) + _dot_nt(wrh_ref[...], ul))
    scores = jax.nn.sigmoid(logits)
    biased = scores + rb_ref[...]
    per_group = N_EXPERTS // N_GROUPS
    b3 = biased.reshape(N_GROUPS, per_group, t)
    i3 = lax.broadcasted_iota(I32, b3.shape, 1)
    m1, hit1 = _first_max(b3, i3, 1, per_group)
    m2 = jnp.max(jnp.where(hit1, -jnp.inf, b3), axis=1, keepdims=True)
    gs = (m1 + m2).reshape(N_GROUPS, t)
    gi = lax.broadcasted_iota(I32, gs.shape, 0)
    gsel = jnp.zeros(gs.shape, F32)
    for _ in range(TOPK_GROUPS):
        _, hit = _first_max(gs, gi, 0, N_GROUPS)
        gsel = jnp.where(hit, 1.0, gsel)
        gs = jnp.where(hit, -jnp.inf, gs)
    cur = jnp.where(gsel.reshape(N_GROUPS, 1, t) > 0.0, b3, -jnp.inf).reshape(N_EXPERTS, t)
    ei = lax.broadcasted_iota(I32, cur.shape, 0)
    hits = []
    gates = []
    for _ in range(TOP_K):
        _, hit = _first_max(cur, ei, 0, N_EXPERTS)
        hits.append(hit)
        gates.append(jnp.sum(jnp.where(hit, scores, 0.0), axis=0, keepdims=True))
        cur = jnp.where(hit, -jnp.inf, cur)
    gate = jnp.concatenate(gates, axis=0)
    gt_ref[...] = gate / jnp.sum(gate, axis=0, keepdims=True) * ROUTED_SCALE
    onehot = jnp.zeros(cur.shape, F32)
    for hit in hits:
        onehot = jnp.where(hit, 1.0, onehot)
    for sub in range(t // tt):
        cols = slice(sub * tt, (sub + 1) * tt)
        oh = onehot[:, cols]
        cnt = jnp.sum(oh, axis=1, keepdims=True)
        c8 = jnp.floor((cnt + (SUBLANES - 1)) * (1.0 / SUBLANES)) * SUBLANES
        c8l = jnp.broadcast_to(c8, (N_EXPERTS, LANES))
        loff = _dot(ltri_ref[...], _bf(c8l))
        slot = _dot(_bf(oh), tri_ref[...]) + loff[:, 0:1]
        loc_ref[:, cols] = jnp.concatenate(
            [jnp.sum(jnp.where(hit[:, cols], slot, 0.0), axis=0, keepdims=True) for hit in hits],
            axis=0).astype(I32)
        c8_ref[sub] = c8l
        loff_ref[sub] = loff
        run0_ref[sub] = run_s[...]
        run_s[...] = run_s[...] + c8
    tot_ref[...] = run_s[...]


def _route(u2, w_router, router_bias):
    n, d = u2.shape
    tt = min(MOE_TILE, n)
    t = min(ROUTE_TILES * tt, n)
    nt = n // tt
    wt = w_router.T
    wrh = wt.astype(BF16)
    wrl = (wt - wrh.astype(F32)).astype(BF16)
    tri = (jnp.arange(tt)[:, None] < jnp.arange(tt)[None, :]).astype(BF16)
    ex = jnp.arange(N_EXPERTS)
    ltri = (ex[None, :] < ex[:, None]).astype(BF16)
    full = lambda a: pl.BlockSpec(a.shape, lambda i: (0,) * a.ndim)
    col = pl.BlockSpec((TOP_K, t), lambda i: (0, i))
    tab = pl.BlockSpec((t // tt, N_EXPERTS, LANES), lambda i: (i, 0, 0))
    tab_sds = jax.ShapeDtypeStruct((nt, N_EXPERTS, LANES), F32)
    rb = router_bias.reshape(N_EXPERTS, 1)
    return pl.pallas_call(
        functools.partial(_route_kernel, t=t, tt=tt),
        out_shape=(jax.ShapeDtypeStruct((TOP_K, n), F32), jax.ShapeDtypeStruct((TOP_K, n), I32),
                   tab_sds, tab_sds, tab_sds, jax.ShapeDtypeStruct((N_EXPERTS, LANES), F32)),
        grid=(n // t,),
        in_specs=[pl.BlockSpec((t, d), lambda i: (i, 0)), full(wrh), full(wrl), full(rb), full(tri), full(ltri)],
        out_specs=(col, col, tab, tab, tab, pl.BlockSpec((N_EXPERTS, LANES), lambda i: (0, 0))),
        scratch_shapes=[pltpu.VMEM((N_EXPERTS, LANES), F32)],
        compiler_params=_cparams(("arbitrary",)),
        name="route",
    )(u2, wrh, wrl, rb, tri, ltri)


RUN_BITS = tuple(1 << b for b in reversed(range((MOE_TILE // SUBLANES).bit_length())))
RUN_LONG = 8


def _for_each_run_piece(n8_ref, src_ref, dst_ref, tile, bits, fn):
    def per_expert(e, _):
        idx = tile * N_EXPERTS + e
        n8 = n8_ref[idx]
        src = src_ref[idx]
        dst = dst_ref[idx]
        def pieces(some_bits):
            for p in some_bits:
                off = (n8 & ~(2 * p - 1)) * SUBLANES

                @pl.when((n8 & p) != 0)
                def _(p=p, off=off):
                    fn(pl.multiple_of(src + off, SUBLANES), pl.multiple_of(dst + off, SUBLANES), p * SUBLANES,
                       bits.index(p) % 2)

        long_bits = tuple(p for p in bits if p >= RUN_LONG)
        if long_bits:
            pl.when(n8 >= RUN_LONG)(lambda: pieces(long_bits))
        pieces(tuple(p for p in bits if p < RUN_LONG))
        return 0

    lax.fori_loop(0, N_EXPERTS, per_expert, 0)


def _issue_runs_inline(n8_ref, src_ref, dst_ref, tile, enable, fn, experts=range(N_EXPERTS)):
    for e in experts:
        idx = tile * N_EXPERTS + e
        n8 = jnp.where(enable, n8_ref[idx], 0)
        src = src_ref[idx]
        dst = dst_ref[idx]
        for b, p in enumerate(RUN_BITS):
            off = (n8 & ~(2 * p - 1)) * SUBLANES

            @pl.when((n8 & p) != 0)
            def _(b=b, p=p, off=off, src=src, dst=dst):
                fn(pl.multiple_of(src + off, SUBLANES), pl.multiple_of(dst + off, SUBLANES), p * SUBLANES, b % 2)


def _wait_rows(n8, make_copy, max_rows):
    for p in tuple(1 << b for b in reversed(range((max_rows // SUBLANES).bit_length()))):
        @pl.when((n8 & p) != 0)
        def _(p=p):
            make_copy(p * SUBLANES).wait()


def _dispatch_kernel(n8_ref, src_ref, dst_ref, tot_ref, zn8_ref, zdst_ref, u_ref, loc_ref, gate_ref, xs_hbm,
                     lbuf0, lbuf1, lbuf2, zx, sems, *, tt, nslot, dh):
    i = pl.program_id(0)
    last = pl.num_programs(0) - 1
    bufs = (lbuf0, lbuf1, lbuf2)
    nbuf = len(bufs)
    zsem = nbuf

    def copy_from(slot):
        def piece(s0, d0, rows, prio):
            pltpu.make_async_copy(
                bufs[slot].at[pl.ds(s0, rows)], xs_hbm.at[pl.ds(d0, rows)], sems.at[slot]).start(priority=prio)
        return piece

    def wait_tile(tile, slot):
        _wait_rows(tot_ref[tile], lambda rows: pltpu.make_async_copy(
            bufs[slot].at[pl.ds(0, rows)], xs_hbm.at[pl.ds(0, rows)], sems.at[slot]), nslot)

    @pl.when(i == 0)
    def _():
        zx[...] = jnp.zeros(zx.shape, I32)

        def zero_piece(s0, d0, rows, prio):
            cx = pltpu.make_async_copy(zx.at[pl.ds(0, rows)], xs_hbm.at[pl.ds(d0, rows)], sems.at[zsem])
            cx.start()
            cx.wait()

        zbits = tuple(1 << b for b in reversed(range((EXPERT_ROWS // SUBLANES - 1).bit_length())))
        _for_each_run_piece(zn8_ref, zdst_ref, zdst_ref, 0, zbits, zero_piece)

    def build(buf):
        ub = _bf(u_ref[...])
        ones = jnp.ones((tt, LANES), BF16)
        loc = loc_ref[...]
        gate = gate_ref[...]
        rows_b = lax.broadcasted_iota(I32, (SLOT_CHUNK, tt), 0).astype(F32).astype(BF16)
        loc_hi = lax.shift_right_logical(loc, SLOT_CHUNK.bit_length() - 1)
        loc_lo = (loc & (SLOT_CHUNK - 1)).astype(F32)
        gate_h = _bf(gate)
        gate_l = _bf(gate - gate_h.astype(F32))
        one_b = jnp.ones((SLOT_CHUNK, tt), BF16)
        for c in range(nslot // SLOT_CHUNK):
            perm = jnp.zeros((SLOT_CHUNK, tt), BF16)
            pgh = jnp.zeros((SLOT_CHUNK, tt), BF16)
            pgl = jnp.zeros((SLOT_CHUNK, tt), BF16)
            for k in range(TOP_K):
                lk = _bf(jnp.where(loc_hi[k:k + 1, :] == c, loc_lo[k:k + 1, :], -1.0))
                eq = rows_b == lk
                perm = jnp.where(eq, one_b, perm)
                pgh = jnp.where(eq, jnp.broadcast_to(gate_h[k:k + 1, :], (SLOT_CHUNK, tt)), pgh)
                pgl = jnp.where(eq, jnp.broadcast_to(gate_l[k:k + 1, :], (SLOT_CHUNK, tt)), pgl)
            cs = slice(c * SLOT_CHUNK, (c + 1) * SLOT_CHUNK)
            xp = lax.bitcast_convert_type(_dot(perm, ub), I32)
            buf[cs, 0:dh] = xp[:, 0:dh] | lax.shift_right_logical(xp[:, dh:2 * dh], 16)
            buf[cs, dh:dh + LANES] = lax.bitcast_convert_type(_dot(pgh, ones) + _dot(pgl, ones), I32)

    def step(cur):
        prv, prv2 = (cur - 1) % nbuf, (cur - 2) % nbuf

        @pl.when(i >= nbuf)
        def _():
            wait_tile(i - nbuf, cur)

        _issue_runs_inline(n8_ref, src_ref, dst_ref, jnp.maximum(i - 1, 0), i >= 1, copy_from(prv))
        build(bufs[cur])

        @pl.when(i == last)
        def _():
            _for_each_run_piece(n8_ref, src_ref, dst_ref, i, RUN_BITS, copy_from(cur))

            @pl.when(i >= 2)
            def _():
                wait_tile(i - 2, prv2)

            @pl.when(i >= 1)
            def _():
                wait_tile(i - 1, prv)
            wait_tile(i, cur)

    for cur in range(nbuf):
        pl.when(i % nbuf == cur)(functools.partial(step, cur))


def _dispatch(tabs, ztabs, u2, loc_t, gate_t, n_rows):
    n, d = u2.shape
    tt = min(MOE_TILE, n)
    nslot = TOP_K * tt + N_EXPERTS * SUBLANES
    dh = d // 2
    assert nslot % SLOT_CHUNK == 0 and tt // SUBLANES == RUN_BITS[0]
    col = pl.BlockSpec((TOP_K, tt), lambda i, *_: (0, i))
    return pl.pallas_call(
        functools.partial(_dispatch_kernel, tt=tt, nslot=nslot, dh=dh),
        out_shape=jax.ShapeDtypeStruct((n_rows, dh + LANES), I32),
        grid_spec=pltpu.PrefetchScalarGridSpec(
            num_scalar_prefetch=6, grid=(n // tt,),
            in_specs=[pl.BlockSpec((tt, d), lambda i, *_: (i, 0)), col, col],
            out_specs=pl.BlockSpec(memory_space=pl.ANY),
            scratch_shapes=[pltpu.VMEM((nslot, dh + LANES), I32)] * 3 + [
                pltpu.VMEM((EXPERT_ROWS // 2, dh + LANES), I32), pltpu.SemaphoreType.DMA((4,))]),
        compiler_params=_cparams(("arbitrary",)),
        name="dispatch",
    )(*tabs, *ztabs, u2, loc_t, gate_t)


def _experts_kernel(be_ref, nu_ref, xs_ref, wg_ref, wu_ref, wd_ref, ys_ref, *, d):
    del be_ref
    dh = d // 2

    @pl.when(pl.program_id(0) < nu_ref[0])
    def _():
        w = xs_ref[:, 0:dh]
        xa = _bf(lax.bitcast_convert_type(w & jnp.int32(-65536), F32))
        xb = _bf(lax.bitcast_convert_type(lax.shift_left(w, 16), F32))
        gate = lax.bitcast_convert_type(xs_ref[:, dh:dh + LANES], F32)
        hg = _dot(xa, _bf(wg_ref[0, 0:dh, :])) + _dot(xb, _bf(wg_ref[0, dh:d, :]))
        hu = _dot(xa, _bf(wu_ref[0, 0:dh, :])) + _dot(xb, _bf(wu_ref[0, dh:d, :]))
        y = _dot(_bf(jax.nn.silu(hg) * hu), _bf(wd_ref[0])) * jnp.tile(gate, (1, d // LANES))
        yb = lax.bitcast_convert_type(_bf(y).astype(F32), I32)
        ys_ref[...] = yb[:, 0:dh] | lax.shift_right_logical(yb[:, dh:d], 16)


def _experts(blk_expert, n_used, xs, wg, wu, wd):
    rows, xw = xs.shape
    d = wg.shape[1]
    de = wg.shape[2]
    nblk = rows // EXPERT_ROWS
    blk = lambda i, be, nu: jnp.minimum(i, nu[0] - 1)
    return pl.pallas_call(
        functools.partial(_experts_kernel, d=d),
        out_shape=jax.ShapeDtypeStruct((rows, d // 2), I32),
        grid_spec=pltpu.PrefetchScalarGridSpec(
            num_scalar_prefetch=2, grid=(nblk,),
            in_specs=[pl.BlockSpec((EXPERT_ROWS, xw), lambda i, be, nu: (blk(i, be, nu), 0)),
                      pl.BlockSpec((1, d, de), lambda i, be, nu: (be[blk(i, be, nu)], 0, 0)),
                      pl.BlockSpec((1, d, de), lambda i, be, nu: (be[blk(i, be, nu)], 0, 0)),
                      pl.BlockSpec((1, de, d), lambda i, be, nu: (be[blk(i, be, nu)], 0, 0))],
            out_specs=pl.BlockSpec((EXPERT_ROWS, d // 2), lambda i, be, nu: (blk(i, be, nu), 0))),
        compiler_params=_cparams(("arbitrary",)),
        name="experts",
    )(blk_expert, n_used, xs, wg, wu, wd)


def _combine_kernel(n8_ref, src_ref, dst_ref, tot_ref, ys_hbm, loc_ref, h_ref, u2_ref, ada_ref, wsg_ref, wsu_ref, wsd_ref,
                    gf_ref, o_ref, ybuf0, ybuf1, pick_s, sems, *, tt, nslot, d):
    i = pl.program_id(0)
    last = pl.num_programs(0) - 1
    bufs = (ybuf0, ybuf1)
    dh = d // 2
    nchunk = nslot // SLOT_CHUNK

    def copy_into(slot):
        def piece(s0, d0, rows, prio):
            pltpu.make_async_copy(
                ys_hbm.at[pl.ds(d0, rows)], bufs[slot].at[pl.ds(s0, rows)], sems.at[slot]).start(priority=prio)
        return piece

    @pl.when(i == 0)
    def _():
        ybuf0[...] = jnp.zeros(ybuf0.shape, I32)
        ybuf1[...] = jnp.zeros(ybuf1.shape, I32)
        _for_each_run_piece(n8_ref, src_ref, dst_ref, 0, RUN_BITS, copy_into(0))

    def step(cur, nxt):
        nxt_tile = jnp.minimum(i + 1, last)
        _issue_runs_inline(n8_ref, src_ref, dst_ref, nxt_tile, i < last, copy_into(nxt), range(N_EXPERTS // 2))
        x = _bf(u2_ref[...])
        shared = _dot(_bf(jax.nn.silu(_dot(x, wsg_ref[...])) * _dot(x, wsu_ref[...])), wsd_ref[...])
        loc = loc_ref[...]
        cols_b = lax.broadcasted_iota(I32, (tt, SLOT_CHUNK), 1).astype(F32).astype(BF16)
        loc_hi = lax.shift_right_logical(loc, SLOT_CHUNK.bit_length() - 1)
        loc_lo = (loc & (SLOT_CHUNK - 1)).astype(F32)
        one_b = jnp.ones((tt, SLOT_CHUNK), BF16)
        for c in range(nchunk):
            pick = jnp.zeros((tt, SLOT_CHUNK), BF16)
            for k in range(TOP_K):
                lk = _bf(jnp.where(loc_hi[:, k:k + 1] == c, loc_lo[:, k:k + 1], -1.0))
                pick = jnp.where(cols_b == lk, one_b, pick)
            pick_s[c] = pick
        _wait_rows(tot_ref[i], lambda rows: pltpu.make_async_copy(
            ys_hbm.at[pl.ds(0, rows)], bufs[cur].at[pl.ds(0, rows)], sems.at[cur]), nslot)
        _issue_runs_inline(n8_ref, src_ref, dst_ref, nxt_tile, i < last, copy_into(nxt),
                           range(N_EXPERTS // 2, N_EXPERTS))
        routed_a = jnp.zeros((tt, dh), F32)
        routed_b = jnp.zeros((tt, dh), F32)
        for c in range(nchunk):
            w = bufs[cur][c * SLOT_CHUNK:(c + 1) * SLOT_CHUNK, :]
            routed_a = routed_a + _dot(pick_s[c], _bf(lax.bitcast_convert_type(w & jnp.int32(-65536), F32)))
            routed_b = routed_b + _dot(pick_s[c], _bf(lax.bitcast_convert_type(lax.shift_left(w, 16), F32)))
        routed = jnp.concatenate([routed_a, routed_b], axis=1)
        gate2 = ada_ref[0, :, 5 * d:6 * d]
        h = h_ref[...] + gate2 * (routed + shared)
        o_ref[...] = _rms(h, gf_ref[...])

    pl.when(i % 2 == 0)(lambda: step(0, 1))
    pl.when(i % 2 == 1)(lambda: step(1, 0))


def _combine(tabs, ys, loc, h1, u2, ada3, wsg, wsu, wsd, gf, seq):
    n, d = h1.shape
    tt = min(MOE_TILE, n)
    nslot = TOP_K * tt + N_EXPERTS * SUBLANES
    per_b = seq // tt
    row = pl.BlockSpec((tt, d), lambda i, *_: (i, 0))
    full = lambda a: pl.BlockSpec(a.shape, lambda i, *_: (0,) * a.ndim)
    wsg, wsu, wsd = wsg.astype(BF16), wsu.astype(BF16), wsd.astype(BF16)
    gf = gf.reshape(1, d)
    return pl.pallas_call(
        functools.partial(_combine_kernel, tt=tt, nslot=nslot, d=d),
        out_shape=jax.ShapeDtypeStruct((n, d), F32),
        grid_spec=pltpu.PrefetchScalarGridSpec(
            num_scalar_prefetch=4, grid=(n // tt,),
            in_specs=[pl.BlockSpec(memory_space=pl.ANY),
                      pl.BlockSpec((tt, TOP_K), lambda i, *_: (i, 0)),
                      row, row,
                      pl.BlockSpec((1, 1, ada3.shape[2]), lambda i, *_: (i // per_b, 0, 0)),
                      full(wsg), full(wsu), full(wsd), full(gf)],
            out_specs=row,
            scratch_shapes=[pltpu.VMEM((nslot, d // 2), I32), pltpu.VMEM((nslot, d // 2), I32),
                            pltpu.VMEM((nslot // SLOT_CHUNK, tt, SLOT_CHUNK), BF16),
                            pltpu.SemaphoreType.DMA((2,))]),
        compiler_params=_cparams(("arbitrary",)),
        name="combine",
    )(*tabs, ys, loc, h1, u2, ada3, wsg, wsu, wsd, gf)


def _moe(h1, u2, ada3, w_router, router_bias, wg, wu, wd, wsg, wsu, wsd, gf):
    bsz, seq, d = h1.shape
    n = bsz * seq
    assert seq % min(MOE_TILE, n) == 0
    h1f = h1.reshape(n, d)
    u2f = u2.reshape(n, d)
    gate_t, loc_t, c8, loff, run0, tot = _route(u2f, w_router, router_bias)
    nt = c8.shape[0]
    as_tab = lambda a: a[:, :, 0].astype(I32)
    tot8 = tot[:, 0].astype(I32)
    padded = (tot8 + EXPERT_ROWS - 1) // EXPERT_ROWS * EXPERT_ROWS
    pend = jnp.cumsum(padded)
    pstart = (pend - padded).astype(I32)
    nblk = (n * TOP_K + nt * N_EXPERTS * (SUBLANES - 1) + N_EXPERTS * (EXPERT_ROWS - 1) + EXPERT_ROWS - 1) // EXPERT_ROWS
    blk_row0 = jnp.arange(nblk, dtype=I32) * EXPERT_ROWS
    blk_expert = jnp.minimum(jnp.sum(pend[None, :] <= blk_row0[:, None], axis=1), N_EXPERTS - 1).astype(I32)
    n_used = (pend[-1:] // EXPERT_ROWS).astype(I32)
    n8 = as_tab(c8) // SUBLANES
    tabs = (n8.reshape(-1), as_tab(loff).reshape(-1), (pstart[None, :] + as_tab(run0)).reshape(-1),
            jnp.sum(n8, axis=1))
    ztabs = ((padded - tot8) // SUBLANES, pstart + tot8)
    xs = _dispatch(tabs, ztabs, u2f, loc_t, gate_t, nblk * EXPERT_ROWS)
    ys = _experts(blk_expert, n_used, xs, wg, wu, wd)
    out = _combine(tabs, ys, loc_t.T, h1f, u2f, ada3, wsg, wsu, wsd, gf, seq)
    return out.reshape(bsz, seq, d)


def kernel(x, c, w_ada, b_ada, norm1_g, w_in, ssm_lambda_re, ssm_lambda_im, ssm_log_dt, ssm_b_re, ssm_b_im,
           ssm_c_re, ssm_c_im, ssm_d, ssm_w_glu, ssm_b_glu, w_proj_ssm, w_proj_attn, w_out, norm2_g, w_router,
           router_bias, w_exp_gate, w_exp_up, w_exp_down, w_sh_gate, w_sh_up, w_sh_down, norm_f_g):
    depth = w_ada.shape[0]
    assert depth == 1, "the final norm is fused into the last (only) layer's combine kernel"
    bsz, seq, d = x.shape
    layer = 0
    ada3 = _ada(c, w_ada[layer], b_ada[layer]).reshape(bsz, 1, 6 * d)
    us, k, ki, qt, qit, vt, wit = _inproj(x, ada3, norm1_g[layer], w_in[layer])
    a_re, a_im, bb_re, bb_im = _s5disc(ssm_lambda_re[layer], ssm_lambda_im[layer], ssm_log_dt[layer],
                                       ssm_b_re[layer], ssm_b_im[layer])
    ys_t = _s5(us.transpose(1, 0, 2), a_re, a_im, bb_re, bb_im, ssm_c_re[layer], ssm_c_im[layer],
               ssm_d[layer], ssm_w_glu[layer], ssm_b_glu[layer])
    ya = _dsa(qt, qit, wit, k, ki, vt)
    h1, u2 = _mix(x, ys_t.transpose(1, 0, 2), ya, ada3, norm1_g[layer], w_in[layer], w_proj_ssm[layer],
                  w_proj_attn[layer], w_out[layer], norm2_g[layer])
    return _moe(h1, u2, ada3, w_router[layer], router_bias[layer], w_exp_gate[layer], w_exp_up[layer],
                w_exp_down[layer], w_sh_gate[layer], w_sh_up[layer], w_sh_down[layer], norm_f_g)
```

```python
import functools
import math

import jax
import jax.numpy as jnp
import numpy as np
from jax import lax
from jax.experimental import pallas as pl
from jax.experimental.pallas import tpu as pltpu

F32 = jnp.float32
BF16 = jnp.bfloat16
I32 = jnp.int32

SSM_GROUP = 16
SSM_STATE = 64
N_HEADS = 8
HEAD_DIM = 64
IDX_HEADS = 8
IDX_DIM = 64
TOPK_MAX = 256
N_EXPERTS = 64
TOP_K = 8
N_GROUPS = 8
TOPK_GROUPS = 4
ROUTED_SCALE = 2.5
EPS = 1e-6

V7X_VMEM_LIMIT_BYTES = 56 * 1024 * 1024
LANES = 128
SUBLANES = 8

INPROJ_ROWS = 256
S5_STEPS = 64
S5_LANE_CHUNK = 128
DSA_Q_COLS = 256
DSA_K_ROWS = 512
DSA_COUNT_ROWS = 64
BITSLICE_ROWS = 256
POS_SPLIT = 64
MIX_ROWS = 512
MOE_TILE = 256
ROUTE_TILES = 2
SLOT_CHUNK = 256
EXPERT_ROWS = 1024

NEG_BIG = -1e30
INT_MIN = -(2 ** 31)


def _cparams(sem):
    return pltpu.CompilerParams(dimension_semantics=sem, vmem_limit_bytes=V7X_VMEM_LIMIT_BYTES)


def _bf(x):
    return x.astype(BF16)


def _dot(a, b):
    return jnp.dot(a, b, preferred_element_type=F32)


def _dot_nt(a, b):
    return lax.dot_general(a, b, (((1,), (1,)), ((), ())), preferred_element_type=F32)


def _split(x):
    hi = _bf(x)
    lo = _bf(x - hi.astype(F32))
    return hi, lo


def _dot3(a, b):
    ah, al = _split(a)
    bh, bl = _split(b)
    return _dot(ah, bh) + (_dot(ah, bl) + _dot(al, bh))


def _rms(x, g):
    return x * lax.rsqrt(jnp.mean(x * x, axis=-1, keepdims=True) + EPS) * g


def _ada_kernel(c_ref, w_ref, b_ref, o_ref):
    c = c_ref[...]
    o_ref[...] = _dot3(c * jax.nn.sigmoid(c), w_ref[...]) + b_ref[...]


def _ada(c, w, b):
    bsz, d = c.shape
    n = w.shape[1]
    tn = 1024
    return pl.pallas_call(
        _ada_kernel,
        out_shape=jax.ShapeDtypeStruct((bsz, n), F32),
        grid=(n // tn,),
        in_specs=[pl.BlockSpec((bsz, d), lambda j: (0, 0)),
                  pl.BlockSpec((d, tn), lambda j: (0, j)),
                  pl.BlockSpec((1, tn), lambda j: (0, j))],
        out_specs=pl.BlockSpec((bsz, tn), lambda j: (0, j)),
        compiler_params=_cparams(("arbitrary",)),
        name="ada",
    )(c, w, b.reshape(1, n))


ALIBI_SLOPES = tuple(2.0 ** (-8.0 * (h + 1) / N_HEADS) for h in range(N_HEADS))
QAUG_ROWS = 16


def _inproj_kernel(x_ref, ada_ref, g1_ref, w_ref, wt_ref,
                   us_ref, k_ref, ki_ref, qt_ref, qit_ref, vt_ref, wit_ref, *, d, ssm_w, attn_w, idx_w, tl):
    x = x_ref[0]
    shift = ada_ref[0, :, 0:d]
    scale = ada_ref[0, :, d:2 * d]
    u = _bf(_rms(x, g1_ref[...]) * (1.0 + scale) + shift)
    r = _dot(u, w_ref[...])
    us_ref[0] = r[:, 0:ssm_w]
    k_ref[0] = _bf(r[:, ssm_w:ssm_w + HEAD_DIM])
    ki_ref[0] = _bf(r[:, ssm_w + LANES:ssm_w + LANES + IDX_DIM])
    rt = _dot_nt(wt_ref[...], u)
    arow = lax.broadcasted_iota(I32, (QAUG_ROWS, tl), 0)
    for h in range(N_HEADS):
        base = h * LANES
        qt_ref[0, base:base + HEAD_DIM, :] = _bf(rt[h * HEAD_DIM:(h + 1) * HEAD_DIM])
        qt_ref[0, base + HEAD_DIM:base + HEAD_DIM + QAUG_ROWS, :] = _bf(jnp.where(arow < 2, ALIBI_SLOPES[h], 0.0))
        qt_ref[0, base + HEAD_DIM + QAUG_ROWS:base + LANES, :] = jnp.zeros((LANES - HEAD_DIM - QAUG_ROWS, tl), BF16)
    qit_ref[0] = _bf(rt[attn_w:attn_w + idx_w])
    vt_ref[0] = _bf(rt[attn_w + idx_w:attn_w + idx_w + HEAD_DIM])
    wit_ref[0] = rt[attn_w + idx_w + HEAD_DIM:attn_w + idx_w + HEAD_DIM + IDX_HEADS]


def _split_w_in(w_in, d):
    ssm_w = 512
    sizes = (ssm_w, N_HEADS * HEAD_DIM, HEAD_DIM, HEAD_DIM, IDX_HEADS * IDX_DIM, IDX_DIM, IDX_HEADS, d, d)
    offs = [0]
    for s in sizes:
        offs.append(offs[-1] + s)
    return [w_in[:, offs[i]:offs[i + 1]] for i in range(9)]


def _inproj(x, ada3, g1, w_in):
    bsz, seq, d = x.shape
    ssm_w = 512
    attn_w = N_HEADS * HEAD_DIM
    idx_w = IDX_HEADS * IDX_DIM
    w_ssm, w_q, w_k, w_v, w_qi, w_ki, w_wi, _, _ = _split_w_in(w_in, d)
    zpad = lambda n: jnp.zeros((d, n), F32)
    wbig = jnp.concatenate([w_ssm, w_k, zpad(LANES - HEAD_DIM), w_ki, zpad(LANES - IDX_DIM)], axis=1).astype(BF16)
    wt = jnp.concatenate([w_q * (HEAD_DIM ** -0.5), w_qi * (IDX_DIM ** -0.5), w_v, w_wi,
                          zpad(LANES - HEAD_DIM - IDX_HEADS)], axis=1).T.astype(BF16)
    tl = INPROJ_ROWS
    kern = functools.partial(_inproj_kernel, d=d, ssm_w=ssm_w, attn_w=attn_w, idx_w=idx_w, tl=tl)
    row = lambda w: pl.BlockSpec((1, tl, w), lambda b, l: (b, l, 0))
    colt = lambda h: pl.BlockSpec((1, h, tl), lambda b, l: (b, 0, l))
    full = lambda a: pl.BlockSpec(a.shape, lambda b, l: (0,) * a.ndim)
    return pl.pallas_call(
        kern,
        out_shape=(jax.ShapeDtypeStruct((bsz, seq, ssm_w), F32),
                   jax.ShapeDtypeStruct((bsz, seq, HEAD_DIM), BF16),
                   jax.ShapeDtypeStruct((bsz, seq, IDX_DIM), BF16),
                   jax.ShapeDtypeStruct((bsz, N_HEADS * LANES, seq), BF16),
                   jax.ShapeDtypeStruct((bsz, idx_w, seq), BF16),
                   jax.ShapeDtypeStruct((bsz, HEAD_DIM, seq), BF16),
                   jax.ShapeDtypeStruct((bsz, IDX_HEADS, seq), F32)),
        grid=(bsz, seq // tl),
        in_specs=[row(d),
                  pl.BlockSpec((1, 1, ada3.shape[2]), lambda b, l: (b, 0, 0)),
                  pl.BlockSpec((1, d), lambda b, l: (0, 0)),
                  full(wbig), full(wt)],
        out_specs=(row(ssm_w), row(HEAD_DIM), row(IDX_DIM),
                   colt(N_HEADS * LANES), colt(idx_w), colt(HEAD_DIM), colt(IDX_HEADS)),
        compiler_params=_cparams(("arbitrary", "arbitrary")),
        name="inproj",
    )(x, ada3, g1.reshape(1, d), wbig, wt)


def _s5disc_kernel(lr_ref, li_ref, ldt_ref, br_ref, bi_ref, are_ref, aim_ref, bbr_ref, bbi_ref):
    lr = lr_ref[...]
    li = li_ref[...]
    dt = jnp.exp(ldt_ref[...])
    mag = jnp.exp(lr * dt)
    a_re = mag * jnp.cos(li * dt)
    a_im = mag * jnp.sin(li * dt)
    den = lr * lr + li * li
    n_re = a_re - 1.0
    f_re = (n_re * lr + a_im * li) / den
    f_im = (a_im * lr - n_re * li) / den
    br = br_ref[...]
    bi = bi_ref[...]
    are_ref[...] = a_re
    aim_ref[...] = a_im
    bbr_ref[...] = f_re * br - f_im * bi
    bbi_ref[...] = f_re * bi + f_im * br


def _s5disc(lam_re, lam_im, log_dt, b_re, b_im):
    g, p = lam_re.shape
    h = b_re.shape[2]
    rep = lambda a: jnp.repeat(a, h, axis=1)
    ldt = jnp.broadcast_to(log_dt[:, None], (g, p * h))
    sds = jax.ShapeDtypeStruct((g, p * h), F32)
    a_re, a_im, bb_re, bb_im = pl.pallas_call(
        _s5disc_kernel, out_shape=(sds, sds, sds, sds), name="s5disc",
    )(rep(lam_re), rep(lam_im), ldt, b_re.reshape(g, p * h), b_im.reshape(g, p * h))
    return a_re[:, ::h], a_im[:, ::h], bb_re.reshape(g, p, h), bb_im.reshape(g, p, h)


def _s5_kernel(u_ref, wb_ref, ar_ref, ai_ref, cc_ref, dsk_ref, wg_ref, bg_ref, o_ref, buf, hst, *, tl, width):
    nch = width // S5_LANE_CHUNK
    sw = S5_LANE_CHUNK // SSM_GROUP * SSM_STATE
    rows = tl * SUBLANES

    @pl.when(pl.program_id(0) == 0)
    def _():
        hst[...] = jnp.zeros_like(hst)

    u = jnp.swapaxes(u_ref[...], 0, 1).reshape(rows, width)
    ub = _bf(u)
    for j in range(nch):
        buf[:, j * 2 * sw:(j + 1) * 2 * sw] = _dot(ub[:, j * S5_LANE_CHUNK:(j + 1) * S5_LANE_CHUNK], wb_ref[j])

    for j in range(nch):
        re_cols = slice(j * 2 * sw, j * 2 * sw + sw)
        im_cols = slice(j * 2 * sw + sw, (j + 1) * 2 * sw)
        a_re = jnp.broadcast_to(ar_ref[:, j * sw:(j + 1) * sw], (SUBLANES, sw))
        a_im = jnp.broadcast_to(ai_ref[:, j * sw:(j + 1) * sw], (SUBLANES, sw))

        def step(t, carry, re_cols=re_cols, im_cols=im_cols, a_re=a_re, a_im=a_im):
            h_re, h_im = carry
            r0 = pl.multiple_of(t * SUBLANES, SUBLANES)
            n_re = (a_re * h_re - a_im * h_im) + buf[pl.ds(r0, SUBLANES), re_cols]
            n_im = (a_re * h_im + a_im * h_re) + buf[pl.ds(r0, SUBLANES), im_cols]
            buf[pl.ds(r0, SUBLANES), re_cols] = n_re
            buf[pl.ds(r0, SUBLANES), im_cols] = n_im
            return n_re, n_im

        h_re, h_im = lax.fori_loop(0, tl, step, (hst[:, re_cols], hst[:, im_cols]), unroll=True)
        hst[:, re_cols] = h_re
        hst[:, im_cols] = h_im

    ys = [_dot(_bf(buf[:, j * 2 * sw:(j + 1) * 2 * sw]), cc_ref[j]) for j in range(nch)]
    y = jnp.concatenate(ys, axis=1) + dsk_ref[...] * u
    y = jax.nn.gelu(y)
    y = y * jax.nn.sigmoid(_dot(_bf(y), wg_ref[...]) + bg_ref[...])
    o_ref[...] = _bf(jnp.swapaxes(y.reshape(tl, SUBLANES, width), 0, 1))


def _s5(u_t, a_re, a_im, bb_re, bb_im, c_re, c_im, d_skip, w_glu, b_glu):
    bsz, seq, width = u_t.shape
    assert bsz == SUBLANES
    nch = width // S5_LANE_CHUNK
    gpc = S5_LANE_CHUNK // SSM_GROUP
    sw = gpc * SSM_STATE
    eye = jnp.eye(gpc, dtype=F32)

    def bmat(bb):
        t = bb.reshape(nch, gpc, SSM_STATE, SSM_GROUP).transpose(0, 1, 3, 2)
        return jnp.einsum('jghp,gk->jghkp', t, eye).reshape(nch, S5_LANE_CHUNK, sw)

    def cmat(cc):
        t = cc.reshape(nch, gpc, SSM_GROUP, SSM_STATE).transpose(0, 1, 3, 2)
        return jnp.einsum('jgph,gk->jgpkh', t, eye).reshape(nch, sw, S5_LANE_CHUNK)

    wb = jnp.concatenate([bmat(bb_re), bmat(bb_im)], axis=2)
    cc = jnp.concatenate([cmat(c_re), -cmat(c_im)], axis=1)
    tl = S5_STEPS
    full = lambda a: pl.BlockSpec(a.shape, lambda i: (0,) * a.ndim)
    args = (u_t, wb.astype(BF16), a_re.reshape(1, -1), a_im.reshape(1, -1), cc.astype(BF16),
            d_skip.reshape(1, width), w_glu.astype(BF16), b_glu.reshape(1, width))
    return pl.pallas_call(
        functools.partial(_s5_kernel, tl=tl, width=width),
        out_shape=jax.ShapeDtypeStruct((bsz, seq, width), BF16),
        grid=(seq // tl,),
        in_specs=[pl.BlockSpec((bsz, tl, width), lambda i: (0, i, 0))] + [full(a) for a in args[1:]],
        out_specs=pl.BlockSpec((bsz, tl, width), lambda i: (0, i, 0)),
        scratch_shapes=[pltpu.VMEM((tl * SUBLANES, nch * 2 * sw), F32),
                        pltpu.VMEM((SUBLANES, nch * 2 * sw), F32)],
        compiler_params=_cparams(("arbitrary",)),
        name="s5",
    )(*args)


def _bit_transpose32(words):
    x = list(words)
    j, m = 16, 0x0000FFFF
    while j:
        k = 0
        while k < 32:
            t = (x[k] ^ lax.shift_right_logical(x[k + j], jnp.int32(j))) & jnp.int32(m - (1 << 32) if m >= 1 << 31 else m)
            x[k] = x[k] ^ t
            x[k + j] = x[k + j] ^ lax.shift_left(t, jnp.int32(j))
            k = (k + j + 1) & ~j
        j >>= 1
        m = (m ^ (m << j)) & 0xFFFFFFFF
    return x


def _dsa_kernel(qt_ref, qit_ref, wit_ref, ka_ref, ki_ref, vt_ref, o_ref, key_s, mb_s, acc_s, pl_s, p_s, *, tq, tk, topk, seq):
    i = pl.program_id(1)
    q0 = i * tq
    nkt = (q0 + tq + tk - 1) // tk
    ch = DSA_COUNT_ROWS
    krow = lax.broadcasted_iota(I32, (tk, tq), 0)
    qcol = q0 + lax.broadcasted_iota(I32, (tk, tq), 1)
    crow = lax.broadcasted_iota(I32, (ch, tq), 0)

    wb = wit_ref[0] * (IDX_HEADS ** -0.5)

    def score_tile(j, _):
        r0 = pl.multiple_of(j * tk, tk)
        kit = ki_ref[0, pl.ds(r0, tk), :]
        acc = jnp.zeros((tk, tq), F32)
        for h in range(IDX_HEADS):
            s = _dot(kit, qit_ref[0, h * IDX_DIM:(h + 1) * IDX_DIM, :])
            acc = acc + wb[h:h + 1, :] * jnp.maximum(s, 0.0)
        bits = lax.bitcast_convert_type(acc, I32)
        key = jnp.where(bits < 0, bits ^ jnp.int32(0x7FFFFFFF), bits)
        key = jnp.where(acc == 0.0, 0, key)
        key = jnp.where(krow + r0 <= qcol, key, INT_MIN)
        key_s[pl.ds(r0, tk), :] = key
        ukey = key ^ INT_MIN
        for c in range(tk // BITSLICE_ROWS):
            words = [ukey[c * BITSLICE_ROWS + v * SUBLANES:c * BITSLICE_ROWS + (v + 1) * SUBLANES, :]
                     for v in range(32)]
            planes = _bit_transpose32(words)
            g0 = pl.multiple_of((j * (tk // BITSLICE_ROWS) + c) * SUBLANES, SUBLANES)
            for it in range(32):
                pl_s[it, pl.ds(g0, SUBLANES), :] = planes[it]
        return 0

    @pl.when((pl.program_id(0) == 0) & (i == 0))
    def _():
        pl_s[...] = jnp.zeros(pl_s.shape, I32)

    def score_pair(jj, _):
        score_tile(2 * jj, 0)
        score_tile(2 * jj + 1, 0)
        return 0

    lax.fori_loop(0, nkt // 2, score_pair, 0)

    @pl.when(nkt % 2 == 1)
    def _():
        score_tile(nkt - 1, 0)

    def count(pred):
        def tile(j, cnt):
            for c in range(tk // ch):
                rr = pl.multiple_of(j * tk + c * ch, ch)
                cnt = cnt + jnp.where(pred(key_s[pl.ds(rr, ch), :], rr), 1, 0)
            return cnt
        cnt = lax.fori_loop(0, nkt, tile, jnp.zeros((ch, tq), I32))
        return jnp.sum(cnt.astype(F32), axis=0, keepdims=True)

    ngrp = seq // 32

    def lane_count(words):
        pc = lax.population_count(words).reshape(ngrp // SUBLANES, SUBLANES, tq)
        return jnp.sum(jnp.sum(pc, axis=0).astype(F32), axis=0, keepdims=True)

    def bit_step(it, carry):
        alive, above, ans_u = carry
        ones = alive & pl_s[it]
        cnt1 = lane_count(ones)
        take = above + cnt1 >= float(topk)
        alive = jnp.where(take, ones, alive ^ ones)
        above = jnp.where(take, above, above + cnt1)
        ans_u = jnp.where(take, ans_u | lax.shift_left(jnp.int32(1), 31 - it), ans_u)
        return alive, above, ans_u

    grow = lax.broadcasted_iota(I32, (ngrp, tq), 0)
    alive0 = jnp.where(grow < nkt * (tk // 32), -1, 0)
    alive, above, ans_u = lax.fori_loop(
        0, 32, bit_step, (alive0, jnp.zeros((1, tq), F32), jnp.zeros((1, tq), I32)))
    thr = jnp.maximum(ans_u ^ INT_MIN, INT_MIN + 1)
    cnt_ge = above + lane_count(alive)
    tied = jnp.where(ans_u != 0, cnt_ge, 0.0) > float(topk)
    has_ties = jnp.max(jnp.where(tied, 1.0, 0.0)) > 0.0

    def tie_cut():
        need = float(topk) - count(lambda kb, rr: kb > thr)
        nbits = max(1, (seq - 1).bit_length())

        def idx_step(b, x):
            cand = x | lax.shift_left(jnp.int32(1), nbits - 1 - b)
            below = count(lambda kb, rr: jnp.where(kb == thr, crow + rr, seq) < cand)
            return jnp.where(below < need, cand, x)

        x = lax.fori_loop(0, nbits, idx_step, jnp.zeros((1, tq), I32))
        return jnp.where(tied, x, seq)

    cut = lax.cond(has_ties, tie_cut, lambda: jnp.full((1, tq), seq, I32))

    def bias_tile(j, _):
        for c in range(tk // ch):
            rr = pl.multiple_of(j * tk + c * ch, ch)
            kb = key_s[pl.ds(rr, ch), :]
            tie_bias = jnp.where(crow + rr <= cut, 0.0, NEG_BIG)
            mb_s[pl.ds(rr, ch), :] = jnp.where(kb > thr, 0.0, jnp.where(kb == thr, tie_bias, NEG_BIG))
        return 0

    lax.fori_loop(0, nkt, bias_tile, 0)

    def logits(j, h):
        r0 = pl.multiple_of(j * tk, tk)
        s = _dot(ka_ref[0, pl.ds(r0, tk), :], qt_ref[0, h * LANES:(h + 1) * LANES, :]) + mb_s[pl.ds(r0, tk), :]
        return s.reshape(tk // SUBLANES, SUBLANES, tq)

    acc_s[...] = jnp.zeros(acc_s.shape, F32)

    def attn_tile(j, carry):
        ms, ls = carry
        r0 = pl.multiple_of(j * tk, tk)
        new_m, new_l, alphas = [], [], []
        for h in range(N_HEADS):
            s = logits(j, h)
            m_new = jnp.maximum(ms[h], jnp.max(jnp.max(s, axis=0), axis=0, keepdims=True))
            alpha = jnp.exp(ms[h] - m_new)
            p = jnp.exp(s - m_new)
            new_m.append(m_new)
            new_l.append(alpha * ls[h] + jnp.sum(p, axis=0))
            alphas.append(alpha)
            p_s[h] = _bf(p.reshape(tk, tq))
        for h in range(N_HEADS):
            rows = slice(h * HEAD_DIM, (h + 1) * HEAD_DIM)
            acc_s[rows, :] = alphas[h] * acc_s[rows, :] + _dot(vt_ref[0, :, pl.ds(r0, tk)], p_s[h])
        return tuple(new_m), tuple(new_l)

    init = ((jnp.full((1, tq), NEG_BIG, F32),) * N_HEADS, (jnp.zeros((SUBLANES, tq), F32),) * N_HEADS)
    _, ls = lax.fori_loop(0, nkt, attn_tile, init)
    for h in range(N_HEADS):
        rows = slice(h * HEAD_DIM, (h + 1) * HEAD_DIM)
        acc_s[rows, :] = acc_s[rows, :] / jnp.sum(ls[h], axis=0, keepdims=True)
    o_ref[0] = _bf(acc_s[...].T)


def _dsa(qt, qit, wit, k, ki, vt):
    bsz, seq = k.shape[0], k.shape[1]
    aw = N_HEADS * HEAD_DIM
    tq = min(DSA_Q_COLS, seq)
    tk = min(DSA_K_ROWS, seq)
    topk = min(TOPK_MAX, seq // 4)
    assert (seq - 1) // POS_SPLIT < 256 and POS_SPLIT <= 256, "key positions must split into two bf16-exact parts"
    assert all(float(np.float32(sl).astype(BF16)) == sl for sl in ALIBI_SLOPES), "ALiBi slopes must be bf16-exact"
    pos = jnp.arange(seq, dtype=I32)
    posc = jnp.stack([(pos // POS_SPLIT) * POS_SPLIT, pos % POS_SPLIT], axis=1).astype(BF16)
    ka = jnp.concatenate([k, jnp.broadcast_to(posc[None], (bsz, seq, 2)),
                          jnp.zeros((bsz, seq, LANES - HEAD_DIM - 2), BF16)], axis=2)
    kern = functools.partial(_dsa_kernel, tq=tq, tk=tk, topk=topk, seq=seq)
    cols = lambda r: pl.BlockSpec((1, r, tq), lambda b, i: (b, 0, i))
    return pl.pallas_call(
        kern,
        out_shape=jax.ShapeDtypeStruct((bsz, seq, aw), BF16),
        grid=(bsz, seq // tq),
        in_specs=[cols(N_HEADS * LANES), cols(qit.shape[1]), cols(IDX_HEADS),
                  pl.BlockSpec((1, seq, LANES), lambda b, i: (b, 0, 0)),
                  pl.BlockSpec((1, seq, IDX_DIM), lambda b, i: (b, 0, 0)),
                  pl.BlockSpec((1, HEAD_DIM, seq), lambda b, i: (b, 0, 0))],
        out_specs=pl.BlockSpec((1, tq, aw), lambda b, i: (b, i, 0)),
        scratch_shapes=[pltpu.VMEM((seq, tq), I32), pltpu.VMEM((seq, tq), F32), pltpu.VMEM((aw, tq), F32),
                        pltpu.VMEM((32, seq // 32, tq), I32), pltpu.VMEM((N_HEADS, tk, tq), BF16)],
        compiler_params=_cparams(("arbitrary", "arbitrary")),
        name="dsa",
    )(qt, qit, wit, ka, ki, vt)


def _mix_kernel(x_ref, ys_ref, ya_ref, ada_ref, g1_ref, wgt_ref, wps_ref, wpa_ref, wo_ref, g2_ref,
                h_ref, u2_ref, *, d):
    gate1 = ada_ref[0, :, 2 * d:3 * d]
    shift2 = ada_ref[0, :, 3 * d:4 * d]
    scale2 = ada_ref[0, :, 4 * d:5 * d]
    x = x_ref[0]
    u = _bf(_rms(x, g1_ref[...]) * (1.0 + ada_ref[0, :, d:2 * d]) + ada_ref[0, :, 0:d])
    g = _dot(u, wgt_ref[...])
    mixed = (jax.nn.sigmoid(g[:, 0:d]) * _dot(ys_ref[0], wps_ref[...])
             + jax.nn.sigmoid(g[:, d:2 * d]) * _dot(ya_ref[0], wpa_ref[...]))
    h = x + gate1 * _dot(_bf(mixed), wo_ref[...])
    h_ref[0] = h
    u2_ref[0] = _rms(h, g2_ref[...]) * (1.0 + scale2) + shift2


def _mix(x, ys, ya, ada3, g1, w_in, wps, wpa, wo, g2):
    bsz, seq, d = x.shape
    tm = MIX_ROWS
    row = lambda w: pl.BlockSpec((1, tm, w), lambda b, l: (b, l, 0))
    full = lambda a: pl.BlockSpec(a.shape, lambda b, l: (0,) * a.ndim)
    wps, wpa, wo = wps.astype(BF16), wpa.astype(BF16), wo.astype(BF16)
    wgt = jnp.concatenate(_split_w_in(w_in, d)[7:9], axis=1).astype(BF16)
    g1 = g1.reshape(1, d)
    g2 = g2.reshape(1, d)
    return pl.pallas_call(
        functools.partial(_mix_kernel, d=d),
        out_shape=(jax.ShapeDtypeStruct((bsz, seq, d), F32), jax.ShapeDtypeStruct((bsz, seq, d), F32)),
        grid=(bsz, seq // tm),
        in_specs=[row(d), row(ys.shape[2]), row(ya.shape[2]),
                  pl.BlockSpec((1, 1, ada3.shape[2]), lambda b, l: (b, 0, 0)),
                  full(g1), full(wgt), full(wps), full(wpa), full(wo), full(g2)],
        out_specs=(row(d), row(d)),
        compiler_params=_cparams(("arbitrary", "arbitrary")),
        name="mix",
    )(x, ys, ya, ada3, g1, wgt, wps, wpa, wo, g2)


def _first_max(cur, idx, axis, big):
    m = jnp.max(cur, axis=axis, keepdims=True)
    first = jnp.min(jnp.where(cur == m, idx, big), axis=axis, keepdims=True)
    return m, idx == first


def _route_kernel(u_ref, wrh_ref, wrl_ref, rb_ref, tri_ref, ltri_ref,
                  gt_ref, loc_ref, c8_ref, loff_ref, run0_ref, tot_ref, run_s, *, t, tt):
    @pl.when(pl.program_id(0) == 0)
    def _():
        run_s[...] = jnp.zeros_like(run_s)

    uh, ul = _split(u_ref[...])
    logits = _dot_nt(wrh_ref[...], uh) + (_dot_nt(wrl_ref[...], uh) + _dot_nt(wrh_ref[...], ul))
    scores = jax.nn.sigmoid(logits)
    biased = scores + rb_ref[...]
    per_group = N_EXPERTS // N_GROUPS
    b3 = biased.reshape(N_GROUPS, per_group, t)
    i3 = lax.broadcasted_iota(I32, b3.shape, 1)
    m1, hit1 = _first_max(b3, i3, 1, per_group)
    m2 = jnp.max(jnp.where(hit1, -jnp.inf, b3), axis=1, keepdims=True)
    gs = (m1 + m2).reshape(N_GROUPS, t)
    gi = lax.broadcasted_iota(I32, gs.shape, 0)
    gsel = jnp.zeros(gs.shape, F32)
    for _ in range(TOPK_GROUPS):
        _, hit = _first_max(gs, gi, 0, N_GROUPS)
        gsel = jnp.where(hit, 1.0, gsel)
        gs = jnp.where(hit, -jnp.inf, gs)
    cur = jnp.where(gsel.reshape(N_GROUPS, 1, t) > 0.0, b3, -jnp.inf).reshape(N_EXPERTS, t)
    ei = lax.broadcasted_iota(I32, cur.shape, 0)
    hits = []
    gates = []
    for _ in range(TOP_K):
        _, hit = _first_max(cur, ei, 0, N_EXPERTS)
        hits.append(hit)
        gates.append(jnp.sum(jnp.where(hit, scores, 0.0), axis=0, keepdims=True))
        cur = jnp.where(hit, -jnp.inf, cur)
    gate = jnp.concatenate(gates, axis=0)
    gt_ref[...] = gate / jnp.sum(gate, axis=0, keepdims=True) * ROUTED_SCALE
    onehot = jnp.zeros(cur.shape, F32)
    for hit in hits:
        onehot = jnp.where(hit, 1.0, onehot)
    for sub in range(t // tt):
        cols = slice(sub * tt, (sub + 1) * tt)
        oh = onehot[:, cols]
        cnt = jnp.sum(oh, axis=1, keepdims=True)
        c8 = jnp.floor((cnt + (SUBLANES - 1)) * (1.0 / SUBLANES)) * SUBLANES
        c8l = jnp.broadcast_to(c8, (N_EXPERTS, LANES))
        loff = _dot(ltri_ref[...], _bf(c8l))
        slot = _dot(_bf(oh), tri_ref[...]) + loff[:, 0:1]
        loc_ref[:, cols] = jnp.concatenate(
            [jnp.sum(jnp.where(hit[:, cols], slot, 0.0), axis=0, keepdims=True) for hit in hits],
            axis=0).astype(I32)
        c8_ref[sub] = c8l
        loff_ref[sub] = loff
        run0_ref[sub] = run_s[...]
        run_s[...] = run_s[...] + c8
    tot_ref[...] = run_s[...]


def _route(u2, w_router, router_bias):
    n, d = u2.shape
    tt = min(MOE_TILE, n)
    t = min(ROUTE_TILES * tt, n)
    nt = n // tt
    wt = w_router.T
    wrh = wt.astype(BF16)
    wrl = (wt - wrh.astype(F32)).astype(BF16)
    tri = (jnp.arange(tt)[:, None] < jnp.arange(tt)[None, :]).astype(BF16)
    ex = jnp.arange(N_EXPERTS)
    ltri = (ex[None, :] < ex[:, None]).astype(BF16)
    full = lambda a: pl.BlockSpec(a.shape, lambda i: (0,) * a.ndim)
    col = pl.BlockSpec((TOP_K, t), lambda i: (0, i))
    tab = pl.BlockSpec((t // tt, N_EXPERTS, LANES), lambda i: (i, 0, 0))
    tab_sds = jax.ShapeDtypeStruct((nt, N_EXPERTS, LANES), F32)
    rb = router_bias.reshape(N_EXPERTS, 1)
    return pl.pallas_call(
        functools.partial(_route_kernel, t=t, tt=tt),
        out_shape=(jax.ShapeDtypeStruct((TOP_K, n), F32), jax.ShapeDtypeStruct((TOP_K, n), I32),
                   tab_sds, tab_sds, tab_sds, jax.ShapeDtypeStruct((N_EXPERTS, LANES), F32)),
        grid=(n // t,),
        in_specs=[pl.BlockSpec((t, d), lambda i: (i, 0)), full(wrh), full(wrl), full(rb), full(tri), full(ltri)],
        out_specs=(col, col, tab, tab, tab, pl.BlockSpec((N_EXPERTS, LANES), lambda i: (0, 0))),
        scratch_shapes=[pltpu.VMEM((N_EXPERTS, LANES), F32)],
        compiler_params=_cparams(("arbitrary",)),
        name="route",
    )(u2, wrh, wrl, rb, tri, ltri)


RUN_BITS = tuple(1 << b for b in reversed(range((MOE_TILE // SUBLANES).bit_length())))
RUN_LONG = 8


def _for_each_run_piece(n8_ref, src_ref, dst_ref, tile, bits, fn):
    def per_expert(e, _):
        idx = tile * N_EXPERTS + e
        n8 = n8_ref[idx]
        src = src_ref[idx]
        dst = dst_ref[idx]
        def pieces(some_bits):
            for p in some_bits:
                off = (n8 & ~(2 * p - 1)) * SUBLANES

                @pl.when((n8 & p) != 0)
                def _(p=p, off=off):
                    fn(pl.multiple_of(src + off, SUBLANES), pl.multiple_of(dst + off, SUBLANES), p * SUBLANES,
                       bits.index(p) % 2)

        long_bits = tuple(p for p in bits if p >= RUN_LONG)
        if long_bits:
            pl.when(n8 >= RUN_LONG)(lambda: pieces(long_bits))
        pieces(tuple(p for p in bits if p < RUN_LONG))
        return 0

    lax.fori_loop(0, N_EXPERTS, per_expert, 0)


def _issue_runs_inline(n8_ref, src_ref, dst_ref, tile, enable, fn, experts=range(N_EXPERTS)):
    for e in experts:
        idx = tile * N_EXPERTS + e
        n8 = jnp.where(enable, n8_ref[idx], 0)
        src = src_ref[idx]
        dst = dst_ref[idx]
        for b, p in enumerate(RUN_BITS):
            off = (n8 & ~(2 * p - 1)) * SUBLANES

            @pl.when((n8 & p) != 0)
            def _(b=b, p=p, off=off, src=src, dst=dst):
                fn(pl.multiple_of(src + off, SUBLANES), pl.multiple_of(dst + off, SUBLANES), p * SUBLANES, b % 2)


def _wait_rows(n8, make_copy, max_rows):
    for p in tuple(1 << b for b in reversed(range((max_rows // SUBLANES).bit_length()))):
        @pl.when((n8 & p) != 0)
        def _(p=p):
            make_copy(p * SUBLANES).wait()


def _dispatch_kernel(n8_ref, src_ref, dst_ref, tot_ref, zn8_ref, zdst_ref, u_ref, loc_ref, gate_ref, xs_hbm,
                     lbuf0, lbuf1, lbuf2, zx, sems, *, tt, nslot, dh):
    i = pl.program_id(0)
    last = pl.num_programs(0) - 1
    bufs = (lbuf0, lbuf1, lbuf2)
    nbuf = len(bufs)
    zsem = nbuf

    def copy_from(slot):
        def piece(s0, d0, rows, prio):
            pltpu.make_async_copy(
                bufs[slot].at[pl.ds(s0, rows)], xs_hbm.at[pl.ds(d0, rows)], sems.at[slot]).start(priority=prio)
        return piece

    def wait_tile(tile, slot):
        _wait_rows(tot_ref[tile], lambda rows: pltpu.make_async_copy(
            bufs[slot].at[pl.ds(0, rows)], xs_hbm.at[pl.ds(0, rows)], sems.at[slot]), nslot)

    @pl.when(i == 0)
    def _():
        zx[...] = jnp.zeros(zx.shape, I32)

        def zero_piece(s0, d0, rows, prio):
            cx = pltpu.make_async_copy(zx.at[pl.ds(0, rows)], xs_hbm.at[pl.ds(d0, rows)], sems.at[zsem])
            cx.start()
            cx.wait()

        zbits = tuple(1 << b for b in reversed(range((EXPERT_ROWS // SUBLANES - 1).bit_length())))
        _for_each_run_piece(zn8_ref, zdst_ref, zdst_ref, 0, zbits, zero_piece)

    def build(buf):
        ub = _bf(u_ref[...])
        ones = jnp.ones((tt, LANES), BF16)
        loc = loc_ref[...]
        gate = gate_ref[...]
        rows_b = lax.broadcasted_iota(I32, (SLOT_CHUNK, tt), 0).astype(F32).astype(BF16)
        loc_hi = lax.shift_right_logical(loc, SLOT_CHUNK.bit_length() - 1)
        loc_lo = (loc & (SLOT_CHUNK - 1)).astype(F32)
        gate_h = _bf(gate)
        gate_l = _bf(gate - gate_h.astype(F32))
        one_b = jnp.ones((SLOT_CHUNK, tt), BF16)
        for c in range(nslot // SLOT_CHUNK):
            perm = jnp.zeros((SLOT_CHUNK, tt), BF16)
            pgh = jnp.zeros((SLOT_CHUNK, tt), BF16)
            pgl = jnp.zeros((SLOT_CHUNK, tt), BF16)
            for k in range(TOP_K):
                lk = _bf(jnp.where(loc_hi[k:k + 1, :] == c, loc_lo[k:k + 1, :], -1.0))
                eq = rows_b == lk
                perm = jnp.where(eq, one_b, perm)
                pgh = jnp.where(eq, jnp.broadcast_to(gate_h[k:k + 1, :], (SLOT_CHUNK, tt)), pgh)
                pgl = jnp.where(eq, jnp.broadcast_to(gate_l[k:k + 1, :], (SLOT_CHUNK, tt)), pgl)
            cs = slice(c * SLOT_CHUNK, (c + 1) * SLOT_CHUNK)
            xp = lax.bitcast_convert_type(_dot(perm, ub), I32)
            buf[cs, 0:dh] = xp[:, 0:dh] | lax.shift_right_logical(xp[:, dh:2 * dh], 16)
            buf[cs, dh:dh + LANES] = lax.bitcast_convert_type(_dot(pgh, ones) + _dot(pgl, ones), I32)

    def step(cur):
        prv, prv2 = (cur - 1) % nbuf, (cur - 2) % nbuf

        @pl.when(i >= nbuf)
        def _():
            wait_tile(i - nbuf, cur)

        _issue_runs_inline(n8_ref, src_ref, dst_ref, jnp.maximum(i - 1, 0), i >= 1, copy_from(prv))
        build(bufs[cur])

        @pl.when(i == last)
        def _():
            _for_each_run_piece(n8_ref, src_ref, dst_ref, i, RUN_BITS, copy_from(cur))

            @pl.when(i >= 2)
            def _():
                wait_tile(i - 2, prv2)

            @pl.when(i >= 1)
            def _():
                wait_tile(i - 1, prv)
            wait_tile(i, cur)

    for cur in range(nbuf):
        pl.when(i % nbuf == cur)(functools.partial(step, cur))


def _dispatch(tabs, ztabs, u2, loc_t, gate_t, n_rows):
    n, d = u2.shape
    tt = min(MOE_TILE, n)
    nslot = TOP_K * tt + N_EXPERTS * SUBLANES
    dh = d // 2
    assert nslot % SLOT_CHUNK == 0 and tt // SUBLANES == RUN_BITS[0]
    col = pl.BlockSpec((TOP_K, tt), lambda i, *_: (0, i))
    return pl.pallas_call(
        functools.partial(_dispatch_kernel, tt=tt, nslot=nslot, dh=dh),
        out_shape=jax.ShapeDtypeStruct((n_rows, dh + LANES), I32),
        grid_spec=pltpu.PrefetchScalarGridSpec(
            num_scalar_prefetch=6, grid=(n // tt,),
            in_specs=[pl.BlockSpec((tt, d), lambda i, *_: (i, 0)), col, col],
            out_specs=pl.BlockSpec(memory_space=pl.ANY),
            scratch_shapes=[pltpu.VMEM((nslot, dh + LANES), I32)] * 3 + [
                pltpu.VMEM((EXPERT_ROWS // 2, dh + LANES), I32), pltpu.SemaphoreType.DMA((4,))]),
        compiler_params=_cparams(("arbitrary",)),
        name="dispatch",
    )(*tabs, *ztabs, u2, loc_t, gate_t)


def _experts_kernel(be_ref, nu_ref, xs_ref, wg_ref, wu_ref, wd_ref, ys_ref, *, d):
    del be_ref
    dh = d // 2

    @pl.when(pl.program_id(0) < nu_ref[0])
    def _():
        w = xs_ref[:, 0:dh]
        xa = _bf(lax.bitcast_convert_type(w & jnp.int32(-65536), F32))
        xb = _bf(lax.bitcast_convert_type(lax.shift_left(w, 16), F32))
        gate = lax.bitcast_convert_type(xs_ref[:, dh:dh + LANES], F32)
        hg = _dot(xa, _bf(wg_ref[0, 0:dh, :])) + _dot(xb, _bf(wg_ref[0, dh:d, :]))
        hu = _dot(xa, _bf(wu_ref[0, 0:dh, :])) + _dot(xb, _bf(wu_ref[0, dh:d, :]))
        y = _dot(_bf(jax.nn.silu(hg) * hu), _bf(wd_ref[0])) * jnp.tile(gate, (1, d // LANES))
        yb = lax.bitcast_convert_type(_bf(y).astype(F32), I32)
        ys_ref[...] = yb[:, 0:dh] | lax.shift_right_logical(yb[:, dh:d], 16)


def _experts(blk_expert, n_used, xs, wg, wu, wd):
    rows, xw = xs.shape
    d = wg.shape[1]
    de = wg.shape[2]
    nblk = rows // EXPERT_ROWS
    blk = lambda i, be, nu: jnp.minimum(i, nu[0] - 1)
    return pl.pallas_call(
        functools.partial(_experts_kernel, d=d),
        out_shape=jax.ShapeDtypeStruct((rows, d // 2), I32),
        grid_spec=pltpu.PrefetchScalarGridSpec(
            num_scalar_prefetch=2, grid=(nblk,),
            in_specs=[pl.BlockSpec((EXPERT_ROWS, xw), lambda i, be, nu: (blk(i, be, nu), 0)),
                      pl.BlockSpec((1, d, de), lambda i, be, nu: (be[blk(i, be, nu)], 0, 0)),
                      pl.BlockSpec((1, d, de), lambda i, be, nu: (be[blk(i, be, nu)], 0, 0)),
                      pl.BlockSpec((1, de, d), lambda i, be, nu: (be[blk(i, be, nu)], 0, 0))],
            out_specs=pl.BlockSpec((EXPERT_ROWS, d // 2), lambda i, be, nu: (blk(i, be, nu), 0))),
        compiler_params=_cparams(("arbitrary",)),
        name="experts",
    )(blk_expert, n_used, xs, wg, wu, wd)


def _combine_kernel(n8_ref, src_ref, dst_ref, tot_ref, ys_hbm, loc_ref, h_ref, u2_ref, ada_ref, wsg_ref, wsu_ref, wsd_ref,
                    gf_ref, o_ref, ybuf0, ybuf1, pick_s, sems, *, tt, nslot, d):
    i = pl.program_id(0)
    last = pl.num_programs(0) - 1
    bufs = (ybuf0, ybuf1)
    dh = d // 2
    nchunk = nslot // SLOT_CHUNK

    def copy_into(slot):
        def piece(s0, d0, rows, prio):
            pltpu.make_async_copy(
                ys_hbm.at[pl.ds(d0, rows)], bufs[slot].at[pl.ds(s0, rows)], sems.at[slot]).start(priority=prio)
        return piece

    @pl.when(i == 0)
    def _():
        ybuf0[...] = jnp.zeros(ybuf0.shape, I32)
        ybuf1[...] = jnp.zeros(ybuf1.shape, I32)
        _for_each_run_piece(n8_ref, src_ref, dst_ref, 0, RUN_BITS, copy_into(0))

    def step(cur, nxt):
        nxt_tile = jnp.minimum(i + 1, last)
        _issue_runs_inline(n8_ref, src_ref, dst_ref, nxt_tile, i < last, copy_into(nxt), range(N_EXPERTS // 2))
        x = _bf(u2_ref[...])
        shared = _dot(_bf(jax.nn.silu(_dot(x, wsg_ref[...])) * _dot(x, wsu_ref[...])), wsd_ref[...])
        loc = loc_ref[...]
        cols_b = lax.broadcasted_iota(I32, (tt, SLOT_CHUNK), 1).astype(F32).astype(BF16)
        loc_hi = lax.shift_right_logical(loc, SLOT_CHUNK.bit_length() - 1)
        loc_lo = (loc & (SLOT_CHUNK - 1)).astype(F32)
        one_b = jnp.ones((tt, SLOT_CHUNK), BF16)
        for c in range(nchunk):
            pick = jnp.zeros((tt, SLOT_CHUNK), BF16)
            for k in range(TOP_K):
                lk = _bf(jnp.where(loc_hi[:, k:k + 1] == c, loc_lo[:, k:k + 1], -1.0))
                pick = jnp.where(cols_b == lk, one_b, pick)
            pick_s[c] = pick
        _wait_rows(tot_ref[i], lambda rows: pltpu.make_async_copy(
            ys_hbm.at[pl.ds(0, rows)], bufs[cur].at[pl.ds(0, rows)], sems.at[cur]), nslot)
        _issue_runs_inline(n8_ref, src_ref, dst_ref, nxt_tile, i < last, copy_into(nxt),
                           range(N_EXPERTS // 2, N_EXPERTS))
        routed_a = jnp.zeros((tt, dh), F32)
        routed_b = jnp.zeros((tt, dh), F32)
        for c in range(nchunk):
            w = bufs[cur][c * SLOT_CHUNK:(c + 1) * SLOT_CHUNK, :]
            routed_a = routed_a + _dot(pick_s[c], _bf(lax.bitcast_convert_type(w & jnp.int32(-65536), F32)))
            routed_b = routed_b + _dot(pick_s[c], _bf(lax.bitcast_convert_type(lax.shift_left(w, 16), F32)))
        routed = jnp.concatenate([routed_a, routed_b], axis=1)
        gate2 = ada_ref[0, :, 5 * d:6 * d]
        h = h_ref[...] + gate2 * (routed + shared)
        o_ref[...] = _rms(h, gf_ref[...])

    pl.when(i % 2 == 0)(lambda: step(0, 1))
    pl.when(i % 2 == 1)(lambda: step(1, 0))


def _combine(tabs, ys, loc, h1, u2, ada3, wsg, wsu, wsd, gf, seq):
    n, d = h1.shape
    tt = min(MOE_TILE, n)
    nslot = TOP_K * tt + N_EXPERTS * SUBLANES
    per_b = seq // tt
    row = pl.BlockSpec((tt, d), lambda i, *_: (i, 0))
    full = lambda a: pl.BlockSpec(a.shape, lambda i, *_: (0,) * a.ndim)
    wsg, wsu, wsd = wsg.astype(BF16), wsu.astype(BF16), wsd.astype(BF16)
    gf = gf.reshape(1, d)
    return pl.pallas_call(
        functools.partial(_combine_kernel, tt=tt, nslot=nslot, d=d),
        out_shape=jax.ShapeDtypeStruct((n, d), F32),
        grid_spec=pltpu.PrefetchScalarGridSpec(
            num_scalar_prefetch=4, grid=(n // tt,),
            in_specs=[pl.BlockSpec(memory_space=pl.ANY),
                      pl.BlockSpec((tt, TOP_K), lambda i, *_: (i, 0)),
                      row, row,
                      pl.BlockSpec((1, 1, ada3.shape[2]), lambda i, *_: (i // per_b, 0, 0)),
                      full(wsg), full(wsu), full(wsd), full(gf)],
            out_specs=row,
            scratch_shapes=[pltpu.VMEM((nslot, d // 2), I32), pltpu.VMEM((nslot, d // 2), I32),
                            pltpu.VMEM((nslot // SLOT_CHUNK, tt, SLOT_CHUNK), BF16),
                            pltpu.SemaphoreType.DMA((2,))]),
        compiler_params=_cparams(("arbitrary",)),
        name="combine",
    )(*tabs, ys, loc, h1, u2, ada3, wsg, wsu, wsd, gf)


def _moe(h1, u2, ada3, w_router, router_bias, wg, wu, wd, wsg, wsu, wsd, gf):
    bsz, seq, d = h1.shape
    n = bsz * seq
    assert seq % min(MOE_TILE, n) == 0
    h1f = h1.reshape(n, d)
    u2f = u2.reshape(n, d)
    gate_t, loc_t, c8, loff, run0, tot = _route(u2f, w_router, router_bias)
    nt = c8.shape[0]
    as_tab = lambda a: a[:, :, 0].astype(I32)
    tot8 = tot[:, 0].astype(I32)
    padded = (tot8 + EXPERT_ROWS - 1) // EXPERT_ROWS * EXPERT_ROWS
    pend = jnp.cumsum(padded)
    pstart = (pend - padded).astype(I32)
    nblk = (n * TOP_K + nt * N_EXPERTS * (SUBLANES - 1) + N_EXPERTS * (EXPERT_ROWS - 1) + EXPERT_ROWS - 1) // EXPERT_ROWS
    blk_row0 = jnp.arange(nblk, dtype=I32) * EXPERT_ROWS
    blk_expert = jnp.minimum(jnp.sum(pend[None, :] <= blk_row0[:, None], axis=1), N_EXPERTS - 1).astype(I32)
    n_used = (pend[-1:] // EXPERT_ROWS).astype(I32)
    n8 = as_tab(c8) // SUBLANES
    tabs = (n8.reshape(-1), as_tab(loff).reshape(-1), (pstart[None, :] + as_tab(run0)).reshape(-1),
            jnp.sum(n8, axis=1))
    ztabs = ((padded - tot8) // SUBLANES, pstart + tot8)
    xs = _dispatch(tabs, ztabs, u2f, loc_t, gate_t, nblk * EXPERT_ROWS)
    ys = _experts(blk_expert, n_used, xs, wg, wu, wd)
    out = _combine(tabs, ys, loc_t.T, h1f, u2f, ada3, wsg, wsu, wsd, gf, seq)
    return out.reshape(bsz, seq, d)


def kernel(x, c, w_ada, b_ada, norm1_g, w_in, ssm_lambda_re, ssm_lambda_im, ssm_log_dt, ssm_b_re, ssm_b_im,
           ssm_c_re, ssm_c_im, ssm_d, ssm_w_glu, ssm_b_glu, w_proj_ssm, w_proj_attn, w_out, norm2_g, w_router,
           router_bias, w_exp_gate, w_exp_up, w_exp_down, w_sh_gate, w_sh_up, w_sh_down, norm_f_g):
    depth = w_ada.shape[0]
    assert depth == 1, "the final norm is fused into the last (only) layer's combine kernel"
    bsz, seq, d = x.shape
    layer = 0
    ada3 = _ada(c, w_ada[layer], b_ada[layer]).reshape(bsz, 1, 6 * d)
    us, k, ki, qt, qit, vt, wit = _inproj(x, ada3, norm1_g[layer], w_in[layer])
    a_re, a_im, bb_re, bb_im = _s5disc(ssm_lambda_re[layer], ssm_lambda_im[layer], ssm_log_dt[layer],
                                       ssm_b_re[layer], ssm_b_im[layer])
    ys_t = _s5(us, a_re, a_im, bb_re, bb_im, ssm_c_re[layer], ssm_c_im[layer],
               ssm_d[layer], ssm_w_glu[layer], ssm_b_glu[layer])
    ya = _dsa(qt, qit, wit, k, ki, vt)
    h1, u2 = _mix(x, ys_t, ya, ada3, norm1_g[layer], w_in[layer], w_proj_ssm[layer],
                  w_proj_attn[layer], w_out[layer], norm2_g[layer])
    return _moe(h1, u2, ada3, w_router[layer], router_bias[layer], w_exp_gate[layer], w_exp_up[layer],
                w_exp_down[layer], w_sh_gate[layer], w_sh_up[layer], w_sh_down[layer], norm_f_g)
```

```python
import functools
import math

import jax
import jax.numpy as jnp
import numpy as np
from jax import lax
from jax.experimental import pallas as pl
from jax.experimental.pallas import tpu as pltpu

F32 = jnp.float32
BF16 = jnp.bfloat16
I32 = jnp.int32

SSM_GROUP = 16
SSM_STATE = 64
N_HEADS = 8
HEAD_DIM = 64
IDX_HEADS = 8
IDX_DIM = 64
TOPK_MAX = 256
N_EXPERTS = 64
TOP_K = 8
N_GROUPS = 8
TOPK_GROUPS = 4
ROUTED_SCALE = 2.5
EPS = 1e-6

V7X_VMEM_LIMIT_BYTES = 56 * 1024 * 1024
LANES = 128
SUBLANES = 8

INPROJ_ROWS = 512
S5_STEPS = 64
S5_LANE_CHUNK = 128
DSA_Q_COLS = 256
DSA_K_ROWS = 512
DSA_COUNT_ROWS = 64
BITSLICE_ROWS = 256
POS_SPLIT = 64
MIX_ROWS = 512
MOE_TILE = 256
ROUTE_TILES = 2
SLOT_CHUNK = 256
EXPERT_ROWS = 1024

NEG_BIG = -1e30
INT_MIN = -(2 ** 31)


def _cparams(sem):
    return pltpu.CompilerParams(dimension_semantics=sem, vmem_limit_bytes=V7X_VMEM_LIMIT_BYTES)


def _bf(x):
    return x.astype(BF16)


def _dot(a, b):
    return jnp.dot(a, b, preferred_element_type=F32)


def _dot_nt(a, b):
    return lax.dot_general(a, b, (((1,), (1,)), ((), ())), preferred_element_type=F32)


def _split(x):
    hi = _bf(x)
    lo = _bf(x - hi.astype(F32))
    return hi, lo


def _dot3(a, b):
    ah, al = _split(a)
    bh, bl = _split(b)
    return _dot(ah, bh) + (_dot(ah, bl) + _dot(al, bh))


def _rms(x, g):
    return x * lax.rsqrt(jnp.mean(x * x, axis=-1, keepdims=True) + EPS) * g


def _ada_kernel(c_ref, w_ref, b_ref, o_ref):
    c = c_ref[...]
    o_ref[...] = _dot3(c * jax.nn.sigmoid(c), w_ref[...]) + b_ref[...]


def _ada(c, w, b):
    bsz, d = c.shape
    n = w.shape[1]
    tn = 1024
    return pl.pallas_call(
        _ada_kernel,
        out_shape=jax.ShapeDtypeStruct((bsz, n), F32),
        grid=(n // tn,),
        in_specs=[pl.BlockSpec((bsz, d), lambda j: (0, 0)),
                  pl.BlockSpec((d, tn), lambda j: (0, j)),
                  pl.BlockSpec((1, tn), lambda j: (0, j))],
        out_specs=pl.BlockSpec((bsz, tn), lambda j: (0, j)),
        compiler_params=_cparams(("arbitrary",)),
        name="ada",
    )(c, w, b.reshape(1, n))


ALIBI_SLOPES = tuple(2.0 ** (-8.0 * (h + 1) / N_HEADS) for h in range(N_HEADS))
QAUG_ROWS = 16


def _inproj_kernel(x_ref, ada_ref, g1_ref, w_ref, wt_ref,
                   us_ref, k_ref, ki_ref, qt_ref, qit_ref, vt_ref, wit_ref, *, d, ssm_w, attn_w, idx_w, tl):
    x = x_ref[0]
    shift = ada_ref[0, :, 0:d]
    scale = ada_ref[0, :, d:2 * d]
    u = _bf(_rms(x, g1_ref[...]) * (1.0 + scale) + shift)
    r = _dot(u, w_ref[...])
    us_ref[0] = r[:, 0:ssm_w]
    k_ref[0] = _bf(r[:, ssm_w:ssm_w + HEAD_DIM])
    ki_ref[0] = _bf(r[:, ssm_w + LANES:ssm_w + LANES + IDX_DIM])
    rt = _dot_nt(wt_ref[...], u)
    arow = lax.broadcasted_iota(I32, (QAUG_ROWS, tl), 0)
    for h in range(N_HEADS):
        base = h * LANES
        qt_ref[0, base:base + HEAD_DIM, :] = _bf(rt[h * HEAD_DIM:(h + 1) * HEAD_DIM])
        qt_ref[0, base + HEAD_DIM:base + HEAD_DIM + QAUG_ROWS, :] = _bf(jnp.where(arow < 2, ALIBI_SLOPES[h], 0.0))
        qt_ref[0, base + HEAD_DIM + QAUG_ROWS:base + LANES, :] = jnp.zeros((LANES - HEAD_DIM - QAUG_ROWS, tl), BF16)
    qit_ref[0] = _bf(rt[attn_w:attn_w + idx_w])
    vt_ref[0] = _bf(rt[attn_w + idx_w:attn_w + idx_w + HEAD_DIM])
    wit_ref[0] = rt[attn_w + idx_w + HEAD_DIM:attn_w + idx_w + HEAD_DIM + IDX_HEADS]


def _split_w_in(w_in, d):
    ssm_w = 512
    sizes = (ssm_w, N_HEADS * HEAD_DIM, HEAD_DIM, HEAD_DIM, IDX_HEADS * IDX_DIM, IDX_DIM, IDX_HEADS, d, d)
    offs = [0]
    for s in sizes:
        offs.append(offs[-1] + s)
    return [w_in[:, offs[i]:offs[i + 1]] for i in range(9)]


def _inproj(x, ada3, g1, w_in):
    bsz, seq, d = x.shape
    ssm_w = 512
    attn_w = N_HEADS * HEAD_DIM
    idx_w = IDX_HEADS * IDX_DIM
    w_ssm, w_q, w_k, w_v, w_qi, w_ki, w_wi, _, _ = _split_w_in(w_in, d)
    zpad = lambda n: jnp.zeros((d, n), F32)
    wbig = jnp.concatenate([w_ssm, w_k, zpad(LANES - HEAD_DIM), w_ki, zpad(LANES - IDX_DIM)], axis=1).astype(BF16)
    wt = jnp.concatenate([w_q * (HEAD_DIM ** -0.5), w_qi * (IDX_DIM ** -0.5), w_v, w_wi,
                          zpad(LANES - HEAD_DIM - IDX_HEADS)], axis=1).T.astype(BF16)
    tl = INPROJ_ROWS
    kern = functools.partial(_inproj_kernel, d=d, ssm_w=ssm_w, attn_w=attn_w, idx_w=idx_w, tl=tl)
    row = lambda w: pl.BlockSpec((1, tl, w), lambda b, l: (b, l, 0))
    colt = lambda h: pl.BlockSpec((1, h, tl), lambda b, l: (b, 0, l))
    full = lambda a: pl.BlockSpec(a.shape, lambda b, l: (0,) * a.ndim)
    return pl.pallas_call(
        kern,
        out_shape=(jax.ShapeDtypeStruct((bsz, seq, ssm_w), F32),
                   jax.ShapeDtypeStruct((bsz, seq, HEAD_DIM), BF16),
                   jax.ShapeDtypeStruct((bsz, seq, IDX_DIM), BF16),
                   jax.ShapeDtypeStruct((bsz, N_HEADS * LANES, seq), BF16),
                   jax.ShapeDtypeStruct((bsz, idx_w, seq), BF16),
                   jax.ShapeDtypeStruct((bsz, HEAD_DIM, seq), BF16),
                   jax.ShapeDtypeStruct((bsz, IDX_HEADS, seq), F32)),
        grid=(bsz, seq // tl),
        in_specs=[row(d),
                  pl.BlockSpec((1, 1, ada3.shape[2]), lambda b, l: (b, 0, 0)),
                  pl.BlockSpec((1, d), lambda b, l: (0, 0)),
                  full(wbig), full(wt)],
        out_specs=(row(ssm_w), row(HEAD_DIM), row(IDX_DIM),
                   colt(N_HEADS * LANES), colt(idx_w), colt(HEAD_DIM), colt(IDX_HEADS)),
        compiler_params=_cparams(("arbitrary", "arbitrary")),
        name="inproj",
    )(x, ada3, g1.reshape(1, d), wbig, wt)


def _s5disc_kernel(lr_ref, li_ref, ldt_ref, br_ref, bi_ref, are_ref, aim_ref, bbr_ref, bbi_ref):
    lr = lr_ref[...]
    li = li_ref[...]
    dt = jnp.exp(ldt_ref[...])
    mag = jnp.exp(lr * dt)
    a_re = mag * jnp.cos(li * dt)
    a_im = mag * jnp.sin(li * dt)
    den = lr * lr + li * li
    n_re = a_re - 1.0
    f_re = (n_re * lr + a_im * li) / den
    f_im = (a_im * lr - n_re * li) / den
    br = br_ref[...]
    bi = bi_ref[...]
    are_ref[...] = a_re
    aim_ref[...] = a_im
    bbr_ref[...] = f_re * br - f_im * bi
    bbi_ref[...] = f_re * bi + f_im * br


def _s5disc(lam_re, lam_im, log_dt, b_re, b_im):
    g, p = lam_re.shape
    h = b_re.shape[2]
    rep = lambda a: jnp.repeat(a, h, axis=1)
    ldt = jnp.broadcast_to(log_dt[:, None], (g, p * h))
    sds = jax.ShapeDtypeStruct((g, p * h), F32)
    a_re, a_im, bb_re, bb_im = pl.pallas_call(
        _s5disc_kernel, out_shape=(sds, sds, sds, sds), name="s5disc",
    )(rep(lam_re), rep(lam_im), ldt, b_re.reshape(g, p * h), b_im.reshape(g, p * h))
    return a_re[:, ::h], a_im[:, ::h], bb_re.reshape(g, p, h), bb_im.reshape(g, p, h)


def _s5_kernel(u_ref, wb_ref, ar_ref, ai_ref, cc_ref, dsk_ref, wg_ref, bg_ref, o_ref, buf, hst, *, tl, width):
    nch = width // S5_LANE_CHUNK
    sw = S5_LANE_CHUNK // SSM_GROUP * SSM_STATE
    rows = tl * SUBLANES

    @pl.when(pl.program_id(0) == 0)
    def _():
        hst[...] = jnp.zeros_like(hst)

    u = jnp.swapaxes(u_ref[...], 0, 1).reshape(rows, width)
    ub = _bf(u)
    for j in range(nch):
        buf[:, j * 2 * sw:(j + 1) * 2 * sw] = _dot(ub[:, j * S5_LANE_CHUNK:(j + 1) * S5_LANE_CHUNK], wb_ref[j])

    for j in range(nch):
        re_cols = slice(j * 2 * sw, j * 2 * sw + sw)
        im_cols = slice(j * 2 * sw + sw, (j + 1) * 2 * sw)
        a_re = jnp.broadcast_to(ar_ref[:, j * sw:(j + 1) * sw], (SUBLANES, sw))
        a_im = jnp.broadcast_to(ai_ref[:, j * sw:(j + 1) * sw], (SUBLANES, sw))

        def step(t, carry, re_cols=re_cols, im_cols=im_cols, a_re=a_re, a_im=a_im):
            h_re, h_im = carry
            r0 = pl.multiple_of(t * SUBLANES, SUBLANES)
            n_re = (a_re * h_re - a_im * h_im) + buf[pl.ds(r0, SUBLANES), re_cols]
            n_im = (a_re * h_im + a_im * h_re) + buf[pl.ds(r0, SUBLANES), im_cols]
            buf[pl.ds(r0, SUBLANES), re_cols] = n_re
            buf[pl.ds(r0, SUBLANES), im_cols] = n_im
            return n_re, n_im

        h_re, h_im = lax.fori_loop(0, tl, step, (hst[:, re_cols], hst[:, im_cols]), unroll=True)
        hst[:, re_cols] = h_re
        hst[:, im_cols] = h_im

    ys = [_dot(_bf(buf[:, j * 2 * sw:(j + 1) * 2 * sw]), cc_ref[j]) for j in range(nch)]
    y = jnp.concatenate(ys, axis=1) + dsk_ref[...] * u
    y = jax.nn.gelu(y)
    y = y * jax.nn.sigmoid(_dot(_bf(y), wg_ref[...]) + bg_ref[...])
    o_ref[...] = _bf(jnp.swapaxes(y.reshape(tl, SUBLANES, width), 0, 1))


def _s5(u_t, a_re, a_im, bb_re, bb_im, c_re, c_im, d_skip, w_glu, b_glu):
    bsz, seq, width = u_t.shape
    assert bsz == SUBLANES
    nch = width // S5_LANE_CHUNK
    gpc = S5_LANE_CHUNK // SSM_GROUP
    sw = gpc * SSM_STATE
    eye = jnp.eye(gpc, dtype=F32)

    def bmat(bb):
        t = bb.reshape(nch, gpc, SSM_STATE, SSM_GROUP).transpose(0, 1, 3, 2)
        return jnp.einsum('jghp,gk->jghkp', t, eye).reshape(nch, S5_LANE_CHUNK, sw)

    def cmat(cc):
        t = cc.reshape(nch, gpc, SSM_GROUP, SSM_STATE).transpose(0, 1, 3, 2)
        return jnp.einsum('jgph,gk->jgpkh', t, eye).reshape(nch, sw, S5_LANE_CHUNK)

    wb = jnp.concatenate([bmat(bb_re), bmat(bb_im)], axis=2)
    cc = jnp.concatenate([cmat(c_re), -cmat(c_im)], axis=1)
    tl = S5_STEPS
    full = lambda a: pl.BlockSpec(a.shape, lambda i: (0,) * a.ndim)
    args = (u_t, wb.astype(BF16), a_re.reshape(1, -1), a_im.reshape(1, -1), cc.astype(BF16),
            d_skip.reshape(1, width), w_glu.astype(BF16), b_glu.reshape(1, width))
    return pl.pallas_call(
        functools.partial(_s5_kernel, tl=tl, width=width),
        out_shape=jax.ShapeDtypeStruct((bsz, seq, width), BF16),
        grid=(seq // tl,),
        in_specs=[pl.BlockSpec((bsz, tl, width), lambda i: (0, i, 0))] + [full(a) for a in args[1:]],
        out_specs=pl.BlockSpec((bsz, tl, width), lambda i: (0, i, 0)),
        scratch_shapes=[pltpu.VMEM((tl * SUBLANES, nch * 2 * sw), F32),
                        pltpu.VMEM((SUBLANES, nch * 2 * sw), F32)],
        compiler_params=_cparams(("arbitrary",)),
        name="s5",
    )(*args)


def _bit_transpose32(words):
    x = list(words)
    j, m = 16, 0x0000FFFF
    while j:
        k = 0
        while k < 32:
            t = (x[k] ^ lax.shift_right_logical(x[k + j], jnp.int32(j))) & jnp.int32(m - (1 << 32) if m >= 1 << 31 else m)
            x[k] = x[k] ^ t
            x[k + j] = x[k + j] ^ lax.shift_left(t, jnp.int32(j))
            k = (k + j + 1) & ~j
        j >>= 1
        m = (m ^ (m << j)) & 0xFFFFFFFF
    return x


def _dsa_kernel(qt_ref, qit_ref, wit_ref, ka_ref, ki_ref, vt_ref, o_ref, key_s, mb_s, acc_s, pl_s, p_s, *, tq, tk, topk, seq):
    i = pl.program_id(1)
    q0 = i * tq
    nkt = (q0 + tq + tk - 1) // tk
    ch = DSA_COUNT_ROWS
    krow = lax.broadcasted_iota(I32, (tk, tq), 0)
    qcol = q0 + lax.broadcasted_iota(I32, (tk, tq), 1)
    crow = lax.broadcasted_iota(I32, (ch, tq), 0)

    wb = wit_ref[0] * (IDX_HEADS ** -0.5)

    def score_tile(j, _):
        r0 = pl.multiple_of(j * tk, tk)
        kit = ki_ref[0, pl.ds(r0, tk), :]
        acc = jnp.zeros((tk, tq), F32)
        for h in range(IDX_HEADS):
            s = _dot(kit, qit_ref[0, h * IDX_DIM:(h + 1) * IDX_DIM, :])
            acc = acc + wb[h:h + 1, :] * jnp.maximum(s, 0.0)
        bits = lax.bitcast_convert_type(acc, I32)
        key = jnp.where(bits < 0, bits ^ jnp.int32(0x7FFFFFFF), bits)
        key = jnp.where(acc == 0.0, 0, key)
        key = jnp.where(krow + r0 <= qcol, key, INT_MIN)
        key_s[pl.ds(r0, tk), :] = key
        ukey = key ^ INT_MIN
        for c in range(tk // BITSLICE_ROWS):
            words = [ukey[c * BITSLICE_ROWS + v * SUBLANES:c * BITSLICE_ROWS + (v + 1) * SUBLANES, :]
                     for v in range(32)]
            planes = _bit_transpose32(words)
            g0 = pl.multiple_of((j * (tk // BITSLICE_ROWS) + c) * SUBLANES, SUBLANES)
            for it in range(32):
                pl_s[it, pl.ds(g0, SUBLANES), :] = planes[it]
        return 0

    @pl.when((pl.program_id(0) == 0) & (i == 0))
    def _():
        pl_s[...] = jnp.zeros(pl_s.shape, I32)

    def score_pair(jj, _):
        score_tile(2 * jj, 0)
        score_tile(2 * jj + 1, 0)
        return 0

    lax.fori_loop(0, nkt // 2, score_pair, 0)

    @pl.when(nkt % 2 == 1)
    def _():
        score_tile(nkt - 1, 0)

    def count(pred):
        def tile(j, cnt):
            for c in range(tk // ch):
                rr = pl.multiple_of(j * tk + c * ch, ch)
                cnt = cnt + jnp.where(pred(key_s[pl.ds(rr, ch), :], rr), 1, 0)
            return cnt
        cnt = lax.fori_loop(0, nkt, tile, jnp.zeros((ch, tq), I32))
        return jnp.sum(cnt.astype(F32), axis=0, keepdims=True)

    ngrp = seq // 32

    def lane_count(words):
        pc = lax.population_count(words).reshape(ngrp // SUBLANES, SUBLANES, tq)
        return jnp.sum(jnp.sum(pc, axis=0).astype(F32), axis=0, keepdims=True)

    def bit_step(it, carry):
        alive, above, ans_u = carry
        ones = alive & pl_s[it]
        cnt1 = lane_count(ones)
        take = above + cnt1 >= float(topk)
        alive = jnp.where(take, ones, alive ^ ones)
        above = jnp.where(take, above, above + cnt1)
        ans_u = jnp.where(take, ans_u | lax.shift_left(jnp.int32(1), 31 - it), ans_u)
        return alive, above, ans_u

    grow = lax.broadcasted_iota(I32, (ngrp, tq), 0)
    alive0 = jnp.where(grow < nkt * (tk // 32), -1, 0)
    alive, above, ans_u = lax.fori_loop(
        0, 32, bit_step, (alive0, jnp.zeros((1, tq), F32), jnp.zeros((1, tq), I32)))
    thr = jnp.maximum(ans_u ^ INT_MIN, INT_MIN + 1)
    cnt_ge = above + lane_count(alive)
    tied = jnp.where(ans_u != 0, cnt_ge, 0.0) > float(topk)
    has_ties = jnp.max(jnp.where(tied, 1.0, 0.0)) > 0.0

    def tie_cut():
        need = float(topk) - count(lambda kb, rr: kb > thr)
        nbits = max(1, (seq - 1).bit_length())

        def idx_step(b, x):
            cand = x | lax.shift_left(jnp.int32(1), nbits - 1 - b)
            below = count(lambda kb, rr: jnp.where(kb == thr, crow + rr, seq) < cand)
            return jnp.where(below < need, cand, x)

        x = lax.fori_loop(0, nbits, idx_step, jnp.zeros((1, tq), I32))
        return jnp.where(tied, x, seq)

    cut = lax.cond(has_ties, tie_cut, lambda: jnp.full((1, tq), seq, I32))

    def bias_tile(j, _):
        for c in range(tk // ch):
            rr = pl.multiple_of(j * tk + c * ch, ch)
            kb = key_s[pl.ds(rr, ch), :]
            tie_bias = jnp.where(crow + rr <= cut, 0.0, NEG_BIG)
            mb_s[pl.ds(rr, ch), :] = jnp.where(kb > thr, 0.0, jnp.where(kb == thr, tie_bias, NEG_BIG))
        return 0

    def logits(j, h):
        r0 = pl.multiple_of(j * tk, tk)
        s = _dot(ka_ref[0, pl.ds(r0, tk), :], qt_ref[0, h * LANES:(h + 1) * LANES, :]) + mb_s[pl.ds(r0, tk), :]
        return s.reshape(tk // SUBLANES, SUBLANES, tq)

    acc_s[...] = jnp.zeros(acc_s.shape, F32)

    def attn_tile(j, carry):
        ms, ls = carry
        r0 = pl.multiple_of(j * tk, tk)
        bias_tile(j, 0)
        new_m, new_l, alphas = [], [], []
        for h in range(N_HEADS):
            s = logits(j, h)
            m_new = jnp.maximum(ms[h], jnp.max(jnp.max(s, axis=0), axis=0, keepdims=True))
            alpha = jnp.exp(ms[h] - m_new)
            p = jnp.exp(s - m_new)
            new_m.append(m_new)
            new_l.append(alpha * ls[h] + jnp.sum(p, axis=0))
            alphas.append(alpha)
            p_s[h] = _bf(p.reshape(tk, tq))
        for h in range(N_HEADS):
            rows = slice(h * HEAD_DIM, (h + 1) * HEAD_DIM)
            acc_s[rows, :] = alphas[h] * acc_s[rows, :] + _dot(vt_ref[0, :, pl.ds(r0, tk)], p_s[h])
        return tuple(new_m), tuple(new_l)

    init = ((jnp.full((1, tq), NEG_BIG, F32),) * N_HEADS, (jnp.zeros((SUBLANES, tq), F32),) * N_HEADS)
    _, ls = lax.fori_loop(0, nkt, attn_tile, init)
    for h in range(N_HEADS):
        rows = slice(h * HEAD_DIM, (h + 1) * HEAD_DIM)
        acc_s[rows, :] = acc_s[rows, :] / jnp.sum(ls[h], axis=0, keepdims=True)
    o_ref[0] = _bf(acc_s[...].T)


def _dsa(qt, qit, wit, k, ki, vt):
    bsz, seq = k.shape[0], k.shape[1]
    aw = N_HEADS * HEAD_DIM
    tq = min(DSA_Q_COLS, seq)
    tk = min(DSA_K_ROWS, seq)
    topk = min(TOPK_MAX, seq // 4)
    assert (seq - 1) // POS_SPLIT < 256 and POS_SPLIT <= 256, "key positions must split into two bf16-exact parts"
    assert all(float(np.float32(sl).astype(BF16)) == sl for sl in ALIBI_SLOPES), "ALiBi slopes must be bf16-exact"
    pos = jnp.arange(seq, dtype=I32)
    posc = jnp.stack([(pos // POS_SPLIT) * POS_SPLIT, pos % POS_SPLIT], axis=1).astype(BF16)
    ka = jnp.concatenate([k, jnp.broadcast_to(posc[None], (bsz, seq, 2)),
                          jnp.zeros((bsz, seq, LANES - HEAD_DIM - 2), BF16)], axis=2)
    kern = functools.partial(_dsa_kernel, tq=tq, tk=tk, topk=topk, seq=seq)
    cols = lambda r: pl.BlockSpec((1, r, tq), lambda b, i: (b, 0, i))
    return pl.pallas_call(
        kern,
        out_shape=jax.ShapeDtypeStruct((bsz, seq, aw), BF16),
        grid=(bsz, seq // tq),
        in_specs=[cols(N_HEADS * LANES), cols(qit.shape[1]), cols(IDX_HEADS),
                  pl.BlockSpec((1, seq, LANES), lambda b, i: (b, 0, 0)),
                  pl.BlockSpec((1, seq, IDX_DIM), lambda b, i: (b, 0, 0)),
                  pl.BlockSpec((1, HEAD_DIM, seq), lambda b, i: (b, 0, 0))],
        out_specs=pl.BlockSpec((1, tq, aw), lambda b, i: (b, i, 0)),
        scratch_shapes=[pltpu.VMEM((seq, tq), I32), pltpu.VMEM((seq, tq), F32), pltpu.VMEM((aw, tq), F32),
                        pltpu.VMEM((32, seq // 32, tq), I32), pltpu.VMEM((N_HEADS, tk, tq), BF16)],
        compiler_params=_cparams(("arbitrary", "arbitrary")),
        name="dsa",
    )(qt, qit, wit, ka, ki, vt)


def _mix_kernel(x_ref, ys_ref, ya_ref, ada_ref, g1_ref, wgt_ref, wps_ref, wpa_ref, wo_ref, g2_ref,
                h_ref, u2_ref, *, d):
    gate1 = ada_ref[0, :, 2 * d:3 * d]
    shift2 = ada_ref[0, :, 3 * d:4 * d]
    scale2 = ada_ref[0, :, 4 * d:5 * d]
    x = x_ref[0]
    u = _bf(_rms(x, g1_ref[...]) * (1.0 + ada_ref[0, :, d:2 * d]) + ada_ref[0, :, 0:d])
    g = _dot(u, wgt_ref[...])
    mixed = (jax.nn.sigmoid(g[:, 0:d]) * _dot(ys_ref[0], wps_ref[...])
             + jax.nn.sigmoid(g[:, d:2 * d]) * _dot(ya_ref[0], wpa_ref[...]))
    h = x + gate1 * _dot(_bf(mixed), wo_ref[...])
    h_ref[0] = h
    u2_ref[0] = _rms(h, g2_ref[...]) * (1.0 + scale2) + shift2


def _mix(x, ys, ya, ada3, g1, w_in, wps, wpa, wo, g2):
    bsz, seq, d = x.shape
    tm = MIX_ROWS
    row = lambda w: pl.BlockSpec((1, tm, w), lambda b, l: (b, l, 0))
    full = lambda a: pl.BlockSpec(a.shape, lambda b, l: (0,) * a.ndim)
    wps, wpa, wo = wps.astype(BF16), wpa.astype(BF16), wo.astype(BF16)
    wgt = jnp.concatenate(_split_w_in(w_in, d)[7:9], axis=1).astype(BF16)
    g1 = g1.reshape(1, d)
    g2 = g2.reshape(1, d)
    return pl.pallas_call(
        functools.partial(_mix_kernel, d=d),
        out_shape=(jax.ShapeDtypeStruct((bsz, seq, d), F32), jax.ShapeDtypeStruct((bsz, seq, d), F32)),
        grid=(bsz, seq // tm),
        in_specs=[row(d), row(ys.shape[2]), row(ya.shape[2]),
                  pl.BlockSpec((1, 1, ada3.shape[2]), lambda b, l: (b, 0, 0)),
                  full(g1), full(wgt), full(wps), full(wpa), full(wo), full(g2)],
        out_specs=(row(d), row(d)),
        compiler_params=_cparams(("arbitrary", "arbitrary")),
        name="mix",
    )(x, ys, ya, ada3, g1, wgt, wps, wpa, wo, g2)


def _first_max(cur, idx, axis, big):
    m = jnp.max(cur, axis=axis, keepdims=True)
    first = jnp.min(jnp.where(cur == m, idx, big), axis=axis, keepdims=True)
    return m, idx == first


def _route_kernel(u_ref, wrh_ref, wrl_ref, rb_ref, tri_ref, ltri_ref,
                  gt_ref, loc_ref, c8_ref, loff_ref, run0_ref, tot_ref, run_s, *, t, tt):
    @pl.when(pl.program_id(0) == 0)
    def _():
        run_s[...] = jnp.zeros_like(run_s)

    uh, ul = _split(u_ref[...])
    logits = _dot_nt(wrh_ref[...], uh) + (_dot_nt(wrl_ref[...], uh) + _dot_nt(wrh_ref[...], ul))
    scores = jax.nn.sigmoid(logits)
    biased = scores + rb_ref[...]
    per_group = N_EXPERTS // N_GROUPS
    b3 = biased.reshape(N_GROUPS, per_group, t)
    i3 = lax.broadcasted_iota(I32, b3.shape, 1)
    m1, hit1 = _first_max(b3, i3, 1, per_group)
    m2 = jnp.max(jnp.where(hit1, -jnp.inf, b3), axis=1, keepdims=True)
    gs = (m1 + m2).reshape(N_GROUPS, t)
    gi = lax.broadcasted_iota(I32, gs.shape, 0)
    gsel = jnp.zeros(gs.shape, F32)
    for _ in range(TOPK_GROUPS):
        _, hit = _first_max(gs, gi, 0, N_GROUPS)
        gsel = jnp.where(hit, 1.0, gsel)
        gs = jnp.where(hit, -jnp.inf, gs)
    cur = jnp.where(gsel.reshape(N_GROUPS, 1, t) > 0.0, b3, -jnp.inf).reshape(N_EXPERTS, t)
    ei = lax.broadcasted_iota(I32, cur.shape, 0)
    hits = []
    gates = []
    for _ in range(TOP_K):
        _, hit = _first_max(cur, ei, 0, N_EXPERTS)
        hits.append(hit)
        gates.append(jnp.sum(jnp.where(hit, scores, 0.0), axis=0, keepdims=True))
        cur = jnp.where(hit, -jnp.inf, cur)
    gate = jnp.concatenate(gates, axis=0)
    gt_ref[...] = gate / jnp.sum(gate, axis=0, keepdims=True) * ROUTED_SCALE
    onehot = jnp.zeros(cur.shape, F32)
    for hit in hits:
        onehot = jnp.where(hit, 1.0, onehot)
    for sub in range(t // tt):
        cols = slice(sub * tt, (sub + 1) * tt)
        oh = onehot[:, cols]
        cnt = jnp.sum(oh, axis=1, keepdims=True)
        c8 = jnp.floor((cnt + (SUBLANES - 1)) * (1.0 / SUBLANES)) * SUBLANES
        c8l = jnp.broadcast_to(c8, (N_EXPERTS, LANES))
        loff = _dot(ltri_ref[...], _bf(c8l))
        slot = _dot(_bf(oh), tri_ref[...]) + loff[:, 0:1]
        loc_ref[:, cols] = jnp.concatenate(
            [jnp.sum(jnp.where(hit[:, cols], slot, 0.0), axis=0, keepdims=True) for hit in hits],
            axis=0).astype(I32)
        c8_ref[sub] = c8l
        loff_ref[sub] = loff
        run0_ref[sub] = run_s[...]
        run_s[...] = run_s[...] + c8
    tot_ref[...] = run_s[...]


def _route(u2, w_router, router_bias):
    n, d = u2.shape
    tt = min(MOE_TILE, n)
    t = min(ROUTE_TILES * tt, n)
    nt = n // tt
    wt = w_router.T
    wrh = wt.astype(BF16)
    wrl = (wt - wrh.astype(F32)).astype(BF16)
    tri = (jnp.arange(tt)[:, None] < jnp.arange(tt)[None, :]).astype(BF16)
    ex = jnp.arange(N_EXPERTS)
    ltri = (ex[None, :] < ex[:, None]).astype(BF16)
    full = lambda a: pl.BlockSpec(a.shape, lambda i: (0,) * a.ndim)
    col = pl.BlockSpec((TOP_K, t), lambda i: (0, i))
    tab = pl.BlockSpec((t // tt, N_EXPERTS, LANES), lambda i: (i, 0, 0))
    tab_sds = jax.ShapeDtypeStruct((nt, N_EXPERTS, LANES), F32)
    rb = router_bias.reshape(N_EXPERTS, 1)
    return pl.pallas_call(
        functools.partial(_route_kernel, t=t, tt=tt),
        out_shape=(jax.ShapeDtypeStruct((TOP_K, n), F32), jax.ShapeDtypeStruct((TOP_K, n), I32),
                   tab_sds, tab_sds, tab_sds, jax.ShapeDtypeStruct((N_EXPERTS, LANES), F32)),
        grid=(n // t,),
        in_specs=[pl.BlockSpec((t, d), lambda i: (i, 0)), full(wrh), full(wrl), full(rb), full(tri), full(ltri)],
        out_specs=(col, col, tab, tab, tab, pl.BlockSpec((N_EXPERTS, LANES), lambda i: (0, 0))),
        scratch_shapes=[pltpu.VMEM((N_EXPERTS, LANES), F32)],
        compiler_params=_cparams(("arbitrary",)),
        name="route",
    )(u2, wrh, wrl, rb, tri, ltri)


RUN_BITS = tuple(1 << b for b in reversed(range((MOE_TILE // SUBLANES).bit_length())))
RUN_LONG = 8


def _for_each_run_piece(n8_ref, src_ref, dst_ref, tile, bits, fn):
    def per_expert(e, _):
        idx = tile * N_EXPERTS + e
        n8 = n8_ref[idx]
        src = src_ref[idx]
        dst = dst_ref[idx]
        def pieces(some_bits):
            for p in some_bits:
                off = (n8 & ~(2 * p - 1)) * SUBLANES

                @pl.when((n8 & p) != 0)
                def _(p=p, off=off):
                    fn(pl.multiple_of(src + off, SUBLANES), pl.multiple_of(dst + off, SUBLANES), p * SUBLANES,
                       bits.index(p) % 2)

        long_bits = tuple(p for p in bits if p >= RUN_LONG)
        if long_bits:
            pl.when(n8 >= RUN_LONG)(lambda: pieces(long_bits))
        pieces(tuple(p for p in bits if p < RUN_LONG))
        return 0

    lax.fori_loop(0, N_EXPERTS, per_expert, 0)


def _issue_runs_inline(n8_ref, src_ref, dst_ref, tile, enable, fn, experts=range(N_EXPERTS)):
    for e in experts:
        idx = tile * N_EXPERTS + e
        n8 = jnp.where(enable, n8_ref[idx], 0)
        src = src_ref[idx]
        dst = dst_ref[idx]
        for b, p in enumerate(RUN_BITS):
            off = (n8 & ~(2 * p - 1)) * SUBLANES

            @pl.when((n8 & p) != 0)
            def _(b=b, p=p, off=off, src=src, dst=dst):
                fn(pl.multiple_of(src + off, SUBLANES), pl.multiple_of(dst + off, SUBLANES), p * SUBLANES, b % 2)


def _wait_rows(n8, make_copy, max_rows):
    for p in tuple(1 << b for b in reversed(range((max_rows // SUBLANES).bit_length()))):
        @pl.when((n8 & p) != 0)
        def _(p=p):
            make_copy(p * SUBLANES).wait()


def _dispatch_kernel(n8_ref, src_ref, dst_ref, tot_ref, zn8_ref, zdst_ref, u_ref, loc_ref, gate_ref, xs_hbm,
                     lbuf0, lbuf1, lbuf2, zx, sems, *, tt, nslot, dh):
    i = pl.program_id(0)
    last = pl.num_programs(0) - 1
    bufs = (lbuf0, lbuf1, lbuf2)
    nbuf = len(bufs)
    zsem = nbuf

    def copy_from(slot):
        def piece(s0, d0, rows, prio):
            pltpu.make_async_copy(
                bufs[slot].at[pl.ds(s0, rows)], xs_hbm.at[pl.ds(d0, rows)], sems.at[slot]).start(priority=prio)
        return piece

    def wait_tile(tile, slot):
        _wait_rows(tot_ref[tile], lambda rows: pltpu.make_async_copy(
            bufs[slot].at[pl.ds(0, rows)], xs_hbm.at[pl.ds(0, rows)], sems.at[slot]), nslot)

    @pl.when(i == 0)
    def _():
        zx[...] = jnp.zeros(zx.shape, I32)

        def zero_piece(s0, d0, rows, prio):
            cx = pltpu.make_async_copy(zx.at[pl.ds(0, rows)], xs_hbm.at[pl.ds(d0, rows)], sems.at[zsem])
            cx.start()
            cx.wait()

        zbits = tuple(1 << b for b in reversed(range((EXPERT_ROWS // SUBLANES - 1).bit_length())))
        _for_each_run_piece(zn8_ref, zdst_ref, zdst_ref, 0, zbits, zero_piece)

    def build(buf):
        ub = _bf(u_ref[...])
        ones = jnp.ones((tt, LANES), BF16)
        loc = loc_ref[...]
        gate = gate_ref[...]
        rows_b = lax.broadcasted_iota(I32, (SLOT_CHUNK, tt), 0).astype(F32).astype(BF16)
        loc_hi = lax.shift_right_logical(loc, SLOT_CHUNK.bit_length() - 1)
        loc_lo = (loc & (SLOT_CHUNK - 1)).astype(F32)
        gate_h = _bf(gate)
        gate_l = _bf(gate - gate_h.astype(F32))
        one_b = jnp.ones((SLOT_CHUNK, tt), BF16)
        for c in range(nslot // SLOT_CHUNK):
            perm = jnp.zeros((SLOT_CHUNK, tt), BF16)
            pgh = jnp.zeros((SLOT_CHUNK, tt), BF16)
            pgl = jnp.zeros((SLOT_CHUNK, tt), BF16)
            for k in range(TOP_K):
                lk = _bf(jnp.where(loc_hi[k:k + 1, :] == c, loc_lo[k:k + 1, :], -1.0))
                eq = rows_b == lk
                perm = jnp.where(eq, one_b, perm)
                pgh = jnp.where(eq, jnp.broadcast_to(gate_h[k:k + 1, :], (SLOT_CHUNK, tt)), pgh)
                pgl = jnp.where(eq, jnp.broadcast_to(gate_l[k:k + 1, :], (SLOT_CHUNK, tt)), pgl)
            cs = slice(c * SLOT_CHUNK, (c + 1) * SLOT_CHUNK)
            xp = lax.bitcast_convert_type(_dot(perm, ub), I32)
            buf[cs, 0:dh] = xp[:, 0:dh] | lax.shift_right_logical(xp[:, dh:2 * dh], 16)
            buf[cs, dh:dh + LANES] = lax.bitcast_convert_type(_dot(pgh, ones) + _dot(pgl, ones), I32)

    def step(cur):
        prv, prv2 = (cur - 1) % nbuf, (cur - 2) % nbuf

        @pl.when(i >= nbuf)
        def _():
            wait_tile(i - nbuf, cur)

        _issue_runs_inline(n8_ref, src_ref, dst_ref, jnp.maximum(i - 1, 0), i >= 1, copy_from(prv))
        build(bufs[cur])

        @pl.when(i == last)
        def _():
            _for_each_run_piece(n8_ref, src_ref, dst_ref, i, RUN_BITS, copy_from(cur))

            @pl.when(i >= 2)
            def _():
                wait_tile(i - 2, prv2)

            @pl.when(i >= 1)
            def _():
                wait_tile(i - 1, prv)
            wait_tile(i, cur)

    for cur in range(nbuf):
        pl.when(i % nbuf == cur)(functools.partial(step, cur))


def _dispatch(tabs, ztabs, u2, loc_t, gate_t, n_rows):
    n, d = u2.shape
    tt = min(MOE_TILE, n)
    nslot = TOP_K * tt + N_EXPERTS * SUBLANES
    dh = d // 2
    assert nslot % SLOT_CHUNK == 0 and tt // SUBLANES == RUN_BITS[0]
    col = pl.BlockSpec((TOP_K, tt), lambda i, *_: (0, i))
    return pl.pallas_call(
        functools.partial(_dispatch_kernel, tt=tt, nslot=nslot, dh=dh),
        out_shape=jax.ShapeDtypeStruct((n_rows, dh + LANES), I32),
        grid_spec=pltpu.PrefetchScalarGridSpec(
            num_scalar_prefetch=6, grid=(n // tt,),
            in_specs=[pl.BlockSpec((tt, d), lambda i, *_: (i, 0)), col, col],
            out_specs=pl.BlockSpec(memory_space=pl.ANY),
            scratch_shapes=[pltpu.VMEM((nslot, dh + LANES), I32)] * 3 + [
                pltpu.VMEM((EXPERT_ROWS // 2, dh + LANES), I32), pltpu.SemaphoreType.DMA((4,))]),
        compiler_params=_cparams(("arbitrary",)),
        name="dispatch",
    )(*tabs, *ztabs, u2, loc_t, gate_t)


def _experts_kernel(be_ref, nu_ref, xs_ref, wg_ref, wu_ref, wd_ref, ys_ref, *, d):
    del be_ref
    dh = d // 2

    @pl.when(pl.program_id(0) < nu_ref[0])
    def _():
        w = xs_ref[:, 0:dh]
        xa = _bf(lax.bitcast_convert_type(w & jnp.int32(-65536), F32))
        xb = _bf(lax.bitcast_convert_type(lax.shift_left(w, 16), F32))
        gate = lax.bitcast_convert_type(xs_ref[:, dh:dh + LANES], F32)
        hg = _dot(xa, _bf(wg_ref[0, 0:dh, :])) + _dot(xb, _bf(wg_ref[0, dh:d, :]))
        hu = _dot(xa, _bf(wu_ref[0, 0:dh, :])) + _dot(xb, _bf(wu_ref[0, dh:d, :]))
        y = _dot(_bf(jax.nn.silu(hg) * hu), _bf(wd_ref[0])) * jnp.tile(gate, (1, d // LANES))
        yb = lax.bitcast_convert_type(_bf(y).astype(F32), I32)
        ys_ref[...] = yb[:, 0:dh] | lax.shift_right_logical(yb[:, dh:d], 16)


def _experts(blk_expert, n_used, xs, wg, wu, wd):
    rows, xw = xs.shape
    d = wg.shape[1]
    de = wg.shape[2]
    nblk = rows // EXPERT_ROWS
    blk = lambda i, be, nu: jnp.minimum(i, nu[0] - 1)
    return pl.pallas_call(
        functools.partial(_experts_kernel, d=d),
        out_shape=jax.ShapeDtypeStruct((rows, d // 2), I32),
        grid_spec=pltpu.PrefetchScalarGridSpec(
            num_scalar_prefetch=2, grid=(nblk,),
            in_specs=[pl.BlockSpec((EXPERT_ROWS, xw), lambda i, be, nu: (blk(i, be, nu), 0)),
                      pl.BlockSpec((1, d, de), lambda i, be, nu: (be[blk(i, be, nu)], 0, 0)),
                      pl.BlockSpec((1, d, de), lambda i, be, nu: (be[blk(i, be, nu)], 0, 0)),
                      pl.BlockSpec((1, de, d), lambda i, be, nu: (be[blk(i, be, nu)], 0, 0))],
            out_specs=pl.BlockSpec((EXPERT_ROWS, d // 2), lambda i, be, nu: (blk(i, be, nu), 0))),
        compiler_params=_cparams(("arbitrary",)),
        name="experts",
    )(blk_expert, n_used, xs, wg, wu, wd)


def _combine_kernel(n8_ref, src_ref, dst_ref, tot_ref, ys_hbm, loc_ref, h_ref, u2_ref, ada_ref, wsg_ref, wsu_ref, wsd_ref,
                    gf_ref, o_ref, ybuf0, ybuf1, pick_s, sems, *, tt, nslot, d):
    i = pl.program_id(0)
    last = pl.num_programs(0) - 1
    bufs = (ybuf0, ybuf1)
    dh = d // 2
    nchunk = nslot // SLOT_CHUNK

    def copy_into(slot):
        def piece(s0, d0, rows, prio):
            pltpu.make_async_copy(
                ys_hbm.at[pl.ds(d0, rows)], bufs[slot].at[pl.ds(s0, rows)], sems.at[slot]).start(priority=prio)
        return piece

    @pl.when(i == 0)
    def _():
        ybuf0[...] = jnp.zeros(ybuf0.shape, I32)
        ybuf1[...] = jnp.zeros(ybuf1.shape, I32)
        _for_each_run_piece(n8_ref, src_ref, dst_ref, 0, RUN_BITS, copy_into(0))

    def step(cur, nxt):
        nxt_tile = jnp.minimum(i + 1, last)
        _issue_runs_inline(n8_ref, src_ref, dst_ref, nxt_tile, i < last, copy_into(nxt), range(N_EXPERTS // 2))
        x = _bf(u2_ref[...])
        shared = _dot(_bf(jax.nn.silu(_dot(x, wsg_ref[...])) * _dot(x, wsu_ref[...])), wsd_ref[...])
        loc = loc_ref[...]
        cols_b = lax.broadcasted_iota(I32, (tt, SLOT_CHUNK), 1).astype(F32).astype(BF16)
        loc_hi = lax.shift_right_logical(loc, SLOT_CHUNK.bit_length() - 1)
        loc_lo = (loc & (SLOT_CHUNK - 1)).astype(F32)
        one_b = jnp.ones((tt, SLOT_CHUNK), BF16)
        for c in range(nchunk):
            pick = jnp.zeros((tt, SLOT_CHUNK), BF16)
            for k in range(TOP_K):
                lk = _bf(jnp.where(loc_hi[:, k:k + 1] == c, loc_lo[:, k:k + 1], -1.0))
                pick = jnp.where(cols_b == lk, one_b, pick)
            pick_s[c] = pick
        _wait_rows(tot_ref[i], lambda rows: pltpu.make_async_copy(
            ys_hbm.at[pl.ds(0, rows)], bufs[cur].at[pl.ds(0, rows)], sems.at[cur]), nslot)
        _issue_runs_inline(n8_ref, src_ref, dst_ref, nxt_tile, i < last, copy_into(nxt),
                           range(N_EXPERTS // 2, N_EXPERTS))
        routed_a = jnp.zeros((tt, dh), F32)
        routed_b = jnp.zeros((tt, dh), F32)
        for c in range(nchunk):
            w = bufs[cur][c * SLOT_CHUNK:(c + 1) * SLOT_CHUNK, :]
            routed_a = routed_a + _dot(pick_s[c], _bf(lax.bitcast_convert_type(w & jnp.int32(-65536), F32)))
            routed_b = routed_b + _dot(pick_s[c], _bf(lax.bitcast_convert_type(lax.shift_left(w, 16), F32)))
        routed = jnp.concatenate([routed_a, routed_b], axis=1)
        gate2 = ada_ref[0, :, 5 * d:6 * d]
        h = h_ref[...] + gate2 * (routed + shared)
        o_ref[...] = _rms(h, gf_ref[...])

    pl.when(i % 2 == 0)(lambda: step(0, 1))
    pl.when(i % 2 == 1)(lambda: step(1, 0))


def _combine(tabs, ys, loc, h1, u2, ada3, wsg, wsu, wsd, gf, seq):
    n, d = h1.shape
    tt = min(MOE_TILE, n)
    nslot = TOP_K * tt + N_EXPERTS * SUBLANES
    per_b = seq // tt
    row = pl.BlockSpec((tt, d), lambda i, *_: (i, 0))
    full = lambda a: pl.BlockSpec(a.shape, lambda i, *_: (0,) * a.ndim)
    wsg, wsu, wsd = wsg.astype(BF16), wsu.astype(BF16), wsd.astype(BF16)
    gf = gf.reshape(1, d)
    return pl.pallas_call(
        functools.partial(_combine_kernel, tt=tt, nslot=nslot, d=d),
        out_shape=jax.ShapeDtypeStruct((n, d), F32),
        grid_spec=pltpu.PrefetchScalarGridSpec(
            num_scalar_prefetch=4, grid=(n // tt,),
            in_specs=[pl.BlockSpec(memory_space=pl.ANY),
                      pl.BlockSpec((tt, TOP_K), lambda i, *_: (i, 0)),
                      row, row,
                      pl.BlockSpec((1, 1, ada3.shape[2]), lambda i, *_: (i // per_b, 0, 0)),
                      full(wsg), full(wsu), full(wsd), full(gf)],
            out_specs=row,
            scratch_shapes=[pltpu.VMEM((nslot, d // 2), I32), pltpu.VMEM((nslot, d // 2), I32),
                            pltpu.VMEM((nslot // SLOT_CHUNK, tt, SLOT_CHUNK), BF16),
                            pltpu.SemaphoreType.DMA((2,))]),
        compiler_params=_cparams(("arbitrary",)),
        name="combine",
    )(*tabs, ys, loc, h1, u2, ada3, wsg, wsu, wsd, gf)


def _moe(h1, u2, ada3, w_router, router_bias, wg, wu, wd, wsg, wsu, wsd, gf):
    bsz, seq, d = h1.shape
    n = bsz * seq
    assert seq % min(MOE_TILE, n) == 0
    h1f = h1.reshape(n, d)
    u2f = u2.reshape(n, d)
    gate_t, loc_t, c8, loff, run0, tot = _route(u2f, w_router, router_bias)
    nt = c8.shape[0]
    as_tab = lambda a: a[:, :, 0].astype(I32)
    tot8 = tot[:, 0].astype(I32)
    padded = (tot8 + EXPERT_ROWS - 1) // EXPERT_ROWS * EXPERT_ROWS
    pend = jnp.cumsum(padded)
    pstart = (pend - padded).astype(I32)
    nblk = (n * TOP_K + nt * N_EXPERTS * (SUBLANES - 1) + N_EXPERTS * (EXPERT_ROWS - 1) + EXPERT_ROWS - 1) // EXPERT_ROWS
    blk_row0 = jnp.arange(nblk, dtype=I32) * EXPERT_ROWS
    blk_expert = jnp.minimum(jnp.sum(pend[None, :] <= blk_row0[:, None], axis=1), N_EXPERTS - 1).astype(I32)
    n_used = (pend[-1:] // EXPERT_ROWS).astype(I32)
    n8 = as_tab(c8) // SUBLANES
    tabs = (n8.reshape(-1), as_tab(loff).reshape(-1), (pstart[None, :] + as_tab(run0)).reshape(-1),
            jnp.sum(n8, axis=1))
    ztabs = ((padded - tot8) // SUBLANES, pstart + tot8)
    xs = _dispatch(tabs, ztabs, u2f, loc_t, gate_t, nblk * EXPERT_ROWS)
    ys = _experts(blk_expert, n_used, xs, wg, wu, wd)
    out = _combine(tabs, ys, loc_t.T, h1f, u2f, ada3, wsg, wsu, wsd, gf, seq)
    return out.reshape(bsz, seq, d)


def kernel(x, c, w_ada, b_ada, norm1_g, w_in, ssm_lambda_re, ssm_lambda_im, ssm_log_dt, ssm_b_re, ssm_b_im,
           ssm_c_re, ssm_c_im, ssm_d, ssm_w_glu, ssm_b_glu, w_proj_ssm, w_proj_attn, w_out, norm2_g, w_router,
           router_bias, w_exp_gate, w_exp_up, w_exp_down, w_sh_gate, w_sh_up, w_sh_down, norm_f_g):
    depth = w_ada.shape[0]
    assert depth == 1, "the final norm is fused into the last (only) layer's combine kernel"
    bsz, seq, d = x.shape
    layer = 0
    ada3 = _ada(c, w_ada[layer], b_ada[layer]).reshape(bsz, 1, 6 * d)
    us, k, ki, qt, qit, vt, wit = _inproj(x, ada3, norm1_g[layer], w_in[layer])
    a_re, a_im, bb_re, bb_im = _s5disc(ssm_lambda_re[layer], ssm_lambda_im[layer], ssm_log_dt[layer],
                                       ssm_b_re[layer], ssm_b_im[layer])
    ys_t = _s5(us, a_re, a_im, bb_re, bb_im, ssm_c_re[layer], ssm_c_im[layer],
               ssm_d[layer], ssm_w_glu[layer], ssm_b_glu[layer])
    ya = _dsa(qt, qit, wit, k, ki, vt)
    h1, u2 = _mix(x, ys_t, ya, ada3, norm1_g[layer], w_in[layer], w_proj_ssm[layer],
                  w_proj_attn[layer], w_out[layer], norm2_g[layer])
    return _moe(h1, u2, ada3, w_router[layer], router_bias[layer], w_exp_gate[layer], w_exp_up[layer],
                w_exp_down[layer], w_sh_gate[layer], w_sh_up[layer], w_sh_down[layer], norm_f_g)
```

```python
import functools
import math

import jax
import jax.numpy as jnp
import numpy as np
from jax import lax
from jax.experimental import pallas as pl
from jax.experimental.pallas import tpu as pltpu

F32 = jnp.float32
BF16 = jnp.bfloat16
I32 = jnp.int32

SSM_GROUP = 16
SSM_STATE = 64
N_HEADS = 8
HEAD_DIM = 64
IDX_HEADS = 8
IDX_DIM = 64
TOPK_MAX = 256
N_EXPERTS = 64
TOP_K = 8
N_GROUPS = 8
TOPK_GROUPS = 4
ROUTED_SCALE = 2.5
EPS = 1e-6

V7X_VMEM_LIMIT_BYTES = 56 * 1024 * 1024
LANES = 128
SUBLANES = 8

INPROJ_ROWS = 512
S5_STEPS = 64
S5_LANE_CHUNK = 128
DSA_Q_COLS = 256
DSA_K_ROWS = 512
DSA_COUNT_ROWS = 64
BITSLICE_ROWS = 256
POS_SPLIT = 64
MIX_ROWS = 512
MOE_TILE = 256
ROUTE_TILES = 4
SLOT_CHUNK = 256
EXPERT_ROWS = 1024

NEG_BIG = -1e30
INT_MIN = -(2 ** 31)


def _cparams(sem):
    return pltpu.CompilerParams(dimension_semantics=sem, vmem_limit_bytes=V7X_VMEM_LIMIT_BYTES)


def _bf(x):
    return x.astype(BF16)


def _dot(a, b):
    return jnp.dot(a, b, preferred_element_type=F32)


def _dot_nt(a, b):
    return lax.dot_general(a, b, (((1,), (1,)), ((), ())), preferred_element_type=F32)


def _split(x):
    hi = _bf(x)
    lo = _bf(x - hi.astype(F32))
    return hi, lo


def _dot3(a, b):
    ah, al = _split(a)
    bh, bl = _split(b)
    return _dot(ah, bh) + (_dot(ah, bl) + _dot(al, bh))


def _rms(x, g):
    return x * lax.rsqrt(jnp.mean(x * x, axis=-1, keepdims=True) + EPS) * g


def _ada_kernel(c_ref, w_ref, b_ref, o_ref):
    c = c_ref[...]
    o_ref[...] = _dot3(c * jax.nn.sigmoid(c), w_ref[...]) + b_ref[...]


def _ada(c, w, b):
    bsz, d = c.shape
    n = w.shape[1]
    tn = 1024
    return pl.pallas_call(
        _ada_kernel,
        out_shape=jax.ShapeDtypeStruct((bsz, n), F32),
        grid=(n // tn,),
        in_specs=[pl.BlockSpec((bsz, d), lambda j: (0, 0)),
                  pl.BlockSpec((d, tn), lambda j: (0, j)),
                  pl.BlockSpec((1, tn), lambda j: (0, j))],
        out_specs=pl.BlockSpec((bsz, tn), lambda j: (0, j)),
        compiler_params=_cparams(("arbitrary",)),
        name="ada",
    )(c, w, b.reshape(1, n))


ALIBI_SLOPES = tuple(2.0 ** (-8.0 * (h + 1) / N_HEADS) for h in range(N_HEADS))
QAUG_ROWS = 16


def _inproj_kernel(x_ref, ada_ref, g1_ref, w_ref, wt_ref,
                   us_ref, k_ref, ki_ref, qt_ref, qit_ref, vt_ref, wit_ref, *, d, ssm_w, attn_w, idx_w, tl):
    x = x_ref[0]
    shift = ada_ref[0, :, 0:d]
    scale = ada_ref[0, :, d:2 * d]
    u = _bf(_rms(x, g1_ref[...]) * (1.0 + scale) + shift)
    r = _dot(u, w_ref[...])
    us_ref[0] = r[:, 0:ssm_w]
    k_ref[0] = _bf(r[:, ssm_w:ssm_w + HEAD_DIM])
    ki_ref[0] = _bf(r[:, ssm_w + LANES:ssm_w + LANES + IDX_DIM])
    rt = _dot_nt(wt_ref[...], u)
    arow = lax.broadcasted_iota(I32, (QAUG_ROWS, tl), 0)
    for h in range(N_HEADS):
        base = h * LANES
        qt_ref[0, base:base + HEAD_DIM, :] = _bf(rt[h * HEAD_DIM:(h + 1) * HEAD_DIM])
        qt_ref[0, base + HEAD_DIM:base + HEAD_DIM + QAUG_ROWS, :] = _bf(jnp.where(arow < 2, ALIBI_SLOPES[h], 0.0))
        qt_ref[0, base + HEAD_DIM + QAUG_ROWS:base + LANES, :] = jnp.zeros((LANES - HEAD_DIM - QAUG_ROWS, tl), BF16)
    qit_ref[0] = _bf(rt[attn_w:attn_w + idx_w])
    vt_ref[0] = _bf(rt[attn_w + idx_w:attn_w + idx_w + HEAD_DIM])
    wit_ref[0] = rt[attn_w + idx_w + HEAD_DIM:attn_w + idx_w + HEAD_DIM + IDX_HEADS]


def _split_w_in(w_in, d):
    ssm_w = 512
    sizes = (ssm_w, N_HEADS * HEAD_DIM, HEAD_DIM, HEAD_DIM, IDX_HEADS * IDX_DIM, IDX_DIM, IDX_HEADS, d, d)
    offs = [0]
    for s in sizes:
        offs.append(offs[-1] + s)
    return [w_in[:, offs[i]:offs[i + 1]] for i in range(9)]


def _inproj(x, ada3, g1, w_in):
    bsz, seq, d = x.shape
    ssm_w = 512
    attn_w = N_HEADS * HEAD_DIM
    idx_w = IDX_HEADS * IDX_DIM
    w_ssm, w_q, w_k, w_v, w_qi, w_ki, w_wi, _, _ = _split_w_in(w_in, d)
    zpad = lambda n: jnp.zeros((d, n), F32)
    wbig = jnp.concatenate([w_ssm, w_k, zpad(LANES - HEAD_DIM), w_ki, zpad(LANES - IDX_DIM)], axis=1).astype(BF16)
    wt = jnp.concatenate([w_q * (HEAD_DIM ** -0.5), w_qi * (IDX_DIM ** -0.5), w_v, w_wi,
                          zpad(LANES - HEAD_DIM - IDX_HEADS)], axis=1).T.astype(BF16)
    tl = INPROJ_ROWS
    kern = functools.partial(_inproj_kernel, d=d, ssm_w=ssm_w, attn_w=attn_w, idx_w=idx_w, tl=tl)
    row = lambda w: pl.BlockSpec((1, tl, w), lambda b, l: (b, l, 0))
    colt = lambda h: pl.BlockSpec((1, h, tl), lambda b, l: (b, 0, l))
    full = lambda a: pl.BlockSpec(a.shape, lambda b, l: (0,) * a.ndim)
    return pl.pallas_call(
        kern,
        out_shape=(jax.ShapeDtypeStruct((bsz, seq, ssm_w), F32),
                   jax.ShapeDtypeStruct((bsz, seq, HEAD_DIM), BF16),
                   jax.ShapeDtypeStruct((bsz, seq, IDX_DIM), BF16),
                   jax.ShapeDtypeStruct((bsz, N_HEADS * LANES, seq), BF16),
                   jax.ShapeDtypeStruct((bsz, idx_w, seq), BF16),
                   jax.ShapeDtypeStruct((bsz, HEAD_DIM, seq), BF16),
                   jax.ShapeDtypeStruct((bsz, IDX_HEADS, seq), F32)),
        grid=(bsz, seq // tl),
        in_specs=[row(d),
                  pl.BlockSpec((1, 1, ada3.shape[2]), lambda b, l: (b, 0, 0)),
                  pl.BlockSpec((1, d), lambda b, l: (0, 0)),
                  full(wbig), full(wt)],
        out_specs=(row(ssm_w), row(HEAD_DIM), row(IDX_DIM),
                   colt(N_HEADS * LANES), colt(idx_w), colt(HEAD_DIM), colt(IDX_HEADS)),
        compiler_params=_cparams(("arbitrary", "arbitrary")),
        name="inproj",
    )(x, ada3, g1.reshape(1, d), wbig, wt)


def _s5disc_kernel(lr_ref, li_ref, ldt_ref, br_ref, bi_ref, are_ref, aim_ref, bbr_ref, bbi_ref):
    lr = lr_ref[...]
    li = li_ref[...]
    dt = jnp.exp(ldt_ref[...])
    mag = jnp.exp(lr * dt)
    a_re = mag * jnp.cos(li * dt)
    a_im = mag * jnp.sin(li * dt)
    den = lr * lr + li * li
    n_re = a_re - 1.0
    f_re = (n_re * lr + a_im * li) / den
    f_im = (a_im * lr - n_re * li) / den
    br = br_ref[...]
    bi = bi_ref[...]
    are_ref[...] = a_re
    aim_ref[...] = a_im
    bbr_ref[...] = f_re * br - f_im * bi
    bbi_ref[...] = f_re * bi + f_im * br


def _s5disc(lam_re, lam_im, log_dt, b_re, b_im):
    g, p = lam_re.shape
    h = b_re.shape[2]
    rep = lambda a: jnp.repeat(a, h, axis=1)
    ldt = jnp.broadcast_to(log_dt[:, None], (g, p * h))
    sds = jax.ShapeDtypeStruct((g, p * h), F32)
    a_re, a_im, bb_re, bb_im = pl.pallas_call(
        _s5disc_kernel, out_shape=(sds, sds, sds, sds), name="s5disc",
    )(rep(lam_re), rep(lam_im), ldt, b_re.reshape(g, p * h), b_im.reshape(g, p * h))
    return a_re[:, ::h], a_im[:, ::h], bb_re.reshape(g, p, h), bb_im.reshape(g, p, h)


def _s5_kernel(u_ref, wb_ref, ar_ref, ai_ref, cc_ref, dsk_ref, wg_ref, bg_ref, o_ref, buf, hst, *, tl, width):
    nch = width // S5_LANE_CHUNK
    sw = S5_LANE_CHUNK // SSM_GROUP * SSM_STATE
    rows = tl * SUBLANES

    @pl.when(pl.program_id(0) == 0)
    def _():
        hst[...] = jnp.zeros_like(hst)

    u = jnp.swapaxes(u_ref[...], 0, 1).reshape(rows, width)
    ub = _bf(u)
    for j in range(nch):
        buf[:, j * 2 * sw:(j + 1) * 2 * sw] = _dot(ub[:, j * S5_LANE_CHUNK:(j + 1) * S5_LANE_CHUNK], wb_ref[j])

    for j in range(nch):
        re_cols = slice(j * 2 * sw, j * 2 * sw + sw)
        im_cols = slice(j * 2 * sw + sw, (j + 1) * 2 * sw)
        a_re = jnp.broadcast_to(ar_ref[:, j * sw:(j + 1) * sw], (SUBLANES, sw))
        a_im = jnp.broadcast_to(ai_ref[:, j * sw:(j + 1) * sw], (SUBLANES, sw))

        def step(t, carry, re_cols=re_cols, im_cols=im_cols, a_re=a_re, a_im=a_im):
            h_re, h_im = carry
            r0 = pl.multiple_of(t * SUBLANES, SUBLANES)
            n_re = (a_re * h_re - a_im * h_im) + buf[pl.ds(r0, SUBLANES), re_cols]
            n_im = (a_re * h_im + a_im * h_re) + buf[pl.ds(r0, SUBLANES), im_cols]
            buf[pl.ds(r0, SUBLANES), re_cols] = n_re
            buf[pl.ds(r0, SUBLANES), im_cols] = n_im
            return n_re, n_im

        h_re, h_im = lax.fori_loop(0, tl, step, (hst[:, re_cols], hst[:, im_cols]), unroll=True)
        hst[:, re_cols] = h_re
        hst[:, im_cols] = h_im

    ys = [_dot(_bf(buf[:, j * 2 * sw:(j + 1) * 2 * sw]), cc_ref[j]) for j in range(nch)]
    y = jnp.concatenate(ys, axis=1) + dsk_ref[...] * u
    y = jax.nn.gelu(y)
    y = y * jax.nn.sigmoid(_dot(_bf(y), wg_ref[...]) + bg_ref[...])
    o_ref[...] = _bf(jnp.swapaxes(y.reshape(tl, SUBLANES, width), 0, 1))


def _s5(u_t, a_re, a_im, bb_re, bb_im, c_re, c_im, d_skip, w_glu, b_glu):
    bsz, seq, width = u_t.shape
    assert bsz == SUBLANES
    nch = width // S5_LANE_CHUNK
    gpc = S5_LANE_CHUNK // SSM_GROUP
    sw = gpc * SSM_STATE
    eye = jnp.eye(gpc, dtype=F32)

    def bmat(bb):
        t = bb.reshape(nch, gpc, SSM_STATE, SSM_GROUP).transpose(0, 1, 3, 2)
        return jnp.einsum('jghp,gk->jghkp', t, eye).reshape(nch, S5_LANE_CHUNK, sw)

    def cmat(cc):
        t = cc.reshape(nch, gpc, SSM_GROUP, SSM_STATE).transpose(0, 1, 3, 2)
        return jnp.einsum('jgph,gk->jgpkh', t, eye).reshape(nch, sw, S5_LANE_CHUNK)

    wb = jnp.concatenate([bmat(bb_re), bmat(bb_im)], axis=2)
    cc = jnp.concatenate([cmat(c_re), -cmat(c_im)], axis=1)
    tl = S5_STEPS
    full = lambda a: pl.BlockSpec(a.shape, lambda i: (0,) * a.ndim)
    args = (u_t, wb.astype(BF16), a_re.reshape(1, -1), a_im.reshape(1, -1), cc.astype(BF16),
            d_skip.reshape(1, width), w_glu.astype(BF16), b_glu.reshape(1, width))
    return pl.pallas_call(
        functools.partial(_s5_kernel, tl=tl, width=width),
        out_shape=jax.ShapeDtypeStruct((bsz, seq, width), BF16),
        grid=(seq // tl,),
        in_specs=[pl.BlockSpec((bsz, tl, width), lambda i: (0, i, 0))] + [full(a) for a in args[1:]],
        out_specs=pl.BlockSpec((bsz, tl, width), lambda i: (0, i, 0)),
        scratch_shapes=[pltpu.VMEM((tl * SUBLANES, nch * 2 * sw), F32),
                        pltpu.VMEM((SUBLANES, nch * 2 * sw), F32)],
        compiler_params=_cparams(("arbitrary",)),
        name="s5",
    )(*args)


def _bit_transpose32(words):
    x = list(words)
    j, m = 16, 0x0000FFFF
    while j:
        k = 0
        while k < 32:
            t = (x[k] ^ lax.shift_right_logical(x[k + j], jnp.int32(j))) & jnp.int32(m - (1 << 32) if m >= 1 << 31 else m)
            x[k] = x[k] ^ t
            x[k + j] = x[k + j] ^ lax.shift_left(t, jnp.int32(j))
            k = (k + j + 1) & ~j
        j >>= 1
        m = (m ^ (m << j)) & 0xFFFFFFFF
    return x


def _dsa_kernel(qt_ref, qit_ref, wit_ref, ka_ref, ki_ref, vt_ref, o_ref, key_s, mb_s, acc_s, pl_s, p_s, *, tq, tk, topk, seq):
    i = pl.program_id(1)
    q0 = i * tq
    nkt = (q0 + tq + tk - 1) // tk
    ch = DSA_COUNT_ROWS
    krow = lax.broadcasted_iota(I32, (tk, tq), 0)
    qcol = q0 + lax.broadcasted_iota(I32, (tk, tq), 1)
    crow = lax.broadcasted_iota(I32, (ch, tq), 0)

    wb = wit_ref[0] * (IDX_HEADS ** -0.5)

    def score_tile(j, _):
        r0 = pl.multiple_of(j * tk, tk)
        kit = ki_ref[0, pl.ds(r0, tk), :]
        acc = jnp.zeros((tk, tq), F32)
        for h in range(IDX_HEADS):
            s = _dot(kit, qit_ref[0, h * IDX_DIM:(h + 1) * IDX_DIM, :])
            acc = acc + wb[h:h + 1, :] * jnp.maximum(s, 0.0)
        bits = lax.bitcast_convert_type(acc, I32)
        key = jnp.where(bits < 0, bits ^ jnp.int32(0x7FFFFFFF), bits)
        key = jnp.where(acc == 0.0, 0, key)
        key = jnp.where(krow + r0 <= qcol, key, INT_MIN)
        key_s[pl.ds(r0, tk), :] = key
        ukey = key ^ INT_MIN
        for c in range(tk // BITSLICE_ROWS):
            words = [ukey[c * BITSLICE_ROWS + v * SUBLANES:c * BITSLICE_ROWS + (v + 1) * SUBLANES, :]
                     for v in range(32)]
            planes = _bit_transpose32(words)
            g0 = pl.multiple_of((j * (tk // BITSLICE_ROWS) + c) * SUBLANES, SUBLANES)
            for it in range(32):
                pl_s[it, pl.ds(g0, SUBLANES), :] = planes[it]
        return 0

    @pl.when((pl.program_id(0) == 0) & (i == 0))
    def _():
        pl_s[...] = jnp.zeros(pl_s.shape, I32)

    def score_pair(jj, _):
        score_tile(2 * jj, 0)
        score_tile(2 * jj + 1, 0)
        return 0

    lax.fori_loop(0, nkt // 2, score_pair, 0)

    @pl.when(nkt % 2 == 1)
    def _():
        score_tile(nkt - 1, 0)

    def count(pred):
        def tile(j, cnt):
            for c in range(tk // ch):
                rr = pl.multiple_of(j * tk + c * ch, ch)
                cnt = cnt + jnp.where(pred(key_s[pl.ds(rr, ch), :], rr), 1, 0)
            return cnt
        cnt = lax.fori_loop(0, nkt, tile, jnp.zeros((ch, tq), I32))
        return jnp.sum(cnt.astype(F32), axis=0, keepdims=True)

    ngrp = seq // 32

    def lane_count(words):
        pc = lax.population_count(words).reshape(ngrp // SUBLANES, SUBLANES, tq)
        return jnp.sum(jnp.sum(pc, axis=0).astype(F32), axis=0, keepdims=True)

    def bit_step(it, carry):
        alive, above, ans_u = carry
        ones = alive & pl_s[it]
        cnt1 = lane_count(ones)
        take = above + cnt1 >= float(topk)
        alive = jnp.where(take, ones, alive ^ ones)
        above = jnp.where(take, above, above + cnt1)
        ans_u = jnp.where(take, ans_u | lax.shift_left(jnp.int32(1), 31 - it), ans_u)
        return alive, above, ans_u

    grow = lax.broadcasted_iota(I32, (ngrp, tq), 0)
    alive0 = jnp.where(grow < nkt * (tk // 32), -1, 0)
    alive, above, ans_u = lax.fori_loop(
        0, 32, bit_step, (alive0, jnp.zeros((1, tq), F32), jnp.zeros((1, tq), I32)))
    thr = jnp.maximum(ans_u ^ INT_MIN, INT_MIN + 1)
    cnt_ge = above + lane_count(alive)
    tied = jnp.where(ans_u != 0, cnt_ge, 0.0) > float(topk)
    has_ties = jnp.max(jnp.where(tied, 1.0, 0.0)) > 0.0

    def tie_cut():
        need = float(topk) - count(lambda kb, rr: kb > thr)
        nbits = max(1, (seq - 1).bit_length())

        def idx_step(b, x):
            cand = x | lax.shift_left(jnp.int32(1), nbits - 1 - b)
            below = count(lambda kb, rr: jnp.where(kb == thr, crow + rr, seq) < cand)
            return jnp.where(below < need, cand, x)

        x = lax.fori_loop(0, nbits, idx_step, jnp.zeros((1, tq), I32))
        return jnp.where(tied, x, seq)

    cut = lax.cond(has_ties, tie_cut, lambda: jnp.full((1, tq), seq, I32))

    def bias_tile(j, _):
        for c in range(tk // ch):
            rr = pl.multiple_of(j * tk + c * ch, ch)
            kb = key_s[pl.ds(rr, ch), :]
            tie_bias = jnp.where(crow + rr <= cut, 0.0, NEG_BIG)
            mb_s[pl.ds(rr, ch), :] = jnp.where(kb > thr, 0.0, jnp.where(kb == thr, tie_bias, NEG_BIG))
        return 0

    def logits(j, h):
        r0 = pl.multiple_of(j * tk, tk)
        s = _dot(ka_ref[0, pl.ds(r0, tk), :], qt_ref[0, h * LANES:(h + 1) * LANES, :]) + mb_s[pl.ds(r0, tk), :]
        return s.reshape(tk // SUBLANES, SUBLANES, tq)

    acc_s[...] = jnp.zeros(acc_s.shape, F32)

    def attn_tile(j, carry):
        ms, ls = carry
        r0 = pl.multiple_of(j * tk, tk)
        bias_tile(j, 0)
        new_m, new_l, alphas = [], [], []
        for h in range(N_HEADS):
            s = logits(j, h)
            m_new = jnp.maximum(ms[h], jnp.max(jnp.max(s, axis=0), axis=0, keepdims=True))
            alpha = jnp.exp(ms[h] - m_new)
            p = jnp.exp(s - m_new)
            new_m.append(m_new)
            new_l.append(alpha * ls[h] + jnp.sum(p, axis=0))
            alphas.append(alpha)
            p_s[h] = _bf(p.reshape(tk, tq))
        for h in range(N_HEADS):
            rows = slice(h * HEAD_DIM, (h + 1) * HEAD_DIM)
            acc_s[rows, :] = alphas[h] * acc_s[rows, :] + _dot(vt_ref[0, :, pl.ds(r0, tk)], p_s[h])
        return tuple(new_m), tuple(new_l)

    init = ((jnp.full((1, tq), NEG_BIG, F32),) * N_HEADS, (jnp.zeros((SUBLANES, tq), F32),) * N_HEADS)
    _, ls = lax.fori_loop(0, nkt, attn_tile, init)
    for h in range(N_HEADS):
        rows = slice(h * HEAD_DIM, (h + 1) * HEAD_DIM)
        acc_s[rows, :] = acc_s[rows, :] / jnp.sum(ls[h], axis=0, keepdims=True)
    o_ref[0] = _bf(acc_s[...].T)


def _dsa(qt, qit, wit, k, ki, vt):
    bsz, seq = k.shape[0], k.shape[1]
    aw = N_HEADS * HEAD_DIM
    tq = min(DSA_Q_COLS, seq)
    tk = min(DSA_K_ROWS, seq)
    topk = min(TOPK_MAX, seq // 4)
    assert (seq - 1) // POS_SPLIT < 256 and POS_SPLIT <= 256, "key positions must split into two bf16-exact parts"
    assert all(float(np.float32(sl).astype(BF16)) == sl for sl in ALIBI_SLOPES), "ALiBi slopes must be bf16-exact"
    pos = jnp.arange(seq, dtype=I32)
    posc = jnp.stack([(pos // POS_SPLIT) * POS_SPLIT, pos % POS_SPLIT], axis=1).astype(BF16)
    ka = jnp.concatenate([k, jnp.broadcast_to(posc[None], (bsz, seq, 2)),
                          jnp.zeros((bsz, seq, LANES - HEAD_DIM - 2), BF16)], axis=2)
    kern = functools.partial(_dsa_kernel, tq=tq, tk=tk, topk=topk, seq=seq)
    cols = lambda r: pl.BlockSpec((1, r, tq), lambda b, i: (b, 0, i))
    return pl.pallas_call(
        kern,
        out_shape=jax.ShapeDtypeStruct((bsz, seq, aw), BF16),
        grid=(bsz, seq // tq),
        in_specs=[cols(N_HEADS * LANES), cols(qit.shape[1]), cols(IDX_HEADS),
                  pl.BlockSpec((1, seq, LANES), lambda b, i: (b, 0, 0)),
                  pl.BlockSpec((1, seq, IDX_DIM), lambda b, i: (b, 0, 0)),
                  pl.BlockSpec((1, HEAD_DIM, seq), lambda b, i: (b, 0, 0))],
        out_specs=pl.BlockSpec((1, tq, aw), lambda b, i: (b, i, 0)),
        scratch_shapes=[pltpu.VMEM((seq, tq), I32), pltpu.VMEM((seq, tq), F32), pltpu.VMEM((aw, tq), F32),
                        pltpu.VMEM((32, seq // 32, tq), I32), pltpu.VMEM((N_HEADS, tk, tq), BF16)],
        compiler_params=_cparams(("arbitrary", "arbitrary")),
        name="dsa",
    )(qt, qit, wit, ka, ki, vt)


def _mix_kernel(x_ref, ys_ref, ya_ref, ada_ref, g1_ref, wgt_ref, wps_ref, wpa_ref, wo_ref, g2_ref,
                h_ref, u2_ref, *, d):
    gate1 = ada_ref[0, :, 2 * d:3 * d]
    shift2 = ada_ref[0, :, 3 * d:4 * d]
    scale2 = ada_ref[0, :, 4 * d:5 * d]
    x = x_ref[0]
    u = _bf(_rms(x, g1_ref[...]) * (1.0 + ada_ref[0, :, d:2 * d]) + ada_ref[0, :, 0:d])
    g = _dot(u, wgt_ref[...])
    mixed = (jax.nn.sigmoid(g[:, 0:d]) * _dot(ys_ref[0], wps_ref[...])
             + jax.nn.sigmoid(g[:, d:2 * d]) * _dot(ya_ref[0], wpa_ref[...]))
    h = x + gate1 * _dot(_bf(mixed), wo_ref[...])
    h_ref[0] = h
    u2_ref[0] = _rms(h, g2_ref[...]) * (1.0 + scale2) + shift2


def _mix(x, ys, ya, ada3, g1, w_in, wps, wpa, wo, g2):
    bsz, seq, d = x.shape
    tm = MIX_ROWS
    row = lambda w: pl.BlockSpec((1, tm, w), lambda b, l: (b, l, 0))
    full = lambda a: pl.BlockSpec(a.shape, lambda b, l: (0,) * a.ndim)
    wps, wpa, wo = wps.astype(BF16), wpa.astype(BF16), wo.astype(BF16)
    wgt = jnp.concatenate(_split_w_in(w_in, d)[7:9], axis=1).astype(BF16)
    g1 = g1.reshape(1, d)
    g2 = g2.reshape(1, d)
    return pl.pallas_call(
        functools.partial(_mix_kernel, d=d),
        out_shape=(jax.ShapeDtypeStruct((bsz, seq, d), F32), jax.ShapeDtypeStruct((bsz, seq, d), F32)),
        grid=(bsz, seq // tm),
        in_specs=[row(d), row(ys.shape[2]), row(ya.shape[2]),
                  pl.BlockSpec((1, 1, ada3.shape[2]), lambda b, l: (b, 0, 0)),
                  full(g1), full(wgt), full(wps), full(wpa), full(wo), full(g2)],
        out_specs=(row(d), row(d)),
        compiler_params=_cparams(("arbitrary", "arbitrary")),
        name="mix",
    )(x, ys, ya, ada3, g1, wgt, wps, wpa, wo, g2)


def _first_max(cur, idx, axis, big):
    m = jnp.max(cur, axis=axis, keepdims=True)
    first = jnp.min(jnp.where(cur == m, idx, big), axis=axis, keepdims=True)
    return m, idx == first


def _route_kernel(u_ref, wrh_ref, wrl_ref, rb_ref, tri_ref, ltri_ref,
                  gt_ref, loc_ref, c8_ref, loff_ref, run0_ref, tot_ref, run_s, *, t, tt):
    @pl.when(pl.program_id(0) == 0)
    def _():
        run_s[...] = jnp.zeros_like(run_s)

    uh, ul = _split(u_ref[...])
    logits = _dot_nt(wrh_ref[...], uh) + (_dot_nt(wrl_ref[...], uh) + _dot_nt(wrh_ref[...], ul))
    scores = jax.nn.sigmoid(logits)
    biased = scores + rb_ref[...]
    per_group = N_EXPERTS // N_GROUPS
    b3 = biased.reshape(N_GROUPS, per_group, t)
    i3 = lax.broadcasted_iota(I32, b3.shape, 1)
    m1, hit1 = _first_max(b3, i3, 1, per_group)
    m2 = jnp.max(jnp.where(hit1, -jnp.inf, b3), axis=1, keepdims=True)
    gs = (m1 + m2).reshape(N_GROUPS, t)
    gi = lax.broadcasted_iota(I32, gs.shape, 0)
    gsel = jnp.zeros(gs.shape, F32)
    for _ in range(TOPK_GROUPS):
        _, hit = _first_max(gs, gi, 0, N_GROUPS)
        gsel = jnp.where(hit, 1.0, gsel)
        gs = jnp.where(hit, -jnp.inf, gs)
    cur = jnp.where(gsel.reshape(N_GROUPS, 1, t) > 0.0, b3, -jnp.inf).reshape(N_EXPERTS, t)
    ei = lax.broadcasted_iota(I32, cur.shape, 0)
    hits = []
    gates = []
    for _ in range(TOP_K):
        _, hit = _first_max(cur, ei, 0, N_EXPERTS)
        hits.append(hit)
        gates.append(jnp.sum(jnp.where(hit, scores, 0.0), axis=0, keepdims=True))
        cur = jnp.where(hit, -jnp.inf, cur)
    gate = jnp.concatenate(gates, axis=0)
    gt_ref[...] = gate / jnp.sum(gate, axis=0, keepdims=True) * ROUTED_SCALE
    onehot = jnp.zeros(cur.shape, F32)
    for hit in hits:
        onehot = jnp.where(hit, 1.0, onehot)
    for sub in range(t // tt):
        cols = slice(sub * tt, (sub + 1) * tt)
        oh = onehot[:, cols]
        cnt = jnp.sum(oh, axis=1, keepdims=True)
        c8 = jnp.floor((cnt + (SUBLANES - 1)) * (1.0 / SUBLANES)) * SUBLANES
        c8l = jnp.broadcast_to(c8, (N_EXPERTS, LANES))
        loff = _dot(ltri_ref[...], _bf(c8l))
        slot = _dot(_bf(oh), tri_ref[...]) + loff[:, 0:1]
        loc_ref[:, cols] = jnp.concatenate(
            [jnp.sum(jnp.where(hit[:, cols], slot, 0.0), axis=0, keepdims=True) for hit in hits],
            axis=0).astype(I32)
        c8_ref[sub] = c8l
        loff_ref[sub] = loff
        run0_ref[sub] = run_s[...]
        run_s[...] = run_s[...] + c8
    tot_ref[...] = run_s[...]


def _route(u2, w_router, router_bias):
    n, d = u2.shape
    tt = min(MOE_TILE, n)
    t = min(ROUTE_TILES * tt, n)
    nt = n // tt
    wt = w_router.T
    wrh = wt.astype(BF16)
    wrl = (wt - wrh.astype(F32)).astype(BF16)
    tri = (jnp.arange(tt)[:, None] < jnp.arange(tt)[None, :]).astype(BF16)
    ex = jnp.arange(N_EXPERTS)
    ltri = (ex[None, :] < ex[:, None]).astype(BF16)
    full = lambda a: pl.BlockSpec(a.shape, lambda i: (0,) * a.ndim)
    col = pl.BlockSpec((TOP_K, t), lambda i: (0, i))
    tab = pl.BlockSpec((t // tt, N_EXPERTS, LANES), lambda i: (i, 0, 0))
    tab_sds = jax.ShapeDtypeStruct((nt, N_EXPERTS, LANES), F32)
    rb = router_bias.reshape(N_EXPERTS, 1)
    return pl.pallas_call(
        functools.partial(_route_kernel, t=t, tt=tt),
        out_shape=(jax.ShapeDtypeStruct((TOP_K, n), F32), jax.ShapeDtypeStruct((TOP_K, n), I32),
                   tab_sds, tab_sds, tab_sds, jax.ShapeDtypeStruct((N_EXPERTS, LANES), F32)),
        grid=(n // t,),
        in_specs=[pl.BlockSpec((t, d), lambda i: (i, 0)), full(wrh), full(wrl), full(rb), full(tri), full(ltri)],
        out_specs=(col, col, tab, tab, tab, pl.BlockSpec((N_EXPERTS, LANES), lambda i: (0, 0))),
        scratch_shapes=[pltpu.VMEM((N_EXPERTS, LANES), F32)],
        compiler_params=_cparams(("arbitrary",)),
        name="route",
    )(u2, wrh, wrl, rb, tri, ltri)


RUN_BITS = tuple(1 << b for b in reversed(range((MOE_TILE // SUBLANES).bit_length())))
RUN_LONG = 8


def _for_each_run_piece(n8_ref, src_ref, dst_ref, tile, bits, fn):
    def per_expert(e, _):
        idx = tile * N_EXPERTS + e
        n8 = n8_ref[idx]
        src = src_ref[idx]
        dst = dst_ref[idx]
        def pieces(some_bits):
            for p in some_bits:
                off = (n8 & ~(2 * p - 1)) * SUBLANES

                @pl.when((n8 & p) != 0)
                def _(p=p, off=off):
                    fn(pl.multiple_of(src + off, SUBLANES), pl.multiple_of(dst + off, SUBLANES), p * SUBLANES,
                       bits.index(p) % 2)

        long_bits = tuple(p for p in bits if p >= RUN_LONG)
        if long_bits:
            pl.when(n8 >= RUN_LONG)(lambda: pieces(long_bits))
        pieces(tuple(p for p in bits if p < RUN_LONG))
        return 0

    lax.fori_loop(0, N_EXPERTS, per_expert, 0)


def _issue_runs_inline(n8_ref, src_ref, dst_ref, tile, enable, fn, experts=range(N_EXPERTS)):
    for e in experts:
        idx = tile * N_EXPERTS + e
        n8 = jnp.where(enable, n8_ref[idx], 0)
        src = src_ref[idx]
        dst = dst_ref[idx]
        for b, p in enumerate(RUN_BITS):
            off = (n8 & ~(2 * p - 1)) * SUBLANES

            @pl.when((n8 & p) != 0)
            def _(b=b, p=p, off=off, src=src, dst=dst):
                fn(pl.multiple_of(src + off, SUBLANES), pl.multiple_of(dst + off, SUBLANES), p * SUBLANES, b % 2)


def _wait_rows(n8, make_copy, max_rows):
    for p in tuple(1 << b for b in reversed(range((max_rows // SUBLANES).bit_length()))):
        @pl.when((n8 & p) != 0)
        def _(p=p):
            make_copy(p * SUBLANES).wait()


def _dispatch_kernel(n8_ref, src_ref, dst_ref, tot_ref, zn8_ref, zdst_ref, u_ref, loc_ref, gate_ref, xs_hbm,
                     lbuf0, lbuf1, lbuf2, zx, sems, *, tt, nslot, dh):
    i = pl.program_id(0)
    last = pl.num_programs(0) - 1
    bufs = (lbuf0, lbuf1, lbuf2)
    nbuf = len(bufs)
    zsem = nbuf

    def copy_from(slot):
        def piece(s0, d0, rows, prio):
            pltpu.make_async_copy(
                bufs[slot].at[pl.ds(s0, rows)], xs_hbm.at[pl.ds(d0, rows)], sems.at[slot]).start(priority=prio)
        return piece

    def wait_tile(tile, slot):
        _wait_rows(tot_ref[tile], lambda rows: pltpu.make_async_copy(
            bufs[slot].at[pl.ds(0, rows)], xs_hbm.at[pl.ds(0, rows)], sems.at[slot]), nslot)

    @pl.when(i == 0)
    def _():
        zx[...] = jnp.zeros(zx.shape, I32)

        def zero_piece(s0, d0, rows, prio):
            cx = pltpu.make_async_copy(zx.at[pl.ds(0, rows)], xs_hbm.at[pl.ds(d0, rows)], sems.at[zsem])
            cx.start()
            cx.wait()

        zbits = tuple(1 << b for b in reversed(range((EXPERT_ROWS // SUBLANES - 1).bit_length())))
        _for_each_run_piece(zn8_ref, zdst_ref, zdst_ref, 0, zbits, zero_piece)

    def build(buf):
        ub = _bf(u_ref[...])
        ones = jnp.ones((tt, LANES), BF16)
        loc = loc_ref[...]
        gate = gate_ref[...]
        rows_b = lax.broadcasted_iota(I32, (SLOT_CHUNK, tt), 0).astype(F32).astype(BF16)
        loc_hi = lax.shift_right_logical(loc, SLOT_CHUNK.bit_length() - 1)
        loc_lo = (loc & (SLOT_CHUNK - 1)).astype(F32)
        gate_h = _bf(gate)
        gate_l = _bf(gate - gate_h.astype(F32))
        one_b = jnp.ones((SLOT_CHUNK, tt), BF16)
        for c in range(nslot // SLOT_CHUNK):
            perm = jnp.zeros((SLOT_CHUNK, tt), BF16)
            pgh = jnp.zeros((SLOT_CHUNK, tt), BF16)
            pgl = jnp.zeros((SLOT_CHUNK, tt), BF16)
            for k in range(TOP_K):
                lk = _bf(jnp.where(loc_hi[k:k + 1, :] == c, loc_lo[k:k + 1, :], -1.0))
                eq = rows_b == lk
                perm = jnp.where(eq, one_b, perm)
                pgh = jnp.where(eq, jnp.broadcast_to(gate_h[k:k + 1, :], (SLOT_CHUNK, tt)), pgh)
                pgl = jnp.where(eq, jnp.broadcast_to(gate_l[k:k + 1, :], (SLOT_CHUNK, tt)), pgl)
            cs = slice(c * SLOT_CHUNK, (c + 1) * SLOT_CHUNK)
            xp = lax.bitcast_convert_type(_dot(perm, ub), I32)
            buf[cs, 0:dh] = xp[:, 0:dh] | lax.shift_right_logical(xp[:, dh:2 * dh], 16)
            buf[cs, dh:dh + LANES] = lax.bitcast_convert_type(_dot(pgh, ones) + _dot(pgl, ones), I32)

    def step(cur):
        prv, prv2 = (cur - 1) % nbuf, (cur - 2) % nbuf

        @pl.when(i >= nbuf)
        def _():
            wait_tile(i - nbuf, cur)

        _issue_runs_inline(n8_ref, src_ref, dst_ref, jnp.maximum(i - 1, 0), i >= 1, copy_from(prv))
        build(bufs[cur])

        @pl.when(i == last)
        def _():
            _for_each_run_piece(n8_ref, src_ref, dst_ref, i, RUN_BITS, copy_from(cur))

            @pl.when(i >= 2)
            def _():
                wait_tile(i - 2, prv2)

            @pl.when(i >= 1)
            def _():
                wait_tile(i - 1, prv)
            wait_tile(i, cur)

    for cur in range(nbuf):
        pl.when(i % nbuf == cur)(functools.partial(step, cur))


def _dispatch(tabs, ztabs, u2, loc_t, gate_t, n_rows):
    n, d = u2.shape
    tt = min(MOE_TILE, n)
    nslot = TOP_K * tt + N_EXPERTS * SUBLANES
    dh = d // 2
    assert nslot % SLOT_CHUNK == 0 and tt // SUBLANES == RUN_BITS[0]
    col = pl.BlockSpec((TOP_K, tt), lambda i, *_: (0, i))
    return pl.pallas_call(
        functools.partial(_dispatch_kernel, tt=tt, nslot=nslot, dh=dh),
        out_shape=jax.ShapeDtypeStruct((n_rows, dh + LANES), I32),
        grid_spec=pltpu.PrefetchScalarGridSpec(
            num_scalar_prefetch=6, grid=(n // tt,),
            in_specs=[pl.BlockSpec((tt, d), lambda i, *_: (i, 0)), col, col],
            out_specs=pl.BlockSpec(memory_space=pl.ANY),
            scratch_shapes=[pltpu.VMEM((nslot, dh + LANES), I32)] * 3 + [
                pltpu.VMEM((EXPERT_ROWS // 2, dh + LANES), I32), pltpu.SemaphoreType.DMA((4,))]),
        compiler_params=_cparams(("arbitrary",)),
        name="dispatch",
    )(*tabs, *ztabs, u2, loc_t, gate_t)


def _experts_kernel(be_ref, nu_ref, xs_ref, wg_ref, wu_ref, wd_ref, ys_ref, *, d):
    del be_ref
    dh = d // 2

    @pl.when(pl.program_id(0) < nu_ref[0])
    def _():
        w = xs_ref[:, 0:dh]
        xa = _bf(lax.bitcast_convert_type(w & jnp.int32(-65536), F32))
        xb = _bf(lax.bitcast_convert_type(lax.shift_left(w, 16), F32))
        gate = lax.bitcast_convert_type(xs_ref[:, dh:dh + LANES], F32)
        hg = _dot(xa, _bf(wg_ref[0, 0:dh, :])) + _dot(xb, _bf(wg_ref[0, dh:d, :]))
        hu = _dot(xa, _bf(wu_ref[0, 0:dh, :])) + _dot(xb, _bf(wu_ref[0, dh:d, :]))
        y = _dot(_bf(jax.nn.silu(hg) * hu), _bf(wd_ref[0])) * jnp.tile(gate, (1, d // LANES))
        yb = lax.bitcast_convert_type(_bf(y).astype(F32), I32)
        ys_ref[...] = yb[:, 0:dh] | lax.shift_right_logical(yb[:, dh:d], 16)


def _experts(blk_expert, n_used, xs, wg, wu, wd):
    rows, xw = xs.shape
    d = wg.shape[1]
    de = wg.shape[2]
    nblk = rows // EXPERT_ROWS
    blk = lambda i, be, nu: jnp.minimum(i, nu[0] - 1)
    return pl.pallas_call(
        functools.partial(_experts_kernel, d=d),
        out_shape=jax.ShapeDtypeStruct((rows, d // 2), I32),
        grid_spec=pltpu.PrefetchScalarGridSpec(
            num_scalar_prefetch=2, grid=(nblk,),
            in_specs=[pl.BlockSpec((EXPERT_ROWS, xw), lambda i, be, nu: (blk(i, be, nu), 0)),
                      pl.BlockSpec((1, d, de), lambda i, be, nu: (be[blk(i, be, nu)], 0, 0)),
                      pl.BlockSpec((1, d, de), lambda i, be, nu: (be[blk(i, be, nu)], 0, 0)),
                      pl.BlockSpec((1, de, d), lambda i, be, nu: (be[blk(i, be, nu)], 0, 0))],
            out_specs=pl.BlockSpec((EXPERT_ROWS, d // 2), lambda i, be, nu: (blk(i, be, nu), 0))),
        compiler_params=_cparams(("arbitrary",)),
        name="experts",
    )(blk_expert, n_used, xs, wg, wu, wd)


def _combine_kernel(n8_ref, src_ref, dst_ref, tot_ref, ys_hbm, loc_ref, h_ref, u2_ref, ada_ref, wsg_ref, wsu_ref, wsd_ref,
                    gf_ref, o_ref, ybuf0, ybuf1, pick_s, sems, *, tt, nslot, d):
    i = pl.program_id(0)
    last = pl.num_programs(0) - 1
    bufs = (ybuf0, ybuf1)
    dh = d // 2
    nchunk = nslot // SLOT_CHUNK

    def copy_into(slot):
        def piece(s0, d0, rows, prio):
            pltpu.make_async_copy(
                ys_hbm.at[pl.ds(d0, rows)], bufs[slot].at[pl.ds(s0, rows)], sems.at[slot]).start(priority=prio)
        return piece

    @pl.when(i == 0)
    def _():
        ybuf0[...] = jnp.zeros(ybuf0.shape, I32)
        ybuf1[...] = jnp.zeros(ybuf1.shape, I32)
        _for_each_run_piece(n8_ref, src_ref, dst_ref, 0, RUN_BITS, copy_into(0))

    def step(cur, nxt):
        nxt_tile = jnp.minimum(i + 1, last)
        _issue_runs_inline(n8_ref, src_ref, dst_ref, nxt_tile, i < last, copy_into(nxt), range(N_EXPERTS // 2))
        x = _bf(u2_ref[...])
        shared = _dot(_bf(jax.nn.silu(_dot(x, wsg_ref[...])) * _dot(x, wsu_ref[...])), wsd_ref[...])
        loc = loc_ref[...]
        cols_b = lax.broadcasted_iota(I32, (tt, SLOT_CHUNK), 1).astype(F32).astype(BF16)
        loc_hi = lax.shift_right_logical(loc, SLOT_CHUNK.bit_length() - 1)
        loc_lo = (loc & (SLOT_CHUNK - 1)).astype(F32)
        one_b = jnp.ones((tt, SLOT_CHUNK), BF16)
        for c in range(nchunk):
            pick = jnp.zeros((tt, SLOT_CHUNK), BF16)
            for k in range(TOP_K):
                lk = _bf(jnp.where(loc_hi[:, k:k + 1] == c, loc_lo[:, k:k + 1], -1.0))
                pick = jnp.where(cols_b == lk, one_b, pick)
            pick_s[c] = pick
        _wait_rows(tot_ref[i], lambda rows: pltpu.make_async_copy(
            ys_hbm.at[pl.ds(0, rows)], bufs[cur].at[pl.ds(0, rows)], sems.at[cur]), nslot)
        _issue_runs_inline(n8_ref, src_ref, dst_ref, nxt_tile, i < last, copy_into(nxt),
                           range(N_EXPERTS // 2, N_EXPERTS))
        routed_a = jnp.zeros((tt, dh), F32)
        routed_b = jnp.zeros((tt, dh), F32)
        for c in range(nchunk):
            w = bufs[cur][c * SLOT_CHUNK:(c + 1) * SLOT_CHUNK, :]
            routed_a = routed_a + _dot(pick_s[c], _bf(lax.bitcast_convert_type(w & jnp.int32(-65536), F32)))
            routed_b = routed_b + _dot(pick_s[c], _bf(lax.bitcast_convert_type(lax.shift_left(w, 16), F32)))
        routed = jnp.concatenate([routed_a, routed_b], axis=1)
        gate2 = ada_ref[0, :, 5 * d:6 * d]
        h = h_ref[...] + gate2 * (routed + shared)
        o_ref[...] = _rms(h, gf_ref[...])

    pl.when(i % 2 == 0)(lambda: step(0, 1))
    pl.when(i % 2 == 1)(lambda: step(1, 0))


def _combine(tabs, ys, loc, h1, u2, ada3, wsg, wsu, wsd, gf, seq):
    n, d = h1.shape
    tt = min(MOE_TILE, n)
    nslot = TOP_K * tt + N_EXPERTS * SUBLANES
    per_b = seq // tt
    row = pl.BlockSpec((tt, d), lambda i, *_: (i, 0))
    full = lambda a: pl.BlockSpec(a.shape, lambda i, *_: (0,) * a.ndim)
    wsg, wsu, wsd = wsg.astype(BF16), wsu.astype(BF16), wsd.astype(BF16)
    gf = gf.reshape(1, d)
    return pl.pallas_call(
        functools.partial(_combine_kernel, tt=tt, nslot=nslot, d=d),
        out_shape=jax.ShapeDtypeStruct((n, d), F32),
        grid_spec=pltpu.PrefetchScalarGridSpec(
            num_scalar_prefetch=4, grid=(n // tt,),
            in_specs=[pl.BlockSpec(memory_space=pl.ANY),
                      pl.BlockSpec((tt, TOP_K), lambda i, *_: (i, 0)),
                      row, row,
                      pl.BlockSpec((1, 1, ada3.shape[2]), lambda i, *_: (i // per_b, 0, 0)),
                      full(wsg), full(wsu), full(wsd), full(gf)],
            out_specs=row,
            scratch_shapes=[pltpu.VMEM((nslot, d // 2), I32), pltpu.VMEM((nslot, d // 2), I32),
                            pltpu.VMEM((nslot // SLOT_CHUNK, tt, SLOT_CHUNK), BF16),
                            pltpu.SemaphoreType.DMA((2,))]),
        compiler_params=_cparams(("arbitrary",)),
        name="combine",
    )(*tabs, ys, loc, h1, u2, ada3, wsg, wsu, wsd, gf)


def _moe(h1, u2, ada3, w_router, router_bias, wg, wu, wd, wsg, wsu, wsd, gf):
    bsz, seq, d = h1.shape
    n = bsz * seq
    assert seq % min(MOE_TILE, n) == 0
    h1f = h1.reshape(n, d)
    u2f = u2.reshape(n, d)
    gate_t, loc_t, c8, loff, run0, tot = _route(u2f, w_router, router_bias)
    nt = c8.shape[0]
    as_tab = lambda a: a[:, :, 0].astype(I32)
    tot8 = tot[:, 0].astype(I32)
    padded = (tot8 + EXPERT_ROWS - 1) // EXPERT_ROWS * EXPERT_ROWS
    pend = jnp.cumsum(padded)
    pstart = (pend - padded).astype(I32)
    nblk = (n * TOP_K + nt * N_EXPERTS * (SUBLANES - 1) + N_EXPERTS * (EXPERT_ROWS - 1) + EXPERT_ROWS - 1) // EXPERT_ROWS
    blk_row0 = jnp.arange(nblk, dtype=I32) * EXPERT_ROWS
    blk_expert = jnp.minimum(jnp.sum(pend[None, :] <= blk_row0[:, None], axis=1), N_EXPERTS - 1).astype(I32)
    n_used = (pend[-1:] // EXPERT_ROWS).astype(I32)
    n8 = as_tab(c8) // SUBLANES
    tabs = (n8.reshape(-1), as_tab(loff).reshape(-1), (pstart[None, :] + as_tab(run0)).reshape(-1),
            jnp.sum(n8, axis=1))
    ztabs = ((padded - tot8) // SUBLANES, pstart + tot8)
    xs = _dispatch(tabs, ztabs, u2f, loc_t, gate_t, nblk * EXPERT_ROWS)
    ys = _experts(blk_expert, n_used, xs, wg, wu, wd)
    out = _combine(tabs, ys, loc_t.T, h1f, u2f, ada3, wsg, wsu, wsd, gf, seq)
    return out.reshape(bsz, seq, d)


def kernel(x, c, w_ada, b_ada, norm1_g, w_in, ssm_lambda_re, ssm_lambda_im, ssm_log_dt, ssm_b_re, ssm_b_im,
           ssm_c_re, ssm_c_im, ssm_d, ssm_w_glu, ssm_b_glu, w_proj_ssm, w_proj_attn, w_out, norm2_g, w_router,
           router_bias, w_exp_gate, w_exp_up, w_exp_down, w_sh_gate, w_sh_up, w_sh_down, norm_f_g):
    depth = w_ada.shape[0]
    assert depth == 1, "the final norm is fused into the last (only) layer's combine kernel"
    bsz, seq, d = x.shape
    layer = 0
    ada3 = _ada(c, w_ada[layer], b_ada[layer]).reshape(bsz, 1, 6 * d)
    us, k, ki, qt, qit, vt, wit = _inproj(x, ada3, norm1_g[layer], w_in[layer])
    a_re, a_im, bb_re, bb_im = _s5disc(ssm_lambda_re[layer], ssm_lambda_im[layer], ssm_log_dt[layer],
                                       ssm_b_re[layer], ssm_b_im[layer])
    ys_t = _s5(us, a_re, a_im, bb_re, bb_im, ssm_c_re[layer], ssm_c_im[layer],
               ssm_d[layer], ssm_w_glu[layer], ssm_b_glu[layer])
    ya = _dsa(qt, qit, wit, k, ki, vt)
    h1, u2 = _mix(x, ys_t, ya, ada3, norm1_g[layer], w_in[layer], w_proj_ssm[layer],
                  w_proj_attn[layer], w_out[layer], norm2_g[layer])
    return _moe(h1, u2, ada3, w_router[layer], router_bias[layer], w_exp_gate[layer], w_exp_up[layer],
                w_exp_down[layer], w_sh_gate[layer], w_sh_up[layer], w_sh_down[layer], norm_f_g)
```

```python
import functools

import jax
import jax.numpy as jnp
import numpy as np
from jax import lax
from jax.experimental import pallas as pl
from jax.experimental.pallas import tpu as pltpu

F32 = jnp.float32
BF16 = jnp.bfloat16
I32 = jnp.int32

SSM_GROUP = 16
SSM_STATE = 64
N_HEADS = 8
HEAD_DIM = 64
IDX_HEADS = 8
IDX_DIM = 64
TOPK_MAX = 256
N_EXPERTS = 64
TOP_K = 8
N_GROUPS = 8
TOPK_GROUPS = 4
ROUTED_SCALE = 2.5
EPS = 1e-6

V7X_VMEM_LIMIT_BYTES = 56 * 1024 * 1024
LANES = 128
SUBLANES = 8

INPROJ_ROWS = 512
S5_STEPS = 64
S5_LANE_CHUNK = 128
DSA_Q_COLS = 256
DSA_K_ROWS = 512
DSA_COUNT_ROWS = 64
BITSLICE_ROWS = 256
POS_SPLIT = 64
MIX_ROWS = 512
MOE_TILE = 256
ROUTE_TILES = 4
SLOT_CHUNK = 256
EXPERT_ROWS = 1024

NEG_BIG = -1e30
INT_MIN = -(2 ** 31)


def _cparams(sem):
    return pltpu.CompilerParams(dimension_semantics=sem, vmem_limit_bytes=V7X_VMEM_LIMIT_BYTES)


def _bf(x):
    return x.astype(BF16)


def _dot(a, b):
    return jnp.dot(a, b, preferred_element_type=F32)


def _dot_nt(a, b):
    return lax.dot_general(a, b, (((1,), (1,)), ((), ())), preferred_element_type=F32)


def _split(x):
    hi = _bf(x)
    lo = _bf(x - hi.astype(F32))
    return hi, lo


def _dot3(a, b):
    ah, al = _split(a)
    bh, bl = _split(b)
    return _dot(ah, bh) + (_dot(ah, bl) + _dot(al, bh))


def _rms(x, g):
    return x * lax.rsqrt(jnp.mean(x * x, axis=-1, keepdims=True) + EPS) * g


def _ada_kernel(c_ref, w_ref, b_ref, o_ref):
    c = c_ref[...]
    o_ref[...] = _dot3(c * jax.nn.sigmoid(c), w_ref[...]) + b_ref[...]


def _ada(c, w, b):
    bsz, d = c.shape
    n = w.shape[1]
    tn = 1024
    return pl.pallas_call(
        _ada_kernel,
        out_shape=jax.ShapeDtypeStruct((bsz, n), F32),
        grid=(n // tn,),
        in_specs=[pl.BlockSpec((bsz, d), lambda j: (0, 0)),
                  pl.BlockSpec((d, tn), lambda j: (0, j)),
                  pl.BlockSpec((1, tn), lambda j: (0, j))],
        out_specs=pl.BlockSpec((bsz, tn), lambda j: (0, j)),
        compiler_params=_cparams(("arbitrary",)),
        name="ada",
    )(c, w, b.reshape(1, n))


ALIBI_SLOPES = tuple(2.0 ** (-8.0 * (h + 1) / N_HEADS) for h in range(N_HEADS))
QAUG_ROWS = 16


def _inproj_kernel(x_ref, ada_ref, g1_ref, w_ref, wt_ref,
                   us_ref, k_ref, ki_ref, qt_ref, qit_ref, vt_ref, wit_ref, *, d, ssm_w, attn_w, idx_w, tl):
    x = x_ref[0]
    shift = ada_ref[0, :, 0:d]
    scale = ada_ref[0, :, d:2 * d]
    u = _bf(_rms(x, g1_ref[...]) * (1.0 + scale) + shift)
    r = _dot(u, w_ref[...])
    us_ref[0] = r[:, 0:ssm_w]
    k_ref[0] = _bf(r[:, ssm_w:ssm_w + HEAD_DIM])
    ki_ref[0] = _bf(r[:, ssm_w + LANES:ssm_w + LANES + IDX_DIM])
    rt = _dot_nt(wt_ref[...], u)
    arow = lax.broadcasted_iota(I32, (QAUG_ROWS, tl), 0)
    for h in range(N_HEADS):
        base = h * LANES
        qt_ref[0, base:base + HEAD_DIM, :] = _bf(rt[h * HEAD_DIM:(h + 1) * HEAD_DIM])
        qt_ref[0, base + HEAD_DIM:base + HEAD_DIM + QAUG_ROWS, :] = _bf(jnp.where(arow < 2, ALIBI_SLOPES[h], 0.0))
        qt_ref[0, base + HEAD_DIM + QAUG_ROWS:base + LANES, :] = jnp.zeros((LANES - HEAD_DIM - QAUG_ROWS, tl), BF16)
    qit_ref[0] = _bf(rt[attn_w:attn_w + idx_w])
    vt_ref[0] = _bf(rt[attn_w + idx_w:attn_w + idx_w + HEAD_DIM])
    wit_ref[0] = rt[attn_w + idx_w + HEAD_DIM:attn_w + idx_w + HEAD_DIM + IDX_HEADS]


def _split_w_in(w_in, d):
    ssm_w = 512
    sizes = (ssm_w, N_HEADS * HEAD_DIM, HEAD_DIM, HEAD_DIM, IDX_HEADS * IDX_DIM, IDX_DIM, IDX_HEADS, d, d)
    offs = [0]
    for s in sizes:
        offs.append(offs[-1] + s)
    return [w_in[:, offs[i]:offs[i + 1]] for i in range(9)]


def _inproj(x, ada3, g1, w_in):
    bsz, seq, d = x.shape
    ssm_w = 512
    attn_w = N_HEADS * HEAD_DIM
    idx_w = IDX_HEADS * IDX_DIM
    w_ssm, w_q, w_k, w_v, w_qi, w_ki, w_wi, _, _ = _split_w_in(w_in, d)
    zpad = lambda n: jnp.zeros((d, n), F32)
    wbig = jnp.concatenate([w_ssm, w_k, zpad(LANES - HEAD_DIM), w_ki, zpad(LANES - IDX_DIM)], axis=1).astype(BF16)
    wt = jnp.concatenate([w_q * (HEAD_DIM ** -0.5), w_qi * (IDX_DIM ** -0.5), w_v, w_wi,
                          zpad(LANES - HEAD_DIM - IDX_HEADS)], axis=1).T.astype(BF16)
    tl = INPROJ_ROWS
    kern = functools.partial(_inproj_kernel, d=d, ssm_w=ssm_w, attn_w=attn_w, idx_w=idx_w, tl=tl)
    row = lambda w: pl.BlockSpec((1, tl, w), lambda b, l: (b, l, 0))
    colt = lambda h: pl.BlockSpec((1, h, tl), lambda b, l: (b, 0, l))
    full = lambda a: pl.BlockSpec(a.shape, lambda b, l: (0,) * a.ndim)
    return pl.pallas_call(
        kern,
        out_shape=(jax.ShapeDtypeStruct((bsz, seq, ssm_w), F32),
                   jax.ShapeDtypeStruct((bsz, seq, HEAD_DIM), BF16),
                   jax.ShapeDtypeStruct((bsz, seq, IDX_DIM), BF16),
                   jax.ShapeDtypeStruct((bsz, N_HEADS * LANES, seq), BF16),
                   jax.ShapeDtypeStruct((bsz, idx_w, seq), BF16),
                   jax.ShapeDtypeStruct((bsz, HEAD_DIM, seq), BF16),
                   jax.ShapeDtypeStruct((bsz, IDX_HEADS, seq), F32)),
        grid=(bsz, seq // tl),
        in_specs=[row(d),
                  pl.BlockSpec((1, 1, ada3.shape[2]), lambda b, l: (b, 0, 0)),
                  pl.BlockSpec((1, d), lambda b, l: (0, 0)),
                  full(wbig), full(wt)],
        out_specs=(row(ssm_w), row(HEAD_DIM), row(IDX_DIM),
                   colt(N_HEADS * LANES), colt(idx_w), colt(HEAD_DIM), colt(IDX_HEADS)),
        compiler_params=_cparams(("arbitrary", "arbitrary")),
        name="inproj",
    )(x, ada3, g1.reshape(1, d), wbig, wt)


def _s5disc_kernel(lr_ref, li_ref, ldt_ref, br_ref, bi_ref, are_ref, aim_ref, bbr_ref, bbi_ref):
    lr = lr_ref[...]
    li = li_ref[...]
    dt = jnp.exp(ldt_ref[...])
    mag = jnp.exp(lr * dt)
    a_re = mag * jnp.cos(li * dt)
    a_im = mag * jnp.sin(li * dt)
    den = lr * lr + li * li
    n_re = a_re - 1.0
    f_re = (n_re * lr + a_im * li) / den
    f_im = (a_im * lr - n_re * li) / den
    br = br_ref[...]
    bi = bi_ref[...]
    are_ref[...] = a_re
    aim_ref[...] = a_im
    bbr_ref[...] = f_re * br - f_im * bi
    bbi_ref[...] = f_re * bi + f_im * br


def _s5disc(lam_re, lam_im, log_dt, b_re, b_im):
    g, p = lam_re.shape
    h = b_re.shape[2]
    rep = lambda a: jnp.repeat(a, h, axis=1)
    ldt = jnp.broadcast_to(log_dt[:, None], (g, p * h))
    sds = jax.ShapeDtypeStruct((g, p * h), F32)
    a_re, a_im, bb_re, bb_im = pl.pallas_call(
        _s5disc_kernel, out_shape=(sds, sds, sds, sds), name="s5disc",
    )(rep(lam_re), rep(lam_im), ldt, b_re.reshape(g, p * h), b_im.reshape(g, p * h))
    return a_re[:, ::h], a_im[:, ::h], bb_re.reshape(g, p, h), bb_im.reshape(g, p, h)


def _s5_kernel(u_ref, wb_ref, ar_ref, ai_ref, cc_ref, dsk_ref, wg_ref, bg_ref, o_ref, buf, hst, *, tl, width):
    nch = width // S5_LANE_CHUNK
    sw = S5_LANE_CHUNK // SSM_GROUP * SSM_STATE
    rows = tl * SUBLANES

    @pl.when(pl.program_id(0) == 0)
    def _():
        hst[...] = jnp.zeros_like(hst)

    u = jnp.swapaxes(u_ref[...], 0, 1).reshape(rows, width)
    ub = _bf(u)
    for j in range(nch):
        buf[:, j * 2 * sw:(j + 1) * 2 * sw] = _dot(ub[:, j * S5_LANE_CHUNK:(j + 1) * S5_LANE_CHUNK], wb_ref[j])

    for j in range(nch):
        re_cols = slice(j * 2 * sw, j * 2 * sw + sw)
        im_cols = slice(j * 2 * sw + sw, (j + 1) * 2 * sw)
        a_re = jnp.broadcast_to(ar_ref[:, j * sw:(j + 1) * sw], (SUBLANES, sw))
        a_im = jnp.broadcast_to(ai_ref[:, j * sw:(j + 1) * sw], (SUBLANES, sw))

        def step(t, carry, re_cols=re_cols, im_cols=im_cols, a_re=a_re, a_im=a_im):
            h_re, h_im = carry
            r0 = pl.multiple_of(t * SUBLANES, SUBLANES)
            n_re = (a_re * h_re - a_im * h_im) + buf[pl.ds(r0, SUBLANES), re_cols]
            n_im = (a_re * h_im + a_im * h_re) + buf[pl.ds(r0, SUBLANES), im_cols]
            buf[pl.ds(r0, SUBLANES), re_cols] = n_re
            buf[pl.ds(r0, SUBLANES), im_cols] = n_im
            return n_re, n_im

        h_re, h_im = lax.fori_loop(0, tl, step, (hst[:, re_cols], hst[:, im_cols]), unroll=True)
        hst[:, re_cols] = h_re
        hst[:, im_cols] = h_im

    ys = [_dot(_bf(buf[:, j * 2 * sw:(j + 1) * 2 * sw]), cc_ref[j]) for j in range(nch)]
    y = jnp.concatenate(ys, axis=1) + dsk_ref[...] * u
    y = jax.nn.gelu(y)
    y = y * jax.nn.sigmoid(_dot(_bf(y), wg_ref[...]) + bg_ref[...])
    o_ref[...] = _bf(jnp.swapaxes(y.reshape(tl, SUBLANES, width), 0, 1))


def _s5(u_t, a_re, a_im, bb_re, bb_im, c_re, c_im, d_skip, w_glu, b_glu):
    bsz, seq, width = u_t.shape
    assert bsz == SUBLANES
    nch = width // S5_LANE_CHUNK
    gpc = S5_LANE_CHUNK // SSM_GROUP
    sw = gpc * SSM_STATE
    eye = jnp.eye(gpc, dtype=F32)

    def bmat(bb):
        t = bb.reshape(nch, gpc, SSM_STATE, SSM_GROUP).transpose(0, 1, 3, 2)
        return jnp.einsum('jghp,gk->jghkp', t, eye).reshape(nch, S5_LANE_CHUNK, sw)

    def cmat(cc):
        t = cc.reshape(nch, gpc, SSM_GROUP, SSM_STATE).transpose(0, 1, 3, 2)
        return jnp.einsum('jgph,gk->jgpkh', t, eye).reshape(nch, sw, S5_LANE_CHUNK)

    wb = jnp.concatenate([bmat(bb_re), bmat(bb_im)], axis=2)
    cc = jnp.concatenate([cmat(c_re), -cmat(c_im)], axis=1)
    tl = S5_STEPS
    full = lambda a: pl.BlockSpec(a.shape, lambda i: (0,) * a.ndim)
    args = (u_t, wb.astype(BF16), a_re.reshape(1, -1), a_im.reshape(1, -1), cc.astype(BF16),
            d_skip.reshape(1, width), w_glu.astype(BF16), b_glu.reshape(1, width))
    return pl.pallas_call(
        functools.partial(_s5_kernel, tl=tl, width=width),
        out_shape=jax.ShapeDtypeStruct((bsz, seq, width), BF16),
        grid=(seq // tl,),
        in_specs=[pl.BlockSpec((bsz, tl, width), lambda i: (0, i, 0))] + [full(a) for a in args[1:]],
        out_specs=pl.BlockSpec((bsz, tl, width), lambda i: (0, i, 0)),
        scratch_shapes=[pltpu.VMEM((tl * SUBLANES, nch * 2 * sw), F32),
                        pltpu.VMEM((SUBLANES, nch * 2 * sw), F32)],
        compiler_params=_cparams(("arbitrary",)),
        name="s5",
    )(*args)


def _bit_transpose32(words):
    x = list(words)
    j, m = 16, 0x0000FFFF
    while j:
        k = 0
        while k < 32:
            t = (x[k] ^ lax.shift_right_logical(x[k + j], jnp.int32(j))) & jnp.int32(m - (1 << 32) if m >= 1 << 31 else m)
            x[k] = x[k] ^ t
            x[k + j] = x[k + j] ^ lax.shift_left(t, jnp.int32(j))
            k = (k + j + 1) & ~j
        j >>= 1
        m = (m ^ (m << j)) & 0xFFFFFFFF
    return x


def _dsa_kernel(qt_ref, qit_ref, wit_ref, ka_ref, ki_ref, vt_ref, o_ref, key_s, mb_s, acc_s, pl_s, p_s, *, tq, tk, topk, seq):
    i = pl.program_id(1)
    q0 = i * tq
    nkt = (q0 + tq + tk - 1) // tk
    ch = DSA_COUNT_ROWS
    krow = lax.broadcasted_iota(I32, (tk, tq), 0)
    qcol = q0 + lax.broadcasted_iota(I32, (tk, tq), 1)
    crow = lax.broadcasted_iota(I32, (ch, tq), 0)

    wb = wit_ref[0] * (IDX_HEADS ** -0.5)

    def score_tile(j, _):
        r0 = pl.multiple_of(j * tk, tk)
        kit = ki_ref[0, pl.ds(r0, tk), :]
        acc = jnp.zeros((tk, tq), F32)
        for h in range(IDX_HEADS):
            s = _dot(kit, qit_ref[0, h * IDX_DIM:(h + 1) * IDX_DIM, :])
            acc = acc + wb[h:h + 1, :] * jnp.maximum(s, 0.0)
        bits = lax.bitcast_convert_type(acc, I32)
        key = jnp.where(bits < 0, bits ^ jnp.int32(0x7FFFFFFF), bits)
        key = jnp.where(acc == 0.0, 0, key)
        key = jnp.where(krow + r0 <= qcol, key, INT_MIN)
        key_s[pl.ds(r0, tk), :] = key
        ukey = key ^ INT_MIN
        for c in range(tk // BITSLICE_ROWS):
            words = [ukey[c * BITSLICE_ROWS + v * SUBLANES:c * BITSLICE_ROWS + (v + 1) * SUBLANES, :]
                     for v in range(32)]
            planes = _bit_transpose32(words)
            g0 = pl.multiple_of((j * (tk // BITSLICE_ROWS) + c) * SUBLANES, SUBLANES)
            for it in range(32):
                pl_s[it, pl.ds(g0, SUBLANES), :] = planes[it]
        return 0

    @pl.when((pl.program_id(0) == 0) & (i == 0))
    def _():
        pl_s[...] = jnp.zeros(pl_s.shape, I32)

    def score_pair(jj, _):
        score_tile(2 * jj, 0)
        score_tile(2 * jj + 1, 0)
        return 0

    lax.fori_loop(0, nkt // 2, score_pair, 0)

    @pl.when(nkt % 2 == 1)
    def _():
        score_tile(nkt - 1, 0)

    def count(pred):
        def tile(j, cnt):
            for c in range(tk // ch):
                rr = pl.multiple_of(j * tk + c * ch, ch)
                cnt = cnt + jnp.where(pred(key_s[pl.ds(rr, ch), :], rr), 1, 0)
            return cnt
        cnt = lax.fori_loop(0, nkt, tile, jnp.zeros((ch, tq), I32))
        return jnp.sum(cnt.astype(F32), axis=0, keepdims=True)

    ngrp = seq // 32

    def lane_count(words):
        pc = lax.population_count(words).reshape(ngrp // SUBLANES, SUBLANES, tq)
        return jnp.sum(jnp.sum(pc, axis=0).astype(F32), axis=0, keepdims=True)

    def bit_step(it, carry):
        alive, above, ans_u = carry
        ones = alive & pl_s[it]
        cnt1 = lane_count(ones)
        take = above + cnt1 >= float(topk)
        alive = jnp.where(take, ones, alive ^ ones)
        above = jnp.where(take, above, above + cnt1)
        ans_u = jnp.where(take, ans_u | lax.shift_left(jnp.int32(1), 31 - it), ans_u)
        return alive, above, ans_u

    grow = lax.broadcasted_iota(I32, (ngrp, tq), 0)
    alive0 = jnp.where(grow < nkt * (tk // 32), -1, 0)
    alive, above, ans_u = lax.fori_loop(
        0, 32, bit_step, (alive0, jnp.zeros((1, tq), F32), jnp.zeros((1, tq), I32)))
    thr = jnp.maximum(ans_u ^ INT_MIN, INT_MIN + 1)
    cnt_ge = above + lane_count(alive)
    tied = jnp.where(ans_u != 0, cnt_ge, 0.0) > float(topk)
    has_ties = jnp.max(jnp.where(tied, 1.0, 0.0)) > 0.0

    def tie_cut():
        need = float(topk) - count(lambda kb, rr: kb > thr)
        nbits = max(1, (seq - 1).bit_length())

        def idx_step(b, x):
            cand = x | lax.shift_left(jnp.int32(1), nbits - 1 - b)
            below = count(lambda kb, rr: jnp.where(kb == thr, crow + rr, seq) < cand)
            return jnp.where(below < need, cand, x)

        x = lax.fori_loop(0, nbits, idx_step, jnp.zeros((1, tq), I32))
        return jnp.where(tied, x, seq)

    cut = lax.cond(has_ties, tie_cut, lambda: jnp.full((1, tq), seq, I32))

    def bias_tile(j, _):
        for c in range(tk // ch):
            rr = pl.multiple_of(j * tk + c * ch, ch)
            kb = key_s[pl.ds(rr, ch), :]
            tie_bias = jnp.where(crow + rr <= cut, 0.0, NEG_BIG)
            mb_s[pl.ds(rr, ch), :] = jnp.where(kb > thr, 0.0, jnp.where(kb == thr, tie_bias, NEG_BIG))
        return 0

    def logits(j, h):
        r0 = pl.multiple_of(j * tk, tk)
        s = _dot(ka_ref[0, pl.ds(r0, tk), :], qt_ref[0, h * LANES:(h + 1) * LANES, :]) + mb_s[pl.ds(r0, tk), :]
        return s.reshape(tk // SUBLANES, SUBLANES, tq)

    acc_s[...] = jnp.zeros(acc_s.shape, F32)

    def attn_tile(j, carry):
        ms, ls = carry
        r0 = pl.multiple_of(j * tk, tk)
        bias_tile(j, 0)
        new_m, new_l, alphas = [], [], []
        for h in range(N_HEADS):
            s = logits(j, h)
            m_new = jnp.maximum(ms[h], jnp.max(jnp.max(s, axis=0), axis=0, keepdims=True))
            alpha = jnp.exp(ms[h] - m_new)
            p = jnp.exp(s - m_new)
            new_m.append(m_new)
            new_l.append(alpha * ls[h] + jnp.sum(p, axis=0))
            alphas.append(alpha)
            p_s[h] = _bf(p.reshape(tk, tq))
        for h in range(N_HEADS):
            rows = slice(h * HEAD_DIM, (h + 1) * HEAD_DIM)
            acc_s[rows, :] = alphas[h] * acc_s[rows, :] + _dot(vt_ref[0, :, pl.ds(r0, tk)], p_s[h])
        return tuple(new_m), tuple(new_l)

    init = ((jnp.full((1, tq), NEG_BIG, F32),) * N_HEADS, (jnp.zeros((SUBLANES, tq), F32),) * N_HEADS)
    _, ls = lax.fori_loop(0, nkt, attn_tile, init)
    for h in range(N_HEADS):
        rows = slice(h * HEAD_DIM, (h + 1) * HEAD_DIM)
        acc_s[rows, :] = acc_s[rows, :] / jnp.sum(ls[h], axis=0, keepdims=True)
    o_ref[0] = _bf(acc_s[...].T)


def _dsa(qt, qit, wit, k, ki, vt):
    bsz, seq = k.shape[0], k.shape[1]
    aw = N_HEADS * HEAD_DIM
    tq = min(DSA_Q_COLS, seq)
    tk = min(DSA_K_ROWS, seq)
    topk = min(TOPK_MAX, seq // 4)
    assert (seq - 1) // POS_SPLIT < 256 and POS_SPLIT <= 256, "key positions must split into two bf16-exact parts"
    assert all(float(np.float32(sl).astype(BF16)) == sl for sl in ALIBI_SLOPES), "ALiBi slopes must be bf16-exact"
    pos = jnp.arange(seq, dtype=I32)
    posc = jnp.stack([(pos // POS_SPLIT) * POS_SPLIT, pos % POS_SPLIT], axis=1).astype(BF16)
    ka = jnp.concatenate([k, jnp.broadcast_to(posc[None], (bsz, seq, 2)),
                          jnp.zeros((bsz, seq, LANES - HEAD_DIM - 2), BF16)], axis=2)
    kern = functools.partial(_dsa_kernel, tq=tq, tk=tk, topk=topk, seq=seq)
    cols = lambda r: pl.BlockSpec((1, r, tq), lambda b, i: (b, 0, i))
    return pl.pallas_call(
        kern,
        out_shape=jax.ShapeDtypeStruct((bsz, seq, aw), BF16),
        grid=(bsz, seq // tq),
        in_specs=[cols(N_HEADS * LANES), cols(qit.shape[1]), cols(IDX_HEADS),
                  pl.BlockSpec((1, seq, LANES), lambda b, i: (b, 0, 0)),
                  pl.BlockSpec((1, seq, IDX_DIM), lambda b, i: (b, 0, 0)),
                  pl.BlockSpec((1, HEAD_DIM, seq), lambda b, i: (b, 0, 0))],
        out_specs=pl.BlockSpec((1, tq, aw), lambda b, i: (b, i, 0)),
        scratch_shapes=[pltpu.VMEM((seq, tq), I32), pltpu.VMEM((seq, tq), F32), pltpu.VMEM((aw, tq), F32),
                        pltpu.VMEM((32, seq // 32, tq), I32), pltpu.VMEM((N_HEADS, tk, tq), BF16)],
        compiler_params=_cparams(("arbitrary", "arbitrary")),
        name="dsa",
    )(qt, qit, wit, ka, ki, vt)


def _mix_kernel(x_ref, ys_ref, ya_ref, ada_ref, g1_ref, wgt_ref, wps_ref, wpa_ref, wo_ref, g2_ref,
                h_ref, u2_ref, *, d):
    gate1 = ada_ref[0, :, 2 * d:3 * d]
    shift2 = ada_ref[0, :, 3 * d:4 * d]
    scale2 = ada_ref[0, :, 4 * d:5 * d]
    x = x_ref[0]
    u = _bf(_rms(x, g1_ref[...]) * (1.0 + ada_ref[0, :, d:2 * d]) + ada_ref[0, :, 0:d])
    g = _dot(u, wgt_ref[...])
    mixed = (jax.nn.sigmoid(g[:, 0:d]) * _dot(ys_ref[0], wps_ref[...])
             + jax.nn.sigmoid(g[:, d:2 * d]) * _dot(ya_ref[0], wpa_ref[...]))
    h = x + gate1 * _dot(_bf(mixed), wo_ref[...])
    h_ref[0] = h
    u2_ref[0] = _rms(h, g2_ref[...]) * (1.0 + scale2) + shift2


def _mix(x, ys, ya, ada3, g1, w_in, wps, wpa, wo, g2):
    bsz, seq, d = x.shape
    tm = MIX_ROWS
    row = lambda w: pl.BlockSpec((1, tm, w), lambda b, l: (b, l, 0))
    full = lambda a: pl.BlockSpec(a.shape, lambda b, l: (0,) * a.ndim)
    wps, wpa, wo = wps.astype(BF16), wpa.astype(BF16), wo.astype(BF16)
    wgt = jnp.concatenate(_split_w_in(w_in, d)[7:9], axis=1).astype(BF16)
    g1 = g1.reshape(1, d)
    g2 = g2.reshape(1, d)
    return pl.pallas_call(
        functools.partial(_mix_kernel, d=d),
        out_shape=(jax.ShapeDtypeStruct((bsz, seq, d), F32), jax.ShapeDtypeStruct((bsz, seq, d), F32)),
        grid=(bsz, seq // tm),
        in_specs=[row(d), row(ys.shape[2]), row(ya.shape[2]),
                  pl.BlockSpec((1, 1, ada3.shape[2]), lambda b, l: (b, 0, 0)),
                  full(g1), full(wgt), full(wps), full(wpa), full(wo), full(g2)],
        out_specs=(row(d), row(d)),
        compiler_params=_cparams(("arbitrary", "arbitrary")),
        name="mix",
    )(x, ys, ya, ada3, g1, wgt, wps, wpa, wo, g2)


def _first_max(cur, idx, axis, big):
    m = jnp.max(cur, axis=axis, keepdims=True)
    first = jnp.min(jnp.where(cur == m, idx, big), axis=axis, keepdims=True)
    return m, idx == first


def _route_kernel(u_ref, wrh_ref, wrl_ref, rb_ref, tri_ref, ltri_ref,
                  gt_ref, loc_ref, c8_ref, loff_ref, run0_ref, tot_ref, run_s, *, t, tt):
    @pl.when(pl.program_id(0) == 0)
    def _():
        run_s[...] = jnp.zeros_like(run_s)

    uh, ul = _split(u_ref[...])
    logits = _dot_nt(wrh_ref[...], uh) + (_dot_nt(wrl_ref[...], uh) + _dot_nt(wrh_ref[...], ul))
    scores = jax.nn.sigmoid(logits)
    biased = scores + rb_ref[...]
    per_group = N_EXPERTS // N_GROUPS
    b3 = biased.reshape(N_GROUPS, per_group, t)
    i3 = lax.broadcasted_iota(I32, b3.shape, 1)
    m1, hit1 = _first_max(b3, i3, 1, per_group)
    m2 = jnp.max(jnp.where(hit1, -jnp.inf, b3), axis=1, keepdims=True)
    gs = (m1 + m2).reshape(N_GROUPS, t)
    gi = lax.broadcasted_iota(I32, gs.shape, 0)
    gsel = jnp.zeros(gs.shape, F32)
    for _ in range(TOPK_GROUPS):
        _, hit = _first_max(gs, gi, 0, N_GROUPS)
        gsel = jnp.where(hit, 1.0, gsel)
        gs = jnp.where(hit, -jnp.inf, gs)
    cur = jnp.where(gsel.reshape(N_GROUPS, 1, t) > 0.0, b3, -jnp.inf).reshape(N_EXPERTS, t)
    ei = lax.broadcasted_iota(I32, cur.shape, 0)
    hits = []
    gates = []
    for _ in range(TOP_K):
        _, hit = _first_max(cur, ei, 0, N_EXPERTS)
        hits.append(hit)
        gates.append(jnp.sum(jnp.where(hit, scores, 0.0), axis=0, keepdims=True))
        cur = jnp.where(hit, -jnp.inf, cur)
    gate = jnp.concatenate(gates, axis=0)
    gt_ref[...] = gate / jnp.sum(gate, axis=0, keepdims=True) * ROUTED_SCALE
    onehot = jnp.zeros(cur.shape, F32)
    for hit in hits:
        onehot = jnp.where(hit, 1.0, onehot)
    for sub in range(t // tt):
        cols = slice(sub * tt, (sub + 1) * tt)
        oh = onehot[:, cols]
        cnt = jnp.sum(oh, axis=1, keepdims=True)
        c8 = jnp.floor((cnt + (SUBLANES - 1)) * (1.0 / SUBLANES)) * SUBLANES
        c8l = jnp.broadcast_to(c8, (N_EXPERTS, LANES))
        loff = _dot(ltri_ref[...], _bf(c8l))
        slot = _dot(_bf(oh), tri_ref[...]) + loff[:, 0:1]
        loc_ref[:, cols] = jnp.concatenate(
            [jnp.sum(jnp.where(hit[:, cols], slot, 0.0), axis=0, keepdims=True) for hit in hits],
            axis=0).astype(I32)
        c8_ref[sub] = c8l
        loff_ref[sub] = loff
        run0_ref[sub] = run_s[...]
        run_s[...] = run_s[...] + c8
    tot_ref[...] = run_s[...]


def _route(u2, w_router, router_bias):
    n, d = u2.shape
    tt = min(MOE_TILE, n)
    t = min(ROUTE_TILES * tt, n)
    nt = n // tt
    wt = w_router.T
    wrh = wt.astype(BF16)
    wrl = (wt - wrh.astype(F32)).astype(BF16)
    tri = (jnp.arange(tt)[:, None] < jnp.arange(tt)[None, :]).astype(BF16)
    ex = jnp.arange(N_EXPERTS)
    ltri = (ex[None, :] < ex[:, None]).astype(BF16)
    full = lambda a: pl.BlockSpec(a.shape, lambda i: (0,) * a.ndim)
    col = pl.BlockSpec((TOP_K, t), lambda i: (0, i))
    tab = pl.BlockSpec((t // tt, N_EXPERTS, LANES), lambda i: (i, 0, 0))
    tab_sds = jax.ShapeDtypeStruct((nt, N_EXPERTS, LANES), F32)
    rb = router_bias.reshape(N_EXPERTS, 1)
    return pl.pallas_call(
        functools.partial(_route_kernel, t=t, tt=tt),
        out_shape=(jax.ShapeDtypeStruct((TOP_K, n), F32), jax.ShapeDtypeStruct((TOP_K, n), I32),
                   tab_sds, tab_sds, tab_sds, jax.ShapeDtypeStruct((N_EXPERTS, LANES), F32)),
        grid=(n // t,),
        in_specs=[pl.BlockSpec((t, d), lambda i: (i, 0)), full(wrh), full(wrl), full(rb), full(tri), full(ltri)],
        out_specs=(col, col, tab, tab, tab, pl.BlockSpec((N_EXPERTS, LANES), lambda i: (0, 0))),
        scratch_shapes=[pltpu.VMEM((N_EXPERTS, LANES), F32)],
        compiler_params=_cparams(("arbitrary",)),
        name="route",
    )(u2, wrh, wrl, rb, tri, ltri)


RUN_BITS = tuple(1 << b for b in reversed(range((MOE_TILE // SUBLANES).bit_length())))


def _for_each_run_piece(n8_ref, src_ref, dst_ref, tile, bits, fn):
    def per_expert(e, _):
        idx = tile * N_EXPERTS + e
        n8 = n8_ref[idx]
        src = src_ref[idx]
        dst = dst_ref[idx]
        for p in bits:
            off = (n8 & ~(2 * p - 1)) * SUBLANES

            @pl.when((n8 & p) != 0)
            def _(p=p, off=off):
                fn(pl.multiple_of(src + off, SUBLANES), pl.multiple_of(dst + off, SUBLANES), p * SUBLANES)
        return 0

    lax.fori_loop(0, N_EXPERTS, per_expert, 0)


def _issue_runs_inline(n8_ref, src_ref, dst_ref, tile, enable, fn, experts=range(N_EXPERTS)):
    for e in experts:
        idx = tile * N_EXPERTS + e
        n8 = jnp.where(enable, n8_ref[idx], 0)
        src = src_ref[idx]
        dst = dst_ref[idx]
        for p in RUN_BITS:
            off = (n8 & ~(2 * p - 1)) * SUBLANES

            @pl.when((n8 & p) != 0)
            def _(p=p, off=off, src=src, dst=dst):
                fn(pl.multiple_of(src + off, SUBLANES), pl.multiple_of(dst + off, SUBLANES), p * SUBLANES)


def _wait_rows(n8, make_copy, max_rows):
    for p in tuple(1 << b for b in reversed(range((max_rows // SUBLANES).bit_length()))):
        @pl.when((n8 & p) != 0)
        def _(p=p):
            make_copy(p * SUBLANES).wait()


def _dispatch_kernel(n8_ref, src_ref, dst_ref, tot_ref, zn8_ref, zdst_ref, u_ref, loc_ref, gate_ref, xs_hbm,
                     lbuf0, lbuf1, lbuf2, zx, sems, *, tt, nslot, dh):
    i = pl.program_id(0)
    last = pl.num_programs(0) - 1
    bufs = (lbuf0, lbuf1, lbuf2)
    nbuf = len(bufs)
    zsem = nbuf

    def copy_from(slot):
        def piece(s0, d0, rows):
            pltpu.make_async_copy(bufs[slot].at[pl.ds(s0, rows)], xs_hbm.at[pl.ds(d0, rows)], sems.at[slot]).start()
        return piece

    def wait_tile(tile, slot):
        _wait_rows(tot_ref[tile], lambda rows: pltpu.make_async_copy(
            bufs[slot].at[pl.ds(0, rows)], xs_hbm.at[pl.ds(0, rows)], sems.at[slot]), nslot)

    @pl.when(i == 0)
    def _():
        zx[...] = jnp.zeros(zx.shape, I32)

        def zero_piece(s0, d0, rows):
            cx = pltpu.make_async_copy(zx.at[pl.ds(0, rows)], xs_hbm.at[pl.ds(d0, rows)], sems.at[zsem])
            cx.start()
            cx.wait()

        zbits = tuple(1 << b for b in reversed(range((EXPERT_ROWS // SUBLANES - 1).bit_length())))
        _for_each_run_piece(zn8_ref, zdst_ref, zdst_ref, 0, zbits, zero_piece)

    def build(buf):
        ub = _bf(u_ref[...])
        ones = jnp.ones((tt, LANES), BF16)
        loc = loc_ref[...]
        gate = gate_ref[...]
        rows_b = lax.broadcasted_iota(I32, (SLOT_CHUNK, tt), 0).astype(F32).astype(BF16)
        loc_hi = lax.shift_right_logical(loc, SLOT_CHUNK.bit_length() - 1)
        loc_lo = (loc & (SLOT_CHUNK - 1)).astype(F32)
        gate_h = _bf(gate)
        gate_l = _bf(gate - gate_h.astype(F32))
        one_b = jnp.ones((SLOT_CHUNK, tt), BF16)
        for c in range(nslot // SLOT_CHUNK):
            perm = jnp.zeros((SLOT_CHUNK, tt), BF16)
            pgh = jnp.zeros((SLOT_CHUNK, tt), BF16)
            pgl = jnp.zeros((SLOT_CHUNK, tt), BF16)
            for k in range(TOP_K):
                lk = _bf(jnp.where(loc_hi[k:k + 1, :] == c, loc_lo[k:k + 1, :], -1.0))
                eq = rows_b == lk
                perm = jnp.where(eq, one_b, perm)
                pgh = jnp.where(eq, jnp.broadcast_to(gate_h[k:k + 1, :], (SLOT_CHUNK, tt)), pgh)
                pgl = jnp.where(eq, jnp.broadcast_to(gate_l[k:k + 1, :], (SLOT_CHUNK, tt)), pgl)
            cs = slice(c * SLOT_CHUNK, (c + 1) * SLOT_CHUNK)
            xp = lax.bitcast_convert_type(_dot(perm, ub), I32)
            buf[cs, 0:dh] = xp[:, 0:dh] | lax.shift_right_logical(xp[:, dh:2 * dh], 16)
            buf[cs, dh:dh + LANES] = lax.bitcast_convert_type(_dot(pgh, ones) + _dot(pgl, ones), I32)

    def step(cur):
        prv, prv2 = (cur - 1) % nbuf, (cur - 2) % nbuf

        @pl.when(i >= nbuf)
        def _():
            wait_tile(i - nbuf, cur)

        _issue_runs_inline(n8_ref, src_ref, dst_ref, jnp.maximum(i - 1, 0), i >= 1, copy_from(prv))
        build(bufs[cur])

        @pl.when(i == last)
        def _():
            _for_each_run_piece(n8_ref, src_ref, dst_ref, i, RUN_BITS, copy_from(cur))

            @pl.when(i >= 2)
            def _():
                wait_tile(i - 2, prv2)

            @pl.when(i >= 1)
            def _():
                wait_tile(i - 1, prv)
            wait_tile(i, cur)

    for cur in range(nbuf):
        pl.when(i % nbuf == cur)(functools.partial(step, cur))


def _dispatch(tabs, ztabs, u2, loc_t, gate_t, n_rows):
    n, d = u2.shape
    tt = min(MOE_TILE, n)
    nslot = TOP_K * tt + N_EXPERTS * SUBLANES
    dh = d // 2
    assert nslot % SLOT_CHUNK == 0 and tt // SUBLANES == RUN_BITS[0]
    col = pl.BlockSpec((TOP_K, tt), lambda i, *_: (0, i))
    return pl.pallas_call(
        functools.partial(_dispatch_kernel, tt=tt, nslot=nslot, dh=dh),
        out_shape=jax.ShapeDtypeStruct((n_rows, dh + LANES), I32),
        grid_spec=pltpu.PrefetchScalarGridSpec(
            num_scalar_prefetch=6, grid=(n // tt,),
            in_specs=[pl.BlockSpec((tt, d), lambda i, *_: (i, 0)), col, col],
            out_specs=pl.BlockSpec(memory_space=pl.ANY),
            scratch_shapes=[pltpu.VMEM((nslot, dh + LANES), I32)] * 3 + [
                pltpu.VMEM((EXPERT_ROWS // 2, dh + LANES), I32), pltpu.SemaphoreType.DMA((4,))]),
        compiler_params=_cparams(("arbitrary",)),
        name="dispatch",
    )(*tabs, *ztabs, u2, loc_t, gate_t)


def _experts_kernel(be_ref, nu_ref, xs_ref, wg_ref, wu_ref, wd_ref, ys_ref, *, d):
    del be_ref
    dh = d // 2

    @pl.when(pl.program_id(0) < nu_ref[0])
    def _():
        w = xs_ref[:, 0:dh]
        xa = _bf(lax.bitcast_convert_type(w & jnp.int32(-65536), F32))
        xb = _bf(lax.bitcast_convert_type(lax.shift_left(w, 16), F32))
        gate = lax.bitcast_convert_type(xs_ref[:, dh:dh + LANES], F32)
        hg = _dot(xa, _bf(wg_ref[0, 0:dh, :])) + _dot(xb, _bf(wg_ref[0, dh:d, :]))
        hu = _dot(xa, _bf(wu_ref[0, 0:dh, :])) + _dot(xb, _bf(wu_ref[0, dh:d, :]))
        y = _dot(_bf(jax.nn.silu(hg) * hu), _bf(wd_ref[0])) * jnp.tile(gate, (1, d // LANES))
        yb = lax.bitcast_convert_type(_bf(y).astype(F32), I32)
        ys_ref[...] = yb[:, 0:dh] | lax.shift_right_logical(yb[:, dh:d], 16)


def _experts(blk_expert, n_used, xs, wg, wu, wd):
    rows, xw = xs.shape
    d = wg.shape[1]
    de = wg.shape[2]
    nblk = rows // EXPERT_ROWS
    blk = lambda i, be, nu: jnp.minimum(i, nu[0] - 1)
    return pl.pallas_call(
        functools.partial(_experts_kernel, d=d),
        out_shape=jax.ShapeDtypeStruct((rows, d // 2), I32),
        grid_spec=pltpu.PrefetchScalarGridSpec(
            num_scalar_prefetch=2, grid=(nblk,),
            in_specs=[pl.BlockSpec((EXPERT_ROWS, xw), lambda i, be, nu: (blk(i, be, nu), 0)),
                      pl.BlockSpec((1, d, de), lambda i, be, nu: (be[blk(i, be, nu)], 0, 0)),
                      pl.BlockSpec((1, d, de), lambda i, be, nu: (be[blk(i, be, nu)], 0, 0)),
                      pl.BlockSpec((1, de, d), lambda i, be, nu: (be[blk(i, be, nu)], 0, 0))],
            out_specs=pl.BlockSpec((EXPERT_ROWS, d // 2), lambda i, be, nu: (blk(i, be, nu), 0))),
        compiler_params=_cparams(("arbitrary",)),
        name="experts",
    )(blk_expert, n_used, xs, wg, wu, wd)


def _combine_kernel(n8_ref, src_ref, dst_ref, tot_ref, ys_hbm, loc_ref, h_ref, u2_ref, ada_ref, wsg_ref, wsu_ref, wsd_ref,
                    gf_ref, o_ref, ybuf0, ybuf1, pick_s, sems, *, tt, nslot, d):
    i = pl.program_id(0)
    last = pl.num_programs(0) - 1
    bufs = (ybuf0, ybuf1)
    dh = d // 2
    nchunk = nslot // SLOT_CHUNK

    def copy_into(slot):
        def piece(s0, d0, rows):
            pltpu.make_async_copy(ys_hbm.at[pl.ds(d0, rows)], bufs[slot].at[pl.ds(s0, rows)], sems.at[slot]).start()
        return piece

    @pl.when(i == 0)
    def _():
        ybuf0[...] = jnp.zeros(ybuf0.shape, I32)
        ybuf1[...] = jnp.zeros(ybuf1.shape, I32)
        _for_each_run_piece(n8_ref, src_ref, dst_ref, 0, RUN_BITS, copy_into(0))

    def step(cur, nxt):
        nxt_tile = jnp.minimum(i + 1, last)
        _issue_runs_inline(n8_ref, src_ref, dst_ref, nxt_tile, i < last, copy_into(nxt), range(N_EXPERTS // 2))
        x = _bf(u2_ref[...])
        shared = _dot(_bf(jax.nn.silu(_dot(x, wsg_ref[...])) * _dot(x, wsu_ref[...])), wsd_ref[...])
        loc = loc_ref[...]
        cols_b = lax.broadcasted_iota(I32, (tt, SLOT_CHUNK), 1).astype(F32).astype(BF16)
        loc_hi = lax.shift_right_logical(loc, SLOT_CHUNK.bit_length() - 1)
        loc_lo = (loc & (SLOT_CHUNK - 1)).astype(F32)
        one_b = jnp.ones((tt, SLOT_CHUNK), BF16)
        for c in range(nchunk):
            pick = jnp.zeros((tt, SLOT_CHUNK), BF16)
            for k in range(TOP_K):
                lk = _bf(jnp.where(loc_hi[:, k:k + 1] == c, loc_lo[:, k:k + 1], -1.0))
                pick = jnp.where(cols_b == lk, one_b, pick)
            pick_s[c] = pick
        _wait_rows(tot_ref[i], lambda rows: pltpu.make_async_copy(
            ys_hbm.at[pl.ds(0, rows)], bufs[cur].at[pl.ds(0, rows)], sems.at[cur]), nslot)
        _issue_runs_inline(n8_ref, src_ref, dst_ref, nxt_tile, i < last, copy_into(nxt),
                           range(N_EXPERTS // 2, N_EXPERTS))
        routed_a = jnp.zeros((tt, dh), F32)
        routed_b = jnp.zeros((tt, dh), F32)
        for c in range(nchunk):
            w = bufs[cur][c * SLOT_CHUNK:(c + 1) * SLOT_CHUNK, :]
            routed_a = routed_a + _dot(pick_s[c], _bf(lax.bitcast_convert_type(w & jnp.int32(-65536), F32)))
            routed_b = routed_b + _dot(pick_s[c], _bf(lax.bitcast_convert_type(lax.shift_left(w, 16), F32)))
        routed = jnp.concatenate([routed_a, routed_b], axis=1)
        gate2 = ada_ref[0, :, 5 * d:6 * d]
        h = h_ref[...] + gate2 * (routed + shared)
        o_ref[...] = _rms(h, gf_ref[...])

    pl.when(i % 2 == 0)(lambda: step(0, 1))
    pl.when(i % 2 == 1)(lambda: step(1, 0))


def _combine(tabs, ys, loc, h1, u2, ada3, wsg, wsu, wsd, gf, seq):
    n, d = h1.shape
    tt = min(MOE_TILE, n)
    nslot = TOP_K * tt + N_EXPERTS * SUBLANES
    per_b = seq // tt
    row = pl.BlockSpec((tt, d), lambda i, *_: (i, 0))
    full = lambda a: pl.BlockSpec(a.shape, lambda i, *_: (0,) * a.ndim)
    wsg, wsu, wsd = wsg.astype(BF16), wsu.astype(BF16), wsd.astype(BF16)
    gf = gf.reshape(1, d)
    return pl.pallas_call(
        functools.partial(_combine_kernel, tt=tt, nslot=nslot, d=d),
        out_shape=jax.ShapeDtypeStruct((n, d), F32),
        grid_spec=pltpu.PrefetchScalarGridSpec(
            num_scalar_prefetch=4, grid=(n // tt,),
            in_specs=[pl.BlockSpec(memory_space=pl.ANY),
                      pl.BlockSpec((tt, TOP_K), lambda i, *_: (i, 0)),
                      row, row,
                      pl.BlockSpec((1, 1, ada3.shape[2]), lambda i, *_: (i // per_b, 0, 0)),
                      full(wsg), full(wsu), full(wsd), full(gf)],
            out_specs=row,
            scratch_shapes=[pltpu.VMEM((nslot, d // 2), I32), pltpu.VMEM((nslot, d // 2), I32),
                            pltpu.VMEM((nslot // SLOT_CHUNK, tt, SLOT_CHUNK), BF16),
                            pltpu.SemaphoreType.DMA((2,))]),
        compiler_params=_cparams(("arbitrary",)),
        name="combine",
    )(*tabs, ys, loc, h1, u2, ada3, wsg, wsu, wsd, gf)


def _moe(h1, u2, ada3, w_router, router_bias, wg, wu, wd, wsg, wsu, wsd, gf):
    bsz, seq, d = h1.shape
    n = bsz * seq
    assert seq % min(MOE_TILE, n) == 0
    h1f = h1.reshape(n, d)
    u2f = u2.reshape(n, d)
    gate_t, loc_t, c8, loff, run0, tot = _route(u2f, w_router, router_bias)
    nt = c8.shape[0]
    as_tab = lambda a: a[:, :, 0].astype(I32)
    tot8 = tot[:, 0].astype(I32)
    padded = (tot8 + EXPERT_ROWS - 1) // EXPERT_ROWS * EXPERT_ROWS
    pend = jnp.cumsum(padded)
    pstart = (pend - padded).astype(I32)
    nblk = (n * TOP_K + nt * N_EXPERTS * (SUBLANES - 1) + N_EXPERTS * (EXPERT_ROWS - 1) + EXPERT_ROWS - 1) // EXPERT_ROWS
    blk_row0 = jnp.arange(nblk, dtype=I32) * EXPERT_ROWS
    blk_expert = jnp.minimum(jnp.sum(pend[None, :] <= blk_row0[:, None], axis=1), N_EXPERTS - 1).astype(I32)
    n_used = (pend[-1:] // EXPERT_ROWS).astype(I32)
    n8 = as_tab(c8) // SUBLANES
    tabs = (n8.reshape(-1), as_tab(loff).reshape(-1), (pstart[None, :] + as_tab(run0)).reshape(-1),
            jnp.sum(n8, axis=1))
    ztabs = ((padded - tot8) // SUBLANES, pstart + tot8)
    xs = _dispatch(tabs, ztabs, u2f, loc_t, gate_t, nblk * EXPERT_ROWS)
    ys = _experts(blk_expert, n_used, xs, wg, wu, wd)
    out = _combine(tabs, ys, loc_t.T, h1f, u2f, ada3, wsg, wsu, wsd, gf, seq)
    return out.reshape(bsz, seq, d)


def kernel(x, c, w_ada, b_ada, norm1_g, w_in, ssm_lambda_re, ssm_lambda_im, ssm_log_dt, ssm_b_re, ssm_b_im,
           ssm_c_re, ssm_c_im, ssm_d, ssm_w_glu, ssm_b_glu, w_proj_ssm, w_proj_attn, w_out, norm2_g, w_router,
           router_bias, w_exp_gate, w_exp_up, w_exp_down, w_sh_gate, w_sh_up, w_sh_down, norm_f_g):
    depth = w_ada.shape[0]
    assert depth == 1, "the final norm is fused into the last (only) layer's combine kernel"
    bsz, seq, d = x.shape
    layer = 0
    ada3 = _ada(c, w_ada[layer], b_ada[layer]).reshape(bsz, 1, 6 * d)
    us, k, ki, qt, qit, vt, wit = _inproj(x, ada3, norm1_g[layer], w_in[layer])
    a_re, a_im, bb_re, bb_im = _s5disc(ssm_lambda_re[layer], ssm_lambda_im[layer], ssm_log_dt[layer],
                                       ssm_b_re[layer], ssm_b_im[layer])
    ys_t = _s5(us, a_re, a_im, bb_re, bb_im, ssm_c_re[layer], ssm_c_im[layer],
               ssm_d[layer], ssm_w_glu[layer], ssm_b_glu[layer])
    ya = _dsa(qt, qit, wit, k, ki, vt)
    h1, u2 = _mix(x, ys_t, ya, ada3, norm1_g[layer], w_in[layer], w_proj_ssm[layer],
                  w_proj_attn[layer], w_out[layer], norm2_g[layer])
    return _moe(h1, u2, ada3, w_router[layer], router_bias[layer], w_exp_gate[layer], w_exp_up[layer],
                w_exp_down[layer], w_sh_gate[layer], w_sh_up[layer], w_sh_down[layer], norm_f_g)
```

```python
import functools

import jax
import jax.numpy as jnp
import numpy as np
from jax import lax
from jax.experimental import pallas as pl
from jax.experimental.pallas import tpu as pltpu

F32 = jnp.float32
BF16 = jnp.bfloat16
I32 = jnp.int32

SSM_GROUP = 16
SSM_STATE = 64
N_HEADS = 8
HEAD_DIM = 64
IDX_HEADS = 8
IDX_DIM = 64
TOPK_MAX = 256
N_EXPERTS = 64
TOP_K = 8
N_GROUPS = 8
TOPK_GROUPS = 4
ROUTED_SCALE = 2.5
EPS = 1e-6
SSM_WIDTH = 512

V7X_VMEM_LIMIT_BYTES = 56 * 1024 * 1024
LANES = 128
SUBLANES = 8

ADA_COLS = 1024
INPROJ_ROWS = 512
S5_STEPS = 64
S5_LANE_CHUNK = 128
DSA_Q_COLS = 256
DSA_K_ROWS = 512
DSA_COUNT_ROWS = 64
BITSLICE_ROWS = 256
POS_SPLIT = 64
MIX_ROWS = 512
MOE_TILE = 256
ROUTE_TILES = 4
SLOT_CHUNK = 256
EXPERT_ROWS = 1024

NEG_BIG = -1e30
INT_MIN = -(2 ** 31)


def _cparams(sem):
    return pltpu.CompilerParams(dimension_semantics=sem, vmem_limit_bytes=V7X_VMEM_LIMIT_BYTES)


def _bf(x):
    return x.astype(BF16)


def _dot(a, b):
    return jnp.dot(a, b, preferred_element_type=F32)


def _dot_nt(a, b):
    return lax.dot_general(a, b, (((1,), (1,)), ((), ())), preferred_element_type=F32)


def _split(x):
    hi = _bf(x)
    lo = _bf(x - hi.astype(F32))
    return hi, lo


def _dot3(a, b):
    ah, al = _split(a)
    bh, bl = _split(b)
    return _dot(ah, bh) + (_dot(ah, bl) + _dot(al, bh))


def _rms(x, g):
    return x * lax.rsqrt(jnp.mean(x * x, axis=-1, keepdims=True) + EPS) * g


def _ada_kernel(c_ref, w_ref, b_ref, o_ref):
    c = c_ref[...]
    o_ref[...] = _dot3(c * jax.nn.sigmoid(c), w_ref[...]) + b_ref[...]


def _ada(c, w, b):
    bsz, d = c.shape
    n = w.shape[1]
    tn = ADA_COLS
    return pl.pallas_call(
        _ada_kernel,
        out_shape=jax.ShapeDtypeStruct((bsz, n), F32),
        grid=(n // tn,),
        in_specs=[pl.BlockSpec((bsz, d), lambda j: (0, 0)),
                  pl.BlockSpec((d, tn), lambda j: (0, j)),
                  pl.BlockSpec((1, tn), lambda j: (0, j))],
        out_specs=pl.BlockSpec((bsz, tn), lambda j: (0, j)),
        compiler_params=_cparams(("arbitrary",)),
        name="ada",
    )(c, w, b.reshape(1, n))


ALIBI_SLOPES = tuple(2.0 ** (-8.0 * (h + 1) / N_HEADS) for h in range(N_HEADS))
QAUG_ROWS = 16


def _inproj_kernel(x_ref, ada_ref, g1_ref, w_ref, wt_ref,
                   us_ref, k_ref, ki_ref, qt_ref, qit_ref, vt_ref, wit_ref, *, d, ssm_w, attn_w, idx_w, tl):
    x = x_ref[0]
    shift = ada_ref[0, :, 0:d]
    scale = ada_ref[0, :, d:2 * d]
    u = _bf(_rms(x, g1_ref[...]) * (1.0 + scale) + shift)
    r = _dot(u, w_ref[...])
    us_ref[0] = r[:, 0:ssm_w]
    pos = pl.program_id(1) * tl + lax.broadcasted_iota(I32, (tl, LANES), 0)
    lane = lax.broadcasted_iota(I32, (tl, LANES), 1)
    pos_cols = jnp.where(lane == HEAD_DIM, pos // POS_SPLIT * POS_SPLIT, jnp.where(lane == HEAD_DIM + 1, pos % POS_SPLIT, 0))
    k_ref[0] = _bf(r[:, ssm_w:ssm_w + LANES] + pos_cols.astype(F32))
    ki_ref[0] = _bf(r[:, ssm_w + LANES:ssm_w + LANES + IDX_DIM])
    rt = _dot_nt(wt_ref[...], u)
    arow = lax.broadcasted_iota(I32, (QAUG_ROWS, tl), 0)
    for h in range(N_HEADS):
        base = h * LANES
        qt_ref[0, base:base + HEAD_DIM, :] = _bf(rt[h * HEAD_DIM:(h + 1) * HEAD_DIM])
        qt_ref[0, base + HEAD_DIM:base + HEAD_DIM + QAUG_ROWS, :] = _bf(jnp.where(arow < 2, ALIBI_SLOPES[h], 0.0))
        qt_ref[0, base + HEAD_DIM + QAUG_ROWS:base + LANES, :] = jnp.zeros((LANES - HEAD_DIM - QAUG_ROWS, tl), BF16)
    qit_ref[0] = _bf(rt[attn_w:attn_w + idx_w])
    vt_ref[0] = _bf(rt[attn_w + idx_w:attn_w + idx_w + HEAD_DIM])
    wit_ref[0] = rt[attn_w + idx_w + HEAD_DIM:attn_w + idx_w + HEAD_DIM + IDX_HEADS]


def _split_w_in(w_in, d):
    sizes = (SSM_WIDTH, N_HEADS * HEAD_DIM, HEAD_DIM, HEAD_DIM, IDX_HEADS * IDX_DIM, IDX_DIM, IDX_HEADS, d, d)
    offs = [0]
    for s in sizes:
        offs.append(offs[-1] + s)
    return [w_in[:, offs[i]:offs[i + 1]] for i in range(9)]


def _inproj(x, ada3, g1, w_in):
    bsz, seq, d = x.shape
    ssm_w = SSM_WIDTH
    attn_w = N_HEADS * HEAD_DIM
    idx_w = IDX_HEADS * IDX_DIM
    w_ssm, w_q, w_k, w_v, w_qi, w_ki, w_wi, _, _ = _split_w_in(w_in, d)
    zpad = lambda n: jnp.zeros((d, n), F32)
    wbig = jnp.concatenate([w_ssm, w_k, zpad(LANES - HEAD_DIM), w_ki, zpad(LANES - IDX_DIM)], axis=1).astype(BF16)
    wt = jnp.concatenate([w_q * (HEAD_DIM ** -0.5), w_qi * (IDX_DIM ** -0.5), w_v, w_wi,
                          zpad(LANES - HEAD_DIM - IDX_HEADS)], axis=1).T.astype(BF16)
    tl = INPROJ_ROWS
    kern = functools.partial(_inproj_kernel, d=d, ssm_w=ssm_w, attn_w=attn_w, idx_w=idx_w, tl=tl)
    row = lambda w: pl.BlockSpec((1, tl, w), lambda b, l: (b, l, 0))
    colt = lambda h: pl.BlockSpec((1, h, tl), lambda b, l: (b, 0, l))
    full = lambda a: pl.BlockSpec(a.shape, lambda b, l: (0,) * a.ndim)
    return pl.pallas_call(
        kern,
        out_shape=(jax.ShapeDtypeStruct((bsz, seq, ssm_w), F32),
                   jax.ShapeDtypeStruct((bsz, seq, LANES), BF16),
                   jax.ShapeDtypeStruct((bsz, seq, IDX_DIM), BF16),
                   jax.ShapeDtypeStruct((bsz, N_HEADS * LANES, seq), BF16),
                   jax.ShapeDtypeStruct((bsz, idx_w, seq), BF16),
                   jax.ShapeDtypeStruct((bsz, HEAD_DIM, seq), BF16),
                   jax.ShapeDtypeStruct((bsz, IDX_HEADS, seq), F32)),
        grid=(bsz, seq // tl),
        in_specs=[row(d),
                  pl.BlockSpec((1, 1, ada3.shape[2]), lambda b, l: (b, 0, 0)),
                  pl.BlockSpec((1, d), lambda b, l: (0, 0)),
                  full(wbig), full(wt)],
        out_specs=(row(ssm_w), row(LANES), row(IDX_DIM),
                   colt(N_HEADS * LANES), colt(idx_w), colt(HEAD_DIM), colt(IDX_HEADS)),
        compiler_params=_cparams(("arbitrary", "arbitrary")),
        name="inproj",
    )(x, ada3, g1.reshape(1, d), wbig, wt)


def _s5disc_kernel(lr_ref, li_ref, ldt_ref, br_ref, bi_ref, are_ref, aim_ref, bbr_ref, bbi_ref):
    lr = lr_ref[...]
    li = li_ref[...]
    dt = jnp.exp(ldt_ref[...])
    mag = jnp.exp(lr * dt)
    a_re = mag * jnp.cos(li * dt)
    a_im = mag * jnp.sin(li * dt)
    den = lr * lr + li * li
    n_re = a_re - 1.0
    f_re = (n_re * lr + a_im * li) / den
    f_im = (a_im * lr - n_re * li) / den
    br = br_ref[...]
    bi = bi_ref[...]
    are_ref[...] = a_re
    aim_ref[...] = a_im
    bbr_ref[...] = f_re * br - f_im * bi
    bbi_ref[...] = f_re * bi + f_im * br


def _s5disc(lam_re, lam_im, log_dt, b_re, b_im):
    g, p = lam_re.shape
    h = b_re.shape[2]
    rep = lambda a: jnp.repeat(a, h, axis=1)
    ldt = jnp.broadcast_to(log_dt[:, None], (g, p * h))
    sds = jax.ShapeDtypeStruct((g, p * h), F32)
    a_re, a_im, bb_re, bb_im = pl.pallas_call(
        _s5disc_kernel, out_shape=(sds, sds, sds, sds), name="s5disc",
    )(rep(lam_re), rep(lam_im), ldt, b_re.reshape(g, p * h), b_im.reshape(g, p * h))
    return a_re[:, ::h], a_im[:, ::h], bb_re.reshape(g, p, h), bb_im.reshape(g, p, h)


def _s5_kernel(u_ref, wb_ref, ar_ref, ai_ref, cc_ref, dsk_ref, wg_ref, bg_ref, o_ref, buf, hst, *, tl, width):
    nch = width // S5_LANE_CHUNK
    sw = S5_LANE_CHUNK // SSM_GROUP * SSM_STATE
    rows = tl * SUBLANES

    @pl.when(pl.program_id(0) == 0)
    def _():
        hst[...] = jnp.zeros_like(hst)

    u = jnp.swapaxes(u_ref[...], 0, 1).reshape(rows, width)
    ub = _bf(u)
    for j in range(nch):
        buf[:, j * 2 * sw:(j + 1) * 2 * sw] = _dot(ub[:, j * S5_LANE_CHUNK:(j + 1) * S5_LANE_CHUNK], wb_ref[j])

    for j in range(nch):
        re_cols = slice(j * 2 * sw, j * 2 * sw + sw)
        im_cols = slice(j * 2 * sw + sw, (j + 1) * 2 * sw)
        a_re = jnp.broadcast_to(ar_ref[:, j * sw:(j + 1) * sw], (SUBLANES, sw))
        a_im = jnp.broadcast_to(ai_ref[:, j * sw:(j + 1) * sw], (SUBLANES, sw))

        def step(t, carry, re_cols=re_cols, im_cols=im_cols, a_re=a_re, a_im=a_im):
            h_re, h_im = carry
            r0 = pl.multiple_of(t * SUBLANES, SUBLANES)
            n_re = (a_re * h_re - a_im * h_im) + buf[pl.ds(r0, SUBLANES), re_cols]
            n_im = (a_re * h_im + a_im * h_re) + buf[pl.ds(r0, SUBLANES), im_cols]
            buf[pl.ds(r0, SUBLANES), re_cols] = n_re
            buf[pl.ds(r0, SUBLANES), im_cols] = n_im
            return n_re, n_im

        h_re, h_im = lax.fori_loop(0, tl, step, (hst[:, re_cols], hst[:, im_cols]), unroll=True)
        hst[:, re_cols] = h_re
        hst[:, im_cols] = h_im

    ys = [_dot(_bf(buf[:, j * 2 * sw:(j + 1) * 2 * sw]), cc_ref[j]) for j in range(nch)]
    y = jnp.concatenate(ys, axis=1) + dsk_ref[...] * u
    y = jax.nn.gelu(y)
    y = y * jax.nn.sigmoid(_dot(_bf(y), wg_ref[...]) + bg_ref[...])
    o_ref[...] = _bf(jnp.swapaxes(y.reshape(tl, SUBLANES, width), 0, 1))


def _s5(u_t, a_re, a_im, bb_re, bb_im, c_re, c_im, d_skip, w_glu, b_glu):
    bsz, seq, width = u_t.shape
    assert bsz == SUBLANES
    nch = width // S5_LANE_CHUNK
    gpc = S5_LANE_CHUNK // SSM_GROUP
    sw = gpc * SSM_STATE
    eye = jnp.eye(gpc, dtype=F32)

    def bmat(bb):
        t = bb.reshape(nch, gpc, SSM_STATE, SSM_GROUP).transpose(0, 1, 3, 2)
        return jnp.einsum('jghp,gk->jghkp', t, eye).reshape(nch, S5_LANE_CHUNK, sw)

    def cmat(cc):
        t = cc.reshape(nch, gpc, SSM_GROUP, SSM_STATE).transpose(0, 1, 3, 2)
        return jnp.einsum('jgph,gk->jgpkh', t, eye).reshape(nch, sw, S5_LANE_CHUNK)

    wb = jnp.concatenate([bmat(bb_re), bmat(bb_im)], axis=2)
    cc = jnp.concatenate([cmat(c_re), -cmat(c_im)], axis=1)
    tl = S5_STEPS
    full = lambda a: pl.BlockSpec(a.shape, lambda i: (0,) * a.ndim)
    args = (u_t, wb.astype(BF16), a_re.reshape(1, -1), a_im.reshape(1, -1), cc.astype(BF16),
            d_skip.reshape(1, width), w_glu.astype(BF16), b_glu.reshape(1, width))
    return pl.pallas_call(
        functools.partial(_s5_kernel, tl=tl, width=width),
        out_shape=jax.ShapeDtypeStruct((bsz, seq, width), BF16),
        grid=(seq // tl,),
        in_specs=[pl.BlockSpec((bsz, tl, width), lambda i: (0, i, 0))] + [full(a) for a in args[1:]],
        out_specs=pl.BlockSpec((bsz, tl, width), lambda i: (0, i, 0)),
        scratch_shapes=[pltpu.VMEM((tl * SUBLANES, nch * 2 * sw), F32),
                        pltpu.VMEM((SUBLANES, nch * 2 * sw), F32)],
        compiler_params=_cparams(("arbitrary",)),
        name="s5",
    )(*args)


def _bit_transpose32(words):
    x = list(words)
    j, m = 16, 0x0000FFFF
    while j:
        k = 0
        while k < 32:
            t = (x[k] ^ lax.shift_right_logical(x[k + j], jnp.int32(j))) & jnp.int32(m - (1 << 32) if m >= 1 << 31 else m)
            x[k] = x[k] ^ t
            x[k + j] = x[k + j] ^ lax.shift_left(t, jnp.int32(j))
            k = (k + j + 1) & ~j
        j >>= 1
        m = (m ^ (m << j)) & 0xFFFFFFFF
    return x


def _dsa_kernel(qt_ref, qit_ref, wit_ref, ka_ref, ki_ref, vt_ref, o_ref, key_s, mb_s, acc_s, pl_s, p_s, *, tq, tk, topk, seq):
    i = pl.program_id(1)
    q0 = i * tq
    nkt = (q0 + tq + tk - 1) // tk
    ch = DSA_COUNT_ROWS
    krow = lax.broadcasted_iota(I32, (tk, tq), 0)
    qcol = q0 + lax.broadcasted_iota(I32, (tk, tq), 1)
    crow = lax.broadcasted_iota(I32, (ch, tq), 0)

    wb = wit_ref[0] * (IDX_HEADS ** -0.5)

    def score_tile(j, _):
        r0 = pl.multiple_of(j * tk, tk)
        kit = ki_ref[0, pl.ds(r0, tk), :]
        acc = jnp.zeros((tk, tq), F32)
        for h in range(IDX_HEADS):
            s = _dot(kit, qit_ref[0, h * IDX_DIM:(h + 1) * IDX_DIM, :])
            acc = acc + wb[h:h + 1, :] * jnp.maximum(s, 0.0)
        bits = lax.bitcast_convert_type(acc, I32)
        key = jnp.where(bits < 0, bits ^ jnp.int32(0x7FFFFFFF), bits)
        key = jnp.where(acc == 0.0, 0, key)
        key = jnp.where(krow + r0 <= qcol, key, INT_MIN)
        key_s[pl.ds(r0, tk), :] = key
        ukey = key ^ INT_MIN
        for c in range(tk // BITSLICE_ROWS):
            words = [ukey[c * BITSLICE_ROWS + v * SUBLANES:c * BITSLICE_ROWS + (v + 1) * SUBLANES, :]
                     for v in range(32)]
            planes = _bit_transpose32(words)
            g0 = pl.multiple_of((j * (tk // BITSLICE_ROWS) + c) * SUBLANES, SUBLANES)
            for it in range(32):
                pl_s[it, pl.ds(g0, SUBLANES), :] = planes[it]
        return 0

    @pl.when((pl.program_id(0) == 0) & (i == 0))
    def _():
        pl_s[...] = jnp.zeros(pl_s.shape, I32)

    def score_pair(jj, _):
        score_tile(2 * jj, 0)
        score_tile(2 * jj + 1, 0)
        return 0

    lax.fori_loop(0, nkt // 2, score_pair, 0)

    @pl.when(nkt % 2 == 1)
    def _():
        score_tile(nkt - 1, 0)

    def count(pred):
        def tile(j, cnt):
            for c in range(tk // ch):
                rr = pl.multiple_of(j * tk + c * ch, ch)
                cnt = cnt + jnp.where(pred(key_s[pl.ds(rr, ch), :], rr), 1, 0)
            return cnt
        cnt = lax.fori_loop(0, nkt, tile, jnp.zeros((ch, tq), I32))
        return jnp.sum(cnt.astype(F32), axis=0, keepdims=True)

    ngrp = seq // 32

    def lane_count(words):
        pc = lax.population_count(words).reshape(ngrp // SUBLANES, SUBLANES, tq)
        return jnp.sum(jnp.sum(pc, axis=0).astype(F32), axis=0, keepdims=True)

    def bit_step(it, carry):
        alive, above, ans_u = carry
        ones = alive & pl_s[it]
        cnt1 = lane_count(ones)
        take = above + cnt1 >= float(topk)
        alive = jnp.where(take, ones, alive ^ ones)
        above = jnp.where(take, above, above + cnt1)
        ans_u = jnp.where(take, ans_u | lax.shift_left(jnp.int32(1), 31 - it), ans_u)
        return alive, above, ans_u

    grow = lax.broadcasted_iota(I32, (ngrp, tq), 0)
    alive0 = jnp.where(grow < nkt * (tk // 32), -1, 0)
    alive, above, ans_u = lax.fori_loop(
        0, 32, bit_step, (alive0, jnp.zeros((1, tq), F32), jnp.zeros((1, tq), I32)))
    thr = jnp.maximum(ans_u ^ INT_MIN, INT_MIN + 1)
    cnt_ge = above + lane_count(alive)
    tied = jnp.where(ans_u != 0, cnt_ge, 0.0) > float(topk)
    has_ties = jnp.max(jnp.where(tied, 1.0, 0.0)) > 0.0

    def tie_cut():
        need = float(topk) - count(lambda kb, rr: kb > thr)
        nbits = max(1, (seq - 1).bit_length())

        def idx_step(b, x):
            cand = x | lax.shift_left(jnp.int32(1), nbits - 1 - b)
            below = count(lambda kb, rr: jnp.where(kb == thr, crow + rr, seq) < cand)
            return jnp.where(below < need, cand, x)

        x = lax.fori_loop(0, nbits, idx_step, jnp.zeros((1, tq), I32))
        return jnp.where(tied, x, seq)

    cut = lax.cond(has_ties, tie_cut, lambda: jnp.full((1, tq), seq, I32))

    def bias_tile(j, _):
        for c in range(tk // ch):
            rr = pl.multiple_of(j * tk + c * ch, ch)
            kb = key_s[pl.ds(rr, ch), :]
            tie_bias = jnp.where(crow + rr <= cut, 0.0, NEG_BIG)
            mb_s[pl.ds(rr, ch), :] = jnp.where(kb > thr, 0.0, jnp.where(kb == thr, tie_bias, NEG_BIG))
        return 0

    def logits(j, h):
        r0 = pl.multiple_of(j * tk, tk)
        s = _dot(ka_ref[0, pl.ds(r0, tk), :], qt_ref[0, h * LANES:(h + 1) * LANES, :]) + mb_s[pl.ds(r0, tk), :]
        return s.reshape(tk // SUBLANES, SUBLANES, tq)

    acc_s[...] = jnp.zeros(acc_s.shape, F32)

    def attn_tile(j, carry):
        ms, ls = carry
        r0 = pl.multiple_of(j * tk, tk)
        bias_tile(j, 0)
        new_m, new_l, alphas = [], [], []
        for h in range(N_HEADS):
            s = logits(j, h)
            m_new = jnp.maximum(ms[h], jnp.max(jnp.max(s, axis=0), axis=0, keepdims=True))
            alpha = jnp.exp(ms[h] - m_new)
            p = jnp.exp(s - m_new)
            new_m.append(m_new)
            new_l.append(alpha * ls[h] + jnp.sum(p, axis=0))
            alphas.append(alpha)
            p_s[h] = _bf(p.reshape(tk, tq))
        for h in range(N_HEADS):
            rows = slice(h * HEAD_DIM, (h + 1) * HEAD_DIM)
            acc_s[rows, :] = alphas[h] * acc_s[rows, :] + _dot(vt_ref[0, :, pl.ds(r0, tk)], p_s[h])
        return tuple(new_m), tuple(new_l)

    init = ((jnp.full((1, tq), NEG_BIG, F32),) * N_HEADS, (jnp.zeros((SUBLANES, tq), F32),) * N_HEADS)
    _, ls = lax.fori_loop(0, nkt, attn_tile, init)
    for h in range(N_HEADS):
        rows = slice(h * HEAD_DIM, (h + 1) * HEAD_DIM)
        acc_s[rows, :] = acc_s[rows, :] / jnp.sum(ls[h], axis=0, keepdims=True)
    o_ref[0] = _bf(acc_s[...].T)


def _dsa(qt, qit, wit, ka, ki, vt):
    bsz, seq = ka.shape[0], ka.shape[1]
    aw = N_HEADS * HEAD_DIM
    tq = min(DSA_Q_COLS, seq)
    tk = min(DSA_K_ROWS, seq)
    topk = min(TOPK_MAX, seq // 4)
    assert (seq - 1) // POS_SPLIT < 256 and POS_SPLIT <= 256, "key positions must split into two bf16-exact parts"
    assert all(float(np.float32(sl).astype(BF16)) == sl for sl in ALIBI_SLOPES), "ALiBi slopes must be bf16-exact"
    kern = functools.partial(_dsa_kernel, tq=tq, tk=tk, topk=topk, seq=seq)
    cols = lambda r: pl.BlockSpec((1, r, tq), lambda b, i: (b, 0, i))
    return pl.pallas_call(
        kern,
        out_shape=jax.ShapeDtypeStruct((bsz, seq, aw), BF16),
        grid=(bsz, seq // tq),
        in_specs=[cols(N_HEADS * LANES), cols(qit.shape[1]), cols(IDX_HEADS),
                  pl.BlockSpec((1, seq, LANES), lambda b, i: (b, 0, 0)),
                  pl.BlockSpec((1, seq, IDX_DIM), lambda b, i: (b, 0, 0)),
                  pl.BlockSpec((1, HEAD_DIM, seq), lambda b, i: (b, 0, 0))],
        out_specs=pl.BlockSpec((1, tq, aw), lambda b, i: (b, i, 0)),
        scratch_shapes=[pltpu.VMEM((seq, tq), I32), pltpu.VMEM((seq, tq), F32), pltpu.VMEM((aw, tq), F32),
                        pltpu.VMEM((32, seq // 32, tq), I32), pltpu.VMEM((N_HEADS, tk, tq), BF16)],
        compiler_params=_cparams(("arbitrary", "arbitrary")),
        name="dsa",
    )(qt, qit, wit, ka, ki, vt)


def _mix_kernel(x_ref, ys_ref, ya_ref, ada_ref, g1_ref, wgt_ref, wps_ref, wpa_ref, wo_ref, g2_ref,
                h_ref, u2_ref, *, d):
    gate1 = ada_ref[0, :, 2 * d:3 * d]
    shift2 = ada_ref[0, :, 3 * d:4 * d]
    scale2 = ada_ref[0, :, 4 * d:5 * d]
    x = x_ref[0]
    u = _bf(_rms(x, g1_ref[...]) * (1.0 + ada_ref[0, :, d:2 * d]) + ada_ref[0, :, 0:d])
    g = _dot(u, wgt_ref[...])
    mixed = (jax.nn.sigmoid(g[:, 0:d]) * _dot(ys_ref[0], wps_ref[...])
             + jax.nn.sigmoid(g[:, d:2 * d]) * _dot(ya_ref[0], wpa_ref[...]))
    h = x + gate1 * _dot(_bf(mixed), wo_ref[...])
    h_ref[0] = h
    u2_ref[0] = _rms(h, g2_ref[...]) * (1.0 + scale2) + shift2


def _mix(x, ys, ya, ada3, g1, w_in, wps, wpa, wo, g2):
    bsz, seq, d = x.shape
    tm = MIX_ROWS
    row = lambda w: pl.BlockSpec((1, tm, w), lambda b, l: (b, l, 0))
    full = lambda a: pl.BlockSpec(a.shape, lambda b, l: (0,) * a.ndim)
    wps, wpa, wo = wps.astype(BF16), wpa.astype(BF16), wo.astype(BF16)
    wgt = jnp.concatenate(_split_w_in(w_in, d)[7:9], axis=1).astype(BF16)
    g1 = g1.reshape(1, d)
    g2 = g2.reshape(1, d)
    return pl.pallas_call(
        functools.partial(_mix_kernel, d=d),
        out_shape=(jax.ShapeDtypeStruct((bsz, seq, d), F32), jax.ShapeDtypeStruct((bsz, seq, d), F32)),
        grid=(bsz, seq // tm),
        in_specs=[row(d), row(ys.shape[2]), row(ya.shape[2]),
                  pl.BlockSpec((1, 1, ada3.shape[2]), lambda b, l: (b, 0, 0)),
                  full(g1), full(wgt), full(wps), full(wpa), full(wo), full(g2)],
        out_specs=(row(d), row(d)),
        compiler_params=_cparams(("arbitrary", "arbitrary")),
        name="mix",
    )(x, ys, ya, ada3, g1, wgt, wps, wpa, wo, g2)


def _first_max(cur, idx, axis, big):
    m = jnp.max(cur, axis=axis, keepdims=True)
    first = jnp.min(jnp.where(cur == m, idx, big), axis=axis, keepdims=True)
    return m, idx == first


def _route_kernel(u_ref, wrh_ref, wrl_ref, rb_ref, tri_ref, ltri_ref,
                  gt_ref, loc_ref, c8_ref, loff_ref, run0_ref, tot_ref, run_s, *, t, tt):
    @pl.when(pl.program_id(0) == 0)
    def _():
        run_s[...] = jnp.zeros_like(run_s)

    uh, ul = _split(u_ref[...])
    logits = _dot_nt(wrh_ref[...], uh) + (_dot_nt(wrl_ref[...], uh) + _dot_nt(wrh_ref[...], ul))
    scores = jax.nn.sigmoid(logits)
    biased = scores + rb_ref[...]
    per_group = N_EXPERTS // N_GROUPS
    b3 = biased.reshape(N_GROUPS, per_group, t)
    i3 = lax.broadcasted_iota(I32, b3.shape, 1)
    m1, hit1 = _first_max(b3, i3, 1, per_group)
    m2 = jnp.max(jnp.where(hit1, -jnp.inf, b3), axis=1, keepdims=True)
    gs = (m1 + m2).reshape(N_GROUPS, t)
    gi = lax.broadcasted_iota(I32, gs.shape, 0)
    gsel = jnp.zeros(gs.shape, F32)
    for _ in range(TOPK_GROUPS):
        _, hit = _first_max(gs, gi, 0, N_GROUPS)
        gsel = jnp.where(hit, 1.0, gsel)
        gs = jnp.where(hit, -jnp.inf, gs)
    cur = jnp.where(gsel.reshape(N_GROUPS, 1, t) > 0.0, b3, -jnp.inf).reshape(N_EXPERTS, t)
    ei = lax.broadcasted_iota(I32, cur.shape, 0)
    hits = []
    gates = []
    for _ in range(TOP_K):
        _, hit = _first_max(cur, ei, 0, N_EXPERTS)
        hits.append(hit)
        gates.append(jnp.sum(jnp.where(hit, scores, 0.0), axis=0, keepdims=True))
        cur = jnp.where(hit, -jnp.inf, cur)
    gate = jnp.concatenate(gates, axis=0)
    gt_ref[...] = gate / jnp.sum(gate, axis=0, keepdims=True) * ROUTED_SCALE
    onehot = jnp.zeros(cur.shape, F32)
    for hit in hits:
        onehot = jnp.where(hit, 1.0, onehot)
    for sub in range(t // tt):
        cols = slice(sub * tt, (sub + 1) * tt)
        oh = onehot[:, cols]
        cnt = jnp.sum(oh, axis=1, keepdims=True)
        c8 = jnp.floor((cnt + (SUBLANES - 1)) * (1.0 / SUBLANES)) * SUBLANES
        c8l = jnp.broadcast_to(c8, (N_EXPERTS, LANES))
        loff = _dot(ltri_ref[...], _bf(c8l))
        slot = _dot(_bf(oh), tri_ref[...]) + loff[:, 0:1]
        loc_ref[:, cols] = jnp.concatenate(
            [jnp.sum(jnp.where(hit[:, cols], slot, 0.0), axis=0, keepdims=True) for hit in hits],
            axis=0).astype(I32)
        c8_ref[sub] = c8l
        loff_ref[sub] = loff
        run0_ref[sub] = run_s[...]
        run_s[...] = run_s[...] + c8
    tot_ref[...] = run_s[...]


def _route(u2, w_router, router_bias):
    n, d = u2.shape
    tt = min(MOE_TILE, n)
    t = min(ROUTE_TILES * tt, n)
    nt = n // tt
    wt = w_router.T
    wrh = wt.astype(BF16)
    wrl = (wt - wrh.astype(F32)).astype(BF16)
    tri = (jnp.arange(tt)[:, None] < jnp.arange(tt)[None, :]).astype(BF16)
    ex = jnp.arange(N_EXPERTS)
    ltri = (ex[None, :] < ex[:, None]).astype(BF16)
    full = lambda a: pl.BlockSpec(a.shape, lambda i: (0,) * a.ndim)
    col = pl.BlockSpec((TOP_K, t), lambda i: (0, i))
    tab = pl.BlockSpec((t // tt, N_EXPERTS, LANES), lambda i: (i, 0, 0))
    tab_sds = jax.ShapeDtypeStruct((nt, N_EXPERTS, LANES), F32)
    rb = router_bias.reshape(N_EXPERTS, 1)
    return pl.pallas_call(
        functools.partial(_route_kernel, t=t, tt=tt),
        out_shape=(jax.ShapeDtypeStruct((TOP_K, n), F32), jax.ShapeDtypeStruct((TOP_K, n), I32),
                   tab_sds, tab_sds, tab_sds, jax.ShapeDtypeStruct((N_EXPERTS, LANES), F32)),
        grid=(n // t,),
        in_specs=[pl.BlockSpec((t, d), lambda i: (i, 0)), full(wrh), full(wrl), full(rb), full(tri), full(ltri)],
        out_specs=(col, col, tab, tab, tab, pl.BlockSpec((N_EXPERTS, LANES), lambda i: (0, 0))),
        scratch_shapes=[pltpu.VMEM((N_EXPERTS, LANES), F32)],
        compiler_params=_cparams(("arbitrary",)),
        name="route",
    )(u2, wrh, wrl, rb, tri, ltri)


RUN_BITS = tuple(1 << b for b in reversed(range((MOE_TILE // SUBLANES).bit_length())))


def _for_each_run_piece(n8_ref, src_ref, dst_ref, tile, bits, fn):
    def per_expert(e, _):
        idx = tile * N_EXPERTS + e
        n8 = n8_ref[idx]
        src = src_ref[idx]
        dst = dst_ref[idx]
        for p in bits:
            off = (n8 & ~(2 * p - 1)) * SUBLANES

            @pl.when((n8 & p) != 0)
            def _(p=p, off=off):
                fn(pl.multiple_of(src + off, SUBLANES), pl.multiple_of(dst + off, SUBLANES), p * SUBLANES)
        return 0

    lax.fori_loop(0, N_EXPERTS, per_expert, 0)


def _issue_runs_inline(n8_ref, src_ref, dst_ref, tile, enable, fn, experts=range(N_EXPERTS)):
    for e in experts:
        idx = tile * N_EXPERTS + e
        n8 = jnp.where(enable, n8_ref[idx], 0)
        src = src_ref[idx]
        dst = dst_ref[idx]
        for p in RUN_BITS:
            off = (n8 & ~(2 * p - 1)) * SUBLANES

            @pl.when((n8 & p) != 0)
            def _(p=p, off=off, src=src, dst=dst):
                fn(pl.multiple_of(src + off, SUBLANES), pl.multiple_of(dst + off, SUBLANES), p * SUBLANES)


def _wait_rows(n8, make_copy, max_rows):
    for p in tuple(1 << b for b in reversed(range((max_rows // SUBLANES).bit_length()))):
        @pl.when((n8 & p) != 0)
        def _(p=p):
            make_copy(p * SUBLANES).wait()


def _dispatch_kernel(n8_ref, src_ref, dst_ref, tot_ref, zn8_ref, zdst_ref, u_ref, loc_ref, gate_ref, xs_hbm,
                     lbuf0, lbuf1, lbuf2, zx, sems, *, tt, nslot, dh):
    i = pl.program_id(0)
    last = pl.num_programs(0) - 1
    bufs = (lbuf0, lbuf1, lbuf2)
    nbuf = len(bufs)
    zsem = nbuf

    def copy_from(slot):
        def piece(s0, d0, rows):
            pltpu.make_async_copy(bufs[slot].at[pl.ds(s0, rows)], xs_hbm.at[pl.ds(d0, rows)], sems.at[slot]).start()
        return piece

    def wait_tile(tile, slot):
        _wait_rows(tot_ref[tile], lambda rows: pltpu.make_async_copy(
            bufs[slot].at[pl.ds(0, rows)], xs_hbm.at[pl.ds(0, rows)], sems.at[slot]), nslot)

    @pl.when(i == 0)
    def _():
        zx[...] = jnp.zeros(zx.shape, I32)

        def zero_piece(s0, d0, rows):
            cx = pltpu.make_async_copy(zx.at[pl.ds(0, rows)], xs_hbm.at[pl.ds(d0, rows)], sems.at[zsem])
            cx.start()
            cx.wait()

        zbits = tuple(1 << b for b in reversed(range((EXPERT_ROWS // SUBLANES - 1).bit_length())))
        _for_each_run_piece(zn8_ref, zdst_ref, zdst_ref, 0, zbits, zero_piece)

    def build(buf):
        ub = _bf(u_ref[...])
        ones = jnp.ones((tt, LANES), BF16)
        loc = loc_ref[...]
        gate = gate_ref[...]
        rows_b = lax.broadcasted_iota(I32, (SLOT_CHUNK, tt), 0).astype(F32).astype(BF16)
        loc_hi = lax.shift_right_logical(loc, SLOT_CHUNK.bit_length() - 1)
        loc_lo = (loc & (SLOT_CHUNK - 1)).astype(F32)
        gate_h = _bf(gate)
        gate_l = _bf(gate - gate_h.astype(F32))
        one_b = jnp.ones((SLOT_CHUNK, tt), BF16)
        for c in range(nslot // SLOT_CHUNK):
            perm = jnp.zeros((SLOT_CHUNK, tt), BF16)
            pgh = jnp.zeros((SLOT_CHUNK, tt), BF16)
            pgl = jnp.zeros((SLOT_CHUNK, tt), BF16)
            for k in range(TOP_K):
                lk = _bf(jnp.where(loc_hi[k:k + 1, :] == c, loc_lo[k:k + 1, :], -1.0))
                eq = rows_b == lk
                perm = jnp.where(eq, one_b, perm)
                pgh = jnp.where(eq, jnp.broadcast_to(gate_h[k:k + 1, :], (SLOT_CHUNK, tt)), pgh)
                pgl = jnp.where(eq, jnp.broadcast_to(gate_l[k:k + 1, :], (SLOT_CHUNK, tt)), pgl)
            cs = slice(c * SLOT_CHUNK, (c + 1) * SLOT_CHUNK)
            xp = lax.bitcast_convert_type(_dot(perm, ub), I32)
            buf[cs, 0:dh] = xp[:, 0:dh] | lax.shift_right_logical(xp[:, dh:2 * dh], 16)
            buf[cs, dh:dh + LANES] = lax.bitcast_convert_type(_dot(pgh, ones) + _dot(pgl, ones), I32)

    def step(cur):
        prv, prv2 = (cur - 1) % nbuf, (cur - 2) % nbuf

        @pl.when(i >= nbuf)
        def _():
            wait_tile(i - nbuf, cur)

        _issue_runs_inline(n8_ref, src_ref, dst_ref, jnp.maximum(i - 1, 0), i >= 1, copy_from(prv))
        build(bufs[cur])

        @pl.when(i == last)
        def _():
            _for_each_run_piece(n8_ref, src_ref, dst_ref, i, RUN_BITS, copy_from(cur))

            @pl.when(i >= 2)
            def _():
                wait_tile(i - 2, prv2)

            @pl.when(i >= 1)
            def _():
                wait_tile(i - 1, prv)
            wait_tile(i, cur)

    for cur in range(nbuf):
        pl.when(i % nbuf == cur)(functools.partial(step, cur))


def _dispatch(tabs, ztabs, u2, loc_t, gate_t, n_rows):
    n, d = u2.shape
    tt = min(MOE_TILE, n)
    nslot = TOP_K * tt + N_EXPERTS * SUBLANES
    dh = d // 2
    assert nslot % SLOT_CHUNK == 0 and tt // SUBLANES == RUN_BITS[0]
    col = pl.BlockSpec((TOP_K, tt), lambda i, *_: (0, i))
    return pl.pallas_call(
        functools.partial(_dispatch_kernel, tt=tt, nslot=nslot, dh=dh),
        out_shape=jax.ShapeDtypeStruct((n_rows, dh + LANES), I32),
        grid_spec=pltpu.PrefetchScalarGridSpec(
            num_scalar_prefetch=6, grid=(n // tt,),
            in_specs=[pl.BlockSpec((tt, d), lambda i, *_: (i, 0)), col, col],
            out_specs=pl.BlockSpec(memory_space=pl.ANY),
            scratch_shapes=[pltpu.VMEM((nslot, dh + LANES), I32)] * 3 + [
                pltpu.VMEM((EXPERT_ROWS // 2, dh + LANES), I32), pltpu.SemaphoreType.DMA((4,))]),
        compiler_params=_cparams(("arbitrary",)),
        name="dispatch",
    )(*tabs, *ztabs, u2, loc_t, gate_t)


def _experts_kernel(be_ref, nu_ref, xs_ref, wg_ref, wu_ref, wd_ref, ys_ref, *, d):
    del be_ref
    dh = d // 2

    @pl.when(pl.program_id(0) < nu_ref[0])
    def _():
        w = xs_ref[:, 0:dh]
        xa = _bf(lax.bitcast_convert_type(w & jnp.int32(-65536), F32))
        xb = _bf(lax.bitcast_convert_type(lax.shift_left(w, 16), F32))
        gate = lax.bitcast_convert_type(xs_ref[:, dh:dh + LANES], F32)
        hg = _dot(xa, _bf(wg_ref[0, 0:dh, :])) + _dot(xb, _bf(wg_ref[0, dh:d, :]))
        hu = _dot(xa, _bf(wu_ref[0, 0:dh, :])) + _dot(xb, _bf(wu_ref[0, dh:d, :]))
        y = _dot(_bf(jax.nn.silu(hg) * hu), _bf(wd_ref[0])) * jnp.tile(gate, (1, d // LANES))
        yb = lax.bitcast_convert_type(_bf(y).astype(F32), I32)
        ys_ref[...] = yb[:, 0:dh] | lax.shift_right_logical(yb[:, dh:d], 16)


def _experts(blk_expert, n_used, xs, wg, wu, wd):
    rows, xw = xs.shape
    d = wg.shape[1]
    de = wg.shape[2]
    nblk = rows // EXPERT_ROWS
    blk = lambda i, be, nu: jnp.minimum(i, nu[0] - 1)
    return pl.pallas_call(
        functools.partial(_experts_kernel, d=d),
        out_shape=jax.ShapeDtypeStruct((rows, d // 2), I32),
        grid_spec=pltpu.PrefetchScalarGridSpec(
            num_scalar_prefetch=2, grid=(nblk,),
            in_specs=[pl.BlockSpec((EXPERT_ROWS, xw), lambda i, be, nu: (blk(i, be, nu), 0)),
                      pl.BlockSpec((1, d, de), lambda i, be, nu: (be[blk(i, be, nu)], 0, 0)),
                      pl.BlockSpec((1, d, de), lambda i, be, nu: (be[blk(i, be, nu)], 0, 0)),
                      pl.BlockSpec((1, de, d), lambda i, be, nu: (be[blk(i, be, nu)], 0, 0))],
            out_specs=pl.BlockSpec((EXPERT_ROWS, d // 2), lambda i, be, nu: (blk(i, be, nu), 0))),
        compiler_params=_cparams(("arbitrary",)),
        name="experts",
    )(blk_expert, n_used, xs, wg, wu, wd)


def _combine_kernel(n8_ref, src_ref, dst_ref, tot_ref, ys_hbm, loc_ref, h_ref, u2_ref, ada_ref, wsg_ref, wsu_ref, wsd_ref,
                    gf_ref, o_ref, ybuf0, ybuf1, pick_s, sems, *, tt, nslot, d):
    i = pl.program_id(0)
    last = pl.num_programs(0) - 1
    bufs = (ybuf0, ybuf1)
    dh = d // 2
    nchunk = nslot // SLOT_CHUNK

    def copy_into(slot):
        def piece(s0, d0, rows):
            pltpu.make_async_copy(ys_hbm.at[pl.ds(d0, rows)], bufs[slot].at[pl.ds(s0, rows)], sems.at[slot]).start()
        return piece

    @pl.when(i == 0)
    def _():
        ybuf0[...] = jnp.zeros(ybuf0.shape, I32)
        ybuf1[...] = jnp.zeros(ybuf1.shape, I32)
        _for_each_run_piece(n8_ref, src_ref, dst_ref, 0, RUN_BITS, copy_into(0))

    def step(cur, nxt):
        nxt_tile = jnp.minimum(i + 1, last)
        _issue_runs_inline(n8_ref, src_ref, dst_ref, nxt_tile, i < last, copy_into(nxt), range(N_EXPERTS // 2))
        x = _bf(u2_ref[...])
        shared = _dot(_bf(jax.nn.silu(_dot(x, wsg_ref[...])) * _dot(x, wsu_ref[...])), wsd_ref[...])
        loc = loc_ref[...]
        cols_b = lax.broadcasted_iota(I32, (tt, SLOT_CHUNK), 1).astype(F32).astype(BF16)
        loc_hi = lax.shift_right_logical(loc, SLOT_CHUNK.bit_length() - 1)
        loc_lo = (loc & (SLOT_CHUNK - 1)).astype(F32)
        one_b = jnp.ones((tt, SLOT_CHUNK), BF16)
        for c in range(nchunk):
            pick = jnp.zeros((tt, SLOT_CHUNK), BF16)
            for k in range(TOP_K):
                lk = _bf(jnp.where(loc_hi[:, k:k + 1] == c, loc_lo[:, k:k + 1], -1.0))
                pick = jnp.where(cols_b == lk, one_b, pick)
            pick_s[c] = pick
        _wait_rows(tot_ref[i], lambda rows: pltpu.make_async_copy(
            ys_hbm.at[pl.ds(0, rows)], bufs[cur].at[pl.ds(0, rows)], sems.at[cur]), nslot)
        _issue_runs_inline(n8_ref, src_ref, dst_ref, nxt_tile, i < last, copy_into(nxt),
                           range(N_EXPERTS // 2, N_EXPERTS))
        routed_a = jnp.zeros((tt, dh), F32)
        routed_b = jnp.zeros((tt, dh), F32)
        for c in range(nchunk):
            w = bufs[cur][c * SLOT_CHUNK:(c + 1) * SLOT_CHUNK, :]
            routed_a = routed_a + _dot(pick_s[c], _bf(lax.bitcast_convert_type(w & jnp.int32(-65536), F32)))
            routed_b = routed_b + _dot(pick_s[c], _bf(lax.bitcast_convert_type(lax.shift_left(w, 16), F32)))
        routed = jnp.concatenate([routed_a, routed_b], axis=1)
        gate2 = ada_ref[0, :, 5 * d:6 * d]
        h = h_ref[...] + gate2 * (routed + shared)
        o_ref[...] = _rms(h, gf_ref[...])

    pl.when(i % 2 == 0)(lambda: step(0, 1))
    pl.when(i % 2 == 1)(lambda: step(1, 0))


def _combine(tabs, ys, loc, h1, u2, ada3, wsg, wsu, wsd, gf, seq):
    n, d = h1.shape
    tt = min(MOE_TILE, n)
    nslot = TOP_K * tt + N_EXPERTS * SUBLANES
    per_b = seq // tt
    row = pl.BlockSpec((tt, d), lambda i, *_: (i, 0))
    full = lambda a: pl.BlockSpec(a.shape, lambda i, *_: (0,) * a.ndim)
    wsg, wsu, wsd = wsg.astype(BF16), wsu.astype(BF16), wsd.astype(BF16)
    gf = gf.reshape(1, d)
    return pl.pallas_call(
        functools.partial(_combine_kernel, tt=tt, nslot=nslot, d=d),
        out_shape=jax.ShapeDtypeStruct((n, d), F32),
        grid_spec=pltpu.PrefetchScalarGridSpec(
            num_scalar_prefetch=4, grid=(n // tt,),
            in_specs=[pl.BlockSpec(memory_space=pl.ANY),
                      pl.BlockSpec((tt, TOP_K), lambda i, *_: (i, 0)),
                      row, row,
                      pl.BlockSpec((1, 1, ada3.shape[2]), lambda i, *_: (i // per_b, 0, 0)),
                      full(wsg), full(wsu), full(wsd), full(gf)],
            out_specs=row,
            scratch_shapes=[pltpu.VMEM((nslot, d // 2), I32), pltpu.VMEM((nslot, d // 2), I32),
                            pltpu.VMEM((nslot // SLOT_CHUNK, tt, SLOT_CHUNK), BF16),
                            pltpu.SemaphoreType.DMA((2,))]),
        compiler_params=_cparams(("arbitrary",)),
        name="combine",
    )(*tabs, ys, loc, h1, u2, ada3, wsg, wsu, wsd, gf)


def _moe(h1, u2, ada3, w_router, router_bias, wg, wu, wd, wsg, wsu, wsd, gf):
    bsz, seq, d = h1.shape
    n = bsz * seq
    assert seq % min(MOE_TILE, n) == 0
    h1f = h1.reshape(n, d)
    u2f = u2.reshape(n, d)
    gate_t, loc_t, c8, loff, run0, tot = _route(u2f, w_router, router_bias)
    nt = c8.shape[0]
    as_tab = lambda a: a[:, :, 0].astype(I32)
    tot8 = tot[:, 0].astype(I32)
    padded = (tot8 + EXPERT_ROWS - 1) // EXPERT_ROWS * EXPERT_ROWS
    pend = jnp.cumsum(padded)
    pstart = (pend - padded).astype(I32)
    nblk = (n * TOP_K + nt * N_EXPERTS * (SUBLANES - 1) + N_EXPERTS * (EXPERT_ROWS - 1) + EXPERT_ROWS - 1) // EXPERT_ROWS
    blk_row0 = jnp.arange(nblk, dtype=I32) * EXPERT_ROWS
    blk_expert = jnp.minimum(jnp.sum(pend[None, :] <= blk_row0[:, None], axis=1), N_EXPERTS - 1).astype(I32)
    n_used = (pend[-1:] // EXPERT_ROWS).astype(I32)
    n8 = as_tab(c8) // SUBLANES
    tabs = (n8.reshape(-1), as_tab(loff).reshape(-1), (pstart[None, :] + as_tab(run0)).reshape(-1),
            jnp.sum(n8, axis=1))
    ztabs = ((padded - tot8) // SUBLANES, pstart + tot8)
    xs = _dispatch(tabs, ztabs, u2f, loc_t, gate_t, nblk * EXPERT_ROWS)
    ys = _experts(blk_expert, n_used, xs, wg, wu, wd)
    out = _combine(tabs, ys, loc_t.T, h1f, u2f, ada3, wsg, wsu, wsd, gf, seq)
    return out.reshape(bsz, seq, d)


def kernel(x, c, w_ada, b_ada, norm1_g, w_in, ssm_lambda_re, ssm_lambda_im, ssm_log_dt, ssm_b_re, ssm_b_im,
           ssm_c_re, ssm_c_im, ssm_d, ssm_w_glu, ssm_b_glu, w_proj_ssm, w_proj_attn, w_out, norm2_g, w_router,
           router_bias, w_exp_gate, w_exp_up, w_exp_down, w_sh_gate, w_sh_up, w_sh_down, norm_f_g):
    depth = w_ada.shape[0]
    assert depth == 1, "the final norm is fused into the last (only) layer's combine kernel"
    bsz, seq, d = x.shape
    layer = 0
    ada3 = _ada(c, w_ada[layer], b_ada[layer]).reshape(bsz, 1, 6 * d)
    us, ka, ki, qt, qit, vt, wit = _inproj(x, ada3, norm1_g[layer], w_in[layer])
    a_re, a_im, bb_re, bb_im = _s5disc(ssm_lambda_re[layer], ssm_lambda_im[layer], ssm_log_dt[layer],
                                       ssm_b_re[layer], ssm_b_im[layer])
    ys_t = _s5(us, a_re, a_im, bb_re, bb_im, ssm_c_re[layer], ssm_c_im[layer],
               ssm_d[layer], ssm_w_glu[layer], ssm_b_glu[layer])
    ya = _dsa(qt, qit, wit, ka, ki, vt)
    h1, u2 = _mix(x, ys_t, ya, ada3, norm1_g[layer], w_in[layer], w_proj_ssm[layer],
                  w_proj_attn[layer], w_out[layer], norm2_g[layer])
    return _moe(h1, u2, ada3, w_router[layer], router_bias[layer], w_exp_gate[layer], w_exp_up[layer],
                w_exp_down[layer], w_sh_gate[layer], w_sh_up[layer], w_sh_down[layer], norm_f_g)
```

```python
import functools

import jax
import jax.numpy as jnp
import numpy as np
from jax import lax
from jax.experimental import pallas as pl
from jax.experimental.pallas import tpu as pltpu

F32 = jnp.float32
BF16 = jnp.bfloat16
I32 = jnp.int32

SSM_GROUP = 16
SSM_STATE = 64
N_HEADS = 8
HEAD_DIM = 64
IDX_HEADS = 8
IDX_DIM = 64
TOPK_MAX = 256
N_EXPERTS = 64
TOP_K = 8
N_GROUPS = 8
TOPK_GROUPS = 4
ROUTED_SCALE = 2.5
EPS = 1e-6
SSM_WIDTH = 512

V7X_VMEM_LIMIT_BYTES = 56 * 1024 * 1024
LANES = 128
SUBLANES = 8

ADA_COLS = 1024
INPROJ_ROWS = 512
S5_STEPS = 64
S5_LANE_CHUNK = 128
DSA_Q_COLS = 256
DSA_K_ROWS = 512
DSA_COUNT_ROWS = 64
BITSLICE_ROWS = 256
POS_SPLIT = 64
MIX_ROWS = 512
MOE_TILE = 256
ROUTE_TILES = 4
SLOT_CHUNK = 256
EXPERT_ROWS = 1024
EXPERT_RING = 3

NEG_BIG = -1e30
INT_MIN = -(2 ** 31)


def _cparams(sem):
    return pltpu.CompilerParams(dimension_semantics=sem, vmem_limit_bytes=V7X_VMEM_LIMIT_BYTES)


def _bf(x):
    return x.astype(BF16)


def _dot(a, b):
    return jnp.dot(a, b, preferred_element_type=F32)


def _dot_nt(a, b):
    return lax.dot_general(a, b, (((1,), (1,)), ((), ())), preferred_element_type=F32)


def _split(x):
    hi = _bf(x)
    lo = _bf(x - hi.astype(F32))
    return hi, lo


def _dot3(a, b):
    ah, al = _split(a)
    bh, bl = _split(b)
    return _dot(ah, bh) + (_dot(ah, bl) + _dot(al, bh))


def _rms(x, g):
    return x * lax.rsqrt(jnp.mean(x * x, axis=-1, keepdims=True) + EPS) * g


def _ada_kernel(c_ref, w_ref, b_ref, o_ref):
    c = c_ref[...]
    o_ref[...] = _dot3(c * jax.nn.sigmoid(c), w_ref[...]) + b_ref[...]


def _ada(c, w, b):
    bsz, d = c.shape
    n = w.shape[1]
    tn = ADA_COLS
    return pl.pallas_call(
        _ada_kernel,
        out_shape=jax.ShapeDtypeStruct((bsz, n), F32),
        grid=(n // tn,),
        in_specs=[pl.BlockSpec((bsz, d), lambda j: (0, 0)),
                  pl.BlockSpec((d, tn), lambda j: (0, j)),
                  pl.BlockSpec((1, tn), lambda j: (0, j))],
        out_specs=pl.BlockSpec((bsz, tn), lambda j: (0, j)),
        compiler_params=_cparams(("arbitrary",)),
        name="ada",
    )(c, w, b.reshape(1, n))


ALIBI_SLOPES = tuple(2.0 ** (-8.0 * (h + 1) / N_HEADS) for h in range(N_HEADS))
QAUG_ROWS = 16


def _inproj_kernel(x_ref, ada_ref, g1_ref, w_ref, wt_ref,
                   us_ref, k_ref, ki_ref, qt_ref, qit_ref, vt_ref, wit_ref, *, d, ssm_w, attn_w, idx_w, tl):
    x = x_ref[0]
    shift = ada_ref[0, :, 0:d]
    scale = ada_ref[0, :, d:2 * d]
    u = _bf(_rms(x, g1_ref[...]) * (1.0 + scale) + shift)
    r = _dot(u, w_ref[...])
    us_ref[0] = r[:, 0:ssm_w]
    pos = pl.program_id(1) * tl + lax.broadcasted_iota(I32, (tl, LANES), 0)
    lane = lax.broadcasted_iota(I32, (tl, LANES), 1)
    pos_cols = jnp.where(lane == HEAD_DIM, pos // POS_SPLIT * POS_SPLIT, jnp.where(lane == HEAD_DIM + 1, pos % POS_SPLIT, 0))
    k_ref[0] = _bf(r[:, ssm_w:ssm_w + LANES] + pos_cols.astype(F32))
    ki_ref[0] = _bf(r[:, ssm_w + LANES:ssm_w + LANES + IDX_DIM])
    rt = _dot_nt(wt_ref[...], u)
    arow = lax.broadcasted_iota(I32, (QAUG_ROWS, tl), 0)
    for h in range(N_HEADS):
        base = h * LANES
        qt_ref[0, base:base + HEAD_DIM, :] = _bf(rt[h * HEAD_DIM:(h + 1) * HEAD_DIM])
        qt_ref[0, base + HEAD_DIM:base + HEAD_DIM + QAUG_ROWS, :] = _bf(jnp.where(arow < 2, ALIBI_SLOPES[h], 0.0))
        qt_ref[0, base + HEAD_DIM + QAUG_ROWS:base + LANES, :] = jnp.zeros((LANES - HEAD_DIM - QAUG_ROWS, tl), BF16)
    qit_ref[0] = _bf(rt[attn_w:attn_w + idx_w])
    vt_ref[0] = _bf(rt[attn_w + idx_w:attn_w + idx_w + HEAD_DIM])
    wit_ref[0] = rt[attn_w + idx_w + HEAD_DIM:attn_w + idx_w + HEAD_DIM + IDX_HEADS]


def _split_w_in(w_in, d):
    sizes = (SSM_WIDTH, N_HEADS * HEAD_DIM, HEAD_DIM, HEAD_DIM, IDX_HEADS * IDX_DIM, IDX_DIM, IDX_HEADS, d, d)
    offs = [0]
    for s in sizes:
        offs.append(offs[-1] + s)
    return [w_in[:, offs[i]:offs[i + 1]] for i in range(9)]


def _inproj(x, ada3, g1, w_in):
    bsz, seq, d = x.shape
    ssm_w = SSM_WIDTH
    attn_w = N_HEADS * HEAD_DIM
    idx_w = IDX_HEADS * IDX_DIM
    w_ssm, w_q, w_k, w_v, w_qi, w_ki, w_wi, _, _ = _split_w_in(w_in, d)
    zpad = lambda n: jnp.zeros((d, n), F32)
    wbig = jnp.concatenate([w_ssm, w_k, zpad(LANES - HEAD_DIM), w_ki, zpad(LANES - IDX_DIM)], axis=1).astype(BF16)
    wt = jnp.concatenate([w_q * (HEAD_DIM ** -0.5), w_qi * (IDX_DIM ** -0.5), w_v, w_wi,
                          zpad(LANES - HEAD_DIM - IDX_HEADS)], axis=1).T.astype(BF16)
    tl = INPROJ_ROWS
    kern = functools.partial(_inproj_kernel, d=d, ssm_w=ssm_w, attn_w=attn_w, idx_w=idx_w, tl=tl)
    row = lambda w: pl.BlockSpec((1, tl, w), lambda b, l: (b, l, 0))
    colt = lambda h: pl.BlockSpec((1, h, tl), lambda b, l: (b, 0, l))
    full = lambda a: pl.BlockSpec(a.shape, lambda b, l: (0,) * a.ndim)
    return pl.pallas_call(
        kern,
        out_shape=(jax.ShapeDtypeStruct((bsz, seq, ssm_w), F32),
                   jax.ShapeDtypeStruct((bsz, seq, LANES), BF16),
                   jax.ShapeDtypeStruct((bsz, seq, IDX_DIM), BF16),
                   jax.ShapeDtypeStruct((bsz, N_HEADS * LANES, seq), BF16),
                   jax.ShapeDtypeStruct((bsz, idx_w, seq), BF16),
                   jax.ShapeDtypeStruct((bsz, HEAD_DIM, seq), BF16),
                   jax.ShapeDtypeStruct((bsz, IDX_HEADS, seq), F32)),
        grid=(bsz, seq // tl),
        in_specs=[row(d),
                  pl.BlockSpec((1, 1, ada3.shape[2]), lambda b, l: (b, 0, 0)),
                  pl.BlockSpec((1, d), lambda b, l: (0, 0)),
                  full(wbig), full(wt)],
        out_specs=(row(ssm_w), row(LANES), row(IDX_DIM),
                   colt(N_HEADS * LANES), colt(idx_w), colt(HEAD_DIM), colt(IDX_HEADS)),
        compiler_params=_cparams(("arbitrary", "arbitrary")),
        name="inproj",
    )(x, ada3, g1.reshape(1, d), wbig, wt)


def _s5disc_kernel(lr_ref, li_ref, ldt_ref, br_ref, bi_ref, are_ref, aim_ref, bbr_ref, bbi_ref):
    lr = lr_ref[...]
    li = li_ref[...]
    dt = jnp.exp(ldt_ref[...])
    mag = jnp.exp(lr * dt)
    a_re = mag * jnp.cos(li * dt)
    a_im = mag * jnp.sin(li * dt)
    den = lr * lr + li * li
    n_re = a_re - 1.0
    f_re = (n_re * lr + a_im * li) / den
    f_im = (a_im * lr - n_re * li) / den
    br = br_ref[...]
    bi = bi_ref[...]
    are_ref[...] = a_re
    aim_ref[...] = a_im
    bbr_ref[...] = f_re * br - f_im * bi
    bbi_ref[...] = f_re * bi + f_im * br


def _s5disc(lam_re, lam_im, log_dt, b_re, b_im):
    g, p = lam_re.shape
    h = b_re.shape[2]
    rep = lambda a: jnp.repeat(a, h, axis=1)
    ldt = jnp.broadcast_to(log_dt[:, None], (g, p * h))
    sds = jax.ShapeDtypeStruct((g, p * h), F32)
    a_re, a_im, bb_re, bb_im = pl.pallas_call(
        _s5disc_kernel, out_shape=(sds, sds, sds, sds), name="s5disc",
    )(rep(lam_re), rep(lam_im), ldt, b_re.reshape(g, p * h), b_im.reshape(g, p * h))
    return a_re[:, ::h], a_im[:, ::h], bb_re.reshape(g, p, h), bb_im.reshape(g, p, h)


def _s5_kernel(u_ref, wb_ref, ar_ref, ai_ref, cc_ref, dsk_ref, wg_ref, bg_ref, o_ref, buf, hst, *, tl, width):
    nch = width // S5_LANE_CHUNK
    sw = S5_LANE_CHUNK // SSM_GROUP * SSM_STATE
    rows = tl * SUBLANES

    @pl.when(pl.program_id(0) == 0)
    def _():
        hst[...] = jnp.zeros_like(hst)

    u = jnp.swapaxes(u_ref[...], 0, 1).reshape(rows, width)
    ub = _bf(u)
    for j in range(nch):
        buf[:, j * 2 * sw:(j + 1) * 2 * sw] = _dot(ub[:, j * S5_LANE_CHUNK:(j + 1) * S5_LANE_CHUNK], wb_ref[j])

    for j in range(nch):
        re_cols = slice(j * 2 * sw, j * 2 * sw + sw)
        im_cols = slice(j * 2 * sw + sw, (j + 1) * 2 * sw)
        a_re = jnp.broadcast_to(ar_ref[:, j * sw:(j + 1) * sw], (SUBLANES, sw))
        a_im = jnp.broadcast_to(ai_ref[:, j * sw:(j + 1) * sw], (SUBLANES, sw))

        def step(t, carry, re_cols=re_cols, im_cols=im_cols, a_re=a_re, a_im=a_im):
            h_re, h_im = carry
            r0 = pl.multiple_of(t * SUBLANES, SUBLANES)
            n_re = (a_re * h_re - a_im * h_im) + buf[pl.ds(r0, SUBLANES), re_cols]
            n_im = (a_re * h_im + a_im * h_re) + buf[pl.ds(r0, SUBLANES), im_cols]
            buf[pl.ds(r0, SUBLANES), re_cols] = n_re
            buf[pl.ds(r0, SUBLANES), im_cols] = n_im
            return n_re, n_im

        h_re, h_im = lax.fori_loop(0, tl, step, (hst[:, re_cols], hst[:, im_cols]), unroll=True)
        hst[:, re_cols] = h_re
        hst[:, im_cols] = h_im

    ys = [_dot(_bf(buf[:, j * 2 * sw:(j + 1) * 2 * sw]), cc_ref[j]) for j in range(nch)]
    y = jnp.concatenate(ys, axis=1) + dsk_ref[...] * u
    y = jax.nn.gelu(y)
    y = y * jax.nn.sigmoid(_dot(_bf(y), wg_ref[...]) + bg_ref[...])
    o_ref[...] = _bf(jnp.swapaxes(y.reshape(tl, SUBLANES, width), 0, 1))


def _s5(u_t, a_re, a_im, bb_re, bb_im, c_re, c_im, d_skip, w_glu, b_glu):
    bsz, seq, width = u_t.shape
    assert bsz == SUBLANES
    nch = width // S5_LANE_CHUNK
    gpc = S5_LANE_CHUNK // SSM_GROUP
    sw = gpc * SSM_STATE
    eye = jnp.eye(gpc, dtype=F32)

    def bmat(bb):
        t = bb.reshape(nch, gpc, SSM_STATE, SSM_GROUP).transpose(0, 1, 3, 2)
        return jnp.einsum('jghp,gk->jghkp', t, eye).reshape(nch, S5_LANE_CHUNK, sw)

    def cmat(cc):
        t = cc.reshape(nch, gpc, SSM_GROUP, SSM_STATE).transpose(0, 1, 3, 2)
        return jnp.einsum('jgph,gk->jgpkh', t, eye).reshape(nch, sw, S5_LANE_CHUNK)

    wb = jnp.concatenate([bmat(bb_re), bmat(bb_im)], axis=2)
    cc = jnp.concatenate([cmat(c_re), -cmat(c_im)], axis=1)
    tl = S5_STEPS
    full = lambda a: pl.BlockSpec(a.shape, lambda i: (0,) * a.ndim)
    args = (u_t, wb.astype(BF16), a_re.reshape(1, -1), a_im.reshape(1, -1), cc.astype(BF16),
            d_skip.reshape(1, width), w_glu.astype(BF16), b_glu.reshape(1, width))
    return pl.pallas_call(
        functools.partial(_s5_kernel, tl=tl, width=width),
        out_shape=jax.ShapeDtypeStruct((bsz, seq, width), BF16),
        grid=(seq // tl,),
        in_specs=[pl.BlockSpec((bsz, tl, width), lambda i: (0, i, 0))] + [full(a) for a in args[1:]],
        out_specs=pl.BlockSpec((bsz, tl, width), lambda i: (0, i, 0)),
        scratch_shapes=[pltpu.VMEM((tl * SUBLANES, nch * 2 * sw), F32),
                        pltpu.VMEM((SUBLANES, nch * 2 * sw), F32)],
        compiler_params=_cparams(("arbitrary",)),
        name="s5",
    )(*args)


def _bit_transpose32(words):
    x = list(words)
    j, m = 16, 0x0000FFFF
    while j:
        k = 0
        while k < 32:
            t = (x[k] ^ lax.shift_right_logical(x[k + j], jnp.int32(j))) & jnp.int32(m - (1 << 32) if m >= 1 << 31 else m)
            x[k] = x[k] ^ t
            x[k + j] = x[k + j] ^ lax.shift_left(t, jnp.int32(j))
            k = (k + j + 1) & ~j
        j >>= 1
        m = (m ^ (m << j)) & 0xFFFFFFFF
    return x


def _dsa_kernel(qt_ref, qit_ref, wit_ref, ka_ref, ki_ref, vt_ref, o_ref, key_s, mb_s, acc_s, pl_s, p_s, *, tq, tk, topk, seq):
    i = pl.program_id(1)
    q0 = i * tq
    nkt = (q0 + tq + tk - 1) // tk
    ch = DSA_COUNT_ROWS
    krow = lax.broadcasted_iota(I32, (tk, tq), 0)
    qcol = q0 + lax.broadcasted_iota(I32, (tk, tq), 1)
    crow = lax.broadcasted_iota(I32, (ch, tq), 0)

    wb = wit_ref[0] * (IDX_HEADS ** -0.5)

    def score_tile(j, _):
        r0 = pl.multiple_of(j * tk, tk)
        kit = ki_ref[0, pl.ds(r0, tk), :]
        acc = jnp.zeros((tk, tq), F32)
        for h in range(IDX_HEADS):
            s = _dot(kit, qit_ref[0, h * IDX_DIM:(h + 1) * IDX_DIM, :])
            acc = acc + wb[h:h + 1, :] * jnp.maximum(s, 0.0)
        bits = lax.bitcast_convert_type(acc, I32)
        key = jnp.where(bits < 0, bits ^ jnp.int32(0x7FFFFFFF), bits)
        key = jnp.where(acc == 0.0, 0, key)
        key = jnp.where(krow + r0 <= qcol, key, INT_MIN)
        key_s[pl.ds(r0, tk), :] = key
        ukey = key ^ INT_MIN
        for c in range(tk // BITSLICE_ROWS):
            words = [ukey[c * BITSLICE_ROWS + v * SUBLANES:c * BITSLICE_ROWS + (v + 1) * SUBLANES, :]
                     for v in range(32)]
            planes = _bit_transpose32(words)
            g0 = pl.multiple_of((j * (tk // BITSLICE_ROWS) + c) * SUBLANES, SUBLANES)
            for it in range(32):
                pl_s[it, pl.ds(g0, SUBLANES), :] = planes[it]
        return 0

    @pl.when((pl.program_id(0) == 0) & (i == 0))
    def _():
        pl_s[...] = jnp.zeros(pl_s.shape, I32)

    def score_pair(jj, _):
        score_tile(2 * jj, 0)
        score_tile(2 * jj + 1, 0)
        return 0

    lax.fori_loop(0, nkt // 2, score_pair, 0)

    @pl.when(nkt % 2 == 1)
    def _():
        score_tile(nkt - 1, 0)

    def count(pred):
        def tile(j, cnt):
            for c in range(tk // ch):
                rr = pl.multiple_of(j * tk + c * ch, ch)
                cnt = cnt + jnp.where(pred(key_s[pl.ds(rr, ch), :], rr), 1, 0)
            return cnt
        cnt = lax.fori_loop(0, nkt, tile, jnp.zeros((ch, tq), I32))
        return jnp.sum(cnt.astype(F32), axis=0, keepdims=True)

    ngrp = seq // 32

    def lane_count(words):
        pc = lax.population_count(words).reshape(ngrp // SUBLANES, SUBLANES, tq)
        return jnp.sum(jnp.sum(pc, axis=0).astype(F32), axis=0, keepdims=True)

    def bit_step(it, carry):
        alive, above, ans_u = carry
        ones = alive & pl_s[it]
        cnt1 = lane_count(ones)
        take = above + cnt1 >= float(topk)
        alive = jnp.where(take, ones, alive ^ ones)
        above = jnp.where(take, above, above + cnt1)
        ans_u = jnp.where(take, ans_u | lax.shift_left(jnp.int32(1), 31 - it), ans_u)
        return alive, above, ans_u

    grow = lax.broadcasted_iota(I32, (ngrp, tq), 0)
    alive0 = jnp.where(grow < nkt * (tk // 32), -1, 0)
    alive, above, ans_u = lax.fori_loop(
        0, 32, bit_step, (alive0, jnp.zeros((1, tq), F32), jnp.zeros((1, tq), I32)))
    thr = jnp.maximum(ans_u ^ INT_MIN, INT_MIN + 1)
    cnt_ge = above + lane_count(alive)
    tied = jnp.where(ans_u != 0, cnt_ge, 0.0) > float(topk)
    has_ties = jnp.max(jnp.where(tied, 1.0, 0.0)) > 0.0

    def tie_cut():
        need = float(topk) - count(lambda kb, rr: kb > thr)
        nbits = max(1, (seq - 1).bit_length())

        def idx_step(b, x):
            cand = x | lax.shift_left(jnp.int32(1), nbits - 1 - b)
            below = count(lambda kb, rr: jnp.where(kb == thr, crow + rr, seq) < cand)
            return jnp.where(below < need, cand, x)

        x = lax.fori_loop(0, nbits, idx_step, jnp.zeros((1, tq), I32))
        return jnp.where(tied, x, seq)

    cut = lax.cond(has_ties, tie_cut, lambda: jnp.full((1, tq), seq, I32))

    def bias_tile(j, _):
        for c in range(tk // ch):
            rr = pl.multiple_of(j * tk + c * ch, ch)
            kb = key_s[pl.ds(rr, ch), :]
            tie_bias = jnp.where(crow + rr <= cut, 0.0, NEG_BIG)
            mb_s[pl.ds(rr, ch), :] = jnp.where(kb > thr, 0.0, jnp.where(kb == thr, tie_bias, NEG_BIG))
        return 0

    def logits(j, h):
        r0 = pl.multiple_of(j * tk, tk)
        s = _dot(ka_ref[0, pl.ds(r0, tk), :], qt_ref[0, h * LANES:(h + 1) * LANES, :]) + mb_s[pl.ds(r0, tk), :]
        return s.reshape(tk // SUBLANES, SUBLANES, tq)

    acc_s[...] = jnp.zeros(acc_s.shape, F32)

    def attn_tile(j, carry):
        ms, ls = carry
        r0 = pl.multiple_of(j * tk, tk)
        bias_tile(j, 0)
        new_m, new_l, alphas = [], [], []
        for h in range(N_HEADS):
            s = logits(j, h)
            m_new = jnp.maximum(ms[h], jnp.max(jnp.max(s, axis=0), axis=0, keepdims=True))
            alpha = jnp.exp(ms[h] - m_new)
            p = jnp.exp(s - m_new)
            new_m.append(m_new)
            new_l.append(alpha * ls[h] + jnp.sum(p, axis=0))
            alphas.append(alpha)
            p_s[h] = _bf(p.reshape(tk, tq))
        for h in range(N_HEADS):
            rows = slice(h * HEAD_DIM, (h + 1) * HEAD_DIM)
            acc_s[rows, :] = alphas[h] * acc_s[rows, :] + _dot(vt_ref[0, :, pl.ds(r0, tk)], p_s[h])
        return tuple(new_m), tuple(new_l)

    init = ((jnp.full((1, tq), NEG_BIG, F32),) * N_HEADS, (jnp.zeros((SUBLANES, tq), F32),) * N_HEADS)
    _, ls = lax.fori_loop(0, nkt, attn_tile, init)
    for h in range(N_HEADS):
        rows = slice(h * HEAD_DIM, (h + 1) * HEAD_DIM)
        acc_s[rows, :] = acc_s[rows, :] / jnp.sum(ls[h], axis=0, keepdims=True)
    o_ref[0] = _bf(acc_s[...].T)


def _dsa(qt, qit, wit, ka, ki, vt):
    bsz, seq = ka.shape[0], ka.shape[1]
    aw = N_HEADS * HEAD_DIM
    tq = min(DSA_Q_COLS, seq)
    tk = min(DSA_K_ROWS, seq)
    topk = min(TOPK_MAX, seq // 4)
    assert (seq - 1) // POS_SPLIT < 256 and POS_SPLIT <= 256, "key positions must split into two bf16-exact parts"
    assert all(float(np.float32(sl).astype(BF16)) == sl for sl in ALIBI_SLOPES), "ALiBi slopes must be bf16-exact"
    kern = functools.partial(_dsa_kernel, tq=tq, tk=tk, topk=topk, seq=seq)
    cols = lambda r: pl.BlockSpec((1, r, tq), lambda b, i: (b, 0, i))
    return pl.pallas_call(
        kern,
        out_shape=jax.ShapeDtypeStruct((bsz, seq, aw), BF16),
        grid=(bsz, seq // tq),
        in_specs=[cols(N_HEADS * LANES), cols(qit.shape[1]), cols(IDX_HEADS),
                  pl.BlockSpec((1, seq, LANES), lambda b, i: (b, 0, 0)),
                  pl.BlockSpec((1, seq, IDX_DIM), lambda b, i: (b, 0, 0)),
                  pl.BlockSpec((1, HEAD_DIM, seq), lambda b, i: (b, 0, 0))],
        out_specs=pl.BlockSpec((1, tq, aw), lambda b, i: (b, i, 0)),
        scratch_shapes=[pltpu.VMEM((seq, tq), I32), pltpu.VMEM((seq, tq), F32), pltpu.VMEM((aw, tq), F32),
                        pltpu.VMEM((32, seq // 32, tq), I32), pltpu.VMEM((N_HEADS, tk, tq), BF16)],
        compiler_params=_cparams(("arbitrary", "arbitrary")),
        name="dsa",
    )(qt, qit, wit, ka, ki, vt)


def _mix_kernel(x_ref, ys_ref, ya_ref, ada_ref, g1_ref, wgt_ref, wps_ref, wpa_ref, wo_ref, g2_ref,
                h_ref, u2_ref, *, d):
    gate1 = ada_ref[0, :, 2 * d:3 * d]
    shift2 = ada_ref[0, :, 3 * d:4 * d]
    scale2 = ada_ref[0, :, 4 * d:5 * d]
    x = x_ref[0]
    u = _bf(_rms(x, g1_ref[...]) * (1.0 + ada_ref[0, :, d:2 * d]) + ada_ref[0, :, 0:d])
    g = _dot(u, wgt_ref[...])
    mixed = (jax.nn.sigmoid(g[:, 0:d]) * _dot(ys_ref[0], wps_ref[...])
             + jax.nn.sigmoid(g[:, d:2 * d]) * _dot(ya_ref[0], wpa_ref[...]))
    h = x + gate1 * _dot(_bf(mixed), wo_ref[...])
    h_ref[0] = h
    u2_ref[0] = _rms(h, g2_ref[...]) * (1.0 + scale2) + shift2


def _mix(x, ys, ya, ada3, g1, w_in, wps, wpa, wo, g2):
    bsz, seq, d = x.shape
    tm = MIX_ROWS
    row = lambda w: pl.BlockSpec((1, tm, w), lambda b, l: (b, l, 0))
    full = lambda a: pl.BlockSpec(a.shape, lambda b, l: (0,) * a.ndim)
    wps, wpa, wo = wps.astype(BF16), wpa.astype(BF16), wo.astype(BF16)
    wgt = jnp.concatenate(_split_w_in(w_in, d)[7:9], axis=1).astype(BF16)
    g1 = g1.reshape(1, d)
    g2 = g2.reshape(1, d)
    return pl.pallas_call(
        functools.partial(_mix_kernel, d=d),
        out_shape=(jax.ShapeDtypeStruct((bsz, seq, d), F32), jax.ShapeDtypeStruct((bsz, seq, d), F32)),
        grid=(bsz, seq // tm),
        in_specs=[row(d), row(ys.shape[2]), row(ya.shape[2]),
                  pl.BlockSpec((1, 1, ada3.shape[2]), lambda b, l: (b, 0, 0)),
                  full(g1), full(wgt), full(wps), full(wpa), full(wo), full(g2)],
        out_specs=(row(d), row(d)),
        compiler_params=_cparams(("arbitrary", "arbitrary")),
        name="mix",
    )(x, ys, ya, ada3, g1, wgt, wps, wpa, wo, g2)


def _first_max(cur, idx, axis, big):
    m = jnp.max(cur, axis=axis, keepdims=True)
    first = jnp.min(jnp.where(cur == m, idx, big), axis=axis, keepdims=True)
    return m, idx == first


def _route_kernel(u_ref, wrh_ref, wrl_ref, rb_ref, tri_ref, ltri_ref,
                  gt_ref, loc_ref, c8_ref, loff_ref, run0_ref, tot_ref, run_s, *, t, tt):
    @pl.when(pl.program_id(0) == 0)
    def _():
        run_s[...] = jnp.zeros_like(run_s)

    uh, ul = _split(u_ref[...])
    logits = _dot_nt(wrh_ref[...], uh) + (_dot_nt(wrl_ref[...], uh) + _dot_nt(wrh_ref[...], ul))
    scores = jax.nn.sigmoid(logits)
    biased = scores + rb_ref[...]
    per_group = N_EXPERTS // N_GROUPS
    b3 = biased.reshape(N_GROUPS, per_group, t)
    i3 = lax.broadcasted_iota(I32, b3.shape, 1)
    m1, hit1 = _first_max(b3, i3, 1, per_group)
    m2 = jnp.max(jnp.where(hit1, -jnp.inf, b3), axis=1, keepdims=True)
    gs = (m1 + m2).reshape(N_GROUPS, t)
    gi = lax.broadcasted_iota(I32, gs.shape, 0)
    gsel = jnp.zeros(gs.shape, F32)
    for _ in range(TOPK_GROUPS):
        _, hit = _first_max(gs, gi, 0, N_GROUPS)
        gsel = jnp.where(hit, 1.0, gsel)
        gs = jnp.where(hit, -jnp.inf, gs)
    cur = jnp.where(gsel.reshape(N_GROUPS, 1, t) > 0.0, b3, -jnp.inf).reshape(N_EXPERTS, t)
    ei = lax.broadcasted_iota(I32, cur.shape, 0)
    hits = []
    gates = []
    for _ in range(TOP_K):
        _, hit = _first_max(cur, ei, 0, N_EXPERTS)
        hits.append(hit)
        gates.append(jnp.sum(jnp.where(hit, scores, 0.0), axis=0, keepdims=True))
        cur = jnp.where(hit, -jnp.inf, cur)
    gate = jnp.concatenate(gates, axis=0)
    gt_ref[...] = gate / jnp.sum(gate, axis=0, keepdims=True) * ROUTED_SCALE
    onehot = jnp.zeros(cur.shape, F32)
    for hit in hits:
        onehot = jnp.where(hit, 1.0, onehot)
    for sub in range(t // tt):
        cols = slice(sub * tt, (sub + 1) * tt)
        oh = onehot[:, cols]
        cnt = jnp.sum(oh, axis=1, keepdims=True)
        c8 = jnp.floor((cnt + (SUBLANES - 1)) * (1.0 / SUBLANES)) * SUBLANES
        c8l = jnp.broadcast_to(c8, (N_EXPERTS, LANES))
        loff = _dot(ltri_ref[...], _bf(c8l))
        slot = _dot(_bf(oh), tri_ref[...]) + loff[:, 0:1]
        loc_ref[:, cols] = jnp.concatenate(
            [jnp.sum(jnp.where(hit[:, cols], slot, 0.0), axis=0, keepdims=True) for hit in hits],
            axis=0).astype(I32)
        c8_ref[sub] = c8l
        loff_ref[sub] = loff
        run0_ref[sub] = run_s[...]
        run_s[...] = run_s[...] + c8
    tot_ref[...] = run_s[...]


def _route(u2, w_router, router_bias):
    n, d = u2.shape
    tt = min(MOE_TILE, n)
    t = min(ROUTE_TILES * tt, n)
    nt = n // tt
    wt = w_router.T
    wrh = wt.astype(BF16)
    wrl = (wt - wrh.astype(F32)).astype(BF16)
    tri = (jnp.arange(tt)[:, None] < jnp.arange(tt)[None, :]).astype(BF16)
    ex = jnp.arange(N_EXPERTS)
    ltri = (ex[None, :] < ex[:, None]).astype(BF16)
    full = lambda a: pl.BlockSpec(a.shape, lambda i: (0,) * a.ndim)
    col = pl.BlockSpec((TOP_K, t), lambda i: (0, i))
    tab = pl.BlockSpec((t // tt, N_EXPERTS, LANES), lambda i: (i, 0, 0))
    tab_sds = jax.ShapeDtypeStruct((nt, N_EXPERTS, LANES), F32)
    rb = router_bias.reshape(N_EXPERTS, 1)
    return pl.pallas_call(
        functools.partial(_route_kernel, t=t, tt=tt),
        out_shape=(jax.ShapeDtypeStruct((TOP_K, n), F32), jax.ShapeDtypeStruct((TOP_K, n), I32),
                   tab_sds, tab_sds, tab_sds, jax.ShapeDtypeStruct((N_EXPERTS, LANES), F32)),
        grid=(n // t,),
        in_specs=[pl.BlockSpec((t, d), lambda i: (i, 0)), full(wrh), full(wrl), full(rb), full(tri), full(ltri)],
        out_specs=(col, col, tab, tab, tab, pl.BlockSpec((N_EXPERTS, LANES), lambda i: (0, 0))),
        scratch_shapes=[pltpu.VMEM((N_EXPERTS, LANES), F32)],
        compiler_params=_cparams(("arbitrary",)),
        name="route",
    )(u2, wrh, wrl, rb, tri, ltri)


RUN_BITS = tuple(1 << b for b in reversed(range((MOE_TILE // SUBLANES).bit_length())))


def _for_each_run_piece(n8_ref, src_ref, dst_ref, tile, bits, fn):
    def per_expert(e, _):
        idx = tile * N_EXPERTS + e
        n8 = n8_ref[idx]
        src = src_ref[idx]
        dst = dst_ref[idx]
        for p in bits:
            off = (n8 & ~(2 * p - 1)) * SUBLANES

            @pl.when((n8 & p) != 0)
            def _(p=p, off=off):
                fn(pl.multiple_of(src + off, SUBLANES), pl.multiple_of(dst + off, SUBLANES), p * SUBLANES)
        return 0

    lax.fori_loop(0, N_EXPERTS, per_expert, 0)


def _issue_runs_inline(n8_ref, src_ref, dst_ref, tile, enable, fn, experts=range(N_EXPERTS)):
    for e in experts:
        idx = tile * N_EXPERTS + e
        n8 = jnp.where(enable, n8_ref[idx], 0)
        src = src_ref[idx]
        dst = dst_ref[idx]
        for p in RUN_BITS:
            off = (n8 & ~(2 * p - 1)) * SUBLANES

            @pl.when((n8 & p) != 0)
            def _(p=p, off=off, src=src, dst=dst):
                fn(pl.multiple_of(src + off, SUBLANES), pl.multiple_of(dst + off, SUBLANES), p * SUBLANES)


def _wait_rows(n8, make_copy, max_rows):
    for p in tuple(1 << b for b in reversed(range((max_rows // SUBLANES).bit_length()))):
        @pl.when((n8 & p) != 0)
        def _(p=p):
            make_copy(p * SUBLANES).wait()


def _dispatch_kernel(n8_ref, src_ref, dst_ref, tot_ref, zn8_ref, zdst_ref, u_ref, loc_ref, gate_ref, xs_hbm,
                     lbuf0, lbuf1, lbuf2, zx, sems, *, tt, nslot, dh):
    i = pl.program_id(0)
    last = pl.num_programs(0) - 1
    bufs = (lbuf0, lbuf1, lbuf2)
    nbuf = len(bufs)
    zsem = nbuf

    def copy_from(slot):
        def piece(s0, d0, rows):
            pltpu.make_async_copy(bufs[slot].at[pl.ds(s0, rows)], xs_hbm.at[pl.ds(d0, rows)], sems.at[slot]).start()
        return piece

    def wait_tile(tile, slot):
        _wait_rows(tot_ref[tile], lambda rows: pltpu.make_async_copy(
            bufs[slot].at[pl.ds(0, rows)], xs_hbm.at[pl.ds(0, rows)], sems.at[slot]), nslot)

    @pl.when(i == 0)
    def _():
        zx[...] = jnp.zeros(zx.shape, I32)

        def zero_piece(s0, d0, rows):
            cx = pltpu.make_async_copy(zx.at[pl.ds(0, rows)], xs_hbm.at[pl.ds(d0, rows)], sems.at[zsem])
            cx.start()
            cx.wait()

        zbits = tuple(1 << b for b in reversed(range((EXPERT_ROWS // SUBLANES - 1).bit_length())))
        _for_each_run_piece(zn8_ref, zdst_ref, zdst_ref, 0, zbits, zero_piece)

    def build(buf):
        ub = _bf(u_ref[...])
        ones = jnp.ones((tt, LANES), BF16)
        loc = loc_ref[...]
        gate = gate_ref[...]
        rows_b = lax.broadcasted_iota(I32, (SLOT_CHUNK, tt), 0).astype(F32).astype(BF16)
        loc_hi = lax.shift_right_logical(loc, SLOT_CHUNK.bit_length() - 1)
        loc_lo = (loc & (SLOT_CHUNK - 1)).astype(F32)
        gate_h = _bf(gate)
        gate_l = _bf(gate - gate_h.astype(F32))
        one_b = jnp.ones((SLOT_CHUNK, tt), BF16)
        for c in range(nslot // SLOT_CHUNK):
            perm = jnp.zeros((SLOT_CHUNK, tt), BF16)
            pgh = jnp.zeros((SLOT_CHUNK, tt), BF16)
            pgl = jnp.zeros((SLOT_CHUNK, tt), BF16)
            for k in range(TOP_K):
                lk = _bf(jnp.where(loc_hi[k:k + 1, :] == c, loc_lo[k:k + 1, :], -1.0))
                eq = rows_b == lk
                perm = jnp.where(eq, one_b, perm)
                pgh = jnp.where(eq, jnp.broadcast_to(gate_h[k:k + 1, :], (SLOT_CHUNK, tt)), pgh)
                pgl = jnp.where(eq, jnp.broadcast_to(gate_l[k:k + 1, :], (SLOT_CHUNK, tt)), pgl)
            cs = slice(c * SLOT_CHUNK, (c + 1) * SLOT_CHUNK)
            xp = lax.bitcast_convert_type(_dot(perm, ub), I32)
            buf[cs, 0:dh] = xp[:, 0:dh] | lax.shift_right_logical(xp[:, dh:2 * dh], 16)
            buf[cs, dh:dh + LANES] = lax.bitcast_convert_type(_dot(pgh, ones) + _dot(pgl, ones), I32)

    def step(cur):
        prv, prv2 = (cur - 1) % nbuf, (cur - 2) % nbuf

        @pl.when(i >= nbuf)
        def _():
            wait_tile(i - nbuf, cur)

        _issue_runs_inline(n8_ref, src_ref, dst_ref, jnp.maximum(i - 1, 0), i >= 1, copy_from(prv))
        build(bufs[cur])

        @pl.when(i == last)
        def _():
            _for_each_run_piece(n8_ref, src_ref, dst_ref, i, RUN_BITS, copy_from(cur))

            @pl.when(i >= 2)
            def _():
                wait_tile(i - 2, prv2)

            @pl.when(i >= 1)
            def _():
                wait_tile(i - 1, prv)
            wait_tile(i, cur)

    for cur in range(nbuf):
        pl.when(i % nbuf == cur)(functools.partial(step, cur))


def _dispatch(tabs, ztabs, u2, loc_t, gate_t, n_rows):
    n, d = u2.shape
    tt = min(MOE_TILE, n)
    nslot = TOP_K * tt + N_EXPERTS * SUBLANES
    dh = d // 2
    assert nslot % SLOT_CHUNK == 0 and tt // SUBLANES == RUN_BITS[0]
    col = pl.BlockSpec((TOP_K, tt), lambda i, *_: (0, i))
    return pl.pallas_call(
        functools.partial(_dispatch_kernel, tt=tt, nslot=nslot, dh=dh),
        out_shape=jax.ShapeDtypeStruct((n_rows, dh + LANES), I32),
        grid_spec=pltpu.PrefetchScalarGridSpec(
            num_scalar_prefetch=6, grid=(n // tt,),
            in_specs=[pl.BlockSpec((tt, d), lambda i, *_: (i, 0)), col, col],
            out_specs=pl.BlockSpec(memory_space=pl.ANY),
            scratch_shapes=[pltpu.VMEM((nslot, dh + LANES), I32)] * 3 + [
                pltpu.VMEM((EXPERT_ROWS // 2, dh + LANES), I32), pltpu.SemaphoreType.DMA((4,))]),
        compiler_params=_cparams(("arbitrary",)),
        name="dispatch",
    )(*tabs, *ztabs, u2, loc_t, gate_t)


def _experts_kernel(be_ref, nu_ref, xs_hbm, wg_ref, wu_ref, wd_ref, ys_ref, xbuf, sems, *, d):
    del be_ref
    dh = d // 2
    i = pl.program_id(0)
    nu = nu_ref[0]
    ahead = EXPERT_RING - 1

    def fetch(j):
        return pltpu.make_async_copy(xs_hbm.at[pl.ds(pl.multiple_of(j * EXPERT_ROWS, EXPERT_ROWS), EXPERT_ROWS)],
                                     xbuf.at[j % EXPERT_RING], sems.at[j % EXPERT_RING])

    @pl.when(i == 0)
    def _():
        for j in range(ahead):
            pl.when(j < nu)(lambda j=j: fetch(j).start())

    @pl.when(i + ahead < nu)
    def _():
        fetch(i + ahead).start()

    @pl.when(i < nu)
    def _():
        fetch(i).wait()
        xs_ref = xbuf.at[i % EXPERT_RING]
        w = xs_ref[:, 0:dh]
        xa = _bf(lax.bitcast_convert_type(w & jnp.int32(-65536), F32))
        xb = _bf(lax.bitcast_convert_type(lax.shift_left(w, 16), F32))
        gate = lax.bitcast_convert_type(xs_ref[:, dh:dh + LANES], F32)
        hg = _dot(xa, _bf(wg_ref[0, 0:dh, :])) + _dot(xb, _bf(wg_ref[0, dh:d, :]))
        hu = _dot(xa, _bf(wu_ref[0, 0:dh, :])) + _dot(xb, _bf(wu_ref[0, dh:d, :]))
        y = _dot(_bf(jax.nn.silu(hg) * hu), _bf(wd_ref[0])) * jnp.tile(gate, (1, d // LANES))
        yb = lax.bitcast_convert_type(_bf(y).astype(F32), I32)
        ys_ref[...] = yb[:, 0:dh] | lax.shift_right_logical(yb[:, dh:d], 16)


def _experts(blk_expert, n_used, xs, wg, wu, wd):
    rows, xw = xs.shape
    d = wg.shape[1]
    de = wg.shape[2]
    nblk = rows // EXPERT_ROWS
    blk = lambda i, be, nu: jnp.minimum(i, nu[0] - 1)
    return pl.pallas_call(
        functools.partial(_experts_kernel, d=d),
        out_shape=jax.ShapeDtypeStruct((rows, d // 2), I32),
        grid_spec=pltpu.PrefetchScalarGridSpec(
            num_scalar_prefetch=2, grid=(nblk,),
            in_specs=[pl.BlockSpec(memory_space=pl.ANY),
                      pl.BlockSpec((1, d, de), lambda i, be, nu: (be[blk(i, be, nu)], 0, 0)),
                      pl.BlockSpec((1, d, de), lambda i, be, nu: (be[blk(i, be, nu)], 0, 0)),
                      pl.BlockSpec((1, de, d), lambda i, be, nu: (be[blk(i, be, nu)], 0, 0))],
            out_specs=pl.BlockSpec((EXPERT_ROWS, d // 2), lambda i, be, nu: (blk(i, be, nu), 0)),
            scratch_shapes=[pltpu.VMEM((EXPERT_RING, EXPERT_ROWS, xw), I32),
                            pltpu.SemaphoreType.DMA((EXPERT_RING,))]),
        compiler_params=_cparams(("arbitrary",)),
        name="experts",
    )(blk_expert, n_used, xs, wg, wu, wd)


def _combine_kernel(n8_ref, src_ref, dst_ref, tot_ref, ys_hbm, loc_ref, h_ref, u2_ref, ada_ref, wsg_ref, wsu_ref, wsd_ref,
                    gf_ref, o_ref, ybuf0, ybuf1, pick_s, sems, *, tt, nslot, d):
    i = pl.program_id(0)
    last = pl.num_programs(0) - 1
    bufs = (ybuf0, ybuf1)
    dh = d // 2
    nchunk = nslot // SLOT_CHUNK

    def copy_into(slot):
        def piece(s0, d0, rows):
            pltpu.make_async_copy(ys_hbm.at[pl.ds(d0, rows)], bufs[slot].at[pl.ds(s0, rows)], sems.at[slot]).start()
        return piece

    @pl.when(i == 0)
    def _():
        ybuf0[...] = jnp.zeros(ybuf0.shape, I32)
        ybuf1[...] = jnp.zeros(ybuf1.shape, I32)
        _for_each_run_piece(n8_ref, src_ref, dst_ref, 0, RUN_BITS, copy_into(0))

    def step(cur, nxt):
        nxt_tile = jnp.minimum(i + 1, last)
        _issue_runs_inline(n8_ref, src_ref, dst_ref, nxt_tile, i < last, copy_into(nxt), range(N_EXPERTS // 2))
        x = _bf(u2_ref[...])
        shared = _dot(_bf(jax.nn.silu(_dot(x, wsg_ref[...])) * _dot(x, wsu_ref[...])), wsd_ref[...])
        loc = loc_ref[...]
        cols_b = lax.broadcasted_iota(I32, (tt, SLOT_CHUNK), 1).astype(F32).astype(BF16)
        loc_hi = lax.shift_right_logical(loc, SLOT_CHUNK.bit_length() - 1)
        loc_lo = (loc & (SLOT_CHUNK - 1)).astype(F32)
        one_b = jnp.ones((tt, SLOT_CHUNK), BF16)
        for c in range(nchunk):
            pick = jnp.zeros((tt, SLOT_CHUNK), BF16)
            for k in range(TOP_K):
                lk = _bf(jnp.where(loc_hi[:, k:k + 1] == c, loc_lo[:, k:k + 1], -1.0))
                pick = jnp.where(cols_b == lk, one_b, pick)
            pick_s[c] = pick
        _wait_rows(tot_ref[i], lambda rows: pltpu.make_async_copy(
            ys_hbm.at[pl.ds(0, rows)], bufs[cur].at[pl.ds(0, rows)], sems.at[cur]), nslot)
        _issue_runs_inline(n8_ref, src_ref, dst_ref, nxt_tile, i < last, copy_into(nxt),
                           range(N_EXPERTS // 2, N_EXPERTS))
        routed_a = jnp.zeros((tt, dh), F32)
        routed_b = jnp.zeros((tt, dh), F32)
        for c in range(nchunk):
            w = bufs[cur][c * SLOT_CHUNK:(c + 1) * SLOT_CHUNK, :]
            routed_a = routed_a + _dot(pick_s[c], _bf(lax.bitcast_convert_type(w & jnp.int32(-65536), F32)))
            routed_b = routed_b + _dot(pick_s[c], _bf(lax.bitcast_convert_type(lax.shift_left(w, 16), F32)))
        routed = jnp.concatenate([routed_a, routed_b], axis=1)
        gate2 = ada_ref[0, :, 5 * d:6 * d]
        h = h_ref[...] + gate2 * (routed + shared)
        o_ref[...] = _rms(h, gf_ref[...])

    pl.when(i % 2 == 0)(lambda: step(0, 1))
    pl.when(i % 2 == 1)(lambda: step(1, 0))


def _combine(tabs, ys, loc, h1, u2, ada3, wsg, wsu, wsd, gf, seq):
    n, d = h1.shape
    tt = min(MOE_TILE, n)
    nslot = TOP_K * tt + N_EXPERTS * SUBLANES
    per_b = seq // tt
    row = pl.BlockSpec((tt, d), lambda i, *_: (i, 0))
    full = lambda a: pl.BlockSpec(a.shape, lambda i, *_: (0,) * a.ndim)
    wsg, wsu, wsd = wsg.astype(BF16), wsu.astype(BF16), wsd.astype(BF16)
    gf = gf.reshape(1, d)
    return pl.pallas_call(
        functools.partial(_combine_kernel, tt=tt, nslot=nslot, d=d),
        out_shape=jax.ShapeDtypeStruct((n, d), F32),
        grid_spec=pltpu.PrefetchScalarGridSpec(
            num_scalar_prefetch=4, grid=(n // tt,),
            in_specs=[pl.BlockSpec(memory_space=pl.ANY),
                      pl.BlockSpec((tt, TOP_K), lambda i, *_: (i, 0)),
                      row, row,
                      pl.BlockSpec((1, 1, ada3.shape[2]), lambda i, *_: (i // per_b, 0, 0)),
                      full(wsg), full(wsu), full(wsd), full(gf)],
            out_specs=row,
            scratch_shapes=[pltpu.VMEM((nslot, d // 2), I32), pltpu.VMEM((nslot, d // 2), I32),
                            pltpu.VMEM((nslot // SLOT_CHUNK, tt, SLOT_CHUNK), BF16),
                            pltpu.SemaphoreType.DMA((2,))]),
        compiler_params=_cparams(("arbitrary",)),
        name="combine",
    )(*tabs, ys, loc, h1, u2, ada3, wsg, wsu, wsd, gf)


def _moe(h1, u2, ada3, w_router, router_bias, wg, wu, wd, wsg, wsu, wsd, gf):
    bsz, seq, d = h1.shape
    n = bsz * seq
    assert seq % min(MOE_TILE, n) == 0
    h1f = h1.reshape(n, d)
    u2f = u2.reshape(n, d)
    gate_t, loc_t, c8, loff, run0, tot = _route(u2f, w_router, router_bias)
    nt = c8.shape[0]
    as_tab = lambda a: a[:, :, 0].astype(I32)
    tot8 = tot[:, 0].astype(I32)
    padded = (tot8 + EXPERT_ROWS - 1) // EXPERT_ROWS * EXPERT_ROWS
    pend = jnp.cumsum(padded)
    pstart = (pend - padded).astype(I32)
    nblk = (n * TOP_K + nt * N_EXPERTS * (SUBLANES - 1) + N_EXPERTS * (EXPERT_ROWS - 1) + EXPERT_ROWS - 1) // EXPERT_ROWS
    blk_row0 = jnp.arange(nblk, dtype=I32) * EXPERT_ROWS
    blk_expert = jnp.minimum(jnp.sum(pend[None, :] <= blk_row0[:, None], axis=1), N_EXPERTS - 1).astype(I32)
    n_used = (pend[-1:] // EXPERT_ROWS).astype(I32)
    n8 = as_tab(c8) // SUBLANES
    tabs = (n8.reshape(-1), as_tab(loff).reshape(-1), (pstart[None, :] + as_tab(run0)).reshape(-1),
            jnp.sum(n8, axis=1))
    ztabs = ((padded - tot8) // SUBLANES, pstart + tot8)
    xs = _dispatch(tabs, ztabs, u2f, loc_t, gate_t, nblk * EXPERT_ROWS)
    ys = _experts(blk_expert, n_used, xs, wg, wu, wd)
    out = _combine(tabs, ys, loc_t.T, h1f, u2f, ada3, wsg, wsu, wsd, gf, seq)
    return out.reshape(bsz, seq, d)


def kernel(x, c, w_ada, b_ada, norm1_g, w_in, ssm_lambda_re, ssm_lambda_im, ssm_log_dt, ssm_b_re, ssm_b_im,
           ssm_c_re, ssm_c_im, ssm_d, ssm_w_glu, ssm_b_glu, w_proj_ssm, w_proj_attn, w_out, norm2_g, w_router,
           router_bias, w_exp_gate, w_exp_up, w_exp_down, w_sh_gate, w_sh_up, w_sh_down, norm_f_g):
    depth = w_ada.shape[0]
    assert depth == 1, "the final norm is fused into the last (only) layer's combine kernel"
    bsz, seq, d = x.shape
    layer = 0
    ada3 = _ada(c, w_ada[layer], b_ada[layer]).reshape(bsz, 1, 6 * d)
    us, ka, ki, qt, qit, vt, wit = _inproj(x, ada3, norm1_g[layer], w_in[layer])
    a_re, a_im, bb_re, bb_im = _s5disc(ssm_lambda_re[layer], ssm_lambda_im[layer], ssm_log_dt[layer],
                                       ssm_b_re[layer], ssm_b_im[layer])
    ys_t = _s5(us, a_re, a_im, bb_re, bb_im, ssm_c_re[layer], ssm_c_im[layer],
               ssm_d[layer], ssm_w_glu[layer], ssm_b_glu[layer])
    ya = _dsa(qt, qit, wit, ka, ki, vt)
    h1, u2 = _mix(x, ys_t, ya, ada3, norm1_g[layer], w_in[layer], w_proj_ssm[layer],
                  w_proj_attn[layer], w_out[layer], norm2_g[layer])
    return _moe(h1, u2, ada3, w_router[layer], router_bias[layer], w_exp_gate[layer], w_exp_up[layer],
                w_exp_down[layer], w_sh_gate[layer], w_sh_up[layer], w_sh_down[layer], norm_f_g)
```

```python
import functools

import jax
import jax.numpy as jnp
import numpy as np
from jax import lax
from jax.experimental import pallas as pl
from jax.experimental.pallas import tpu as pltpu

F32 = jnp.float32
BF16 = jnp.bfloat16
I32 = jnp.int32

SSM_GROUP = 16
SSM_STATE = 64
N_HEADS = 8
HEAD_DIM = 64
IDX_HEADS = 8
IDX_DIM = 64
TOPK_MAX = 256
N_EXPERTS = 64
TOP_K = 8
N_GROUPS = 8
TOPK_GROUPS = 4
ROUTED_SCALE = 2.5
EPS = 1e-6
SSM_WIDTH = 512

V7X_VMEM_LIMIT_BYTES = 56 * 1024 * 1024
LANES = 128
SUBLANES = 8

ADA_COLS = 1024
INPROJ_ROWS = 512
S5_STEPS = 64
S5_LANE_CHUNK = 128
DSA_Q_COLS = 256
DSA_K_ROWS = 512
DSA_COUNT_ROWS = 64
BITSLICE_ROWS = 256
POS_SPLIT = 64
MIX_ROWS = 512
MOE_TILE = 256
ROUTE_TILES = 4
SLOT_CHUNK = 256
EXPERT_ROWS = 1024
EXPERT_RING = 3

NEG_BIG = -1e30
INT_MIN = -(2 ** 31)


def _cparams(sem):
    return pltpu.CompilerParams(dimension_semantics=sem, vmem_limit_bytes=V7X_VMEM_LIMIT_BYTES)


def _bf(x):
    return x.astype(BF16)


def _dot(a, b):
    return jnp.dot(a, b, preferred_element_type=F32)


def _dot_nt(a, b):
    return lax.dot_general(a, b, (((1,), (1,)), ((), ())), preferred_element_type=F32)


def _split(x):
    hi = _bf(x)
    lo = _bf(x - hi.astype(F32))
    return hi, lo


def _dot3(a, b):
    ah, al = _split(a)
    bh, bl = _split(b)
    return _dot(ah, bh) + (_dot(ah, bl) + _dot(al, bh))


def _rms(x, g):
    return x * lax.rsqrt(jnp.mean(x * x, axis=-1, keepdims=True) + EPS) * g


def _ada_kernel(c_ref, w_ref, b_ref, o_ref):
    c = c_ref[...]
    o_ref[...] = _dot3(c * jax.nn.sigmoid(c), w_ref[...]) + b_ref[...]


def _ada(c, w, b):
    bsz, d = c.shape
    n = w.shape[1]
    tn = ADA_COLS
    return pl.pallas_call(
        _ada_kernel,
        out_shape=jax.ShapeDtypeStruct((bsz, n), F32),
        grid=(n // tn,),
        in_specs=[pl.BlockSpec((bsz, d), lambda j: (0, 0)),
                  pl.BlockSpec((d, tn), lambda j: (0, j)),
                  pl.BlockSpec((1, tn), lambda j: (0, j))],
        out_specs=pl.BlockSpec((bsz, tn), lambda j: (0, j)),
        compiler_params=_cparams(("arbitrary",)),
        name="ada",
    )(c, w, b.reshape(1, n))


ALIBI_SLOPES = tuple(2.0 ** (-8.0 * (h + 1) / N_HEADS) for h in range(N_HEADS))
QAUG_ROWS = 16


def _inproj_kernel(x_ref, ada_ref, g1_ref, w_ref, wt_ref,
                   us_ref, k_ref, ki_ref, qt_ref, qit_ref, vt_ref, wit_ref, *, d, ssm_w, attn_w, idx_w, tl):
    x = x_ref[0]
    shift = ada_ref[0, :, 0:d]
    scale = ada_ref[0, :, d:2 * d]
    u = _bf(_rms(x, g1_ref[...]) * (1.0 + scale) + shift)
    r = _dot(u, w_ref[...])
    us_ref[0] = r[:, 0:ssm_w]
    pos = pl.program_id(1) * tl + lax.broadcasted_iota(I32, (tl, LANES), 0)
    lane = lax.broadcasted_iota(I32, (tl, LANES), 1)
    pos_cols = jnp.where(lane == HEAD_DIM, pos // POS_SPLIT * POS_SPLIT, jnp.where(lane == HEAD_DIM + 1, pos % POS_SPLIT, 0))
    k_ref[0] = _bf(r[:, ssm_w:ssm_w + LANES] + pos_cols.astype(F32))
    ki_ref[0] = _bf(r[:, ssm_w + LANES:ssm_w + LANES + IDX_DIM])
    rt = _dot_nt(wt_ref[...], u)
    arow = lax.broadcasted_iota(I32, (QAUG_ROWS, tl), 0)
    for h in range(N_HEADS):
        base = h * LANES
        qt_ref[0, base:base + HEAD_DIM, :] = _bf(rt[h * HEAD_DIM:(h + 1) * HEAD_DIM])
        qt_ref[0, base + HEAD_DIM:base + HEAD_DIM + QAUG_ROWS, :] = _bf(jnp.where(arow < 2, ALIBI_SLOPES[h], 0.0))
        qt_ref[0, base + HEAD_DIM + QAUG_ROWS:base + LANES, :] = jnp.zeros((LANES - HEAD_DIM - QAUG_ROWS, tl), BF16)
    qit_ref[0] = _bf(rt[attn_w:attn_w + idx_w])
    vt_ref[0] = _bf(rt[attn_w + idx_w:attn_w + idx_w + HEAD_DIM])
    wit_ref[0] = rt[attn_w + idx_w + HEAD_DIM:attn_w + idx_w + HEAD_DIM + IDX_HEADS]


def _split_w_in(w_in, d):
    sizes = (SSM_WIDTH, N_HEADS * HEAD_DIM, HEAD_DIM, HEAD_DIM, IDX_HEADS * IDX_DIM, IDX_DIM, IDX_HEADS, d, d)
    offs = [0]
    for s in sizes:
        offs.append(offs[-1] + s)
    return [w_in[:, offs[i]:offs[i + 1]] for i in range(9)]


def _inproj(x, ada3, g1, w_in):
    bsz, seq, d = x.shape
    ssm_w = SSM_WIDTH
    attn_w = N_HEADS * HEAD_DIM
    idx_w = IDX_HEADS * IDX_DIM
    w_ssm, w_q, w_k, w_v, w_qi, w_ki, w_wi, _, _ = _split_w_in(w_in, d)
    zpad = lambda n: jnp.zeros((d, n), F32)
    wbig = jnp.concatenate([w_ssm, w_k, zpad(LANES - HEAD_DIM), w_ki, zpad(LANES - IDX_DIM)], axis=1).astype(BF16)
    wt = jnp.concatenate([w_q * (HEAD_DIM ** -0.5), w_qi * (IDX_DIM ** -0.5), w_v, w_wi,
                          zpad(LANES - HEAD_DIM - IDX_HEADS)], axis=1).T.astype(BF16)
    tl = INPROJ_ROWS
    kern = functools.partial(_inproj_kernel, d=d, ssm_w=ssm_w, attn_w=attn_w, idx_w=idx_w, tl=tl)
    row = lambda w: pl.BlockSpec((1, tl, w), lambda b, l: (b, l, 0))
    colt = lambda h: pl.BlockSpec((1, h, tl), lambda b, l: (b, 0, l))
    full = lambda a: pl.BlockSpec(a.shape, lambda b, l: (0,) * a.ndim)
    return pl.pallas_call(
        kern,
        out_shape=(jax.ShapeDtypeStruct((bsz, seq, ssm_w), F32),
                   jax.ShapeDtypeStruct((bsz, seq, LANES), BF16),
                   jax.ShapeDtypeStruct((bsz, seq, IDX_DIM), BF16),
                   jax.ShapeDtypeStruct((bsz, N_HEADS * LANES, seq), BF16),
                   jax.ShapeDtypeStruct((bsz, idx_w, seq), BF16),
                   jax.ShapeDtypeStruct((bsz, HEAD_DIM, seq), BF16),
                   jax.ShapeDtypeStruct((bsz, IDX_HEADS, seq), F32)),
        grid=(bsz, seq // tl),
        in_specs=[row(d),
                  pl.BlockSpec((1, 1, ada3.shape[2]), lambda b, l: (b, 0, 0)),
                  pl.BlockSpec((1, d), lambda b, l: (0, 0)),
                  full(wbig), full(wt)],
        out_specs=(row(ssm_w), row(LANES), row(IDX_DIM),
                   colt(N_HEADS * LANES), colt(idx_w), colt(HEAD_DIM), colt(IDX_HEADS)),
        compiler_params=_cparams(("arbitrary", "arbitrary")),
        name="inproj",
    )(x, ada3, g1.reshape(1, d), wbig, wt)


def _s5disc_kernel(lr_ref, li_ref, ldt_ref, br_ref, bi_ref, are_ref, aim_ref, bbr_ref, bbi_ref):
    lr = lr_ref[...]
    li = li_ref[...]
    dt = jnp.exp(ldt_ref[...])
    mag = jnp.exp(lr * dt)
    a_re = mag * jnp.cos(li * dt)
    a_im = mag * jnp.sin(li * dt)
    den = lr * lr + li * li
    n_re = a_re - 1.0
    f_re = (n_re * lr + a_im * li) / den
    f_im = (a_im * lr - n_re * li) / den
    br = br_ref[...]
    bi = bi_ref[...]
    are_ref[...] = a_re
    aim_ref[...] = a_im
    bbr_ref[...] = f_re * br - f_im * bi
    bbi_ref[...] = f_re * bi + f_im * br


def _s5disc(lam_re, lam_im, log_dt, b_re, b_im):
    g, p = lam_re.shape
    h = b_re.shape[2]
    rep = lambda a: jnp.repeat(a, h, axis=1)
    ldt = jnp.broadcast_to(log_dt[:, None], (g, p * h))
    sds = jax.ShapeDtypeStruct((g, p * h), F32)
    a_re, a_im, bb_re, bb_im = pl.pallas_call(
        _s5disc_kernel, out_shape=(sds, sds, sds, sds), name="s5disc",
    )(rep(lam_re), rep(lam_im), ldt, b_re.reshape(g, p * h), b_im.reshape(g, p * h))
    return a_re[:, ::h], a_im[:, ::h], bb_re.reshape(g, p, h), bb_im.reshape(g, p, h)


def _s5_kernel(u_ref, wb_ref, ar_ref, ai_ref, cc_ref, dsk_ref, wg_ref, bg_ref, o_ref, buf, hst, *, tl, width):
    nch = width // S5_LANE_CHUNK
    sw = S5_LANE_CHUNK // SSM_GROUP * SSM_STATE
    rows = tl * SUBLANES

    @pl.when(pl.program_id(0) == 0)
    def _():
        hst[...] = jnp.zeros_like(hst)

    u = jnp.swapaxes(u_ref[...], 0, 1).reshape(rows, width)
    ub = _bf(u)
    for j in range(nch):
        buf[:, j * 2 * sw:(j + 1) * 2 * sw] = _dot(ub[:, j * S5_LANE_CHUNK:(j + 1) * S5_LANE_CHUNK], wb_ref[j])

    for j in range(nch):
        re_cols = slice(j * 2 * sw, j * 2 * sw + sw)
        im_cols = slice(j * 2 * sw + sw, (j + 1) * 2 * sw)
        a_re = jnp.broadcast_to(ar_ref[:, j * sw:(j + 1) * sw], (SUBLANES, sw))
        a_im = jnp.broadcast_to(ai_ref[:, j * sw:(j + 1) * sw], (SUBLANES, sw))

        def step(t, carry, re_cols=re_cols, im_cols=im_cols, a_re=a_re, a_im=a_im):
            h_re, h_im = carry
            r0 = pl.multiple_of(t * SUBLANES, SUBLANES)
            n_re = (a_re * h_re - a_im * h_im) + buf[pl.ds(r0, SUBLANES), re_cols]
            n_im = (a_re * h_im + a_im * h_re) + buf[pl.ds(r0, SUBLANES), im_cols]
            buf[pl.ds(r0, SUBLANES), re_cols] = n_re
            buf[pl.ds(r0, SUBLANES), im_cols] = n_im
            return n_re, n_im

        h_re, h_im = lax.fori_loop(0, tl, step, (hst[:, re_cols], hst[:, im_cols]), unroll=True)
        hst[:, re_cols] = h_re
        hst[:, im_cols] = h_im

    ys = [_dot(_bf(buf[:, j * 2 * sw:(j + 1) * 2 * sw]), cc_ref[j]) for j in range(nch)]
    y = jnp.concatenate(ys, axis=1) + dsk_ref[...] * u
    y = jax.nn.gelu(y)
    y = y * jax.nn.sigmoid(_dot(_bf(y), wg_ref[...]) + bg_ref[...])
    o_ref[...] = _bf(jnp.swapaxes(y.reshape(tl, SUBLANES, width), 0, 1))


def _s5(u_t, a_re, a_im, bb_re, bb_im, c_re, c_im, d_skip, w_glu, b_glu):
    bsz, seq, width = u_t.shape
    assert bsz == SUBLANES
    nch = width // S5_LANE_CHUNK
    gpc = S5_LANE_CHUNK // SSM_GROUP
    sw = gpc * SSM_STATE
    eye = jnp.eye(gpc, dtype=F32)

    def bmat(bb):
        t = bb.reshape(nch, gpc, SSM_STATE, SSM_GROUP).transpose(0, 1, 3, 2)
        return jnp.einsum('jghp,gk->jghkp', t, eye).reshape(nch, S5_LANE_CHUNK, sw)

    def cmat(cc):
        t = cc.reshape(nch, gpc, SSM_GROUP, SSM_STATE).transpose(0, 1, 3, 2)
        return jnp.einsum('jgph,gk->jgpkh', t, eye).reshape(nch, sw, S5_LANE_CHUNK)

    wb = jnp.concatenate([bmat(bb_re), bmat(bb_im)], axis=2)
    cc = jnp.concatenate([cmat(c_re), -cmat(c_im)], axis=1)
    tl = S5_STEPS
    full = lambda a: pl.BlockSpec(a.shape, lambda i: (0,) * a.ndim)
    args = (u_t, wb.astype(BF16), a_re.reshape(1, -1), a_im.reshape(1, -1), cc.astype(BF16),
            d_skip.reshape(1, width), w_glu.astype(BF16), b_glu.reshape(1, width))
    return pl.pallas_call(
        functools.partial(_s5_kernel, tl=tl, width=width),
        out_shape=jax.ShapeDtypeStruct((bsz, seq, width), BF16),
        grid=(seq // tl,),
        in_specs=[pl.BlockSpec((bsz, tl, width), lambda i: (0, i, 0))] + [full(a) for a in args[1:]],
        out_specs=pl.BlockSpec((bsz, tl, width), lambda i: (0, i, 0)),
        scratch_shapes=[pltpu.VMEM((tl * SUBLANES, nch * 2 * sw), F32),
                        pltpu.VMEM((SUBLANES, nch * 2 * sw), F32)],
        compiler_params=_cparams(("arbitrary",)),
        name="s5",
    )(*args)


def _bit_transpose32(words):
    x = list(words)
    j, m = 16, 0x0000FFFF
    while j:
        k = 0
        while k < 32:
            t = (x[k] ^ lax.shift_right_logical(x[k + j], jnp.int32(j))) & jnp.int32(m - (1 << 32) if m >= 1 << 31 else m)
            x[k] = x[k] ^ t
            x[k + j] = x[k + j] ^ lax.shift_left(t, jnp.int32(j))
            k = (k + j + 1) & ~j
        j >>= 1
        m = (m ^ (m << j)) & 0xFFFFFFFF
    return x


def _dsa_kernel(qt_ref, qit_ref, wit_ref, ka_ref, ki_ref, vt_ref, o_ref, key_s, mb_s, acc_s, pl_s, p_s, *, tq, tk, topk, seq):
    i = pl.program_id(1)
    q0 = i * tq
    nkt = (q0 + tq + tk - 1) // tk
    ch = DSA_COUNT_ROWS
    krow = lax.broadcasted_iota(I32, (tk, tq), 0)
    qcol = q0 + lax.broadcasted_iota(I32, (tk, tq), 1)
    crow = lax.broadcasted_iota(I32, (ch, tq), 0)

    wb = wit_ref[0] * (IDX_HEADS ** -0.5)

    def score_tile(j, _):
        r0 = pl.multiple_of(j * tk, tk)
        kit = ki_ref[0, pl.ds(r0, tk), :]
        acc = jnp.zeros((tk, tq), F32)
        for h in range(IDX_HEADS):
            s = _dot(kit, qit_ref[0, h * IDX_DIM:(h + 1) * IDX_DIM, :])
            acc = acc + wb[h:h + 1, :] * jnp.maximum(s, 0.0)
        bits = lax.bitcast_convert_type(acc, I32)
        key = jnp.where(bits < 0, bits ^ jnp.int32(0x7FFFFFFF), bits)
        key = jnp.where(acc == 0.0, 0, key)
        key = jnp.where(krow + r0 <= qcol, key, INT_MIN)
        key_s[pl.ds(r0, tk), :] = key
        ukey = key ^ INT_MIN
        for c in range(tk // BITSLICE_ROWS):
            words = [ukey[c * BITSLICE_ROWS + v * SUBLANES:c * BITSLICE_ROWS + (v + 1) * SUBLANES, :]
                     for v in range(32)]
            planes = _bit_transpose32(words)
            g0 = pl.multiple_of((j * (tk // BITSLICE_ROWS) + c) * SUBLANES, SUBLANES)
            for it in range(32):
                pl_s[it, pl.ds(g0, SUBLANES), :] = planes[it]
        return 0

    @pl.when((pl.program_id(0) == 0) & (i == 0))
    def _():
        pl_s[...] = jnp.zeros(pl_s.shape, I32)

    def score_pair(jj, _):
        score_tile(2 * jj, 0)
        score_tile(2 * jj + 1, 0)
        return 0

    lax.fori_loop(0, nkt // 2, score_pair, 0)

    @pl.when(nkt % 2 == 1)
    def _():
        score_tile(nkt - 1, 0)

    def count(pred):
        def tile(j, cnt):
            for c in range(tk // ch):
                rr = pl.multiple_of(j * tk + c * ch, ch)
                cnt = cnt + jnp.where(pred(key_s[pl.ds(rr, ch), :], rr), 1, 0)
            return cnt
        cnt = lax.fori_loop(0, nkt, tile, jnp.zeros((ch, tq), I32))
        return jnp.sum(cnt.astype(F32), axis=0, keepdims=True)

    ngrp = seq // 32

    def lane_count(words):
        pc = lax.population_count(words).reshape(ngrp // SUBLANES, SUBLANES, tq)
        return jnp.sum(jnp.sum(pc, axis=0).astype(F32), axis=0, keepdims=True)

    def bit_step(it, carry):
        alive, above, ans_u = carry
        ones = alive & pl_s[it]
        cnt1 = lane_count(ones)
        take = above + cnt1 >= float(topk)
        alive = jnp.where(take, ones, alive ^ ones)
        above = jnp.where(take, above, above + cnt1)
        ans_u = jnp.where(take, ans_u | lax.shift_left(jnp.int32(1), 31 - it), ans_u)
        return alive, above, ans_u

    grow = lax.broadcasted_iota(I32, (ngrp, tq), 0)
    alive0 = jnp.where(grow < nkt * (tk // 32), -1, 0)
    alive, above, ans_u = lax.fori_loop(
        0, 32, bit_step, (alive0, jnp.zeros((1, tq), F32), jnp.zeros((1, tq), I32)))
    thr = jnp.maximum(ans_u ^ INT_MIN, INT_MIN + 1)
    cnt_ge = above + lane_count(alive)
    tied = jnp.where(ans_u != 0, cnt_ge, 0.0) > float(topk)
    has_ties = jnp.max(jnp.where(tied, 1.0, 0.0)) > 0.0

    def tie_cut():
        need = float(topk) - count(lambda kb, rr: kb > thr)
        nbits = max(1, (seq - 1).bit_length())

        def idx_step(b, x):
            cand = x | lax.shift_left(jnp.int32(1), nbits - 1 - b)
            below = count(lambda kb, rr: jnp.where(kb == thr, crow + rr, seq) < cand)
            return jnp.where(below < need, cand, x)

        x = lax.fori_loop(0, nbits, idx_step, jnp.zeros((1, tq), I32))
        return jnp.where(tied, x, seq)

    cut = lax.cond(has_ties, tie_cut, lambda: jnp.full((1, tq), seq, I32))

    def bias_tile(j, _):
        for c in range(tk // ch):
            rr = pl.multiple_of(j * tk + c * ch, ch)
            kb = key_s[pl.ds(rr, ch), :]
            tie_bias = jnp.where(crow + rr <= cut, 0.0, NEG_BIG)
            mb_s[pl.ds(rr, ch), :] = jnp.where(kb > thr, 0.0, jnp.where(kb == thr, tie_bias, NEG_BIG))
        return 0

    def logits(j, h):
        r0 = pl.multiple_of(j * tk, tk)
        s = _dot(ka_ref[0, pl.ds(r0, tk), :], qt_ref[0, h * LANES:(h + 1) * LANES, :]) + mb_s[pl.ds(r0, tk), :]
        return s.reshape(tk // SUBLANES, SUBLANES, tq)

    acc_s[...] = jnp.zeros(acc_s.shape, F32)

    def attn_tile(j, carry):
        ms, ls = carry
        r0 = pl.multiple_of(j * tk, tk)
        bias_tile(j, 0)
        new_m, new_l, alphas = [], [], []
        for h in range(N_HEADS):
            s = logits(j, h)
            m_new = jnp.maximum(ms[h], jnp.max(jnp.max(s, axis=0), axis=0, keepdims=True))
            alpha = jnp.exp(ms[h] - m_new)
            p = jnp.exp(s - m_new)
            new_m.append(m_new)
            new_l.append(alpha * ls[h] + jnp.sum(p, axis=0))
            alphas.append(alpha)
            p_s[h] = _bf(p.reshape(tk, tq))
        for h in range(N_HEADS):
            rows = slice(h * HEAD_DIM, (h + 1) * HEAD_DIM)
            acc_s[rows, :] = alphas[h] * acc_s[rows, :] + _dot(vt_ref[0, :, pl.ds(r0, tk)], p_s[h])
        return tuple(new_m), tuple(new_l)

    init = ((jnp.full((1, tq), NEG_BIG, F32),) * N_HEADS, (jnp.zeros((SUBLANES, tq), F32),) * N_HEADS)
    _, ls = lax.fori_loop(0, nkt, attn_tile, init)
    for h in range(N_HEADS):
        rows = slice(h * HEAD_DIM, (h + 1) * HEAD_DIM)
        acc_s[rows, :] = acc_s[rows, :] / jnp.sum(ls[h], axis=0, keepdims=True)
    o_ref[0] = _bf(acc_s[...].T)


def _dsa(qt, qit, wit, ka, ki, vt):
    bsz, seq = ka.shape[0], ka.shape[1]
    aw = N_HEADS * HEAD_DIM
    tq = min(DSA_Q_COLS, seq)
    tk = min(DSA_K_ROWS, seq)
    topk = min(TOPK_MAX, seq // 4)
    assert (seq - 1) // POS_SPLIT < 256 and POS_SPLIT <= 256, "key positions must split into two bf16-exact parts"
    assert all(float(np.float32(sl).astype(BF16)) == sl for sl in ALIBI_SLOPES), "ALiBi slopes must be bf16-exact"
    kern = functools.partial(_dsa_kernel, tq=tq, tk=tk, topk=topk, seq=seq)
    cols = lambda r: pl.BlockSpec((1, r, tq), lambda b, i: (b, 0, i))
    return pl.pallas_call(
        kern,
        out_shape=jax.ShapeDtypeStruct((bsz, seq, aw), BF16),
        grid=(bsz, seq // tq),
        in_specs=[cols(N_HEADS * LANES), cols(qit.shape[1]), cols(IDX_HEADS),
                  pl.BlockSpec((1, seq, LANES), lambda b, i: (b, 0, 0)),
                  pl.BlockSpec((1, seq, IDX_DIM), lambda b, i: (b, 0, 0)),
                  pl.BlockSpec((1, HEAD_DIM, seq), lambda b, i: (b, 0, 0))],
        out_specs=pl.BlockSpec((1, tq, aw), lambda b, i: (b, i, 0)),
        scratch_shapes=[pltpu.VMEM((seq, tq), I32), pltpu.VMEM((seq, tq), F32), pltpu.VMEM((aw, tq), F32),
                        pltpu.VMEM((32, seq // 32, tq), I32), pltpu.VMEM((N_HEADS, tk, tq), BF16)],
        compiler_params=_cparams(("arbitrary", "arbitrary")),
        name="dsa",
    )(qt, qit, wit, ka, ki, vt)


def _mix_kernel(x_ref, ys_ref, ya_ref, ada_ref, g1_ref, wgt_ref, wps_ref, wpa_ref, wo_ref, g2_ref,
                h_ref, u2_ref, *, d):
    gate1 = ada_ref[0, :, 2 * d:3 * d]
    shift2 = ada_ref[0, :, 3 * d:4 * d]
    scale2 = ada_ref[0, :, 4 * d:5 * d]
    x = x_ref[0]
    u = _bf(_rms(x, g1_ref[...]) * (1.0 + ada_ref[0, :, d:2 * d]) + ada_ref[0, :, 0:d])
    g = _dot(u, wgt_ref[...])
    mixed = (jax.nn.sigmoid(g[:, 0:d]) * _dot(ys_ref[0], wps_ref[...])
             + jax.nn.sigmoid(g[:, d:2 * d]) * _dot(ya_ref[0], wpa_ref[...]))
    h = x + gate1 * _dot(_bf(mixed), wo_ref[...])
    h_ref[0] = h
    u2_ref[0] = _rms(h, g2_ref[...]) * (1.0 + scale2) + shift2


def _mix(x, ys, ya, ada3, g1, w_in, wps, wpa, wo, g2):
    bsz, seq, d = x.shape
    tm = MIX_ROWS
    row = lambda w: pl.BlockSpec((1, tm, w), lambda b, l: (b, l, 0))
    full = lambda a: pl.BlockSpec(a.shape, lambda b, l: (0,) * a.ndim)
    wps, wpa, wo = wps.astype(BF16), wpa.astype(BF16), wo.astype(BF16)
    wgt = jnp.concatenate(_split_w_in(w_in, d)[7:9], axis=1).astype(BF16)
    g1 = g1.reshape(1, d)
    g2 = g2.reshape(1, d)
    return pl.pallas_call(
        functools.partial(_mix_kernel, d=d),
        out_shape=(jax.ShapeDtypeStruct((bsz, seq, d), F32), jax.ShapeDtypeStruct((bsz, seq, d), F32)),
        grid=(bsz, seq // tm),
        in_specs=[row(d), row(ys.shape[2]), row(ya.shape[2]),
                  pl.BlockSpec((1, 1, ada3.shape[2]), lambda b, l: (b, 0, 0)),
                  full(g1), full(wgt), full(wps), full(wpa), full(wo), full(g2)],
        out_specs=(row(d), row(d)),
        compiler_params=_cparams(("arbitrary", "arbitrary")),
        name="mix",
    )(x, ys, ya, ada3, g1, wgt, wps, wpa, wo, g2)


def _first_max(cur, idx, axis, big):
    m = jnp.max(cur, axis=axis, keepdims=True)
    first = jnp.min(jnp.where(cur == m, idx, big), axis=axis, keepdims=True)
    return m, idx == first


def _route_kernel(u_ref, wrh_ref, wrl_ref, rb_ref, tri_ref, ltri_ref,
                  gt_ref, loc_ref, c8_ref, loff_ref, run0_ref, tot_ref, run_s, *, t, tt):
    @pl.when(pl.program_id(0) == 0)
    def _():
        run_s[...] = jnp.zeros_like(run_s)

    uh, ul = _split(u_ref[...])
    logits = _dot_nt(wrh_ref[...], uh) + (_dot_nt(wrl_ref[...], uh) + _dot_nt(wrh_ref[...], ul))
    scores = jax.nn.sigmoid(logits)
    biased = scores + rb_ref[...]
    per_group = N_EXPERTS // N_GROUPS
    b3 = biased.reshape(N_GROUPS, per_group, t)
    i3 = lax.broadcasted_iota(I32, b3.shape, 1)
    m1, hit1 = _first_max(b3, i3, 1, per_group)
    m2 = jnp.max(jnp.where(hit1, -jnp.inf, b3), axis=1, keepdims=True)
    gs = (m1 + m2).reshape(N_GROUPS, t)
    gi = lax.broadcasted_iota(I32, gs.shape, 0)
    gsel = jnp.zeros(gs.shape, F32)
    for _ in range(TOPK_GROUPS):
        _, hit = _first_max(gs, gi, 0, N_GROUPS)
        gsel = jnp.where(hit, 1.0, gsel)
        gs = jnp.where(hit, -jnp.inf, gs)
    cur = jnp.where(gsel.reshape(N_GROUPS, 1, t) > 0.0, b3, -jnp.inf).reshape(N_EXPERTS, t)
    ei = lax.broadcasted_iota(I32, cur.shape, 0)
    hits = []
    gates = []
    for _ in range(TOP_K):
        _, hit = _first_max(cur, ei, 0, N_EXPERTS)
        hits.append(hit)
        gates.append(jnp.sum(jnp.where(hit, scores, 0.0), axis=0, keepdims=True))
        cur = jnp.where(hit, -jnp.inf, cur)
    gate = jnp.concatenate(gates, axis=0)
    gt_ref[...] = gate / jnp.sum(gate, axis=0, keepdims=True) * ROUTED_SCALE
    onehot = jnp.zeros(cur.shape, F32)
    for hit in hits:
        onehot = jnp.where(hit, 1.0, onehot)
    for sub in range(t // tt):
        cols = slice(sub * tt, (sub + 1) * tt)
        oh = onehot[:, cols]
        cnt = jnp.sum(oh, axis=1, keepdims=True)
        c8 = jnp.floor((cnt + (SUBLANES - 1)) * (1.0 / SUBLANES)) * SUBLANES
        c8l = jnp.broadcast_to(c8, (N_EXPERTS, LANES))
        loff = _dot(ltri_ref[...], _bf(c8l))
        slot = _dot(_bf(oh), tri_ref[...]) + loff[:, 0:1]
        loc_ref[:, cols] = jnp.concatenate(
            [jnp.sum(jnp.where(hit[:, cols], slot, 0.0), axis=0, keepdims=True) for hit in hits],
            axis=0).astype(I32)
        c8_ref[sub] = c8l
        loff_ref[sub] = loff
        run0_ref[sub] = run_s[...]
        run_s[...] = run_s[...] + c8
    tot_ref[...] = run_s[...]


def _route(u2, w_router, router_bias):
    n, d = u2.shape
    tt = min(MOE_TILE, n)
    t = min(ROUTE_TILES * tt, n)
    nt = n // tt
    wt = w_router.T
    wrh = wt.astype(BF16)
    wrl = (wt - wrh.astype(F32)).astype(BF16)
    tri = (jnp.arange(tt)[:, None] < jnp.arange(tt)[None, :]).astype(BF16)
    ex = jnp.arange(N_EXPERTS)
    ltri = (ex[None, :] < ex[:, None]).astype(BF16)
    full = lambda a: pl.BlockSpec(a.shape, lambda i: (0,) * a.ndim)
    col = pl.BlockSpec((TOP_K, t), lambda i: (0, i))
    tab = pl.BlockSpec((t // tt, N_EXPERTS, LANES), lambda i: (i, 0, 0))
    tab_sds = jax.ShapeDtypeStruct((nt, N_EXPERTS, LANES), F32)
    rb = router_bias.reshape(N_EXPERTS, 1)
    return pl.pallas_call(
        functools.partial(_route_kernel, t=t, tt=tt),
        out_shape=(jax.ShapeDtypeStruct((TOP_K, n), F32), jax.ShapeDtypeStruct((TOP_K, n), I32),
                   tab_sds, tab_sds, tab_sds, jax.ShapeDtypeStruct((N_EXPERTS, LANES), F32)),
        grid=(n // t,),
        in_specs=[pl.BlockSpec((t, d), lambda i: (i, 0)), full(wrh), full(wrl), full(rb), full(tri), full(ltri)],
        out_specs=(col, col, tab, tab, tab, pl.BlockSpec((N_EXPERTS, LANES), lambda i: (0, 0))),
        scratch_shapes=[pltpu.VMEM((N_EXPERTS, LANES), F32)],
        compiler_params=_cparams(("arbitrary",)),
        name="route",
    )(u2, wrh, wrl, rb, tri, ltri)


RUN_BITS = tuple(1 << b for b in reversed(range((MOE_TILE // SUBLANES).bit_length())))


def _for_each_run_piece(n8_ref, src_ref, dst_ref, tile, bits, fn):
    def per_expert(e, _):
        idx = tile * N_EXPERTS + e
        n8 = n8_ref[idx]
        src = src_ref[idx]
        dst = dst_ref[idx]
        for p in bits:
            off = (n8 & ~(2 * p - 1)) * SUBLANES

            @pl.when((n8 & p) != 0)
            def _(p=p, off=off):
                fn(pl.multiple_of(src + off, SUBLANES), pl.multiple_of(dst + off, SUBLANES), p * SUBLANES)
        return 0

    lax.fori_loop(0, N_EXPERTS, per_expert, 0)


def _issue_runs_inline(n8_ref, src_ref, dst_ref, tile, enable, fn, experts=range(N_EXPERTS)):
    for e in experts:
        idx = tile * N_EXPERTS + e
        n8 = jnp.where(enable, n8_ref[idx], 0)
        src = src_ref[idx]
        dst = dst_ref[idx]
        for p in RUN_BITS:
            off = (n8 & ~(2 * p - 1)) * SUBLANES

            @pl.when((n8 & p) != 0)
            def _(p=p, off=off, src=src, dst=dst):
                fn(pl.multiple_of(src + off, SUBLANES), pl.multiple_of(dst + off, SUBLANES), p * SUBLANES)


def _wait_rows(n8, make_copy, max_rows):
    for p in tuple(1 << b for b in reversed(range((max_rows // SUBLANES).bit_length()))):
        @pl.when((n8 & p) != 0)
        def _(p=p):
            make_copy(p * SUBLANES).wait()


def _dispatch_kernel(n8_ref, src_ref, dst_ref, tot_ref, zn8_ref, zdst_ref, u_ref, loc_ref, gate_ref, xs_hbm,
                     lbuf0, lbuf1, lbuf2, zx, sems, *, tt, nslot, dh):
    i = pl.program_id(0)
    last = pl.num_programs(0) - 1
    bufs = (lbuf0, lbuf1, lbuf2)
    nbuf = len(bufs)
    zsem = nbuf

    def copy_from(slot):
        def piece(s0, d0, rows):
            pltpu.make_async_copy(bufs[slot].at[pl.ds(s0, rows)], xs_hbm.at[pl.ds(d0, rows)], sems.at[slot]).start()
        return piece

    def wait_tile(tile, slot):
        _wait_rows(tot_ref[tile], lambda rows: pltpu.make_async_copy(
            bufs[slot].at[pl.ds(0, rows)], xs_hbm.at[pl.ds(0, rows)], sems.at[slot]), nslot)

    @pl.when(i == 0)
    def _():
        zx[...] = jnp.zeros(zx.shape, I32)

        def zero_piece(s0, d0, rows):
            cx = pltpu.make_async_copy(zx.at[pl.ds(0, rows)], xs_hbm.at[pl.ds(d0, rows)], sems.at[zsem])
            cx.start()
            cx.wait()

        zbits = tuple(1 << b for b in reversed(range((EXPERT_ROWS // SUBLANES - 1).bit_length())))
        _for_each_run_piece(zn8_ref, zdst_ref, zdst_ref, 0, zbits, zero_piece)

    def build(buf):
        ub = _bf(u_ref[...])
        ones = jnp.ones((tt, LANES), BF16)
        loc = loc_ref[...]
        gate = gate_ref[...]
        rows_b = lax.broadcasted_iota(I32, (SLOT_CHUNK, tt), 0).astype(F32).astype(BF16)
        loc_hi = lax.shift_right_logical(loc, SLOT_CHUNK.bit_length() - 1)
        loc_lo = (loc & (SLOT_CHUNK - 1)).astype(F32)
        gate_h = _bf(gate)
        gate_l = _bf(gate - gate_h.astype(F32))
        one_b = jnp.ones((SLOT_CHUNK, tt), BF16)
        for c in range(nslot // SLOT_CHUNK):
            perm = jnp.zeros((SLOT_CHUNK, tt), BF16)
            pgh = jnp.zeros((SLOT_CHUNK, tt), BF16)
            pgl = jnp.zeros((SLOT_CHUNK, tt), BF16)
            for k in range(TOP_K):
                lk = _bf(jnp.where(loc_hi[k:k + 1, :] == c, loc_lo[k:k + 1, :], -1.0))
                eq = rows_b == lk
                perm = jnp.where(eq, one_b, perm)
                pgh = jnp.where(eq, jnp.broadcast_to(gate_h[k:k + 1, :], (SLOT_CHUNK, tt)), pgh)
                pgl = jnp.where(eq, jnp.broadcast_to(gate_l[k:k + 1, :], (SLOT_CHUNK, tt)), pgl)
            cs = slice(c * SLOT_CHUNK, (c + 1) * SLOT_CHUNK)
            xp = lax.bitcast_convert_type(_dot(perm, ub), I32)
            buf[cs, 0:dh] = xp[:, 0:dh] | lax.shift_right_logical(xp[:, dh:2 * dh], 16)
            buf[cs, dh:dh + LANES] = lax.bitcast_convert_type(_dot(pgh, ones) + _dot(pgl, ones), I32)

    def step(cur):
        prv, prv2 = (cur - 1) % nbuf, (cur - 2) % nbuf

        @pl.when(i >= nbuf)
        def _():
            wait_tile(i - nbuf, cur)

        _issue_runs_inline(n8_ref, src_ref, dst_ref, jnp.maximum(i - 1, 0), i >= 1, copy_from(prv))
        build(bufs[cur])

        @pl.when(i == last)
        def _():
            _for_each_run_piece(n8_ref, src_ref, dst_ref, i, RUN_BITS, copy_from(cur))

            @pl.when(i >= 2)
            def _():
                wait_tile(i - 2, prv2)

            @pl.when(i >= 1)
            def _():
                wait_tile(i - 1, prv)
            wait_tile(i, cur)

    for cur in range(nbuf):
        pl.when(i % nbuf == cur)(functools.partial(step, cur))


def _dispatch(tabs, ztabs, u2, loc_t, gate_t, n_rows):
    n, d = u2.shape
    tt = min(MOE_TILE, n)
    nslot = TOP_K * tt + N_EXPERTS * SUBLANES
    dh = d // 2
    assert nslot % SLOT_CHUNK == 0 and tt // SUBLANES == RUN_BITS[0]
    col = pl.BlockSpec((TOP_K, tt), lambda i, *_: (0, i))
    return pl.pallas_call(
        functools.partial(_dispatch_kernel, tt=tt, nslot=nslot, dh=dh),
        out_shape=jax.ShapeDtypeStruct((n_rows, dh + LANES), I32),
        grid_spec=pltpu.PrefetchScalarGridSpec(
            num_scalar_prefetch=6, grid=(n // tt,),
            in_specs=[pl.BlockSpec((tt, d), lambda i, *_: (i, 0)), col, col],
            out_specs=pl.BlockSpec(memory_space=pl.ANY),
            scratch_shapes=[pltpu.VMEM((nslot, dh + LANES), I32)] * 3 + [
                pltpu.VMEM((EXPERT_ROWS // 2, dh + LANES), I32), pltpu.SemaphoreType.DMA((4,))]),
        compiler_params=_cparams(("arbitrary",)),
        name="dispatch",
    )(*tabs, *ztabs, u2, loc_t, gate_t)


def _experts_kernel(be_ref, nu_ref, first_ref, ord_ref, nxt_ref, xs_hbm, wg_hbm, wu_hbm, wd_hbm, ys_ref,
                    xbuf, wgb, wub, wdb, sems, wsems, *, d):
    dh = d // 2
    i = pl.program_id(0)
    nu = nu_ref[0]
    ahead = EXPERT_RING - 1

    def wcopies(e, slot):
        return (pltpu.make_async_copy(wg_hbm.at[e], wgb.at[slot], wsems.at[slot]),
                pltpu.make_async_copy(wu_hbm.at[e], wub.at[slot], wsems.at[slot]),
                pltpu.make_async_copy(wd_hbm.at[e], wdb.at[slot], wsems.at[slot]))

    def fetch(j):
        return pltpu.make_async_copy(xs_hbm.at[pl.ds(pl.multiple_of(j * EXPERT_ROWS, EXPERT_ROWS), EXPERT_ROWS)],
                                     xbuf.at[j % EXPERT_RING], sems.at[j % EXPERT_RING])

    @pl.when(i == 0)
    def _():
        for j in range(ahead):
            pl.when(j < nu)(lambda j=j: fetch(j).start())
        for cp in wcopies(be_ref[0], 0):
            cp.start()

    @pl.when(i + ahead < nu)
    def _():
        fetch(i + ahead).start()

    @pl.when(i < nu)
    def _():
        ws = ord_ref[i] % 2

        @pl.when(first_ref[i] == 1)
        def _():
            for cp in wcopies(be_ref[i], ws):
                cp.wait()

            @pl.when(nxt_ref[i] >= 0)
            def _():
                for cp in wcopies(nxt_ref[i], 1 - ws):
                    cp.start()

        fetch(i).wait()
        xs_ref = xbuf.at[i % EXPERT_RING]
        w = xs_ref[:, 0:dh]
        xa = _bf(lax.bitcast_convert_type(w & jnp.int32(-65536), F32))
        xb = _bf(lax.bitcast_convert_type(lax.shift_left(w, 16), F32))
        gate = lax.bitcast_convert_type(xs_ref[:, dh:dh + LANES], F32)
        hg = _dot(xa, _bf(wgb[ws, 0:dh, :])) + _dot(xb, _bf(wgb[ws, dh:d, :]))
        hu = _dot(xa, _bf(wub[ws, 0:dh, :])) + _dot(xb, _bf(wub[ws, dh:d, :]))
        y = _dot(_bf(jax.nn.silu(hg) * hu), _bf(wdb[ws])) * jnp.tile(gate, (1, d // LANES))
        yb = lax.bitcast_convert_type(_bf(y).astype(F32), I32)
        ys_ref[...] = yb[:, 0:dh] | lax.shift_right_logical(yb[:, dh:d], 16)


def _experts(blk_expert, n_used, xs, wg, wu, wd):
    rows, xw = xs.shape
    d = wg.shape[1]
    de = wg.shape[2]
    nblk = rows // EXPERT_ROWS
    blk = lambda i, be, nu, *_: jnp.minimum(i, nu[0] - 1)
    idx = jnp.arange(nblk, dtype=I32)
    first = ((idx == 0) | (blk_expert != jnp.roll(blk_expert, 1))) & (idx < n_used[0])
    ordinal = jnp.maximum(jnp.cumsum(first.astype(I32)) - 1, 0)
    first_at = lax.cummin(jnp.where(first, idx, nblk), reverse=True)
    next_first = jnp.concatenate([first_at[1:], jnp.full((1,), nblk, I32)])
    nxt = jnp.where(next_first < nblk, blk_expert[jnp.minimum(next_first, nblk - 1)], -1).astype(I32)
    hbm = pl.BlockSpec(memory_space=pl.ANY)
    return pl.pallas_call(
        functools.partial(_experts_kernel, d=d),
        out_shape=jax.ShapeDtypeStruct((rows, d // 2), I32),
        grid_spec=pltpu.PrefetchScalarGridSpec(
            num_scalar_prefetch=5, grid=(nblk,),
            in_specs=[hbm, hbm, hbm, hbm],
            out_specs=pl.BlockSpec((EXPERT_ROWS, d // 2), lambda i, be, nu, *_: (blk(i, be, nu), 0)),
            scratch_shapes=[pltpu.VMEM((EXPERT_RING, EXPERT_ROWS, xw), I32),
                            pltpu.VMEM((2, d, de), F32), pltpu.VMEM((2, d, de), F32), pltpu.VMEM((2, de, d), F32),
                            pltpu.SemaphoreType.DMA((EXPERT_RING,)), pltpu.SemaphoreType.DMA((2,))]),
        compiler_params=_cparams(("arbitrary",)),
        name="experts",
    )(blk_expert, n_used, first.astype(I32), ordinal, nxt, xs, wg, wu, wd)


def _combine_kernel(n8_ref, src_ref, dst_ref, tot_ref, ys_hbm, loc_ref, h_ref, u2_ref, ada_ref, wsg_ref, wsu_ref, wsd_ref,
                    gf_ref, o_ref, ybuf0, ybuf1, pick_s, sems, *, tt, nslot, d):
    i = pl.program_id(0)
    last = pl.num_programs(0) - 1
    bufs = (ybuf0, ybuf1)
    dh = d // 2
    nchunk = nslot // SLOT_CHUNK

    def copy_into(slot):
        def piece(s0, d0, rows):
            pltpu.make_async_copy(ys_hbm.at[pl.ds(d0, rows)], bufs[slot].at[pl.ds(s0, rows)], sems.at[slot]).start()
        return piece

    @pl.when(i == 0)
    def _():
        ybuf0[...] = jnp.zeros(ybuf0.shape, I32)
        ybuf1[...] = jnp.zeros(ybuf1.shape, I32)
        _for_each_run_piece(n8_ref, src_ref, dst_ref, 0, RUN_BITS, copy_into(0))

    def step(cur, nxt):
        nxt_tile = jnp.minimum(i + 1, last)
        _issue_runs_inline(n8_ref, src_ref, dst_ref, nxt_tile, i < last, copy_into(nxt), range(N_EXPERTS // 2))
        x = _bf(u2_ref[...])
        shared = _dot(_bf(jax.nn.silu(_dot(x, wsg_ref[...])) * _dot(x, wsu_ref[...])), wsd_ref[...])
        loc = loc_ref[...]
        cols_b = lax.broadcasted_iota(I32, (tt, SLOT_CHUNK), 1).astype(F32).astype(BF16)
        loc_hi = lax.shift_right_logical(loc, SLOT_CHUNK.bit_length() - 1)
        loc_lo = (loc & (SLOT_CHUNK - 1)).astype(F32)
        one_b = jnp.ones((tt, SLOT_CHUNK), BF16)
        for c in range(nchunk):
            pick = jnp.zeros((tt, SLOT_CHUNK), BF16)
            for k in range(TOP_K):
                lk = _bf(jnp.where(loc_hi[:, k:k + 1] == c, loc_lo[:, k:k + 1], -1.0))
                pick = jnp.where(cols_b == lk, one_b, pick)
            pick_s[c] = pick
        _wait_rows(tot_ref[i], lambda rows: pltpu.make_async_copy(
            ys_hbm.at[pl.ds(0, rows)], bufs[cur].at[pl.ds(0, rows)], sems.at[cur]), nslot)
        _issue_runs_inline(n8_ref, src_ref, dst_ref, nxt_tile, i < last, copy_into(nxt),
                           range(N_EXPERTS // 2, N_EXPERTS))
        routed_a = jnp.zeros((tt, dh), F32)
        routed_b = jnp.zeros((tt, dh), F32)
        for c in range(nchunk):
            w = bufs[cur][c * SLOT_CHUNK:(c + 1) * SLOT_CHUNK, :]
            routed_a = routed_a + _dot(pick_s[c], _bf(lax.bitcast_convert_type(w & jnp.int32(-65536), F32)))
            routed_b = routed_b + _dot(pick_s[c], _bf(lax.bitcast_convert_type(lax.shift_left(w, 16), F32)))
        routed = jnp.concatenate([routed_a, routed_b], axis=1)
        gate2 = ada_ref[0, :, 5 * d:6 * d]
        h = h_ref[...] + gate2 * (routed + shared)
        o_ref[...] = _rms(h, gf_ref[...])

    pl.when(i % 2 == 0)(lambda: step(0, 1))
    pl.when(i % 2 == 1)(lambda: step(1, 0))


def _combine(tabs, ys, loc, h1, u2, ada3, wsg, wsu, wsd, gf, seq):
    n, d = h1.shape
    tt = min(MOE_TILE, n)
    nslot = TOP_K * tt + N_EXPERTS * SUBLANES
    per_b = seq // tt
    row = pl.BlockSpec((tt, d), lambda i, *_: (i, 0))
    full = lambda a: pl.BlockSpec(a.shape, lambda i, *_: (0,) * a.ndim)
    wsg, wsu, wsd = wsg.astype(BF16), wsu.astype(BF16), wsd.astype(BF16)
    gf = gf.reshape(1, d)
    return pl.pallas_call(
        functools.partial(_combine_kernel, tt=tt, nslot=nslot, d=d),
        out_shape=jax.ShapeDtypeStruct((n, d), F32),
        grid_spec=pltpu.PrefetchScalarGridSpec(
            num_scalar_prefetch=4, grid=(n // tt,),
            in_specs=[pl.BlockSpec(memory_space=pl.ANY),
                      pl.BlockSpec((tt, TOP_K), lambda i, *_: (i, 0)),
                      row, row,
                      pl.BlockSpec((1, 1, ada3.shape[2]), lambda i, *_: (i // per_b, 0, 0)),
                      full(wsg), full(wsu), full(wsd), full(gf)],
            out_specs=row,
            scratch_shapes=[pltpu.VMEM((nslot, d // 2), I32), pltpu.VMEM((nslot, d // 2), I32),
                            pltpu.VMEM((nslot // SLOT_CHUNK, tt, SLOT_CHUNK), BF16),
                            pltpu.SemaphoreType.DMA((2,))]),
        compiler_params=_cparams(("arbitrary",)),
        name="combine",
    )(*tabs, ys, loc, h1, u2, ada3, wsg, wsu, wsd, gf)


def _moe(h1, u2, ada3, w_router, router_bias, wg, wu, wd, wsg, wsu, wsd, gf):
    bsz, seq, d = h1.shape
    n = bsz * seq
    assert seq % min(MOE_TILE, n) == 0
    h1f = h1.reshape(n, d)
    u2f = u2.reshape(n, d)
    gate_t, loc_t, c8, loff, run0, tot = _route(u2f, w_router, router_bias)
    nt = c8.shape[0]
    as_tab = lambda a: a[:, :, 0].astype(I32)
    tot8 = tot[:, 0].astype(I32)
    padded = (tot8 + EXPERT_ROWS - 1) // EXPERT_ROWS * EXPERT_ROWS
    pend = jnp.cumsum(padded)
    pstart = (pend - padded).astype(I32)
    nblk = (n * TOP_K + nt * N_EXPERTS * (SUBLANES - 1) + N_EXPERTS * (EXPERT_ROWS - 1) + EXPERT_ROWS - 1) // EXPERT_ROWS
    blk_row0 = jnp.arange(nblk, dtype=I32) * EXPERT_ROWS
    blk_expert = jnp.minimum(jnp.sum(pend[None, :] <= blk_row0[:, None], axis=1), N_EXPERTS - 1).astype(I32)
    n_used = (pend[-1:] // EXPERT_ROWS).astype(I32)
    n8 = as_tab(c8) // SUBLANES
    tabs = (n8.reshape(-1), as_tab(loff).reshape(-1), (pstart[None, :] + as_tab(run0)).reshape(-1),
            jnp.sum(n8, axis=1))
    ztabs = ((padded - tot8) // SUBLANES, pstart + tot8)
    xs = _dispatch(tabs, ztabs, u2f, loc_t, gate_t, nblk * EXPERT_ROWS)
    ys = _experts(blk_expert, n_used, xs, wg, wu, wd)
    out = _combine(tabs, ys, loc_t.T, h1f, u2f, ada3, wsg, wsu, wsd, gf, seq)
    return out.reshape(bsz, seq, d)


def kernel(x, c, w_ada, b_ada, norm1_g, w_in, ssm_lambda_re, ssm_lambda_im, ssm_log_dt, ssm_b_re, ssm_b_im,
           ssm_c_re, ssm_c_im, ssm_d, ssm_w_glu, ssm_b_glu, w_proj_ssm, w_proj_attn, w_out, norm2_g, w_router,
           router_bias, w_exp_gate, w_exp_up, w_exp_down, w_sh_gate, w_sh_up, w_sh_down, norm_f_g):
    depth = w_ada.shape[0]
    assert depth == 1, "the final norm is fused into the last (only) layer's combine kernel"
    bsz, seq, d = x.shape
    layer = 0
    ada3 = _ada(c, w_ada[layer], b_ada[layer]).reshape(bsz, 1, 6 * d)
    us, ka, ki, qt, qit, vt, wit = _inproj(x, ada3, norm1_g[layer], w_in[layer])
    a_re, a_im, bb_re, bb_im = _s5disc(ssm_lambda_re[layer], ssm_lambda_im[layer], ssm_log_dt[layer],
                                       ssm_b_re[layer], ssm_b_im[layer])
    ys_t = _s5(us, a_re, a_im, bb_re, bb_im, ssm_c_re[layer], ssm_c_im[layer],
               ssm_d[layer], ssm_w_glu[layer], ssm_b_glu[layer])
    ya = _dsa(qt, qit, wit, ka, ki, vt)
    h1, u2 = _mix(x, ys_t, ya, ada3, norm1_g[layer], w_in[layer], w_proj_ssm[layer],
                  w_proj_attn[layer], w_out[layer], norm2_g[layer])
    return _moe(h1, u2, ada3, w_router[layer], router_bias[layer], w_exp_gate[layer], w_exp_up[layer],
                w_exp_down[layer], w_sh_gate[layer], w_sh_up[layer], w_sh_down[layer], norm_f_g)
```

```python
import functools

import jax
import jax.numpy as jnp
import numpy as np
from jax import lax
from jax.experimental import pallas as pl
from jax.experimental.pallas import tpu as pltpu

F32 = jnp.float32
BF16 = jnp.bfloat16
I32 = jnp.int32

SSM_GROUP = 16
SSM_STATE = 64
N_HEADS = 8
HEAD_DIM = 64
IDX_HEADS = 8
IDX_DIM = 64
TOPK_MAX = 256
N_EXPERTS = 64
TOP_K = 8
N_GROUPS = 8
TOPK_GROUPS = 4
ROUTED_SCALE = 2.5
EPS = 1e-6
SSM_WIDTH = 512

V7X_VMEM_LIMIT_BYTES = 56 * 1024 * 1024
LANES = 128
SUBLANES = 8

ADA_COLS = 1024
INPROJ_ROWS = 512
S5_STEPS = 64
S5_LANE_CHUNK = 128
DSA_Q_COLS = 256
DSA_K_ROWS = 512
DSA_COUNT_ROWS = 64
BITSLICE_ROWS = 256
POS_SPLIT = 64
MIX_ROWS = 512
MOE_TILE = 256
ROUTE_TILES = 4
SLOT_CHUNK = 256
EXPERT_ROWS = 1024
EXPERT_RING = 3

NEG_BIG = -1e30
INT_MIN = -(2 ** 31)


def _cparams(sem):
    return pltpu.CompilerParams(dimension_semantics=sem, vmem_limit_bytes=V7X_VMEM_LIMIT_BYTES)


def _bf(x):
    return x.astype(BF16)


def _dot(a, b):
    return jnp.dot(a, b, preferred_element_type=F32)


def _dot_nt(a, b):
    return lax.dot_general(a, b, (((1,), (1,)), ((), ())), preferred_element_type=F32)


def _split(x):
    hi = _bf(x)
    lo = _bf(x - hi.astype(F32))
    return hi, lo


def _dot3(a, b):
    ah, al = _split(a)
    bh, bl = _split(b)
    return _dot(ah, bh) + (_dot(ah, bl) + _dot(al, bh))


def _rms(x, g):
    return x * lax.rsqrt(jnp.mean(x * x, axis=-1, keepdims=True) + EPS) * g


def _ada_kernel(c_ref, w_ref, b_ref, o_ref):
    c = c_ref[...]
    o_ref[...] = _dot3(c * jax.nn.sigmoid(c), w_ref[...]) + b_ref[...]


def _ada(c, w, b):
    bsz, d = c.shape
    n = w.shape[1]
    tn = ADA_COLS
    return pl.pallas_call(
        _ada_kernel,
        out_shape=jax.ShapeDtypeStruct((bsz, n), F32),
        grid=(n // tn,),
        in_specs=[pl.BlockSpec((bsz, d), lambda j: (0, 0)),
                  pl.BlockSpec((d, tn), lambda j: (0, j)),
                  pl.BlockSpec((1, tn), lambda j: (0, j))],
        out_specs=pl.BlockSpec((bsz, tn), lambda j: (0, j)),
        compiler_params=_cparams(("arbitrary",)),
        name="ada",
    )(c, w, b.reshape(1, n))


ALIBI_SLOPES = tuple(2.0 ** (-8.0 * (h + 1) / N_HEADS) for h in range(N_HEADS))
QAUG_ROWS = 16


def _inproj_kernel(x_ref, ada_ref, g1_ref, w_ref, wt_ref,
                   us_ref, k_ref, ki_ref, qt_ref, qit_ref, vt_ref, wit_ref, *, d, ssm_w, attn_w, idx_w, tl):
    x = x_ref[0]
    shift = ada_ref[0, :, 0:d]
    scale = ada_ref[0, :, d:2 * d]
    u = _bf(_rms(x, g1_ref[...]) * (1.0 + scale) + shift)
    r = _dot(u, w_ref[...])
    us_ref[0] = r[:, 0:ssm_w]
    pos = pl.program_id(1) * tl + lax.broadcasted_iota(I32, (tl, LANES), 0)
    lane = lax.broadcasted_iota(I32, (tl, LANES), 1)
    pos_cols = jnp.where(lane == HEAD_DIM, pos // POS_SPLIT * POS_SPLIT, jnp.where(lane == HEAD_DIM + 1, pos % POS_SPLIT, 0))
    k_ref[0] = _bf(r[:, ssm_w:ssm_w + LANES] + pos_cols.astype(F32))
    ki_ref[0] = _bf(r[:, ssm_w + LANES:ssm_w + LANES + IDX_DIM])
    rt = _dot_nt(wt_ref[...], u)
    arow = lax.broadcasted_iota(I32, (QAUG_ROWS, tl), 0)
    for h in range(N_HEADS):
        base = h * LANES
        qt_ref[0, base:base + HEAD_DIM, :] = _bf(rt[h * HEAD_DIM:(h + 1) * HEAD_DIM])
        qt_ref[0, base + HEAD_DIM:base + HEAD_DIM + QAUG_ROWS, :] = _bf(jnp.where(arow < 2, ALIBI_SLOPES[h], 0.0))
        qt_ref[0, base + HEAD_DIM + QAUG_ROWS:base + LANES, :] = jnp.zeros((LANES - HEAD_DIM - QAUG_ROWS, tl), BF16)
    qit_ref[0] = _bf(rt[attn_w:attn_w + idx_w])
    vt_ref[0] = _bf(rt[attn_w + idx_w:attn_w + idx_w + HEAD_DIM])
    wit_ref[0] = rt[attn_w + idx_w + HEAD_DIM:attn_w + idx_w + HEAD_DIM + IDX_HEADS]


def _split_w_in(w_in, d):
    sizes = (SSM_WIDTH, N_HEADS * HEAD_DIM, HEAD_DIM, HEAD_DIM, IDX_HEADS * IDX_DIM, IDX_DIM, IDX_HEADS, d, d)
    offs = [0]
    for s in sizes:
        offs.append(offs[-1] + s)
    return [w_in[:, offs[i]:offs[i + 1]] for i in range(9)]


def _inproj(x, ada3, g1, w_in):
    bsz, seq, d = x.shape
    ssm_w = SSM_WIDTH
    attn_w = N_HEADS * HEAD_DIM
    idx_w = IDX_HEADS * IDX_DIM
    w_ssm, w_q, w_k, w_v, w_qi, w_ki, w_wi, _, _ = _split_w_in(w_in, d)
    zpad = lambda n: jnp.zeros((d, n), F32)
    wbig = jnp.concatenate([w_ssm, w_k, zpad(LANES - HEAD_DIM), w_ki, zpad(LANES - IDX_DIM)], axis=1).astype(BF16)
    wt = jnp.concatenate([w_q * (HEAD_DIM ** -0.5), w_qi * (IDX_DIM ** -0.5), w_v, w_wi,
                          zpad(LANES - HEAD_DIM - IDX_HEADS)], axis=1).T.astype(BF16)
    tl = INPROJ_ROWS
    kern = functools.partial(_inproj_kernel, d=d, ssm_w=ssm_w, attn_w=attn_w, idx_w=idx_w, tl=tl)
    row = lambda w: pl.BlockSpec((1, tl, w), lambda b, l: (b, l, 0))
    colt = lambda h: pl.BlockSpec((1, h, tl), lambda b, l: (b, 0, l))
    full = lambda a: pl.BlockSpec(a.shape, lambda b, l: (0,) * a.ndim)
    return pl.pallas_call(
        kern,
        out_shape=(jax.ShapeDtypeStruct((bsz, seq, ssm_w), F32),
                   jax.ShapeDtypeStruct((bsz, seq, LANES), BF16),
                   jax.ShapeDtypeStruct((bsz, seq, IDX_DIM), BF16),
                   jax.ShapeDtypeStruct((bsz, N_HEADS * LANES, seq), BF16),
                   jax.ShapeDtypeStruct((bsz, idx_w, seq), BF16),
                   jax.ShapeDtypeStruct((bsz, HEAD_DIM, seq), BF16),
                   jax.ShapeDtypeStruct((bsz, IDX_HEADS, seq), F32)),
        grid=(bsz, seq // tl),
        in_specs=[row(d),
                  pl.BlockSpec((1, 1, ada3.shape[2]), lambda b, l: (b, 0, 0)),
                  pl.BlockSpec((1, d), lambda b, l: (0, 0)),
                  full(wbig), full(wt)],
        out_specs=(row(ssm_w), row(LANES), row(IDX_DIM),
                   colt(N_HEADS * LANES), colt(idx_w), colt(HEAD_DIM), colt(IDX_HEADS)),
        compiler_params=_cparams(("arbitrary", "arbitrary")),
        name="inproj",
    )(x, ada3, g1.reshape(1, d), wbig, wt)


def _s5disc_kernel(lr_ref, li_ref, ldt_ref, br_ref, bi_ref, are_ref, aim_ref, bbr_ref, bbi_ref):
    lr = lr_ref[...]
    li = li_ref[...]
    dt = jnp.exp(ldt_ref[...])
    mag = jnp.exp(lr * dt)
    a_re = mag * jnp.cos(li * dt)
    a_im = mag * jnp.sin(li * dt)
    den = lr * lr + li * li
    n_re = a_re - 1.0
    f_re = (n_re * lr + a_im * li) / den
    f_im = (a_im * lr - n_re * li) / den
    br = br_ref[...]
    bi = bi_ref[...]
    are_ref[...] = a_re
    aim_ref[...] = a_im
    bbr_ref[...] = f_re * br - f_im * bi
    bbi_ref[...] = f_re * bi + f_im * br


def _s5disc(lam_re, lam_im, log_dt, b_re, b_im):
    g, p = lam_re.shape
    h = b_re.shape[2]
    rep = lambda a: jnp.repeat(a, h, axis=1)
    ldt = jnp.broadcast_to(log_dt[:, None], (g, p * h))
    sds = jax.ShapeDtypeStruct((g, p * h), F32)
    a_re, a_im, bb_re, bb_im = pl.pallas_call(
        _s5disc_kernel, out_shape=(sds, sds, sds, sds), name="s5disc",
    )(rep(lam_re), rep(lam_im), ldt, b_re.reshape(g, p * h), b_im.reshape(g, p * h))
    return a_re[:, ::h], a_im[:, ::h], bb_re.reshape(g, p, h), bb_im.reshape(g, p, h)


def _s5_kernel(u_ref, wb_ref, ar_ref, ai_ref, cc_ref, dsk_ref, wg_ref, bg_ref, o_ref, buf, hst, *, tl, width):
    nch = width // S5_LANE_CHUNK
    sw = S5_LANE_CHUNK // SSM_GROUP * SSM_STATE
    rows = tl * SUBLANES

    @pl.when(pl.program_id(0) == 0)
    def _():
        hst[...] = jnp.zeros_like(hst)

    u = jnp.swapaxes(u_ref[...], 0, 1).reshape(rows, width)
    ub = _bf(u)
    for j in range(nch):
        buf[:, j * 2 * sw:(j + 1) * 2 * sw] = _dot(ub[:, j * S5_LANE_CHUNK:(j + 1) * S5_LANE_CHUNK], wb_ref[j])

    for j in range(nch):
        re_cols = slice(j * 2 * sw, j * 2 * sw + sw)
        im_cols = slice(j * 2 * sw + sw, (j + 1) * 2 * sw)
        a_re = jnp.broadcast_to(ar_ref[:, j * sw:(j + 1) * sw], (SUBLANES, sw))
        a_im = jnp.broadcast_to(ai_ref[:, j * sw:(j + 1) * sw], (SUBLANES, sw))

        def step(t, carry, re_cols=re_cols, im_cols=im_cols, a_re=a_re, a_im=a_im):
            h_re, h_im = carry
            r0 = pl.multiple_of(t * SUBLANES, SUBLANES)
            n_re = (a_re * h_re - a_im * h_im) + buf[pl.ds(r0, SUBLANES), re_cols]
            n_im = (a_re * h_im + a_im * h_re) + buf[pl.ds(r0, SUBLANES), im_cols]
            buf[pl.ds(r0, SUBLANES), re_cols] = n_re
            buf[pl.ds(r0, SUBLANES), im_cols] = n_im
            return n_re, n_im

        h_re, h_im = lax.fori_loop(0, tl, step, (hst[:, re_cols], hst[:, im_cols]), unroll=True)
        hst[:, re_cols] = h_re
        hst[:, im_cols] = h_im

    ys = [_dot(_bf(buf[:, j * 2 * sw:(j + 1) * 2 * sw]), cc_ref[j]) for j in range(nch)]
    y = jnp.concatenate(ys, axis=1) + dsk_ref[...] * u
    y = jax.nn.gelu(y)
    y = y * jax.nn.sigmoid(_dot(_bf(y), wg_ref[...]) + bg_ref[...])
    o_ref[...] = _bf(jnp.swapaxes(y.reshape(tl, SUBLANES, width), 0, 1))


def _s5(u_t, a_re, a_im, bb_re, bb_im, c_re, c_im, d_skip, w_glu, b_glu):
    bsz, seq, width = u_t.shape
    assert bsz == SUBLANES
    nch = width // S5_LANE_CHUNK
    gpc = S5_LANE_CHUNK // SSM_GROUP
    sw = gpc * SSM_STATE
    eye = jnp.eye(gpc, dtype=F32)

    def bmat(bb):
        t = bb.reshape(nch, gpc, SSM_STATE, SSM_GROUP).transpose(0, 1, 3, 2)
        return jnp.einsum('jghp,gk->jghkp', t, eye).reshape(nch, S5_LANE_CHUNK, sw)

    def cmat(cc):
        t = cc.reshape(nch, gpc, SSM_GROUP, SSM_STATE).transpose(0, 1, 3, 2)
        return jnp.einsum('jgph,gk->jgpkh', t, eye).reshape(nch, sw, S5_LANE_CHUNK)

    wb = jnp.concatenate([bmat(bb_re), bmat(bb_im)], axis=2)
    cc = jnp.concatenate([cmat(c_re), -cmat(c_im)], axis=1)
    tl = S5_STEPS
    full = lambda a: pl.BlockSpec(a.shape, lambda i: (0,) * a.ndim)
    args = (u_t, wb.astype(BF16), a_re.reshape(1, -1), a_im.reshape(1, -1), cc.astype(BF16),
            d_skip.reshape(1, width), w_glu.astype(BF16), b_glu.reshape(1, width))
    return pl.pallas_call(
        functools.partial(_s5_kernel, tl=tl, width=width),
        out_shape=jax.ShapeDtypeStruct((bsz, seq, width), BF16),
        grid=(seq // tl,),
        in_specs=[pl.BlockSpec((bsz, tl, width), lambda i: (0, i, 0))] + [full(a) for a in args[1:]],
        out_specs=pl.BlockSpec((bsz, tl, width), lambda i: (0, i, 0)),
        scratch_shapes=[pltpu.VMEM((tl * SUBLANES, nch * 2 * sw), F32),
                        pltpu.VMEM((SUBLANES, nch * 2 * sw), F32)],
        compiler_params=_cparams(("arbitrary",)),
        name="s5",
    )(*args)


def _bit_transpose32(words):
    x = list(words)
    j, m = 16, 0x0000FFFF
    while j:
        k = 0
        while k < 32:
            t = (x[k] ^ lax.shift_right_logical(x[k + j], jnp.int32(j))) & jnp.int32(m - (1 << 32) if m >= 1 << 31 else m)
            x[k] = x[k] ^ t
            x[k + j] = x[k + j] ^ lax.shift_left(t, jnp.int32(j))
            k = (k + j + 1) & ~j
        j >>= 1
        m = (m ^ (m << j)) & 0xFFFFFFFF
    return x


def _dsa_kernel(qt_ref, qit_ref, wit_ref, ka_ref, ki_ref, vt_ref, o_ref, key_s, mb_s, acc_s, pl_s, p_s, *, tq, tk, topk, seq):
    i = pl.program_id(1)
    q0 = i * tq
    nkt = (q0 + tq + tk - 1) // tk
    ch = DSA_COUNT_ROWS
    krow = lax.broadcasted_iota(I32, (tk, tq), 0)
    qcol = q0 + lax.broadcasted_iota(I32, (tk, tq), 1)
    crow = lax.broadcasted_iota(I32, (ch, tq), 0)

    wb = wit_ref[0] * (IDX_HEADS ** -0.5)

    def score_tile(j, _):
        r0 = pl.multiple_of(j * tk, tk)
        kit = ki_ref[0, pl.ds(r0, tk), :]
        acc = jnp.zeros((tk, tq), F32)
        for h in range(IDX_HEADS):
            s = _dot(kit, qit_ref[0, h * IDX_DIM:(h + 1) * IDX_DIM, :])
            acc = acc + wb[h:h + 1, :] * jnp.maximum(s, 0.0)
        bits = lax.bitcast_convert_type(acc, I32)
        key = jnp.where(bits < 0, bits ^ jnp.int32(0x7FFFFFFF), bits)
        key = jnp.where(acc == 0.0, 0, key)
        key = jnp.where(krow + r0 <= qcol, key, INT_MIN)
        key_s[pl.ds(r0, tk), :] = key
        ukey = key ^ INT_MIN
        for c in range(tk // BITSLICE_ROWS):
            words = [ukey[c * BITSLICE_ROWS + v * SUBLANES:c * BITSLICE_ROWS + (v + 1) * SUBLANES, :]
                     for v in range(32)]
            planes = _bit_transpose32(words)
            g0 = pl.multiple_of((j * (tk // BITSLICE_ROWS) + c) * SUBLANES, SUBLANES)
            for it in range(32):
                pl_s[it, pl.ds(g0, SUBLANES), :] = planes[it]
        return 0

    @pl.when((pl.program_id(0) == 0) & (i == 0))
    def _():
        pl_s[...] = jnp.zeros(pl_s.shape, I32)

    def score_pair(jj, _):
        score_tile(2 * jj, 0)
        score_tile(2 * jj + 1, 0)
        return 0

    lax.fori_loop(0, nkt // 2, score_pair, 0)

    @pl.when(nkt % 2 == 1)
    def _():
        score_tile(nkt - 1, 0)

    def count(pred):
        def tile(j, cnt):
            for c in range(tk // ch):
                rr = pl.multiple_of(j * tk + c * ch, ch)
                cnt = cnt + jnp.where(pred(key_s[pl.ds(rr, ch), :], rr), 1, 0)
            return cnt
        cnt = lax.fori_loop(0, nkt, tile, jnp.zeros((ch, tq), I32))
        return jnp.sum(cnt.astype(F32), axis=0, keepdims=True)

    ngrp = seq // 32

    def lane_count(words):
        pc = lax.population_count(words).reshape(ngrp // SUBLANES, SUBLANES, tq)
        return jnp.sum(jnp.sum(pc, axis=0).astype(F32), axis=0, keepdims=True)

    def bit_step(it, carry):
        alive, above, ans_u = carry
        ones = alive & pl_s[it]
        cnt1 = lane_count(ones)
        take = above + cnt1 >= float(topk)
        alive = jnp.where(take, ones, alive ^ ones)
        above = jnp.where(take, above, above + cnt1)
        ans_u = jnp.where(take, ans_u | lax.shift_left(jnp.int32(1), 31 - it), ans_u)
        return alive, above, ans_u

    grow = lax.broadcasted_iota(I32, (ngrp, tq), 0)
    alive0 = jnp.where(grow < nkt * (tk // 32), -1, 0)
    alive, above, ans_u = lax.fori_loop(
        0, 32, bit_step, (alive0, jnp.zeros((1, tq), F32), jnp.zeros((1, tq), I32)))
    thr = jnp.maximum(ans_u ^ INT_MIN, INT_MIN + 1)
    cnt_ge = above + lane_count(alive)
    tied = jnp.where(ans_u != 0, cnt_ge, 0.0) > float(topk)
    has_ties = jnp.max(jnp.where(tied, 1.0, 0.0)) > 0.0

    def tie_cut():
        need = float(topk) - count(lambda kb, rr: kb > thr)
        nbits = max(1, (seq - 1).bit_length())

        def idx_step(b, x):
            cand = x | lax.shift_left(jnp.int32(1), nbits - 1 - b)
            below = count(lambda kb, rr: jnp.where(kb == thr, crow + rr, seq) < cand)
            return jnp.where(below < need, cand, x)

        x = lax.fori_loop(0, nbits, idx_step, jnp.zeros((1, tq), I32))
        return jnp.where(tied, x, seq)

    cut = lax.cond(has_ties, tie_cut, lambda: jnp.full((1, tq), seq, I32))

    def bias_tile(j, _):
        for c in range(tk // ch):
            rr = pl.multiple_of(j * tk + c * ch, ch)
            kb = key_s[pl.ds(rr, ch), :]
            tie_bias = jnp.where(crow + rr <= cut, 0.0, NEG_BIG)
            mb_s[pl.ds(rr, ch), :] = jnp.where(kb > thr, 0.0, jnp.where(kb == thr, tie_bias, NEG_BIG))
        return 0

    def logits(j, h):
        r0 = pl.multiple_of(j * tk, tk)
        s = _dot(ka_ref[0, pl.ds(r0, tk), :], qt_ref[0, h * LANES:(h + 1) * LANES, :]) + mb_s[pl.ds(r0, tk), :]
        return s.reshape(tk // SUBLANES, SUBLANES, tq)

    acc_s[...] = jnp.zeros(acc_s.shape, F32)

    def attn_tile(j, carry):
        ms, ls = carry
        r0 = pl.multiple_of(j * tk, tk)
        bias_tile(j, 0)
        new_m, new_l, alphas = [], [], []
        for h in range(N_HEADS):
            s = logits(j, h)
            m_new = jnp.maximum(ms[h], jnp.max(jnp.max(s, axis=0), axis=0, keepdims=True))
            alpha = jnp.exp(ms[h] - m_new)
            p = jnp.exp(s - m_new)
            new_m.append(m_new)
            new_l.append(alpha * ls[h] + jnp.sum(p, axis=0))
            alphas.append(alpha)
            p_s[h] = _bf(p.reshape(tk, tq))
        for h in range(N_HEADS):
            rows = slice(h * HEAD_DIM, (h + 1) * HEAD_DIM)
            acc_s[rows, :] = alphas[h] * acc_s[rows, :] + _dot(vt_ref[0, :, pl.ds(r0, tk)], p_s[h])
        return tuple(new_m), tuple(new_l)

    init = ((jnp.full((1, tq), NEG_BIG, F32),) * N_HEADS, (jnp.zeros((SUBLANES, tq), F32),) * N_HEADS)
    _, ls = lax.fori_loop(0, nkt, attn_tile, init)
    for h in range(N_HEADS):
        rows = slice(h * HEAD_DIM, (h + 1) * HEAD_DIM)
        acc_s[rows, :] = acc_s[rows, :] / jnp.sum(ls[h], axis=0, keepdims=True)
    o_ref[0] = _bf(acc_s[...].T)


def _dsa(qt, qit, wit, ka, ki, vt):
    bsz, seq = ka.shape[0], ka.shape[1]
    aw = N_HEADS * HEAD_DIM
    tq = min(DSA_Q_COLS, seq)
    tk = min(DSA_K_ROWS, seq)
    topk = min(TOPK_MAX, seq // 4)
    assert (seq - 1) // POS_SPLIT < 256 and POS_SPLIT <= 256, "key positions must split into two bf16-exact parts"
    assert all(float(np.float32(sl).astype(BF16)) == sl for sl in ALIBI_SLOPES), "ALiBi slopes must be bf16-exact"
    kern = functools.partial(_dsa_kernel, tq=tq, tk=tk, topk=topk, seq=seq)
    cols = lambda r: pl.BlockSpec((1, r, tq), lambda b, i: (b, 0, i))
    return pl.pallas_call(
        kern,
        out_shape=jax.ShapeDtypeStruct((bsz, seq, aw), BF16),
        grid=(bsz, seq // tq),
        in_specs=[cols(N_HEADS * LANES), cols(qit.shape[1]), cols(IDX_HEADS),
                  pl.BlockSpec((1, seq, LANES), lambda b, i: (b, 0, 0)),
                  pl.BlockSpec((1, seq, IDX_DIM), lambda b, i: (b, 0, 0)),
                  pl.BlockSpec((1, HEAD_DIM, seq), lambda b, i: (b, 0, 0))],
        out_specs=pl.BlockSpec((1, tq, aw), lambda b, i: (b, i, 0)),
        scratch_shapes=[pltpu.VMEM((seq, tq), I32), pltpu.VMEM((seq, tq), F32), pltpu.VMEM((aw, tq), F32),
                        pltpu.VMEM((32, seq // 32, tq), I32), pltpu.VMEM((N_HEADS, tk, tq), BF16)],
        compiler_params=_cparams(("arbitrary", "arbitrary")),
        name="dsa",
    )(qt, qit, wit, ka, ki, vt)


def _mix_kernel(x_ref, ys_ref, ya_ref, ada_ref, g1_ref, wgt_ref, wps_ref, wpa_ref, wo_ref, g2_ref,
                h_ref, u2_ref, *, d):
    gate1 = ada_ref[0, :, 2 * d:3 * d]
    shift2 = ada_ref[0, :, 3 * d:4 * d]
    scale2 = ada_ref[0, :, 4 * d:5 * d]
    x = x_ref[0]
    u = _bf(_rms(x, g1_ref[...]) * (1.0 + ada_ref[0, :, d:2 * d]) + ada_ref[0, :, 0:d])
    g = _dot(u, wgt_ref[...])
    mixed = (jax.nn.sigmoid(g[:, 0:d]) * _dot(ys_ref[0], wps_ref[...])
             + jax.nn.sigmoid(g[:, d:2 * d]) * _dot(ya_ref[0], wpa_ref[...]))
    h = x + gate1 * _dot(_bf(mixed), wo_ref[...])
    h_ref[0] = h
    u2_ref[0] = _rms(h, g2_ref[...]) * (1.0 + scale2) + shift2


def _mix(x, ys, ya, ada3, g1, w_in, wps, wpa, wo, g2):
    bsz, seq, d = x.shape
    tm = MIX_ROWS
    row = lambda w: pl.BlockSpec((1, tm, w), lambda b, l: (b, l, 0))
    full = lambda a: pl.BlockSpec(a.shape, lambda b, l: (0,) * a.ndim)
    wps, wpa, wo = wps.astype(BF16), wpa.astype(BF16), wo.astype(BF16)
    wgt = jnp.concatenate(_split_w_in(w_in, d)[7:9], axis=1).astype(BF16)
    g1 = g1.reshape(1, d)
    g2 = g2.reshape(1, d)
    return pl.pallas_call(
        functools.partial(_mix_kernel, d=d),
        out_shape=(jax.ShapeDtypeStruct((bsz, seq, d), F32), jax.ShapeDtypeStruct((bsz, seq, d), F32)),
        grid=(bsz, seq // tm),
        in_specs=[row(d), row(ys.shape[2]), row(ya.shape[2]),
                  pl.BlockSpec((1, 1, ada3.shape[2]), lambda b, l: (b, 0, 0)),
                  full(g1), full(wgt), full(wps), full(wpa), full(wo), full(g2)],
        out_specs=(row(d), row(d)),
        compiler_params=_cparams(("arbitrary", "arbitrary")),
        name="mix",
    )(x, ys, ya, ada3, g1, wgt, wps, wpa, wo, g2)


def _first_max(cur, idx, axis, big):
    m = jnp.max(cur, axis=axis, keepdims=True)
    first = jnp.min(jnp.where(cur == m, idx, big), axis=axis, keepdims=True)
    return m, idx == first


def _route_kernel(u_ref, wrh_ref, wrl_ref, rb_ref, tri_ref, ltri_ref,
                  gt_ref, loc_ref, c8_ref, loff_ref, run0_ref, tot_ref, run_s, *, t, tt):
    @pl.when(pl.program_id(0) == 0)
    def _():
        run_s[...] = jnp.zeros_like(run_s)

    uh, ul = _split(u_ref[...])
    logits = _dot_nt(wrh_ref[...], uh) + (_dot_nt(wrl_ref[...], uh) + _dot_nt(wrh_ref[...], ul))
    scores = jax.nn.sigmoid(logits)
    biased = scores + rb_ref[...]
    per_group = N_EXPERTS // N_GROUPS
    b3 = biased.reshape(N_GROUPS, per_group, t)
    i3 = lax.broadcasted_iota(I32, b3.shape, 1)
    m1, hit1 = _first_max(b3, i3, 1, per_group)
    m2 = jnp.max(jnp.where(hit1, -jnp.inf, b3), axis=1, keepdims=True)
    gs = (m1 + m2).reshape(N_GROUPS, t)
    gi = lax.broadcasted_iota(I32, gs.shape, 0)
    gsel = jnp.zeros(gs.shape, F32)
    for _ in range(TOPK_GROUPS):
        _, hit = _first_max(gs, gi, 0, N_GROUPS)
        gsel = jnp.where(hit, 1.0, gsel)
        gs = jnp.where(hit, -jnp.inf, gs)
    cur = jnp.where(gsel.reshape(N_GROUPS, 1, t) > 0.0, b3, -jnp.inf).reshape(N_EXPERTS, t)
    ei = lax.broadcasted_iota(I32, cur.shape, 0)
    hits = []
    gates = []
    for _ in range(TOP_K):
        _, hit = _first_max(cur, ei, 0, N_EXPERTS)
        hits.append(hit)
        gates.append(jnp.sum(jnp.where(hit, scores, 0.0), axis=0, keepdims=True))
        cur = jnp.where(hit, -jnp.inf, cur)
    gate = jnp.concatenate(gates, axis=0)
    gt_ref[...] = gate / jnp.sum(gate, axis=0, keepdims=True) * ROUTED_SCALE
    onehot = jnp.zeros(cur.shape, F32)
    for hit in hits:
        onehot = jnp.where(hit, 1.0, onehot)
    for sub in range(t // tt):
        cols = slice(sub * tt, (sub + 1) * tt)
        oh = onehot[:, cols]
        cnt = jnp.sum(oh, axis=1, keepdims=True)
        c8 = jnp.floor((cnt + (SUBLANES - 1)) * (1.0 / SUBLANES)) * SUBLANES
        c8l = jnp.broadcast_to(c8, (N_EXPERTS, LANES))
        loff = _dot(ltri_ref[...], _bf(c8l))
        slot = _dot(_bf(oh), tri_ref[...]) + loff[:, 0:1]
        loc_ref[:, cols] = jnp.concatenate(
            [jnp.sum(jnp.where(hit[:, cols], slot, 0.0), axis=0, keepdims=True) for hit in hits],
            axis=0).astype(I32)
        c8_ref[sub] = c8l
        loff_ref[sub] = loff
        run0_ref[sub] = run_s[...]
        run_s[...] = run_s[...] + c8
    tot_ref[...] = run_s[...]


def _route(u2, w_router, router_bias):
    n, d = u2.shape
    tt = min(MOE_TILE, n)
    t = min(ROUTE_TILES * tt, n)
    nt = n // tt
    wt = w_router.T
    wrh = wt.astype(BF16)
    wrl = (wt - wrh.astype(F32)).astype(BF16)
    tri = (jnp.arange(tt)[:, None] < jnp.arange(tt)[None, :]).astype(BF16)
    ex = jnp.arange(N_EXPERTS)
    ltri = (ex[None, :] < ex[:, None]).astype(BF16)
    full = lambda a: pl.BlockSpec(a.shape, lambda i: (0,) * a.ndim)
    col = pl.BlockSpec((TOP_K, t), lambda i: (0, i))
    tab = pl.BlockSpec((t // tt, N_EXPERTS, LANES), lambda i: (i, 0, 0))
    tab_sds = jax.ShapeDtypeStruct((nt, N_EXPERTS, LANES), F32)
    rb = router_bias.reshape(N_EXPERTS, 1)
    return pl.pallas_call(
        functools.partial(_route_kernel, t=t, tt=tt),
        out_shape=(jax.ShapeDtypeStruct((TOP_K, n), F32), jax.ShapeDtypeStruct((TOP_K, n), I32),
                   tab_sds, tab_sds, tab_sds, jax.ShapeDtypeStruct((N_EXPERTS, LANES), F32)),
        grid=(n // t,),
        in_specs=[pl.BlockSpec((t, d), lambda i: (i, 0)), full(wrh), full(wrl), full(rb), full(tri), full(ltri)],
        out_specs=(col, col, tab, tab, tab, pl.BlockSpec((N_EXPERTS, LANES), lambda i: (0, 0))),
        scratch_shapes=[pltpu.VMEM((N_EXPERTS, LANES), F32)],
        compiler_params=_cparams(("arbitrary",)),
        name="route",
    )(u2, wrh, wrl, rb, tri, ltri)


RUN_BITS = tuple(1 << b for b in reversed(range((MOE_TILE // SUBLANES).bit_length())))


def _for_each_run_piece(n8_ref, src_ref, dst_ref, tile, bits, fn):
    def per_expert(e, _):
        idx = tile * N_EXPERTS + e
        n8 = n8_ref[idx]
        src = src_ref[idx]
        dst = dst_ref[idx]
        for p in bits:
            off = (n8 & ~(2 * p - 1)) * SUBLANES

            @pl.when((n8 & p) != 0)
            def _(p=p, off=off):
                fn(pl.multiple_of(src + off, SUBLANES), pl.multiple_of(dst + off, SUBLANES), p * SUBLANES)
        return 0

    lax.fori_loop(0, N_EXPERTS, per_expert, 0)


def _issue_runs_inline(n8_ref, src_ref, dst_ref, tile, enable, fn, experts=range(N_EXPERTS)):
    for e in experts:
        idx = tile * N_EXPERTS + e
        n8 = jnp.where(enable, n8_ref[idx], 0)
        src = src_ref[idx]
        dst = dst_ref[idx]
        for p in RUN_BITS:
            off = (n8 & ~(2 * p - 1)) * SUBLANES

            @pl.when((n8 & p) != 0)
            def _(p=p, off=off, src=src, dst=dst):
                fn(pl.multiple_of(src + off, SUBLANES), pl.multiple_of(dst + off, SUBLANES), p * SUBLANES)


def _wait_rows(n8, make_copy, max_rows):
    for p in tuple(1 << b for b in reversed(range((max_rows // SUBLANES).bit_length()))):
        @pl.when((n8 & p) != 0)
        def _(p=p):
            make_copy(p * SUBLANES).wait()


def _dispatch_kernel(n8_ref, src_ref, dst_ref, tot_ref, zn8_ref, zdst_ref, u_ref, loc_ref, gate_ref, xs_hbm,
                     lbuf0, lbuf1, lbuf2, zx, sems, *, tt, nslot, dh):
    i = pl.program_id(0)
    last = pl.num_programs(0) - 1
    bufs = (lbuf0, lbuf1, lbuf2)
    nbuf = len(bufs)
    zsem = nbuf

    def copy_from(slot):
        def piece(s0, d0, rows):
            pltpu.make_async_copy(bufs[slot].at[pl.ds(s0, rows)], xs_hbm.at[pl.ds(d0, rows)], sems.at[slot]).start()
        return piece

    def wait_tile(tile, slot):
        _wait_rows(tot_ref[tile], lambda rows: pltpu.make_async_copy(
            bufs[slot].at[pl.ds(0, rows)], xs_hbm.at[pl.ds(0, rows)], sems.at[slot]), nslot)

    @pl.when(i == 0)
    def _():
        zx[...] = jnp.zeros(zx.shape, I32)

        def zero_piece(s0, d0, rows):
            cx = pltpu.make_async_copy(zx.at[pl.ds(0, rows)], xs_hbm.at[pl.ds(d0, rows)], sems.at[zsem])
            cx.start()
            cx.wait()

        zbits = tuple(1 << b for b in reversed(range((EXPERT_ROWS // SUBLANES - 1).bit_length())))
        _for_each_run_piece(zn8_ref, zdst_ref, zdst_ref, 0, zbits, zero_piece)

    def build(buf):
        ub = _bf(u_ref[...])
        ones = jnp.ones((tt, LANES), BF16)
        loc = loc_ref[...]
        gate = gate_ref[...]
        rows_b = lax.broadcasted_iota(I32, (SLOT_CHUNK, tt), 0).astype(F32).astype(BF16)
        loc_hi = lax.shift_right_logical(loc, SLOT_CHUNK.bit_length() - 1)
        loc_lo = (loc & (SLOT_CHUNK - 1)).astype(F32)
        gate_h = _bf(gate)
        gate_l = _bf(gate - gate_h.astype(F32))
        one_b = jnp.ones((SLOT_CHUNK, tt), BF16)
        for c in range(nslot // SLOT_CHUNK):
            perm = jnp.zeros((SLOT_CHUNK, tt), BF16)
            pgh = jnp.zeros((SLOT_CHUNK, tt), BF16)
            pgl = jnp.zeros((SLOT_CHUNK, tt), BF16)
            for k in range(TOP_K):
                lk = _bf(jnp.where(loc_hi[k:k + 1, :] == c, loc_lo[k:k + 1, :], -1.0))
                eq = rows_b == lk
                perm = jnp.where(eq, one_b, perm)
                pgh = jnp.where(eq, jnp.broadcast_to(gate_h[k:k + 1, :], (SLOT_CHUNK, tt)), pgh)
                pgl = jnp.where(eq, jnp.broadcast_to(gate_l[k:k + 1, :], (SLOT_CHUNK, tt)), pgl)
            cs = slice(c * SLOT_CHUNK, (c + 1) * SLOT_CHUNK)
            xp = lax.bitcast_convert_type(_dot(perm, ub), I32)
            buf[cs, 0:dh] = xp[:, 0:dh] | lax.shift_right_logical(xp[:, dh:2 * dh], 16)
            buf[cs, dh:dh + LANES] = lax.bitcast_convert_type(_dot(pgh, ones) + _dot(pgl, ones), I32)

    def step(cur):
        prv, prv2 = (cur - 1) % nbuf, (cur - 2) % nbuf

        @pl.when(i >= nbuf)
        def _():
            wait_tile(i - nbuf, cur)

        _issue_runs_inline(n8_ref, src_ref, dst_ref, jnp.maximum(i - 1, 0), i >= 1, copy_from(prv))
        build(bufs[cur])

        @pl.when(i == last)
        def _():
            _for_each_run_piece(n8_ref, src_ref, dst_ref, i, RUN_BITS, copy_from(cur))

            @pl.when(i >= 2)
            def _():
                wait_tile(i - 2, prv2)

            @pl.when(i >= 1)
            def _():
                wait_tile(i - 1, prv)
            wait_tile(i, cur)

    for cur in range(nbuf):
        pl.when(i % nbuf == cur)(functools.partial(step, cur))


def _dispatch(tabs, ztabs, u2, loc_t, gate_t, n_rows):
    n, d = u2.shape
    tt = min(MOE_TILE, n)
    nslot = TOP_K * tt + N_EXPERTS * SUBLANES
    dh = d // 2
    assert nslot % SLOT_CHUNK == 0 and tt // SUBLANES == RUN_BITS[0]
    col = pl.BlockSpec((TOP_K, tt), lambda i, *_: (0, i))
    return pl.pallas_call(
        functools.partial(_dispatch_kernel, tt=tt, nslot=nslot, dh=dh),
        out_shape=jax.ShapeDtypeStruct((n_rows, dh + LANES), I32),
        grid_spec=pltpu.PrefetchScalarGridSpec(
            num_scalar_prefetch=6, grid=(n // tt,),
            in_specs=[pl.BlockSpec((tt, d), lambda i, *_: (i, 0)), col, col],
            out_specs=pl.BlockSpec(memory_space=pl.ANY),
            scratch_shapes=[pltpu.VMEM((nslot, dh + LANES), I32)] * 3 + [
                pltpu.VMEM((EXPERT_ROWS // 2, dh + LANES), I32), pltpu.SemaphoreType.DMA((4,))]),
        compiler_params=_cparams(("arbitrary",)),
        name="dispatch",
    )(*tabs, *ztabs, u2, loc_t, gate_t)


def _experts_kernel(be_ref, nu_ref, first_ref, ord_ref, nxt_ref, xs_hbm, wg_hbm, wu_hbm, wd_hbm, ys_ref,
                    xbuf, wgb, wub, wdb, sems, wsems, *, d):
    dh = d // 2
    i = pl.program_id(0)
    nu = nu_ref[0]
    ahead = EXPERT_RING - 1

    def wcopies(e, slot):
        return (pltpu.make_async_copy(wg_hbm.at[e], wgb.at[slot], wsems.at[slot]),
                pltpu.make_async_copy(wu_hbm.at[e], wub.at[slot], wsems.at[slot]),
                pltpu.make_async_copy(wd_hbm.at[e], wdb.at[slot], wsems.at[slot]))

    def fetch(j):
        return pltpu.make_async_copy(xs_hbm.at[pl.ds(pl.multiple_of(j * EXPERT_ROWS, EXPERT_ROWS), EXPERT_ROWS)],
                                     xbuf.at[j % EXPERT_RING], sems.at[j % EXPERT_RING])

    @pl.when(i == 0)
    def _():
        for j in range(ahead):
            pl.when(j < nu)(lambda j=j: fetch(j).start())
        for cp in wcopies(be_ref[0], 0):
            cp.start()

    @pl.when(i + ahead < nu)
    def _():
        fetch(i + ahead).start()

    @pl.when(i < nu)
    def _():
        ws = ord_ref[i] % 2

        @pl.when(first_ref[i] == 1)
        def _():
            for cp in wcopies(be_ref[i], ws):
                cp.wait()

            @pl.when(nxt_ref[i] >= 0)
            def _():
                for cp in wcopies(nxt_ref[i], 1 - ws):
                    cp.start()

        fetch(i).wait()
        xs_ref = xbuf.at[i % EXPERT_RING]
        w = xs_ref[:, 0:dh]
        xa = _bf(lax.bitcast_convert_type(w & jnp.int32(-65536), F32))
        xb = _bf(lax.bitcast_convert_type(lax.shift_left(w, 16), F32))
        gate = lax.bitcast_convert_type(xs_ref[:, dh:dh + LANES], F32)
        hg = _dot(xa, _bf(wgb[ws, 0:dh, :])) + _dot(xb, _bf(wgb[ws, dh:d, :]))
        hu = _dot(xa, _bf(wub[ws, 0:dh, :])) + _dot(xb, _bf(wub[ws, dh:d, :]))
        y = _dot(_bf(jax.nn.silu(hg) * hu), _bf(wdb[ws])) * jnp.tile(gate, (1, d // LANES))
        yb = lax.bitcast_convert_type(_bf(y).astype(F32), I32)
        ys_ref[...] = yb[:, 0:dh] | lax.shift_right_logical(yb[:, dh:d], 16)


def _experts(blk_expert, n_used, xs, wg, wu, wd):
    rows, xw = xs.shape
    d = wg.shape[1]
    de = wg.shape[2]
    nblk = rows // EXPERT_ROWS
    blk = lambda i, be, nu, *_: jnp.minimum(i, nu[0] - 1)
    idx = jnp.arange(nblk, dtype=I32)
    first = ((idx == 0) | (blk_expert != jnp.roll(blk_expert, 1))) & (idx < n_used[0])
    ordinal = jnp.maximum(jnp.cumsum(first.astype(I32)) - 1, 0)
    first_at = lax.cummin(jnp.where(first, idx, nblk), reverse=True)
    next_first = jnp.concatenate([first_at[1:], jnp.full((1,), nblk, I32)])
    nxt = jnp.where(next_first < nblk, blk_expert[jnp.minimum(next_first, nblk - 1)], -1).astype(I32)
    hbm = pl.BlockSpec(memory_space=pl.ANY)
    return pl.pallas_call(
        functools.partial(_experts_kernel, d=d),
        out_shape=jax.ShapeDtypeStruct((rows, d // 2), I32),
        grid_spec=pltpu.PrefetchScalarGridSpec(
            num_scalar_prefetch=5, grid=(nblk,),
            in_specs=[hbm, hbm, hbm, hbm],
            out_specs=pl.BlockSpec((EXPERT_ROWS, d // 2), lambda i, be, nu, *_: (blk(i, be, nu), 0)),
            scratch_shapes=[pltpu.VMEM((EXPERT_RING, EXPERT_ROWS, xw), I32),
                            pltpu.VMEM((2, d, de), F32), pltpu.VMEM((2, d, de), F32), pltpu.VMEM((2, de, d), F32),
                            pltpu.SemaphoreType.DMA((EXPERT_RING,)), pltpu.SemaphoreType.DMA((2,))]),
        compiler_params=_cparams(("arbitrary",)),
        name="experts",
    )(blk_expert, n_used, first.astype(I32), ordinal, nxt, xs, wg, wu, wd)


def _combine_kernel(n8_ref, src_ref, dst_ref, tot_ref, ys_hbm, loc_ref, h_ref, u2_ref, ada_ref, wsg_ref, wsu_ref, wsd_ref,
                    gf_ref, o_ref, ybuf0, ybuf1, pick_s, sems, *, tt, nslot, d):
    i = pl.program_id(0)
    last = pl.num_programs(0) - 1
    bufs = (ybuf0, ybuf1)
    dh = d // 2
    nchunk = nslot // SLOT_CHUNK

    def copy_into(slot):
        def piece(s0, d0, rows):
            pltpu.make_async_copy(ys_hbm.at[pl.ds(d0, rows)], bufs[slot].at[pl.ds(s0, rows)], sems.at[slot]).start()
        return piece

    @pl.when(i == 0)
    def _():
        ybuf0[...] = jnp.zeros(ybuf0.shape, I32)
        ybuf1[...] = jnp.zeros(ybuf1.shape, I32)
        _for_each_run_piece(n8_ref, src_ref, dst_ref, 0, RUN_BITS, copy_into(0))

    def step(cur, nxt):
        nxt_tile = jnp.minimum(i + 1, last)
        _issue_runs_inline(n8_ref, src_ref, dst_ref, nxt_tile, i < last, copy_into(nxt), range(N_EXPERTS // 2))
        x = _bf(u2_ref[...])
        shared = _dot(_bf(jax.nn.silu(_dot(x, wsg_ref[...])) * _dot(x, wsu_ref[...])), wsd_ref[...])
        loc = loc_ref[...]
        cols_b = lax.broadcasted_iota(I32, (tt, SLOT_CHUNK), 1).astype(F32).astype(BF16)
        loc_hi = lax.shift_right_logical(loc, SLOT_CHUNK.bit_length() - 1)
        loc_lo = (loc & (SLOT_CHUNK - 1)).astype(F32)
        one_b = jnp.ones((tt, SLOT_CHUNK), BF16)
        for c in range(nchunk):
            pick = jnp.zeros((tt, SLOT_CHUNK), BF16)
            for k in range(TOP_K):
                lk = _bf(jnp.where(loc_hi[:, k:k + 1] == c, loc_lo[:, k:k + 1], -1.0))
                pick = jnp.where(cols_b == lk, one_b, pick)
            pick_s[:, c * SLOT_CHUNK:(c + 1) * SLOT_CHUNK] = pick
        _wait_rows(tot_ref[i], lambda rows: pltpu.make_async_copy(
            ys_hbm.at[pl.ds(0, rows)], bufs[cur].at[pl.ds(0, rows)], sems.at[cur]), nslot)
        _issue_runs_inline(n8_ref, src_ref, dst_ref, nxt_tile, i < last, copy_into(nxt),
                           range(N_EXPERTS // 2, N_EXPERTS))
        w = bufs[cur][...]
        routed_a = _dot(pick_s[...], _bf(lax.bitcast_convert_type(w & jnp.int32(-65536), F32)))
        routed_b = _dot(pick_s[...], _bf(lax.bitcast_convert_type(lax.shift_left(w, 16), F32)))
        routed = jnp.concatenate([routed_a, routed_b], axis=1)
        gate2 = ada_ref[0, :, 5 * d:6 * d]
        h = h_ref[...] + gate2 * (routed + shared)
        o_ref[...] = _rms(h, gf_ref[...])

    pl.when(i % 2 == 0)(lambda: step(0, 1))
    pl.when(i % 2 == 1)(lambda: step(1, 0))


def _combine(tabs, ys, loc, h1, u2, ada3, wsg, wsu, wsd, gf, seq):
    n, d = h1.shape
    tt = min(MOE_TILE, n)
    nslot = TOP_K * tt + N_EXPERTS * SUBLANES
    per_b = seq // tt
    row = pl.BlockSpec((tt, d), lambda i, *_: (i, 0))
    full = lambda a: pl.BlockSpec(a.shape, lambda i, *_: (0,) * a.ndim)
    wsg, wsu, wsd = wsg.astype(BF16), wsu.astype(BF16), wsd.astype(BF16)
    gf = gf.reshape(1, d)
    return pl.pallas_call(
        functools.partial(_combine_kernel, tt=tt, nslot=nslot, d=d),
        out_shape=jax.ShapeDtypeStruct((n, d), F32),
        grid_spec=pltpu.PrefetchScalarGridSpec(
            num_scalar_prefetch=4, grid=(n // tt,),
            in_specs=[pl.BlockSpec(memory_space=pl.ANY),
                      pl.BlockSpec((tt, TOP_K), lambda i, *_: (i, 0)),
                      row, row,
                      pl.BlockSpec((1, 1, ada3.shape[2]), lambda i, *_: (i // per_b, 0, 0)),
                      full(wsg), full(wsu), full(wsd), full(gf)],
            out_specs=row,
            scratch_shapes=[pltpu.VMEM((nslot, d // 2), I32), pltpu.VMEM((nslot, d // 2), I32),
                            pltpu.VMEM((tt, nslot), BF16),
                            pltpu.SemaphoreType.DMA((2,))]),
        compiler_params=_cparams(("arbitrary",)),
        name="combine",
    )(*tabs, ys, loc, h1, u2, ada3, wsg, wsu, wsd, gf)


def _moe(h1, u2, ada3, w_router, router_bias, wg, wu, wd, wsg, wsu, wsd, gf):
    bsz, seq, d = h1.shape
    n = bsz * seq
    assert seq % min(MOE_TILE, n) == 0
    h1f = h1.reshape(n, d)
    u2f = u2.reshape(n, d)
    gate_t, loc_t, c8, loff, run0, tot = _route(u2f, w_router, router_bias)
    nt = c8.shape[0]
    as_tab = lambda a: a[:, :, 0].astype(I32)
    tot8 = tot[:, 0].astype(I32)
    padded = (tot8 + EXPERT_ROWS - 1) // EXPERT_ROWS * EXPERT_ROWS
    pend = jnp.cumsum(padded)
    pstart = (pend - padded).astype(I32)
    nblk = (n * TOP_K + nt * N_EXPERTS * (SUBLANES - 1) + N_EXPERTS * (EXPERT_ROWS - 1) + EXPERT_ROWS - 1) // EXPERT_ROWS
    blk_row0 = jnp.arange(nblk, dtype=I32) * EXPERT_ROWS
    blk_expert = jnp.minimum(jnp.sum(pend[None, :] <= blk_row0[:, None], axis=1), N_EXPERTS - 1).astype(I32)
    n_used = (pend[-1:] // EXPERT_ROWS).astype(I32)
    n8 = as_tab(c8) // SUBLANES
    tabs = (n8.reshape(-1), as_tab(loff).reshape(-1), (pstart[None, :] + as_tab(run0)).reshape(-1),
            jnp.sum(n8, axis=1))
    ztabs = ((padded - tot8) // SUBLANES, pstart + tot8)
    xs = _dispatch(tabs, ztabs, u2f, loc_t, gate_t, nblk * EXPERT_ROWS)
    ys = _experts(blk_expert, n_used, xs, wg, wu, wd)
    out = _combine(tabs, ys, loc_t.T, h1f, u2f, ada3, wsg, wsu, wsd, gf, seq)
    return out.reshape(bsz, seq, d)


def kernel(x, c, w_ada, b_ada, norm1_g, w_in, ssm_lambda_re, ssm_lambda_im, ssm_log_dt, ssm_b_re, ssm_b_im,
           ssm_c_re, ssm_c_im, ssm_d, ssm_w_glu, ssm_b_glu, w_proj_ssm, w_proj_attn, w_out, norm2_g, w_router,
           router_bias, w_exp_gate, w_exp_up, w_exp_down, w_sh_gate, w_sh_up, w_sh_down, norm_f_g):
    depth = w_ada.shape[0]
    assert depth == 1, "the final norm is fused into the last (only) layer's combine kernel"
    bsz, seq, d = x.shape
    layer = 0
    ada3 = _ada(c, w_ada[layer], b_ada[layer]).reshape(bsz, 1, 6 * d)
    us, ka, ki, qt, qit, vt, wit = _inproj(x, ada3, norm1_g[layer], w_in[layer])
    a_re, a_im, bb_re, bb_im = _s5disc(ssm_lambda_re[layer], ssm_lambda_im[layer], ssm_log_dt[layer],
                                       ssm_b_re[layer], ssm_b_im[layer])
    ys_t = _s5(us, a_re, a_im, bb_re, bb_im, ssm_c_re[layer], ssm_c_im[layer],
               ssm_d[layer], ssm_w_glu[layer], ssm_b_glu[layer])
    ya = _dsa(qt, qit, wit, ka, ki, vt)
    h1, u2 = _mix(x, ys_t, ya, ada3, norm1_g[layer], w_in[layer], w_proj_ssm[layer],
                  w_proj_attn[layer], w_out[layer], norm2_g[layer])
    return _moe(h1, u2, ada3, w_router[layer], router_bias[layer], w_exp_gate[layer], w_exp_up[layer],
                w_exp_down[layer], w_sh_gate[layer], w_sh_up[layer], w_sh_down[layer], norm_f_g)
```
